```python
import jax, jax.numpy as jnp
from jax import lax
import numpy as np

D_MODEL = 2048
BATCH = 8
SEQ = 4096
DEPTH = 4

CHUNK = 64
N_MIXERS = 2
N_LAYERS_A = (DEPTH + 1) // 2
N_LAYERS_B = DEPTH // 2
CONV_A_WIDTH = 31
CONV_B_WIDTH = 3
D_FF = 4 * D_MODEL
PLE_DIM = 256
EPS = 1e-6

kernel_name = "hybrid_conformer_shortconv_trunk"


def rmsnorm(x, g):
    xf = x.astype(jnp.float32)
    r = lax.rsqrt(jnp.mean(xf * xf, axis=-1, keepdims=True) + EPS)
    return (xf * r).astype(x.dtype) * g


def causal_depthwise_conv(x, w):
    k_width, channels = w.shape
    return lax.conv_general_dilated(
        x, w[:, None, :],
        window_strides=(1,),
        padding=[(k_width - 1, 0)],
        dimension_numbers=("NWC", "WIO", "NWC"),
        feature_group_count=channels)


def conformer_conv_module(u, w_pw1, b_pw1, w_dw, b_dw, g_norm, w_pw2, b_pw2):
    a = jnp.einsum("bsd,de->bse", u, w_pw1) + b_pw1
    val, gate = jnp.split(a, 2, axis=-1)
    v = val * jax.nn.sigmoid(gate)
    v = causal_depthwise_conv(v, w_dw) + b_dw
    v = rmsnorm(v, g_norm)
    v = jax.nn.silu(v)
    return jnp.einsum("bsd,de->bse", v, w_pw2) + b_pw2


def short_gated_conv(u, w_in, w_conv, w_out):
    bcv = jnp.einsum("bsd,de->bse", u, w_in)
    gate_b, gate_c, v = jnp.split(bcv, 3, axis=-1)
    y = gate_b * causal_depthwise_conv(gate_c * v, w_conv)
    return jnp.einsum("bsd,de->bse", y, w_out)


def squared_relu_mlp(u, w1, w2):
    hdn = jnp.square(jax.nn.relu(jnp.einsum("bsd,df->bsf", u, w1)))
    return jnp.einsum("bsf,fd->bsd", hdn, w2)


def per_layer_embedding(h, p_i, g_norm, w_proj, w_gate):
    e = jnp.einsum("bsk,kd->bsd", p_i, w_proj)
    g = jax.nn.sigmoid(jnp.einsum("bsd,de->bse", rmsnorm(h, g_norm), w_gate))
    return g * e


def _fwd_setup_inputs(seed: int = 0) -> dict:
    key = jax.random.key(seed)
    ks = jax.random.split(key, 24)

    def nrm(k, shape, scale):
        return jax.random.normal(k, shape, jnp.float32) * scale

    def gain(k, shape):
        return 1.0 + 0.05 * jax.random.normal(k, shape, jnp.float32)

    D = D_MODEL
    return {
        "x": nrm(ks[0], (BATCH, SEQ, D), 1.0),
        "p": nrm(ks[1], (DEPTH, BATCH, SEQ, PLE_DIM), 1.0),
        "norm_mix": gain(ks[2], (DEPTH, D)),
        "norm_mlp": gain(ks[3], (DEPTH, D)),
        "norm_ple": gain(ks[4], (DEPTH, D)),
        "cf_w_pw1": nrm(ks[5], (N_LAYERS_A, D, 2 * D), D ** -0.5),
        "cf_b_pw1": nrm(ks[6], (N_LAYERS_A, 2 * D), 0.02),
        "cf_w_dw": nrm(ks[7], (N_LAYERS_A, CONV_A_WIDTH, D), CONV_A_WIDTH ** -0.5),
        "cf_b_dw": nrm(ks[8], (N_LAYERS_A, D), 0.02),
        "cf_norm": gain(ks[9], (N_LAYERS_A, D)),
        "cf_w_pw2": nrm(ks[10], (N_LAYERS_A, D, D), D ** -0.5),
        "cf_b_pw2": nrm(ks[11], (N_LAYERS_A, D), 0.02),
        "sc_w_in": nrm(ks[12], (N_LAYERS_B, D, 3 * D), D ** -0.5),
        "sc_w_conv": nrm(ks[13], (N_LAYERS_B, CONV_B_WIDTH, D), CONV_B_WIDTH ** -0.5),
        "sc_w_out": nrm(ks[14], (N_LAYERS_B, D, D), D ** -0.5),
        "mlp_w1": nrm(ks[15], (DEPTH, D, D_FF), D ** -0.5),
        "mlp_w2": nrm(ks[16], (DEPTH, D_FF, D), D_FF ** -0.5),
        "ple_w_proj": nrm(ks[17], (DEPTH, PLE_DIM, D), PLE_DIM ** -0.5),
        "ple_w_gate": nrm(ks[18], (DEPTH, D, D), D ** -0.5),
        "norm_final": gain(ks[19], (D,)),
    }


def _fwd_reference(x, p, norm_mix, norm_mlp, norm_ple,
              cf_w_pw1, cf_b_pw1, cf_w_dw, cf_b_dw, cf_norm, cf_w_pw2, cf_b_pw2,
              sc_w_in, sc_w_conv, sc_w_out,
              mlp_w1, mlp_w2, ple_w_proj, ple_w_gate, norm_final):
    h = x
    for i in range(DEPTH):
        j = i // N_MIXERS
        u = rmsnorm(h, norm_mix[i])
        if i % N_MIXERS == 0:
            m = conformer_conv_module(u, cf_w_pw1[j], cf_b_pw1[j], cf_w_dw[j], cf_b_dw[j],
                                      cf_norm[j], cf_w_pw2[j], cf_b_pw2[j])
        else:
            m = short_gated_conv(u, sc_w_in[j], sc_w_conv[j], sc_w_out[j])
        h = h + m
        h = h + squared_relu_mlp(rmsnorm(h, norm_mlp[i]), mlp_w1[i], mlp_w2[i])
        h = h + per_layer_embedding(h, p[i], norm_ple[i], ple_w_proj[i], ple_w_gate[i])
    return rmsnorm(h, norm_final)


import jax as _jax
import jax.numpy as _jnp

TWIN_FORMAT = 'train_step'
FWD_PARAMS = ['x', 'p', 'norm_mix', 'norm_mlp', 'norm_ple', 'cf_w_pw1', 'cf_b_pw1', 'cf_w_dw', 'cf_b_dw', 'cf_norm', 'cf_w_pw2', 'cf_b_pw2', 'sc_w_in', 'sc_w_conv', 'sc_w_out', 'mlp_w1', 'mlp_w2', 'ple_w_proj', 'ple_w_gate', 'norm_final']
TWIN_WEIGHTS = ['norm_mix', 'norm_mlp', 'norm_ple', 'cf_w_pw1', 'cf_b_pw1', 'cf_w_dw', 'cf_b_dw', 'cf_norm', 'cf_w_pw2', 'cf_b_pw2', 'sc_w_in', 'sc_w_conv', 'sc_w_out', 'mlp_w1', 'mlp_w2', 'ple_w_proj', 'ple_w_gate', 'norm_final']
TWIN_DIFF_INPUT = 'x'
TWIN_INPUTS = ['x', 'p', 'norm_mix', 'norm_mlp', 'norm_ple', 'cf_w_pw1', 'cf_b_pw1', 'cf_w_dw', 'cf_b_dw', 'cf_norm', 'cf_w_pw2', 'cf_b_pw2', 'sc_w_in', 'sc_w_conv', 'sc_w_out', 'mlp_w1', 'mlp_w2', 'ple_w_proj', 'ple_w_gate', 'norm_final', 'loss_target', 'm_norm_mix', 'm_norm_mlp', 'm_norm_ple', 'm_cf_w_pw1', 'm_cf_b_pw1', 'm_cf_w_dw', 'm_cf_b_dw', 'm_cf_norm', 'm_cf_w_pw2', 'm_cf_b_pw2', 'm_sc_w_in', 'm_sc_w_conv', 'm_sc_w_out', 'm_mlp_w1', 'm_mlp_w2', 'm_ple_w_proj', 'm_ple_w_gate', 'm_norm_final', 'v_norm_mix', 'v_norm_mlp', 'v_norm_ple', 'v_cf_w_pw1', 'v_cf_b_pw1', 'v_cf_w_dw', 'v_cf_b_dw', 'v_cf_norm', 'v_cf_w_pw2', 'v_cf_b_pw2', 'v_sc_w_in', 'v_sc_w_conv', 'v_sc_w_out', 'v_mlp_w1', 'v_mlp_w2', 'v_ple_w_proj', 'v_ple_w_gate', 'v_norm_final']
TWIN_OUTPUTS = ['loss', 'grad_x', 'grad_norm_mix', 'grad_norm_mlp', 'grad_norm_ple', 'grad_cf_w_pw1', 'grad_cf_b_pw1', 'grad_cf_w_dw', 'grad_cf_b_dw', 'grad_cf_norm', 'grad_cf_w_pw2', 'grad_cf_b_pw2', 'grad_sc_w_in', 'grad_sc_w_conv', 'grad_sc_w_out', 'grad_mlp_w1', 'grad_mlp_w2', 'grad_ple_w_proj', 'grad_ple_w_gate', 'grad_norm_final', 'delta_norm_mix', 'delta_norm_mlp', 'delta_norm_ple', 'delta_cf_w_pw1', 'delta_cf_b_pw1', 'delta_cf_w_dw', 'delta_cf_b_dw', 'delta_cf_norm', 'delta_cf_w_pw2', 'delta_cf_b_pw2', 'delta_sc_w_in', 'delta_sc_w_conv', 'delta_sc_w_out', 'delta_mlp_w1', 'delta_mlp_w2', 'delta_ple_w_proj', 'delta_ple_w_gate', 'delta_norm_final', 'new_m_norm_mix', 'new_m_norm_mlp', 'new_m_norm_ple', 'new_m_cf_w_pw1', 'new_m_cf_b_pw1', 'new_m_cf_w_dw', 'new_m_cf_b_dw', 'new_m_cf_norm', 'new_m_cf_w_pw2', 'new_m_cf_b_pw2', 'new_m_sc_w_in', 'new_m_sc_w_conv', 'new_m_sc_w_out', 'new_m_mlp_w1', 'new_m_mlp_w2', 'new_m_ple_w_proj', 'new_m_ple_w_gate', 'new_m_norm_final', 'new_v_norm_mix', 'new_v_norm_mlp', 'new_v_norm_ple', 'new_v_cf_w_pw1', 'new_v_cf_b_pw1', 'new_v_cf_w_dw', 'new_v_cf_b_dw', 'new_v_cf_norm', 'new_v_cf_w_pw2', 'new_v_cf_b_pw2', 'new_v_sc_w_in', 'new_v_sc_w_conv', 'new_v_sc_w_out', 'new_v_mlp_w1', 'new_v_mlp_w2', 'new_v_ple_w_proj', 'new_v_ple_w_gate', 'new_v_norm_final']
TWIN_LEAF_KINDS = {'loss': 'loss', 'grad_x': 'grad_x', 'grad_norm_mix': 'grad_w', 'grad_norm_mlp': 'grad_w', 'grad_norm_ple': 'grad_w', 'grad_cf_w_pw1': 'grad_w', 'grad_cf_b_pw1': 'grad_w', 'grad_cf_w_dw': 'grad_w', 'grad_cf_b_dw': 'grad_w', 'grad_cf_norm': 'grad_w', 'grad_cf_w_pw2': 'grad_w', 'grad_cf_b_pw2': 'grad_w', 'grad_sc_w_in': 'grad_w', 'grad_sc_w_conv': 'grad_w', 'grad_sc_w_out': 'grad_w', 'grad_mlp_w1': 'grad_w', 'grad_mlp_w2': 'grad_w', 'grad_ple_w_proj': 'grad_w', 'grad_ple_w_gate': 'grad_w', 'grad_norm_final': 'grad_w', 'delta_norm_mix': 'delta_w', 'delta_norm_mlp': 'delta_w', 'delta_norm_ple': 'delta_w', 'delta_cf_w_pw1': 'delta_w', 'delta_cf_b_pw1': 'delta_w', 'delta_cf_w_dw': 'delta_w', 'delta_cf_b_dw': 'delta_w', 'delta_cf_norm': 'delta_w', 'delta_cf_w_pw2': 'delta_w', 'delta_cf_b_pw2': 'delta_w', 'delta_sc_w_in': 'delta_w', 'delta_sc_w_conv': 'delta_w', 'delta_sc_w_out': 'delta_w', 'delta_mlp_w1': 'delta_w', 'delta_mlp_w2': 'delta_w', 'delta_ple_w_proj': 'delta_w', 'delta_ple_w_gate': 'delta_w', 'delta_norm_final': 'delta_w', 'new_m_norm_mix': 'new_m', 'new_m_norm_mlp': 'new_m', 'new_m_norm_ple': 'new_m', 'new_m_cf_w_pw1': 'new_m', 'new_m_cf_b_pw1': 'new_m', 'new_m_cf_w_dw': 'new_m', 'new_m_cf_b_dw': 'new_m', 'new_m_cf_norm': 'new_m', 'new_m_cf_w_pw2': 'new_m', 'new_m_cf_b_pw2': 'new_m', 'new_m_sc_w_in': 'new_m', 'new_m_sc_w_conv': 'new_m', 'new_m_sc_w_out': 'new_m', 'new_m_mlp_w1': 'new_m', 'new_m_mlp_w2': 'new_m', 'new_m_ple_w_proj': 'new_m', 'new_m_ple_w_gate': 'new_m', 'new_m_norm_final': 'new_m', 'new_v_norm_mix': 'new_v', 'new_v_norm_mlp': 'new_v', 'new_v_norm_ple': 'new_v', 'new_v_cf_w_pw1': 'new_v', 'new_v_cf_b_pw1': 'new_v', 'new_v_cf_w_dw': 'new_v', 'new_v_cf_b_dw': 'new_v', 'new_v_cf_norm': 'new_v', 'new_v_cf_w_pw2': 'new_v', 'new_v_cf_b_pw2': 'new_v', 'new_v_sc_w_in': 'new_v', 'new_v_sc_w_conv': 'new_v', 'new_v_sc_w_out': 'new_v', 'new_v_mlp_w1': 'new_v', 'new_v_mlp_w2': 'new_v', 'new_v_ple_w_proj': 'new_v', 'new_v_ple_w_gate': 'new_v', 'new_v_norm_final': 'new_v'}


def _forward(args):
    return _fwd_reference(*[args[k] for k in FWD_PARAMS])


def _output_shape():
    def fwd():
        inp = _fwd_setup_inputs(0)
        return _fwd_reference(*[inp[k] for k in FWD_PARAMS])
    out = _jax.eval_shape(fwd)
    return out.shape, out.dtype

N_MICROBATCH = 1
ADAM_LR = 0.001
ADAM_B1 = 0.9
ADAM_B2 = 0.999
ADAM_EPS = 1e-08
ADAM_WD = 0.01
ADAM_STEP = 10
PER_EXAMPLE_BATCH_AXIS = {'x': 0, 'p': 1, 'loss_target': 0}
SHARED_INPUTS = []
_WEIGHT_DTYPES = {'norm_mix': _jnp.float32, 'norm_mlp': _jnp.float32, 'norm_ple': _jnp.float32, 'cf_w_pw1': _jnp.float32, 'cf_b_pw1': _jnp.float32, 'cf_w_dw': _jnp.float32, 'cf_b_dw': _jnp.float32, 'cf_norm': _jnp.float32, 'cf_w_pw2': _jnp.float32, 'cf_b_pw2': _jnp.float32, 'sc_w_in': _jnp.float32, 'sc_w_conv': _jnp.float32, 'sc_w_out': _jnp.float32, 'mlp_w1': _jnp.float32, 'mlp_w2': _jnp.float32, 'ple_w_proj': _jnp.float32, 'ple_w_gate': _jnp.float32, 'norm_final': _jnp.float32}
MOMENT_SCALE = {'norm_mix': 6.833251e-02, 'norm_mlp': 7.506528e-02, 'norm_ple': 1.046030e-02, 'cf_w_pw1': 4.121046e-02, 'cf_b_pw1': 7.422126e-02, 'cf_w_dw': 5.488235e-02, 'cf_b_dw': 1.696558e-01, 'cf_norm': 8.463219e-02, 'cf_w_pw2': 6.235126e-02, 'cf_b_pw2': 1.982646e-01, 'sc_w_in': 4.235458e-02, 'sc_w_conv': 4.288453e-02, 'sc_w_out': 4.248327e-02, 'mlp_w1': 3.664129e-02, 'mlp_w2': 9.456037e-02, 'ple_w_proj': 2.526682e-02, 'ple_w_gate': 1.046113e-02, 'norm_final': 1.628623e+01}


def _to_microbatches(a, axis):
    t = _jnp.moveaxis(a, axis, 0)
    t = t.reshape((N_MICROBATCH, t.shape[0] // N_MICROBATCH) + t.shape[1:])
    return _jnp.moveaxis(t, 1, axis + 1)


def setup_inputs(seed: int = 0) -> dict:
    inp = _fwd_setup_inputs(seed)
    key = _jax.random.fold_in(_jax.random.key(seed), 7919)
    shape, _ = _output_shape()
    out = dict(inp)
    out["loss_target"] = _jax.random.normal(_jax.random.fold_in(key, 0), shape, _jnp.float32)
    for i, name in enumerate(TWIN_WEIGHTS):
        w = inp[name].astype(_jnp.float32)
        if MOMENT_SCALE is None:
            s = _jnp.sqrt(_jnp.mean(_jnp.square(w)) + 1e-30)
        else:
            s = MOMENT_SCALE[name]
        km, kv = _jax.random.split(_jax.random.fold_in(key, i + 1))
        out[name] = w
        out["m_" + name] = s * _jax.random.normal(km, w.shape, _jnp.float32)
        out["v_" + name] = (s * s) * _jax.random.uniform(kv, w.shape, _jnp.float32, 0.5, 1.5)
    if N_MICROBATCH > 1:
        for name, axis in PER_EXAMPLE_BATCH_AXIS.items():
            out[name] = _to_microbatches(out[name], axis)
    return {'x': out['x'], 'p': out['p'], 'norm_mix': out['norm_mix'], 'norm_mlp': out['norm_mlp'], 'norm_ple': out['norm_ple'], 'cf_w_pw1': out['cf_w_pw1'], 'cf_b_pw1': out['cf_b_pw1'], 'cf_w_dw': out['cf_w_dw'], 'cf_b_dw': out['cf_b_dw'], 'cf_norm': out['cf_norm'], 'cf_w_pw2': out['cf_w_pw2'], 'cf_b_pw2': out['cf_b_pw2'], 'sc_w_in': out['sc_w_in'], 'sc_w_conv': out['sc_w_conv'], 'sc_w_out': out['sc_w_out'], 'mlp_w1': out['mlp_w1'], 'mlp_w2': out['mlp_w2'], 'ple_w_proj': out['ple_w_proj'], 'ple_w_gate': out['ple_w_gate'], 'norm_final': out['norm_final'], 'loss_target': out['loss_target'], 'm_norm_mix': out['m_norm_mix'], 'm_norm_mlp': out['m_norm_mlp'], 'm_norm_ple': out['m_norm_ple'], 'm_cf_w_pw1': out['m_cf_w_pw1'], 'm_cf_b_pw1': out['m_cf_b_pw1'], 'm_cf_w_dw': out['m_cf_w_dw'], 'm_cf_b_dw': out['m_cf_b_dw'], 'm_cf_norm': out['m_cf_norm'], 'm_cf_w_pw2': out['m_cf_w_pw2'], 'm_cf_b_pw2': out['m_cf_b_pw2'], 'm_sc_w_in': out['m_sc_w_in'], 'm_sc_w_conv': out['m_sc_w_conv'], 'm_sc_w_out': out['m_sc_w_out'], 'm_mlp_w1': out['m_mlp_w1'], 'm_mlp_w2': out['m_mlp_w2'], 'm_ple_w_proj': out['m_ple_w_proj'], 'm_ple_w_gate': out['m_ple_w_gate'], 'm_norm_final': out['m_norm_final'], 'v_norm_mix': out['v_norm_mix'], 'v_norm_mlp': out['v_norm_mlp'], 'v_norm_ple': out['v_norm_ple'], 'v_cf_w_pw1': out['v_cf_w_pw1'], 'v_cf_b_pw1': out['v_cf_b_pw1'], 'v_cf_w_dw': out['v_cf_w_dw'], 'v_cf_b_dw': out['v_cf_b_dw'], 'v_cf_norm': out['v_cf_norm'], 'v_cf_w_pw2': out['v_cf_w_pw2'], 'v_cf_b_pw2': out['v_cf_b_pw2'], 'v_sc_w_in': out['v_sc_w_in'], 'v_sc_w_conv': out['v_sc_w_conv'], 'v_sc_w_out': out['v_sc_w_out'], 'v_mlp_w1': out['v_mlp_w1'], 'v_mlp_w2': out['v_mlp_w2'], 'v_ple_w_proj': out['v_ple_w_proj'], 'v_ple_w_gate': out['v_ple_w_gate'], 'v_norm_final': out['v_norm_final']}


def _loss(weights, diff, rest, loss_target):
    with _jax.named_scope("forward"):
        args = {**rest, TWIN_DIFF_INPUT: diff, **{k: w.astype(_WEIGHT_DTYPES[k]) for k, w in weights.items()}}
        y = _forward(args)
    with _jax.named_scope("loss_head"):
        err = _jnp.square(y.astype(_jnp.float32) - loss_target)
        return 0.5 * _jnp.sum(_jnp.mean(err, axis=-1)) if err.ndim else 0.5 * err


def _adamw(w, g, m, v):
    m = ADAM_B1 * m + (1.0 - ADAM_B1) * g
    v = ADAM_B2 * v + (1.0 - ADAM_B2) * _jnp.square(g)
    m_hat = m / (1.0 - ADAM_B1 ** ADAM_STEP)
    v_hat = v / (1.0 - ADAM_B2 ** ADAM_STEP)
    delta = -ADAM_LR * (m_hat / (_jnp.sqrt(v_hat) + ADAM_EPS) + ADAM_WD * w)
    return delta, m, v


def reference(x, p, norm_mix, norm_mlp, norm_ple, cf_w_pw1, cf_b_pw1, cf_w_dw, cf_b_dw, cf_norm, cf_w_pw2, cf_b_pw2, sc_w_in, sc_w_conv, sc_w_out, mlp_w1, mlp_w2, ple_w_proj, ple_w_gate, norm_final, loss_target, m_norm_mix, m_norm_mlp, m_norm_ple, m_cf_w_pw1, m_cf_b_pw1, m_cf_w_dw, m_cf_b_dw, m_cf_norm, m_cf_w_pw2, m_cf_b_pw2, m_sc_w_in, m_sc_w_conv, m_sc_w_out, m_mlp_w1, m_mlp_w2, m_ple_w_proj, m_ple_w_gate, m_norm_final, v_norm_mix, v_norm_mlp, v_norm_ple, v_cf_w_pw1, v_cf_b_pw1, v_cf_w_dw, v_cf_b_dw, v_cf_norm, v_cf_w_pw2, v_cf_b_pw2, v_sc_w_in, v_sc_w_conv, v_sc_w_out, v_mlp_w1, v_mlp_w2, v_ple_w_proj, v_ple_w_gate, v_norm_final):
    given = dict(x=x, p=p, norm_mix=norm_mix, norm_mlp=norm_mlp, norm_ple=norm_ple, cf_w_pw1=cf_w_pw1, cf_b_pw1=cf_b_pw1, cf_w_dw=cf_w_dw, cf_b_dw=cf_b_dw, cf_norm=cf_norm, cf_w_pw2=cf_w_pw2, cf_b_pw2=cf_b_pw2, sc_w_in=sc_w_in, sc_w_conv=sc_w_conv, sc_w_out=sc_w_out, mlp_w1=mlp_w1, mlp_w2=mlp_w2, ple_w_proj=ple_w_proj, ple_w_gate=ple_w_gate, norm_final=norm_final, loss_target=loss_target, m_norm_mix=m_norm_mix, m_norm_mlp=m_norm_mlp, m_norm_ple=m_norm_ple, m_cf_w_pw1=m_cf_w_pw1, m_cf_b_pw1=m_cf_b_pw1, m_cf_w_dw=m_cf_w_dw, m_cf_b_dw=m_cf_b_dw, m_cf_norm=m_cf_norm, m_cf_w_pw2=m_cf_w_pw2, m_cf_b_pw2=m_cf_b_pw2, m_sc_w_in=m_sc_w_in, m_sc_w_conv=m_sc_w_conv, m_sc_w_out=m_sc_w_out, m_mlp_w1=m_mlp_w1, m_mlp_w2=m_mlp_w2, m_ple_w_proj=m_ple_w_proj, m_ple_w_gate=m_ple_w_gate, m_norm_final=m_norm_final, v_norm_mix=v_norm_mix, v_norm_mlp=v_norm_mlp, v_norm_ple=v_norm_ple, v_cf_w_pw1=v_cf_w_pw1, v_cf_b_pw1=v_cf_b_pw1, v_cf_w_dw=v_cf_w_dw, v_cf_b_dw=v_cf_b_dw, v_cf_norm=v_cf_norm, v_cf_w_pw2=v_cf_w_pw2, v_cf_b_pw2=v_cf_b_pw2, v_sc_w_in=v_sc_w_in, v_sc_w_conv=v_sc_w_conv, v_sc_w_out=v_sc_w_out, v_mlp_w1=v_mlp_w1, v_mlp_w2=v_mlp_w2, v_ple_w_proj=v_ple_w_proj, v_ple_w_gate=v_ple_w_gate, v_norm_final=v_norm_final)
    weights = {n: given[n] for n in TWIN_WEIGHTS}
    shared = {n: given[n] for n in SHARED_INPUTS}
    per_example = {n: given[n] for n in ['x', 'p']}
    grad_fn = _jax.value_and_grad(_loss, argnums=(0, 1))

    def one_microbatch(ex, loss_target):
        ex = dict(ex)
        diff = ex.pop(TWIN_DIFF_INPUT)
        return grad_fn(weights, diff, {**shared, **ex}, loss_target)

    if N_MICROBATCH == 1:
        loss, (grad_w, grad_x) = one_microbatch(per_example, given["loss_target"])
    else:
        def body(carry, xs):
            loss_sum, grad_sum = carry
            l_k, (gw_k, gx_k) = one_microbatch(xs[0], xs[1])
            with _jax.named_scope("update"):
                return (loss_sum + l_k, _jax.tree.map(_jnp.add, grad_sum, gw_k)), gx_k

        init = (_jnp.zeros((), _jnp.float32), _jax.tree.map(_jnp.zeros_like, weights))
        (loss, grad_w), grad_x = _jax.lax.scan(body, init, (per_example, given["loss_target"]))
    with _jax.named_scope("update"):
        delta_w, new_m, new_v = {}, {}, {}
        for n in TWIN_WEIGHTS:
            delta_w[n], new_m[n], new_v[n] = _adamw(weights[n], grad_w[n], given["m_" + n], given["v_" + n])
    return (loss, grad_x, *[grad_w[n] for n in TWIN_WEIGHTS], *[delta_w[n] for n in TWIN_WEIGHTS],
            *[new_m[n] for n in TWIN_WEIGHTS], *[new_v[n] for n in TWIN_WEIGHTS])
```

```python
import functools

import jax
import jax.numpy as jnp
from jax import lax
from jax.experimental import pallas as pl
from jax.experimental.pallas import tpu as pltpu

F32 = jnp.float32
BF16 = jnp.bfloat16

EPS = 1e-6
ADAM_LR = 0.001
ADAM_B1 = 0.9
ADAM_B2 = 0.999
ADAM_EPS = 1e-08
ADAM_WD = 0.01
ADAM_STEP = 10

DEPTH = 4
N_SHARDS = 4
N_DEVICES = 8
V7X_VMEM_LIMIT_BYTES = 56 * 1024 * 1024
SUBLANES = 8
MESH = pl.DeviceIdType.MESH

MM_TM = 1024
MM_TM_FUSED = 512
MM_TN = 1024
MM_TK = 2048
ADAMW_TILE_ELEMS = 256 * 2048

CONV_ROW_CHUNK = 32
CONV_LANE_CHUNK = 512
CONV_TILE_ROWS = 128
ROW_TILE = 256


def _tile(dim, pref):
    if dim <= pref:
        return dim
    t = pref
    while dim % t:
        t //= 2
    return t


def _cparams(*sem):
    return pltpu.CompilerParams(dimension_semantics=sem, vmem_limit_bytes=V7X_VMEM_LIMIT_BYTES)


def _sigmoid(x):
    return 1.0 / (1.0 + jnp.exp(-x))


def _rms_r(x):
    return lax.rsqrt(jnp.mean(x * x, axis=-1, keepdims=True) + EPS)


def _mm_nn(name, a, b3, epilogue, out_dtypes, extras=()):
    M, K = a.shape
    S, Kb, Ns = b3.shape
    assert Kb == K
    N = S * Ns
    fused = len(out_dtypes) > 1 or any(kind == "mn" for _, kind in extras)
    tm, tn, tk = _tile(M, MM_TM_FUSED if fused else MM_TM), _tile(Ns, MM_TN), _tile(K, MM_TK)
    per = Ns // tn
    nk = K // tk
    in_specs = [pl.BlockSpec((tm, tk), lambda i, j, k: (i, k)),
                pl.BlockSpec((None, tk, tn), lambda i, j, k: (j // per, k, j % per))]
    for _, kind in extras:
        if kind == "mn":
            in_specs.append(pl.BlockSpec((tm, tn), lambda i, j, k: (i, j)))
        else:
            in_specs.append(pl.BlockSpec((1, tn), lambda i, j, k: (0, j)))
    n_ex, n_o = len(extras), len(out_dtypes)

    def body(*refs):
        a_ref, b_ref = refs[:2]
        ex = refs[2:2 + n_ex]
        outs = refs[2 + n_ex:2 + n_ex + n_o]
        part = jnp.dot(a_ref[...].astype(BF16), b_ref[...], preferred_element_type=F32)

        def finish(acc):
            res = epilogue(acc, *[e[...] for e in ex])
            for r, o in zip(res, outs):
                o[...] = r.astype(o.dtype)

        if nk == 1:
            finish(part)
        else:
            acc_ref = refs[-1]
            k = pl.program_id(2)

            @pl.when(k == 0)
            def _():
                acc_ref[...] = part

            @pl.when(k > 0)
            def _():
                acc_ref[...] += part

            @pl.when(k == nk - 1)
            def _():
                finish(acc_ref[...])

    res = pl.pallas_call(
        body, name=name, grid=(M // tm, N // tn, nk),
        in_specs=in_specs,
        out_specs=[pl.BlockSpec((tm, tn), lambda i, j, k: (i, j)) for _ in out_dtypes],
        out_shape=[jax.ShapeDtypeStruct((M, N), dt) for dt in out_dtypes],
        scratch_shapes=[pltpu.VMEM((tm, tn), F32)] if nk > 1 else [],
        compiler_params=_cparams("parallel", "parallel", "arbitrary"),
    )(a, b3, *[e for e, _ in extras])
    return res


def _mm_nt(name, g, w3, epilogue, out_dtypes, extras=()):
    M, N = g.shape
    S, K, Ns = w3.shape
    assert S * Ns == N
    tm, tn, tkk = _tile(M, MM_TM_FUSED), _tile(Ns, MM_TN), _tile(K, MM_TK)
    per = Ns // tn
    nn = N // tn
    n_ex, n_o = len(extras), len(out_dtypes)

    def body(*refs):
        g_ref, w_ref = refs[:2]
        ex = refs[2:2 + n_ex]
        outs = refs[2 + n_ex:2 + n_ex + n_o]
        part = lax.dot_general(g_ref[...].astype(BF16), w_ref[...], (((1,), (1,)), ((), ())),
                               preferred_element_type=F32)

        def finish(acc):
            res = epilogue(acc, *[e[...] for e in ex])
            for r, o in zip(res, outs):
                o[...] = r.astype(o.dtype)

        if nn == 1:
            finish(part)
        else:
            acc_ref = refs[-1]
            n = pl.program_id(2)

            @pl.when(n == 0)
            def _():
                acc_ref[...] = part

            @pl.when(n > 0)
            def _():
                acc_ref[...] += part

            @pl.when(n == nn - 1)
            def _():
                finish(acc_ref[...])

    return pl.pallas_call(
        body, name=name, grid=(M // tm, K // tkk, nn),
        in_specs=[pl.BlockSpec((tm, tn), lambda i, kk, n: (i, n)),
                  pl.BlockSpec((None, tkk, tn), lambda i, kk, n: (n // per, kk, n % per))]
        + [pl.BlockSpec((tm, tkk), lambda i, kk, n: (i, kk)) for _ in extras],
        out_specs=[pl.BlockSpec((tm, tkk), lambda i, kk, n: (i, kk)) for _ in out_dtypes],
        out_shape=[jax.ShapeDtypeStruct((M, K), dt) for dt in out_dtypes],
        scratch_shapes=[pltpu.VMEM((tm, tkk), F32)] if nn > 1 else [],
        compiler_params=_cparams("parallel", "parallel", "arbitrary"),
    )(g, w3, *extras)


def _mm_tn(name, a, g, n_shards):
    T, K = a.shape
    _, N = g.shape
    Ns = N // n_shards
    tk, tn, tt = _tile(K, MM_TK), _tile(Ns, MM_TN), _tile(T, MM_TM)
    per = Ns // tn
    nt = T // tt

    def body(a_ref, g_ref, o_ref, acc_ref):
        t = pl.program_id(2)
        part = lax.dot_general(a_ref[...].astype(BF16), g_ref[...].astype(BF16), (((0,), (0,)), ((), ())),
                               preferred_element_type=F32)

        @pl.when(t == 0)
        def _():
            acc_ref[...] = part

        @pl.when(t > 0)
        def _():
            acc_ref[...] += part

        @pl.when(t == nt - 1)
        def _():
            o_ref[...] = acc_ref[...].astype(o_ref.dtype)

    return pl.pallas_call(
        body, name=name, grid=(K // tk, N // tn, nt),
        in_specs=[pl.BlockSpec((tt, tk), lambda i, j, t: (t, i)),
                  pl.BlockSpec((tt, tn), lambda i, j, t: (t, j))],
        out_specs=pl.BlockSpec((None, tk, tn), lambda i, j, t: (j // per, i, j % per)),
        out_shape=jax.ShapeDtypeStruct((n_shards, K, Ns), BF16),
        scratch_shapes=[pltpu.VMEM((tk, tn), F32)],
        compiler_params=_cparams("parallel", "parallel", "arbitrary"),
    )(a, g)


def _rowwise(name, fn, ins, outs, accs=(), scratch=(), tt=ROW_TILE):
    T = next(a.shape[0] for a, kind in ins if kind == "row")
    tt = _tile(T, tt)
    n = T // tt
    in_specs = []
    for a, kind in ins:
        w = a.shape[1]
        if kind == "row":
            in_specs.append(pl.BlockSpec((tt, w), lambda i: (i, 0)))
        elif kind == "vec":
            in_specs.append(pl.BlockSpec(a.shape, lambda i: (0, 0)))
        elif kind[0] == "prev":
            pad = kind[1]
            in_specs.append(pl.BlockSpec((pad, w), lambda i, q=tt // pad: (jnp.maximum(i * q - 1, 0), 0)))
        else:
            pad = kind[1]
            in_specs.append(pl.BlockSpec((pad, w), lambda i, q=tt // pad, last=T // pad - 1:
                                         (jnp.minimum((i + 1) * q, last), 0)))
    n_in, n_out, n_acc = len(ins), len(outs), len(accs)

    def body(*refs):
        i = pl.program_id(0)
        in_refs = refs[:n_in]
        out_refs = refs[n_in:n_in + n_out]
        acc_refs = refs[n_in + n_out:n_in + n_out + n_acc]
        scr = refs[n_in + n_out + n_acc:]
        if n_acc:
            @pl.when(i == 0)
            def _():
                for r in acc_refs:
                    r[...] = jnp.zeros_like(r)
        fn(i, n, in_refs, out_refs, acc_refs, scr)

    res = pl.pallas_call(
        body, name=name, grid=(n,),
        in_specs=in_specs,
        out_specs=[pl.BlockSpec((tt, w), lambda i: (i, 0)) for w, _ in outs]
        + [pl.BlockSpec((r, w), lambda i: (0, 0)) for r, w in accs],
        out_shape=[jax.ShapeDtypeStruct((T, w), dt) for w, dt in outs]
        + [jax.ShapeDtypeStruct((r, w), F32) for r, w in accs],
        scratch_shapes=list(scratch),
        compiler_params=_cparams("arbitrary"),
    )(*[a for a, _ in ins])
    return res


def _colsum(x):
    return jnp.sum(x, axis=0, keepdims=True)


def _rms_fwd(name, h, g):
    D = h.shape[1]

    def fn(i, n, ins, outs, accs, scr):
        x = ins[0][...]
        outs[0][...] = (x * _rms_r(x) * ins[1][...]).astype(BF16)

    return _rowwise(name, fn, [(h, "row"), (g, "vec")], [(D, BF16)])[0]


def _rms_bwd(name, h, g, du, dh_in, want_colsum=False):
    D = h.shape[1]

    def fn(i, n, ins, outs, accs, scr):
        x = ins[0][...]
        gg = ins[1][...]
        d = ins[2][...].astype(F32)
        r = _rms_r(x)
        xn = x * r
        t = d * gg
        dh = ins[3][...] + r * (t - xn * jnp.mean(t * xn, axis=-1, keepdims=True))
        outs[0][...] = dh
        outs[1][...] = dh.astype(BF16)
        accs[0][...] += _colsum(d * xn)
        if want_colsum:
            accs[1][...] += _colsum(dh)

    return _rowwise(name, fn, [(h, "row"), (g, "vec"), (du, "row"), (dh_in, "row")],
                    [(D, F32), (D, BF16)], accs=[(1, D)] * (2 if want_colsum else 1))


def _loss_bwd(name, h, g, tgt):
    D = h.shape[1]

    def fn(i, n, ins, outs, accs, scr):
        x = ins[0][...]
        gg = ins[1][...]
        r = _rms_r(x)
        xn = x * r
        err = xn * gg - ins[2][...]
        dy = err / D
        t = dy * gg
        dh = r * (t - xn * jnp.mean(t * xn, axis=-1, keepdims=True))
        outs[0][...] = dh
        outs[1][...] = dh.astype(BF16)
        accs[0][...] += _colsum(dy * xn)
        accs[1][...] += _colsum(err * err)

    return _rowwise(name, fn, [(h, "row"), (g, "vec"), (tgt, "row")], [(D, F32), (D, BF16)],
                    accs=[(1, D), (1, D)])


def _ple_elem_bwd(name, dh, q, e):
    D = dh.shape[1]

    def fn(i, n, ins, outs, accs, scr):
        d = ins[0][...]
        s = _sigmoid(ins[1][...].astype(F32))
        ee = ins[2][...].astype(F32)
        outs[0][...] = (d * ee * s * (1.0 - s)).astype(BF16)
        outs[1][...] = (d * s).astype(BF16)

    return _rowwise(name, fn, [(dh, "row"), (q, "row"), (e, "row")], [(D, BF16), (D, BF16)])


def _cf_norm_bwd(name, v2, g, dv4):
    D = v2.shape[1]

    def fn(i, n, ins, outs, accs, scr):
        x = ins[0][...]
        gg = ins[1][...]
        r = _rms_r(x)
        xn = x * r
        v3 = xn * gg
        s = _sigmoid(v3)
        dv3 = ins[2][...].astype(F32) * (s * (1.0 + v3 * (1.0 - s)))
        t = dv3 * gg
        dv2 = r * (t - xn * jnp.mean(t * xn, axis=-1, keepdims=True))
        outs[0][...] = dv2
        accs[0][...] += _colsum(dv3 * xn)
        accs[1][...] += _colsum(dv2)

    return _rowwise(name, fn, [(v2, "row"), (g, "vec"), (dv4, "row")], [(D, F32)], accs=[(1, D), (1, D)])


def _chunks(tt, width):
    cc = min(CONV_LANE_CHUNK, width)
    rc = min(CONV_ROW_CHUNK, tt)
    for c0 in range(0, width, cc):
        for r0 in range(0, tt, rc):
            yield r0, rc, c0, cc


def _fir(win_ref, w_ref, n_taps, base, sign, tt, width, emit):
    for r0, rc, c0, cc in _chunks(tt, width):
        acc = jnp.zeros((rc, cc), F32)
        for k in range(n_taps):
            row = base + r0 + sign * (n_taps - 1 - k)
            acc = acc + w_ref[k:k + 1, c0:c0 + cc] * win_ref[row:row + rc, c0:c0 + cc]
        emit(r0, rc, c0, cc, acc)


def _fir_wgrad(d_ref, win_ref, dw8_ref, n_taps, pad, tt, width):
    for c0 in range(0, width, min(CONV_LANE_CHUNK, width)):
        cc = min(CONV_LANE_CHUNK, width)
        rc = min(CONV_ROW_CHUNK, tt)
        for k in range(n_taps):
            s = n_taps - 1 - k
            acc = jnp.zeros((SUBLANES, cc), F32)
            for r0 in range(0, tt, rc):
                prod = d_ref[r0:r0 + rc, c0:c0 + cc] * win_ref[pad + r0 - s:pad + r0 - s + rc, c0:c0 + cc]
                for q in range(0, rc, SUBLANES):
                    acc = acc + prod[q:q + SUBLANES]
            dw8_ref[SUBLANES * k:SUBLANES * (k + 1), c0:c0 + cc] += acc


def _glu(blk, D):
    return blk[:, :D].astype(F32) * _sigmoid(blk[:, D:].astype(F32))


CF_PAD = 32
SC_PAD = 16


def _cf_conv_fwd(name, a, w_dw, b_dw, g_cf):
    T, D2 = a.shape
    D = D2 // 2
    K = w_dw.shape[0]
    tt = _tile(T, CONV_TILE_ROWS)

    def fn(i, n, ins, outs, accs, scr):
        a_ref, prev_ref, w_ref, b_ref, g_ref = ins
        win_ref, v2_ref = scr
        win_ref[0:CF_PAD, :] = jnp.where(i > 0, _glu(prev_ref[...], D), 0.0)
        win_ref[CF_PAD:CF_PAD + tt, :] = _glu(a_ref[...], D)

        def emit(r0, rc, c0, cc, acc):
            v2_ref[r0:r0 + rc, c0:c0 + cc] = acc + b_ref[:, c0:c0 + cc]

        _fir(win_ref, w_ref, K, CF_PAD, -1, tt, D, emit)
        v2 = v2_ref[...]
        v3 = v2 * _rms_r(v2) * g_ref[...]
        outs[0][...] = v2
        outs[1][...] = (v3 * _sigmoid(v3)).astype(BF16)

    return _rowwise(name, fn, [(a, "row"), (a, ("prev", CF_PAD)), (w_dw, "vec"), (b_dw, "vec"), (g_cf, "vec")],
                    [(D, F32), (D, BF16)],
                    scratch=[pltpu.VMEM((CF_PAD + tt, D), F32), pltpu.VMEM((tt, D), F32)], tt=tt)


def _cf_conv_bwd(name, dv2, a, w_dw):
    T, D2 = a.shape
    D = D2 // 2
    K = w_dw.shape[0]
    tt = _tile(T, CONV_TILE_ROWS)

    def fn(i, n, ins, outs, accs, scr):
        d_ref, dnext_ref, a_ref, prev_ref, w_ref = ins
        v1win_ref, dwin_ref, dv1_ref, dw8_ref = scr

        @pl.when(i == 0)
        def _():
            dw8_ref[...] = jnp.zeros_like(dw8_ref)

        v1win_ref[0:CF_PAD, :] = jnp.where(i > 0, _glu(prev_ref[...], D), 0.0)
        v1win_ref[CF_PAD:CF_PAD + tt, :] = _glu(a_ref[...], D)
        dwin_ref[0:tt, :] = d_ref[...]
        dwin_ref[tt:tt + CF_PAD, :] = jnp.where(i < n - 1, dnext_ref[...], 0.0)

        def emit(r0, rc, c0, cc, acc):
            dv1_ref[r0:r0 + rc, c0:c0 + cc] = acc

        _fir(dwin_ref, w_ref, K, 0, 1, tt, D, emit)
        _fir_wgrad(d_ref, v1win_ref, dw8_ref, K, CF_PAD, tt, D)

        blk = a_ref[...]
        val = blk[:, :D].astype(F32)
        sg = _sigmoid(blk[:, D:].astype(F32))
        dv1 = dv1_ref[...]
        dval = dv1 * sg
        dgate = dv1 * val * sg * (1.0 - sg)
        outs[0][:, :D] = dval.astype(BF16)
        outs[0][:, D:] = dgate.astype(BF16)
        accs[1][:, :D] += _colsum(dval)
        accs[1][:, D:] += _colsum(dgate)

        @pl.when(i == n - 1)
        def _():
            for k in range(K):
                accs[0][k:k + 1, :] = _colsum(dw8_ref[SUBLANES * k:SUBLANES * (k + 1), :])

    return _rowwise(name, fn, [(dv2, "row"), (dv2, ("next", CF_PAD)), (a, "row"), (a, ("prev", CF_PAD)),
                               (w_dw, "vec")],
                    [(D2, BF16)], accs=[(K, D), (1, D2)],
                    scratch=[pltpu.VMEM((CF_PAD + tt, D), F32), pltpu.VMEM((tt + CF_PAD, D), F32),
                             pltpu.VMEM((tt, D), F32), pltpu.VMEM((SUBLANES * K, D), F32)], tt=tt)


def _sc_conv_fwd(name, bcv, w_conv):
    T, D3 = bcv.shape
    D = D3 // 3
    K = w_conv.shape[0]
    tt = _tile(T, CONV_TILE_ROWS)

    def cv_of(blk):
        return blk[:, D:2 * D].astype(F32) * blk[:, 2 * D:].astype(F32)

    def fn(i, n, ins, outs, accs, scr):
        x_ref, prev_ref, w_ref = ins
        win_ref, cc_ref = scr
        win_ref[0:SC_PAD, :] = jnp.where(i > 0, cv_of(prev_ref[...]), 0.0)
        win_ref[SC_PAD:SC_PAD + tt, :] = cv_of(x_ref[...])

        def emit(r0, rc, c0, cw, acc):
            cc_ref[r0:r0 + rc, c0:c0 + cw] = acc

        _fir(win_ref, w_ref, K, SC_PAD, -1, tt, D, emit)
        outs[0][...] = (x_ref[:, :D].astype(F32) * cc_ref[...]).astype(BF16)

    return _rowwise(name, fn, [(bcv, "row"), (bcv, ("prev", SC_PAD)), (w_conv, "vec")], [(D, BF16)],
                    scratch=[pltpu.VMEM((SC_PAD + tt, D), F32), pltpu.VMEM((tt, D), F32)], tt=tt)


def _sc_conv_bwd(name, dy, bcv, w_conv):
    T, D3 = bcv.shape
    D = D3 // 3
    K = w_conv.shape[0]
    tt = _tile(T, CONV_TILE_ROWS)

    def cv_of(blk):
        return blk[:, D:2 * D].astype(F32) * blk[:, 2 * D:].astype(F32)

    def fn(i, n, ins, outs, accs, scr):
        dy_ref, dynext_ref, x_ref, prev_ref, next_ref, w_ref = ins
        cvwin_ref, dccwin_ref, tmp_ref, dw8_ref = scr

        @pl.when(i == 0)
        def _():
            dw8_ref[...] = jnp.zeros_like(dw8_ref)

        cvwin_ref[0:SC_PAD, :] = jnp.where(i > 0, cv_of(prev_ref[...]), 0.0)
        cvwin_ref[SC_PAD:SC_PAD + tt, :] = cv_of(x_ref[...])

        def emit_cc(r0, rc, c0, cw, acc):
            tmp_ref[r0:r0 + rc, c0:c0 + cw] = acc

        _fir(cvwin_ref, w_ref, K, SC_PAD, -1, tt, D, emit_cc)
        dy_v = dy_ref[...].astype(F32)
        outs[0][:, :D] = (dy_v * tmp_ref[...]).astype(BF16)
        dccwin_ref[0:tt, :] = dy_v * x_ref[:, :D].astype(F32)
        dccwin_ref[tt:tt + SC_PAD, :] = jnp.where(
            i < n - 1, dynext_ref[...].astype(F32) * next_ref[:, :D].astype(F32), 0.0)

        def emit_dcv(r0, rc, c0, cw, acc):
            tmp_ref[r0:r0 + rc, c0:c0 + cw] = acc

        _fir(dccwin_ref, w_ref, K, 0, 1, tt, D, emit_dcv)
        _fir_wgrad(dccwin_ref, cvwin_ref, dw8_ref, K, SC_PAD, tt, D)
        dcv = tmp_ref[...]
        outs[0][:, D:2 * D] = (dcv * x_ref[:, 2 * D:].astype(F32)).astype(BF16)
        outs[0][:, 2 * D:] = (dcv * x_ref[:, D:2 * D].astype(F32)).astype(BF16)

        @pl.when(i == n - 1)
        def _():
            for k in range(K):
                accs[0][k:k + 1, :] = _colsum(dw8_ref[SUBLANES * k:SUBLANES * (k + 1), :])

    return _rowwise(name, fn, [(dy, "row"), (dy, ("next", SC_PAD)), (bcv, "row"), (bcv, ("prev", SC_PAD)),
                               (bcv, ("next", SC_PAD)), (w_conv, "vec")],
                    [(D3, BF16)], accs=[(K, D)],
                    scratch=[pltpu.VMEM((SC_PAD + tt, D), F32), pltpu.VMEM((tt + SC_PAD, D), F32),
                             pltpu.VMEM((tt, D), F32), pltpu.VMEM((SUBLANES * K, D), F32)], tt=tt)


def _place():
    x, y, c = lax.axis_index("x"), lax.axis_index("y"), lax.axis_index("c")
    chips = [(1 - x, y), (x, 1 - y), (1 - x, 1 - y)]
    return x, y, c, 2 * x + y, chips, (x, y, 1 - c)


def _half(rows, which):
    return pl.ds(pl.multiple_of(which * (rows // 2), SUBLANES), rows // 2)


_HBM = pl.BlockSpec(memory_space=pl.ANY)


def _gather_shards(name, items):
    n = len(items)
    shapes = [a.shape[-2:] for a, _ in items]

    def body(*refs):
        srcs, outs = refs[:n], refs[n:2 * n]
        send1, recv1, send2, recv2, lsem = refs[2 * n:]
        x, y, c, k, chips, sib = _place()

        def shard(i):
            return srcs[i] if items[i][1] is None else srcs[i].at[items[i][1]]

        started, locs = [], []
        for i in range(n):
            rows = shapes[i][0]
            lc = pltpu.make_async_copy(shard(i), outs[i].at[k], lsem.at[i])
            lc.start()
            locs.append(lc)
            for j, (cx, cy) in enumerate(chips):
                cp = pltpu.make_async_remote_copy(
                    src_ref=shard(i).at[_half(rows, c)], dst_ref=outs[i].at[k, _half(rows, c)],
                    send_sem=send1.at[i, j], recv_sem=recv1.at[i, j], device_id=(cx, cy, c), device_id_type=MESH)
                cp.start()
                started.append(cp)
        for i in range(n):
            rows = shapes[i][0]
            for j, (cx, cy) in enumerate(chips):
                blk = outs[i].at[2 * cx + cy, _half(rows, c)]
                pltpu.make_async_remote_copy(
                    src_ref=blk, dst_ref=blk, send_sem=send1.at[i, j], recv_sem=recv1.at[i, j],
                    device_id=(cx, cy, c), device_id_type=MESH).wait_recv()
                fw = pltpu.make_async_remote_copy(
                    src_ref=blk, dst_ref=blk, send_sem=send2.at[i, j], recv_sem=recv2.at[i, j],
                    device_id=sib, device_id_type=MESH)
                fw.start()
                started.append(fw)
        for i in range(n):
            rows = shapes[i][0]
            for j, (cx, cy) in enumerate(chips):
                blk = outs[i].at[2 * cx + cy, _half(rows, 1 - c)]
                pltpu.make_async_remote_copy(
                    src_ref=blk, dst_ref=blk, send_sem=send2.at[i, j], recv_sem=recv2.at[i, j],
                    device_id=sib, device_id_type=MESH).wait_recv()
        for cp in started:
            cp.wait_send()
        for lc in locs:
            lc.wait()

    return pl.pallas_call(
        body, name=name,
        in_specs=[_HBM] * n, out_specs=[_HBM] * n,
        out_shape=[jax.ShapeDtypeStruct((N_SHARDS,) + tuple(s), a.dtype) for s, (a, _) in zip(shapes, items)],
        scratch_shapes=[pltpu.SemaphoreType.DMA((n, 3))] * 4 + [pltpu.SemaphoreType.DMA((n,))],
    )(*[a for a, _ in items])


def _swap_halves(name, parts):
    n = len(parts)

    def body(*refs):
        srcs, outs = refs[:n], refs[n:2 * n]
        send, recv = refs[2 * n:]
        x, y, c, k, chips, sib = _place()
        cps = []
        for i in range(n):
            rows = parts[i].shape[1]
            cp = pltpu.make_async_remote_copy(
                src_ref=srcs[i].at[:, _half(rows, 1 - c)], dst_ref=outs[i],
                send_sem=send.at[i], recv_sem=recv.at[i], device_id=sib, device_id_type=MESH)
            cp.start()
            cps.append(cp)
        for cp in cps:
            cp.wait()

    return pl.pallas_call(
        body, name=name, in_specs=[_HBM] * n, out_specs=[_HBM] * n,
        out_shape=[jax.ShapeDtypeStruct((p.shape[0], p.shape[1] // 2, p.shape[2]), p.dtype) for p in parts],
        scratch_shapes=[pltpu.SemaphoreType.DMA((n,))] * 2,
    )(*parts)


def _scatter_to_owner(name, sums):
    n = len(sums)

    def body(*refs):
        srcs, outs = refs[:n], refs[n:2 * n]
        send, recv, lsem = refs[2 * n:]
        x, y, c, k, chips, sib = _place()
        sends, locs = [], []
        for i in range(n):
            lc = pltpu.make_async_copy(srcs[i].at[k], outs[i].at[k], lsem.at[i])
            lc.start()
            locs.append(lc)
            for j, (cx, cy) in enumerate(chips):
                cp = pltpu.make_async_remote_copy(
                    src_ref=srcs[i].at[2 * cx + cy], dst_ref=outs[i].at[k],
                    send_sem=send.at[i, j], recv_sem=recv.at[i, j], device_id=(cx, cy, c), device_id_type=MESH)
                cp.start()
                sends.append(cp)
        for i in range(n):
            for j, (cx, cy) in enumerate(chips):
                blk = outs[i].at[2 * cx + cy]
                pltpu.make_async_remote_copy(
                    src_ref=blk, dst_ref=blk, send_sem=send.at[i, j], recv_sem=recv.at[i, j],
                    device_id=(cx, cy, c), device_id_type=MESH).wait_recv()
        for cp in sends:
            cp.wait_send()
        for lc in locs:
            lc.wait()

    return pl.pallas_call(
        body, name=name, in_specs=[_HBM] * n, out_specs=[_HBM] * n,
        out_shape=[jax.ShapeDtypeStruct(s.shape, s.dtype) for s in sums],
        scratch_shapes=[pltpu.SemaphoreType.DMA((n, 3))] * 2 + [pltpu.SemaphoreType.DMA((n,))],
    )(*sums)


def _join_halves(name, finals):
    n = len(finals)

    def body(*refs):
        srcs, outs = refs[:n], refs[n:2 * n]
        send, recv, lsem = refs[2 * n:]
        x, y, c, k, chips, sib = _place()
        cps, locs = [], []
        for i in range(n):
            rows = 2 * finals[i].shape[0]
            lc = pltpu.make_async_copy(srcs[i], outs[i].at[_half(rows, c)], lsem.at[i])
            lc.start()
            locs.append(lc)
            cp = pltpu.make_async_remote_copy(
                src_ref=srcs[i], dst_ref=outs[i].at[_half(rows, c)],
                send_sem=send.at[i], recv_sem=recv.at[i], device_id=sib, device_id_type=MESH)
            cp.start()
            cps.append(cp)
        for i in range(n):
            rows = 2 * finals[i].shape[0]
            blk = outs[i].at[_half(rows, 1 - c)]
            pltpu.make_async_remote_copy(
                src_ref=blk, dst_ref=blk, send_sem=send.at[i], recv_sem=recv.at[i],
                device_id=sib, device_id_type=MESH).wait_recv()
        for cp in cps:
            cp.wait_send()
        for lc in locs:
            lc.wait()

    return pl.pallas_call(
        body, name=name, in_specs=[_HBM] * n, out_specs=[_HBM] * n,
        out_shape=[jax.ShapeDtypeStruct((2 * f.shape[0], f.shape[1]), f.dtype) for f in finals],
        scratch_shapes=[pltpu.SemaphoreType.DMA((n,))] * 3,
    )(*finals)


def _sum_over_devices(name, buf, loss_row):
    R, D = buf.shape

    def body(x_ref, all_ref, tot_ref, loss_ref, send_sems, recv_sems, local_sem):
        x, y, c, k, chips, sib = _place()
        me = (x, y, c)

        def block(px, py, pc):
            return all_ref.at[4 * px + 2 * py + pc]

        def copy(kk, blk, to, src=None):
            return pltpu.make_async_remote_copy(
                src_ref=block(*blk) if src is None else src, dst_ref=block(*blk),
                send_sem=send_sems.at[kk], recv_sem=recv_sems.at[kk], device_id=to, device_id_type=MESH)

        mine = pltpu.make_async_copy(x_ref, block(*me), local_sem)
        mine.start()
        first = [copy(0, me, sib, src=x_ref)]
        first += [copy(1 + j, me, (*chip, c), src=x_ref) for j, chip in enumerate(chips)]
        for cp in first:
            cp.start()
        passed = [copy(4 + j, (*chip, c), sib) for j, chip in enumerate(chips)]
        for j, chip in enumerate(chips):
            copy(1 + j, (*chip, c), me).wait_recv()
            passed[j].start()
        copy(0, sib, me).wait_recv()
        for j, chip in enumerate(chips):
            copy(4 + j, (*chip, 1 - c), me).wait_recv()
        for cp in first + passed:
            cp.wait_send()
        mine.wait()
        rc = _tile(R, 32)
        for r0 in range(0, R, rc):
            tot = all_ref[0, r0:r0 + rc, :]
            for d in range(1, N_DEVICES):
                tot = tot + all_ref[d, r0:r0 + rc, :]
            tot_ref[r0:r0 + rc, :] = tot
        loss = 0.5 * jnp.sum(tot_ref[loss_row:loss_row + 1, :]) / D
        loss_ref[...] = jnp.full(loss_ref.shape, loss, F32)

    vm = pl.BlockSpec(memory_space=pltpu.VMEM)
    return pl.pallas_call(
        body, name=name, in_specs=[vm], out_specs=[vm, vm, vm],
        out_shape=[jax.ShapeDtypeStruct((N_DEVICES, R, D), F32), jax.ShapeDtypeStruct((R, D), F32),
                   jax.ShapeDtypeStruct((SUBLANES, 128), F32)],
        scratch_shapes=[pltpu.SemaphoreType.DMA((7,)), pltpu.SemaphoreType.DMA((7,)), pltpu.SemaphoreType.DMA],
        compiler_params=pltpu.CompilerParams(vmem_limit_bytes=V7X_VMEM_LIMIT_BYTES),
    )(buf)[1:]


def _add_my_half(name, part, got, core):
    S, R, C = part.shape
    R2 = R // 2
    tr = _tile(R2, 512)
    q = R2 // tr

    def body(c_ref, p_ref, g_ref, o_ref):
        o_ref[...] = (p_ref[...].astype(F32) + g_ref[...].astype(F32)).astype(o_ref.dtype)

    return pl.pallas_call(
        body, name=name,
        grid_spec=pltpu.PrefetchScalarGridSpec(
            num_scalar_prefetch=1, grid=(S, q),
            in_specs=[pl.BlockSpec((None, tr, C), lambda s, r, c_ref: (s, c_ref[0] * q + r, 0)),
                      pl.BlockSpec((None, tr, C), lambda s, r, c_ref: (s, r, 0))],
            out_specs=pl.BlockSpec((None, tr, C), lambda s, r, c_ref: (s, r, 0))),
        out_shape=jax.ShapeDtypeStruct((S, R2, C), BF16),
        compiler_params=_cparams("parallel", "parallel"),
    )(core, part, got)


def _add_four(name, q4):
    S, R2, C = q4.shape
    tr = _tile(R2, 512)

    def body(q_ref, o_ref):
        acc = q_ref[0].astype(F32)
        for s in range(1, S):
            acc = acc + q_ref[s].astype(F32)
        o_ref[...] = acc

    return pl.pallas_call(
        body, name=name, grid=(R2 // tr,),
        in_specs=[pl.BlockSpec((S, tr, C), lambda r: (0, r, 0))],
        out_specs=pl.BlockSpec((tr, C), lambda r: (r, 0)),
        out_shape=jax.ShapeDtypeStruct((R2, C), F32),
        compiler_params=_cparams("parallel"),
    )(q4)


def _adamw(name, w, m, v, g):
    R, C = w.shape
    tr = SUBLANES
    while 2 * tr * C <= ADAMW_TILE_ELEMS:
        tr *= 2
    tr = _tile(R, tr)
    bc1 = 1.0 - ADAM_B1 ** ADAM_STEP
    bc2 = 1.0 - ADAM_B2 ** ADAM_STEP

    def body(w_ref, m_ref, v_ref, g_ref, go_ref, d_ref, mo_ref, vo_ref):
        gg = g_ref[...]
        m2 = ADAM_B1 * m_ref[...] + (1.0 - ADAM_B1) * gg
        v2 = ADAM_B2 * v_ref[...] + (1.0 - ADAM_B2) * (gg * gg)
        go_ref[...] = gg
        mo_ref[...] = m2
        vo_ref[...] = v2
        d_ref[...] = -ADAM_LR * ((m2 / bc1) / (jnp.sqrt(v2 / bc2) + ADAM_EPS) + ADAM_WD * w_ref[...])

    spec = pl.BlockSpec((tr, C), lambda r: (r, 0))
    return pl.pallas_call(
        body, name=name, grid=(R // tr,), in_specs=[spec] * 4, out_specs=[spec] * 4,
        out_shape=[jax.ShapeDtypeStruct((R, C), F32)] * 4,
        compiler_params=_cparams("parallel"),
    )(w, m, v, g)


def _reduce_layer(tag, parts, core):
    got = _swap_halves(f"rs_swap_{tag}", parts)
    sums = [_add_my_half(f"rs_add2_{tag}_{i}", p, g, core) for i, (p, g) in enumerate(zip(parts, got))]
    q4 = _scatter_to_owner(f"rs_scatter_{tag}", sums)
    finals = [_add_four(f"rs_add4_{tag}_{i}", q) for i, q in enumerate(q4)]
    return _join_halves(f"rs_join_{tag}", finals)


def _pad_rows(a):
    r = (-a.shape[0]) % SUBLANES
    return jnp.pad(a, ((0, r), (0, 0))) if r else a


def kernel(x, p, norm_mix, norm_mlp, norm_ple, cf_w_pw1, cf_b_pw1, cf_w_dw, cf_b_dw, cf_norm, cf_w_pw2, cf_b_pw2, sc_w_in, sc_w_conv, sc_w_out, mlp_w1, mlp_w2, ple_w_proj, ple_w_gate, norm_final, loss_target, m_norm_mix, m_norm_mlp, m_norm_ple, m_cf_w_pw1, m_cf_b_pw1, m_cf_w_dw, m_cf_b_dw, m_cf_norm, m_cf_w_pw2, m_cf_b_pw2, m_sc_w_in, m_sc_w_conv, m_sc_w_out, m_mlp_w1, m_mlp_w2, m_ple_w_proj, m_ple_w_gate, m_norm_final, v_norm_mix, v_norm_mlp, v_norm_ple, v_cf_w_pw1, v_cf_b_pw1, v_cf_w_dw, v_cf_b_dw, v_cf_norm, v_cf_w_pw2, v_cf_b_pw2, v_sc_w_in, v_sc_w_conv, v_sc_w_out, v_mlp_w1, v_mlp_w2, v_ple_w_proj, v_ple_w_gate, v_norm_final):
    T, D = x.shape[1], x.shape[2]
    KA, KB = cf_w_dw.shape[1], sc_w_conv.shape[1]
    core = lax.axis_index("c").astype(jnp.int32).reshape(1)
    chip = 2 * lax.axis_index("x") + lax.axis_index("y")

    big = dict(cf_w_pw1=cf_w_pw1, cf_w_pw2=cf_w_pw2, sc_w_in=sc_w_in, sc_w_out=sc_w_out,
               mlp_w1=mlp_w1, mlp_w2=mlp_w2, ple_w_proj=ple_w_proj, ple_w_gate=ple_w_gate)
    big16 = {k: v.astype(BF16) for k, v in big.items()}
    row_sharded = ("cf_w_pw2", "sc_w_out", "mlp_w2", "ple_w_gate")

    def layer_names(i):
        return (["cf_w_pw1", "cf_w_pw2"] if i % 2 == 0 else ["sc_w_in", "sc_w_out"]) + \
            ["mlp_w1", "mlp_w2", "ple_w_proj", "ple_w_gate"]

    def layer_index(i, name):
        return i if name.startswith(("mlp", "ple")) else i // 2

    W = []
    for i in range(DEPTH):
        names = layer_names(i)
        got = _gather_shards(f"gather_w_{i}", [(big16[nm], layer_index(i, nm)) for nm in names])
        wl = {}
        for nm, g4 in zip(names, got):
            wl[nm] = g4.reshape(1, N_SHARDS * g4.shape[1], g4.shape[2]) if nm in row_sharded else g4
        W.append(wl)
    conv_small = jnp.concatenate([_pad_rows(cf_w_dw[j]) for j in range(cf_w_dw.shape[0])]
                                 + [_pad_rows(sc_w_conv[j]) for j in range(sc_w_conv.shape[0])], axis=0)
    conv_all = _gather_shards("gather_conv_w", [(conv_small, None)])[0]
    conv_all = jnp.transpose(conv_all, (1, 0, 2)).reshape(conv_small.shape[0], D)
    ka_pad = KA + (-KA) % SUBLANES
    kb_pad = KB + (-KB) % SUBLANES
    w_dw_full = [conv_all[j * ka_pad:j * ka_pad + KA] for j in range(cf_w_dw.shape[0])]
    off = cf_w_dw.shape[0] * ka_pad
    w_conv_full = [conv_all[off + j * kb_pad:off + j * kb_pad + KB] for j in range(sc_w_conv.shape[0])]

    def vec(a):
        return a.reshape(1, -1)

    ident = lambda acc: (acc,)

    h = x[0]
    saved = []
    for i in range(DEPTH):
        j = i // 2
        wl = W[i]
        s = dict(h=h)
        s["u"] = _rms_fwd(f"rms_mix_{i}", h, vec(norm_mix[i]))
        if i % 2 == 0:
            s["a"] = _mm_nn(f"cf_pw1_{i}", s["u"], wl["cf_w_pw1"], lambda acc, b: (acc + b,), [BF16],
                            extras=[(vec(cf_b_pw1[j]), "n")])[0]
            s["v2"], s["v4"] = _cf_conv_fwd(f"cf_conv_{i}", s["a"], w_dw_full[j], vec(cf_b_dw[j]), vec(cf_norm[j]))
            h1 = _mm_nn(f"cf_pw2_{i}", s["v4"], wl["cf_w_pw2"], lambda acc, b, r: (r + (acc + b),), [F32],
                        extras=[(vec(cf_b_pw2[j]), "n"), (h, "mn")])[0]
        else:
            s["bcv"] = _mm_nn(f"sc_in_{i}", s["u"], wl["sc_w_in"], ident, [BF16])[0]
            s["y"] = _sc_conv_fwd(f"sc_conv_{i}", s["bcv"], w_conv_full[j])[0]
            h1 = _mm_nn(f"sc_out_{i}", s["y"], wl["sc_w_out"], lambda acc, r: (r + acc,), [F32],
                        extras=[(h, "mn")])[0]
        s["h1"] = h1
        s["u2"] = _rms_fwd(f"rms_mlp_{i}", h1, vec(norm_mlp[i]))
        s["z"], s["hd"] = _mm_nn(f"mlp_w1_{i}", s["u2"], wl["mlp_w1"],
                                 lambda acc: (acc, jnp.square(jnp.maximum(acc, 0.0))), [BF16, BF16])
        h2 = _mm_nn(f"mlp_w2_{i}", s["hd"], wl["mlp_w2"], lambda acc, r: (r + acc,), [F32], extras=[(h1, "mn")])[0]
        s["h2"] = h2
        s["n3"] = _rms_fwd(f"rms_ple_{i}", h2, vec(norm_ple[i]))
        s["p"] = p[i, 0]
        s["e"] = _mm_nn(f"ple_proj_{i}", s["p"], wl["ple_w_proj"], ident, [BF16])[0]
        h, s["q"] = _mm_nn(f"ple_gate_{i}", s["n3"], wl["ple_w_gate"],
                           lambda acc, r, e: (r + _sigmoid(acc) * e.astype(F32), acc), [F32, BF16],
                           extras=[(h2, "mn"), (s["e"], "mn")])
        saved.append(s)

    dh, dh16, dg_final, loss_cols = _loss_bwd("loss_bwd", h, vec(norm_final), loss_target[0])
    small = {"norm_final": dg_final, "loss": loss_cols}
    grads_big = {nm: [None] * v.shape[0] for nm, v in big.items()}
    for i in reversed(range(DEPTH)):
        j = i // 2
        wl, s = W[i], saved[i]
        dq, de = _ple_elem_bwd(f"ple_elem_bwd_{i}", dh, s["q"], s["e"])
        d_proj = _mm_tn(f"ple_proj_dw_{i}", s["p"], de, N_SHARDS)
        d_gate = _mm_tn(f"ple_gate_dw_{i}", s["n3"], dq, 1)
        dn3 = _mm_nt(f"ple_gate_dx_{i}", dq, wl["ple_w_gate"], ident, [F32])[0]
        dh, dh16, small[f"norm_ple_{i}"] = _rms_bwd(f"rms_ple_bwd_{i}", s["h2"], vec(norm_ple[i]), dn3, dh)

        d_w2 = _mm_tn(f"mlp_w2_dw_{i}", s["hd"], dh16, 1)
        dz = _mm_nt(f"mlp_w2_dx_{i}", dh16, wl["mlp_w2"],
                    lambda acc, z: (acc * (2.0 * jnp.maximum(z.astype(F32), 0.0)),), [BF16], extras=[s["z"]])[0]
        d_w1 = _mm_tn(f"mlp_w1_dw_{i}", s["u2"], dz, N_SHARDS)
        du2 = _mm_nt(f"mlp_w1_dx_{i}", dz, wl["mlp_w1"], ident, [F32])[0]
        if i % 2 == 0:
            dh, dh16, small[f"norm_mlp_{i}"], small[f"cf_b_pw2_{j}"] = _rms_bwd(
                f"rms_mlp_bwd_{i}", s["h1"], vec(norm_mlp[i]), du2, dh, want_colsum=True)
            d_mix_out = _mm_tn(f"cf_pw2_dw_{i}", s["v4"], dh16, 1)
            dv4 = _mm_nt(f"cf_pw2_dx_{i}", dh16, wl["cf_w_pw2"], ident, [F32])[0]
            dv2, small[f"cf_norm_{j}"], small[f"cf_b_dw_{j}"] = _cf_norm_bwd(
                f"cf_norm_bwd_{i}", s["v2"], vec(cf_norm[j]), dv4)
            da, small[f"cf_w_dw_{j}"], db1 = _cf_conv_bwd(f"cf_conv_bwd_{i}", dv2, s["a"], w_dw_full[j])
            small[f"cf_b_pw1_{j}"] = db1.reshape(2, D)
            d_mix_in = _mm_tn(f"cf_pw1_dw_{i}", s["u"], da, N_SHARDS)
            du = _mm_nt(f"cf_pw1_dx_{i}", da, wl["cf_w_pw1"], ident, [F32])[0]
        else:
            dh, dh16, small[f"norm_mlp_{i}"] = _rms_bwd(f"rms_mlp_bwd_{i}", s["h1"], vec(norm_mlp[i]), du2, dh)
            d_mix_out = _mm_tn(f"sc_out_dw_{i}", s["y"], dh16, 1)
            dy = _mm_nt(f"sc_out_dx_{i}", dh16, wl["sc_w_out"], ident, [F32])[0]
            da, small[f"sc_w_conv_{j}"] = _sc_conv_bwd(f"sc_conv_bwd_{i}", dy, s["bcv"], w_conv_full[j])
            d_mix_in = _mm_tn(f"sc_in_dw_{i}", s["u"], da, N_SHARDS)
            du = _mm_nt(f"sc_in_dx_{i}", da, wl["sc_w_in"], ident, [F32])[0]
        dh, dh16, small[f"norm_mix_{i}"] = _rms_bwd(f"rms_mix_bwd_{i}", s["h"], vec(norm_mix[i]), du, dh)

        names = layer_names(i)
        parts = [d_mix_in, d_mix_out, d_w1, d_w2, d_proj, d_gate]
        parts = [pt.reshape(N_SHARDS, pt.shape[1] // N_SHARDS, pt.shape[2]) if nm in row_sharded else pt
                 for nm, pt in zip(names, parts)]
        for nm, gsum in zip(names, _reduce_layer(f"{i}", parts, core)):
            grads_big[nm][layer_index(i, nm)] = gsum
    grad_x = dh.reshape(x.shape)

    order = sorted(small)
    pieces, where, row = [], {}, 0
    for nm in order:
        pc = _pad_rows(small[nm])
        where[nm] = (row, small[nm].shape[0])
        row += pc.shape[0]
        pieces.append(pc)
    total, loss_tile = _sum_over_devices("small_allsum", jnp.concatenate(pieces, axis=0), where["loss"][0])
    loss = loss_tile[0, 0]

    def small_sum(nm):
        r0, nr = where[nm]
        return total[r0:r0 + nr]

    def my_cols(a):
        return lax.dynamic_slice_in_dim(a, chip * (D // N_SHARDS), D // N_SHARDS, axis=1)

    g_small = {
        "norm_mix": jnp.concatenate([small_sum(f"norm_mix_{i}") for i in range(DEPTH)], axis=0),
        "norm_mlp": jnp.concatenate([small_sum(f"norm_mlp_{i}") for i in range(DEPTH)], axis=0),
        "norm_ple": jnp.concatenate([small_sum(f"norm_ple_{i}") for i in range(DEPTH)], axis=0),
        "cf_b_pw1": jnp.stack([small_sum(f"cf_b_pw1_{j}").reshape(2 * D) for j in range(DEPTH // 2)]),
        "cf_w_dw": jnp.stack([my_cols(small_sum(f"cf_w_dw_{j}")) for j in range(DEPTH // 2)]),
        "cf_b_dw": jnp.concatenate([small_sum(f"cf_b_dw_{j}") for j in range(DEPTH // 2)], axis=0),
        "cf_norm": jnp.concatenate([small_sum(f"cf_norm_{j}") for j in range(DEPTH // 2)], axis=0),
        "cf_b_pw2": jnp.concatenate([small_sum(f"cf_b_pw2_{j}") for j in range(DEPTH // 2)], axis=0),
        "sc_w_conv": jnp.stack([my_cols(small_sum(f"sc_w_conv_{j}")) for j in range(DEPTH // 2)]),
        "norm_final": small_sum("norm_final").reshape(D),
    }

    params = dict(norm_mix=norm_mix, norm_mlp=norm_mlp, norm_ple=norm_ple, cf_w_pw1=cf_w_pw1, cf_b_pw1=cf_b_pw1,
                  cf_w_dw=cf_w_dw, cf_b_dw=cf_b_dw, cf_norm=cf_norm, cf_w_pw2=cf_w_pw2, cf_b_pw2=cf_b_pw2,
                  sc_w_in=sc_w_in, sc_w_conv=sc_w_conv, sc_w_out=sc_w_out, mlp_w1=mlp_w1, mlp_w2=mlp_w2,
                  ple_w_proj=ple_w_proj, ple_w_gate=ple_w_gate, norm_final=norm_final)
    mom1 = dict(norm_mix=m_norm_mix, norm_mlp=m_norm_mlp, norm_ple=m_norm_ple, cf_w_pw1=m_cf_w_pw1,
                cf_b_pw1=m_cf_b_pw1, cf_w_dw=m_cf_w_dw, cf_b_dw=m_cf_b_dw, cf_norm=m_cf_norm, cf_w_pw2=m_cf_w_pw2,
                cf_b_pw2=m_cf_b_pw2, sc_w_in=m_sc_w_in, sc_w_conv=m_sc_w_conv, sc_w_out=m_sc_w_out,
                mlp_w1=m_mlp_w1, mlp_w2=m_mlp_w2, ple_w_proj=m_ple_w_proj, ple_w_gate=m_ple_w_gate,
                norm_final=m_norm_final)
    mom2 = dict(norm_mix=v_norm_mix, norm_mlp=v_norm_mlp, norm_ple=v_norm_ple, cf_w_pw1=v_cf_w_pw1,
                cf_b_pw1=v_cf_b_pw1, cf_w_dw=v_cf_w_dw, cf_b_dw=v_cf_b_dw, cf_norm=v_cf_norm, cf_w_pw2=v_cf_w_pw2,
                cf_b_pw2=v_cf_b_pw2, sc_w_in=v_sc_w_in, sc_w_conv=v_sc_w_conv, sc_w_out=v_sc_w_out,
                mlp_w1=v_mlp_w1, mlp_w2=v_mlp_w2, ple_w_proj=v_ple_w_proj, ple_w_gate=v_ple_w_gate,
                norm_final=v_norm_final)
    names_out = ["norm_mix", "norm_mlp", "norm_ple", "cf_w_pw1", "cf_b_pw1", "cf_w_dw", "cf_b_dw", "cf_norm",
                 "cf_w_pw2", "cf_b_pw2", "sc_w_in", "sc_w_conv", "sc_w_out", "mlp_w1", "mlp_w2", "ple_w_proj",
                 "ple_w_gate", "norm_final"]
    grad, delta, new_m, new_v = {}, {}, {}, {}
    for nm in names_out:
        w = params[nm]
        g = jnp.stack(grads_big[nm]) if nm in big else g_small[nm]
        cols = w.shape[-1] if w.ndim > 1 else w.shape[0]
        two_d = lambda a: a.reshape(-1, cols)
        res = _adamw(f"adamw_{nm}", two_d(w), two_d(mom1[nm]), two_d(mom2[nm]), two_d(g))
        grad[nm], delta[nm], new_m[nm], new_v[nm] = [r.reshape(w.shape) for r in res]

    return (loss, grad_x, *[grad[n] for n in names_out], *[delta[n] for n in names_out],
            *[new_m[n] for n in names_out], *[new_v[n] for n in names_out])
```

```python
import functools

import jax
import jax.numpy as jnp
from jax import lax
from jax.experimental import pallas as pl
from jax.experimental.pallas import tpu as pltpu

F32 = jnp.float32
BF16 = jnp.bfloat16

EPS = 1e-6
ADAM_LR = 0.001
ADAM_B1 = 0.9
ADAM_B2 = 0.999
ADAM_EPS = 1e-08
ADAM_WD = 0.01
ADAM_STEP = 10

DEPTH = 4
N_SHARDS = 4
N_DEVICES = 8
V7X_VMEM_LIMIT_BYTES = 56 * 1024 * 1024
SUBLANES = 8
MESH = pl.DeviceIdType.MESH

MM_TM = 1024
MM_TM_FUSED = 512
MM_TN = 1024
MM_TK = 2048
ADAMW_TILE_ELEMS = 256 * 2048

CONV_ROW_CHUNK = 32
CONV_LANE_CHUNK = 512
CONV_TILE_ROWS = 128
ROW_TILE = 256


def _tile(dim, pref):
    if dim <= pref:
        return dim
    t = pref
    while dim % t:
        t //= 2
    return t


def _cparams(*sem):
    return pltpu.CompilerParams(dimension_semantics=sem, vmem_limit_bytes=V7X_VMEM_LIMIT_BYTES)


def _sigmoid(x):
    return 1.0 / (1.0 + jnp.exp(-x))


def _rms_r(x):
    return lax.rsqrt(jnp.mean(x * x, axis=-1, keepdims=True) + EPS)


def _mm_nn(name, a, b3, epilogue, out_dtypes, extras=()):
    M, K = a.shape
    S, Kb, Ns = b3.shape
    assert Kb == K
    N = S * Ns
    fused = len(out_dtypes) > 1 or any(kind == "mn" for _, kind in extras)
    tm, tn, tk = _tile(M, MM_TM_FUSED if fused else MM_TM), _tile(Ns, MM_TN), _tile(K, MM_TK)
    per = Ns // tn
    nk = K // tk
    in_specs = [pl.BlockSpec((tm, tk), lambda i, j, k: (i, k)),
                pl.BlockSpec((None, tk, tn), lambda i, j, k: (j // per, k, j % per))]
    for _, kind in extras:
        if kind == "mn":
            in_specs.append(pl.BlockSpec((tm, tn), lambda i, j, k: (i, j)))
        else:
            in_specs.append(pl.BlockSpec((1, tn), lambda i, j, k: (0, j)))
    n_ex, n_o = len(extras), len(out_dtypes)

    def body(*refs):
        a_ref, b_ref = refs[:2]
        ex = refs[2:2 + n_ex]
        outs = refs[2 + n_ex:2 + n_ex + n_o]
        part = jnp.dot(a_ref[...].astype(BF16), b_ref[...], preferred_element_type=F32)

        def finish(acc):
            res = epilogue(acc, *[e[...] for e in ex])
            for r, o in zip(res, outs):
                o[...] = r.astype(o.dtype)

        if nk == 1:
            finish(part)
        else:
            acc_ref = refs[-1]
            k = pl.program_id(2)

            @pl.when(k == 0)
            def _():
                acc_ref[...] = part

            @pl.when(k > 0)
            def _():
                acc_ref[...] += part

            @pl.when(k == nk - 1)
            def _():
                finish(acc_ref[...])

    res = pl.pallas_call(
        body, name=name, grid=(M // tm, N // tn, nk),
        in_specs=in_specs,
        out_specs=[pl.BlockSpec((tm, tn), lambda i, j, k: (i, j)) for _ in out_dtypes],
        out_shape=[jax.ShapeDtypeStruct((M, N), dt) for dt in out_dtypes],
        scratch_shapes=[pltpu.VMEM((tm, tn), F32)] if nk > 1 else [],
        compiler_params=_cparams("parallel", "parallel", "arbitrary"),
    )(a, b3, *[e for e, _ in extras])
    return res


def _mm_nt(name, g, w3, epilogue, out_dtypes, extras=()):
    M, N = g.shape
    S, K, Ns = w3.shape
    assert S * Ns == N
    tm, tn, tkk = _tile(M, MM_TM_FUSED), _tile(Ns, MM_TN), _tile(K, MM_TK)
    per = Ns // tn
    nn = N // tn
    n_ex, n_o = len(extras), len(out_dtypes)

    def body(*refs):
        g_ref, w_ref = refs[:2]
        ex = refs[2:2 + n_ex]
        outs = refs[2 + n_ex:2 + n_ex + n_o]
        part = lax.dot_general(g_ref[...].astype(BF16), w_ref[...], (((1,), (1,)), ((), ())),
                               preferred_element_type=F32)

        def finish(acc):
            res = epilogue(acc, *[e[...] for e in ex])
            for r, o in zip(res, outs):
                o[...] = r.astype(o.dtype)

        if nn == 1:
            finish(part)
        else:
            acc_ref = refs[-1]
            n = pl.program_id(2)

            @pl.when(n == 0)
            def _():
                acc_ref[...] = part

            @pl.when(n > 0)
            def _():
                acc_ref[...] += part

            @pl.when(n == nn - 1)
            def _():
                finish(acc_ref[...])

    return pl.pallas_call(
        body, name=name, grid=(M // tm, K // tkk, nn),
        in_specs=[pl.BlockSpec((tm, tn), lambda i, kk, n: (i, n)),
                  pl.BlockSpec((None, tkk, tn), lambda i, kk, n: (n // per, kk, n % per))]
        + [pl.BlockSpec((tm, tkk), lambda i, kk, n: (i, kk)) for _ in extras],
        out_specs=[pl.BlockSpec((tm, tkk), lambda i, kk, n: (i, kk)) for _ in out_dtypes],
        out_shape=[jax.ShapeDtypeStruct((M, K), dt) for dt in out_dtypes],
        scratch_shapes=[pltpu.VMEM((tm, tkk), F32)] if nn > 1 else [],
        compiler_params=_cparams("parallel", "parallel", "arbitrary"),
    )(g, w3, *extras)


def _mm_tn(name, a, g, n_shards):
    T, K = a.shape
    _, N = g.shape
    Ns = N // n_shards
    tk, tn, tt = _tile(K, MM_TK), _tile(Ns, MM_TN), _tile(T, MM_TM)
    per = Ns // tn
    nt = T // tt

    def body(a_ref, g_ref, o_ref, acc_ref):
        t = pl.program_id(2)
        part = lax.dot_general(a_ref[...].astype(BF16), g_ref[...].astype(BF16), (((0,), (0,)), ((), ())),
                               preferred_element_type=F32)

        @pl.when(t == 0)
        def _():
            acc_ref[...] = part

        @pl.when(t > 0)
        def _():
            acc_ref[...] += part

        @pl.when(t == nt - 1)
        def _():
            o_ref[...] = acc_ref[...].astype(o_ref.dtype)

    return pl.pallas_call(
        body, name=name, grid=(K // tk, N // tn, nt),
        in_specs=[pl.BlockSpec((tt, tk), lambda i, j, t: (t, i)),
                  pl.BlockSpec((tt, tn), lambda i, j, t: (t, j))],
        out_specs=pl.BlockSpec((None, tk, tn), lambda i, j, t: (j // per, i, j % per)),
        out_shape=jax.ShapeDtypeStruct((n_shards, K, Ns), BF16),
        scratch_shapes=[pltpu.VMEM((tk, tn), F32)],
        compiler_params=_cparams("parallel", "parallel", "arbitrary"),
    )(a, g)


def _rowwise(name, fn, ins, outs, accs=(), scratch=(), tt=ROW_TILE):
    T = next(a.shape[0] for a, kind in ins if kind == "row")
    tt = _tile(T, tt)
    n = T // tt
    in_specs = []
    for a, kind in ins:
        w = a.shape[1]
        if kind == "row":
            in_specs.append(pl.BlockSpec((tt, w), lambda i: (i, 0)))
        elif kind == "vec":
            in_specs.append(pl.BlockSpec(a.shape, lambda i: (0, 0)))
        elif kind[0] == "prev":
            pad = kind[1]
            in_specs.append(pl.BlockSpec((pad, w), lambda i, q=tt // pad: (jnp.maximum(i * q - 1, 0), 0)))
        else:
            pad = kind[1]
            in_specs.append(pl.BlockSpec((pad, w), lambda i, q=tt // pad, last=T // pad - 1:
                                         (jnp.minimum((i + 1) * q, last), 0)))
    n_in, n_out, n_acc = len(ins), len(outs), len(accs)

    def body(*refs):
        i = pl.program_id(0)
        in_refs = refs[:n_in]
        out_refs = refs[n_in:n_in + n_out]
        acc_refs = refs[n_in + n_out:n_in + n_out + n_acc]
        scr = refs[n_in + n_out + n_acc:]
        if n_acc:
            @pl.when(i == 0)
            def _():
                for r in acc_refs:
                    r[...] = jnp.zeros_like(r)
        fn(i, n, in_refs, out_refs, acc_refs, scr)

    res = pl.pallas_call(
        body, name=name, grid=(n,),
        in_specs=in_specs,
        out_specs=[pl.BlockSpec((tt, w), lambda i: (i, 0)) for w, _ in outs]
        + [pl.BlockSpec((r, w), lambda i: (0, 0)) for r, w in accs],
        out_shape=[jax.ShapeDtypeStruct((T, w), dt) for w, dt in outs]
        + [jax.ShapeDtypeStruct((r, w), F32) for r, w in accs],
        scratch_shapes=list(scratch),
        compiler_params=_cparams("arbitrary"),
    )(*[a for a, _ in ins])
    return res


def _colsum(x):
    return jnp.sum(x, axis=0, keepdims=True)


def _rms_fwd(name, h, g):
    D = h.shape[1]

    def fn(i, n, ins, outs, accs, scr):
        x = ins[0][...]
        outs[0][...] = (x * _rms_r(x) * ins[1][...]).astype(BF16)

    return _rowwise(name, fn, [(h, "row"), (g, "vec")], [(D, BF16)])[0]


def _rms_bwd(name, h, g, du, dh_in, want_colsum=False):
    D = h.shape[1]

    def fn(i, n, ins, outs, accs, scr):
        x = ins[0][...]
        gg = ins[1][...]
        d = ins[2][...].astype(F32)
        r = _rms_r(x)
        xn = x * r
        t = d * gg
        dh = ins[3][...] + r * (t - xn * jnp.mean(t * xn, axis=-1, keepdims=True))
        outs[0][...] = dh
        outs[1][...] = dh.astype(BF16)
        accs[0][...] += _colsum(d * xn)
        if want_colsum:
            accs[1][...] += _colsum(dh)

    return _rowwise(name, fn, [(h, "row"), (g, "vec"), (du, "row"), (dh_in, "row")],
                    [(D, F32), (D, BF16)], accs=[(1, D)] * (2 if want_colsum else 1))


def _loss_bwd(name, h, g, tgt):
    D = h.shape[1]

    def fn(i, n, ins, outs, accs, scr):
        x = ins[0][...]
        gg = ins[1][...]
        r = _rms_r(x)
        xn = x * r
        err = xn * gg - ins[2][...]
        dy = err / D
        t = dy * gg
        dh = r * (t - xn * jnp.mean(t * xn, axis=-1, keepdims=True))
        outs[0][...] = dh
        outs[1][...] = dh.astype(BF16)
        accs[0][...] += _colsum(dy * xn)
        accs[1][...] += _colsum(err * err)

    return _rowwise(name, fn, [(h, "row"), (g, "vec"), (tgt, "row")], [(D, F32), (D, BF16)],
                    accs=[(1, D), (1, D)])


def _ple_elem_bwd(name, dh, q, e):
    D = dh.shape[1]

    def fn(i, n, ins, outs, accs, scr):
        d = ins[0][...]
        s = _sigmoid(ins[1][...].astype(F32))
        ee = ins[2][...].astype(F32)
        outs[0][...] = (d * ee * s * (1.0 - s)).astype(BF16)
        outs[1][...] = (d * s).astype(BF16)

    return _rowwise(name, fn, [(dh, "row"), (q, "row"), (e, "row")], [(D, BF16), (D, BF16)])


def _cf_norm_bwd(name, v2, g, dv4):
    D = v2.shape[1]

    def fn(i, n, ins, outs, accs, scr):
        x = ins[0][...]
        gg = ins[1][...]
        r = _rms_r(x)
        xn = x * r
        v3 = xn * gg
        s = _sigmoid(v3)
        dv3 = ins[2][...].astype(F32) * (s * (1.0 + v3 * (1.0 - s)))
        t = dv3 * gg
        dv2 = r * (t - xn * jnp.mean(t * xn, axis=-1, keepdims=True))
        outs[0][...] = dv2
        accs[0][...] += _colsum(dv3 * xn)
        accs[1][...] += _colsum(dv2)

    return _rowwise(name, fn, [(v2, "row"), (g, "vec"), (dv4, "row")], [(D, F32)], accs=[(1, D), (1, D)])


def _chunks(tt, width):
    cc = min(CONV_LANE_CHUNK, width)
    rc = min(CONV_ROW_CHUNK, tt)
    for c0 in range(0, width, cc):
        for r0 in range(0, tt, rc):
            yield r0, rc, c0, cc


def _fir(win_ref, w_ref, n_taps, base, sign, tt, width, emit):
    for r0, rc, c0, cc in _chunks(tt, width):
        acc = jnp.zeros((rc, cc), F32)
        for k in range(n_taps):
            row = base + r0 + sign * (n_taps - 1 - k)
            acc = acc + w_ref[k:k + 1, c0:c0 + cc] * win_ref[row:row + rc, c0:c0 + cc]
        emit(r0, rc, c0, cc, acc)


def _fir_wgrad(d_ref, win_ref, dw8_ref, n_taps, pad, tt, width):
    for c0 in range(0, width, min(CONV_LANE_CHUNK, width)):
        cc = min(CONV_LANE_CHUNK, width)
        rc = min(CONV_ROW_CHUNK, tt)
        for k in range(n_taps):
            s = n_taps - 1 - k
            acc = jnp.zeros((SUBLANES, cc), F32)
            for r0 in range(0, tt, rc):
                prod = d_ref[r0:r0 + rc, c0:c0 + cc] * win_ref[pad + r0 - s:pad + r0 - s + rc, c0:c0 + cc]
                for q in range(0, rc, SUBLANES):
                    acc = acc + prod[q:q + SUBLANES]
            dw8_ref[SUBLANES * k:SUBLANES * (k + 1), c0:c0 + cc] += acc


def _glu(blk, D):
    return blk[:, :D].astype(F32) * _sigmoid(blk[:, D:].astype(F32))


CF_PAD = 32
SC_PAD = 16


def _cf_conv_fwd(name, a, w_dw, b_dw, g_cf):
    T, D2 = a.shape
    D = D2 // 2
    K = w_dw.shape[0]
    tt = _tile(T, CONV_TILE_ROWS)

    def fn(i, n, ins, outs, accs, scr):
        a_ref, prev_ref, w_ref, b_ref, g_ref = ins
        win_ref, v2_ref = scr
        win_ref[0:CF_PAD, :] = jnp.where(i > 0, _glu(prev_ref[...], D), 0.0)
        win_ref[CF_PAD:CF_PAD + tt, :] = _glu(a_ref[...], D)

        def emit(r0, rc, c0, cc, acc):
            v2_ref[r0:r0 + rc, c0:c0 + cc] = acc + b_ref[:, c0:c0 + cc]

        _fir(win_ref, w_ref, K, CF_PAD, -1, tt, D, emit)
        v2 = v2_ref[...]
        v3 = v2 * _rms_r(v2) * g_ref[...]
        outs[0][...] = v2
        outs[1][...] = (v3 * _sigmoid(v3)).astype(BF16)

    return _rowwise(name, fn, [(a, "row"), (a, ("prev", CF_PAD)), (w_dw, "vec"), (b_dw, "vec"), (g_cf, "vec")],
                    [(D, F32), (D, BF16)],
                    scratch=[pltpu.VMEM((CF_PAD + tt, D), F32), pltpu.VMEM((tt, D), F32)], tt=tt)


def _cf_conv_bwd(name, dv2, a, w_dw):
    T, D2 = a.shape
    D = D2 // 2
    K = w_dw.shape[0]
    tt = _tile(T, CONV_TILE_ROWS)

    def fn(i, n, ins, outs, accs, scr):
        d_ref, dnext_ref, a_ref, prev_ref, w_ref = ins
        v1win_ref, dwin_ref, dv1_ref, dw8_ref = scr

        @pl.when(i == 0)
        def _():
            dw8_ref[...] = jnp.zeros_like(dw8_ref)

        v1win_ref[0:CF_PAD, :] = jnp.where(i > 0, _glu(prev_ref[...], D), 0.0)
        v1win_ref[CF_PAD:CF_PAD + tt, :] = _glu(a_ref[...], D)
        dwin_ref[0:tt, :] = d_ref[...]
        dwin_ref[tt:tt + CF_PAD, :] = jnp.where(i < n - 1, dnext_ref[...], 0.0)

        def emit(r0, rc, c0, cc, acc):
            dv1_ref[r0:r0 + rc, c0:c0 + cc] = acc

        _fir(dwin_ref, w_ref, K, 0, 1, tt, D, emit)
        _fir_wgrad(d_ref, v1win_ref, dw8_ref, K, CF_PAD, tt, D)

        blk = a_ref[...]
        val = blk[:, :D].astype(F32)
        sg = _sigmoid(blk[:, D:].astype(F32))
        dv1 = dv1_ref[...]
        dval = dv1 * sg
        dgate = dv1 * val * sg * (1.0 - sg)
        outs[0][:, :D] = dval.astype(BF16)
        outs[0][:, D:] = dgate.astype(BF16)
        accs[1][:, :D] += _colsum(dval)
        accs[1][:, D:] += _colsum(dgate)

        @pl.when(i == n - 1)
        def _():
            for k in range(K):
                accs[0][k:k + 1, :] = _colsum(dw8_ref[SUBLANES * k:SUBLANES * (k + 1), :])

    return _rowwise(name, fn, [(dv2, "row"), (dv2, ("next", CF_PAD)), (a, "row"), (a, ("prev", CF_PAD)),
                               (w_dw, "vec")],
                    [(D2, BF16)], accs=[(K, D), (1, D2)],
                    scratch=[pltpu.VMEM((CF_PAD + tt, D), F32), pltpu.VMEM((tt + CF_PAD, D), F32),
                             pltpu.VMEM((tt, D), F32), pltpu.VMEM((SUBLANES * K, D), F32)], tt=tt)


def _sc_conv_fwd(name, bcv, w_conv):
    T, D3 = bcv.shape
    D = D3 // 3
    K = w_conv.shape[0]
    tt = _tile(T, CONV_TILE_ROWS)

    def cv_of(blk):
        return blk[:, D:2 * D].astype(F32) * blk[:, 2 * D:].astype(F32)

    def fn(i, n, ins, outs, accs, scr):
        x_ref, prev_ref, w_ref = ins
        win_ref, cc_ref = scr
        win_ref[0:SC_PAD, :] = jnp.where(i > 0, cv_of(prev_ref[...]), 0.0)
        win_ref[SC_PAD:SC_PAD + tt, :] = cv_of(x_ref[...])

        def emit(r0, rc, c0, cw, acc):
            cc_ref[r0:r0 + rc, c0:c0 + cw] = acc

        _fir(win_ref, w_ref, K, SC_PAD, -1, tt, D, emit)
        outs[0][...] = (x_ref[:, :D].astype(F32) * cc_ref[...]).astype(BF16)

    return _rowwise(name, fn, [(bcv, "row"), (bcv, ("prev", SC_PAD)), (w_conv, "vec")], [(D, BF16)],
                    scratch=[pltpu.VMEM((SC_PAD + tt, D), F32), pltpu.VMEM((tt, D), F32)], tt=tt)


def _sc_conv_bwd(name, dy, bcv, w_conv):
    T, D3 = bcv.shape
    D = D3 // 3
    K = w_conv.shape[0]
    tt = _tile(T, CONV_TILE_ROWS)

    def cv_of(blk):
        return blk[:, D:2 * D].astype(F32) * blk[:, 2 * D:].astype(F32)

    def fn(i, n, ins, outs, accs, scr):
        dy_ref, dynext_ref, x_ref, prev_ref, next_ref, w_ref = ins
        cvwin_ref, dccwin_ref, tmp_ref, dw8_ref = scr

        @pl.when(i == 0)
        def _():
            dw8_ref[...] = jnp.zeros_like(dw8_ref)

        cvwin_ref[0:SC_PAD, :] = jnp.where(i > 0, cv_of(prev_ref[...]), 0.0)
        cvwin_ref[SC_PAD:SC_PAD + tt, :] = cv_of(x_ref[...])

        def emit_cc(r0, rc, c0, cw, acc):
            tmp_ref[r0:r0 + rc, c0:c0 + cw] = acc

        _fir(cvwin_ref, w_ref, K, SC_PAD, -1, tt, D, emit_cc)
        dy_v = dy_ref[...].astype(F32)
        outs[0][:, :D] = (dy_v * tmp_ref[...]).astype(BF16)
        dccwin_ref[0:tt, :] = dy_v * x_ref[:, :D].astype(F32)
        dccwin_ref[tt:tt + SC_PAD, :] = jnp.where(
            i < n - 1, dynext_ref[...].astype(F32) * next_ref[:, :D].astype(F32), 0.0)

        def emit_dcv(r0, rc, c0, cw, acc):
            tmp_ref[r0:r0 + rc, c0:c0 + cw] = acc

        _fir(dccwin_ref, w_ref, K, 0, 1, tt, D, emit_dcv)
        _fir_wgrad(dccwin_ref, cvwin_ref, dw8_ref, K, SC_PAD, tt, D)
        dcv = tmp_ref[...]
        outs[0][:, D:2 * D] = (dcv * x_ref[:, 2 * D:].astype(F32)).astype(BF16)
        outs[0][:, 2 * D:] = (dcv * x_ref[:, D:2 * D].astype(F32)).astype(BF16)

        @pl.when(i == n - 1)
        def _():
            for k in range(K):
                accs[0][k:k + 1, :] = _colsum(dw8_ref[SUBLANES * k:SUBLANES * (k + 1), :])

    return _rowwise(name, fn, [(dy, "row"), (dy, ("next", SC_PAD)), (bcv, "row"), (bcv, ("prev", SC_PAD)),
                               (bcv, ("next", SC_PAD)), (w_conv, "vec")],
                    [(D3, BF16)], accs=[(K, D)],
                    scratch=[pltpu.VMEM((SC_PAD + tt, D), F32), pltpu.VMEM((tt + SC_PAD, D), F32),
                             pltpu.VMEM((tt, D), F32), pltpu.VMEM((SUBLANES * K, D), F32)], tt=tt)


def _place():
    x, y, c = lax.axis_index("x"), lax.axis_index("y"), lax.axis_index("c")
    chips = [(1 - x, y), (x, 1 - y), (1 - x, 1 - y)]
    return x, y, c, 2 * x + y, chips, (x, y, 1 - c)


def _half(rows, which):
    return pl.ds(pl.multiple_of(which * (rows // 2), SUBLANES), rows // 2)


_HBM = pl.BlockSpec(memory_space=pl.ANY)


def _gather_shards(name, items):
    n = len(items)
    shapes = [a.shape[-2:] for a, _ in items]

    def body(*refs):
        srcs, outs = refs[:n], refs[n:2 * n]
        send1, recv1, send2, recv2, lsem = refs[2 * n:]
        x, y, c, k, chips, sib = _place()

        def shard(i):
            return srcs[i] if items[i][1] is None else srcs[i].at[items[i][1]]

        started, locs = [], []
        for i in range(n):
            rows = shapes[i][0]
            lc = pltpu.make_async_copy(shard(i), outs[i].at[k], lsem.at[i])
            lc.start()
            locs.append(lc)
            for j, (cx, cy) in enumerate(chips):
                cp = pltpu.make_async_remote_copy(
                    src_ref=shard(i).at[_half(rows, c)], dst_ref=outs[i].at[k, _half(rows, c)],
                    send_sem=send1.at[i, j], recv_sem=recv1.at[i, j], device_id=(cx, cy, c), device_id_type=MESH)
                cp.start()
                started.append(cp)
        for i in range(n):
            rows = shapes[i][0]
            for j, (cx, cy) in enumerate(chips):
                blk = outs[i].at[2 * cx + cy, _half(rows, c)]
                pltpu.make_async_remote_copy(
                    src_ref=blk, dst_ref=blk, send_sem=send1.at[i, j], recv_sem=recv1.at[i, j],
                    device_id=(cx, cy, c), device_id_type=MESH).wait_recv()
                fw = pltpu.make_async_remote_copy(
                    src_ref=blk, dst_ref=blk, send_sem=send2.at[i, j], recv_sem=recv2.at[i, j],
                    device_id=sib, device_id_type=MESH)
                fw.start()
                started.append(fw)
        for i in range(n):
            rows = shapes[i][0]
            for j, (cx, cy) in enumerate(chips):
                blk = outs[i].at[2 * cx + cy, _half(rows, 1 - c)]
                pltpu.make_async_remote_copy(
                    src_ref=blk, dst_ref=blk, send_sem=send2.at[i, j], recv_sem=recv2.at[i, j],
                    device_id=sib, device_id_type=MESH).wait_recv()
        for cp in started:
            cp.wait_send()
        for lc in locs:
            lc.wait()

    return pl.pallas_call(
        body, name=name,
        in_specs=[_HBM] * n, out_specs=[_HBM] * n,
        out_shape=[jax.ShapeDtypeStruct((N_SHARDS,) + tuple(s), a.dtype) for s, (a, _) in zip(shapes, items)],
        scratch_shapes=[pltpu.SemaphoreType.DMA((n, 3))] * 4 + [pltpu.SemaphoreType.DMA((n,))],
    )(*[a for a, _ in items])


def _cast_place(name, w, layer, chip):
    _, R, C = w.shape
    tr = _tile(R, 256)

    def body(k_ref, w_ref, o_ref):
        o_ref[...] = w_ref[...].astype(BF16)

    return pl.pallas_call(
        body, name=name,
        grid_spec=pltpu.PrefetchScalarGridSpec(
            num_scalar_prefetch=1, grid=(R // tr,),
            in_specs=[pl.BlockSpec((None, tr, C), lambda r, k_ref: (layer, r, 0))],
            out_specs=pl.BlockSpec((None, tr, C), lambda r, k_ref: (k_ref[0], r, 0))),
        out_shape=jax.ShapeDtypeStruct((N_SHARDS, R, C), BF16),
        compiler_params=_cparams("parallel"),
    )(chip, w)


def _gather_placed(name, bufs):
    n = len(bufs)

    def body(*refs):
        outs = refs[n:2 * n]
        send1, recv1, send2, recv2 = refs[2 * n:]
        x, y, c, k, chips, sib = _place()
        started = []
        for i in range(n):
            rows = bufs[i].shape[1]
            for j, (cx, cy) in enumerate(chips):
                blk = outs[i].at[k, _half(rows, c)]
                cp = pltpu.make_async_remote_copy(
                    src_ref=blk, dst_ref=blk, send_sem=send1.at[i, j], recv_sem=recv1.at[i, j],
                    device_id=(cx, cy, c), device_id_type=MESH)
                cp.start()
                started.append(cp)
        for i in range(n):
            rows = bufs[i].shape[1]
            for j, (cx, cy) in enumerate(chips):
                blk = outs[i].at[2 * cx + cy, _half(rows, c)]
                pltpu.make_async_remote_copy(
                    src_ref=blk, dst_ref=blk, send_sem=send1.at[i, j], recv_sem=recv1.at[i, j],
                    device_id=(cx, cy, c), device_id_type=MESH).wait_recv()
                fw = pltpu.make_async_remote_copy(
                    src_ref=blk, dst_ref=blk, send_sem=send2.at[i, j], recv_sem=recv2.at[i, j],
                    device_id=sib, device_id_type=MESH)
                fw.start()
                started.append(fw)
        for i in range(n):
            rows = bufs[i].shape[1]
            for j, (cx, cy) in enumerate(chips):
                blk = outs[i].at[2 * cx + cy, _half(rows, 1 - c)]
                pltpu.make_async_remote_copy(
                    src_ref=blk, dst_ref=blk, send_sem=send2.at[i, j], recv_sem=recv2.at[i, j],
                    device_id=sib, device_id_type=MESH).wait_recv()
        for cp in started:
            cp.wait_send()

    return pl.pallas_call(
        body, name=name, in_specs=[_HBM] * n, out_specs=[_HBM] * n,
        out_shape=[jax.ShapeDtypeStruct(b.shape, b.dtype) for b in bufs],
        input_output_aliases={i: i for i in range(n)},
        scratch_shapes=[pltpu.SemaphoreType.DMA((n, 3))] * 4,
    )(*bufs)


def _swap_halves(name, parts):
    n = len(parts)

    def body(*refs):
        srcs, outs = refs[:n], refs[n:2 * n]
        send, recv = refs[2 * n:]
        x, y, c, k, chips, sib = _place()
        cps = []
        for i in range(n):
            rows = parts[i].shape[1]
            cp = pltpu.make_async_remote_copy(
                src_ref=srcs[i].at[:, _half(rows, 1 - c)], dst_ref=outs[i],
                send_sem=send.at[i], recv_sem=recv.at[i], device_id=sib, device_id_type=MESH)
            cp.start()
            cps.append(cp)
        for cp in cps:
            cp.wait()

    return pl.pallas_call(
        body, name=name, in_specs=[_HBM] * n, out_specs=[_HBM] * n,
        out_shape=[jax.ShapeDtypeStruct((p.shape[0], p.shape[1] // 2, p.shape[2]), p.dtype) for p in parts],
        scratch_shapes=[pltpu.SemaphoreType.DMA((n,))] * 2,
    )(*parts)


def _scatter_to_owner(name, sums):
    n = len(sums)

    def body(*refs):
        srcs, outs = refs[:n], refs[n:2 * n]
        send, recv = refs[2 * n:]
        x, y, c, k, chips, sib = _place()
        sends = []
        for i in range(n):
            for j, (cx, cy) in enumerate(chips):
                cp = pltpu.make_async_remote_copy(
                    src_ref=srcs[i].at[2 * cx + cy], dst_ref=outs[i].at[j],
                    send_sem=send.at[i, j], recv_sem=recv.at[i, j], device_id=(cx, cy, c), device_id_type=MESH)
                cp.start()
                sends.append(cp)
        for i in range(n):
            for j, (cx, cy) in enumerate(chips):
                blk = outs[i].at[j]
                pltpu.make_async_remote_copy(
                    src_ref=blk, dst_ref=blk, send_sem=send.at[i, j], recv_sem=recv.at[i, j],
                    device_id=(cx, cy, c), device_id_type=MESH).wait_recv()
        for cp in sends:
            cp.wait_send()

    return pl.pallas_call(
        body, name=name, in_specs=[_HBM] * n, out_specs=[_HBM] * n,
        out_shape=[jax.ShapeDtypeStruct((3,) + s.shape[1:], s.dtype) for s in sums],
        scratch_shapes=[pltpu.SemaphoreType.DMA((n, 3))] * 2,
    )(*sums)


def _join_halves(name, fulls):
    n = len(fulls)

    def body(*refs):
        outs = refs[n:2 * n]
        send, recv = refs[2 * n:]
        x, y, c, k, chips, sib = _place()
        cps = []
        for i in range(n):
            blk = outs[i].at[_half(fulls[i].shape[0], c)]
            cp = pltpu.make_async_remote_copy(
                src_ref=blk, dst_ref=blk, send_sem=send.at[i], recv_sem=recv.at[i],
                device_id=sib, device_id_type=MESH)
            cp.start()
            cps.append(cp)
        for i in range(n):
            blk = outs[i].at[_half(fulls[i].shape[0], 1 - c)]
            pltpu.make_async_remote_copy(
                src_ref=blk, dst_ref=blk, send_sem=send.at[i], recv_sem=recv.at[i],
                device_id=sib, device_id_type=MESH).wait_recv()
        for cp in cps:
            cp.wait_send()

    return pl.pallas_call(
        body, name=name, in_specs=[_HBM] * n, out_specs=[_HBM] * n,
        out_shape=[jax.ShapeDtypeStruct(f.shape, f.dtype) for f in fulls],
        input_output_aliases={i: i for i in range(n)},
        scratch_shapes=[pltpu.SemaphoreType.DMA((n,))] * 2,
    )(*fulls)


def _sum_over_devices(name, buf, loss_row):
    R, D = buf.shape

    def body(x_ref, all_ref, tot_ref, loss_ref, send_sems, recv_sems, local_sem):
        x, y, c, k, chips, sib = _place()
        me = (x, y, c)

        def block(px, py, pc):
            return all_ref.at[4 * px + 2 * py + pc]

        def copy(kk, blk, to, src=None):
            return pltpu.make_async_remote_copy(
                src_ref=block(*blk) if src is None else src, dst_ref=block(*blk),
                send_sem=send_sems.at[kk], recv_sem=recv_sems.at[kk], device_id=to, device_id_type=MESH)

        mine = pltpu.make_async_copy(x_ref, block(*me), local_sem)
        mine.start()
        first = [copy(0, me, sib, src=x_ref)]
        first += [copy(1 + j, me, (*chip, c), src=x_ref) for j, chip in enumerate(chips)]
        for cp in first:
            cp.start()
        passed = [copy(4 + j, (*chip, c), sib) for j, chip in enumerate(chips)]
        for j, chip in enumerate(chips):
            copy(1 + j, (*chip, c), me).wait_recv()
            passed[j].start()
        copy(0, sib, me).wait_recv()
        for j, chip in enumerate(chips):
            copy(4 + j, (*chip, 1 - c), me).wait_recv()
        for cp in first + passed:
            cp.wait_send()
        mine.wait()
        rc = _tile(R, 32)
        for r0 in range(0, R, rc):
            tot = all_ref[0, r0:r0 + rc, :]
            for d in range(1, N_DEVICES):
                tot = tot + all_ref[d, r0:r0 + rc, :]
            tot_ref[r0:r0 + rc, :] = tot
        loss = 0.5 * jnp.sum(tot_ref[loss_row:loss_row + 1, :]) / D
        loss_ref[...] = jnp.full(loss_ref.shape, loss, F32)

    vm = pl.BlockSpec(memory_space=pltpu.VMEM)
    return pl.pallas_call(
        body, name=name, in_specs=[vm], out_specs=[vm, vm, vm],
        out_shape=[jax.ShapeDtypeStruct((N_DEVICES, R, D), F32), jax.ShapeDtypeStruct((R, D), F32),
                   jax.ShapeDtypeStruct((SUBLANES, 128), F32)],
        scratch_shapes=[pltpu.SemaphoreType.DMA((7,)), pltpu.SemaphoreType.DMA((7,)), pltpu.SemaphoreType.DMA],
        compiler_params=pltpu.CompilerParams(vmem_limit_bytes=V7X_VMEM_LIMIT_BYTES),
    )(buf)[1:]


def _add_my_half(name, part, got, core):
    S, R, C = part.shape
    R2 = R // 2
    tr = _tile(R2, 512)
    q = R2 // tr

    def body(c_ref, p_ref, g_ref, o_ref):
        o_ref[...] = (p_ref[...].astype(F32) + g_ref[...].astype(F32)).astype(o_ref.dtype)

    return pl.pallas_call(
        body, name=name,
        grid_spec=pltpu.PrefetchScalarGridSpec(
            num_scalar_prefetch=1, grid=(S, q),
            in_specs=[pl.BlockSpec((None, tr, C), lambda s, r, c_ref: (s, c_ref[0] * q + r, 0)),
                      pl.BlockSpec((None, tr, C), lambda s, r, c_ref: (s, r, 0))],
            out_specs=pl.BlockSpec((None, tr, C), lambda s, r, c_ref: (s, r, 0))),
        out_shape=jax.ShapeDtypeStruct((S, R2, C), BF16),
        compiler_params=_cparams("parallel", "parallel"),
    )(core, part, got)


def _add_owner(name, sums, got, core_chip):
    _, R2, C = sums.shape
    tr = _tile(R2, 512)
    q = R2 // tr

    def body(ck_ref, s_ref, g_ref, o_ref):
        acc = s_ref[...].astype(F32)
        for j in range(3):
            acc = acc + g_ref[j].astype(F32)
        o_ref[...] = acc

    return pl.pallas_call(
        body, name=name,
        grid_spec=pltpu.PrefetchScalarGridSpec(
            num_scalar_prefetch=1, grid=(q,),
            in_specs=[pl.BlockSpec((None, tr, C), lambda r, ck: (ck[1], r, 0)),
                      pl.BlockSpec((3, tr, C), lambda r, ck: (0, r, 0))],
            out_specs=pl.BlockSpec((tr, C), lambda r, ck: (ck[0] * q + r, 0))),
        out_shape=jax.ShapeDtypeStruct((2 * R2, C), F32),
        compiler_params=_cparams("parallel"),
    )(core_chip, sums, got)


def _adamw(name, w, m, v, g):
    R, C = w.shape
    tr = SUBLANES
    while 2 * tr * C <= ADAMW_TILE_ELEMS:
        tr *= 2
    tr = _tile(R, tr)
    bc1 = 1.0 - ADAM_B1 ** ADAM_STEP
    bc2 = 1.0 - ADAM_B2 ** ADAM_STEP

    def body(w_ref, m_ref, v_ref, g_ref, go_ref, d_ref, mo_ref, vo_ref):
        gg = g_ref[...]
        m2 = ADAM_B1 * m_ref[...] + (1.0 - ADAM_B1) * gg
        v2 = ADAM_B2 * v_ref[...] + (1.0 - ADAM_B2) * (gg * gg)
        go_ref[...] = gg
        mo_ref[...] = m2
        vo_ref[...] = v2
        d_ref[...] = -ADAM_LR * ((m2 / bc1) / (jnp.sqrt(v2 / bc2) + ADAM_EPS) + ADAM_WD * w_ref[...])

    spec = pl.BlockSpec((tr, C), lambda r: (r, 0))
    return pl.pallas_call(
        body, name=name, grid=(R // tr,), in_specs=[spec] * 4, out_specs=[spec] * 4,
        out_shape=[jax.ShapeDtypeStruct((R, C), F32)] * 4,
        compiler_params=_cparams("parallel"),
    )(w, m, v, g)


def _adamw_slab(name, w, m, v, g, layer, prev):
    L, R, C = w.shape
    tr = SUBLANES
    while 2 * tr * C <= ADAMW_TILE_ELEMS:
        tr *= 2
    tr = _tile(R, tr)
    bc1 = 1.0 - ADAM_B1 ** ADAM_STEP
    bc2 = 1.0 - ADAM_B2 ** ADAM_STEP

    def body(w_ref, m_ref, v_ref, g_ref, *rest):
        go_ref, d_ref, mo_ref, vo_ref = rest[-4:]
        gg = g_ref[...]
        m2 = ADAM_B1 * m_ref[...] + (1.0 - ADAM_B1) * gg
        v2 = ADAM_B2 * v_ref[...] + (1.0 - ADAM_B2) * (gg * gg)
        go_ref[...] = gg
        mo_ref[...] = m2
        vo_ref[...] = v2
        d_ref[...] = -ADAM_LR * ((m2 / bc1) / (jnp.sqrt(v2 / bc2) + ADAM_EPS) + ADAM_WD * w_ref[...])

    slab = pl.BlockSpec((None, tr, C), lambda r: (layer, r, 0))
    n_prev = 0 if prev is None else 4
    return pl.pallas_call(
        body, name=name, grid=(R // tr,),
        in_specs=[slab] * 3 + [pl.BlockSpec((tr, C), lambda r: (r, 0))] + [_HBM] * n_prev,
        out_specs=[slab] * 4,
        out_shape=[jax.ShapeDtypeStruct((L, R, C), F32)] * 4,
        input_output_aliases={4 + i: i for i in range(n_prev)},
        compiler_params=_cparams("parallel"),
    )(w, m, v, g, *(prev or ()))


def _reduce_layer(tag, parts, core, core_chip):
    got = _swap_halves(f"rs_swap_{tag}", parts)
    sums = [_add_my_half(f"rs_add2_{tag}_{i}", p, g, core) for i, (p, g) in enumerate(zip(parts, got))]
    q3 = _scatter_to_owner(f"rs_scatter_{tag}", sums)
    fulls = [_add_owner(f"rs_add4_{tag}_{i}", s, q, core_chip) for i, (s, q) in enumerate(zip(sums, q3))]
    return _join_halves(f"rs_join_{tag}", fulls)


def _pad_rows(a):
    r = (-a.shape[0]) % SUBLANES
    return jnp.pad(a, ((0, r), (0, 0))) if r else a


def kernel(x, p, norm_mix, norm_mlp, norm_ple, cf_w_pw1, cf_b_pw1, cf_w_dw, cf_b_dw, cf_norm, cf_w_pw2, cf_b_pw2, sc_w_in, sc_w_conv, sc_w_out, mlp_w1, mlp_w2, ple_w_proj, ple_w_gate, norm_final, loss_target, m_norm_mix, m_norm_mlp, m_norm_ple, m_cf_w_pw1, m_cf_b_pw1, m_cf_w_dw, m_cf_b_dw, m_cf_norm, m_cf_w_pw2, m_cf_b_pw2, m_sc_w_in, m_sc_w_conv, m_sc_w_out, m_mlp_w1, m_mlp_w2, m_ple_w_proj, m_ple_w_gate, m_norm_final, v_norm_mix, v_norm_mlp, v_norm_ple, v_cf_w_pw1, v_cf_b_pw1, v_cf_w_dw, v_cf_b_dw, v_cf_norm, v_cf_w_pw2, v_cf_b_pw2, v_sc_w_in, v_sc_w_conv, v_sc_w_out, v_mlp_w1, v_mlp_w2, v_ple_w_proj, v_ple_w_gate, v_norm_final):
    T, D = x.shape[1], x.shape[2]
    KA, KB = cf_w_dw.shape[1], sc_w_conv.shape[1]
    chip = (2 * lax.axis_index("x") + lax.axis_index("y")).astype(jnp.int32)
    core = lax.axis_index("c").astype(jnp.int32).reshape(1)
    chip1 = chip.reshape(1)
    core_chip = jnp.stack([lax.axis_index("c").astype(jnp.int32), chip])

    big = dict(cf_w_pw1=cf_w_pw1, cf_w_pw2=cf_w_pw2, sc_w_in=sc_w_in, sc_w_out=sc_w_out,
               mlp_w1=mlp_w1, mlp_w2=mlp_w2, ple_w_proj=ple_w_proj, ple_w_gate=ple_w_gate)
    row_sharded = ("cf_w_pw2", "sc_w_out", "mlp_w2", "ple_w_gate")

    def layer_names(i):
        return (["cf_w_pw1", "cf_w_pw2"] if i % 2 == 0 else ["sc_w_in", "sc_w_out"]) + \
            ["mlp_w1", "mlp_w2", "ple_w_proj", "ple_w_gate"]

    def layer_index(i, name):
        return i if name.startswith(("mlp", "ple")) else i // 2

    W = []
    for i in range(DEPTH):
        names = layer_names(i)
        got = _gather_placed(f"gather_w_{i}", [_cast_place(f"place_{nm}_{i}", big[nm], layer_index(i, nm), chip1)
                                               for nm in names])
        wl = {}
        for nm, g4 in zip(names, got):
            wl[nm] = g4.reshape(1, N_SHARDS * g4.shape[1], g4.shape[2]) if nm in row_sharded else g4
        W.append(wl)
    conv_small = jnp.concatenate([_pad_rows(cf_w_dw[j]) for j in range(cf_w_dw.shape[0])]
                                 + [_pad_rows(sc_w_conv[j]) for j in range(sc_w_conv.shape[0])], axis=0)
    conv_all = _gather_shards("gather_conv_w", [(conv_small, None)])[0]
    conv_all = jnp.transpose(conv_all, (1, 0, 2)).reshape(conv_small.shape[0], D)
    ka_pad = KA + (-KA) % SUBLANES
    kb_pad = KB + (-KB) % SUBLANES
    w_dw_full = [conv_all[j * ka_pad:j * ka_pad + KA] for j in range(cf_w_dw.shape[0])]
    off = cf_w_dw.shape[0] * ka_pad
    w_conv_full = [conv_all[off + j * kb_pad:off + j * kb_pad + KB] for j in range(sc_w_conv.shape[0])]

    def vec(a):
        return a.reshape(1, -1)

    ident = lambda acc: (acc,)

    h = x[0]
    saved = []
    for i in range(DEPTH):
        j = i // 2
        wl = W[i]
        s = dict(h=h)
        s["u"] = _rms_fwd(f"rms_mix_{i}", h, vec(norm_mix[i]))
        if i % 2 == 0:
            s["a"] = _mm_nn(f"cf_pw1_{i}", s["u"], wl["cf_w_pw1"], lambda acc, b: (acc + b,), [BF16],
                            extras=[(vec(cf_b_pw1[j]), "n")])[0]
            s["v2"], s["v4"] = _cf_conv_fwd(f"cf_conv_{i}", s["a"], w_dw_full[j], vec(cf_b_dw[j]), vec(cf_norm[j]))
            h1 = _mm_nn(f"cf_pw2_{i}", s["v4"], wl["cf_w_pw2"], lambda acc, b, r: (r + (acc + b),), [F32],
                        extras=[(vec(cf_b_pw2[j]), "n"), (h, "mn")])[0]
        else:
            s["bcv"] = _mm_nn(f"sc_in_{i}", s["u"], wl["sc_w_in"], ident, [BF16])[0]
            s["y"] = _sc_conv_fwd(f"sc_conv_{i}", s["bcv"], w_conv_full[j])[0]
            h1 = _mm_nn(f"sc_out_{i}", s["y"], wl["sc_w_out"], lambda acc, r: (r + acc,), [F32],
                        extras=[(h, "mn")])[0]
        s["h1"] = h1
        s["u2"] = _rms_fwd(f"rms_mlp_{i}", h1, vec(norm_mlp[i]))
        s["z"], s["hd"] = _mm_nn(f"mlp_w1_{i}", s["u2"], wl["mlp_w1"],
                                 lambda acc: (acc, jnp.square(jnp.maximum(acc, 0.0))), [BF16, BF16])
        h2 = _mm_nn(f"mlp_w2_{i}", s["hd"], wl["mlp_w2"], lambda acc, r: (r + acc,), [F32], extras=[(h1, "mn")])[0]
        s["h2"] = h2
        s["n3"] = _rms_fwd(f"rms_ple_{i}", h2, vec(norm_ple[i]))
        s["p"] = p[i, 0]
        s["e"] = _mm_nn(f"ple_proj_{i}", s["p"], wl["ple_w_proj"], ident, [BF16])[0]
        h, s["q"] = _mm_nn(f"ple_gate_{i}", s["n3"], wl["ple_w_gate"],
                           lambda acc, r, e: (r + _sigmoid(acc) * e.astype(F32), acc), [F32, BF16],
                           extras=[(h2, "mn"), (s["e"], "mn")])
        saved.append(s)

    dh, dh16, dg_final, loss_cols = _loss_bwd("loss_bwd", h, vec(norm_final), loss_target[0])
    small = {"norm_final": dg_final, "loss": loss_cols}
    grads_big = {nm: [None] * v.shape[0] for nm, v in big.items()}
    for i in reversed(range(DEPTH)):
        j = i // 2
        wl, s = W[i], saved[i]
        dq, de = _ple_elem_bwd(f"ple_elem_bwd_{i}", dh, s["q"], s["e"])
        d_proj = _mm_tn(f"ple_proj_dw_{i}", s["p"], de, N_SHARDS)
        d_gate = _mm_tn(f"ple_gate_dw_{i}", s["n3"], dq, 1)
        dn3 = _mm_nt(f"ple_gate_dx_{i}", dq, wl["ple_w_gate"], ident, [F32])[0]
        dh, dh16, small[f"norm_ple_{i}"] = _rms_bwd(f"rms_ple_bwd_{i}", s["h2"], vec(norm_ple[i]), dn3, dh)

        d_w2 = _mm_tn(f"mlp_w2_dw_{i}", s["hd"], dh16, 1)
        dz = _mm_nt(f"mlp_w2_dx_{i}", dh16, wl["mlp_w2"],
                    lambda acc, z: (acc * (2.0 * jnp.maximum(z.astype(F32), 0.0)),), [BF16], extras=[s["z"]])[0]
        d_w1 = _mm_tn(f"mlp_w1_dw_{i}", s["u2"], dz, N_SHARDS)
        du2 = _mm_nt(f"mlp_w1_dx_{i}", dz, wl["mlp_w1"], ident, [F32])[0]
        if i % 2 == 0:
            dh, dh16, small[f"norm_mlp_{i}"], small[f"cf_b_pw2_{j}"] = _rms_bwd(
                f"rms_mlp_bwd_{i}", s["h1"], vec(norm_mlp[i]), du2, dh, want_colsum=True)
            d_mix_out = _mm_tn(f"cf_pw2_dw_{i}", s["v4"], dh16, 1)
            dv4 = _mm_nt(f"cf_pw2_dx_{i}", dh16, wl["cf_w_pw2"], ident, [F32])[0]
            dv2, small[f"cf_norm_{j}"], small[f"cf_b_dw_{j}"] = _cf_norm_bwd(
                f"cf_norm_bwd_{i}", s["v2"], vec(cf_norm[j]), dv4)
            da, small[f"cf_w_dw_{j}"], db1 = _cf_conv_bwd(f"cf_conv_bwd_{i}", dv2, s["a"], w_dw_full[j])
            small[f"cf_b_pw1_{j}"] = db1.reshape(2, D)
            d_mix_in = _mm_tn(f"cf_pw1_dw_{i}", s["u"], da, N_SHARDS)
            du = _mm_nt(f"cf_pw1_dx_{i}", da, wl["cf_w_pw1"], ident, [F32])[0]
        else:
            dh, dh16, small[f"norm_mlp_{i}"] = _rms_bwd(f"rms_mlp_bwd_{i}", s["h1"], vec(norm_mlp[i]), du2, dh)
            d_mix_out = _mm_tn(f"sc_out_dw_{i}", s["y"], dh16, 1)
            dy = _mm_nt(f"sc_out_dx_{i}", dh16, wl["sc_w_out"], ident, [F32])[0]
            da, small[f"sc_w_conv_{j}"] = _sc_conv_bwd(f"sc_conv_bwd_{i}", dy, s["bcv"], w_conv_full[j])
            d_mix_in = _mm_tn(f"sc_in_dw_{i}", s["u"], da, N_SHARDS)
            du = _mm_nt(f"sc_in_dx_{i}", da, wl["sc_w_in"], ident, [F32])[0]
        dh, dh16, small[f"norm_mix_{i}"] = _rms_bwd(f"rms_mix_bwd_{i}", s["h"], vec(norm_mix[i]), du, dh)

        names = layer_names(i)
        parts = [d_mix_in, d_mix_out, d_w1, d_w2, d_proj, d_gate]
        parts = [pt.reshape(N_SHARDS, pt.shape[1] // N_SHARDS, pt.shape[2]) if nm in row_sharded else pt
                 for nm, pt in zip(names, parts)]
        for nm, gsum in zip(names, _reduce_layer(f"{i}", parts, core, core_chip)):
            grads_big[nm][layer_index(i, nm)] = gsum
    grad_x = dh.reshape(x.shape)

    order = sorted(small)
    pieces, where, row = [], {}, 0
    for nm in order:
        pc = _pad_rows(small[nm])
        where[nm] = (row, small[nm].shape[0])
        row += pc.shape[0]
        pieces.append(pc)
    total, loss_tile = _sum_over_devices("small_allsum", jnp.concatenate(pieces, axis=0), where["loss"][0])
    loss = loss_tile[0, 0]

    def small_sum(nm):
        r0, nr = where[nm]
        return total[r0:r0 + nr]

    def my_cols(a):
        return lax.dynamic_slice_in_dim(a, chip * (D // N_SHARDS), D // N_SHARDS, axis=1)

    g_small = {
        "norm_mix": jnp.concatenate([small_sum(f"norm_mix_{i}") for i in range(DEPTH)], axis=0),
        "norm_mlp": jnp.concatenate([small_sum(f"norm_mlp_{i}") for i in range(DEPTH)], axis=0),
        "norm_ple": jnp.concatenate([small_sum(f"norm_ple_{i}") for i in range(DEPTH)], axis=0),
        "cf_b_pw1": jnp.stack([small_sum(f"cf_b_pw1_{j}").reshape(2 * D) for j in range(DEPTH // 2)]),
        "cf_w_dw": jnp.stack([my_cols(small_sum(f"cf_w_dw_{j}")) for j in range(DEPTH // 2)]),
        "cf_b_dw": jnp.concatenate([small_sum(f"cf_b_dw_{j}") for j in range(DEPTH // 2)], axis=0),
        "cf_norm": jnp.concatenate([small_sum(f"cf_norm_{j}") for j in range(DEPTH // 2)], axis=0),
        "cf_b_pw2": jnp.concatenate([small_sum(f"cf_b_pw2_{j}") for j in range(DEPTH // 2)], axis=0),
        "sc_w_conv": jnp.stack([my_cols(small_sum(f"sc_w_conv_{j}")) for j in range(DEPTH // 2)]),
        "norm_final": small_sum("norm_final").reshape(D),
    }

    params = dict(norm_mix=norm_mix, norm_mlp=norm_mlp, norm_ple=norm_ple, cf_w_pw1=cf_w_pw1, cf_b_pw1=cf_b_pw1,
                  cf_w_dw=cf_w_dw, cf_b_dw=cf_b_dw, cf_norm=cf_norm, cf_w_pw2=cf_w_pw2, cf_b_pw2=cf_b_pw2,
                  sc_w_in=sc_w_in, sc_w_conv=sc_w_conv, sc_w_out=sc_w_out, mlp_w1=mlp_w1, mlp_w2=mlp_w2,
                  ple_w_proj=ple_w_proj, ple_w_gate=ple_w_gate, norm_final=norm_final)
    mom1 = dict(norm_mix=m_norm_mix, norm_mlp=m_norm_mlp, norm_ple=m_norm_ple, cf_w_pw1=m_cf_w_pw1,
                cf_b_pw1=m_cf_b_pw1, cf_w_dw=m_cf_w_dw, cf_b_dw=m_cf_b_dw, cf_norm=m_cf_norm, cf_w_pw2=m_cf_w_pw2,
                cf_b_pw2=m_cf_b_pw2, sc_w_in=m_sc_w_in, sc_w_conv=m_sc_w_conv, sc_w_out=m_sc_w_out,
                mlp_w1=m_mlp_w1, mlp_w2=m_mlp_w2, ple_w_proj=m_ple_w_proj, ple_w_gate=m_ple_w_gate,
                norm_final=m_norm_final)
    mom2 = dict(norm_mix=v_norm_mix, norm_mlp=v_norm_mlp, norm_ple=v_norm_ple, cf_w_pw1=v_cf_w_pw1,
                cf_b_pw1=v_cf_b_pw1, cf_w_dw=v_cf_w_dw, cf_b_dw=v_cf_b_dw, cf_norm=v_cf_norm, cf_w_pw2=v_cf_w_pw2,
                cf_b_pw2=v_cf_b_pw2, sc_w_in=v_sc_w_in, sc_w_conv=v_sc_w_conv, sc_w_out=v_sc_w_out,
                mlp_w1=v_mlp_w1, mlp_w2=v_mlp_w2, ple_w_proj=v_ple_w_proj, ple_w_gate=v_ple_w_gate,
                norm_final=v_norm_final)
    names_out = ["norm_mix", "norm_mlp", "norm_ple", "cf_w_pw1", "cf_b_pw1", "cf_w_dw", "cf_b_dw", "cf_norm",
                 "cf_w_pw2", "cf_b_pw2", "sc_w_in", "sc_w_conv", "sc_w_out", "mlp_w1", "mlp_w2", "ple_w_proj",
                 "ple_w_gate", "norm_final"]
    grad, delta, new_m, new_v = {}, {}, {}, {}
    for nm in names_out:
        w = params[nm]
        if nm in big:
            res = None
            for l, g in enumerate(grads_big[nm]):
                res = _adamw_slab(f"adamw_{nm}_{l}", w, mom1[nm], mom2[nm], g, l, res)
            grad[nm], delta[nm], new_m[nm], new_v[nm] = res
            continue
        g = g_small[nm]
        cols = w.shape[-1] if w.ndim > 1 else w.shape[0]
        two_d = lambda a: a.reshape(-1, cols)
        res = _adamw(f"adamw_{nm}", two_d(w), two_d(mom1[nm]), two_d(mom2[nm]), two_d(g))
        grad[nm], delta[nm], new_m[nm], new_v[nm] = [r.reshape(w.shape) for r in res]

    return (loss, grad_x, *[grad[n] for n in names_out], *[delta[n] for n in names_out],
            *[new_m[n] for n in names_out], *[new_v[n] for n in names_out])
```

```python
import functools

import jax
import jax.numpy as jnp
from jax import lax
from jax.experimental import pallas as pl
from jax.experimental.pallas import tpu as pltpu

F32 = jnp.float32
BF16 = jnp.bfloat16

EPS = 1e-6
ADAM_LR = 0.001
ADAM_B1 = 0.9
ADAM_B2 = 0.999
ADAM_EPS = 1e-08
ADAM_WD = 0.01
ADAM_STEP = 10

DEPTH = 4
N_SHARDS = 4
N_DEVICES = 8
V7X_VMEM_LIMIT_BYTES = 56 * 1024 * 1024
SUBLANES = 8
MESH = pl.DeviceIdType.MESH

MM_TM = 1024
MM_TM_FUSED = 512
MM_TN = 1024
MM_TK = 2048
ADAMW_TILE_ELEMS = 256 * 2048

CONV_ROW_CHUNK = 32
CONV_LANE_CHUNK = 512
CONV_TILE_ROWS = 128
ROW_TILE = 256


def _tile(dim, pref):
    if dim <= pref:
        return dim
    t = pref
    while dim % t:
        t //= 2
    return t


def _cparams(*sem):
    return pltpu.CompilerParams(dimension_semantics=sem, vmem_limit_bytes=V7X_VMEM_LIMIT_BYTES)


def _sigmoid(x):
    return 1.0 / (1.0 + jnp.exp(-x))


def _rms_r(x):
    return lax.rsqrt(jnp.mean(x * x, axis=-1, keepdims=True) + EPS)


def _mm_nn(name, a, b3, epilogue, out_dtypes, extras=()):
    M, K = a.shape
    S, Kb, Ns = b3.shape
    assert Kb == K
    N = S * Ns
    fused = len(out_dtypes) > 1 or any(kind == "mn" for _, kind in extras)
    tm, tn, tk = _tile(M, MM_TM_FUSED if fused else MM_TM), _tile(Ns, MM_TN), _tile(K, MM_TK)
    per = Ns // tn
    nk = K // tk
    in_specs = [pl.BlockSpec((tm, tk), lambda i, j, k: (i, k)),
                pl.BlockSpec((None, tk, tn), lambda i, j, k: (j // per, k, j % per))]
    for _, kind in extras:
        if kind == "mn":
            in_specs.append(pl.BlockSpec((tm, tn), lambda i, j, k: (i, j)))
        else:
            in_specs.append(pl.BlockSpec((1, tn), lambda i, j, k: (0, j)))
    n_ex, n_o = len(extras), len(out_dtypes)

    def body(*refs):
        a_ref, b_ref = refs[:2]
        ex = refs[2:2 + n_ex]
        outs = refs[2 + n_ex:2 + n_ex + n_o]
        part = jnp.dot(a_ref[...].astype(BF16), b_ref[...], preferred_element_type=F32)

        def finish(acc):
            res = epilogue(acc, *[e[...] for e in ex])
            for r, o in zip(res, outs):
                o[...] = r.astype(o.dtype)

        if nk == 1:
            finish(part)
        else:
            acc_ref = refs[-1]
            k = pl.program_id(2)

            @pl.when(k == 0)
            def _():
                acc_ref[...] = part

            @pl.when(k > 0)
            def _():
                acc_ref[...] += part

            @pl.when(k == nk - 1)
            def _():
                finish(acc_ref[...])

    res = pl.pallas_call(
        body, name=name, grid=(M // tm, N // tn, nk),
        in_specs=in_specs,
        out_specs=[pl.BlockSpec((tm, tn), lambda i, j, k: (i, j)) for _ in out_dtypes],
        out_shape=[jax.ShapeDtypeStruct((M, N), dt) for dt in out_dtypes],
        scratch_shapes=[pltpu.VMEM((tm, tn), F32)] if nk > 1 else [],
        compiler_params=_cparams("parallel", "parallel", "arbitrary"),
    )(a, b3, *[e for e, _ in extras])
    return res


def _mm_nt(name, g, w3, epilogue, out_dtypes, extras=()):
    M, N = g.shape
    S, K, Ns = w3.shape
    assert S * Ns == N
    tm, tn, tkk = _tile(M, MM_TM_FUSED), _tile(Ns, MM_TN), _tile(K, MM_TK)
    per = Ns // tn
    nn = N // tn
    n_ex, n_o = len(extras), len(out_dtypes)

    def body(*refs):
        g_ref, w_ref = refs[:2]
        ex = refs[2:2 + n_ex]
        outs = refs[2 + n_ex:2 + n_ex + n_o]
        part = lax.dot_general(g_ref[...].astype(BF16), w_ref[...], (((1,), (1,)), ((), ())),
                               preferred_element_type=F32)

        def finish(acc):
            res = epilogue(acc, *[e[...] for e in ex])
            for r, o in zip(res, outs):
                o[...] = r.astype(o.dtype)

        if nn == 1:
            finish(part)
        else:
            acc_ref = refs[-1]
            n = pl.program_id(2)

            @pl.when(n == 0)
            def _():
                acc_ref[...] = part

            @pl.when(n > 0)
            def _():
                acc_ref[...] += part

            @pl.when(n == nn - 1)
            def _():
                finish(acc_ref[...])

    return pl.pallas_call(
        body, name=name, grid=(M // tm, K // tkk, nn),
        in_specs=[pl.BlockSpec((tm, tn), lambda i, kk, n: (i, n)),
                  pl.BlockSpec((None, tkk, tn), lambda i, kk, n: (n // per, kk, n % per))]
        + [pl.BlockSpec((tm, tkk), lambda i, kk, n: (i, kk)) for _ in extras],
        out_specs=[pl.BlockSpec((tm, tkk), lambda i, kk, n: (i, kk)) for _ in out_dtypes],
        out_shape=[jax.ShapeDtypeStruct((M, K), dt) for dt in out_dtypes],
        scratch_shapes=[pltpu.VMEM((tm, tkk), F32)] if nn > 1 else [],
        compiler_params=_cparams("parallel", "parallel", "arbitrary"),
    )(g, w3, *extras)


def _mm_tn(name, a, g, n_shards):
    T, K = a.shape
    _, N = g.shape
    Ns = N // n_shards
    tk, tn, tt = _tile(K, MM_TK), _tile(Ns, MM_TN), _tile(T, MM_TM)
    per = Ns // tn
    nt = T // tt

    def body(a_ref, g_ref, o_ref, acc_ref):
        t = pl.program_id(2)
        part = lax.dot_general(a_ref[...].astype(BF16), g_ref[...].astype(BF16), (((0,), (0,)), ((), ())),
                               preferred_element_type=F32)

        @pl.when(t == 0)
        def _():
            acc_ref[...] = part

        @pl.when(t > 0)
        def _():
            acc_ref[...] += part

        @pl.when(t == nt - 1)
        def _():
            o_ref[...] = acc_ref[...].astype(o_ref.dtype)

    return pl.pallas_call(
        body, name=name, grid=(K // tk, N // tn, nt),
        in_specs=[pl.BlockSpec((tt, tk), lambda i, j, t: (t, i)),
                  pl.BlockSpec((tt, tn), lambda i, j, t: (t, j))],
        out_specs=pl.BlockSpec((None, tk, tn), lambda i, j, t: (j // per, i, j % per)),
        out_shape=jax.ShapeDtypeStruct((n_shards, K, Ns), BF16),
        scratch_shapes=[pltpu.VMEM((tk, tn), F32)],
        compiler_params=_cparams("parallel", "parallel", "arbitrary"),
    )(a, g)


def _rowwise(name, fn, ins, outs, accs=(), scratch=(), tt=ROW_TILE):
    T = next(a.shape[0] for a, kind in ins if kind == "row")
    tt = _tile(T, tt)
    n = T // tt
    in_specs = []
    for a, kind in ins:
        w = a.shape[1]
        if kind == "row":
            in_specs.append(pl.BlockSpec((tt, w), lambda i: (i, 0)))
        elif kind == "vec":
            in_specs.append(pl.BlockSpec(a.shape, lambda i: (0, 0)))
        elif kind[0] == "prev":
            pad = kind[1]
            in_specs.append(pl.BlockSpec((pad, w), lambda i, q=tt // pad: (jnp.maximum(i * q - 1, 0), 0)))
        else:
            pad = kind[1]
            in_specs.append(pl.BlockSpec((pad, w), lambda i, q=tt // pad, last=T // pad - 1:
                                         (jnp.minimum((i + 1) * q, last), 0)))
    n_in, n_out, n_acc = len(ins), len(outs), len(accs)

    def body(*refs):
        i = pl.program_id(0)
        in_refs = refs[:n_in]
        out_refs = refs[n_in:n_in + n_out]
        acc_refs = refs[n_in + n_out:n_in + n_out + n_acc]
        scr = refs[n_in + n_out + n_acc:]
        if n_acc:
            @pl.when(i == 0)
            def _():
                for r in acc_refs:
                    r[...] = jnp.zeros_like(r)
        fn(i, n, in_refs, out_refs, acc_refs, scr)

    res = pl.pallas_call(
        body, name=name, grid=(n,),
        in_specs=in_specs,
        out_specs=[pl.BlockSpec((tt, w), lambda i: (i, 0)) for w, _ in outs]
        + [pl.BlockSpec((r, w), lambda i: (0, 0)) for r, w in accs],
        out_shape=[jax.ShapeDtypeStruct((T, w), dt) for w, dt in outs]
        + [jax.ShapeDtypeStruct((r, w), F32) for r, w in accs],
        scratch_shapes=list(scratch),
        compiler_params=_cparams("arbitrary"),
    )(*[a for a, _ in ins])
    return res


def _colsum(x):
    return jnp.sum(x, axis=0, keepdims=True)


def _rms_fwd(name, h, g):
    D = h.shape[1]

    def fn(i, n, ins, outs, accs, scr):
        x = ins[0][...]
        outs[0][...] = (x * _rms_r(x) * ins[1][...]).astype(BF16)

    return _rowwise(name, fn, [(h, "row"), (g, "vec")], [(D, BF16)])[0]


def _rms_bwd(name, h, g, du, dh_in, want_colsum=False):
    D = h.shape[1]

    def fn(i, n, ins, outs, accs, scr):
        x = ins[0][...]
        gg = ins[1][...]
        d = ins[2][...].astype(F32)
        r = _rms_r(x)
        xn = x * r
        t = d * gg
        dh = ins[3][...] + r * (t - xn * jnp.mean(t * xn, axis=-1, keepdims=True))
        outs[0][...] = dh
        outs[1][...] = dh.astype(BF16)
        accs[0][...] += _colsum(d * xn)
        if want_colsum:
            accs[1][...] += _colsum(dh)

    return _rowwise(name, fn, [(h, "row"), (g, "vec"), (du, "row"), (dh_in, "row")],
                    [(D, F32), (D, BF16)], accs=[(1, D)] * (2 if want_colsum else 1))


def _loss_bwd(name, h, g, tgt):
    D = h.shape[1]

    def fn(i, n, ins, outs, accs, scr):
        x = ins[0][...]
        gg = ins[1][...]
        r = _rms_r(x)
        xn = x * r
        err = xn * gg - ins[2][...]
        dy = err / D
        t = dy * gg
        dh = r * (t - xn * jnp.mean(t * xn, axis=-1, keepdims=True))
        outs[0][...] = dh
        outs[1][...] = dh.astype(BF16)
        accs[0][...] += _colsum(dy * xn)
        accs[1][...] += _colsum(err * err)

    return _rowwise(name, fn, [(h, "row"), (g, "vec"), (tgt, "row")], [(D, F32), (D, BF16)],
                    accs=[(1, D), (1, D)])


def _ple_elem_bwd(name, dh, q, e):
    D = dh.shape[1]

    def fn(i, n, ins, outs, accs, scr):
        d = ins[0][...]
        s = _sigmoid(ins[1][...].astype(F32))
        ee = ins[2][...].astype(F32)
        outs[0][...] = (d * ee * s * (1.0 - s)).astype(BF16)
        outs[1][...] = (d * s).astype(BF16)

    return _rowwise(name, fn, [(dh, "row"), (q, "row"), (e, "row")], [(D, BF16), (D, BF16)])


def _cf_norm_bwd(name, v2, g, dv4):
    D = v2.shape[1]

    def fn(i, n, ins, outs, accs, scr):
        x = ins[0][...]
        gg = ins[1][...]
        r = _rms_r(x)
        xn = x * r
        v3 = xn * gg
        s = _sigmoid(v3)
        dv3 = ins[2][...].astype(F32) * (s * (1.0 + v3 * (1.0 - s)))
        t = dv3 * gg
        dv2 = r * (t - xn * jnp.mean(t * xn, axis=-1, keepdims=True))
        outs[0][...] = dv2
        accs[0][...] += _colsum(dv3 * xn)
        accs[1][...] += _colsum(dv2)

    return _rowwise(name, fn, [(v2, "row"), (g, "vec"), (dv4, "row")], [(D, F32)], accs=[(1, D), (1, D)])


def _chunks(tt, width):
    cc = min(CONV_LANE_CHUNK, width)
    rc = min(CONV_ROW_CHUNK, tt)
    for c0 in range(0, width, cc):
        for r0 in range(0, tt, rc):
            yield r0, rc, c0, cc


def _fir(win_ref, w_ref, n_taps, base, sign, tt, width, emit):
    for r0, rc, c0, cc in _chunks(tt, width):
        acc = jnp.zeros((rc, cc), F32)
        for k in range(n_taps):
            row = base + r0 + sign * (n_taps - 1 - k)
            acc = acc + w_ref[k:k + 1, c0:c0 + cc] * win_ref[row:row + rc, c0:c0 + cc]
        emit(r0, rc, c0, cc, acc)


def _fir_wgrad(d_ref, win_ref, dw8_ref, n_taps, pad, tt, width):
    for c0 in range(0, width, min(CONV_LANE_CHUNK, width)):
        cc = min(CONV_LANE_CHUNK, width)
        rc = min(CONV_ROW_CHUNK, tt)
        for k in range(n_taps):
            s = n_taps - 1 - k
            acc = jnp.zeros((SUBLANES, cc), F32)
            for r0 in range(0, tt, rc):
                prod = d_ref[r0:r0 + rc, c0:c0 + cc] * win_ref[pad + r0 - s:pad + r0 - s + rc, c0:c0 + cc]
                for q in range(0, rc, SUBLANES):
                    acc = acc + prod[q:q + SUBLANES]
            dw8_ref[SUBLANES * k:SUBLANES * (k + 1), c0:c0 + cc] += acc


def _glu(blk, D):
    return blk[:, :D].astype(F32) * _sigmoid(blk[:, D:].astype(F32))


CF_PAD = 32
SC_PAD = 16


def _cf_conv_fwd(name, a, w_dw, b_dw, g_cf):
    T, D2 = a.shape
    D = D2 // 2
    K = w_dw.shape[0]
    tt = _tile(T, CONV_TILE_ROWS)

    def fn(i, n, ins, outs, accs, scr):
        a_ref, prev_ref, w_ref, b_ref, g_ref = ins
        win_ref, v2_ref = scr
        win_ref[0:CF_PAD, :] = jnp.where(i > 0, _glu(prev_ref[...], D), 0.0)
        win_ref[CF_PAD:CF_PAD + tt, :] = _glu(a_ref[...], D)

        def emit(r0, rc, c0, cc, acc):
            v2_ref[r0:r0 + rc, c0:c0 + cc] = acc + b_ref[:, c0:c0 + cc]

        _fir(win_ref, w_ref, K, CF_PAD, -1, tt, D, emit)
        v2 = v2_ref[...]
        v3 = v2 * _rms_r(v2) * g_ref[...]
        outs[0][...] = v2
        outs[1][...] = (v3 * _sigmoid(v3)).astype(BF16)

    return _rowwise(name, fn, [(a, "row"), (a, ("prev", CF_PAD)), (w_dw, "vec"), (b_dw, "vec"), (g_cf, "vec")],
                    [(D, F32), (D, BF16)],
                    scratch=[pltpu.VMEM((CF_PAD + tt, D), F32), pltpu.VMEM((tt, D), F32)], tt=tt)


def _cf_conv_bwd(name, dv2, a, w_dw):
    T, D2 = a.shape
    D = D2 // 2
    K = w_dw.shape[0]
    tt = _tile(T, CONV_TILE_ROWS)

    def fn(i, n, ins, outs, accs, scr):
        d_ref, dnext_ref, a_ref, prev_ref, w_ref = ins
        v1win_ref, dwin_ref, dv1_ref, dw8_ref = scr

        @pl.when(i == 0)
        def _():
            dw8_ref[...] = jnp.zeros_like(dw8_ref)

        v1win_ref[0:CF_PAD, :] = jnp.where(i > 0, _glu(prev_ref[...], D), 0.0)
        v1win_ref[CF_PAD:CF_PAD + tt, :] = _glu(a_ref[...], D)
        dwin_ref[0:tt, :] = d_ref[...]
        dwin_ref[tt:tt + CF_PAD, :] = jnp.where(i < n - 1, dnext_ref[...], 0.0)

        def emit(r0, rc, c0, cc, acc):
            dv1_ref[r0:r0 + rc, c0:c0 + cc] = acc

        _fir(dwin_ref, w_ref, K, 0, 1, tt, D, emit)
        _fir_wgrad(d_ref, v1win_ref, dw8_ref, K, CF_PAD, tt, D)

        blk = a_ref[...]
        val = blk[:, :D].astype(F32)
        sg = _sigmoid(blk[:, D:].astype(F32))
        dv1 = dv1_ref[...]
        dval = dv1 * sg
        dgate = dv1 * val * sg * (1.0 - sg)
        outs[0][:, :D] = dval.astype(BF16)
        outs[0][:, D:] = dgate.astype(BF16)
        accs[1][:, :D] += _colsum(dval)
        accs[1][:, D:] += _colsum(dgate)

        @pl.when(i == n - 1)
        def _():
            for k in range(K):
                accs[0][k:k + 1, :] = _colsum(dw8_ref[SUBLANES * k:SUBLANES * (k + 1), :])

    return _rowwise(name, fn, [(dv2, "row"), (dv2, ("next", CF_PAD)), (a, "row"), (a, ("prev", CF_PAD)),
                               (w_dw, "vec")],
                    [(D2, BF16)], accs=[(K, D), (1, D2)],
                    scratch=[pltpu.VMEM((CF_PAD + tt, D), F32), pltpu.VMEM((tt + CF_PAD, D), F32),
                             pltpu.VMEM((tt, D), F32), pltpu.VMEM((SUBLANES * K, D), F32)], tt=tt)


def _sc_conv_fwd(name, bcv, w_conv):
    T, D3 = bcv.shape
    D = D3 // 3
    K = w_conv.shape[0]
    tt = _tile(T, CONV_TILE_ROWS)

    def cv_of(blk):
        return blk[:, D:2 * D].astype(F32) * blk[:, 2 * D:].astype(F32)

    def fn(i, n, ins, outs, accs, scr):
        x_ref, prev_ref, w_ref = ins
        win_ref, cc_ref = scr
        win_ref[0:SC_PAD, :] = jnp.where(i > 0, cv_of(prev_ref[...]), 0.0)
        win_ref[SC_PAD:SC_PAD + tt, :] = cv_of(x_ref[...])

        def emit(r0, rc, c0, cw, acc):
            cc_ref[r0:r0 + rc, c0:c0 + cw] = acc

        _fir(win_ref, w_ref, K, SC_PAD, -1, tt, D, emit)
        outs[0][...] = (x_ref[:, :D].astype(F32) * cc_ref[...]).astype(BF16)

    return _rowwise(name, fn, [(bcv, "row"), (bcv, ("prev", SC_PAD)), (w_conv, "vec")], [(D, BF16)],
                    scratch=[pltpu.VMEM((SC_PAD + tt, D), F32), pltpu.VMEM((tt, D), F32)], tt=tt)


def _sc_conv_bwd(name, dy, bcv, w_conv):
    T, D3 = bcv.shape
    D = D3 // 3
    K = w_conv.shape[0]
    tt = _tile(T, CONV_TILE_ROWS)

    def cv_of(blk):
        return blk[:, D:2 * D].astype(F32) * blk[:, 2 * D:].astype(F32)

    def fn(i, n, ins, outs, accs, scr):
        dy_ref, dynext_ref, x_ref, prev_ref, next_ref, w_ref = ins
        cvwin_ref, dccwin_ref, tmp_ref, dw8_ref = scr

        @pl.when(i == 0)
        def _():
            dw8_ref[...] = jnp.zeros_like(dw8_ref)

        cvwin_ref[0:SC_PAD, :] = jnp.where(i > 0, cv_of(prev_ref[...]), 0.0)
        cvwin_ref[SC_PAD:SC_PAD + tt, :] = cv_of(x_ref[...])

        def emit_cc(r0, rc, c0, cw, acc):
            tmp_ref[r0:r0 + rc, c0:c0 + cw] = acc

        _fir(cvwin_ref, w_ref, K, SC_PAD, -1, tt, D, emit_cc)
        dy_v = dy_ref[...].astype(F32)
        outs[0][:, :D] = (dy_v * tmp_ref[...]).astype(BF16)
        dccwin_ref[0:tt, :] = dy_v * x_ref[:, :D].astype(F32)
        dccwin_ref[tt:tt + SC_PAD, :] = jnp.where(
            i < n - 1, dynext_ref[...].astype(F32) * next_ref[:, :D].astype(F32), 0.0)

        def emit_dcv(r0, rc, c0, cw, acc):
            tmp_ref[r0:r0 + rc, c0:c0 + cw] = acc

        _fir(dccwin_ref, w_ref, K, 0, 1, tt, D, emit_dcv)
        _fir_wgrad(dccwin_ref, cvwin_ref, dw8_ref, K, SC_PAD, tt, D)
        dcv = tmp_ref[...]
        outs[0][:, D:2 * D] = (dcv * x_ref[:, 2 * D:].astype(F32)).astype(BF16)
        outs[0][:, 2 * D:] = (dcv * x_ref[:, D:2 * D].astype(F32)).astype(BF16)

        @pl.when(i == n - 1)
        def _():
            for k in range(K):
                accs[0][k:k + 1, :] = _colsum(dw8_ref[SUBLANES * k:SUBLANES * (k + 1), :])

    return _rowwise(name, fn, [(dy, "row"), (dy, ("next", SC_PAD)), (bcv, "row"), (bcv, ("prev", SC_PAD)),
                               (bcv, ("next", SC_PAD)), (w_conv, "vec")],
                    [(D3, BF16)], accs=[(K, D)],
                    scratch=[pltpu.VMEM((SC_PAD + tt, D), F32), pltpu.VMEM((tt + SC_PAD, D), F32),
                             pltpu.VMEM((tt, D), F32), pltpu.VMEM((SUBLANES * K, D), F32)], tt=tt)


def _place():
    x, y, c = lax.axis_index("x"), lax.axis_index("y"), lax.axis_index("c")
    chips = [(1 - x, y), (x, 1 - y), (1 - x, 1 - y)]
    return x, y, c, 2 * x + y, chips, (x, y, 1 - c)


def _half(rows, which):
    return pl.ds(pl.multiple_of(which * (rows // 2), SUBLANES), rows // 2)


_HBM = pl.BlockSpec(memory_space=pl.ANY)


def _gather_shards(name, items):
    n = len(items)
    shapes = [a.shape[-2:] for a, _ in items]

    def body(*refs):
        srcs, outs = refs[:n], refs[n:2 * n]
        send1, recv1, send2, recv2, lsem = refs[2 * n:]
        x, y, c, k, chips, sib = _place()

        def shard(i):
            return srcs[i] if items[i][1] is None else srcs[i].at[items[i][1]]

        started, locs = [], []
        for i in range(n):
            rows = shapes[i][0]
            lc = pltpu.make_async_copy(shard(i), outs[i].at[k], lsem.at[i])
            lc.start()
            locs.append(lc)
            for j, (cx, cy) in enumerate(chips):
                cp = pltpu.make_async_remote_copy(
                    src_ref=shard(i).at[_half(rows, c)], dst_ref=outs[i].at[k, _half(rows, c)],
                    send_sem=send1.at[i, j], recv_sem=recv1.at[i, j], device_id=(cx, cy, c), device_id_type=MESH)
                cp.start()
                started.append(cp)
        for i in range(n):
            rows = shapes[i][0]
            for j, (cx, cy) in enumerate(chips):
                blk = outs[i].at[2 * cx + cy, _half(rows, c)]
                pltpu.make_async_remote_copy(
                    src_ref=blk, dst_ref=blk, send_sem=send1.at[i, j], recv_sem=recv1.at[i, j],
                    device_id=(cx, cy, c), device_id_type=MESH).wait_recv()
                fw = pltpu.make_async_remote_copy(
                    src_ref=blk, dst_ref=blk, send_sem=send2.at[i, j], recv_sem=recv2.at[i, j],
                    device_id=sib, device_id_type=MESH)
                fw.start()
                started.append(fw)
        for i in range(n):
            rows = shapes[i][0]
            for j, (cx, cy) in enumerate(chips):
                blk = outs[i].at[2 * cx + cy, _half(rows, 1 - c)]
                pltpu.make_async_remote_copy(
                    src_ref=blk, dst_ref=blk, send_sem=send2.at[i, j], recv_sem=recv2.at[i, j],
                    device_id=sib, device_id_type=MESH).wait_recv()
        for cp in started:
            cp.wait_send()
        for lc in locs:
            lc.wait()

    return pl.pallas_call(
        body, name=name,
        in_specs=[_HBM] * n, out_specs=[_HBM] * n,
        out_shape=[jax.ShapeDtypeStruct((N_SHARDS,) + tuple(s), a.dtype) for s, (a, _) in zip(shapes, items)],
        scratch_shapes=[pltpu.SemaphoreType.DMA((n, 3))] * 4 + [pltpu.SemaphoreType.DMA((n,))],
    )(*[a for a, _ in items])


def _cast_place(name, w, layer, pos):
    _, R, C = w.shape
    tr = _tile(R, 256)

    def body(x_ref, y_ref, c_ref, w_ref, o_ref):
        o_ref[...] = w_ref[...].astype(BF16)

    return pl.pallas_call(
        body, name=name,
        grid_spec=pltpu.PrefetchScalarGridSpec(
            num_scalar_prefetch=3, grid=(R // tr,),
            in_specs=[pl.BlockSpec((None, tr, C), lambda r, xr, yr, cr: (layer, r, 0))],
            out_specs=pl.BlockSpec((None, tr, C), lambda r, xr, yr, cr: (2 * xr[0] + yr[0], r, 0))),
        out_shape=jax.ShapeDtypeStruct((N_SHARDS, R, C), BF16),
        compiler_params=_cparams("parallel"),
    )(*pos, w)


_IN_HBM = pl.BlockSpec(memory_space=pltpu.HBM)
_SEM = pl.BlockSpec(memory_space=pltpu.SEMAPHORE)
_SPLIT_COPY_PARAMS = pltpu.CompilerParams(has_side_effects=pltpu.SideEffectType.DATAFLOW_SIDE_EFFECTING)


def _in_hbm(a):
    return pltpu.with_memory_space_constraint(a, pltpu.HBM)


def _gather_copy(ref, i, j, chip_xy, c, k_src, rows, send, recv):
    blk = ref.at[k_src, _half(rows, c)]
    return pltpu.make_async_remote_copy(
        src_ref=blk, dst_ref=blk, send_sem=send.at[3 * i + j], recv_sem=recv.at[3 * i + j],
        device_id=(*chip_xy, c), device_id_type=MESH)


def _gather_start(name, bufs, after):
    n = len(bufs)

    def body(*refs):
        ins = refs[:n]
        send, recv = refs[n + 1], refs[n + 2]
        token = refs[-1]
        x, y, c, k, chips, sib = _place()
        for i in range(n):
            for j, chip_xy in enumerate(chips):
                _gather_copy(ins[i], i, j, chip_xy, c, k, bufs[i].shape[1], send, recv).start()
        token[...] = jnp.zeros_like(token)

    res = pl.pallas_call(
        body, name=name,
        in_specs=[_IN_HBM] * n + [_HBM],
        out_specs=[_SEM, _SEM] + [_IN_HBM] * n + [pl.BlockSpec(memory_space=pltpu.VMEM)],
        out_shape=[pltpu.SemaphoreType.DMA((3 * n,)), pltpu.SemaphoreType.DMA((3 * n,))]
        + [pltpu.HBM(b.shape, b.dtype) for b in bufs] + [jax.ShapeDtypeStruct((SUBLANES, 128), F32)],
        input_output_aliases={i: 2 + i for i in range(n)},
        compiler_params=_SPLIT_COPY_PARAMS,
    )(*[_in_hbm(b) for b in bufs], after)
    return res[0], res[1], list(res[2:2 + n]), res[-1]


def _gather_wait(name, bufs, send, recv, after):
    n = len(bufs)

    def body(*refs):
        ins = refs[:n]
        send_ref, recv_ref = refs[n], refs[n + 1]
        x, y, c, k, chips, sib = _place()
        for i in range(n):
            for j, chip_xy in enumerate(chips):
                rows = bufs[i].shape[1]
                _gather_copy(ins[i], i, j, chip_xy, c, k, rows, send_ref, recv_ref).wait_send()
                _gather_copy(ins[i], i, j, chip_xy, c, 2 * chip_xy[0] + chip_xy[1], rows, send_ref, recv_ref).wait_recv()

    return pl.pallas_call(
        body, name=name,
        in_specs=[_IN_HBM] * n + [_SEM, _SEM, _HBM],
        out_specs=[_IN_HBM] * n,
        out_shape=[pltpu.HBM(b.shape, b.dtype) for b in bufs],
        input_output_aliases={i: i for i in range(n)},
        compiler_params=_SPLIT_COPY_PARAMS,
    )(*bufs, send, recv, after)


def _gather_forward(name, bufs):
    n = len(bufs)

    def body(*refs):
        outs = refs[n:2 * n]
        send, recv = refs[2 * n:]
        x, y, c, k, chips, sib = _place()
        started = []
        for i in range(n):
            rows = bufs[i].shape[1]
            for j, (cx, cy) in enumerate(chips):
                blk = outs[i].at[2 * cx + cy, _half(rows, c)]
                fw = pltpu.make_async_remote_copy(
                    src_ref=blk, dst_ref=blk, send_sem=send.at[i, j], recv_sem=recv.at[i, j],
                    device_id=sib, device_id_type=MESH)
                fw.start()
                started.append(fw)
        for i in range(n):
            rows = bufs[i].shape[1]
            for j, (cx, cy) in enumerate(chips):
                blk = outs[i].at[2 * cx + cy, _half(rows, 1 - c)]
                pltpu.make_async_remote_copy(
                    src_ref=blk, dst_ref=blk, send_sem=send.at[i, j], recv_sem=recv.at[i, j],
                    device_id=sib, device_id_type=MESH).wait_recv()
        for cp in started:
            cp.wait_send()

    return pl.pallas_call(
        body, name=name, in_specs=[_HBM] * n, out_specs=[_HBM] * n,
        out_shape=[jax.ShapeDtypeStruct(b.shape, b.dtype) for b in bufs],
        input_output_aliases={i: i for i in range(n)},
        scratch_shapes=[pltpu.SemaphoreType.DMA((n, 3))] * 2,
    )(*bufs)


def _swap_halves(name, parts):
    n = len(parts)

    def body(*refs):
        srcs, outs = refs[:n], refs[n:2 * n]
        send, recv = refs[2 * n:]
        x, y, c, k, chips, sib = _place()
        cps = []
        for i in range(n):
            rows = parts[i].shape[1]
            cp = pltpu.make_async_remote_copy(
                src_ref=srcs[i].at[:, _half(rows, 1 - c)], dst_ref=outs[i],
                send_sem=send.at[i], recv_sem=recv.at[i], device_id=sib, device_id_type=MESH)
            cp.start()
            cps.append(cp)
        for cp in cps:
            cp.wait()

    return pl.pallas_call(
        body, name=name, in_specs=[_HBM] * n, out_specs=[_HBM] * n,
        out_shape=[jax.ShapeDtypeStruct((p.shape[0], p.shape[1] // 2, p.shape[2]), p.dtype) for p in parts],
        scratch_shapes=[pltpu.SemaphoreType.DMA((n,))] * 2,
    )(*parts)


def _scatter_copy(src_ref, land_ref, i, j, chip_xy, c, send, recv):
    return pltpu.make_async_remote_copy(
        src_ref=src_ref.at[2 * chip_xy[0] + chip_xy[1]], dst_ref=land_ref.at[j],
        send_sem=send.at[3 * i + j], recv_sem=recv.at[3 * i + j], device_id=(*chip_xy, c), device_id_type=MESH)


def _scatter_start(name, sums):
    n = len(sums)
    lands = [lax.empty((3,) + s.shape[1:], s.dtype) for s in sums]

    def body(*refs):
        srcs, lnds = refs[:n], refs[n:2 * n]
        send, recv = refs[2 * n], refs[2 * n + 1]
        token = refs[-1]
        x, y, c, k, chips, sib = _place()
        for i in range(n):
            for j, chip_xy in enumerate(chips):
                _scatter_copy(srcs[i], lnds[i], i, j, chip_xy, c, send, recv).start()
        token[...] = jnp.zeros_like(token)

    res = pl.pallas_call(
        body, name=name,
        in_specs=[_IN_HBM] * (2 * n),
        out_specs=[_SEM, _SEM] + [_IN_HBM] * (2 * n) + [pl.BlockSpec(memory_space=pltpu.VMEM)],
        out_shape=[pltpu.SemaphoreType.DMA((3 * n,)), pltpu.SemaphoreType.DMA((3 * n,))]
        + [pltpu.HBM(a.shape, a.dtype) for a in list(sums) + lands] + [jax.ShapeDtypeStruct((SUBLANES, 128), F32)],
        input_output_aliases={i: 2 + i for i in range(2 * n)},
        compiler_params=_SPLIT_COPY_PARAMS,
    )(*[_in_hbm(a) for a in list(sums) + lands])
    return res[0], res[1], list(res[2:2 + n]), list(res[2 + n:2 + 2 * n]), res[-1]


def _scatter_wait(name, sums, lands, send, recv, afters):
    n = len(sums)

    def body(*refs):
        srcs, lnds = refs[:n], refs[n:2 * n]
        send_ref, recv_ref = refs[2 * n], refs[2 * n + 1]
        x, y, c, k, chips, sib = _place()
        for i in range(n):
            for j, chip_xy in enumerate(chips):
                cp = _scatter_copy(srcs[i], lnds[i], i, j, chip_xy, c, send_ref, recv_ref)
                cp.wait_send()
                cp.wait_recv()

    res = pl.pallas_call(
        body, name=name,
        in_specs=[_IN_HBM] * (2 * n) + [_SEM, _SEM] + [_HBM] * len(afters),
        out_specs=[_IN_HBM] * (2 * n),
        out_shape=[pltpu.HBM(a.shape, a.dtype) for a in list(sums) + list(lands)],
        input_output_aliases={i: i for i in range(2 * n)},
        compiler_params=_SPLIT_COPY_PARAMS,
    )(*sums, *lands, send, recv, *afters)
    return list(res[:n]), list(res[n:])


def _join_halves(name, fulls):
    n = len(fulls)

    def body(*refs):
        outs = refs[n:2 * n]
        send, recv = refs[2 * n:]
        x, y, c, k, chips, sib = _place()
        cps = []
        for i in range(n):
            blk = outs[i].at[_half(fulls[i].shape[0], c)]
            cp = pltpu.make_async_remote_copy(
                src_ref=blk, dst_ref=blk, send_sem=send.at[i], recv_sem=recv.at[i],
                device_id=sib, device_id_type=MESH)
            cp.start()
            cps.append(cp)
        for i in range(n):
            blk = outs[i].at[_half(fulls[i].shape[0], 1 - c)]
            pltpu.make_async_remote_copy(
                src_ref=blk, dst_ref=blk, send_sem=send.at[i], recv_sem=recv.at[i],
                device_id=sib, device_id_type=MESH).wait_recv()
        for cp in cps:
            cp.wait_send()

    return pl.pallas_call(
        body, name=name, in_specs=[_HBM] * n, out_specs=[_HBM] * n,
        out_shape=[jax.ShapeDtypeStruct(f.shape, f.dtype) for f in fulls],
        input_output_aliases={i: i for i in range(n)},
        scratch_shapes=[pltpu.SemaphoreType.DMA((n,))] * 2,
    )(*fulls)


def _sum_over_devices(name, buf, loss_row):
    R, D = buf.shape

    def body(x_ref, all_ref, tot_ref, loss_ref, send_sems, recv_sems, local_sem):
        x, y, c, k, chips, sib = _place()
        me = (x, y, c)

        def block(px, py, pc):
            return all_ref.at[4 * px + 2 * py + pc]

        def copy(kk, blk, to, src=None):
            return pltpu.make_async_remote_copy(
                src_ref=block(*blk) if src is None else src, dst_ref=block(*blk),
                send_sem=send_sems.at[kk], recv_sem=recv_sems.at[kk], device_id=to, device_id_type=MESH)

        mine = pltpu.make_async_copy(x_ref, block(*me), local_sem)
        mine.start()
        first = [copy(0, me, sib, src=x_ref)]
        first += [copy(1 + j, me, (*chip, c), src=x_ref) for j, chip in enumerate(chips)]
        for cp in first:
            cp.start()
        passed = [copy(4 + j, (*chip, c), sib) for j, chip in enumerate(chips)]
        for j, chip in enumerate(chips):
            copy(1 + j, (*chip, c), me).wait_recv()
            passed[j].start()
        copy(0, sib, me).wait_recv()
        for j, chip in enumerate(chips):
            copy(4 + j, (*chip, 1 - c), me).wait_recv()
        for cp in first + passed:
            cp.wait_send()
        mine.wait()
        rc = _tile(R, 32)
        for r0 in range(0, R, rc):
            tot = all_ref[0, r0:r0 + rc, :]
            for d in range(1, N_DEVICES):
                tot = tot + all_ref[d, r0:r0 + rc, :]
            tot_ref[r0:r0 + rc, :] = tot
        loss = 0.5 * jnp.sum(tot_ref[loss_row:loss_row + 1, :]) / D
        loss_ref[...] = jnp.full(loss_ref.shape, loss, F32)

    vm = pl.BlockSpec(memory_space=pltpu.VMEM)
    return pl.pallas_call(
        body, name=name, in_specs=[vm], out_specs=[vm, vm, vm],
        out_shape=[jax.ShapeDtypeStruct((N_DEVICES, R, D), F32), jax.ShapeDtypeStruct((R, D), F32),
                   jax.ShapeDtypeStruct((SUBLANES, 128), F32)],
        scratch_shapes=[pltpu.SemaphoreType.DMA((7,)), pltpu.SemaphoreType.DMA((7,)), pltpu.SemaphoreType.DMA],
        compiler_params=pltpu.CompilerParams(vmem_limit_bytes=V7X_VMEM_LIMIT_BYTES),
    )(buf)[1:]


def _add_my_half(name, part, got, pos):
    S, R, C = part.shape
    R2 = R // 2
    tr = _tile(R2, 512)
    q = R2 // tr

    def body(x_ref, y_ref, c_ref, p_ref, g_ref, o_ref):
        o_ref[...] = (p_ref[...].astype(F32) + g_ref[...].astype(F32)).astype(o_ref.dtype)

    return pl.pallas_call(
        body, name=name,
        grid_spec=pltpu.PrefetchScalarGridSpec(
            num_scalar_prefetch=3, grid=(S, q),
            in_specs=[pl.BlockSpec((None, tr, C), lambda s, r, xr, yr, cr: (s, cr[0] * q + r, 0)),
                      pl.BlockSpec((None, tr, C), lambda s, r, xr, yr, cr: (s, r, 0))],
            out_specs=pl.BlockSpec((None, tr, C), lambda s, r, xr, yr, cr: (s, r, 0))),
        out_shape=jax.ShapeDtypeStruct((S, R2, C), BF16),
        compiler_params=_cparams("parallel", "parallel"),
    )(*pos, part, got)


def _add_owner(name, sums, got, pos):
    _, R2, C = sums.shape
    tr = _tile(R2, 512)
    q = R2 // tr

    def body(x_ref, y_ref, c_ref, s_ref, g_ref, o_ref):
        acc = s_ref[...].astype(F32)
        for j in range(3):
            acc = acc + g_ref[j].astype(F32)
        o_ref[...] = acc

    return pl.pallas_call(
        body, name=name,
        grid_spec=pltpu.PrefetchScalarGridSpec(
            num_scalar_prefetch=3, grid=(q,),
            in_specs=[pl.BlockSpec((None, tr, C), lambda r, xr, yr, cr: (2 * xr[0] + yr[0], r, 0)),
                      pl.BlockSpec((3, tr, C), lambda r, xr, yr, cr: (0, r, 0))],
            out_specs=pl.BlockSpec((tr, C), lambda r, xr, yr, cr: (cr[0] * q + r, 0))),
        out_shape=jax.ShapeDtypeStruct((2 * R2, C), F32),
        compiler_params=_cparams("parallel"),
    )(*pos, sums, got)


def _adamw(name, w, m, v, g):
    R, C = w.shape
    tr = SUBLANES
    while 2 * tr * C <= ADAMW_TILE_ELEMS:
        tr *= 2
    tr = _tile(R, tr)
    bc1 = 1.0 - ADAM_B1 ** ADAM_STEP
    bc2 = 1.0 - ADAM_B2 ** ADAM_STEP

    def body(w_ref, m_ref, v_ref, g_ref, go_ref, d_ref, mo_ref, vo_ref):
        gg = g_ref[...]
        m2 = ADAM_B1 * m_ref[...] + (1.0 - ADAM_B1) * gg
        v2 = ADAM_B2 * v_ref[...] + (1.0 - ADAM_B2) * (gg * gg)
        go_ref[...] = gg
        mo_ref[...] = m2
        vo_ref[...] = v2
        d_ref[...] = -ADAM_LR * ((m2 / bc1) / (jnp.sqrt(v2 / bc2) + ADAM_EPS) + ADAM_WD * w_ref[...])

    spec = pl.BlockSpec((tr, C), lambda r: (r, 0))
    return pl.pallas_call(
        body, name=name, grid=(R // tr,), in_specs=[spec] * 4, out_specs=[spec] * 4,
        out_shape=[jax.ShapeDtypeStruct((R, C), F32)] * 4,
        compiler_params=_cparams("parallel"),
    )(w, m, v, g)


def _adamw_slab(name, w, m, v, g, layer, prev):
    L, R, C = w.shape
    tr = SUBLANES
    while 2 * tr * C <= ADAMW_TILE_ELEMS:
        tr *= 2
    tr = _tile(R, tr)
    bc1 = 1.0 - ADAM_B1 ** ADAM_STEP
    bc2 = 1.0 - ADAM_B2 ** ADAM_STEP

    def body(w_ref, m_ref, v_ref, g_ref, *rest):
        go_ref, d_ref, mo_ref, vo_ref = rest[-4:]
        gg = g_ref[...]
        m2 = ADAM_B1 * m_ref[...] + (1.0 - ADAM_B1) * gg
        v2 = ADAM_B2 * v_ref[...] + (1.0 - ADAM_B2) * (gg * gg)
        go_ref[...] = gg
        mo_ref[...] = m2
        vo_ref[...] = v2
        d_ref[...] = -ADAM_LR * ((m2 / bc1) / (jnp.sqrt(v2 / bc2) + ADAM_EPS) + ADAM_WD * w_ref[...])

    slab = pl.BlockSpec((None, tr, C), lambda r: (layer, r, 0))
    n_prev = 0 if prev is None else 4
    return pl.pallas_call(
        body, name=name, grid=(R // tr,),
        in_specs=[slab] * 3 + [pl.BlockSpec((tr, C), lambda r: (r, 0))] + [_HBM] * n_prev,
        out_specs=[slab] * 4,
        out_shape=[jax.ShapeDtypeStruct((L, R, C), F32)] * 4,
        input_output_aliases={4 + i: i for i in range(n_prev)},
        compiler_params=_cparams("parallel"),
    )(w, m, v, g, *(prev or ()))


def _reduce_begin(tag, parts, pos):
    got = _swap_halves(f"rs_swap_{tag}", parts)
    sums = [_add_my_half(f"rs_add2_{tag}_{i}", p, g, pos) for i, (p, g) in enumerate(zip(parts, got))]
    return _scatter_start(f"rs_scatter_start_{tag}", sums)


def _reduce_end(tag, started, pos, afters):
    send, recv, sums, lands, _ = started
    sums, lands = _scatter_wait(f"rs_scatter_wait_{tag}", sums, lands, send, recv, afters)
    fulls = [_add_owner(f"rs_add4_{tag}_{i}", s, q, pos) for i, (s, q) in enumerate(zip(sums, lands))]
    return _join_halves(f"rs_join_{tag}", fulls)


def _pad_rows(a):
    r = (-a.shape[0]) % SUBLANES
    return jnp.pad(a, ((0, r), (0, 0))) if r else a


def kernel(x, p, norm_mix, norm_mlp, norm_ple, cf_w_pw1, cf_b_pw1, cf_w_dw, cf_b_dw, cf_norm, cf_w_pw2, cf_b_pw2, sc_w_in, sc_w_conv, sc_w_out, mlp_w1, mlp_w2, ple_w_proj, ple_w_gate, norm_final, loss_target, m_norm_mix, m_norm_mlp, m_norm_ple, m_cf_w_pw1, m_cf_b_pw1, m_cf_w_dw, m_cf_b_dw, m_cf_norm, m_cf_w_pw2, m_cf_b_pw2, m_sc_w_in, m_sc_w_conv, m_sc_w_out, m_mlp_w1, m_mlp_w2, m_ple_w_proj, m_ple_w_gate, m_norm_final, v_norm_mix, v_norm_mlp, v_norm_ple, v_cf_w_pw1, v_cf_b_pw1, v_cf_w_dw, v_cf_b_dw, v_cf_norm, v_cf_w_pw2, v_cf_b_pw2, v_sc_w_in, v_sc_w_conv, v_sc_w_out, v_mlp_w1, v_mlp_w2, v_ple_w_proj, v_ple_w_gate, v_norm_final):
    T, D = x.shape[1], x.shape[2]
    KA, KB = cf_w_dw.shape[1], sc_w_conv.shape[1]
    chip = (2 * lax.axis_index("x") + lax.axis_index("y")).astype(jnp.int32)
    pos = tuple(lax.axis_index(ax).astype(jnp.int32).reshape(1) for ax in ("x", "y", "c"))

    params = dict(norm_mix=norm_mix, norm_mlp=norm_mlp, norm_ple=norm_ple, cf_w_pw1=cf_w_pw1, cf_b_pw1=cf_b_pw1,
                  cf_w_dw=cf_w_dw, cf_b_dw=cf_b_dw, cf_norm=cf_norm, cf_w_pw2=cf_w_pw2, cf_b_pw2=cf_b_pw2,
                  sc_w_in=sc_w_in, sc_w_conv=sc_w_conv, sc_w_out=sc_w_out, mlp_w1=mlp_w1, mlp_w2=mlp_w2,
                  ple_w_proj=ple_w_proj, ple_w_gate=ple_w_gate, norm_final=norm_final)
    mom1 = dict(norm_mix=m_norm_mix, norm_mlp=m_norm_mlp, norm_ple=m_norm_ple, cf_w_pw1=m_cf_w_pw1,
                cf_b_pw1=m_cf_b_pw1, cf_w_dw=m_cf_w_dw, cf_b_dw=m_cf_b_dw, cf_norm=m_cf_norm, cf_w_pw2=m_cf_w_pw2,
                cf_b_pw2=m_cf_b_pw2, sc_w_in=m_sc_w_in, sc_w_conv=m_sc_w_conv, sc_w_out=m_sc_w_out,
                mlp_w1=m_mlp_w1, mlp_w2=m_mlp_w2, ple_w_proj=m_ple_w_proj, ple_w_gate=m_ple_w_gate,
                norm_final=m_norm_final)
    mom2 = dict(norm_mix=v_norm_mix, norm_mlp=v_norm_mlp, norm_ple=v_norm_ple, cf_w_pw1=v_cf_w_pw1,
                cf_b_pw1=v_cf_b_pw1, cf_w_dw=v_cf_w_dw, cf_b_dw=v_cf_b_dw, cf_norm=v_cf_norm, cf_w_pw2=v_cf_w_pw2,
                cf_b_pw2=v_cf_b_pw2, sc_w_in=v_sc_w_in, sc_w_conv=v_sc_w_conv, sc_w_out=v_sc_w_out,
                mlp_w1=v_mlp_w1, mlp_w2=v_mlp_w2, ple_w_proj=v_ple_w_proj, ple_w_gate=v_ple_w_gate,
                norm_final=v_norm_final)

    big = ("cf_w_pw1", "cf_w_pw2", "sc_w_in", "sc_w_out", "mlp_w1", "mlp_w2", "ple_w_proj", "ple_w_gate")
    row_sharded = ("cf_w_pw2", "sc_w_out", "mlp_w2", "ple_w_gate")

    def layer_names(i):
        return (["cf_w_pw1", "cf_w_pw2"] if i % 2 == 0 else ["sc_w_in", "sc_w_out"]) + \
            ["mlp_w1", "mlp_w2", "ple_w_proj", "ple_w_gate"]

    def layer_index(i, name):
        return i if name.startswith(("mlp", "ple")) else i // 2

    def gather_begin(i, after):
        bufs = [_cast_place(f"place_{nm}_{i}", params[nm], layer_index(i, nm), pos) for nm in layer_names(i)]
        return _gather_start(f"gather_start_{i}", bufs, after)

    def gather_end(i, started, after):
        send, recv, bufs, _ = started
        bufs = _gather_wait(f"gather_wait_{i}", bufs, send, recv, after)
        bufs = _gather_forward(f"gather_fwd_{i}", bufs)
        return {nm: g4.reshape(1, N_SHARDS * g4.shape[1], g4.shape[2]) if nm in row_sharded else g4
                for nm, g4 in zip(layer_names(i), bufs)}

    conv_small = jnp.concatenate([_pad_rows(cf_w_dw[j]) for j in range(cf_w_dw.shape[0])]
                                 + [_pad_rows(sc_w_conv[j]) for j in range(sc_w_conv.shape[0])], axis=0)
    conv_all = _gather_shards("gather_conv_w", [(conv_small, None)])[0]
    conv_all = jnp.transpose(conv_all, (1, 0, 2)).reshape(conv_small.shape[0], D)
    ka_pad = KA + (-KA) % SUBLANES
    kb_pad = KB + (-KB) % SUBLANES
    w_dw_full = [conv_all[j * ka_pad:j * ka_pad + KA] for j in range(cf_w_dw.shape[0])]
    off = cf_w_dw.shape[0] * ka_pad
    w_conv_full = [conv_all[off + j * kb_pad:off + j * kb_pad + KB] for j in range(sc_w_conv.shape[0])]

    def vec(a):
        return a.reshape(1, -1)

    ident = lambda acc: (acc,)

    h = x[0]
    saved = []
    W = [gather_end(0, gather_begin(0, h), h)]
    for i in range(DEPTH):
        j = i // 2
        wl = W[i]
        s = dict(h=h)
        g_mix = vec(norm_mix[i])
        if i + 1 < DEPTH:
            nxt = gather_begin(i + 1, wl["mlp_w1"])
            g_mix = g_mix + nxt[3][0, 0]
        s["u"] = _rms_fwd(f"rms_mix_{i}", h, g_mix)
        if i % 2 == 0:
            s["a"] = _mm_nn(f"cf_pw1_{i}", s["u"], wl["cf_w_pw1"], lambda acc, b: (acc + b,), [BF16],
                            extras=[(vec(cf_b_pw1[j]), "n")])[0]
            s["v2"], s["v4"] = _cf_conv_fwd(f"cf_conv_{i}", s["a"], w_dw_full[j], vec(cf_b_dw[j]), vec(cf_norm[j]))
            h1 = _mm_nn(f"cf_pw2_{i}", s["v4"], wl["cf_w_pw2"], lambda acc, b, r: (r + (acc + b),), [F32],
                        extras=[(vec(cf_b_pw2[j]), "n"), (h, "mn")])[0]
        else:
            s["bcv"] = _mm_nn(f"sc_in_{i}", s["u"], wl["sc_w_in"], ident, [BF16])[0]
            s["y"] = _sc_conv_fwd(f"sc_conv_{i}", s["bcv"], w_conv_full[j])[0]
            h1 = _mm_nn(f"sc_out_{i}", s["y"], wl["sc_w_out"], lambda acc, r: (r + acc,), [F32],
                        extras=[(h, "mn")])[0]
        s["h1"] = h1
        s["u2"] = _rms_fwd(f"rms_mlp_{i}", h1, vec(norm_mlp[i]))
        s["z"], s["hd"] = _mm_nn(f"mlp_w1_{i}", s["u2"], wl["mlp_w1"],
                                 lambda acc: (acc, jnp.square(jnp.maximum(acc, 0.0))), [BF16, BF16])
        h2 = _mm_nn(f"mlp_w2_{i}", s["hd"], wl["mlp_w2"], lambda acc, r: (r + acc,), [F32], extras=[(h1, "mn")])[0]
        s["h2"] = h2
        s["n3"] = _rms_fwd(f"rms_ple_{i}", h2, vec(norm_ple[i]))
        s["p"] = p[i, 0]
        s["e"] = _mm_nn(f"ple_proj_{i}", s["p"], wl["ple_w_proj"], ident, [BF16])[0]
        h, s["q"] = _mm_nn(f"ple_gate_{i}", s["n3"], wl["ple_w_gate"],
                           lambda acc, r, e: (r + _sigmoid(acc) * e.astype(F32), acc), [F32, BF16],
                           extras=[(h2, "mn"), (s["e"], "mn")])
        saved.append(s)
        if i + 1 < DEPTH:
            W.append(gather_end(i + 1, nxt, h))

    dh, dh16, dg_final, loss_cols = _loss_bwd("loss_bwd", h, vec(norm_final), loss_target[0])
    small = {"norm_final": dg_final, "loss": loss_cols}
    adam = {nm: None for nm in big}

    def finish_layer(i, started, afters):
        for nm, g in zip(layer_names(i), _reduce_end(f"{i}", started, pos, afters)):
            l = layer_index(i, nm)
            adam[nm] = _adamw_slab(f"adamw_{nm}_{l}", params[nm], mom1[nm], mom2[nm], g, l, adam[nm])

    pending = None
    for i in reversed(range(DEPTH)):
        j = i // 2
        wl, s = W[i], saved[i]
        dq, de = _ple_elem_bwd(f"ple_elem_bwd_{i}", dh, s["q"], s["e"])
        d_proj = _mm_tn(f"ple_proj_dw_{i}", s["p"], de, N_SHARDS)
        d_gate = _mm_tn(f"ple_gate_dw_{i}", s["n3"], dq, 1)
        dn3 = _mm_nt(f"ple_gate_dx_{i}", dq, wl["ple_w_gate"], ident, [F32])[0]
        dh, dh16, small[f"norm_ple_{i}"] = _rms_bwd(f"rms_ple_bwd_{i}", s["h2"], vec(norm_ple[i]), dn3, dh)

        d_w2 = _mm_tn(f"mlp_w2_dw_{i}", s["hd"], dh16, 1)
        dz = _mm_nt(f"mlp_w2_dx_{i}", dh16, wl["mlp_w2"],
                    lambda acc, z: (acc * (2.0 * jnp.maximum(z.astype(F32), 0.0)),), [BF16], extras=[s["z"]])[0]
        d_w1 = _mm_tn(f"mlp_w1_dw_{i}", s["u2"], dz, N_SHARDS)
        du2 = _mm_nt(f"mlp_w1_dx_{i}", dz, wl["mlp_w1"], ident, [F32])[0]
        if i % 2 == 0:
            dh, dh16, small[f"norm_mlp_{i}"], small[f"cf_b_pw2_{j}"] = _rms_bwd(
                f"rms_mlp_bwd_{i}", s["h1"], vec(norm_mlp[i]), du2, dh, want_colsum=True)
            d_mix_out = _mm_tn(f"cf_pw2_dw_{i}", s["v4"], dh16, 1)
            dv4 = _mm_nt(f"cf_pw2_dx_{i}", dh16, wl["cf_w_pw2"], ident, [F32])[0]
            dv2, small[f"cf_norm_{j}"], small[f"cf_b_dw_{j}"] = _cf_norm_bwd(
                f"cf_norm_bwd_{i}", s["v2"], vec(cf_norm[j]), dv4)
            da, small[f"cf_w_dw_{j}"], db1 = _cf_conv_bwd(f"cf_conv_bwd_{i}", dv2, s["a"], w_dw_full[j])
            small[f"cf_b_pw1_{j}"] = db1.reshape(2, D)
            d_mix_in = _mm_tn(f"cf_pw1_dw_{i}", s["u"], da, N_SHARDS)
            du = _mm_nt(f"cf_pw1_dx_{i}", da, wl["cf_w_pw1"], ident, [F32])[0]
        else:
            dh, dh16, small[f"norm_mlp_{i}"] = _rms_bwd(f"rms_mlp_bwd_{i}", s["h1"], vec(norm_mlp[i]), du2, dh)
            d_mix_out = _mm_tn(f"sc_out_dw_{i}", s["y"], dh16, 1)
            dy = _mm_nt(f"sc_out_dx_{i}", dh16, wl["sc_w_out"], ident, [F32])[0]
            da, small[f"sc_w_conv_{j}"] = _sc_conv_bwd(f"sc_conv_bwd_{i}", dy, s["bcv"], w_conv_full[j])
            d_mix_in = _mm_tn(f"sc_in_dw_{i}", s["u"], da, N_SHARDS)
            du = _mm_nt(f"sc_in_dx_{i}", da, wl["sc_w_in"], ident, [F32])[0]

        parts = [d_mix_in, d_mix_out, d_w1, d_w2, d_proj, d_gate]
        parts = [pt.reshape(N_SHARDS, pt.shape[1] // N_SHARDS, pt.shape[2]) if nm in row_sharded else pt
                 for nm, pt in zip(layer_names(i), parts)]
        started = _reduce_begin(f"{i}", parts, pos)
        token = started[4]
        dh, dh16, small[f"norm_mix_{i}"] = _rms_bwd(f"rms_mix_bwd_{i}", s["h"], vec(norm_mix[i]) + token[0, 0], du, dh)
        if pending is not None:
            finish_layer(i + 1, pending, [token])
        pending = started
    grad_x = dh.reshape(x.shape)

    order = sorted(small)
    pieces, where, row = [], {}, 0
    for nm in order:
        pc = _pad_rows(small[nm])
        where[nm] = (row, small[nm].shape[0])
        row += pc.shape[0]
        pieces.append(pc)
    total, loss_tile = _sum_over_devices("small_allsum", jnp.concatenate(pieces, axis=0), where["loss"][0])
    loss = loss_tile[0, 0]
    finish_layer(0, pending, [total] + [res[3] for res in adam.values() if res is not None])

    def small_sum(nm):
        r0, nr = where[nm]
        return total[r0:r0 + nr]

    def my_cols(a):
        return lax.dynamic_slice_in_dim(a, chip * (D // N_SHARDS), D // N_SHARDS, axis=1)

    g_small = {
        "norm_mix": jnp.concatenate([small_sum(f"norm_mix_{i}") for i in range(DEPTH)], axis=0),
        "norm_mlp": jnp.concatenate([small_sum(f"norm_mlp_{i}") for i in range(DEPTH)], axis=0),
        "norm_ple": jnp.concatenate([small_sum(f"norm_ple_{i}") for i in range(DEPTH)], axis=0),
        "cf_b_pw1": jnp.stack([small_sum(f"cf_b_pw1_{j}").reshape(2 * D) for j in range(DEPTH // 2)]),
        "cf_w_dw": jnp.stack([my_cols(small_sum(f"cf_w_dw_{j}")) for j in range(DEPTH // 2)]),
        "cf_b_dw": jnp.concatenate([small_sum(f"cf_b_dw_{j}") for j in range(DEPTH // 2)], axis=0),
        "cf_norm": jnp.concatenate([small_sum(f"cf_norm_{j}") for j in range(DEPTH // 2)], axis=0),
        "cf_b_pw2": jnp.concatenate([small_sum(f"cf_b_pw2_{j}") for j in range(DEPTH // 2)], axis=0),
        "sc_w_conv": jnp.stack([my_cols(small_sum(f"sc_w_conv_{j}")) for j in range(DEPTH // 2)]),
        "norm_final": small_sum("norm_final").reshape(D),
    }

    names_out = ["norm_mix", "norm_mlp", "norm_ple", "cf_w_pw1", "cf_b_pw1", "cf_w_dw", "cf_b_dw", "cf_norm",
                 "cf_w_pw2", "cf_b_pw2", "sc_w_in", "sc_w_conv", "sc_w_out", "mlp_w1", "mlp_w2", "ple_w_proj",
                 "ple_w_gate", "norm_final"]
    grad, delta, new_m, new_v = {}, {}, {}, {}
    for nm in names_out:
        w = params[nm]
        if nm in big:
            grad[nm], delta[nm], new_m[nm], new_v[nm] = adam[nm]
            continue
        g = g_small[nm]
        cols = w.shape[-1] if w.ndim > 1 else w.shape[0]
        two_d = lambda a: a.reshape(-1, cols)
        res = _adamw(f"adamw_{nm}", two_d(w), two_d(mom1[nm]), two_d(mom2[nm]), two_d(g))
        grad[nm], delta[nm], new_m[nm], new_v[nm] = [r.reshape(w.shape) for r in res]

    return (loss, grad_x, *[grad[n] for n in names_out], *[delta[n] for n in names_out],
            *[new_m[n] for n in names_out], *[new_v[n] for n in names_out])
```

```python
import functools

import jax
import jax.numpy as jnp
from jax import lax
from jax.experimental import pallas as pl
from jax.experimental.pallas import tpu as pltpu

F32 = jnp.float32
BF16 = jnp.bfloat16

EPS = 1e-6
ADAM_LR = 0.001
ADAM_B1 = 0.9
ADAM_B2 = 0.999
ADAM_EPS = 1e-08
ADAM_WD = 0.01
ADAM_STEP = 10

DEPTH = 4
N_SHARDS = 4
N_DEVICES = 8
V7X_VMEM_LIMIT_BYTES = 56 * 1024 * 1024
SUBLANES = 8
MESH = pl.DeviceIdType.MESH

MM_TM = 1024
MM_TM_FUSED = 512
MM_TN = 1024
MM_TK = 2048
ADAMW_TILE_ELEMS = 256 * 2048

CONV_ROW_CHUNK = 32
CONV_LANE_CHUNK = 512
CONV_TILE_ROWS = 128
ROW_TILE = 256


def _tile(dim, pref):
    if dim <= pref:
        return dim
    t = pref
    while dim % t:
        t //= 2
    return t


def _cparams(*sem):
    return pltpu.CompilerParams(dimension_semantics=sem, vmem_limit_bytes=V7X_VMEM_LIMIT_BYTES)


def _sigmoid(x):
    return 1.0 / (1.0 + jnp.exp(-x))


def _rms_r(x):
    return lax.rsqrt(jnp.mean(x * x, axis=-1, keepdims=True) + EPS)


def _mm_nn(name, a, b3, epilogue, out_dtypes, extras=()):
    M, K = a.shape
    S, Kb, Ns = b3.shape
    assert Kb == K
    N = S * Ns
    crowded = sum(kind == "mn" for _, kind in extras) > 1
    tm, tn, tk = _tile(M, MM_TM_FUSED if crowded else MM_TM), _tile(Ns, MM_TN), _tile(K, MM_TK)
    per = Ns // tn
    nk = K // tk
    in_specs = [pl.BlockSpec((tm, tk), lambda i, j, k: (i, k)),
                pl.BlockSpec((None, tk, tn), lambda i, j, k: (j // per, k, j % per))]
    for _, kind in extras:
        if kind == "mn":
            in_specs.append(pl.BlockSpec((tm, tn), lambda i, j, k: (i, j)))
        else:
            in_specs.append(pl.BlockSpec((1, tn), lambda i, j, k: (0, j)))
    n_ex, n_o = len(extras), len(out_dtypes)

    def body(*refs):
        a_ref, b_ref = refs[:2]
        ex = refs[2:2 + n_ex]
        outs = refs[2 + n_ex:2 + n_ex + n_o]
        part = jnp.dot(a_ref[...].astype(BF16), b_ref[...], preferred_element_type=F32)

        def finish(acc):
            res = epilogue(acc, *[e[...] for e in ex])
            for r, o in zip(res, outs):
                o[...] = r.astype(o.dtype)

        if nk == 1:
            finish(part)
        else:
            acc_ref = refs[-1]
            k = pl.program_id(2)

            @pl.when(k == 0)
            def _():
                acc_ref[...] = part

            @pl.when(k > 0)
            def _():
                acc_ref[...] += part

            @pl.when(k == nk - 1)
            def _():
                finish(acc_ref[...])

    res = pl.pallas_call(
        body, name=name, grid=(M // tm, N // tn, nk),
        in_specs=in_specs,
        out_specs=[pl.BlockSpec((tm, tn), lambda i, j, k: (i, j)) for _ in out_dtypes],
        out_shape=[jax.ShapeDtypeStruct((M, N), dt) for dt in out_dtypes],
        scratch_shapes=[pltpu.VMEM((tm, tn), F32)] if nk > 1 else [],
        compiler_params=_cparams("parallel", "parallel", "arbitrary"),
    )(a, b3, *[e for e, _ in extras])
    return res


def _mm_nt(name, g, w3, epilogue, out_dtypes, extras=()):
    M, N = g.shape
    S, K, Ns = w3.shape
    assert S * Ns == N
    tm, tn, tkk = _tile(M, MM_TM), _tile(Ns, MM_TN), _tile(K, MM_TK)
    per = Ns // tn
    nn = N // tn
    n_ex, n_o = len(extras), len(out_dtypes)

    def body(*refs):
        g_ref, w_ref = refs[:2]
        ex = refs[2:2 + n_ex]
        outs = refs[2 + n_ex:2 + n_ex + n_o]
        part = lax.dot_general(g_ref[...].astype(BF16), w_ref[...], (((1,), (1,)), ((), ())),
                               preferred_element_type=F32)

        def finish(acc):
            res = epilogue(acc, *[e[...] for e in ex])
            for r, o in zip(res, outs):
                o[...] = r.astype(o.dtype)

        if nn == 1:
            finish(part)
        else:
            acc_ref = refs[-1]
            n = pl.program_id(2)

            @pl.when(n == 0)
            def _():
                acc_ref[...] = part

            @pl.when(n > 0)
            def _():
                acc_ref[...] += part

            @pl.when(n == nn - 1)
            def _():
                finish(acc_ref[...])

    return pl.pallas_call(
        body, name=name, grid=(M // tm, K // tkk, nn),
        in_specs=[pl.BlockSpec((tm, tn), lambda i, kk, n: (i, n)),
                  pl.BlockSpec((None, tkk, tn), lambda i, kk, n: (n // per, kk, n % per))]
        + [pl.BlockSpec((tm, tkk), lambda i, kk, n: (i, kk)) for _ in extras],
        out_specs=[pl.BlockSpec((tm, tkk), lambda i, kk, n: (i, kk)) for _ in out_dtypes],
        out_shape=[jax.ShapeDtypeStruct((M, K), dt) for dt in out_dtypes],
        scratch_shapes=[pltpu.VMEM((tm, tkk), F32)] if nn > 1 else [],
        compiler_params=_cparams("parallel", "parallel", "arbitrary"),
    )(g, w3, *extras)


def _mm_tn(name, a, g, n_shards):
    T, K = a.shape
    _, N = g.shape
    Ns = N // n_shards
    tk, tn, tt = _tile(K, MM_TK), _tile(Ns, MM_TN), _tile(T, MM_TM)
    per = Ns // tn
    nt = T // tt

    def body(a_ref, g_ref, o_ref, acc_ref):
        t = pl.program_id(2)
        part = lax.dot_general(a_ref[...].astype(BF16), g_ref[...].astype(BF16), (((0,), (0,)), ((), ())),
                               preferred_element_type=F32)

        @pl.when(t == 0)
        def _():
            acc_ref[...] = part

        @pl.when(t > 0)
        def _():
            acc_ref[...] += part

        @pl.when(t == nt - 1)
        def _():
            o_ref[...] = acc_ref[...].astype(o_ref.dtype)

    return pl.pallas_call(
        body, name=name, grid=(K // tk, N // tn, nt),
        in_specs=[pl.BlockSpec((tt, tk), lambda i, j, t: (t, i)),
                  pl.BlockSpec((tt, tn), lambda i, j, t: (t, j))],
        out_specs=pl.BlockSpec((None, tk, tn), lambda i, j, t: (j // per, i, j % per)),
        out_shape=jax.ShapeDtypeStruct((n_shards, K, Ns), BF16),
        scratch_shapes=[pltpu.VMEM((tk, tn), F32)],
        compiler_params=_cparams("parallel", "parallel", "arbitrary"),
    )(a, g)


def _rowwise(name, fn, ins, outs, accs=(), scratch=(), tt=ROW_TILE):
    T = next(a.shape[0] for a, kind in ins if kind == "row")
    tt = _tile(T, tt)
    n = T // tt
    in_specs = []
    for a, kind in ins:
        w = a.shape[1]
        if kind == "row":
            in_specs.append(pl.BlockSpec((tt, w), lambda i: (i, 0)))
        elif kind == "vec":
            in_specs.append(pl.BlockSpec(a.shape, lambda i: (0, 0)))
        elif kind[0] == "prev":
            pad = kind[1]
            in_specs.append(pl.BlockSpec((pad, w), lambda i, q=tt // pad: (jnp.maximum(i * q - 1, 0), 0)))
        else:
            pad = kind[1]
            in_specs.append(pl.BlockSpec((pad, w), lambda i, q=tt // pad, last=T // pad - 1:
                                         (jnp.minimum((i + 1) * q, last), 0)))
    n_in, n_out, n_acc = len(ins), len(outs), len(accs)

    def body(*refs):
        i = pl.program_id(0)
        in_refs = refs[:n_in]
        out_refs = refs[n_in:n_in + n_out]
        acc_refs = refs[n_in + n_out:n_in + n_out + n_acc]
        scr = refs[n_in + n_out + n_acc:]
        if n_acc:
            @pl.when(i == 0)
            def _():
                for r in acc_refs:
                    r[...] = jnp.zeros_like(r)
        fn(i, n, in_refs, out_refs, acc_refs, scr)

    res = pl.pallas_call(
        body, name=name, grid=(n,),
        in_specs=in_specs,
        out_specs=[pl.BlockSpec((tt, w), lambda i: (i, 0)) for w, _ in outs]
        + [pl.BlockSpec((r, w), lambda i: (0, 0)) for r, w in accs],
        out_shape=[jax.ShapeDtypeStruct((T, w), dt) for w, dt in outs]
        + [jax.ShapeDtypeStruct((r, w), F32) for r, w in accs],
        scratch_shapes=list(scratch),
        compiler_params=_cparams("arbitrary"),
    )(*[a for a, _ in ins])
    return res


def _colsum(x):
    return jnp.sum(x, axis=0, keepdims=True)


def _rms_fwd(name, h, g):
    D = h.shape[1]

    def fn(i, n, ins, outs, accs, scr):
        x = ins[0][...]
        outs[0][...] = (x * _rms_r(x) * ins[1][...]).astype(BF16)

    return _rowwise(name, fn, [(h, "row"), (g, "vec")], [(D, BF16)])[0]


def _rms_bwd(name, h, g, du, dh_in, want_colsum=False):
    D = h.shape[1]

    def fn(i, n, ins, outs, accs, scr):
        x = ins[0][...]
        gg = ins[1][...]
        d = ins[2][...].astype(F32)
        r = _rms_r(x)
        xn = x * r
        t = d * gg
        dh = ins[3][...] + r * (t - xn * jnp.mean(t * xn, axis=-1, keepdims=True))
        outs[0][...] = dh
        outs[1][...] = dh.astype(BF16)
        accs[0][...] += _colsum(d * xn)
        if want_colsum:
            accs[1][...] += _colsum(dh)

    return _rowwise(name, fn, [(h, "row"), (g, "vec"), (du, "row"), (dh_in, "row")],
                    [(D, F32), (D, BF16)], accs=[(1, D)] * (2 if want_colsum else 1))


def _loss_bwd(name, h, g, tgt):
    D = h.shape[1]

    def fn(i, n, ins, outs, accs, scr):
        x = ins[0][...]
        gg = ins[1][...]
        r = _rms_r(x)
        xn = x * r
        err = xn * gg - ins[2][...]
        dy = err / D
        t = dy * gg
        dh = r * (t - xn * jnp.mean(t * xn, axis=-1, keepdims=True))
        outs[0][...] = dh
        outs[1][...] = dh.astype(BF16)
        accs[0][...] += _colsum(dy * xn)
        accs[1][...] += _colsum(err * err)

    return _rowwise(name, fn, [(h, "row"), (g, "vec"), (tgt, "row")], [(D, F32), (D, BF16)],
                    accs=[(1, D), (1, D)])


def _ple_elem_bwd(name, dh, q, e):
    D = dh.shape[1]

    def fn(i, n, ins, outs, accs, scr):
        d = ins[0][...]
        s = _sigmoid(ins[1][...].astype(F32))
        ee = ins[2][...].astype(F32)
        outs[0][...] = (d * ee * s * (1.0 - s)).astype(BF16)
        outs[1][...] = (d * s).astype(BF16)

    return _rowwise(name, fn, [(dh, "row"), (q, "row"), (e, "row")], [(D, BF16), (D, BF16)])


def _cf_norm_bwd(name, v2, g, dv4):
    D = v2.shape[1]

    def fn(i, n, ins, outs, accs, scr):
        x = ins[0][...]
        gg = ins[1][...]
        r = _rms_r(x)
        xn = x * r
        v3 = xn * gg
        s = _sigmoid(v3)
        dv3 = ins[2][...].astype(F32) * (s * (1.0 + v3 * (1.0 - s)))
        t = dv3 * gg
        dv2 = r * (t - xn * jnp.mean(t * xn, axis=-1, keepdims=True))
        outs[0][...] = dv2
        accs[0][...] += _colsum(dv3 * xn)
        accs[1][...] += _colsum(dv2)

    return _rowwise(name, fn, [(v2, "row"), (g, "vec"), (dv4, "row")], [(D, F32)], accs=[(1, D), (1, D)])


def _chunks(tt, width):
    cc = min(CONV_LANE_CHUNK, width)
    rc = min(CONV_ROW_CHUNK, tt)
    for c0 in range(0, width, cc):
        for r0 in range(0, tt, rc):
            yield r0, rc, c0, cc


def _n_shifts(n_taps):
    return min(SUBLANES - 1, n_taps - 1)


def _shifted_scratch(n_taps, rows, width):
    return pltpu.VMEM((_n_shifts(n_taps), rows, width), F32)


def _shift_window(win_ref, sh_ref, n_taps, sign):
    rows = win_ref.shape[0] - SUBLANES
    width = win_ref.shape[1]
    cc = min(CONV_LANE_CHUNK, width)
    for b in range(1, _n_shifts(n_taps) + 1):
        off = SUBLANES - b if sign < 0 else b
        for c0 in range(0, width, cc):
            sh_ref[b - 1, 0:rows, c0:c0 + cc] = win_ref[off:off + rows, c0:c0 + cc]


def _tap(win_ref, sh_ref, base, sign, s, r0, rc, c0, cc):
    a, b = divmod(s, SUBLANES)
    if b == 0:
        row = base + r0 + sign * SUBLANES * a
        return win_ref[row:row + rc, c0:c0 + cc]
    row = base + r0 - SUBLANES * (a + 1) if sign < 0 else base + r0 + SUBLANES * a
    return sh_ref[b - 1, row:row + rc, c0:c0 + cc]


def _fir(win_ref, sh_ref, w_ref, n_taps, base, sign, tt, width, emit):
    for r0, rc, c0, cc in _chunks(tt, width):
        acc = jnp.zeros((rc, cc), F32)
        for k in range(n_taps):
            acc = acc + w_ref[k:k + 1, c0:c0 + cc] * _tap(win_ref, sh_ref, base, sign, n_taps - 1 - k, r0, rc, c0, cc)
        emit(r0, rc, c0, cc, acc)


def _fir_wgrad(d_ref, win_ref, sh_ref, dw8_ref, n_taps, pad, tt, width):
    for c0 in range(0, width, min(CONV_LANE_CHUNK, width)):
        cc = min(CONV_LANE_CHUNK, width)
        rc = min(CONV_ROW_CHUNK, tt)
        for k in range(n_taps):
            acc = jnp.zeros((SUBLANES, cc), F32)
            for r0 in range(0, tt, rc):
                prod = d_ref[r0:r0 + rc, c0:c0 + cc] * _tap(win_ref, sh_ref, pad, -1, n_taps - 1 - k, r0, rc, c0, cc)
                for q in range(0, rc, SUBLANES):
                    acc = acc + prod[q:q + SUBLANES]
            dw8_ref[SUBLANES * k:SUBLANES * (k + 1), c0:c0 + cc] += acc


def _glu(blk, D):
    return blk[:, :D].astype(F32) * _sigmoid(blk[:, D:].astype(F32))


CF_PAD = 32
SC_PAD = 16


def _cf_conv_fwd(name, a, w_dw, b_dw, g_cf):
    T, D2 = a.shape
    D = D2 // 2
    K = w_dw.shape[0]
    tt = _tile(T, CONV_TILE_ROWS)

    def fn(i, n, ins, outs, accs, scr):
        a_ref, prev_ref, w_ref, b_ref, g_ref = ins
        win_ref, v2_ref, sh_ref = scr
        win_ref[0:CF_PAD, :] = jnp.where(i > 0, _glu(prev_ref[...], D), 0.0)
        win_ref[CF_PAD:CF_PAD + tt, :] = _glu(a_ref[...], D)
        _shift_window(win_ref, sh_ref, K, -1)

        def emit(r0, rc, c0, cc, acc):
            v2_ref[r0:r0 + rc, c0:c0 + cc] = acc + b_ref[:, c0:c0 + cc]

        _fir(win_ref, sh_ref, w_ref, K, CF_PAD, -1, tt, D, emit)
        v2 = v2_ref[...]
        v3 = v2 * _rms_r(v2) * g_ref[...]
        outs[0][...] = v2
        outs[1][...] = (v3 * _sigmoid(v3)).astype(BF16)

    return _rowwise(name, fn, [(a, "row"), (a, ("prev", CF_PAD)), (w_dw, "vec"), (b_dw, "vec"), (g_cf, "vec")],
                    [(D, F32), (D, BF16)],
                    scratch=[pltpu.VMEM((CF_PAD + tt, D), F32), pltpu.VMEM((tt, D), F32),
                             _shifted_scratch(K, CF_PAD + tt, D)], tt=tt)


def _cf_conv_bwd(name, dv2, a, w_dw):
    T, D2 = a.shape
    D = D2 // 2
    K = w_dw.shape[0]
    tt = _tile(T, CONV_TILE_ROWS)

    def fn(i, n, ins, outs, accs, scr):
        d_ref, dnext_ref, a_ref, prev_ref, w_ref = ins
        v1win_ref, dwin_ref, dv1_ref, dw8_ref, v1sh_ref, dsh_ref = scr

        @pl.when(i == 0)
        def _():
            dw8_ref[...] = jnp.zeros_like(dw8_ref)

        v1win_ref[0:CF_PAD, :] = jnp.where(i > 0, _glu(prev_ref[...], D), 0.0)
        v1win_ref[CF_PAD:CF_PAD + tt, :] = _glu(a_ref[...], D)
        dwin_ref[0:tt, :] = d_ref[...]
        dwin_ref[tt:tt + CF_PAD, :] = jnp.where(i < n - 1, dnext_ref[...], 0.0)
        _shift_window(v1win_ref, v1sh_ref, K, -1)
        _shift_window(dwin_ref, dsh_ref, K, 1)

        def emit(r0, rc, c0, cc, acc):
            dv1_ref[r0:r0 + rc, c0:c0 + cc] = acc

        _fir(dwin_ref, dsh_ref, w_ref, K, 0, 1, tt, D, emit)
        _fir_wgrad(d_ref, v1win_ref, v1sh_ref, dw8_ref, K, CF_PAD, tt, D)

        blk = a_ref[...]
        val = blk[:, :D].astype(F32)
        sg = _sigmoid(blk[:, D:].astype(F32))
        dv1 = dv1_ref[...]
        dval = dv1 * sg
        dgate = dv1 * val * sg * (1.0 - sg)
        outs[0][:, :D] = dval.astype(BF16)
        outs[0][:, D:] = dgate.astype(BF16)
        accs[1][:, :D] += _colsum(dval)
        accs[1][:, D:] += _colsum(dgate)

        @pl.when(i == n - 1)
        def _():
            for k in range(K):
                accs[0][k:k + 1, :] = _colsum(dw8_ref[SUBLANES * k:SUBLANES * (k + 1), :])

    return _rowwise(name, fn, [(dv2, "row"), (dv2, ("next", CF_PAD)), (a, "row"), (a, ("prev", CF_PAD)),
                               (w_dw, "vec")],
                    [(D2, BF16)], accs=[(K, D), (1, D2)],
                    scratch=[pltpu.VMEM((CF_PAD + tt, D), F32), pltpu.VMEM((tt + CF_PAD, D), F32),
                             pltpu.VMEM((tt, D), F32), pltpu.VMEM((SUBLANES * K, D), F32),
                             _shifted_scratch(K, CF_PAD + tt, D), _shifted_scratch(K, tt + CF_PAD, D)], tt=tt)


def _sc_conv_fwd(name, bcv, w_conv):
    T, D3 = bcv.shape
    D = D3 // 3
    K = w_conv.shape[0]
    tt = _tile(T, CONV_TILE_ROWS)

    def cv_of(blk):
        return blk[:, D:2 * D].astype(F32) * blk[:, 2 * D:].astype(F32)

    def fn(i, n, ins, outs, accs, scr):
        x_ref, prev_ref, w_ref = ins
        win_ref, cc_ref, sh_ref = scr
        win_ref[0:SC_PAD, :] = jnp.where(i > 0, cv_of(prev_ref[...]), 0.0)
        win_ref[SC_PAD:SC_PAD + tt, :] = cv_of(x_ref[...])
        _shift_window(win_ref, sh_ref, K, -1)

        def emit(r0, rc, c0, cw, acc):
            cc_ref[r0:r0 + rc, c0:c0 + cw] = acc

        _fir(win_ref, sh_ref, w_ref, K, SC_PAD, -1, tt, D, emit)
        outs[0][...] = (x_ref[:, :D].astype(F32) * cc_ref[...]).astype(BF16)

    return _rowwise(name, fn, [(bcv, "row"), (bcv, ("prev", SC_PAD)), (w_conv, "vec")], [(D, BF16)],
                    scratch=[pltpu.VMEM((SC_PAD + tt, D), F32), pltpu.VMEM((tt, D), F32),
                             _shifted_scratch(K, SC_PAD + tt, D)], tt=tt)


def _sc_conv_bwd(name, dy, bcv, w_conv):
    T, D3 = bcv.shape
    D = D3 // 3
    K = w_conv.shape[0]
    tt = _tile(T, CONV_TILE_ROWS)

    def cv_of(blk):
        return blk[:, D:2 * D].astype(F32) * blk[:, 2 * D:].astype(F32)

    def fn(i, n, ins, outs, accs, scr):
        dy_ref, dynext_ref, x_ref, prev_ref, next_ref, w_ref = ins
        cvwin_ref, dccwin_ref, tmp_ref, dw8_ref, cvsh_ref, dccsh_ref = scr

        @pl.when(i == 0)
        def _():
            dw8_ref[...] = jnp.zeros_like(dw8_ref)

        cvwin_ref[0:SC_PAD, :] = jnp.where(i > 0, cv_of(prev_ref[...]), 0.0)
        cvwin_ref[SC_PAD:SC_PAD + tt, :] = cv_of(x_ref[...])
        _shift_window(cvwin_ref, cvsh_ref, K, -1)

        def emit_cc(r0, rc, c0, cw, acc):
            tmp_ref[r0:r0 + rc, c0:c0 + cw] = acc

        _fir(cvwin_ref, cvsh_ref, w_ref, K, SC_PAD, -1, tt, D, emit_cc)
        dy_v = dy_ref[...].astype(F32)
        outs[0][:, :D] = (dy_v * tmp_ref[...]).astype(BF16)
        dccwin_ref[0:tt, :] = dy_v * x_ref[:, :D].astype(F32)
        dccwin_ref[tt:tt + SC_PAD, :] = jnp.where(
            i < n - 1, dynext_ref[...].astype(F32) * next_ref[:, :D].astype(F32), 0.0)

        _shift_window(dccwin_ref, dccsh_ref, K, 1)

        def emit_dcv(r0, rc, c0, cw, acc):
            tmp_ref[r0:r0 + rc, c0:c0 + cw] = acc

        _fir(dccwin_ref, dccsh_ref, w_ref, K, 0, 1, tt, D, emit_dcv)
        _fir_wgrad(dccwin_ref, cvwin_ref, cvsh_ref, dw8_ref, K, SC_PAD, tt, D)
        dcv = tmp_ref[...]
        outs[0][:, D:2 * D] = (dcv * x_ref[:, 2 * D:].astype(F32)).astype(BF16)
        outs[0][:, 2 * D:] = (dcv * x_ref[:, D:2 * D].astype(F32)).astype(BF16)

        @pl.when(i == n - 1)
        def _():
            for k in range(K):
                accs[0][k:k + 1, :] = _colsum(dw8_ref[SUBLANES * k:SUBLANES * (k + 1), :])

    return _rowwise(name, fn, [(dy, "row"), (dy, ("next", SC_PAD)), (bcv, "row"), (bcv, ("prev", SC_PAD)),
                               (bcv, ("next", SC_PAD)), (w_conv, "vec")],
                    [(D3, BF16)], accs=[(K, D)],
                    scratch=[pltpu.VMEM((SC_PAD + tt, D), F32), pltpu.VMEM((tt + SC_PAD, D), F32),
                             pltpu.VMEM((tt, D), F32), pltpu.VMEM((SUBLANES * K, D), F32),
                             _shifted_scratch(K, SC_PAD + tt, D), _shifted_scratch(K, tt + SC_PAD, D)], tt=tt)


def _place():
    x, y, c = lax.axis_index("x"), lax.axis_index("y"), lax.axis_index("c")
    chips = [(1 - x, y), (x, 1 - y), (1 - x, 1 - y)]
    return x, y, c, 2 * x + y, chips, (x, y, 1 - c)


def _half(rows, which):
    return pl.ds(pl.multiple_of(which * (rows // 2), SUBLANES), rows // 2)


_HBM = pl.BlockSpec(memory_space=pl.ANY)


def _gather_shards(name, items):
    n = len(items)
    shapes = [a.shape[-2:] for a, _ in items]

    def body(*refs):
        srcs, outs = refs[:n], refs[n:2 * n]
        send1, recv1, send2, recv2, lsem = refs[2 * n:]
        x, y, c, k, chips, sib = _place()

        def shard(i):
            return srcs[i] if items[i][1] is None else srcs[i].at[items[i][1]]

        started, locs = [], []
        for i in range(n):
            rows = shapes[i][0]
            lc = pltpu.make_async_copy(shard(i), outs[i].at[k], lsem.at[i])
            lc.start()
            locs.append(lc)
            for j, (cx, cy) in enumerate(chips):
                cp = pltpu.make_async_remote_copy(
                    src_ref=shard(i).at[_half(rows, c)], dst_ref=outs[i].at[k, _half(rows, c)],
                    send_sem=send1.at[i, j], recv_sem=recv1.at[i, j], device_id=(cx, cy, c), device_id_type=MESH)
                cp.start()
                started.append(cp)
        for i in range(n):
            rows = shapes[i][0]
            for j, (cx, cy) in enumerate(chips):
                blk = outs[i].at[2 * cx + cy, _half(rows, c)]
                pltpu.make_async_remote_copy(
                    src_ref=blk, dst_ref=blk, send_sem=send1.at[i, j], recv_sem=recv1.at[i, j],
                    device_id=(cx, cy, c), device_id_type=MESH).wait_recv()
                fw = pltpu.make_async_remote_copy(
                    src_ref=blk, dst_ref=blk, send_sem=send2.at[i, j], recv_sem=recv2.at[i, j],
                    device_id=sib, device_id_type=MESH)
                fw.start()
                started.append(fw)
        for i in range(n):
            rows = shapes[i][0]
            for j, (cx, cy) in enumerate(chips):
                blk = outs[i].at[2 * cx + cy, _half(rows, 1 - c)]
                pltpu.make_async_remote_copy(
                    src_ref=blk, dst_ref=blk, send_sem=send2.at[i, j], recv_sem=recv2.at[i, j],
                    device_id=sib, device_id_type=MESH).wait_recv()
        for cp in started:
            cp.wait_send()
        for lc in locs:
            lc.wait()

    return pl.pallas_call(
        body, name=name,
        in_specs=[_HBM] * n, out_specs=[_HBM] * n,
        out_shape=[jax.ShapeDtypeStruct((N_SHARDS,) + tuple(s), a.dtype) for s, (a, _) in zip(shapes, items)],
        scratch_shapes=[pltpu.SemaphoreType.DMA((n, 3))] * 4 + [pltpu.SemaphoreType.DMA((n,))],
    )(*[a for a, _ in items])


def _cast_place(name, w, layer, pos):
    _, R, C = w.shape
    tr = _tile(R, 256)

    def body(x_ref, y_ref, c_ref, w_ref, o_ref):
        o_ref[...] = w_ref[...].astype(BF16)

    return pl.pallas_call(
        body, name=name,
        grid_spec=pltpu.PrefetchScalarGridSpec(
            num_scalar_prefetch=3, grid=(R // tr,),
            in_specs=[pl.BlockSpec((None, tr, C), lambda r, xr, yr, cr: (layer, r, 0))],
            out_specs=pl.BlockSpec((None, tr, C), lambda r, xr, yr, cr: (2 * xr[0] + yr[0], r, 0))),
        out_shape=jax.ShapeDtypeStruct((N_SHARDS, R, C), BF16),
        compiler_params=_cparams("parallel"),
    )(*pos, w)


_IN_HBM = pl.BlockSpec(memory_space=pltpu.HBM)
_SEM = pl.BlockSpec(memory_space=pltpu.SEMAPHORE)
_SPLIT_COPY_PARAMS = pltpu.CompilerParams(has_side_effects=pltpu.SideEffectType.DATAFLOW_SIDE_EFFECTING)


def _in_hbm(a):
    return pltpu.with_memory_space_constraint(a, pltpu.HBM)


def _gather_copy(ref, i, j, chip_xy, c, k_src, rows, send, recv):
    blk = ref.at[k_src, _half(rows, c)]
    return pltpu.make_async_remote_copy(
        src_ref=blk, dst_ref=blk, send_sem=send.at[3 * i + j], recv_sem=recv.at[3 * i + j],
        device_id=(*chip_xy, c), device_id_type=MESH)


def _gather_start(name, bufs, after):
    n = len(bufs)

    def body(*refs):
        ins = refs[:n]
        send, recv = refs[n + 1], refs[n + 2]
        token = refs[-1]
        x, y, c, k, chips, sib = _place()
        for i in range(n):
            for j, chip_xy in enumerate(chips):
                _gather_copy(ins[i], i, j, chip_xy, c, k, bufs[i].shape[1], send, recv).start()
        token[...] = jnp.zeros_like(token)

    res = pl.pallas_call(
        body, name=name,
        in_specs=[_IN_HBM] * n + [_HBM],
        out_specs=[_SEM, _SEM] + [_IN_HBM] * n + [pl.BlockSpec(memory_space=pltpu.VMEM)],
        out_shape=[pltpu.SemaphoreType.DMA((3 * n,)), pltpu.SemaphoreType.DMA((3 * n,))]
        + [pltpu.HBM(b.shape, b.dtype) for b in bufs] + [jax.ShapeDtypeStruct((SUBLANES, 128), F32)],
        input_output_aliases={i: 2 + i for i in range(n)},
        compiler_params=_SPLIT_COPY_PARAMS,
    )(*[_in_hbm(b) for b in bufs], after)
    return res[0], res[1], list(res[2:2 + n]), res[-1]


def _gather_wait(name, bufs, send, recv, after):
    n = len(bufs)

    def body(*refs):
        ins = refs[:n]
        send_ref, recv_ref = refs[n], refs[n + 1]
        x, y, c, k, chips, sib = _place()
        for i in range(n):
            for j, chip_xy in enumerate(chips):
                rows = bufs[i].shape[1]
                _gather_copy(ins[i], i, j, chip_xy, c, k, rows, send_ref, recv_ref).wait_send()
                _gather_copy(ins[i], i, j, chip_xy, c, 2 * chip_xy[0] + chip_xy[1], rows, send_ref, recv_ref).wait_recv()

    return pl.pallas_call(
        body, name=name,
        in_specs=[_IN_HBM] * n + [_SEM, _SEM, _HBM],
        out_specs=[_IN_HBM] * n,
        out_shape=[pltpu.HBM(b.shape, b.dtype) for b in bufs],
        input_output_aliases={i: i for i in range(n)},
        compiler_params=_SPLIT_COPY_PARAMS,
    )(*bufs, send, recv, after)


def _gather_forward(name, bufs):
    n = len(bufs)

    def body(*refs):
        outs = refs[n:2 * n]
        send, recv = refs[2 * n:]
        x, y, c, k, chips, sib = _place()
        started = []
        for i in range(n):
            rows = bufs[i].shape[1]
            for j, (cx, cy) in enumerate(chips):
                blk = outs[i].at[2 * cx + cy, _half(rows, c)]
                fw = pltpu.make_async_remote_copy(
                    src_ref=blk, dst_ref=blk, send_sem=send.at[i, j], recv_sem=recv.at[i, j],
                    device_id=sib, device_id_type=MESH)
                fw.start()
                started.append(fw)
        for i in range(n):
            rows = bufs[i].shape[1]
            for j, (cx, cy) in enumerate(chips):
                blk = outs[i].at[2 * cx + cy, _half(rows, 1 - c)]
                pltpu.make_async_remote_copy(
                    src_ref=blk, dst_ref=blk, send_sem=send.at[i, j], recv_sem=recv.at[i, j],
                    device_id=sib, device_id_type=MESH).wait_recv()
        for cp in started:
            cp.wait_send()

    return pl.pallas_call(
        body, name=name, in_specs=[_HBM] * n, out_specs=[_HBM] * n,
        out_shape=[jax.ShapeDtypeStruct(b.shape, b.dtype) for b in bufs],
        input_output_aliases={i: i for i in range(n)},
        scratch_shapes=[pltpu.SemaphoreType.DMA((n, 3))] * 2,
    )(*bufs)


def _swap_halves(name, parts):
    n = len(parts)

    def body(*refs):
        srcs, outs = refs[:n], refs[n:2 * n]
        send, recv = refs[2 * n:]
        x, y, c, k, chips, sib = _place()
        cps = []
        for i in range(n):
            rows = parts[i].shape[1]
            cp = pltpu.make_async_remote_copy(
                src_ref=srcs[i].at[:, _half(rows, 1 - c)], dst_ref=outs[i],
                send_sem=send.at[i], recv_sem=recv.at[i], device_id=sib, device_id_type=MESH)
            cp.start()
            cps.append(cp)
        for cp in cps:
            cp.wait()

    return pl.pallas_call(
        body, name=name, in_specs=[_HBM] * n, out_specs=[_HBM] * n,
        out_shape=[jax.ShapeDtypeStruct((p.shape[0], p.shape[1] // 2, p.shape[2]), p.dtype) for p in parts],
        scratch_shapes=[pltpu.SemaphoreType.DMA((n,))] * 2,
    )(*parts)


def _scatter_copy(src_ref, land_ref, i, j, chip_xy, c, send, recv):
    return pltpu.make_async_remote_copy(
        src_ref=src_ref.at[2 * chip_xy[0] + chip_xy[1]], dst_ref=land_ref.at[j],
        send_sem=send.at[3 * i + j], recv_sem=recv.at[3 * i + j], device_id=(*chip_xy, c), device_id_type=MESH)


def _scatter_start(name, sums):
    n = len(sums)
    lands = [lax.empty((3,) + s.shape[1:], s.dtype) for s in sums]

    def body(*refs):
        srcs, lnds = refs[:n], refs[n:2 * n]
        send, recv = refs[2 * n], refs[2 * n + 1]
        token = refs[-1]
        x, y, c, k, chips, sib = _place()
        for i in range(n):
            for j, chip_xy in enumerate(chips):
                _scatter_copy(srcs[i], lnds[i], i, j, chip_xy, c, send, recv).start()
        token[...] = jnp.zeros_like(token)

    res = pl.pallas_call(
        body, name=name,
        in_specs=[_IN_HBM] * (2 * n),
        out_specs=[_SEM, _SEM] + [_IN_HBM] * (2 * n) + [pl.BlockSpec(memory_space=pltpu.VMEM)],
        out_shape=[pltpu.SemaphoreType.DMA((3 * n,)), pltpu.SemaphoreType.DMA((3 * n,))]
        + [pltpu.HBM(a.shape, a.dtype) for a in list(sums) + lands] + [jax.ShapeDtypeStruct((SUBLANES, 128), F32)],
        input_output_aliases={i: 2 + i for i in range(2 * n)},
        compiler_params=_SPLIT_COPY_PARAMS,
    )(*[_in_hbm(a) for a in list(sums) + lands])
    return res[0], res[1], list(res[2:2 + n]), list(res[2 + n:2 + 2 * n]), res[-1]


def _scatter_wait(name, sums, lands, send, recv, afters):
    n = len(sums)

    def body(*refs):
        srcs, lnds = refs[:n], refs[n:2 * n]
        send_ref, recv_ref = refs[2 * n], refs[2 * n + 1]
        x, y, c, k, chips, sib = _place()
        for i in range(n):
            for j, chip_xy in enumerate(chips):
                cp = _scatter_copy(srcs[i], lnds[i], i, j, chip_xy, c, send_ref, recv_ref)
                cp.wait_send()
                cp.wait_recv()

    res = pl.pallas_call(
        body, name=name,
        in_specs=[_IN_HBM] * (2 * n) + [_SEM, _SEM] + [_HBM] * len(afters),
        out_specs=[_IN_HBM] * (2 * n),
        out_shape=[pltpu.HBM(a.shape, a.dtype) for a in list(sums) + list(lands)],
        input_output_aliases={i: i for i in range(2 * n)},
        compiler_params=_SPLIT_COPY_PARAMS,
    )(*sums, *lands, send, recv, *afters)
    return list(res[:n]), list(res[n:])


def _join_halves(name, fulls):
    n = len(fulls)

    def body(*refs):
        outs = refs[n:2 * n]
        send, recv = refs[2 * n:]
        x, y, c, k, chips, sib = _place()
        cps = []
        for i in range(n):
            blk = outs[i].at[_half(fulls[i].shape[0], c)]
            cp = pltpu.make_async_remote_copy(
                src_ref=blk, dst_ref=blk, send_sem=send.at[i], recv_sem=recv.at[i],
                device_id=sib, device_id_type=MESH)
            cp.start()
            cps.append(cp)
        for i in range(n):
            blk = outs[i].at[_half(fulls[i].shape[0], 1 - c)]
            pltpu.make_async_remote_copy(
                src_ref=blk, dst_ref=blk, send_sem=send.at[i], recv_sem=recv.at[i],
                device_id=sib, device_id_type=MESH).wait_recv()
        for cp in cps:
            cp.wait_send()

    return pl.pallas_call(
        body, name=name, in_specs=[_HBM] * n, out_specs=[_HBM] * n,
        out_shape=[jax.ShapeDtypeStruct(f.shape, f.dtype) for f in fulls],
        input_output_aliases={i: i for i in range(n)},
        scratch_shapes=[pltpu.SemaphoreType.DMA((n,))] * 2,
    )(*fulls)


def _sum_over_devices(name, buf, loss_row):
    R, D = buf.shape

    def body(x_ref, all_ref, tot_ref, loss_ref, send_sems, recv_sems, local_sem):
        x, y, c, k, chips, sib = _place()
        me = (x, y, c)

        def block(px, py, pc):
            return all_ref.at[4 * px + 2 * py + pc]

        def copy(kk, blk, to, src=None):
            return pltpu.make_async_remote_copy(
                src_ref=block(*blk) if src is None else src, dst_ref=block(*blk),
                send_sem=send_sems.at[kk], recv_sem=recv_sems.at[kk], device_id=to, device_id_type=MESH)

        mine = pltpu.make_async_copy(x_ref, block(*me), local_sem)
        mine.start()
        first = [copy(0, me, sib, src=x_ref)]
        first += [copy(1 + j, me, (*chip, c), src=x_ref) for j, chip in enumerate(chips)]
        for cp in first:
            cp.start()
        passed = [copy(4 + j, (*chip, c), sib) for j, chip in enumerate(chips)]
        for j, chip in enumerate(chips):
            copy(1 + j, (*chip, c), me).wait_recv()
            passed[j].start()
        copy(0, sib, me).wait_recv()
        for j, chip in enumerate(chips):
            copy(4 + j, (*chip, 1 - c), me).wait_recv()
        for cp in first + passed:
            cp.wait_send()
        mine.wait()
        rc = _tile(R, 32)
        for r0 in range(0, R, rc):
            tot = all_ref[0, r0:r0 + rc, :]
            for d in range(1, N_DEVICES):
                tot = tot + all_ref[d, r0:r0 + rc, :]
            tot_ref[r0:r0 + rc, :] = tot
        loss = 0.5 * jnp.sum(tot_ref[loss_row:loss_row + 1, :]) / D
        loss_ref[...] = jnp.full(loss_ref.shape, loss, F32)

    vm = pl.BlockSpec(memory_space=pltpu.VMEM)
    return pl.pallas_call(
        body, name=name, in_specs=[vm], out_specs=[vm, vm, vm],
        out_shape=[jax.ShapeDtypeStruct((N_DEVICES, R, D), F32), jax.ShapeDtypeStruct((R, D), F32),
                   jax.ShapeDtypeStruct((SUBLANES, 128), F32)],
        scratch_shapes=[pltpu.SemaphoreType.DMA((7,)), pltpu.SemaphoreType.DMA((7,)), pltpu.SemaphoreType.DMA],
        compiler_params=pltpu.CompilerParams(vmem_limit_bytes=V7X_VMEM_LIMIT_BYTES),
    )(buf)[1:]


def _add_my_half(name, part, got, pos):
    S, R, C = part.shape
    R2 = R // 2
    tr = _tile(R2, 512)
    q = R2 // tr

    def body(x_ref, y_ref, c_ref, p_ref, g_ref, o_ref):
        o_ref[...] = (p_ref[...].astype(F32) + g_ref[...].astype(F32)).astype(o_ref.dtype)

    return pl.pallas_call(
        body, name=name,
        grid_spec=pltpu.PrefetchScalarGridSpec(
            num_scalar_prefetch=3, grid=(S, q),
            in_specs=[pl.BlockSpec((None, tr, C), lambda s, r, xr, yr, cr: (s, cr[0] * q + r, 0)),
                      pl.BlockSpec((None, tr, C), lambda s, r, xr, yr, cr: (s, r, 0))],
            out_specs=pl.BlockSpec((None, tr, C), lambda s, r, xr, yr, cr: (s, r, 0))),
        out_shape=jax.ShapeDtypeStruct((S, R2, C), BF16),
        compiler_params=_cparams("parallel", "parallel"),
    )(*pos, part, got)


def _add_owner(name, sums, got, pos):
    _, R2, C = sums.shape
    tr = _tile(R2, 512)
    q = R2 // tr

    def body(x_ref, y_ref, c_ref, s_ref, g_ref, o_ref):
        acc = s_ref[...].astype(F32)
        for j in range(3):
            acc = acc + g_ref[j].astype(F32)
        o_ref[...] = acc

    return pl.pallas_call(
        body, name=name,
        grid_spec=pltpu.PrefetchScalarGridSpec(
            num_scalar_prefetch=3, grid=(q,),
            in_specs=[pl.BlockSpec((None, tr, C), lambda r, xr, yr, cr: (2 * xr[0] + yr[0], r, 0)),
                      pl.BlockSpec((3, tr, C), lambda r, xr, yr, cr: (0, r, 0))],
            out_specs=pl.BlockSpec((tr, C), lambda r, xr, yr, cr: (cr[0] * q + r, 0))),
        out_shape=jax.ShapeDtypeStruct((2 * R2, C), F32),
        compiler_params=_cparams("parallel"),
    )(*pos, sums, got)


def _adamw(name, w, m, v, g):
    R, C = w.shape
    tr = SUBLANES
    while 2 * tr * C <= ADAMW_TILE_ELEMS:
        tr *= 2
    tr = _tile(R, tr)
    bc1 = 1.0 - ADAM_B1 ** ADAM_STEP
    bc2 = 1.0 - ADAM_B2 ** ADAM_STEP

    def body(w_ref, m_ref, v_ref, g_ref, go_ref, d_ref, mo_ref, vo_ref):
        gg = g_ref[...]
        m2 = ADAM_B1 * m_ref[...] + (1.0 - ADAM_B1) * gg
        v2 = ADAM_B2 * v_ref[...] + (1.0 - ADAM_B2) * (gg * gg)
        go_ref[...] = gg
        mo_ref[...] = m2
        vo_ref[...] = v2
        d_ref[...] = -ADAM_LR * ((m2 / bc1) / (jnp.sqrt(v2 / bc2) + ADAM_EPS) + ADAM_WD * w_ref[...])

    spec = pl.BlockSpec((tr, C), lambda r: (r, 0))
    return pl.pallas_call(
        body, name=name, grid=(R // tr,), in_specs=[spec] * 4, out_specs=[spec] * 4,
        out_shape=[jax.ShapeDtypeStruct((R, C), F32)] * 4,
        compiler_params=_cparams("parallel"),
    )(w, m, v, g)


def _adamw_slab(name, w, m, v, g, layer, prev):
    L, R, C = w.shape
    tr = SUBLANES
    while 2 * tr * C <= ADAMW_TILE_ELEMS:
        tr *= 2
    tr = _tile(R, tr)
    bc1 = 1.0 - ADAM_B1 ** ADAM_STEP
    bc2 = 1.0 - ADAM_B2 ** ADAM_STEP

    def body(w_ref, m_ref, v_ref, g_ref, *rest):
        go_ref, d_ref, mo_ref, vo_ref = rest[-4:]
        gg = g_ref[...]
        m2 = ADAM_B1 * m_ref[...] + (1.0 - ADAM_B1) * gg
        v2 = ADAM_B2 * v_ref[...] + (1.0 - ADAM_B2) * (gg * gg)
        go_ref[...] = gg
        mo_ref[...] = m2
        vo_ref[...] = v2
        d_ref[...] = -ADAM_LR * ((m2 / bc1) / (jnp.sqrt(v2 / bc2) + ADAM_EPS) + ADAM_WD * w_ref[...])

    slab = pl.BlockSpec((None, tr, C), lambda r: (layer, r, 0))
    n_prev = 0 if prev is None else 4
    return pl.pallas_call(
        body, name=name, grid=(R // tr,),
        in_specs=[slab] * 3 + [pl.BlockSpec((tr, C), lambda r: (r, 0))] + [_HBM] * n_prev,
        out_specs=[slab] * 4,
        out_shape=[jax.ShapeDtypeStruct((L, R, C), F32)] * 4,
        input_output_aliases={4 + i: i for i in range(n_prev)},
        compiler_params=_cparams("parallel"),
    )(w, m, v, g, *(prev or ()))


def _reduce_begin(tag, parts, pos):
    got = _swap_halves(f"rs_swap_{tag}", parts)
    sums = [_add_my_half(f"rs_add2_{tag}_{i}", p, g, pos) for i, (p, g) in enumerate(zip(parts, got))]
    return _scatter_start(f"rs_scatter_start_{tag}", sums)


def _reduce_end(tag, started, pos, afters):
    send, recv, sums, lands, _ = started
    sums, lands = _scatter_wait(f"rs_scatter_wait_{tag}", sums, lands, send, recv, afters)
    fulls = [_add_owner(f"rs_add4_{tag}_{i}", s, q, pos) for i, (s, q) in enumerate(zip(sums, lands))]
    return _join_halves(f"rs_join_{tag}", fulls)


def _pad_rows(a):
    r = (-a.shape[0]) % SUBLANES
    return jnp.pad(a, ((0, r), (0, 0))) if r else a


def kernel(x, p, norm_mix, norm_mlp, norm_ple, cf_w_pw1, cf_b_pw1, cf_w_dw, cf_b_dw, cf_norm, cf_w_pw2, cf_b_pw2, sc_w_in, sc_w_conv, sc_w_out, mlp_w1, mlp_w2, ple_w_proj, ple_w_gate, norm_final, loss_target, m_norm_mix, m_norm_mlp, m_norm_ple, m_cf_w_pw1, m_cf_b_pw1, m_cf_w_dw, m_cf_b_dw, m_cf_norm, m_cf_w_pw2, m_cf_b_pw2, m_sc_w_in, m_sc_w_conv, m_sc_w_out, m_mlp_w1, m_mlp_w2, m_ple_w_proj, m_ple_w_gate, m_norm_final, v_norm_mix, v_norm_mlp, v_norm_ple, v_cf_w_pw1, v_cf_b_pw1, v_cf_w_dw, v_cf_b_dw, v_cf_norm, v_cf_w_pw2, v_cf_b_pw2, v_sc_w_in, v_sc_w_conv, v_sc_w_out, v_mlp_w1, v_mlp_w2, v_ple_w_proj, v_ple_w_gate, v_norm_final):
    T, D = x.shape[1], x.shape[2]
    KA, KB = cf_w_dw.shape[1], sc_w_conv.shape[1]
    chip = (2 * lax.axis_index("x") + lax.axis_index("y")).astype(jnp.int32)
    pos = tuple(lax.axis_index(ax).astype(jnp.int32).reshape(1) for ax in ("x", "y", "c"))

    params = dict(norm_mix=norm_mix, norm_mlp=norm_mlp, norm_ple=norm_ple, cf_w_pw1=cf_w_pw1, cf_b_pw1=cf_b_pw1,
                  cf_w_dw=cf_w_dw, cf_b_dw=cf_b_dw, cf_norm=cf_norm, cf_w_pw2=cf_w_pw2, cf_b_pw2=cf_b_pw2,
                  sc_w_in=sc_w_in, sc_w_conv=sc_w_conv, sc_w_out=sc_w_out, mlp_w1=mlp_w1, mlp_w2=mlp_w2,
                  ple_w_proj=ple_w_proj, ple_w_gate=ple_w_gate, norm_final=norm_final)
    mom1 = dict(norm_mix=m_norm_mix, norm_mlp=m_norm_mlp, norm_ple=m_norm_ple, cf_w_pw1=m_cf_w_pw1,
                cf_b_pw1=m_cf_b_pw1, cf_w_dw=m_cf_w_dw, cf_b_dw=m_cf_b_dw, cf_norm=m_cf_norm, cf_w_pw2=m_cf_w_pw2,
                cf_b_pw2=m_cf_b_pw2, sc_w_in=m_sc_w_in, sc_w_conv=m_sc_w_conv, sc_w_out=m_sc_w_out,
                mlp_w1=m_mlp_w1, mlp_w2=m_mlp_w2, ple_w_proj=m_ple_w_proj, ple_w_gate=m_ple_w_gate,
                norm_final=m_norm_final)
    mom2 = dict(norm_mix=v_norm_mix, norm_mlp=v_norm_mlp, norm_ple=v_norm_ple, cf_w_pw1=v_cf_w_pw1,
                cf_b_pw1=v_cf_b_pw1, cf_w_dw=v_cf_w_dw, cf_b_dw=v_cf_b_dw, cf_norm=v_cf_norm, cf_w_pw2=v_cf_w_pw2,
                cf_b_pw2=v_cf_b_pw2, sc_w_in=v_sc_w_in, sc_w_conv=v_sc_w_conv, sc_w_out=v_sc_w_out,
                mlp_w1=v_mlp_w1, mlp_w2=v_mlp_w2, ple_w_proj=v_ple_w_proj, ple_w_gate=v_ple_w_gate,
                norm_final=v_norm_final)

    big = ("cf_w_pw1", "cf_w_pw2", "sc_w_in", "sc_w_out", "mlp_w1", "mlp_w2", "ple_w_proj", "ple_w_gate")
    row_sharded = ("cf_w_pw2", "sc_w_out", "mlp_w2", "ple_w_gate")

    def layer_names(i):
        return (["cf_w_pw1", "cf_w_pw2"] if i % 2 == 0 else ["sc_w_in", "sc_w_out"]) + \
            ["mlp_w1", "mlp_w2", "ple_w_proj", "ple_w_gate"]

    def layer_index(i, name):
        return i if name.startswith(("mlp", "ple")) else i // 2

    def gather_begin(i, after):
        bufs = [_cast_place(f"place_{nm}_{i}", params[nm], layer_index(i, nm), pos) for nm in layer_names(i)]
        return _gather_start(f"gather_start_{i}", bufs, after)

    def gather_end(i, started, after):
        send, recv, bufs, _ = started
        bufs = _gather_wait(f"gather_wait_{i}", bufs, send, recv, after)
        bufs = _gather_forward(f"gather_fwd_{i}", bufs)
        return {nm: g4.reshape(1, N_SHARDS * g4.shape[1], g4.shape[2]) if nm in row_sharded else g4
                for nm, g4 in zip(layer_names(i), bufs)}

    conv_small = jnp.concatenate([_pad_rows(cf_w_dw[j]) for j in range(cf_w_dw.shape[0])]
                                 + [_pad_rows(sc_w_conv[j]) for j in range(sc_w_conv.shape[0])], axis=0)
    conv_all = _gather_shards("gather_conv_w", [(conv_small, None)])[0]
    conv_all = jnp.transpose(conv_all, (1, 0, 2)).reshape(conv_small.shape[0], D)
    ka_pad = KA + (-KA) % SUBLANES
    kb_pad = KB + (-KB) % SUBLANES
    w_dw_full = [conv_all[j * ka_pad:j * ka_pad + KA] for j in range(cf_w_dw.shape[0])]
    off = cf_w_dw.shape[0] * ka_pad
    w_conv_full = [conv_all[off + j * kb_pad:off + j * kb_pad + KB] for j in range(sc_w_conv.shape[0])]

    def vec(a):
        return a.reshape(1, -1)

    ident = lambda acc: (acc,)

    h = x[0]
    saved = []
    W = [gather_end(0, gather_begin(0, h), h)]
    for i in range(DEPTH):
        j = i // 2
        wl = W[i]
        s = dict(h=h)
        g_mix = vec(norm_mix[i])
        if i + 1 < DEPTH:
            nxt = gather_begin(i + 1, wl["mlp_w1"])
            g_mix = g_mix + nxt[3][0, 0]
        s["u"] = _rms_fwd(f"rms_mix_{i}", h, g_mix)
        if i % 2 == 0:
            s["a"] = _mm_nn(f"cf_pw1_{i}", s["u"], wl["cf_w_pw1"], lambda acc, b: (acc + b,), [BF16],
                            extras=[(vec(cf_b_pw1[j]), "n")])[0]
            s["v2"], s["v4"] = _cf_conv_fwd(f"cf_conv_{i}", s["a"], w_dw_full[j], vec(cf_b_dw[j]), vec(cf_norm[j]))
            h1 = _mm_nn(f"cf_pw2_{i}", s["v4"], wl["cf_w_pw2"], lambda acc, b, r: (r + (acc + b),), [F32],
                        extras=[(vec(cf_b_pw2[j]), "n"), (h, "mn")])[0]
        else:
            s["bcv"] = _mm_nn(f"sc_in_{i}", s["u"], wl["sc_w_in"], ident, [BF16])[0]
            s["y"] = _sc_conv_fwd(f"sc_conv_{i}", s["bcv"], w_conv_full[j])[0]
            h1 = _mm_nn(f"sc_out_{i}", s["y"], wl["sc_w_out"], lambda acc, r: (r + acc,), [F32],
                        extras=[(h, "mn")])[0]
        s["h1"] = h1
        s["u2"] = _rms_fwd(f"rms_mlp_{i}", h1, vec(norm_mlp[i]))
        s["z"], s["hd"] = _mm_nn(f"mlp_w1_{i}", s["u2"], wl["mlp_w1"],
                                 lambda acc: (acc, jnp.square(jnp.maximum(acc, 0.0))), [BF16, BF16])
        h2 = _mm_nn(f"mlp_w2_{i}", s["hd"], wl["mlp_w2"], lambda acc, r: (r + acc,), [F32], extras=[(h1, "mn")])[0]
        s["h2"] = h2
        s["n3"] = _rms_fwd(f"rms_ple_{i}", h2, vec(norm_ple[i]))
        s["p"] = p[i, 0]
        s["e"] = _mm_nn(f"ple_proj_{i}", s["p"], wl["ple_w_proj"], ident, [BF16])[0]
        h, s["q"] = _mm_nn(f"ple_gate_{i}", s["n3"], wl["ple_w_gate"],
                           lambda acc, r, e: (r + _sigmoid(acc) * e.astype(F32), acc), [F32, BF16],
                           extras=[(h2, "mn"), (s["e"], "mn")])
        saved.append(s)
        if i + 1 < DEPTH:
            W.append(gather_end(i + 1, nxt, h))

    dh, dh16, dg_final, loss_cols = _loss_bwd("loss_bwd", h, vec(norm_final), loss_target[0])
    small = {"norm_final": dg_final, "loss": loss_cols}
    adam = {nm: None for nm in big}

    def finish_layer(i, started, afters):
        for nm, g in zip(layer_names(i), _reduce_end(f"{i}", started, pos, afters)):
            l = layer_index(i, nm)
            adam[nm] = _adamw_slab(f"adamw_{nm}_{l}", params[nm], mom1[nm], mom2[nm], g, l, adam[nm])

    pending = None
    for i in reversed(range(DEPTH)):
        j = i // 2
        wl, s = W[i], saved[i]
        dq, de = _ple_elem_bwd(f"ple_elem_bwd_{i}", dh, s["q"], s["e"])
        d_proj = _mm_tn(f"ple_proj_dw_{i}", s["p"], de, N_SHARDS)
        d_gate = _mm_tn(f"ple_gate_dw_{i}", s["n3"], dq, 1)
        dn3 = _mm_nt(f"ple_gate_dx_{i}", dq, wl["ple_w_gate"], ident, [F32])[0]
        dh, dh16, small[f"norm_ple_{i}"] = _rms_bwd(f"rms_ple_bwd_{i}", s["h2"], vec(norm_ple[i]), dn3, dh)

        d_w2 = _mm_tn(f"mlp_w2_dw_{i}", s["hd"], dh16, 1)
        dz = _mm_nt(f"mlp_w2_dx_{i}", dh16, wl["mlp_w2"],
                    lambda acc, z: (acc * (2.0 * jnp.maximum(z.astype(F32), 0.0)),), [BF16], extras=[s["z"]])[0]
        d_w1 = _mm_tn(f"mlp_w1_dw_{i}", s["u2"], dz, N_SHARDS)
        du2 = _mm_nt(f"mlp_w1_dx_{i}", dz, wl["mlp_w1"], ident, [F32])[0]
        if i % 2 == 0:
            dh, dh16, small[f"norm_mlp_{i}"], small[f"cf_b_pw2_{j}"] = _rms_bwd(
                f"rms_mlp_bwd_{i}", s["h1"], vec(norm_mlp[i]), du2, dh, want_colsum=True)
            d_mix_out = _mm_tn(f"cf_pw2_dw_{i}", s["v4"], dh16, 1)
            dv4 = _mm_nt(f"cf_pw2_dx_{i}", dh16, wl["cf_w_pw2"], ident, [F32])[0]
            dv2, small[f"cf_norm_{j}"], small[f"cf_b_dw_{j}"] = _cf_norm_bwd(
                f"cf_norm_bwd_{i}", s["v2"], vec(cf_norm[j]), dv4)
            da, small[f"cf_w_dw_{j}"], db1 = _cf_conv_bwd(f"cf_conv_bwd_{i}", dv2, s["a"], w_dw_full[j])
            small[f"cf_b_pw1_{j}"] = db1.reshape(2, D)
            d_mix_in = _mm_tn(f"cf_pw1_dw_{i}", s["u"], da, N_SHARDS)
            du = _mm_nt(f"cf_pw1_dx_{i}", da, wl["cf_w_pw1"], ident, [F32])[0]
        else:
            dh, dh16, small[f"norm_mlp_{i}"] = _rms_bwd(f"rms_mlp_bwd_{i}", s["h1"], vec(norm_mlp[i]), du2, dh)
            d_mix_out = _mm_tn(f"sc_out_dw_{i}", s["y"], dh16, 1)
            dy = _mm_nt(f"sc_out_dx_{i}", dh16, wl["sc_w_out"], ident, [F32])[0]
            da, small[f"sc_w_conv_{j}"] = _sc_conv_bwd(f"sc_conv_bwd_{i}", dy, s["bcv"], w_conv_full[j])
            d_mix_in = _mm_tn(f"sc_in_dw_{i}", s["u"], da, N_SHARDS)
            du = _mm_nt(f"sc_in_dx_{i}", da, wl["sc_w_in"], ident, [F32])[0]

        parts = [d_mix_in, d_mix_out, d_w1, d_w2, d_proj, d_gate]
        parts = [pt.reshape(N_SHARDS, pt.shape[1] // N_SHARDS, pt.shape[2]) if nm in row_sharded else pt
                 for nm, pt in zip(layer_names(i), parts)]
        started = _reduce_begin(f"{i}", parts, pos)
        token = started[4]
        dh, dh16, small[f"norm_mix_{i}"] = _rms_bwd(f"rms_mix_bwd_{i}", s["h"], vec(norm_mix[i]) + token[0, 0], du, dh)
        if pending is not None:
            finish_layer(i + 1, pending, [token])
        pending = started
    grad_x = dh.reshape(x.shape)

    order = sorted(small)
    pieces, where, row = [], {}, 0
    for nm in order:
        pc = _pad_rows(small[nm])
        where[nm] = (row, small[nm].shape[0])
        row += pc.shape[0]
        pieces.append(pc)
    total, loss_tile = _sum_over_devices("small_allsum", jnp.concatenate(pieces, axis=0), where["loss"][0])
    loss = loss_tile[0, 0]
    finish_layer(0, pending, [total] + [res[3] for res in adam.values() if res is not None])

    def small_sum(nm):
        r0, nr = where[nm]
        return total[r0:r0 + nr]

    def my_cols(a):
        return lax.dynamic_slice_in_dim(a, chip * (D // N_SHARDS), D // N_SHARDS, axis=1)

    g_small = {
        "norm_mix": jnp.concatenate([small_sum(f"norm_mix_{i}") for i in range(DEPTH)], axis=0),
        "norm_mlp": jnp.concatenate([small_sum(f"norm_mlp_{i}") for i in range(DEPTH)], axis=0),
        "norm_ple": jnp.concatenate([small_sum(f"norm_ple_{i}") for i in range(DEPTH)], axis=0),
        "cf_b_pw1": jnp.stack([small_sum(f"cf_b_pw1_{j}").reshape(2 * D) for j in range(DEPTH // 2)]),
        "cf_w_dw": jnp.stack([my_cols(small_sum(f"cf_w_dw_{j}")) for j in range(DEPTH // 2)]),
        "cf_b_dw": jnp.concatenate([small_sum(f"cf_b_dw_{j}") for j in range(DEPTH // 2)], axis=0),
        "cf_norm": jnp.concatenate([small_sum(f"cf_norm_{j}") for j in range(DEPTH // 2)], axis=0),
        "cf_b_pw2": jnp.concatenate([small_sum(f"cf_b_pw2_{j}") for j in range(DEPTH // 2)], axis=0),
        "sc_w_conv": jnp.stack([my_cols(small_sum(f"sc_w_conv_{j}")) for j in range(DEPTH // 2)]),
        "norm_final": small_sum("norm_final").reshape(D),
    }

    names_out = ["norm_mix", "norm_mlp", "norm_ple", "cf_w_pw1", "cf_b_pw1", "cf_w_dw", "cf_b_dw", "cf_norm",
                 "cf_w_pw2", "cf_b_pw2", "sc_w_in", "sc_w_conv", "sc_w_out", "mlp_w1", "mlp_w2", "ple_w_proj",
                 "ple_w_gate", "norm_final"]
    grad, delta, new_m, new_v = {}, {}, {}, {}
    for nm in names_out:
        w = params[nm]
        if nm in big:
            grad[nm], delta[nm], new_m[nm], new_v[nm] = adam[nm]
            continue
        g = g_small[nm]
        cols = w.shape[-1] if w.ndim > 1 else w.shape[0]
        two_d = lambda a: a.reshape(-1, cols)
        res = _adamw(f"adamw_{nm}", two_d(w), two_d(mom1[nm]), two_d(mom2[nm]), two_d(g))
        grad[nm], delta[nm], new_m[nm], new_v[nm] = [r.reshape(w.shape) for r in res]

    return (loss, grad_x, *[grad[n] for n in names_out], *[delta[n] for n in names_out],
            *[new_m[n] for n in names_out], *[new_v[n] for n in names_out])
```

```python
import functools

import jax
import jax.numpy as jnp
from jax import lax
from jax.experimental import pallas as pl
from jax.experimental.pallas import tpu as pltpu

F32 = jnp.float32
BF16 = jnp.bfloat16

EPS = 1e-6
ADAM_LR = 0.001
ADAM_B1 = 0.9
ADAM_B2 = 0.999
ADAM_EPS = 1e-08
ADAM_WD = 0.01
ADAM_STEP = 10

DEPTH = 4
N_SHARDS = 4
N_DEVICES = 8
V7X_VMEM_LIMIT_BYTES = 56 * 1024 * 1024
SUBLANES = 8
MESH = pl.DeviceIdType.MESH

MM_TM = 1024
MM_TM_FUSED = 512
MM_TN = 1024
MM_TK = 2048
MM_TW_K = 1024
MM_TW_T = 4096
ADAMW_TILE_ELEMS = 256 * 2048

CONV_ROW_CHUNK = 32
CONV_LANE_CHUNK = 512
CONV_TILE_ROWS = 128
ROW_TILE = 256


def _tile(dim, pref):
    if dim <= pref:
        return dim
    t = pref
    while dim % t:
        t //= 2
    return t


def _cparams(*sem):
    return pltpu.CompilerParams(dimension_semantics=sem, vmem_limit_bytes=V7X_VMEM_LIMIT_BYTES)


def _sigmoid(x):
    return 1.0 / (1.0 + jnp.exp(-x))


def _rms_r(x):
    return lax.rsqrt(jnp.mean(x * x, axis=-1, keepdims=True) + EPS)


def _mm_nn(name, a, b3, epilogue, out_dtypes, extras=()):
    M, K = a.shape
    S, Kb, Ns = b3.shape
    assert Kb == K
    N = S * Ns
    crowded = sum(kind == "mn" for _, kind in extras) > 1
    tm, tn, tk = _tile(M, MM_TM_FUSED if crowded else MM_TM), _tile(Ns, MM_TN), _tile(K, MM_TK)
    per = Ns // tn
    nk = K // tk
    in_specs = [pl.BlockSpec((tm, tk), lambda i, j, k: (i, k)),
                pl.BlockSpec((None, tk, tn), lambda i, j, k: (j // per, k, j % per))]
    for _, kind in extras:
        if kind == "mn":
            in_specs.append(pl.BlockSpec((tm, tn), lambda i, j, k: (i, j)))
        else:
            in_specs.append(pl.BlockSpec((1, tn), lambda i, j, k: (0, j)))
    n_ex, n_o = len(extras), len(out_dtypes)

    def body(*refs):
        a_ref, b_ref = refs[:2]
        ex = refs[2:2 + n_ex]
        outs = refs[2 + n_ex:2 + n_ex + n_o]
        part = jnp.dot(a_ref[...].astype(BF16), b_ref[...], preferred_element_type=F32)

        def finish(acc):
            res = epilogue(acc, *[e[...] for e in ex])
            for r, o in zip(res, outs):
                o[...] = r.astype(o.dtype)

        if nk == 1:
            finish(part)
        else:
            acc_ref = refs[-1]
            k = pl.program_id(2)

            @pl.when(k == 0)
            def _():
                acc_ref[...] = part

            @pl.when(k > 0)
            def _():
                acc_ref[...] += part

            @pl.when(k == nk - 1)
            def _():
                finish(acc_ref[...])

    res = pl.pallas_call(
        body, name=name, grid=(M // tm, N // tn, nk),
        in_specs=in_specs,
        out_specs=[pl.BlockSpec((tm, tn), lambda i, j, k: (i, j)) for _ in out_dtypes],
        out_shape=[jax.ShapeDtypeStruct((M, N), dt) for dt in out_dtypes],
        scratch_shapes=[pltpu.VMEM((tm, tn), F32)] if nk > 1 else [],
        compiler_params=_cparams("parallel", "parallel", "arbitrary"),
    )(a, b3, *[e for e, _ in extras])
    return res


def _mm_nt(name, g, w3, epilogue, out_dtypes, extras=()):
    M, N = g.shape
    S, K, Ns = w3.shape
    assert S * Ns == N
    tm, tn, tkk = _tile(M, MM_TM), _tile(Ns, MM_TN), _tile(K, MM_TK)
    per = Ns // tn
    nn = N // tn
    n_ex, n_o = len(extras), len(out_dtypes)

    def body(*refs):
        g_ref, w_ref = refs[:2]
        ex = refs[2:2 + n_ex]
        outs = refs[2 + n_ex:2 + n_ex + n_o]
        part = lax.dot_general(g_ref[...].astype(BF16), w_ref[...], (((1,), (1,)), ((), ())),
                               preferred_element_type=F32)

        def finish(acc):
            res = epilogue(acc, *[e[...] for e in ex])
            for r, o in zip(res, outs):
                o[...] = r.astype(o.dtype)

        if nn == 1:
            finish(part)
        else:
            acc_ref = refs[-1]
            n = pl.program_id(2)

            @pl.when(n == 0)
            def _():
                acc_ref[...] = part

            @pl.when(n > 0)
            def _():
                acc_ref[...] += part

            @pl.when(n == nn - 1)
            def _():
                finish(acc_ref[...])

    return pl.pallas_call(
        body, name=name, grid=(M // tm, K // tkk, nn),
        in_specs=[pl.BlockSpec((tm, tn), lambda i, kk, n: (i, n)),
                  pl.BlockSpec((None, tkk, tn), lambda i, kk, n: (n // per, kk, n % per))]
        + [pl.BlockSpec((tm, tkk), lambda i, kk, n: (i, kk)) for _ in extras],
        out_specs=[pl.BlockSpec((tm, tkk), lambda i, kk, n: (i, kk)) for _ in out_dtypes],
        out_shape=[jax.ShapeDtypeStruct((M, K), dt) for dt in out_dtypes],
        scratch_shapes=[pltpu.VMEM((tm, tkk), F32)] if nn > 1 else [],
        compiler_params=_cparams("parallel", "parallel", "arbitrary"),
    )(g, w3, *extras)


def _mm_tn(name, a, g, n_shards):
    T, K = a.shape
    _, N = g.shape
    Ns = N // n_shards
    tk, tn, tt = _tile(K, MM_TW_K), _tile(Ns, MM_TN), _tile(T, MM_TW_T)
    per = Ns // tn
    nt = T // tt

    def body(a_ref, g_ref, o_ref, *scratch):
        part = lax.dot_general(a_ref[...].astype(BF16), g_ref[...].astype(BF16), (((0,), (0,)), ((), ())),
                               preferred_element_type=F32)
        if nt == 1:
            o_ref[...] = part.astype(o_ref.dtype)
            return
        acc_ref, = scratch
        t = pl.program_id(2)

        @pl.when(t == 0)
        def _():
            acc_ref[...] = part

        @pl.when(t > 0)
        def _():
            acc_ref[...] += part

        @pl.when(t == nt - 1)
        def _():
            o_ref[...] = acc_ref[...].astype(o_ref.dtype)

    return pl.pallas_call(
        body, name=name, grid=(K // tk, N // tn, nt),
        in_specs=[pl.BlockSpec((tt, tk), lambda i, j, t: (t, i)),
                  pl.BlockSpec((tt, tn), lambda i, j, t: (t, j))],
        out_specs=pl.BlockSpec((None, tk, tn), lambda i, j, t: (j // per, i, j % per)),
        out_shape=jax.ShapeDtypeStruct((n_shards, K, Ns), BF16),
        scratch_shapes=[pltpu.VMEM((tk, tn), F32)] if nt > 1 else [],
        compiler_params=_cparams("parallel", "parallel", "arbitrary"),
    )(a, g)


def _rowwise(name, fn, ins, outs, accs=(), scratch=(), tt=ROW_TILE):
    T = next(a.shape[0] for a, kind in ins if kind == "row")
    tt = _tile(T, tt)
    n = T // tt
    in_specs = []
    for a, kind in ins:
        w = a.shape[1]
        if kind == "row":
            in_specs.append(pl.BlockSpec((tt, w), lambda i: (i, 0)))
        elif kind == "vec":
            in_specs.append(pl.BlockSpec(a.shape, lambda i: (0, 0)))
        elif kind[0] == "prev":
            pad = kind[1]
            in_specs.append(pl.BlockSpec((pad, w), lambda i, q=tt // pad: (jnp.maximum(i * q - 1, 0), 0)))
        else:
            pad = kind[1]
            in_specs.append(pl.BlockSpec((pad, w), lambda i, q=tt // pad, last=T // pad - 1:
                                         (jnp.minimum((i + 1) * q, last), 0)))
    n_in, n_out, n_acc = len(ins), len(outs), len(accs)

    def body(*refs):
        i = pl.program_id(0)
        in_refs = refs[:n_in]
        out_refs = refs[n_in:n_in + n_out]
        acc_refs = refs[n_in + n_out:n_in + n_out + n_acc]
        scr = refs[n_in + n_out + n_acc:]
        if n_acc:
            @pl.when(i == 0)
            def _():
                for r in acc_refs:
                    r[...] = jnp.zeros_like(r)
        fn(i, n, in_refs, out_refs, acc_refs, scr)

    res = pl.pallas_call(
        body, name=name, grid=(n,),
        in_specs=in_specs,
        out_specs=[pl.BlockSpec((tt, w), lambda i: (i, 0)) for w, _ in outs]
        + [pl.BlockSpec((r, w), lambda i: (0, 0)) for r, w in accs],
        out_shape=[jax.ShapeDtypeStruct((T, w), dt) for w, dt in outs]
        + [jax.ShapeDtypeStruct((r, w), F32) for r, w in accs],
        scratch_shapes=list(scratch),
        compiler_params=_cparams("arbitrary"),
    )(*[a for a, _ in ins])
    return res


def _colsum(x):
    return jnp.sum(x, axis=0, keepdims=True)


def _rms_fwd(name, h, g):
    D = h.shape[1]

    def fn(i, n, ins, outs, accs, scr):
        x = ins[0][...]
        outs[0][...] = (x * _rms_r(x) * ins[1][...]).astype(BF16)

    return _rowwise(name, fn, [(h, "row"), (g, "vec")], [(D, BF16)])[0]


def _rms_bwd(name, h, g, du, dh_in, want_colsum=False):
    D = h.shape[1]

    def fn(i, n, ins, outs, accs, scr):
        x = ins[0][...]
        gg = ins[1][...]
        d = ins[2][...].astype(F32)
        r = _rms_r(x)
        xn = x * r
        t = d * gg
        dh = ins[3][...] + r * (t - xn * jnp.mean(t * xn, axis=-1, keepdims=True))
        outs[0][...] = dh
        outs[1][...] = dh.astype(BF16)
        accs[0][...] += _colsum(d * xn)
        if want_colsum:
            accs[1][...] += _colsum(dh)

    return _rowwise(name, fn, [(h, "row"), (g, "vec"), (du, "row"), (dh_in, "row")],
                    [(D, F32), (D, BF16)], accs=[(1, D)] * (2 if want_colsum else 1))


def _loss_bwd(name, h, g, tgt):
    D = h.shape[1]

    def fn(i, n, ins, outs, accs, scr):
        x = ins[0][...]
        gg = ins[1][...]
        r = _rms_r(x)
        xn = x * r
        err = xn * gg - ins[2][...]
        dy = err / D
        t = dy * gg
        dh = r * (t - xn * jnp.mean(t * xn, axis=-1, keepdims=True))
        outs[0][...] = dh
        outs[1][...] = dh.astype(BF16)
        accs[0][...] += _colsum(dy * xn)
        accs[1][...] += _colsum(err * err)

    return _rowwise(name, fn, [(h, "row"), (g, "vec"), (tgt, "row")], [(D, F32), (D, BF16)],
                    accs=[(1, D), (1, D)])


def _ple_elem_bwd(name, dh, q, e):
    D = dh.shape[1]

    def fn(i, n, ins, outs, accs, scr):
        d = ins[0][...]
        s = _sigmoid(ins[1][...].astype(F32))
        ee = ins[2][...].astype(F32)
        outs[0][...] = (d * ee * s * (1.0 - s)).astype(BF16)
        outs[1][...] = (d * s).astype(BF16)

    return _rowwise(name, fn, [(dh, "row"), (q, "row"), (e, "row")], [(D, BF16), (D, BF16)])


def _cf_norm_bwd(name, v2, g, dv4):
    D = v2.shape[1]

    def fn(i, n, ins, outs, accs, scr):
        x = ins[0][...]
        gg = ins[1][...]
        r = _rms_r(x)
        xn = x * r
        v3 = xn * gg
        s = _sigmoid(v3)
        dv3 = ins[2][...].astype(F32) * (s * (1.0 + v3 * (1.0 - s)))
        t = dv3 * gg
        dv2 = r * (t - xn * jnp.mean(t * xn, axis=-1, keepdims=True))
        outs[0][...] = dv2
        accs[0][...] += _colsum(dv3 * xn)
        accs[1][...] += _colsum(dv2)

    return _rowwise(name, fn, [(v2, "row"), (g, "vec"), (dv4, "row")], [(D, F32)], accs=[(1, D), (1, D)])


def _chunks(tt, width):
    cc = min(CONV_LANE_CHUNK, width)
    rc = min(CONV_ROW_CHUNK, tt)
    for c0 in range(0, width, cc):
        for r0 in range(0, tt, rc):
            yield r0, rc, c0, cc


def _n_shifts(n_taps):
    return min(SUBLANES - 1, n_taps - 1)


def _shifted_scratch(n_taps, rows, width):
    return pltpu.VMEM((_n_shifts(n_taps), rows, width), F32)


def _shift_window(win_ref, sh_ref, n_taps, sign):
    rows = win_ref.shape[0] - SUBLANES
    width = win_ref.shape[1]
    cc = min(CONV_LANE_CHUNK, width)
    for b in range(1, _n_shifts(n_taps) + 1):
        off = SUBLANES - b if sign < 0 else b
        for c0 in range(0, width, cc):
            sh_ref[b - 1, 0:rows, c0:c0 + cc] = win_ref[off:off + rows, c0:c0 + cc]


def _tap(win_ref, sh_ref, base, sign, s, r0, rc, c0, cc):
    a, b = divmod(s, SUBLANES)
    if b == 0:
        row = base + r0 + sign * SUBLANES * a
        return win_ref[row:row + rc, c0:c0 + cc]
    row = base + r0 - SUBLANES * (a + 1) if sign < 0 else base + r0 + SUBLANES * a
    return sh_ref[b - 1, row:row + rc, c0:c0 + cc]


def _fir(win_ref, sh_ref, w_ref, n_taps, base, sign, tt, width, emit):
    for r0, rc, c0, cc in _chunks(tt, width):
        acc = jnp.zeros((rc, cc), F32)
        for k in range(n_taps):
            acc = acc + w_ref[k:k + 1, c0:c0 + cc] * _tap(win_ref, sh_ref, base, sign, n_taps - 1 - k, r0, rc, c0, cc)
        emit(r0, rc, c0, cc, acc)


def _fir_wgrad(d_ref, win_ref, sh_ref, dw8_ref, n_taps, pad, tt, width):
    for c0 in range(0, width, min(CONV_LANE_CHUNK, width)):
        cc = min(CONV_LANE_CHUNK, width)
        rc = min(CONV_ROW_CHUNK, tt)
        for k in range(n_taps):
            acc = jnp.zeros((SUBLANES, cc), F32)
            for r0 in range(0, tt, rc):
                prod = d_ref[r0:r0 + rc, c0:c0 + cc] * _tap(win_ref, sh_ref, pad, -1, n_taps - 1 - k, r0, rc, c0, cc)
                for q in range(0, rc, SUBLANES):
                    acc = acc + prod[q:q + SUBLANES]
            dw8_ref[SUBLANES * k:SUBLANES * (k + 1), c0:c0 + cc] += acc


def _glu(blk, D):
    return blk[:, :D].astype(F32) * _sigmoid(blk[:, D:].astype(F32))


CF_PAD = 32
SC_PAD = 16


def _cf_conv_fwd(name, a, w_dw, b_dw, g_cf):
    T, D2 = a.shape
    D = D2 // 2
    K = w_dw.shape[0]
    tt = _tile(T, CONV_TILE_ROWS)

    def fn(i, n, ins, outs, accs, scr):
        a_ref, prev_ref, w_ref, b_ref, g_ref = ins
        win_ref, v2_ref, sh_ref = scr
        win_ref[0:CF_PAD, :] = jnp.where(i > 0, _glu(prev_ref[...], D), 0.0)
        win_ref[CF_PAD:CF_PAD + tt, :] = _glu(a_ref[...], D)
        _shift_window(win_ref, sh_ref, K, -1)

        def emit(r0, rc, c0, cc, acc):
            v2_ref[r0:r0 + rc, c0:c0 + cc] = acc + b_ref[:, c0:c0 + cc]

        _fir(win_ref, sh_ref, w_ref, K, CF_PAD, -1, tt, D, emit)
        v2 = v2_ref[...]
        v3 = v2 * _rms_r(v2) * g_ref[...]
        outs[0][...] = v2
        outs[1][...] = (v3 * _sigmoid(v3)).astype(BF16)

    return _rowwise(name, fn, [(a, "row"), (a, ("prev", CF_PAD)), (w_dw, "vec"), (b_dw, "vec"), (g_cf, "vec")],
                    [(D, F32), (D, BF16)],
                    scratch=[pltpu.VMEM((CF_PAD + tt, D), F32), pltpu.VMEM((tt, D), F32),
                             _shifted_scratch(K, CF_PAD + tt, D)], tt=tt)


def _cf_conv_bwd(name, dv2, a, w_dw):
    T, D2 = a.shape
    D = D2 // 2
    K = w_dw.shape[0]
    tt = _tile(T, CONV_TILE_ROWS)

    def fn(i, n, ins, outs, accs, scr):
        d_ref, dnext_ref, a_ref, prev_ref, w_ref = ins
        v1win_ref, dwin_ref, dv1_ref, dw8_ref, v1sh_ref, dsh_ref = scr

        @pl.when(i == 0)
        def _():
            dw8_ref[...] = jnp.zeros_like(dw8_ref)

        v1win_ref[0:CF_PAD, :] = jnp.where(i > 0, _glu(prev_ref[...], D), 0.0)
        v1win_ref[CF_PAD:CF_PAD + tt, :] = _glu(a_ref[...], D)
        dwin_ref[0:tt, :] = d_ref[...]
        dwin_ref[tt:tt + CF_PAD, :] = jnp.where(i < n - 1, dnext_ref[...], 0.0)
        _shift_window(v1win_ref, v1sh_ref, K, -1)
        _shift_window(dwin_ref, dsh_ref, K, 1)

        def emit(r0, rc, c0, cc, acc):
            dv1_ref[r0:r0 + rc, c0:c0 + cc] = acc

        _fir(dwin_ref, dsh_ref, w_ref, K, 0, 1, tt, D, emit)
        _fir_wgrad(d_ref, v1win_ref, v1sh_ref, dw8_ref, K, CF_PAD, tt, D)

        blk = a_ref[...]
        val = blk[:, :D].astype(F32)
        sg = _sigmoid(blk[:, D:].astype(F32))
        dv1 = dv1_ref[...]
        dval = dv1 * sg
        dgate = dv1 * val * sg * (1.0 - sg)
        outs[0][:, :D] = dval.astype(BF16)
        outs[0][:, D:] = dgate.astype(BF16)
        accs[1][:, :D] += _colsum(dval)
        accs[1][:, D:] += _colsum(dgate)

        @pl.when(i == n - 1)
        def _():
            for k in range(K):
                accs[0][k:k + 1, :] = _colsum(dw8_ref[SUBLANES * k:SUBLANES * (k + 1), :])

    return _rowwise(name, fn, [(dv2, "row"), (dv2, ("next", CF_PAD)), (a, "row"), (a, ("prev", CF_PAD)),
                               (w_dw, "vec")],
                    [(D2, BF16)], accs=[(K, D), (1, D2)],
                    scratch=[pltpu.VMEM((CF_PAD + tt, D), F32), pltpu.VMEM((tt + CF_PAD, D), F32),
                             pltpu.VMEM((tt, D), F32), pltpu.VMEM((SUBLANES * K, D), F32),
                             _shifted_scratch(K, CF_PAD + tt, D), _shifted_scratch(K, tt + CF_PAD, D)], tt=tt)


def _sc_conv_fwd(name, bcv, w_conv):
    T, D3 = bcv.shape
    D = D3 // 3
    K = w_conv.shape[0]
    tt = _tile(T, CONV_TILE_ROWS)

    def cv_of(blk):
        return blk[:, D:2 * D].astype(F32) * blk[:, 2 * D:].astype(F32)

    def fn(i, n, ins, outs, accs, scr):
        x_ref, prev_ref, w_ref = ins
        win_ref, cc_ref, sh_ref = scr
        win_ref[0:SC_PAD, :] = jnp.where(i > 0, cv_of(prev_ref[...]), 0.0)
        win_ref[SC_PAD:SC_PAD + tt, :] = cv_of(x_ref[...])
        _shift_window(win_ref, sh_ref, K, -1)

        def emit(r0, rc, c0, cw, acc):
            cc_ref[r0:r0 + rc, c0:c0 + cw] = acc

        _fir(win_ref, sh_ref, w_ref, K, SC_PAD, -1, tt, D, emit)
        outs[0][...] = (x_ref[:, :D].astype(F32) * cc_ref[...]).astype(BF16)

    return _rowwise(name, fn, [(bcv, "row"), (bcv, ("prev", SC_PAD)), (w_conv, "vec")], [(D, BF16)],
                    scratch=[pltpu.VMEM((SC_PAD + tt, D), F32), pltpu.VMEM((tt, D), F32),
                             _shifted_scratch(K, SC_PAD + tt, D)], tt=tt)


def _sc_conv_bwd(name, dy, bcv, w_conv):
    T, D3 = bcv.shape
    D = D3 // 3
    K = w_conv.shape[0]
    tt = _tile(T, CONV_TILE_ROWS)

    def cv_of(blk):
        return blk[:, D:2 * D].astype(F32) * blk[:, 2 * D:].astype(F32)

    def fn(i, n, ins, outs, accs, scr):
        dy_ref, dynext_ref, x_ref, prev_ref, next_ref, w_ref = ins
        cvwin_ref, dccwin_ref, tmp_ref, dw8_ref, cvsh_ref, dccsh_ref = scr

        @pl.when(i == 0)
        def _():
            dw8_ref[...] = jnp.zeros_like(dw8_ref)

        cvwin_ref[0:SC_PAD, :] = jnp.where(i > 0, cv_of(prev_ref[...]), 0.0)
        cvwin_ref[SC_PAD:SC_PAD + tt, :] = cv_of(x_ref[...])
        _shift_window(cvwin_ref, cvsh_ref, K, -1)

        def emit_cc(r0, rc, c0, cw, acc):
            tmp_ref[r0:r0 + rc, c0:c0 + cw] = acc

        _fir(cvwin_ref, cvsh_ref, w_ref, K, SC_PAD, -1, tt, D, emit_cc)
        dy_v = dy_ref[...].astype(F32)
        outs[0][:, :D] = (dy_v * tmp_ref[...]).astype(BF16)
        dccwin_ref[0:tt, :] = dy_v * x_ref[:, :D].astype(F32)
        dccwin_ref[tt:tt + SC_PAD, :] = jnp.where(
            i < n - 1, dynext_ref[...].astype(F32) * next_ref[:, :D].astype(F32), 0.0)

        _shift_window(dccwin_ref, dccsh_ref, K, 1)

        def emit_dcv(r0, rc, c0, cw, acc):
            tmp_ref[r0:r0 + rc, c0:c0 + cw] = acc

        _fir(dccwin_ref, dccsh_ref, w_ref, K, 0, 1, tt, D, emit_dcv)
        _fir_wgrad(dccwin_ref, cvwin_ref, cvsh_ref, dw8_ref, K, SC_PAD, tt, D)
        dcv = tmp_ref[...]
        outs[0][:, D:2 * D] = (dcv * x_ref[:, 2 * D:].astype(F32)).astype(BF16)
        outs[0][:, 2 * D:] = (dcv * x_ref[:, D:2 * D].astype(F32)).astype(BF16)

        @pl.when(i == n - 1)
        def _():
            for k in range(K):
                accs[0][k:k + 1, :] = _colsum(dw8_ref[SUBLANES * k:SUBLANES * (k + 1), :])

    return _rowwise(name, fn, [(dy, "row"), (dy, ("next", SC_PAD)), (bcv, "row"), (bcv, ("prev", SC_PAD)),
                               (bcv, ("next", SC_PAD)), (w_conv, "vec")],
                    [(D3, BF16)], accs=[(K, D)],
                    scratch=[pltpu.VMEM((SC_PAD + tt, D), F32), pltpu.VMEM((tt + SC_PAD, D), F32),
                             pltpu.VMEM((tt, D), F32), pltpu.VMEM((SUBLANES * K, D), F32),
                             _shifted_scratch(K, SC_PAD + tt, D), _shifted_scratch(K, tt + SC_PAD, D)], tt=tt)


def _place():
    x, y, c = lax.axis_index("x"), lax.axis_index("y"), lax.axis_index("c")
    chips = [(1 - x, y), (x, 1 - y), (1 - x, 1 - y)]
    return x, y, c, 2 * x + y, chips, (x, y, 1 - c)


def _half(rows, which):
    return pl.ds(pl.multiple_of(which * (rows // 2), SUBLANES), rows // 2)


_HBM = pl.BlockSpec(memory_space=pl.ANY)


def _gather_shards(name, items):
    n = len(items)
    shapes = [a.shape[-2:] for a, _ in items]

    def body(*refs):
        srcs, outs = refs[:n], refs[n:2 * n]
        send1, recv1, send2, recv2, lsem = refs[2 * n:]
        x, y, c, k, chips, sib = _place()

        def shard(i):
            return srcs[i] if items[i][1] is None else srcs[i].at[items[i][1]]

        started, locs = [], []
        for i in range(n):
            rows = shapes[i][0]
            lc = pltpu.make_async_copy(shard(i), outs[i].at[k], lsem.at[i])
            lc.start()
            locs.append(lc)
            for j, (cx, cy) in enumerate(chips):
                cp = pltpu.make_async_remote_copy(
                    src_ref=shard(i).at[_half(rows, c)], dst_ref=outs[i].at[k, _half(rows, c)],
                    send_sem=send1.at[i, j], recv_sem=recv1.at[i, j], device_id=(cx, cy, c), device_id_type=MESH)
                cp.start()
                started.append(cp)
        for i in range(n):
            rows = shapes[i][0]
            for j, (cx, cy) in enumerate(chips):
                blk = outs[i].at[2 * cx + cy, _half(rows, c)]
                pltpu.make_async_remote_copy(
                    src_ref=blk, dst_ref=blk, send_sem=send1.at[i, j], recv_sem=recv1.at[i, j],
                    device_id=(cx, cy, c), device_id_type=MESH).wait_recv()
                fw = pltpu.make_async_remote_copy(
                    src_ref=blk, dst_ref=blk, send_sem=send2.at[i, j], recv_sem=recv2.at[i, j],
                    device_id=sib, device_id_type=MESH)
                fw.start()
                started.append(fw)
        for i in range(n):
            rows = shapes[i][0]
            for j, (cx, cy) in enumerate(chips):
                blk = outs[i].at[2 * cx + cy, _half(rows, 1 - c)]
                pltpu.make_async_remote_copy(
                    src_ref=blk, dst_ref=blk, send_sem=send2.at[i, j], recv_sem=recv2.at[i, j],
                    device_id=sib, device_id_type=MESH).wait_recv()
        for cp in started:
            cp.wait_send()
        for lc in locs:
            lc.wait()

    return pl.pallas_call(
        body, name=name,
        in_specs=[_HBM] * n, out_specs=[_HBM] * n,
        out_shape=[jax.ShapeDtypeStruct((N_SHARDS,) + tuple(s), a.dtype) for s, (a, _) in zip(shapes, items)],
        scratch_shapes=[pltpu.SemaphoreType.DMA((n, 3))] * 4 + [pltpu.SemaphoreType.DMA((n,))],
    )(*[a for a, _ in items])


def _cast_place(name, w, layer, pos):
    _, R, C = w.shape
    tr = _tile(R, 256)

    def body(x_ref, y_ref, c_ref, w_ref, o_ref):
        o_ref[...] = w_ref[...].astype(BF16)

    return pl.pallas_call(
        body, name=name,
        grid_spec=pltpu.PrefetchScalarGridSpec(
            num_scalar_prefetch=3, grid=(R // tr,),
            in_specs=[pl.BlockSpec((None, tr, C), lambda r, xr, yr, cr: (layer, r, 0))],
            out_specs=pl.BlockSpec((None, tr, C), lambda r, xr, yr, cr: (2 * xr[0] + yr[0], r, 0))),
        out_shape=jax.ShapeDtypeStruct((N_SHARDS, R, C), BF16),
        compiler_params=_cparams("parallel"),
    )(*pos, w)


_IN_HBM = pl.BlockSpec(memory_space=pltpu.HBM)
_SEM = pl.BlockSpec(memory_space=pltpu.SEMAPHORE)
_SPLIT_COPY_PARAMS = pltpu.CompilerParams(has_side_effects=pltpu.SideEffectType.DATAFLOW_SIDE_EFFECTING)


def _in_hbm(a):
    return pltpu.with_memory_space_constraint(a, pltpu.HBM)


def _gather_copy(ref, i, j, chip_xy, c, k_src, rows, send, recv):
    blk = ref.at[k_src, _half(rows, c)]
    return pltpu.make_async_remote_copy(
        src_ref=blk, dst_ref=blk, send_sem=send.at[3 * i + j], recv_sem=recv.at[3 * i + j],
        device_id=(*chip_xy, c), device_id_type=MESH)


def _gather_start(name, bufs, after):
    n = len(bufs)

    def body(*refs):
        ins = refs[:n]
        send, recv = refs[n + 1], refs[n + 2]
        token = refs[-1]
        x, y, c, k, chips, sib = _place()
        for i in range(n):
            for j, chip_xy in enumerate(chips):
                _gather_copy(ins[i], i, j, chip_xy, c, k, bufs[i].shape[1], send, recv).start()
        token[...] = jnp.zeros_like(token)

    res = pl.pallas_call(
        body, name=name,
        in_specs=[_IN_HBM] * n + [_HBM],
        out_specs=[_SEM, _SEM] + [_IN_HBM] * n + [pl.BlockSpec(memory_space=pltpu.VMEM)],
        out_shape=[pltpu.SemaphoreType.DMA((3 * n,)), pltpu.SemaphoreType.DMA((3 * n,))]
        + [pltpu.HBM(b.shape, b.dtype) for b in bufs] + [jax.ShapeDtypeStruct((SUBLANES, 128), F32)],
        input_output_aliases={i: 2 + i for i in range(n)},
        compiler_params=_SPLIT_COPY_PARAMS,
    )(*[_in_hbm(b) for b in bufs], after)
    return res[0], res[1], list(res[2:2 + n]), res[-1]


def _gather_wait(name, bufs, send, recv, after):
    n = len(bufs)

    def body(*refs):
        ins = refs[:n]
        send_ref, recv_ref = refs[n], refs[n + 1]
        x, y, c, k, chips, sib = _place()
        for i in range(n):
            for j, chip_xy in enumerate(chips):
                rows = bufs[i].shape[1]
                _gather_copy(ins[i], i, j, chip_xy, c, k, rows, send_ref, recv_ref).wait_send()
                _gather_copy(ins[i], i, j, chip_xy, c, 2 * chip_xy[0] + chip_xy[1], rows, send_ref, recv_ref).wait_recv()

    return pl.pallas_call(
        body, name=name,
        in_specs=[_IN_HBM] * n + [_SEM, _SEM, _HBM],
        out_specs=[_IN_HBM] * n,
        out_shape=[pltpu.HBM(b.shape, b.dtype) for b in bufs],
        input_output_aliases={i: i for i in range(n)},
        compiler_params=_SPLIT_COPY_PARAMS,
    )(*bufs, send, recv, after)


def _gather_forward(name, bufs):
    n = len(bufs)

    def body(*refs):
        outs = refs[n:2 * n]
        send, recv = refs[2 * n:]
        x, y, c, k, chips, sib = _place()
        started = []
        for i in range(n):
            rows = bufs[i].shape[1]
            for j, (cx, cy) in enumerate(chips):
                blk = outs[i].at[2 * cx + cy, _half(rows, c)]
                fw = pltpu.make_async_remote_copy(
                    src_ref=blk, dst_ref=blk, send_sem=send.at[i, j], recv_sem=recv.at[i, j],
                    device_id=sib, device_id_type=MESH)
                fw.start()
                started.append(fw)
        for i in range(n):
            rows = bufs[i].shape[1]
            for j, (cx, cy) in enumerate(chips):
                blk = outs[i].at[2 * cx + cy, _half(rows, 1 - c)]
                pltpu.make_async_remote_copy(
                    src_ref=blk, dst_ref=blk, send_sem=send.at[i, j], recv_sem=recv.at[i, j],
                    device_id=sib, device_id_type=MESH).wait_recv()
        for cp in started:
            cp.wait_send()

    return pl.pallas_call(
        body, name=name, in_specs=[_HBM] * n, out_specs=[_HBM] * n,
        out_shape=[jax.ShapeDtypeStruct(b.shape, b.dtype) for b in bufs],
        input_output_aliases={i: i for i in range(n)},
        scratch_shapes=[pltpu.SemaphoreType.DMA((n, 3))] * 2,
    )(*bufs)


def _swap_halves(name, parts):
    n = len(parts)

    def body(*refs):
        srcs, outs = refs[:n], refs[n:2 * n]
        send, recv = refs[2 * n:]
        x, y, c, k, chips, sib = _place()
        cps = []
        for i in range(n):
            rows = parts[i].shape[1]
            cp = pltpu.make_async_remote_copy(
                src_ref=srcs[i].at[:, _half(rows, 1 - c)], dst_ref=outs[i],
                send_sem=send.at[i], recv_sem=recv.at[i], device_id=sib, device_id_type=MESH)
            cp.start()
            cps.append(cp)
        for cp in cps:
            cp.wait()

    return pl.pallas_call(
        body, name=name, in_specs=[_HBM] * n, out_specs=[_HBM] * n,
        out_shape=[jax.ShapeDtypeStruct((p.shape[0], p.shape[1] // 2, p.shape[2]), p.dtype) for p in parts],
        scratch_shapes=[pltpu.SemaphoreType.DMA((n,))] * 2,
    )(*parts)


def _scatter_copy(src_ref, land_ref, i, j, chip_xy, c, send, recv):
    return pltpu.make_async_remote_copy(
        src_ref=src_ref.at[2 * chip_xy[0] + chip_xy[1]], dst_ref=land_ref.at[j],
        send_sem=send.at[3 * i + j], recv_sem=recv.at[3 * i + j], device_id=(*chip_xy, c), device_id_type=MESH)


def _scatter_start(name, sums):
    n = len(sums)
    lands = [lax.empty((3,) + s.shape[1:], s.dtype) for s in sums]

    def body(*refs):
        srcs, lnds = refs[:n], refs[n:2 * n]
        send, recv = refs[2 * n], refs[2 * n + 1]
        token = refs[-1]
        x, y, c, k, chips, sib = _place()
        for i in range(n):
            for j, chip_xy in enumerate(chips):
                _scatter_copy(srcs[i], lnds[i], i, j, chip_xy, c, send, recv).start()
        token[...] = jnp.zeros_like(token)

    res = pl.pallas_call(
        body, name=name,
        in_specs=[_IN_HBM] * (2 * n),
        out_specs=[_SEM, _SEM] + [_IN_HBM] * (2 * n) + [pl.BlockSpec(memory_space=pltpu.VMEM)],
        out_shape=[pltpu.SemaphoreType.DMA((3 * n,)), pltpu.SemaphoreType.DMA((3 * n,))]
        + [pltpu.HBM(a.shape, a.dtype) for a in list(sums) + lands] + [jax.ShapeDtypeStruct((SUBLANES, 128), F32)],
        input_output_aliases={i: 2 + i for i in range(2 * n)},
        compiler_params=_SPLIT_COPY_PARAMS,
    )(*[_in_hbm(a) for a in list(sums) + lands])
    return res[0], res[1], list(res[2:2 + n]), list(res[2 + n:2 + 2 * n]), res[-1]


def _scatter_wait(name, sums, lands, send, recv, afters):
    n = len(sums)

    def body(*refs):
        srcs, lnds = refs[:n], refs[n:2 * n]
        send_ref, recv_ref = refs[2 * n], refs[2 * n + 1]
        x, y, c, k, chips, sib = _place()
        for i in range(n):
            for j, chip_xy in enumerate(chips):
                cp = _scatter_copy(srcs[i], lnds[i], i, j, chip_xy, c, send_ref, recv_ref)
                cp.wait_send()
                cp.wait_recv()

    res = pl.pallas_call(
        body, name=name,
        in_specs=[_IN_HBM] * (2 * n) + [_SEM, _SEM] + [_HBM] * len(afters),
        out_specs=[_IN_HBM] * (2 * n),
        out_shape=[pltpu.HBM(a.shape, a.dtype) for a in list(sums) + list(lands)],
        input_output_aliases={i: i for i in range(2 * n)},
        compiler_params=_SPLIT_COPY_PARAMS,
    )(*sums, *lands, send, recv, *afters)
    return list(res[:n]), list(res[n:])


def _join_halves(name, fulls):
    n = len(fulls)

    def body(*refs):
        outs = refs[n:2 * n]
        send, recv = refs[2 * n:]
        x, y, c, k, chips, sib = _place()
        cps = []
        for i in range(n):
            blk = outs[i].at[_half(fulls[i].shape[0], c)]
            cp = pltpu.make_async_remote_copy(
                src_ref=blk, dst_ref=blk, send_sem=send.at[i], recv_sem=recv.at[i],
                device_id=sib, device_id_type=MESH)
            cp.start()
            cps.append(cp)
        for i in range(n):
            blk = outs[i].at[_half(fulls[i].shape[0], 1 - c)]
            pltpu.make_async_remote_copy(
                src_ref=blk, dst_ref=blk, send_sem=send.at[i], recv_sem=recv.at[i],
                device_id=sib, device_id_type=MESH).wait_recv()
        for cp in cps:
            cp.wait_send()

    return pl.pallas_call(
        body, name=name, in_specs=[_HBM] * n, out_specs=[_HBM] * n,
        out_shape=[jax.ShapeDtypeStruct(f.shape, f.dtype) for f in fulls],
        input_output_aliases={i: i for i in range(n)},
        scratch_shapes=[pltpu.SemaphoreType.DMA((n,))] * 2,
    )(*fulls)


def _sum_over_devices(name, buf, loss_row, afters):
    R, D = buf.shape
    n_after = len(afters)

    def body(x_ref, *rest):
        all_ref, tot_ref, loss_ref, send_sems, recv_sems, local_sem = rest[n_after:]
        x, y, c, k, chips, sib = _place()
        me = (x, y, c)

        def block(px, py, pc):
            return all_ref.at[4 * px + 2 * py + pc]

        def copy(kk, blk, to, src=None):
            return pltpu.make_async_remote_copy(
                src_ref=block(*blk) if src is None else src, dst_ref=block(*blk),
                send_sem=send_sems.at[kk], recv_sem=recv_sems.at[kk], device_id=to, device_id_type=MESH)

        mine = pltpu.make_async_copy(x_ref, block(*me), local_sem)
        mine.start()
        first = [copy(0, me, sib, src=x_ref)]
        first += [copy(1 + j, me, (*chip, c), src=x_ref) for j, chip in enumerate(chips)]
        for cp in first:
            cp.start()
        passed = [copy(4 + j, (*chip, c), sib) for j, chip in enumerate(chips)]
        for j, chip in enumerate(chips):
            copy(1 + j, (*chip, c), me).wait_recv()
            passed[j].start()
        copy(0, sib, me).wait_recv()
        for j, chip in enumerate(chips):
            copy(4 + j, (*chip, 1 - c), me).wait_recv()
        for cp in first + passed:
            cp.wait_send()
        mine.wait()
        rc = _tile(R, 32)
        for r0 in range(0, R, rc):
            tot = all_ref[0, r0:r0 + rc, :]
            for d in range(1, N_DEVICES):
                tot = tot + all_ref[d, r0:r0 + rc, :]
            tot_ref[r0:r0 + rc, :] = tot
        loss = 0.5 * jnp.sum(tot_ref[loss_row:loss_row + 1, :]) / D
        loss_ref[...] = jnp.full(loss_ref.shape, loss, F32)

    vm = pl.BlockSpec(memory_space=pltpu.VMEM)
    return pl.pallas_call(
        body, name=name, in_specs=[vm] + [_HBM] * n_after, out_specs=[vm, vm, vm],
        out_shape=[jax.ShapeDtypeStruct((N_DEVICES, R, D), F32), jax.ShapeDtypeStruct((R, D), F32),
                   jax.ShapeDtypeStruct((SUBLANES, 128), F32)],
        scratch_shapes=[pltpu.SemaphoreType.DMA((7,)), pltpu.SemaphoreType.DMA((7,)), pltpu.SemaphoreType.DMA],
        compiler_params=pltpu.CompilerParams(vmem_limit_bytes=V7X_VMEM_LIMIT_BYTES),
    )(buf, *afters)[1:]


def _add_my_half(name, part, got, pos):
    S, R, C = part.shape
    R2 = R // 2
    tr = _tile(R2, 512)
    q = R2 // tr

    def body(x_ref, y_ref, c_ref, p_ref, g_ref, o_ref):
        o_ref[...] = (p_ref[...].astype(F32) + g_ref[...].astype(F32)).astype(o_ref.dtype)

    return pl.pallas_call(
        body, name=name,
        grid_spec=pltpu.PrefetchScalarGridSpec(
            num_scalar_prefetch=3, grid=(S, q),
            in_specs=[pl.BlockSpec((None, tr, C), lambda s, r, xr, yr, cr: (s, cr[0] * q + r, 0)),
                      pl.BlockSpec((None, tr, C), lambda s, r, xr, yr, cr: (s, r, 0))],
            out_specs=pl.BlockSpec((None, tr, C), lambda s, r, xr, yr, cr: (s, r, 0))),
        out_shape=jax.ShapeDtypeStruct((S, R2, C), BF16),
        compiler_params=_cparams("parallel", "parallel"),
    )(*pos, part, got)


def _add_owner(name, sums, got, pos):
    _, R2, C = sums.shape
    tr = _tile(R2, 512)
    q = R2 // tr

    def body(x_ref, y_ref, c_ref, s_ref, g_ref, o_ref):
        acc = s_ref[...].astype(F32)
        for j in range(3):
            acc = acc + g_ref[j].astype(F32)
        o_ref[...] = acc

    return pl.pallas_call(
        body, name=name,
        grid_spec=pltpu.PrefetchScalarGridSpec(
            num_scalar_prefetch=3, grid=(q,),
            in_specs=[pl.BlockSpec((None, tr, C), lambda r, xr, yr, cr: (2 * xr[0] + yr[0], r, 0)),
                      pl.BlockSpec((3, tr, C), lambda r, xr, yr, cr: (0, r, 0))],
            out_specs=pl.BlockSpec((tr, C), lambda r, xr, yr, cr: (cr[0] * q + r, 0))),
        out_shape=jax.ShapeDtypeStruct((2 * R2, C), F32),
        compiler_params=_cparams("parallel"),
    )(*pos, sums, got)


def _adamw(name, w, m, v, g):
    R, C = w.shape
    tr = SUBLANES
    while 2 * tr * C <= ADAMW_TILE_ELEMS:
        tr *= 2
    tr = _tile(R, tr)
    bc1 = 1.0 - ADAM_B1 ** ADAM_STEP
    bc2 = 1.0 - ADAM_B2 ** ADAM_STEP

    def body(w_ref, m_ref, v_ref, g_ref, go_ref, d_ref, mo_ref, vo_ref):
        gg = g_ref[...]
        m2 = ADAM_B1 * m_ref[...] + (1.0 - ADAM_B1) * gg
        v2 = ADAM_B2 * v_ref[...] + (1.0 - ADAM_B2) * (gg * gg)
        go_ref[...] = gg
        mo_ref[...] = m2
        vo_ref[...] = v2
        d_ref[...] = -ADAM_LR * ((m2 / bc1) / (jnp.sqrt(v2 / bc2) + ADAM_EPS) + ADAM_WD * w_ref[...])

    spec = pl.BlockSpec((tr, C), lambda r: (r, 0))
    return pl.pallas_call(
        body, name=name, grid=(R // tr,), in_specs=[spec] * 4, out_specs=[spec] * 4,
        out_shape=[jax.ShapeDtypeStruct((R, C), F32)] * 4,
        compiler_params=_cparams("parallel"),
    )(w, m, v, g)


def _adamw_slab(name, w, m, v, g, layer, prev):
    L, R, C = w.shape
    tr = SUBLANES
    while 2 * tr * C <= ADAMW_TILE_ELEMS:
        tr *= 2
    tr = _tile(R, tr)
    bc1 = 1.0 - ADAM_B1 ** ADAM_STEP
    bc2 = 1.0 - ADAM_B2 ** ADAM_STEP

    def body(w_ref, m_ref, v_ref, g_ref, *rest):
        go_ref, d_ref, mo_ref, vo_ref = rest[-4:]
        gg = g_ref[...]
        m2 = ADAM_B1 * m_ref[...] + (1.0 - ADAM_B1) * gg
        v2 = ADAM_B2 * v_ref[...] + (1.0 - ADAM_B2) * (gg * gg)
        go_ref[...] = gg
        mo_ref[...] = m2
        vo_ref[...] = v2
        d_ref[...] = -ADAM_LR * ((m2 / bc1) / (jnp.sqrt(v2 / bc2) + ADAM_EPS) + ADAM_WD * w_ref[...])

    slab = pl.BlockSpec((None, tr, C), lambda r: (layer, r, 0))
    n_prev = 0 if prev is None else 4
    return pl.pallas_call(
        body, name=name, grid=(R // tr,),
        in_specs=[slab] * 3 + [pl.BlockSpec((tr, C), lambda r: (r, 0))] + [_HBM] * n_prev,
        out_specs=[slab] * 4,
        out_shape=[jax.ShapeDtypeStruct((L, R, C), F32)] * 4,
        input_output_aliases={4 + i: i for i in range(n_prev)},
        compiler_params=_cparams("parallel"),
    )(w, m, v, g, *(prev or ()))


def _reduce_begin(tag, parts, pos):
    got = _swap_halves(f"rs_swap_{tag}", parts)
    sums = [_add_my_half(f"rs_add2_{tag}_{i}", p, g, pos) for i, (p, g) in enumerate(zip(parts, got))]
    return _scatter_start(f"rs_scatter_start_{tag}", sums)


def _reduce_end(tag, started, pos, afters):
    send, recv, sums, lands, _ = started
    sums, lands = _scatter_wait(f"rs_scatter_wait_{tag}", sums, lands, send, recv, afters)
    fulls = [_add_owner(f"rs_add4_{tag}_{i}", s, q, pos) for i, (s, q) in enumerate(zip(sums, lands))]
    return _join_halves(f"rs_join_{tag}", fulls)


def _pad_rows(a):
    r = (-a.shape[0]) % SUBLANES
    return jnp.pad(a, ((0, r), (0, 0))) if r else a


def kernel(x, p, norm_mix, norm_mlp, norm_ple, cf_w_pw1, cf_b_pw1, cf_w_dw, cf_b_dw, cf_norm, cf_w_pw2, cf_b_pw2, sc_w_in, sc_w_conv, sc_w_out, mlp_w1, mlp_w2, ple_w_proj, ple_w_gate, norm_final, loss_target, m_norm_mix, m_norm_mlp, m_norm_ple, m_cf_w_pw1, m_cf_b_pw1, m_cf_w_dw, m_cf_b_dw, m_cf_norm, m_cf_w_pw2, m_cf_b_pw2, m_sc_w_in, m_sc_w_conv, m_sc_w_out, m_mlp_w1, m_mlp_w2, m_ple_w_proj, m_ple_w_gate, m_norm_final, v_norm_mix, v_norm_mlp, v_norm_ple, v_cf_w_pw1, v_cf_b_pw1, v_cf_w_dw, v_cf_b_dw, v_cf_norm, v_cf_w_pw2, v_cf_b_pw2, v_sc_w_in, v_sc_w_conv, v_sc_w_out, v_mlp_w1, v_mlp_w2, v_ple_w_proj, v_ple_w_gate, v_norm_final):
    T, D = x.shape[1], x.shape[2]
    KA, KB = cf_w_dw.shape[1], sc_w_conv.shape[1]
    chip = (2 * lax.axis_index("x") + lax.axis_index("y")).astype(jnp.int32)
    pos = tuple(lax.axis_index(ax).astype(jnp.int32).reshape(1) for ax in ("x", "y", "c"))

    params = dict(norm_mix=norm_mix, norm_mlp=norm_mlp, norm_ple=norm_ple, cf_w_pw1=cf_w_pw1, cf_b_pw1=cf_b_pw1,
                  cf_w_dw=cf_w_dw, cf_b_dw=cf_b_dw, cf_norm=cf_norm, cf_w_pw2=cf_w_pw2, cf_b_pw2=cf_b_pw2,
                  sc_w_in=sc_w_in, sc_w_conv=sc_w_conv, sc_w_out=sc_w_out, mlp_w1=mlp_w1, mlp_w2=mlp_w2,
                  ple_w_proj=ple_w_proj, ple_w_gate=ple_w_gate, norm_final=norm_final)
    mom1 = dict(norm_mix=m_norm_mix, norm_mlp=m_norm_mlp, norm_ple=m_norm_ple, cf_w_pw1=m_cf_w_pw1,
                cf_b_pw1=m_cf_b_pw1, cf_w_dw=m_cf_w_dw, cf_b_dw=m_cf_b_dw, cf_norm=m_cf_norm, cf_w_pw2=m_cf_w_pw2,
                cf_b_pw2=m_cf_b_pw2, sc_w_in=m_sc_w_in, sc_w_conv=m_sc_w_conv, sc_w_out=m_sc_w_out,
                mlp_w1=m_mlp_w1, mlp_w2=m_mlp_w2, ple_w_proj=m_ple_w_proj, ple_w_gate=m_ple_w_gate,
                norm_final=m_norm_final)
    mom2 = dict(norm_mix=v_norm_mix, norm_mlp=v_norm_mlp, norm_ple=v_norm_ple, cf_w_pw1=v_cf_w_pw1,
                cf_b_pw1=v_cf_b_pw1, cf_w_dw=v_cf_w_dw, cf_b_dw=v_cf_b_dw, cf_norm=v_cf_norm, cf_w_pw2=v_cf_w_pw2,
                cf_b_pw2=v_cf_b_pw2, sc_w_in=v_sc_w_in, sc_w_conv=v_sc_w_conv, sc_w_out=v_sc_w_out,
                mlp_w1=v_mlp_w1, mlp_w2=v_mlp_w2, ple_w_proj=v_ple_w_proj, ple_w_gate=v_ple_w_gate,
                norm_final=v_norm_final)

    big = ("cf_w_pw1", "cf_w_pw2", "sc_w_in", "sc_w_out", "mlp_w1", "mlp_w2", "ple_w_proj", "ple_w_gate")
    row_sharded = ("cf_w_pw2", "sc_w_out", "mlp_w2", "ple_w_gate")

    def layer_names(i):
        return (["cf_w_pw1", "cf_w_pw2"] if i % 2 == 0 else ["sc_w_in", "sc_w_out"]) + \
            ["mlp_w1", "mlp_w2", "ple_w_proj", "ple_w_gate"]

    def layer_index(i, name):
        return i if name.startswith(("mlp", "ple")) else i // 2

    def gather_begin(i, after):
        bufs = [_cast_place(f"place_{nm}_{i}", params[nm], layer_index(i, nm), pos) for nm in layer_names(i)]
        return _gather_start(f"gather_start_{i}", bufs, after)

    def gather_end(i, started, after):
        send, recv, bufs, _ = started
        bufs = _gather_wait(f"gather_wait_{i}", bufs, send, recv, after)
        bufs = _gather_forward(f"gather_fwd_{i}", bufs)
        return {nm: g4.reshape(1, N_SHARDS * g4.shape[1], g4.shape[2]) if nm in row_sharded else g4
                for nm, g4 in zip(layer_names(i), bufs)}

    conv_small = jnp.concatenate([_pad_rows(cf_w_dw[j]) for j in range(cf_w_dw.shape[0])]
                                 + [_pad_rows(sc_w_conv[j]) for j in range(sc_w_conv.shape[0])], axis=0)
    conv_shards = _gather_shards("gather_conv_w", [(conv_small, None)])[0]
    conv_all = jnp.transpose(conv_shards, (1, 0, 2)).reshape(conv_small.shape[0], D)
    ka_pad = KA + (-KA) % SUBLANES
    kb_pad = KB + (-KB) % SUBLANES
    w_dw_full = [conv_all[j * ka_pad:j * ka_pad + KA] for j in range(cf_w_dw.shape[0])]
    off = cf_w_dw.shape[0] * ka_pad
    w_conv_full = [conv_all[off + j * kb_pad:off + j * kb_pad + KB] for j in range(sc_w_conv.shape[0])]

    def vec(a):
        return a.reshape(1, -1)

    ident = lambda acc: (acc,)

    h = x[0]
    saved = []
    W = [gather_end(0, gather_begin(0, conv_shards), h)]
    for i in range(DEPTH):
        j = i // 2
        wl = W[i]
        s = dict(h=h)
        g_mix = vec(norm_mix[i])
        if i + 1 < DEPTH:
            nxt = gather_begin(i + 1, wl["mlp_w1"])
            g_mix = g_mix + nxt[3][0, 0]
        s["u"] = _rms_fwd(f"rms_mix_{i}", h, g_mix)
        if i % 2 == 0:
            s["a"] = _mm_nn(f"cf_pw1_{i}", s["u"], wl["cf_w_pw1"], lambda acc, b: (acc + b,), [BF16],
                            extras=[(vec(cf_b_pw1[j]), "n")])[0]
            s["v2"], s["v4"] = _cf_conv_fwd(f"cf_conv_{i}", s["a"], w_dw_full[j], vec(cf_b_dw[j]), vec(cf_norm[j]))
            h1 = _mm_nn(f"cf_pw2_{i}", s["v4"], wl["cf_w_pw2"], lambda acc, b, r: (r + (acc + b),), [F32],
                        extras=[(vec(cf_b_pw2[j]), "n"), (h, "mn")])[0]
        else:
            s["bcv"] = _mm_nn(f"sc_in_{i}", s["u"], wl["sc_w_in"], ident, [BF16])[0]
            s["y"] = _sc_conv_fwd(f"sc_conv_{i}", s["bcv"], w_conv_full[j])[0]
            h1 = _mm_nn(f"sc_out_{i}", s["y"], wl["sc_w_out"], lambda acc, r: (r + acc,), [F32],
                        extras=[(h, "mn")])[0]
        s["h1"] = h1
        s["u2"] = _rms_fwd(f"rms_mlp_{i}", h1, vec(norm_mlp[i]))
        s["z"], s["hd"] = _mm_nn(f"mlp_w1_{i}", s["u2"], wl["mlp_w1"],
                                 lambda acc: (acc, jnp.square(jnp.maximum(acc, 0.0))), [BF16, BF16])
        h2 = _mm_nn(f"mlp_w2_{i}", s["hd"], wl["mlp_w2"], lambda acc, r: (r + acc,), [F32], extras=[(h1, "mn")])[0]
        s["h2"] = h2
        s["n3"] = _rms_fwd(f"rms_ple_{i}", h2, vec(norm_ple[i]))
        s["p"] = p[i, 0]
        s["e"] = _mm_nn(f"ple_proj_{i}", s["p"], wl["ple_w_proj"], ident, [BF16])[0]
        h, s["q"] = _mm_nn(f"ple_gate_{i}", s["n3"], wl["ple_w_gate"],
                           lambda acc, r, e: (r + _sigmoid(acc) * e.astype(F32), acc), [F32, BF16],
                           extras=[(h2, "mn"), (s["e"], "mn")])
        saved.append(s)
        if i + 1 < DEPTH:
            W.append(gather_end(i + 1, nxt, h))

    dh, dh16, dg_final, loss_cols = _loss_bwd("loss_bwd", h, vec(norm_final), loss_target[0])
    small = {"norm_final": dg_final, "loss": loss_cols}
    adam = {nm: None for nm in big}

    def finish_layer(i, started, afters):
        for nm, g in zip(layer_names(i), _reduce_end(f"{i}", started, pos, afters)):
            l = layer_index(i, nm)
            adam[nm] = _adamw_slab(f"adamw_{nm}_{l}", params[nm], mom1[nm], mom2[nm], g, l, adam[nm])

    pending = None
    for i in reversed(range(DEPTH)):
        j = i // 2
        wl, s = W[i], saved[i]
        dq, de = _ple_elem_bwd(f"ple_elem_bwd_{i}", dh, s["q"], s["e"])
        d_proj = _mm_tn(f"ple_proj_dw_{i}", s["p"], de, N_SHARDS)
        d_gate = _mm_tn(f"ple_gate_dw_{i}", s["n3"], dq, 1)
        dn3 = _mm_nt(f"ple_gate_dx_{i}", dq, wl["ple_w_gate"], ident, [F32])[0]
        dh, dh16, small[f"norm_ple_{i}"] = _rms_bwd(f"rms_ple_bwd_{i}", s["h2"], vec(norm_ple[i]), dn3, dh)

        d_w2 = _mm_tn(f"mlp_w2_dw_{i}", s["hd"], dh16, 1)
        dz = _mm_nt(f"mlp_w2_dx_{i}", dh16, wl["mlp_w2"],
                    lambda acc, z: (acc * (2.0 * jnp.maximum(z.astype(F32), 0.0)),), [BF16], extras=[s["z"]])[0]
        d_w1 = _mm_tn(f"mlp_w1_dw_{i}", s["u2"], dz, N_SHARDS)
        du2 = _mm_nt(f"mlp_w1_dx_{i}", dz, wl["mlp_w1"], ident, [F32])[0]
        if i % 2 == 0:
            dh, dh16, small[f"norm_mlp_{i}"], small[f"cf_b_pw2_{j}"] = _rms_bwd(
                f"rms_mlp_bwd_{i}", s["h1"], vec(norm_mlp[i]), du2, dh, want_colsum=True)
            d_mix_out = _mm_tn(f"cf_pw2_dw_{i}", s["v4"], dh16, 1)
            dv4 = _mm_nt(f"cf_pw2_dx_{i}", dh16, wl["cf_w_pw2"], ident, [F32])[0]
            dv2, small[f"cf_norm_{j}"], small[f"cf_b_dw_{j}"] = _cf_norm_bwd(
                f"cf_norm_bwd_{i}", s["v2"], vec(cf_norm[j]), dv4)
            da, small[f"cf_w_dw_{j}"], db1 = _cf_conv_bwd(f"cf_conv_bwd_{i}", dv2, s["a"], w_dw_full[j])
            small[f"cf_b_pw1_{j}"] = db1.reshape(2, D)
            d_mix_in = _mm_tn(f"cf_pw1_dw_{i}", s["u"], da, N_SHARDS)
            du = _mm_nt(f"cf_pw1_dx_{i}", da, wl["cf_w_pw1"], ident, [F32])[0]
        else:
            dh, dh16, small[f"norm_mlp_{i}"] = _rms_bwd(f"rms_mlp_bwd_{i}", s["h1"], vec(norm_mlp[i]), du2, dh)
            d_mix_out = _mm_tn(f"sc_out_dw_{i}", s["y"], dh16, 1)
            dy = _mm_nt(f"sc_out_dx_{i}", dh16, wl["sc_w_out"], ident, [F32])[0]
            da, small[f"sc_w_conv_{j}"] = _sc_conv_bwd(f"sc_conv_bwd_{i}", dy, s["bcv"], w_conv_full[j])
            d_mix_in = _mm_tn(f"sc_in_dw_{i}", s["u"], da, N_SHARDS)
            du = _mm_nt(f"sc_in_dx_{i}", da, wl["sc_w_in"], ident, [F32])[0]

        parts = [d_mix_in, d_mix_out, d_w1, d_w2, d_proj, d_gate]
        parts = [pt.reshape(N_SHARDS, pt.shape[1] // N_SHARDS, pt.shape[2]) if nm in row_sharded else pt
                 for nm, pt in zip(layer_names(i), parts)]
        started = _reduce_begin(f"{i}", parts, pos)
        token = started[4]
        dh, dh16, small[f"norm_mix_{i}"] = _rms_bwd(f"rms_mix_bwd_{i}", s["h"], vec(norm_mix[i]) + token[0, 0], du, dh)
        if pending is not None:
            finish_layer(i + 1, pending, [token])
        pending = started
    grad_x = dh.reshape(x.shape)

    order = sorted(small)
    pieces, where, row = [], {}, 0
    for nm in order:
        pc = _pad_rows(small[nm])
        where[nm] = (row, small[nm].shape[0])
        row += pc.shape[0]
        pieces.append(pc)
    updated = [res[3] for res in adam.values() if res is not None]
    total, loss_tile = _sum_over_devices("small_allsum", jnp.concatenate(pieces, axis=0), where["loss"][0], updated)
    loss = loss_tile[0, 0]
    finish_layer(0, pending, [total] + updated)

    def small_sum(nm):
        r0, nr = where[nm]
        return total[r0:r0 + nr]

    def my_cols(a):
        return lax.dynamic_slice_in_dim(a, chip * (D // N_SHARDS), D // N_SHARDS, axis=1)

    g_small = {
        "norm_mix": jnp.concatenate([small_sum(f"norm_mix_{i}") for i in range(DEPTH)], axis=0),
        "norm_mlp": jnp.concatenate([small_sum(f"norm_mlp_{i}") for i in range(DEPTH)], axis=0),
        "norm_ple": jnp.concatenate([small_sum(f"norm_ple_{i}") for i in range(DEPTH)], axis=0),
        "cf_b_pw1": jnp.stack([small_sum(f"cf_b_pw1_{j}").reshape(2 * D) for j in range(DEPTH // 2)]),
        "cf_w_dw": jnp.stack([my_cols(small_sum(f"cf_w_dw_{j}")) for j in range(DEPTH // 2)]),
        "cf_b_dw": jnp.concatenate([small_sum(f"cf_b_dw_{j}") for j in range(DEPTH // 2)], axis=0),
        "cf_norm": jnp.concatenate([small_sum(f"cf_norm_{j}") for j in range(DEPTH // 2)], axis=0),
        "cf_b_pw2": jnp.concatenate([small_sum(f"cf_b_pw2_{j}") for j in range(DEPTH // 2)], axis=0),
        "sc_w_conv": jnp.stack([my_cols(small_sum(f"sc_w_conv_{j}")) for j in range(DEPTH // 2)]),
        "norm_final": small_sum("norm_final").reshape(D),
    }

    names_out = ["norm_mix", "norm_mlp", "norm_ple", "cf_w_pw1", "cf_b_pw1", "cf_w_dw", "cf_b_dw", "cf_norm",
                 "cf_w_pw2", "cf_b_pw2", "sc_w_in", "sc_w_conv", "sc_w_out", "mlp_w1", "mlp_w2", "ple_w_proj",
                 "ple_w_gate", "norm_final"]
    grad, delta, new_m, new_v = {}, {}, {}, {}
    for nm in names_out:
        w = params[nm]
        if nm in big:
            grad[nm], delta[nm], new_m[nm], new_v[nm] = adam[nm]
            continue
        g = g_small[nm]
        cols = w.shape[-1] if w.ndim > 1 else w.shape[0]
        two_d = lambda a: a.reshape(-1, cols)
        res = _adamw(f"adamw_{nm}", two_d(w), two_d(mom1[nm]), two_d(mom2[nm]), two_d(g))
        grad[nm], delta[nm], new_m[nm], new_v[nm] = [r.reshape(w.shape) for r in res]

    return (loss, grad_x, *[grad[n] for n in names_out], *[delta[n] for n in names_out],
            *[new_m[n] for n in names_out], *[new_v[n] for n in names_out])
```

```python
import functools

import jax
import jax.numpy as jnp
from jax import lax
from jax.experimental import pallas as pl
from jax.experimental.pallas import tpu as pltpu

F32 = jnp.float32
BF16 = jnp.bfloat16

EPS = 1e-6
ADAM_LR = 0.001
ADAM_B1 = 0.9
ADAM_B2 = 0.999
ADAM_EPS = 1e-08
ADAM_WD = 0.01
ADAM_STEP = 10

DEPTH = 4
N_SHARDS = 4
N_DEVICES = 8
V7X_VMEM_LIMIT_BYTES = 56 * 1024 * 1024
SUBLANES = 8
MESH = pl.DeviceIdType.MESH

MM_TM = 1024
MM_TN = 1024
MM_TK = 2048
MM_TW_K = 1024
MM_TW_T = 4096
MM_TX_K = 1024
MM_TX_N = 2048
MM_TN_CROWDED = 512
ADAMW_TILE_ELEMS = 256 * 2048

CONV_ROW_CHUNK = 32
CONV_LANE_CHUNK = 512
CONV_TILE_ROWS = 128
ROW_TILE = 256


def _tile(dim, pref):
    if dim <= pref:
        return dim
    t = pref
    while dim % t:
        t //= 2
    return t


def _cparams(*sem):
    return pltpu.CompilerParams(dimension_semantics=sem, vmem_limit_bytes=V7X_VMEM_LIMIT_BYTES)


def _sigmoid(x):
    return 1.0 / (1.0 + jnp.exp(-x))


def _rms_r(x):
    return lax.rsqrt(jnp.mean(x * x, axis=-1, keepdims=True) + EPS)


def _mm_nn(name, a, b3, epilogue, out_dtypes, extras=()):
    M, K = a.shape
    S, Kb, Ns = b3.shape
    assert Kb == K
    N = S * Ns
    crowded = sum(kind == "mn" for _, kind in extras) > 1
    tm, tn, tk = _tile(M, MM_TM), _tile(Ns, MM_TN_CROWDED if crowded else MM_TN), _tile(K, MM_TK)
    per = Ns // tn
    nk = K // tk
    in_specs = [pl.BlockSpec((tm, tk), lambda i, j, k: (i, k)),
                pl.BlockSpec((None, tk, tn), lambda i, j, k: (j // per, k, j % per))]
    for _, kind in extras:
        if kind == "mn":
            in_specs.append(pl.BlockSpec((tm, tn), lambda i, j, k: (i, j)))
        else:
            in_specs.append(pl.BlockSpec((1, tn), lambda i, j, k: (0, j)))
    n_ex, n_o = len(extras), len(out_dtypes)

    def body(*refs):
        a_ref, b_ref = refs[:2]
        ex = refs[2:2 + n_ex]
        outs = refs[2 + n_ex:2 + n_ex + n_o]
        part = jnp.dot(a_ref[...].astype(BF16), b_ref[...], preferred_element_type=F32)

        def finish(acc):
            res = epilogue(acc, *[e[...] for e in ex])
            for r, o in zip(res, outs):
                o[...] = r.astype(o.dtype)

        if nk == 1:
            finish(part)
        else:
            acc_ref = refs[-1]
            k = pl.program_id(2)

            @pl.when(k == 0)
            def _():
                acc_ref[...] = part

            @pl.when(k > 0)
            def _():
                acc_ref[...] += part

            @pl.when(k == nk - 1)
            def _():
                finish(acc_ref[...])

    res = pl.pallas_call(
        body, name=name, grid=(M // tm, N // tn, nk),
        in_specs=in_specs,
        out_specs=[pl.BlockSpec((tm, tn), lambda i, j, k: (i, j)) for _ in out_dtypes],
        out_shape=[jax.ShapeDtypeStruct((M, N), dt) for dt in out_dtypes],
        scratch_shapes=[pltpu.VMEM((tm, tn), F32)] if nk > 1 else [],
        compiler_params=_cparams("parallel", "parallel", "arbitrary"),
    )(a, b3, *[e for e, _ in extras])
    return res


def _mm_nt(name, g, w3, epilogue, out_dtypes, extras=()):
    M, N = g.shape
    S, K, Ns = w3.shape
    assert S * Ns == N
    tm, tn, tkk = _tile(M, MM_TM), _tile(Ns, MM_TX_N), _tile(K, MM_TX_K)
    per = Ns // tn
    nn = N // tn
    n_ex, n_o = len(extras), len(out_dtypes)

    def body(*refs):
        g_ref, w_ref = refs[:2]
        ex = refs[2:2 + n_ex]
        outs = refs[2 + n_ex:2 + n_ex + n_o]
        part = lax.dot_general(g_ref[...].astype(BF16), w_ref[...], (((1,), (1,)), ((), ())),
                               preferred_element_type=F32)

        def finish(acc):
            res = epilogue(acc, *[e[...] for e in ex])
            for r, o in zip(res, outs):
                o[...] = r.astype(o.dtype)

        if nn == 1:
            finish(part)
        else:
            acc_ref = refs[-1]
            n = pl.program_id(2)

            @pl.when(n == 0)
            def _():
                acc_ref[...] = part

            @pl.when(n > 0)
            def _():
                acc_ref[...] += part

            @pl.when(n == nn - 1)
            def _():
                finish(acc_ref[...])

    return pl.pallas_call(
        body, name=name, grid=(M // tm, K // tkk, nn),
        in_specs=[pl.BlockSpec((tm, tn), lambda i, kk, n: (i, n)),
                  pl.BlockSpec((None, tkk, tn), lambda i, kk, n: (n // per, kk, n % per))]
        + [pl.BlockSpec((tm, tkk), lambda i, kk, n: (i, kk)) for _ in extras],
        out_specs=[pl.BlockSpec((tm, tkk), lambda i, kk, n: (i, kk)) for _ in out_dtypes],
        out_shape=[jax.ShapeDtypeStruct((M, K), dt) for dt in out_dtypes],
        scratch_shapes=[pltpu.VMEM((tm, tkk), F32)] if nn > 1 else [],
        compiler_params=_cparams("parallel", "parallel", "arbitrary"),
    )(g, w3, *extras)


def _mm_tn(name, a, g, n_shards):
    T, K = a.shape
    _, N = g.shape
    Ns = N // n_shards
    tk, tn, tt = _tile(K, MM_TW_K), _tile(Ns, MM_TN), _tile(T, MM_TW_T)
    per = Ns // tn
    nt = T // tt

    def body(a_ref, g_ref, o_ref, *scratch):
        part = lax.dot_general(a_ref[...].astype(BF16), g_ref[...].astype(BF16), (((0,), (0,)), ((), ())),
                               preferred_element_type=F32)
        if nt == 1:
            o_ref[...] = part.astype(o_ref.dtype)
            return
        acc_ref, = scratch
        t = pl.program_id(2)

        @pl.when(t == 0)
        def _():
            acc_ref[...] = part

        @pl.when(t > 0)
        def _():
            acc_ref[...] += part

        @pl.when(t == nt - 1)
        def _():
            o_ref[...] = acc_ref[...].astype(o_ref.dtype)

    return pl.pallas_call(
        body, name=name, grid=(K // tk, N // tn, nt),
        in_specs=[pl.BlockSpec((tt, tk), lambda i, j, t: (t, i)),
                  pl.BlockSpec((tt, tn), lambda i, j, t: (t, j))],
        out_specs=pl.BlockSpec((None, tk, tn), lambda i, j, t: (j // per, i, j % per)),
        out_shape=jax.ShapeDtypeStruct((n_shards, K, Ns), BF16),
        scratch_shapes=[pltpu.VMEM((tk, tn), F32)] if nt > 1 else [],
        compiler_params=_cparams("parallel", "parallel", "arbitrary"),
    )(a, g)


def _rowwise(name, fn, ins, outs, accs=(), scratch=(), tt=ROW_TILE):
    T = next(a.shape[0] for a, kind in ins if kind == "row")
    tt = _tile(T, tt)
    n = T // tt
    in_specs = []
    for a, kind in ins:
        w = a.shape[1]
        if kind == "row":
            in_specs.append(pl.BlockSpec((tt, w), lambda i: (i, 0)))
        elif kind == "vec":
            in_specs.append(pl.BlockSpec(a.shape, lambda i: (0, 0)))
        elif kind[0] == "prev":
            pad = kind[1]
            in_specs.append(pl.BlockSpec((pad, w), lambda i, q=tt // pad: (jnp.maximum(i * q - 1, 0), 0)))
        else:
            pad = kind[1]
            in_specs.append(pl.BlockSpec((pad, w), lambda i, q=tt // pad, last=T // pad - 1:
                                         (jnp.minimum((i + 1) * q, last), 0)))
    n_in, n_out, n_acc = len(ins), len(outs), len(accs)

    def body(*refs):
        i = pl.program_id(0)
        in_refs = refs[:n_in]
        out_refs = refs[n_in:n_in + n_out]
        acc_refs = refs[n_in + n_out:n_in + n_out + n_acc]
        scr = refs[n_in + n_out + n_acc:]
        if n_acc:
            @pl.when(i == 0)
            def _():
                for r in acc_refs:
                    r[...] = jnp.zeros_like(r)
        fn(i, n, in_refs, out_refs, acc_refs, scr)

    res = pl.pallas_call(
        body, name=name, grid=(n,),
        in_specs=in_specs,
        out_specs=[pl.BlockSpec((tt, w), lambda i: (i, 0)) for w, _ in outs]
        + [pl.BlockSpec((r, w), lambda i: (0, 0)) for r, w in accs],
        out_shape=[jax.ShapeDtypeStruct((T, w), dt) for w, dt in outs]
        + [jax.ShapeDtypeStruct((r, w), F32) for r, w in accs],
        scratch_shapes=list(scratch),
        compiler_params=_cparams("arbitrary"),
    )(*[a for a, _ in ins])
    return res


def _colsum(x):
    return jnp.sum(x, axis=0, keepdims=True)


def _rms_fwd(name, h, g):
    D = h.shape[1]

    def fn(i, n, ins, outs, accs, scr):
        x = ins[0][...]
        outs[0][...] = (x * _rms_r(x) * ins[1][...]).astype(BF16)

    return _rowwise(name, fn, [(h, "row"), (g, "vec")], [(D, BF16)])[0]


def _rms_bwd(name, h, g, du, dh_in, want_colsum=False):
    D = h.shape[1]

    def fn(i, n, ins, outs, accs, scr):
        x = ins[0][...]
        gg = ins[1][...]
        d = ins[2][...].astype(F32)
        r = _rms_r(x)
        xn = x * r
        t = d * gg
        dh = ins[3][...] + r * (t - xn * jnp.mean(t * xn, axis=-1, keepdims=True))
        outs[0][...] = dh
        outs[1][...] = dh.astype(BF16)
        accs[0][...] += _colsum(d * xn)
        if want_colsum:
            accs[1][...] += _colsum(dh)

    return _rowwise(name, fn, [(h, "row"), (g, "vec"), (du, "row"), (dh_in, "row")],
                    [(D, F32), (D, BF16)], accs=[(1, D)] * (2 if want_colsum else 1))


def _loss_bwd(name, h, g, tgt):
    D = h.shape[1]

    def fn(i, n, ins, outs, accs, scr):
        x = ins[0][...]
        gg = ins[1][...]
        r = _rms_r(x)
        xn = x * r
        err = xn * gg - ins[2][...]
        dy = err / D
        t = dy * gg
        dh = r * (t - xn * jnp.mean(t * xn, axis=-1, keepdims=True))
        outs[0][...] = dh
        outs[1][...] = dh.astype(BF16)
        accs[0][...] += _colsum(dy * xn)
        accs[1][...] += _colsum(err * err)

    return _rowwise(name, fn, [(h, "row"), (g, "vec"), (tgt, "row")], [(D, F32), (D, BF16)],
                    accs=[(1, D), (1, D)])


def _ple_elem_bwd(name, dh, q, e):
    D = dh.shape[1]

    def fn(i, n, ins, outs, accs, scr):
        d = ins[0][...]
        s = _sigmoid(ins[1][...].astype(F32))
        ee = ins[2][...].astype(F32)
        outs[0][...] = (d * ee * s * (1.0 - s)).astype(BF16)
        outs[1][...] = (d * s).astype(BF16)

    return _rowwise(name, fn, [(dh, "row"), (q, "row"), (e, "row")], [(D, BF16), (D, BF16)])


def _cf_norm_bwd(name, v2, g, dv4):
    D = v2.shape[1]

    def fn(i, n, ins, outs, accs, scr):
        x = ins[0][...]
        gg = ins[1][...]
        r = _rms_r(x)
        xn = x * r
        v3 = xn * gg
        s = _sigmoid(v3)
        dv3 = ins[2][...].astype(F32) * (s * (1.0 + v3 * (1.0 - s)))
        t = dv3 * gg
        dv2 = r * (t - xn * jnp.mean(t * xn, axis=-1, keepdims=True))
        outs[0][...] = dv2
        accs[0][...] += _colsum(dv3 * xn)
        accs[1][...] += _colsum(dv2)

    return _rowwise(name, fn, [(v2, "row"), (g, "vec"), (dv4, "row")], [(D, F32)], accs=[(1, D), (1, D)])


def _chunks(tt, width):
    cc = min(CONV_LANE_CHUNK, width)
    rc = min(CONV_ROW_CHUNK, tt)
    for c0 in range(0, width, cc):
        for r0 in range(0, tt, rc):
            yield r0, rc, c0, cc


def _n_shifts(n_taps):
    return min(SUBLANES - 1, n_taps - 1)


def _shifted_scratch(n_taps, rows, width):
    return pltpu.VMEM((_n_shifts(n_taps), rows, width), F32)


def _shift_window(win_ref, sh_ref, n_taps, sign):
    rows = win_ref.shape[0] - SUBLANES
    width = win_ref.shape[1]
    cc = min(CONV_LANE_CHUNK, width)
    for b in range(1, _n_shifts(n_taps) + 1):
        off = SUBLANES - b if sign < 0 else b
        for c0 in range(0, width, cc):
            sh_ref[b - 1, 0:rows, c0:c0 + cc] = win_ref[off:off + rows, c0:c0 + cc]


def _tap(win_ref, sh_ref, base, sign, s, r0, rc, c0, cc):
    a, b = divmod(s, SUBLANES)
    if b == 0:
        row = base + r0 + sign * SUBLANES * a
        return win_ref[row:row + rc, c0:c0 + cc]
    row = base + r0 - SUBLANES * (a + 1) if sign < 0 else base + r0 + SUBLANES * a
    return sh_ref[b - 1, row:row + rc, c0:c0 + cc]


def _fir(win_ref, sh_ref, w_ref, n_taps, base, sign, tt, width, emit):
    for r0, rc, c0, cc in _chunks(tt, width):
        acc = jnp.zeros((rc, cc), F32)
        for k in range(n_taps):
            acc = acc + w_ref[k:k + 1, c0:c0 + cc] * _tap(win_ref, sh_ref, base, sign, n_taps - 1 - k, r0, rc, c0, cc)
        emit(r0, rc, c0, cc, acc)


def _fir_wgrad(d_ref, win_ref, sh_ref, dw8_ref, n_taps, pad, tt, width):
    for c0 in range(0, width, min(CONV_LANE_CHUNK, width)):
        cc = min(CONV_LANE_CHUNK, width)
        rc = min(CONV_ROW_CHUNK, tt)
        for k in range(n_taps):
            acc = jnp.zeros((SUBLANES, cc), F32)
            for r0 in range(0, tt, rc):
                prod = d_ref[r0:r0 + rc, c0:c0 + cc] * _tap(win_ref, sh_ref, pad, -1, n_taps - 1 - k, r0, rc, c0, cc)
                for q in range(0, rc, SUBLANES):
                    acc = acc + prod[q:q + SUBLANES]
            dw8_ref[SUBLANES * k:SUBLANES * (k + 1), c0:c0 + cc] += acc


def _glu(blk, D):
    return blk[:, :D].astype(F32) * _sigmoid(blk[:, D:].astype(F32))


CF_PAD = 32
SC_PAD = 16


def _cf_conv_fwd(name, a, w_dw, b_dw, g_cf):
    T, D2 = a.shape
    D = D2 // 2
    K = w_dw.shape[0]
    tt = _tile(T, CONV_TILE_ROWS)

    def fn(i, n, ins, outs, accs, scr):
        a_ref, prev_ref, w_ref, b_ref, g_ref = ins
        win_ref, v2_ref, sh_ref = scr
        win_ref[0:CF_PAD, :] = jnp.where(i > 0, _glu(prev_ref[...], D), 0.0)
        win_ref[CF_PAD:CF_PAD + tt, :] = _glu(a_ref[...], D)
        _shift_window(win_ref, sh_ref, K, -1)

        def emit(r0, rc, c0, cc, acc):
            v2_ref[r0:r0 + rc, c0:c0 + cc] = acc + b_ref[:, c0:c0 + cc]

        _fir(win_ref, sh_ref, w_ref, K, CF_PAD, -1, tt, D, emit)
        v2 = v2_ref[...]
        v3 = v2 * _rms_r(v2) * g_ref[...]
        outs[0][...] = v2
        outs[1][...] = (v3 * _sigmoid(v3)).astype(BF16)

    return _rowwise(name, fn, [(a, "row"), (a, ("prev", CF_PAD)), (w_dw, "vec"), (b_dw, "vec"), (g_cf, "vec")],
                    [(D, F32), (D, BF16)],
                    scratch=[pltpu.VMEM((CF_PAD + tt, D), F32), pltpu.VMEM((tt, D), F32),
                             _shifted_scratch(K, CF_PAD + tt, D)], tt=tt)


def _cf_conv_bwd(name, dv2, a, w_dw):
    T, D2 = a.shape
    D = D2 // 2
    K = w_dw.shape[0]
    tt = _tile(T, CONV_TILE_ROWS)

    def fn(i, n, ins, outs, accs, scr):
        d_ref, dnext_ref, a_ref, prev_ref, w_ref = ins
        v1win_ref, dwin_ref, dv1_ref, dw8_ref, v1sh_ref, dsh_ref = scr

        @pl.when(i == 0)
        def _():
            dw8_ref[...] = jnp.zeros_like(dw8_ref)

        v1win_ref[0:CF_PAD, :] = jnp.where(i > 0, _glu(prev_ref[...], D), 0.0)
        v1win_ref[CF_PAD:CF_PAD + tt, :] = _glu(a_ref[...], D)
        dwin_ref[0:tt, :] = d_ref[...]
        dwin_ref[tt:tt + CF_PAD, :] = jnp.where(i < n - 1, dnext_ref[...], 0.0)
        _shift_window(v1win_ref, v1sh_ref, K, -1)
        _shift_window(dwin_ref, dsh_ref, K, 1)

        def emit(r0, rc, c0, cc, acc):
            dv1_ref[r0:r0 + rc, c0:c0 + cc] = acc

        _fir(dwin_ref, dsh_ref, w_ref, K, 0, 1, tt, D, emit)
        _fir_wgrad(d_ref, v1win_ref, v1sh_ref, dw8_ref, K, CF_PAD, tt, D)

        blk = a_ref[...]
        val = blk[:, :D].astype(F32)
        sg = _sigmoid(blk[:, D:].astype(F32))
        dv1 = dv1_ref[...]
        dval = dv1 * sg
        dgate = dv1 * val * sg * (1.0 - sg)
        outs[0][:, :D] = dval.astype(BF16)
        outs[0][:, D:] = dgate.astype(BF16)
        accs[1][:, :D] += _colsum(dval)
        accs[1][:, D:] += _colsum(dgate)

        @pl.when(i == n - 1)
        def _():
            for k in range(K):
                accs[0][k:k + 1, :] = _colsum(dw8_ref[SUBLANES * k:SUBLANES * (k + 1), :])

    return _rowwise(name, fn, [(dv2, "row"), (dv2, ("next", CF_PAD)), (a, "row"), (a, ("prev", CF_PAD)),
                               (w_dw, "vec")],
                    [(D2, BF16)], accs=[(K, D), (1, D2)],
                    scratch=[pltpu.VMEM((CF_PAD + tt, D), F32), pltpu.VMEM((tt + CF_PAD, D), F32),
                             pltpu.VMEM((tt, D), F32), pltpu.VMEM((SUBLANES * K, D), F32),
                             _shifted_scratch(K, CF_PAD + tt, D), _shifted_scratch(K, tt + CF_PAD, D)], tt=tt)


def _sc_conv_fwd(name, bcv, w_conv):
    T, D3 = bcv.shape
    D = D3 // 3
    K = w_conv.shape[0]
    tt = _tile(T, CONV_TILE_ROWS)

    def cv_of(blk):
        return blk[:, D:2 * D].astype(F32) * blk[:, 2 * D:].astype(F32)

    def fn(i, n, ins, outs, accs, scr):
        x_ref, prev_ref, w_ref = ins
        win_ref, cc_ref, sh_ref = scr
        win_ref[0:SC_PAD, :] = jnp.where(i > 0, cv_of(prev_ref[...]), 0.0)
        win_ref[SC_PAD:SC_PAD + tt, :] = cv_of(x_ref[...])
        _shift_window(win_ref, sh_ref, K, -1)

        def emit(r0, rc, c0, cw, acc):
            cc_ref[r0:r0 + rc, c0:c0 + cw] = acc

        _fir(win_ref, sh_ref, w_ref, K, SC_PAD, -1, tt, D, emit)
        outs[0][...] = (x_ref[:, :D].astype(F32) * cc_ref[...]).astype(BF16)

    return _rowwise(name, fn, [(bcv, "row"), (bcv, ("prev", SC_PAD)), (w_conv, "vec")], [(D, BF16)],
                    scratch=[pltpu.VMEM((SC_PAD + tt, D), F32), pltpu.VMEM((tt, D), F32),
                             _shifted_scratch(K, SC_PAD + tt, D)], tt=tt)


def _sc_conv_bwd(name, dy, bcv, w_conv):
    T, D3 = bcv.shape
    D = D3 // 3
    K = w_conv.shape[0]
    tt = _tile(T, CONV_TILE_ROWS)

    def cv_of(blk):
        return blk[:, D:2 * D].astype(F32) * blk[:, 2 * D:].astype(F32)

    def fn(i, n, ins, outs, accs, scr):
        dy_ref, dynext_ref, x_ref, prev_ref, next_ref, w_ref = ins
        cvwin_ref, dccwin_ref, tmp_ref, dw8_ref, cvsh_ref, dccsh_ref = scr

        @pl.when(i == 0)
        def _():
            dw8_ref[...] = jnp.zeros_like(dw8_ref)

        cvwin_ref[0:SC_PAD, :] = jnp.where(i > 0, cv_of(prev_ref[...]), 0.0)
        cvwin_ref[SC_PAD:SC_PAD + tt, :] = cv_of(x_ref[...])
        _shift_window(cvwin_ref, cvsh_ref, K, -1)

        def emit_cc(r0, rc, c0, cw, acc):
            tmp_ref[r0:r0 + rc, c0:c0 + cw] = acc

        _fir(cvwin_ref, cvsh_ref, w_ref, K, SC_PAD, -1, tt, D, emit_cc)
        dy_v = dy_ref[...].astype(F32)
        outs[0][:, :D] = (dy_v * tmp_ref[...]).astype(BF16)
        dccwin_ref[0:tt, :] = dy_v * x_ref[:, :D].astype(F32)
        dccwin_ref[tt:tt + SC_PAD, :] = jnp.where(
            i < n - 1, dynext_ref[...].astype(F32) * next_ref[:, :D].astype(F32), 0.0)

        _shift_window(dccwin_ref, dccsh_ref, K, 1)

        def emit_dcv(r0, rc, c0, cw, acc):
            tmp_ref[r0:r0 + rc, c0:c0 + cw] = acc

        _fir(dccwin_ref, dccsh_ref, w_ref, K, 0, 1, tt, D, emit_dcv)
        _fir_wgrad(dccwin_ref, cvwin_ref, cvsh_ref, dw8_ref, K, SC_PAD, tt, D)
        dcv = tmp_ref[...]
        outs[0][:, D:2 * D] = (dcv * x_ref[:, 2 * D:].astype(F32)).astype(BF16)
        outs[0][:, 2 * D:] = (dcv * x_ref[:, D:2 * D].astype(F32)).astype(BF16)

        @pl.when(i == n - 1)
        def _():
            for k in range(K):
                accs[0][k:k + 1, :] = _colsum(dw8_ref[SUBLANES * k:SUBLANES * (k + 1), :])

    return _rowwise(name, fn, [(dy, "row"), (dy, ("next", SC_PAD)), (bcv, "row"), (bcv, ("prev", SC_PAD)),
                               (bcv, ("next", SC_PAD)), (w_conv, "vec")],
                    [(D3, BF16)], accs=[(K, D)],
                    scratch=[pltpu.VMEM((SC_PAD + tt, D), F32), pltpu.VMEM((tt + SC_PAD, D), F32),
                             pltpu.VMEM((tt, D), F32), pltpu.VMEM((SUBLANES * K, D), F32),
                             _shifted_scratch(K, SC_PAD + tt, D), _shifted_scratch(K, tt + SC_PAD, D)], tt=tt)


def _place():
    x, y, c = lax.axis_index("x"), lax.axis_index("y"), lax.axis_index("c")
    chips = [(1 - x, y), (x, 1 - y), (1 - x, 1 - y)]
    return x, y, c, 2 * x + y, chips, (x, y, 1 - c)


def _half(rows, which):
    return pl.ds(pl.multiple_of(which * (rows // 2), SUBLANES), rows // 2)


_HBM = pl.BlockSpec(memory_space=pl.ANY)


def _gather_shards(name, items):
    n = len(items)
    shapes = [a.shape[-2:] for a, _ in items]

    def body(*refs):
        srcs, outs = refs[:n], refs[n:2 * n]
        send1, recv1, send2, recv2, lsem = refs[2 * n:]
        x, y, c, k, chips, sib = _place()

        def shard(i):
            return srcs[i] if items[i][1] is None else srcs[i].at[items[i][1]]

        started, locs = [], []
        for i in range(n):
            rows = shapes[i][0]
            lc = pltpu.make_async_copy(shard(i), outs[i].at[k], lsem.at[i])
            lc.start()
            locs.append(lc)
            for j, (cx, cy) in enumerate(chips):
                cp = pltpu.make_async_remote_copy(
                    src_ref=shard(i).at[_half(rows, c)], dst_ref=outs[i].at[k, _half(rows, c)],
                    send_sem=send1.at[i, j], recv_sem=recv1.at[i, j], device_id=(cx, cy, c), device_id_type=MESH)
                cp.start()
                started.append(cp)
        for i in range(n):
            rows = shapes[i][0]
            for j, (cx, cy) in enumerate(chips):
                blk = outs[i].at[2 * cx + cy, _half(rows, c)]
                pltpu.make_async_remote_copy(
                    src_ref=blk, dst_ref=blk, send_sem=send1.at[i, j], recv_sem=recv1.at[i, j],
                    device_id=(cx, cy, c), device_id_type=MESH).wait_recv()
                fw = pltpu.make_async_remote_copy(
                    src_ref=blk, dst_ref=blk, send_sem=send2.at[i, j], recv_sem=recv2.at[i, j],
                    device_id=sib, device_id_type=MESH)
                fw.start()
                started.append(fw)
        for i in range(n):
            rows = shapes[i][0]
            for j, (cx, cy) in enumerate(chips):
                blk = outs[i].at[2 * cx + cy, _half(rows, 1 - c)]
                pltpu.make_async_remote_copy(
                    src_ref=blk, dst_ref=blk, send_sem=send2.at[i, j], recv_sem=recv2.at[i, j],
                    device_id=sib, device_id_type=MESH).wait_recv()
        for cp in started:
            cp.wait_send()
        for lc in locs:
            lc.wait()

    return pl.pallas_call(
        body, name=name,
        in_specs=[_HBM] * n, out_specs=[_HBM] * n,
        out_shape=[jax.ShapeDtypeStruct((N_SHARDS,) + tuple(s), a.dtype) for s, (a, _) in zip(shapes, items)],
        scratch_shapes=[pltpu.SemaphoreType.DMA((n, 3))] * 4 + [pltpu.SemaphoreType.DMA((n,))],
    )(*[a for a, _ in items])


def _cast_place(name, w, layer, pos):
    _, R, C = w.shape
    tr = _tile(R, 256)

    def body(x_ref, y_ref, c_ref, w_ref, o_ref):
        o_ref[...] = w_ref[...].astype(BF16)

    return pl.pallas_call(
        body, name=name,
        grid_spec=pltpu.PrefetchScalarGridSpec(
            num_scalar_prefetch=3, grid=(R // tr,),
            in_specs=[pl.BlockSpec((None, tr, C), lambda r, xr, yr, cr: (layer, r, 0))],
            out_specs=pl.BlockSpec((None, tr, C), lambda r, xr, yr, cr: (2 * xr[0] + yr[0], r, 0))),
        out_shape=jax.ShapeDtypeStruct((N_SHARDS, R, C), BF16),
        compiler_params=_cparams("parallel"),
    )(*pos, w)


_IN_HBM = pl.BlockSpec(memory_space=pltpu.HBM)
_SEM = pl.BlockSpec(memory_space=pltpu.SEMAPHORE)
_SPLIT_COPY_PARAMS = pltpu.CompilerParams(has_side_effects=pltpu.SideEffectType.DATAFLOW_SIDE_EFFECTING)


def _in_hbm(a):
    return pltpu.with_memory_space_constraint(a, pltpu.HBM)


def _gather_copy(ref, i, j, chip_xy, c, k_src, rows, send, recv):
    blk = ref.at[k_src, _half(rows, c)]
    return pltpu.make_async_remote_copy(
        src_ref=blk, dst_ref=blk, send_sem=send.at[3 * i + j], recv_sem=recv.at[3 * i + j],
        device_id=(*chip_xy, c), device_id_type=MESH)


def _gather_start(name, bufs, after):
    n = len(bufs)

    def body(*refs):
        ins = refs[:n]
        send, recv = refs[n + 1], refs[n + 2]
        token = refs[-1]
        x, y, c, k, chips, sib = _place()
        for i in range(n):
            for j, chip_xy in enumerate(chips):
                _gather_copy(ins[i], i, j, chip_xy, c, k, bufs[i].shape[1], send, recv).start()
        token[...] = jnp.zeros_like(token)

    res = pl.pallas_call(
        body, name=name,
        in_specs=[_IN_HBM] * n + [_HBM],
        out_specs=[_SEM, _SEM] + [_IN_HBM] * n + [pl.BlockSpec(memory_space=pltpu.VMEM)],
        out_shape=[pltpu.SemaphoreType.DMA((3 * n,)), pltpu.SemaphoreType.DMA((3 * n,))]
        + [pltpu.HBM(b.shape, b.dtype) for b in bufs] + [jax.ShapeDtypeStruct((SUBLANES, 128), F32)],
        input_output_aliases={i: 2 + i for i in range(n)},
        compiler_params=_SPLIT_COPY_PARAMS,
    )(*[_in_hbm(b) for b in bufs], after)
    return res[0], res[1], list(res[2:2 + n]), res[-1]


def _gather_wait(name, bufs, send, recv, after):
    n = len(bufs)

    def body(*refs):
        ins = refs[:n]
        send_ref, recv_ref = refs[n], refs[n + 1]
        x, y, c, k, chips, sib = _place()
        for i in range(n):
            for j, chip_xy in enumerate(chips):
                rows = bufs[i].shape[1]
                _gather_copy(ins[i], i, j, chip_xy, c, k, rows, send_ref, recv_ref).wait_send()
                _gather_copy(ins[i], i, j, chip_xy, c, 2 * chip_xy[0] + chip_xy[1], rows, send_ref, recv_ref).wait_recv()

    return pl.pallas_call(
        body, name=name,
        in_specs=[_IN_HBM] * n + [_SEM, _SEM, _HBM],
        out_specs=[_IN_HBM] * n,
        out_shape=[pltpu.HBM(b.shape, b.dtype) for b in bufs],
        input_output_aliases={i: i for i in range(n)},
        compiler_params=_SPLIT_COPY_PARAMS,
    )(*bufs, send, recv, after)


def _gather_forward(name, bufs):
    n = len(bufs)

    def body(*refs):
        outs = refs[n:2 * n]
        send, recv = refs[2 * n:]
        x, y, c, k, chips, sib = _place()
        started = []
        for i in range(n):
            rows = bufs[i].shape[1]
            for j, (cx, cy) in enumerate(chips):
                blk = outs[i].at[2 * cx + cy, _half(rows, c)]
                fw = pltpu.make_async_remote_copy(
                    src_ref=blk, dst_ref=blk, send_sem=send.at[i, j], recv_sem=recv.at[i, j],
                    device_id=sib, device_id_type=MESH)
                fw.start()
                started.append(fw)
        for i in range(n):
            rows = bufs[i].shape[1]
            for j, (cx, cy) in enumerate(chips):
                blk = outs[i].at[2 * cx + cy, _half(rows, 1 - c)]
                pltpu.make_async_remote_copy(
                    src_ref=blk, dst_ref=blk, send_sem=send.at[i, j], recv_sem=recv.at[i, j],
                    device_id=sib, device_id_type=MESH).wait_recv()
        for cp in started:
            cp.wait_send()

    return pl.pallas_call(
        body, name=name, in_specs=[_HBM] * n, out_specs=[_HBM] * n,
        out_shape=[jax.ShapeDtypeStruct(b.shape, b.dtype) for b in bufs],
        input_output_aliases={i: i for i in range(n)},
        scratch_shapes=[pltpu.SemaphoreType.DMA((n, 3))] * 2,
    )(*bufs)


def _swap_halves(name, parts):
    n = len(parts)

    def body(*refs):
        srcs, outs = refs[:n], refs[n:2 * n]
        send, recv = refs[2 * n:]
        x, y, c, k, chips, sib = _place()
        cps = []
        for i in range(n):
            rows = parts[i].shape[1]
            cp = pltpu.make_async_remote_copy(
                src_ref=srcs[i].at[:, _half(rows, 1 - c)], dst_ref=outs[i],
                send_sem=send.at[i], recv_sem=recv.at[i], device_id=sib, device_id_type=MESH)
            cp.start()
            cps.append(cp)
        for cp in cps:
            cp.wait()

    return pl.pallas_call(
        body, name=name, in_specs=[_HBM] * n, out_specs=[_HBM] * n,
        out_shape=[jax.ShapeDtypeStruct((p.shape[0], p.shape[1] // 2, p.shape[2]), p.dtype) for p in parts],
        scratch_shapes=[pltpu.SemaphoreType.DMA((n,))] * 2,
    )(*parts)


def _scatter_copy(src_ref, land_ref, i, j, chip_xy, c, send, recv):
    return pltpu.make_async_remote_copy(
        src_ref=src_ref.at[2 * chip_xy[0] + chip_xy[1]], dst_ref=land_ref.at[j],
        send_sem=send.at[3 * i + j], recv_sem=recv.at[3 * i + j], device_id=(*chip_xy, c), device_id_type=MESH)


def _scatter_start(name, sums):
    n = len(sums)
    lands = [lax.empty((3,) + s.shape[1:], s.dtype) for s in sums]

    def body(*refs):
        srcs, lnds = refs[:n], refs[n:2 * n]
        send, recv = refs[2 * n], refs[2 * n + 1]
        token = refs[-1]
        x, y, c, k, chips, sib = _place()
        for i in range(n):
            for j, chip_xy in enumerate(chips):
                _scatter_copy(srcs[i], lnds[i], i, j, chip_xy, c, send, recv).start()
        token[...] = jnp.zeros_like(token)

    res = pl.pallas_call(
        body, name=name,
        in_specs=[_IN_HBM] * (2 * n),
        out_specs=[_SEM, _SEM] + [_IN_HBM] * (2 * n) + [pl.BlockSpec(memory_space=pltpu.VMEM)],
        out_shape=[pltpu.SemaphoreType.DMA((3 * n,)), pltpu.SemaphoreType.DMA((3 * n,))]
        + [pltpu.HBM(a.shape, a.dtype) for a in list(sums) + lands] + [jax.ShapeDtypeStruct((SUBLANES, 128), F32)],
        input_output_aliases={i: 2 + i for i in range(2 * n)},
        compiler_params=_SPLIT_COPY_PARAMS,
    )(*[_in_hbm(a) for a in list(sums) + lands])
    return res[0], res[1], list(res[2:2 + n]), list(res[2 + n:2 + 2 * n]), res[-1]


def _scatter_wait(name, sums, lands, send, recv, afters):
    n = len(sums)

    def body(*refs):
        srcs, lnds = refs[:n], refs[n:2 * n]
        send_ref, recv_ref = refs[2 * n], refs[2 * n + 1]
        x, y, c, k, chips, sib = _place()
        for i in range(n):
            for j, chip_xy in enumerate(chips):
                cp = _scatter_copy(srcs[i], lnds[i], i, j, chip_xy, c, send_ref, recv_ref)
                cp.wait_send()
                cp.wait_recv()

    res = pl.pallas_call(
        body, name=name,
        in_specs=[_IN_HBM] * (2 * n) + [_SEM, _SEM] + [_HBM] * len(afters),
        out_specs=[_IN_HBM] * (2 * n),
        out_shape=[pltpu.HBM(a.shape, a.dtype) for a in list(sums) + list(lands)],
        input_output_aliases={i: i for i in range(2 * n)},
        compiler_params=_SPLIT_COPY_PARAMS,
    )(*sums, *lands, send, recv, *afters)
    return list(res[:n]), list(res[n:])


def _join_halves(name, fulls):
    n = len(fulls)

    def body(*refs):
        outs = refs[n:2 * n]
        send, recv = refs[2 * n:]
        x, y, c, k, chips, sib = _place()
        cps = []
        for i in range(n):
            blk = outs[i].at[_half(fulls[i].shape[0], c)]
            cp = pltpu.make_async_remote_copy(
                src_ref=blk, dst_ref=blk, send_sem=send.at[i], recv_sem=recv.at[i],
                device_id=sib, device_id_type=MESH)
            cp.start()
            cps.append(cp)
        for i in range(n):
            blk = outs[i].at[_half(fulls[i].shape[0], 1 - c)]
            pltpu.make_async_remote_copy(
                src_ref=blk, dst_ref=blk, send_sem=send.at[i], recv_sem=recv.at[i],
                device_id=sib, device_id_type=MESH).wait_recv()
        for cp in cps:
            cp.wait_send()

    return pl.pallas_call(
        body, name=name, in_specs=[_HBM] * n, out_specs=[_HBM] * n,
        out_shape=[jax.ShapeDtypeStruct(f.shape, f.dtype) for f in fulls],
        input_output_aliases={i: i for i in range(n)},
        scratch_shapes=[pltpu.SemaphoreType.DMA((n,))] * 2,
    )(*fulls)


def _sum_over_devices(name, buf, loss_row, afters):
    R, D = buf.shape
    n_after = len(afters)

    def body(x_ref, *rest):
        all_ref, tot_ref, loss_ref, send_sems, recv_sems, local_sem = rest[n_after:]
        x, y, c, k, chips, sib = _place()
        me = (x, y, c)

        def block(px, py, pc):
            return all_ref.at[4 * px + 2 * py + pc]

        def copy(kk, blk, to, src=None):
            return pltpu.make_async_remote_copy(
                src_ref=block(*blk) if src is None else src, dst_ref=block(*blk),
                send_sem=send_sems.at[kk], recv_sem=recv_sems.at[kk], device_id=to, device_id_type=MESH)

        mine = pltpu.make_async_copy(x_ref, block(*me), local_sem)
        mine.start()
        first = [copy(0, me, sib, src=x_ref)]
        first += [copy(1 + j, me, (*chip, c), src=x_ref) for j, chip in enumerate(chips)]
        for cp in first:
            cp.start()
        passed = [copy(4 + j, (*chip, c), sib) for j, chip in enumerate(chips)]
        for j, chip in enumerate(chips):
            copy(1 + j, (*chip, c), me).wait_recv()
            passed[j].start()
        copy(0, sib, me).wait_recv()
        for j, chip in enumerate(chips):
            copy(4 + j, (*chip, 1 - c), me).wait_recv()
        for cp in first + passed:
            cp.wait_send()
        mine.wait()
        rc = _tile(R, 32)
        for r0 in range(0, R, rc):
            tot = all_ref[0, r0:r0 + rc, :]
            for d in range(1, N_DEVICES):
                tot = tot + all_ref[d, r0:r0 + rc, :]
            tot_ref[r0:r0 + rc, :] = tot
        loss = 0.5 * jnp.sum(tot_ref[loss_row:loss_row + 1, :]) / D
        loss_ref[...] = jnp.full(loss_ref.shape, loss, F32)

    vm = pl.BlockSpec(memory_space=pltpu.VMEM)
    return pl.pallas_call(
        body, name=name, in_specs=[vm] + [_HBM] * n_after, out_specs=[vm, vm, vm],
        out_shape=[jax.ShapeDtypeStruct((N_DEVICES, R, D), F32), jax.ShapeDtypeStruct((R, D), F32),
                   jax.ShapeDtypeStruct((SUBLANES, 128), F32)],
        scratch_shapes=[pltpu.SemaphoreType.DMA((7,)), pltpu.SemaphoreType.DMA((7,)), pltpu.SemaphoreType.DMA],
        compiler_params=pltpu.CompilerParams(vmem_limit_bytes=V7X_VMEM_LIMIT_BYTES),
    )(buf, *afters)[1:]


def _add_my_half(name, part, got, pos):
    S, R, C = part.shape
    R2 = R // 2
    tr = _tile(R2, 512)
    q = R2 // tr

    def body(x_ref, y_ref, c_ref, p_ref, g_ref, o_ref):
        o_ref[...] = (p_ref[...].astype(F32) + g_ref[...].astype(F32)).astype(o_ref.dtype)

    return pl.pallas_call(
        body, name=name,
        grid_spec=pltpu.PrefetchScalarGridSpec(
            num_scalar_prefetch=3, grid=(S, q),
            in_specs=[pl.BlockSpec((None, tr, C), lambda s, r, xr, yr, cr: (s, cr[0] * q + r, 0)),
                      pl.BlockSpec((None, tr, C), lambda s, r, xr, yr, cr: (s, r, 0))],
            out_specs=pl.BlockSpec((None, tr, C), lambda s, r, xr, yr, cr: (s, r, 0))),
        out_shape=jax.ShapeDtypeStruct((S, R2, C), BF16),
        compiler_params=_cparams("parallel", "parallel"),
    )(*pos, part, got)


def _add_owner(name, sums, got, pos):
    _, R2, C = sums.shape
    tr = _tile(R2, 512)
    q = R2 // tr

    def body(x_ref, y_ref, c_ref, s_ref, g_ref, o_ref):
        acc = s_ref[...].astype(F32)
        for j in range(3):
            acc = acc + g_ref[j].astype(F32)
        o_ref[...] = acc

    return pl.pallas_call(
        body, name=name,
        grid_spec=pltpu.PrefetchScalarGridSpec(
            num_scalar_prefetch=3, grid=(q,),
            in_specs=[pl.BlockSpec((None, tr, C), lambda r, xr, yr, cr: (2 * xr[0] + yr[0], r, 0)),
                      pl.BlockSpec((3, tr, C), lambda r, xr, yr, cr: (0, r, 0))],
            out_specs=pl.BlockSpec((tr, C), lambda r, xr, yr, cr: (cr[0] * q + r, 0))),
        out_shape=jax.ShapeDtypeStruct((2 * R2, C), F32),
        compiler_params=_cparams("parallel"),
    )(*pos, sums, got)


def _adamw(name, w, m, v, g):
    R, C = w.shape
    tr = SUBLANES
    while 2 * tr * C <= ADAMW_TILE_ELEMS:
        tr *= 2
    tr = _tile(R, tr)
    bc1 = 1.0 - ADAM_B1 ** ADAM_STEP
    bc2 = 1.0 - ADAM_B2 ** ADAM_STEP

    def body(w_ref, m_ref, v_ref, g_ref, go_ref, d_ref, mo_ref, vo_ref):
        gg = g_ref[...]
        m2 = ADAM_B1 * m_ref[...] + (1.0 - ADAM_B1) * gg
        v2 = ADAM_B2 * v_ref[...] + (1.0 - ADAM_B2) * (gg * gg)
        go_ref[...] = gg
        mo_ref[...] = m2
        vo_ref[...] = v2
        d_ref[...] = -ADAM_LR * ((m2 / bc1) / (jnp.sqrt(v2 / bc2) + ADAM_EPS) + ADAM_WD * w_ref[...])

    spec = pl.BlockSpec((tr, C), lambda r: (r, 0))
    return pl.pallas_call(
        body, name=name, grid=(R // tr,), in_specs=[spec] * 4, out_specs=[spec] * 4,
        out_shape=[jax.ShapeDtypeStruct((R, C), F32)] * 4,
        compiler_params=_cparams("parallel"),
    )(w, m, v, g)


def _adamw_slab(name, w, m, v, g, layer, prev):
    L, R, C = w.shape
    tr = SUBLANES
    while 2 * tr * C <= ADAMW_TILE_ELEMS:
        tr *= 2
    tr = _tile(R, tr)
    bc1 = 1.0 - ADAM_B1 ** ADAM_STEP
    bc2 = 1.0 - ADAM_B2 ** ADAM_STEP

    def body(w_ref, m_ref, v_ref, g_ref, *rest):
        go_ref, d_ref, mo_ref, vo_ref = rest[-4:]
        gg = g_ref[...]
        m2 = ADAM_B1 * m_ref[...] + (1.0 - ADAM_B1) * gg
        v2 = ADAM_B2 * v_ref[...] + (1.0 - ADAM_B2) * (gg * gg)
        go_ref[...] = gg
        mo_ref[...] = m2
        vo_ref[...] = v2
        d_ref[...] = -ADAM_LR * ((m2 / bc1) / (jnp.sqrt(v2 / bc2) + ADAM_EPS) + ADAM_WD * w_ref[...])

    slab = pl.BlockSpec((None, tr, C), lambda r: (layer, r, 0))
    n_prev = 0 if prev is None else 4
    return pl.pallas_call(
        body, name=name, grid=(R // tr,),
        in_specs=[slab] * 3 + [pl.BlockSpec((tr, C), lambda r: (r, 0))] + [_HBM] * n_prev,
        out_specs=[slab] * 4,
        out_shape=[jax.ShapeDtypeStruct((L, R, C), F32)] * 4,
        input_output_aliases={4 + i: i for i in range(n_prev)},
        compiler_params=_cparams("parallel"),
    )(w, m, v, g, *(prev or ()))


def _reduce_begin(tag, parts, pos):
    got = _swap_halves(f"rs_swap_{tag}", parts)
    sums = [_add_my_half(f"rs_add2_{tag}_{i}", p, g, pos) for i, (p, g) in enumerate(zip(parts, got))]
    return _scatter_start(f"rs_scatter_start_{tag}", sums)


def _reduce_end(tag, started, pos, afters):
    send, recv, sums, lands, _ = started
    sums, lands = _scatter_wait(f"rs_scatter_wait_{tag}", sums, lands, send, recv, afters)
    fulls = [_add_owner(f"rs_add4_{tag}_{i}", s, q, pos) for i, (s, q) in enumerate(zip(sums, lands))]
    return _join_halves(f"rs_join_{tag}", fulls)


def _pad_rows(a):
    r = (-a.shape[0]) % SUBLANES
    return jnp.pad(a, ((0, r), (0, 0))) if r else a


def kernel(x, p, norm_mix, norm_mlp, norm_ple, cf_w_pw1, cf_b_pw1, cf_w_dw, cf_b_dw, cf_norm, cf_w_pw2, cf_b_pw2, sc_w_in, sc_w_conv, sc_w_out, mlp_w1, mlp_w2, ple_w_proj, ple_w_gate, norm_final, loss_target, m_norm_mix, m_norm_mlp, m_norm_ple, m_cf_w_pw1, m_cf_b_pw1, m_cf_w_dw, m_cf_b_dw, m_cf_norm, m_cf_w_pw2, m_cf_b_pw2, m_sc_w_in, m_sc_w_conv, m_sc_w_out, m_mlp_w1, m_mlp_w2, m_ple_w_proj, m_ple_w_gate, m_norm_final, v_norm_mix, v_norm_mlp, v_norm_ple, v_cf_w_pw1, v_cf_b_pw1, v_cf_w_dw, v_cf_b_dw, v_cf_norm, v_cf_w_pw2, v_cf_b_pw2, v_sc_w_in, v_sc_w_conv, v_sc_w_out, v_mlp_w1, v_mlp_w2, v_ple_w_proj, v_ple_w_gate, v_norm_final):
    T, D = x.shape[1], x.shape[2]
    KA, KB = cf_w_dw.shape[1], sc_w_conv.shape[1]
    chip = (2 * lax.axis_index("x") + lax.axis_index("y")).astype(jnp.int32)
    pos = tuple(lax.axis_index(ax).astype(jnp.int32).reshape(1) for ax in ("x", "y", "c"))

    params = dict(norm_mix=norm_mix, norm_mlp=norm_mlp, norm_ple=norm_ple, cf_w_pw1=cf_w_pw1, cf_b_pw1=cf_b_pw1,
                  cf_w_dw=cf_w_dw, cf_b_dw=cf_b_dw, cf_norm=cf_norm, cf_w_pw2=cf_w_pw2, cf_b_pw2=cf_b_pw2,
                  sc_w_in=sc_w_in, sc_w_conv=sc_w_conv, sc_w_out=sc_w_out, mlp_w1=mlp_w1, mlp_w2=mlp_w2,
                  ple_w_proj=ple_w_proj, ple_w_gate=ple_w_gate, norm_final=norm_final)
    mom1 = dict(norm_mix=m_norm_mix, norm_mlp=m_norm_mlp, norm_ple=m_norm_ple, cf_w_pw1=m_cf_w_pw1,
                cf_b_pw1=m_cf_b_pw1, cf_w_dw=m_cf_w_dw, cf_b_dw=m_cf_b_dw, cf_norm=m_cf_norm, cf_w_pw2=m_cf_w_pw2,
                cf_b_pw2=m_cf_b_pw2, sc_w_in=m_sc_w_in, sc_w_conv=m_sc_w_conv, sc_w_out=m_sc_w_out,
                mlp_w1=m_mlp_w1, mlp_w2=m_mlp_w2, ple_w_proj=m_ple_w_proj, ple_w_gate=m_ple_w_gate,
                norm_final=m_norm_final)
    mom2 = dict(norm_mix=v_norm_mix, norm_mlp=v_norm_mlp, norm_ple=v_norm_ple, cf_w_pw1=v_cf_w_pw1,
                cf_b_pw1=v_cf_b_pw1, cf_w_dw=v_cf_w_dw, cf_b_dw=v_cf_b_dw, cf_norm=v_cf_norm, cf_w_pw2=v_cf_w_pw2,
                cf_b_pw2=v_cf_b_pw2, sc_w_in=v_sc_w_in, sc_w_conv=v_sc_w_conv, sc_w_out=v_sc_w_out,
                mlp_w1=v_mlp_w1, mlp_w2=v_mlp_w2, ple_w_proj=v_ple_w_proj, ple_w_gate=v_ple_w_gate,
                norm_final=v_norm_final)

    big = ("cf_w_pw1", "cf_w_pw2", "sc_w_in", "sc_w_out", "mlp_w1", "mlp_w2", "ple_w_proj", "ple_w_gate")
    row_sharded = ("cf_w_pw2", "sc_w_out", "mlp_w2", "ple_w_gate")

    def layer_names(i):
        return (["cf_w_pw1", "cf_w_pw2"] if i % 2 == 0 else ["sc_w_in", "sc_w_out"]) + \
            ["mlp_w1", "mlp_w2", "ple_w_proj", "ple_w_gate"]

    def layer_index(i, name):
        return i if name.startswith(("mlp", "ple")) else i // 2

    def gather_begin(tag, i, names, after):
        bufs = [_cast_place(f"place_{nm}_{i}", params[nm], layer_index(i, nm), pos) for nm in names]
        return names, _gather_start(f"gather_start_{tag}", bufs, after)

    def gather_end(tag, begun, after):
        names, (send, recv, bufs, _) = begun
        bufs = _gather_wait(f"gather_wait_{tag}", bufs, send, recv, after)
        bufs = _gather_forward(f"gather_fwd_{tag}", bufs)
        return {nm: g4.reshape(1, N_SHARDS * g4.shape[1], g4.shape[2]) if nm in row_sharded else g4
                for nm, g4 in zip(names, bufs)}

    conv_small = jnp.concatenate([_pad_rows(cf_w_dw[j]) for j in range(cf_w_dw.shape[0])]
                                 + [_pad_rows(sc_w_conv[j]) for j in range(sc_w_conv.shape[0])], axis=0)
    conv_shards = _gather_shards("gather_conv_w", [(conv_small, None)])[0]
    conv_all = jnp.transpose(conv_shards, (1, 0, 2)).reshape(conv_small.shape[0], D)
    ka_pad = KA + (-KA) % SUBLANES
    kb_pad = KB + (-KB) % SUBLANES
    w_dw_full = [conv_all[j * ka_pad:j * ka_pad + KA] for j in range(cf_w_dw.shape[0])]
    off = cf_w_dw.shape[0] * ka_pad
    w_conv_full = [conv_all[off + j * kb_pad:off + j * kb_pad + KB] for j in range(sc_w_conv.shape[0])]

    def vec(a):
        return a.reshape(1, -1)

    ident = lambda acc: (acc,)

    h = x[0]
    saved = []
    first = gather_begin("0m", 0, layer_names(0)[:2], conv_shards)
    rest = gather_begin("0r", 0, layer_names(0)[2:], first[1][2][0])
    W = [gather_end("0m", first, h)]
    for i in range(DEPTH):
        j = i // 2
        wl = W[i]
        s = dict(h=h)
        g_mix = vec(norm_mix[i])
        if i + 1 < DEPTH:
            nxt = gather_begin(f"{i + 1}", i + 1, layer_names(i + 1), wl[layer_names(i)[0]])
            g_mix = g_mix + nxt[1][3][0, 0]
        s["u"] = _rms_fwd(f"rms_mix_{i}", h, g_mix)
        if i % 2 == 0:
            s["a"] = _mm_nn(f"cf_pw1_{i}", s["u"], wl["cf_w_pw1"], lambda acc, b: (acc + b,), [BF16],
                            extras=[(vec(cf_b_pw1[j]), "n")])[0]
            s["v2"], s["v4"] = _cf_conv_fwd(f"cf_conv_{i}", s["a"], w_dw_full[j], vec(cf_b_dw[j]), vec(cf_norm[j]))
            h1 = _mm_nn(f"cf_pw2_{i}", s["v4"], wl["cf_w_pw2"], lambda acc, b, r: (r + (acc + b),), [F32],
                        extras=[(vec(cf_b_pw2[j]), "n"), (h, "mn")])[0]
        else:
            s["bcv"] = _mm_nn(f"sc_in_{i}", s["u"], wl["sc_w_in"], ident, [BF16])[0]
            s["y"] = _sc_conv_fwd(f"sc_conv_{i}", s["bcv"], w_conv_full[j])[0]
            h1 = _mm_nn(f"sc_out_{i}", s["y"], wl["sc_w_out"], lambda acc, r: (r + acc,), [F32],
                        extras=[(h, "mn")])[0]
        s["h1"] = h1
        if i == 0:
            wl.update(gather_end("0r", rest, h1))
        s["u2"] = _rms_fwd(f"rms_mlp_{i}", h1, vec(norm_mlp[i]))
        s["z"], s["hd"] = _mm_nn(f"mlp_w1_{i}", s["u2"], wl["mlp_w1"],
                                 lambda acc: (acc, jnp.square(jnp.maximum(acc, 0.0))), [BF16, BF16])
        h2 = _mm_nn(f"mlp_w2_{i}", s["hd"], wl["mlp_w2"], lambda acc, r: (r + acc,), [F32], extras=[(h1, "mn")])[0]
        s["h2"] = h2
        s["n3"] = _rms_fwd(f"rms_ple_{i}", h2, vec(norm_ple[i]))
        s["p"] = p[i, 0]
        s["e"] = _mm_nn(f"ple_proj_{i}", s["p"], wl["ple_w_proj"], ident, [BF16])[0]
        h, s["q"] = _mm_nn(f"ple_gate_{i}", s["n3"], wl["ple_w_gate"],
                           lambda acc, r, e: (r + _sigmoid(acc) * e.astype(F32), acc), [F32, BF16],
                           extras=[(h2, "mn"), (s["e"], "mn")])
        saved.append(s)
        if i + 1 < DEPTH:
            W.append(gather_end(f"{i + 1}", nxt, h))

    dh, dh16, dg_final, loss_cols = _loss_bwd("loss_bwd", h, vec(norm_final), loss_target[0])
    small = {"norm_final": dg_final, "loss": loss_cols}
    adam = {nm: None for nm in big}

    def finish_layer(i, started, afters):
        for nm, g in zip(layer_names(i), _reduce_end(f"{i}", started, pos, afters)):
            l = layer_index(i, nm)
            adam[nm] = _adamw_slab(f"adamw_{nm}_{l}", params[nm], mom1[nm], mom2[nm], g, l, adam[nm])

    pending = None
    for i in reversed(range(DEPTH)):
        j = i // 2
        wl, s = W[i], saved[i]
        dq, de = _ple_elem_bwd(f"ple_elem_bwd_{i}", dh, s["q"], s["e"])
        d_proj = _mm_tn(f"ple_proj_dw_{i}", s["p"], de, N_SHARDS)
        d_gate = _mm_tn(f"ple_gate_dw_{i}", s["n3"], dq, 1)
        dn3 = _mm_nt(f"ple_gate_dx_{i}", dq, wl["ple_w_gate"], ident, [F32])[0]
        dh, dh16, small[f"norm_ple_{i}"] = _rms_bwd(f"rms_ple_bwd_{i}", s["h2"], vec(norm_ple[i]), dn3, dh)

        d_w2 = _mm_tn(f"mlp_w2_dw_{i}", s["hd"], dh16, 1)
        dz = _mm_nt(f"mlp_w2_dx_{i}", dh16, wl["mlp_w2"],
                    lambda acc, z: (acc * (2.0 * jnp.maximum(z.astype(F32), 0.0)),), [BF16], extras=[s["z"]])[0]
        d_w1 = _mm_tn(f"mlp_w1_dw_{i}", s["u2"], dz, N_SHARDS)
        du2 = _mm_nt(f"mlp_w1_dx_{i}", dz, wl["mlp_w1"], ident, [F32])[0]
        if i % 2 == 0:
            dh, dh16, small[f"norm_mlp_{i}"], small[f"cf_b_pw2_{j}"] = _rms_bwd(
                f"rms_mlp_bwd_{i}", s["h1"], vec(norm_mlp[i]), du2, dh, want_colsum=True)
            d_mix_out = _mm_tn(f"cf_pw2_dw_{i}", s["v4"], dh16, 1)
            dv4 = _mm_nt(f"cf_pw2_dx_{i}", dh16, wl["cf_w_pw2"], ident, [F32])[0]
            dv2, small[f"cf_norm_{j}"], small[f"cf_b_dw_{j}"] = _cf_norm_bwd(
                f"cf_norm_bwd_{i}", s["v2"], vec(cf_norm[j]), dv4)
            da, small[f"cf_w_dw_{j}"], db1 = _cf_conv_bwd(f"cf_conv_bwd_{i}", dv2, s["a"], w_dw_full[j])
            small[f"cf_b_pw1_{j}"] = db1.reshape(2, D)
            d_mix_in = _mm_tn(f"cf_pw1_dw_{i}", s["u"], da, N_SHARDS)
            du = _mm_nt(f"cf_pw1_dx_{i}", da, wl["cf_w_pw1"], ident, [F32])[0]
        else:
            dh, dh16, small[f"norm_mlp_{i}"] = _rms_bwd(f"rms_mlp_bwd_{i}", s["h1"], vec(norm_mlp[i]), du2, dh)
            d_mix_out = _mm_tn(f"sc_out_dw_{i}", s["y"], dh16, 1)
            dy = _mm_nt(f"sc_out_dx_{i}", dh16, wl["sc_w_out"], ident, [F32])[0]
            da, small[f"sc_w_conv_{j}"] = _sc_conv_bwd(f"sc_conv_bwd_{i}", dy, s["bcv"], w_conv_full[j])
            d_mix_in = _mm_tn(f"sc_in_dw_{i}", s["u"], da, N_SHARDS)
            du = _mm_nt(f"sc_in_dx_{i}", da, wl["sc_w_in"], ident, [F32])[0]

        parts = [d_mix_in, d_mix_out, d_w1, d_w2, d_proj, d_gate]
        parts = [pt.reshape(N_SHARDS, pt.shape[1] // N_SHARDS, pt.shape[2]) if nm in row_sharded else pt
                 for nm, pt in zip(layer_names(i), parts)]
        started = _reduce_begin(f"{i}", parts, pos)
        token = started[4]
        dh, dh16, small[f"norm_mix_{i}"] = _rms_bwd(f"rms_mix_bwd_{i}", s["h"], vec(norm_mix[i]) + token[0, 0], du, dh)
        if pending is not None:
            finish_layer(i + 1, pending, [token])
        pending = started
    grad_x = dh.reshape(x.shape)

    order = sorted(small)
    pieces, where, row = [], {}, 0
    for nm in order:
        pc = _pad_rows(small[nm])
        where[nm] = (row, small[nm].shape[0])
        row += pc.shape[0]
        pieces.append(pc)
    updated = [res[3] for res in adam.values() if res is not None]
    total, loss_tile = _sum_over_devices("small_allsum", jnp.concatenate(pieces, axis=0), where["loss"][0], updated)
    loss = loss_tile[0, 0]
    finish_layer(0, pending, [total] + updated)

    def small_sum(nm):
        r0, nr = where[nm]
        return total[r0:r0 + nr]

    def my_cols(a):
        return lax.dynamic_slice_in_dim(a, chip * (D // N_SHARDS), D // N_SHARDS, axis=1)

    g_small = {
        "norm_mix": jnp.concatenate([small_sum(f"norm_mix_{i}") for i in range(DEPTH)], axis=0),
        "norm_mlp": jnp.concatenate([small_sum(f"norm_mlp_{i}") for i in range(DEPTH)], axis=0),
        "norm_ple": jnp.concatenate([small_sum(f"norm_ple_{i}") for i in range(DEPTH)], axis=0),
        "cf_b_pw1": jnp.stack([small_sum(f"cf_b_pw1_{j}").reshape(2 * D) for j in range(DEPTH // 2)]),
        "cf_w_dw": jnp.stack([my_cols(small_sum(f"cf_w_dw_{j}")) for j in range(DEPTH // 2)]),
        "cf_b_dw": jnp.concatenate([small_sum(f"cf_b_dw_{j}") for j in range(DEPTH // 2)], axis=0),
        "cf_norm": jnp.concatenate([small_sum(f"cf_norm_{j}") for j in range(DEPTH // 2)], axis=0),
        "cf_b_pw2": jnp.concatenate([small_sum(f"cf_b_pw2_{j}") for j in range(DEPTH // 2)], axis=0),
        "sc_w_conv": jnp.stack([my_cols(small_sum(f"sc_w_conv_{j}")) for j in range(DEPTH // 2)]),
        "norm_final": small_sum("norm_final").reshape(D),
    }

    names_out = ["norm_mix", "norm_mlp", "norm_ple", "cf_w_pw1", "cf_b_pw1", "cf_w_dw", "cf_b_dw", "cf_norm",
                 "cf_w_pw2", "cf_b_pw2", "sc_w_in", "sc_w_conv", "sc_w_out", "mlp_w1", "mlp_w2", "ple_w_proj",
                 "ple_w_gate", "norm_final"]
    grad, delta, new_m, new_v = {}, {}, {}, {}
    for nm in names_out:
        w = params[nm]
        if nm in big:
            grad[nm], delta[nm], new_m[nm], new_v[nm] = adam[nm]
            continue
        g = g_small[nm]
        cols = w.shape[-1] if w.ndim > 1 else w.shape[0]
        two_d = lambda a: a.reshape(-1, cols)
        res = _adamw(f"adamw_{nm}", two_d(w), two_d(mom1[nm]), two_d(mom2[nm]), two_d(g))
        grad[nm], delta[nm], new_m[nm], new_v[nm] = [r.reshape(w.shape) for r in res]

    return (loss, grad_x, *[grad[n] for n in names_out], *[delta[n] for n in names_out],
            *[new_m[n] for n in names_out], *[new_v[n] for n in names_out])
```

```python
import functools

import jax
import jax.numpy as jnp
from jax import lax
from jax.experimental import pallas as pl
from jax.experimental.pallas import tpu as pltpu

F32 = jnp.float32
BF16 = jnp.bfloat16

EPS = 1e-6
ADAM_LR = 0.001
ADAM_B1 = 0.9
ADAM_B2 = 0.999
ADAM_EPS = 1e-08
ADAM_WD = 0.01
ADAM_STEP = 10

DEPTH = 4
N_SHARDS = 4
N_DEVICES = 8
V7X_VMEM_LIMIT_BYTES = 56 * 1024 * 1024
SUBLANES = 8
MESH = pl.DeviceIdType.MESH

MM_TM = 1024
MM_TN = 1024
MM_TK = 2048
MM_TW_K = 1024
MM_TW_T = 4096
MM_TX_K = 1024
MM_TX_N = 2048
MM_TN_CROWDED = 512
MM_LONG_K = 8192
MM_TN_LONG_K = 256
ADAMW_TILE_ELEMS = 256 * 2048

CONV_ROW_CHUNK = 32
CONV_LANE_CHUNK = 512
CONV_TILE_ROWS = 128
ROW_TILE = 256


def _tile(dim, pref):
    if dim <= pref:
        return dim
    t = pref
    while dim % t:
        t //= 2
    return t


def _cparams(*sem):
    return pltpu.CompilerParams(dimension_semantics=sem, vmem_limit_bytes=V7X_VMEM_LIMIT_BYTES)


def _sigmoid(x):
    return 1.0 / (1.0 + jnp.exp(-x))


def _rms_r(x):
    return lax.rsqrt(jnp.mean(x * x, axis=-1, keepdims=True) + EPS)


def _mm_nn(name, a, b3, epilogue, out_dtypes, extras=()):
    M, K = a.shape
    S, Kb, Ns = b3.shape
    assert Kb == K
    N = S * Ns
    crowded = sum(kind == "mn" for _, kind in extras) > 1
    tm, tn, tk = _tile(M, MM_TM), _tile(Ns, MM_TN_CROWDED if crowded else MM_TN), _tile(K, MM_TK)
    if K >= MM_LONG_K:
        tn, tk = _tile(Ns, MM_TN_LONG_K), K
    per = Ns // tn
    nk = K // tk
    in_specs = [pl.BlockSpec((tm, tk), lambda i, j, k: (i, k)),
                pl.BlockSpec((None, tk, tn), lambda i, j, k: (j // per, k, j % per))]
    for _, kind in extras:
        if kind == "mn":
            in_specs.append(pl.BlockSpec((tm, tn), lambda i, j, k: (i, j)))
        else:
            in_specs.append(pl.BlockSpec((1, tn), lambda i, j, k: (0, j)))
    n_ex, n_o = len(extras), len(out_dtypes)

    def body(*refs):
        a_ref, b_ref = refs[:2]
        ex = refs[2:2 + n_ex]
        outs = refs[2 + n_ex:2 + n_ex + n_o]
        part = jnp.dot(a_ref[...].astype(BF16), b_ref[...], preferred_element_type=F32)

        def finish(acc):
            res = epilogue(acc, *[e[...] for e in ex])
            for r, o in zip(res, outs):
                o[...] = r.astype(o.dtype)

        if nk == 1:
            finish(part)
        else:
            acc_ref = refs[-1]
            k = pl.program_id(2)

            @pl.when(k == 0)
            def _():
                acc_ref[...] = part

            @pl.when(k > 0)
            def _():
                acc_ref[...] += part

            @pl.when(k == nk - 1)
            def _():
                finish(acc_ref[...])

    res = pl.pallas_call(
        body, name=name, grid=(M // tm, N // tn, nk),
        in_specs=in_specs,
        out_specs=[pl.BlockSpec((tm, tn), lambda i, j, k: (i, j)) for _ in out_dtypes],
        out_shape=[jax.ShapeDtypeStruct((M, N), dt) for dt in out_dtypes],
        scratch_shapes=[pltpu.VMEM((tm, tn), F32)] if nk > 1 else [],
        compiler_params=_cparams("parallel", "parallel", "arbitrary"),
    )(a, b3, *[e for e, _ in extras])
    return res


def _mm_nt(name, g, w3, epilogue, out_dtypes, extras=()):
    M, N = g.shape
    S, K, Ns = w3.shape
    assert S * Ns == N
    tm, tn, tkk = _tile(M, MM_TM), _tile(Ns, MM_TX_N), _tile(K, MM_TX_K)
    per = Ns // tn
    nn = N // tn
    n_ex, n_o = len(extras), len(out_dtypes)

    def body(*refs):
        g_ref, w_ref = refs[:2]
        ex = refs[2:2 + n_ex]
        outs = refs[2 + n_ex:2 + n_ex + n_o]
        part = lax.dot_general(g_ref[...].astype(BF16), w_ref[...], (((1,), (1,)), ((), ())),
                               preferred_element_type=F32)

        def finish(acc):
            res = epilogue(acc, *[e[...] for e in ex])
            for r, o in zip(res, outs):
                o[...] = r.astype(o.dtype)

        if nn == 1:
            finish(part)
        else:
            acc_ref = refs[-1]
            n = pl.program_id(2)

            @pl.when(n == 0)
            def _():
                acc_ref[...] = part

            @pl.when(n > 0)
            def _():
                acc_ref[...] += part

            @pl.when(n == nn - 1)
            def _():
                finish(acc_ref[...])

    return pl.pallas_call(
        body, name=name, grid=(M // tm, K // tkk, nn),
        in_specs=[pl.BlockSpec((tm, tn), lambda i, kk, n: (i, n)),
                  pl.BlockSpec((None, tkk, tn), lambda i, kk, n: (n // per, kk, n % per))]
        + [pl.BlockSpec((tm, tkk), lambda i, kk, n: (i, kk)) for _ in extras],
        out_specs=[pl.BlockSpec((tm, tkk), lambda i, kk, n: (i, kk)) for _ in out_dtypes],
        out_shape=[jax.ShapeDtypeStruct((M, K), dt) for dt in out_dtypes],
        scratch_shapes=[pltpu.VMEM((tm, tkk), F32)] if nn > 1 else [],
        compiler_params=_cparams("parallel", "parallel", "arbitrary"),
    )(g, w3, *extras)


def _mm_tn(name, a, g, n_shards):
    T, K = a.shape
    _, N = g.shape
    Ns = N // n_shards
    tk, tn, tt = _tile(K, MM_TW_K), _tile(Ns, MM_TN), _tile(T, MM_TW_T)
    per = Ns // tn
    nt = T // tt

    def body(a_ref, g_ref, o_ref, *scratch):
        part = lax.dot_general(a_ref[...].astype(BF16), g_ref[...].astype(BF16), (((0,), (0,)), ((), ())),
                               preferred_element_type=F32)
        if nt == 1:
            o_ref[...] = part.astype(o_ref.dtype)
            return
        acc_ref, = scratch
        t = pl.program_id(2)

        @pl.when(t == 0)
        def _():
            acc_ref[...] = part

        @pl.when(t > 0)
        def _():
            acc_ref[...] += part

        @pl.when(t == nt - 1)
        def _():
            o_ref[...] = acc_ref[...].astype(o_ref.dtype)

    return pl.pallas_call(
        body, name=name, grid=(K // tk, N // tn, nt),
        in_specs=[pl.BlockSpec((tt, tk), lambda i, j, t: (t, i)),
                  pl.BlockSpec((tt, tn), lambda i, j, t: (t, j))],
        out_specs=pl.BlockSpec((None, tk, tn), lambda i, j, t: (j // per, i, j % per)),
        out_shape=jax.ShapeDtypeStruct((n_shards, K, Ns), BF16),
        scratch_shapes=[pltpu.VMEM((tk, tn), F32)] if nt > 1 else [],
        compiler_params=_cparams("parallel", "parallel", "arbitrary"),
    )(a, g)


def _rowwise(name, fn, ins, outs, accs=(), scratch=(), tt=ROW_TILE):
    T = next(a.shape[0] for a, kind in ins if kind == "row")
    tt = _tile(T, tt)
    n = T // tt
    in_specs = []
    for a, kind in ins:
        w = a.shape[1]
        if kind == "row":
            in_specs.append(pl.BlockSpec((tt, w), lambda i: (i, 0)))
        elif kind == "vec":
            in_specs.append(pl.BlockSpec(a.shape, lambda i: (0, 0)))
        elif kind[0] == "prev":
            pad = kind[1]
            in_specs.append(pl.BlockSpec((pad, w), lambda i, q=tt // pad: (jnp.maximum(i * q - 1, 0), 0)))
        else:
            pad = kind[1]
            in_specs.append(pl.BlockSpec((pad, w), lambda i, q=tt // pad, last=T // pad - 1:
                                         (jnp.minimum((i + 1) * q, last), 0)))
    n_in, n_out, n_acc = len(ins), len(outs), len(accs)

    def body(*refs):
        i = pl.program_id(0)
        in_refs = refs[:n_in]
        out_refs = refs[n_in:n_in + n_out]
        acc_refs = refs[n_in + n_out:n_in + n_out + n_acc]
        scr = refs[n_in + n_out + n_acc:]
        if n_acc:
            @pl.when(i == 0)
            def _():
                for r in acc_refs:
                    r[...] = jnp.zeros_like(r)
        fn(i, n, in_refs, out_refs, acc_refs, scr)

    res = pl.pallas_call(
        body, name=name, grid=(n,),
        in_specs=in_specs,
        out_specs=[pl.BlockSpec((tt, w), lambda i: (i, 0)) for w, _ in outs]
        + [pl.BlockSpec((r, w), lambda i: (0, 0)) for r, w in accs],
        out_shape=[jax.ShapeDtypeStruct((T, w), dt) for w, dt in outs]
        + [jax.ShapeDtypeStruct((r, w), F32) for r, w in accs],
        scratch_shapes=list(scratch),
        compiler_params=_cparams("arbitrary"),
    )(*[a for a, _ in ins])
    return res


def _colsum(x):
    return jnp.sum(x, axis=0, keepdims=True)


def _rms_fwd(name, h, g):
    D = h.shape[1]

    def fn(i, n, ins, outs, accs, scr):
        x = ins[0][...]
        outs[0][...] = (x * _rms_r(x) * ins[1][...]).astype(BF16)

    return _rowwise(name, fn, [(h, "row"), (g, "vec")], [(D, BF16)])[0]


def _rms_bwd(name, h, g, du, dh_in, want_colsum=False):
    D = h.shape[1]

    def fn(i, n, ins, outs, accs, scr):
        x = ins[0][...]
        gg = ins[1][...]
        d = ins[2][...].astype(F32)
        r = _rms_r(x)
        xn = x * r
        t = d * gg
        dh = ins[3][...] + r * (t - xn * jnp.mean(t * xn, axis=-1, keepdims=True))
        outs[0][...] = dh
        outs[1][...] = dh.astype(BF16)
        accs[0][...] += _colsum(d * xn)
        if want_colsum:
            accs[1][...] += _colsum(dh)

    return _rowwise(name, fn, [(h, "row"), (g, "vec"), (du, "row"), (dh_in, "row")],
                    [(D, F32), (D, BF16)], accs=[(1, D)] * (2 if want_colsum else 1))


def _loss_bwd(name, h, g, tgt):
    D = h.shape[1]

    def fn(i, n, ins, outs, accs, scr):
        x = ins[0][...]
        gg = ins[1][...]
        r = _rms_r(x)
        xn = x * r
        err = xn * gg - ins[2][...]
        dy = err / D
        t = dy * gg
        dh = r * (t - xn * jnp.mean(t * xn, axis=-1, keepdims=True))
        outs[0][...] = dh
        outs[1][...] = dh.astype(BF16)
        accs[0][...] += _colsum(dy * xn)
        accs[1][...] += _colsum(err * err)

    return _rowwise(name, fn, [(h, "row"), (g, "vec"), (tgt, "row")], [(D, F32), (D, BF16)],
                    accs=[(1, D), (1, D)])


def _ple_elem_bwd(name, dh, q, e):
    D = dh.shape[1]

    def fn(i, n, ins, outs, accs, scr):
        d = ins[0][...]
        s = _sigmoid(ins[1][...].astype(F32))
        ee = ins[2][...].astype(F32)
        outs[0][...] = (d * ee * s * (1.0 - s)).astype(BF16)
        outs[1][...] = (d * s).astype(BF16)

    return _rowwise(name, fn, [(dh, "row"), (q, "row"), (e, "row")], [(D, BF16), (D, BF16)])


def _cf_norm_bwd(name, v2, g, dv4):
    D = v2.shape[1]

    def fn(i, n, ins, outs, accs, scr):
        x = ins[0][...]
        gg = ins[1][...]
        r = _rms_r(x)
        xn = x * r
        v3 = xn * gg
        s = _sigmoid(v3)
        dv3 = ins[2][...].astype(F32) * (s * (1.0 + v3 * (1.0 - s)))
        t = dv3 * gg
        dv2 = r * (t - xn * jnp.mean(t * xn, axis=-1, keepdims=True))
        outs[0][...] = dv2
        accs[0][...] += _colsum(dv3 * xn)
        accs[1][...] += _colsum(dv2)

    return _rowwise(name, fn, [(v2, "row"), (g, "vec"), (dv4, "row")], [(D, F32)], accs=[(1, D), (1, D)])


def _chunks(tt, width):
    cc = min(CONV_LANE_CHUNK, width)
    rc = min(CONV_ROW_CHUNK, tt)
    for c0 in range(0, width, cc):
        for r0 in range(0, tt, rc):
            yield r0, rc, c0, cc


def _n_shifts(n_taps):
    return min(SUBLANES - 1, n_taps - 1)


def _shifted_scratch(n_taps, rows, width):
    return pltpu.VMEM((_n_shifts(n_taps), rows, width), F32)


def _shift_window(win_ref, sh_ref, n_taps, sign):
    rows = win_ref.shape[0] - SUBLANES
    width = win_ref.shape[1]
    cc = min(CONV_LANE_CHUNK, width)
    for b in range(1, _n_shifts(n_taps) + 1):
        off = SUBLANES - b if sign < 0 else b
        for c0 in range(0, width, cc):
            sh_ref[b - 1, 0:rows, c0:c0 + cc] = win_ref[off:off + rows, c0:c0 + cc]


def _tap(win_ref, sh_ref, base, sign, s, r0, rc, c0, cc):
    a, b = divmod(s, SUBLANES)
    if b == 0:
        row = base + r0 + sign * SUBLANES * a
        return win_ref[row:row + rc, c0:c0 + cc]
    row = base + r0 - SUBLANES * (a + 1) if sign < 0 else base + r0 + SUBLANES * a
    return sh_ref[b - 1, row:row + rc, c0:c0 + cc]


def _fir(win_ref, sh_ref, w_ref, n_taps, base, sign, tt, width, emit):
    for r0, rc, c0, cc in _chunks(tt, width):
        acc = jnp.zeros((rc, cc), F32)
        for k in range(n_taps):
            acc = acc + w_ref[k:k + 1, c0:c0 + cc] * _tap(win_ref, sh_ref, base, sign, n_taps - 1 - k, r0, rc, c0, cc)
        emit(r0, rc, c0, cc, acc)


def _fir_wgrad(d_ref, win_ref, sh_ref, dw8_ref, n_taps, pad, tt, width):
    for c0 in range(0, width, min(CONV_LANE_CHUNK, width)):
        cc = min(CONV_LANE_CHUNK, width)
        rc = min(CONV_ROW_CHUNK, tt)
        for k in range(n_taps):
            acc = jnp.zeros((SUBLANES, cc), F32)
            for r0 in range(0, tt, rc):
                prod = d_ref[r0:r0 + rc, c0:c0 + cc] * _tap(win_ref, sh_ref, pad, -1, n_taps - 1 - k, r0, rc, c0, cc)
                for q in range(0, rc, SUBLANES):
                    acc = acc + prod[q:q + SUBLANES]
            dw8_ref[SUBLANES * k:SUBLANES * (k + 1), c0:c0 + cc] += acc


def _glu(blk, D):
    return blk[:, :D].astype(F32) * _sigmoid(blk[:, D:].astype(F32))


CF_PAD = 32
SC_PAD = 16


def _cf_conv_fwd(name, a, w_dw, b_dw, g_cf):
    T, D2 = a.shape
    D = D2 // 2
    K = w_dw.shape[0]
    tt = _tile(T, CONV_TILE_ROWS)

    def fn(i, n, ins, outs, accs, scr):
        a_ref, prev_ref, w_ref, b_ref, g_ref = ins
        win_ref, v2_ref, sh_ref = scr
        win_ref[0:CF_PAD, :] = jnp.where(i > 0, _glu(prev_ref[...], D), 0.0)
        win_ref[CF_PAD:CF_PAD + tt, :] = _glu(a_ref[...], D)
        _shift_window(win_ref, sh_ref, K, -1)

        def emit(r0, rc, c0, cc, acc):
            v2_ref[r0:r0 + rc, c0:c0 + cc] = acc + b_ref[:, c0:c0 + cc]

        _fir(win_ref, sh_ref, w_ref, K, CF_PAD, -1, tt, D, emit)
        v2 = v2_ref[...]
        v3 = v2 * _rms_r(v2) * g_ref[...]
        outs[0][...] = v2
        outs[1][...] = (v3 * _sigmoid(v3)).astype(BF16)

    return _rowwise(name, fn, [(a, "row"), (a, ("prev", CF_PAD)), (w_dw, "vec"), (b_dw, "vec"), (g_cf, "vec")],
                    [(D, F32), (D, BF16)],
                    scratch=[pltpu.VMEM((CF_PAD + tt, D), F32), pltpu.VMEM((tt, D), F32),
                             _shifted_scratch(K, CF_PAD + tt, D)], tt=tt)


def _cf_conv_bwd(name, dv2, a, w_dw):
    T, D2 = a.shape
    D = D2 // 2
    K = w_dw.shape[0]
    tt = _tile(T, CONV_TILE_ROWS)

    def fn(i, n, ins, outs, accs, scr):
        d_ref, dnext_ref, a_ref, prev_ref, w_ref = ins
        v1win_ref, dwin_ref, dv1_ref, dw8_ref, v1sh_ref, dsh_ref = scr

        @pl.when(i == 0)
        def _():
            dw8_ref[...] = jnp.zeros_like(dw8_ref)

        v1win_ref[0:CF_PAD, :] = jnp.where(i > 0, _glu(prev_ref[...], D), 0.0)
        v1win_ref[CF_PAD:CF_PAD + tt, :] = _glu(a_ref[...], D)
        dwin_ref[0:tt, :] = d_ref[...]
        dwin_ref[tt:tt + CF_PAD, :] = jnp.where(i < n - 1, dnext_ref[...], 0.0)
        _shift_window(v1win_ref, v1sh_ref, K, -1)
        _shift_window(dwin_ref, dsh_ref, K, 1)

        def emit(r0, rc, c0, cc, acc):
            dv1_ref[r0:r0 + rc, c0:c0 + cc] = acc

        _fir(dwin_ref, dsh_ref, w_ref, K, 0, 1, tt, D, emit)
        _fir_wgrad(d_ref, v1win_ref, v1sh_ref, dw8_ref, K, CF_PAD, tt, D)

        blk = a_ref[...]
        val = blk[:, :D].astype(F32)
        sg = _sigmoid(blk[:, D:].astype(F32))
        dv1 = dv1_ref[...]
        dval = dv1 * sg
        dgate = dv1 * val * sg * (1.0 - sg)
        outs[0][:, :D] = dval.astype(BF16)
        outs[0][:, D:] = dgate.astype(BF16)
        accs[1][:, :D] += _colsum(dval)
        accs[1][:, D:] += _colsum(dgate)

        @pl.when(i == n - 1)
        def _():
            for k in range(K):
                accs[0][k:k + 1, :] = _colsum(dw8_ref[SUBLANES * k:SUBLANES * (k + 1), :])

    return _rowwise(name, fn, [(dv2, "row"), (dv2, ("next", CF_PAD)), (a, "row"), (a, ("prev", CF_PAD)),
                               (w_dw, "vec")],
                    [(D2, BF16)], accs=[(K, D), (1, D2)],
                    scratch=[pltpu.VMEM((CF_PAD + tt, D), F32), pltpu.VMEM((tt + CF_PAD, D), F32),
                             pltpu.VMEM((tt, D), F32), pltpu.VMEM((SUBLANES * K, D), F32),
                             _shifted_scratch(K, CF_PAD + tt, D), _shifted_scratch(K, tt + CF_PAD, D)], tt=tt)


def _sc_conv_fwd(name, bcv, w_conv):
    T, D3 = bcv.shape
    D = D3 // 3
    K = w_conv.shape[0]
    tt = _tile(T, CONV_TILE_ROWS)

    def cv_of(blk):
        return blk[:, D:2 * D].astype(F32) * blk[:, 2 * D:].astype(F32)

    def fn(i, n, ins, outs, accs, scr):
        x_ref, prev_ref, w_ref = ins
        win_ref, cc_ref, sh_ref = scr
        win_ref[0:SC_PAD, :] = jnp.where(i > 0, cv_of(prev_ref[...]), 0.0)
        win_ref[SC_PAD:SC_PAD + tt, :] = cv_of(x_ref[...])
        _shift_window(win_ref, sh_ref, K, -1)

        def emit(r0, rc, c0, cw, acc):
            cc_ref[r0:r0 + rc, c0:c0 + cw] = acc

        _fir(win_ref, sh_ref, w_ref, K, SC_PAD, -1, tt, D, emit)
        outs[0][...] = (x_ref[:, :D].astype(F32) * cc_ref[...]).astype(BF16)

    return _rowwise(name, fn, [(bcv, "row"), (bcv, ("prev", SC_PAD)), (w_conv, "vec")], [(D, BF16)],
                    scratch=[pltpu.VMEM((SC_PAD + tt, D), F32), pltpu.VMEM((tt, D), F32),
                             _shifted_scratch(K, SC_PAD + tt, D)], tt=tt)


def _sc_conv_bwd(name, dy, bcv, w_conv):
    T, D3 = bcv.shape
    D = D3 // 3
    K = w_conv.shape[0]
    tt = _tile(T, CONV_TILE_ROWS)

    def cv_of(blk):
        return blk[:, D:2 * D].astype(F32) * blk[:, 2 * D:].astype(F32)

    def fn(i, n, ins, outs, accs, scr):
        dy_ref, dynext_ref, x_ref, prev_ref, next_ref, w_ref = ins
        cvwin_ref, dccwin_ref, tmp_ref, dw8_ref, cvsh_ref, dccsh_ref = scr

        @pl.when(i == 0)
        def _():
            dw8_ref[...] = jnp.zeros_like(dw8_ref)

        cvwin_ref[0:SC_PAD, :] = jnp.where(i > 0, cv_of(prev_ref[...]), 0.0)
        cvwin_ref[SC_PAD:SC_PAD + tt, :] = cv_of(x_ref[...])
        _shift_window(cvwin_ref, cvsh_ref, K, -1)

        def emit_cc(r0, rc, c0, cw, acc):
            tmp_ref[r0:r0 + rc, c0:c0 + cw] = acc

        _fir(cvwin_ref, cvsh_ref, w_ref, K, SC_PAD, -1, tt, D, emit_cc)
        dy_v = dy_ref[...].astype(F32)
        outs[0][:, :D] = (dy_v * tmp_ref[...]).astype(BF16)
        dccwin_ref[0:tt, :] = dy_v * x_ref[:, :D].astype(F32)
        dccwin_ref[tt:tt + SC_PAD, :] = jnp.where(
            i < n - 1, dynext_ref[...].astype(F32) * next_ref[:, :D].astype(F32), 0.0)

        _shift_window(dccwin_ref, dccsh_ref, K, 1)

        def emit_dcv(r0, rc, c0, cw, acc):
            tmp_ref[r0:r0 + rc, c0:c0 + cw] = acc

        _fir(dccwin_ref, dccsh_ref, w_ref, K, 0, 1, tt, D, emit_dcv)
        _fir_wgrad(dccwin_ref, cvwin_ref, cvsh_ref, dw8_ref, K, SC_PAD, tt, D)
        dcv = tmp_ref[...]
        outs[0][:, D:2 * D] = (dcv * x_ref[:, 2 * D:].astype(F32)).astype(BF16)
        outs[0][:, 2 * D:] = (dcv * x_ref[:, D:2 * D].astype(F32)).astype(BF16)

        @pl.when(i == n - 1)
        def _():
            for k in range(K):
                accs[0][k:k + 1, :] = _colsum(dw8_ref[SUBLANES * k:SUBLANES * (k + 1), :])

    return _rowwise(name, fn, [(dy, "row"), (dy, ("next", SC_PAD)), (bcv, "row"), (bcv, ("prev", SC_PAD)),
                               (bcv, ("next", SC_PAD)), (w_conv, "vec")],
                    [(D3, BF16)], accs=[(K, D)],
                    scratch=[pltpu.VMEM((SC_PAD + tt, D), F32), pltpu.VMEM((tt + SC_PAD, D), F32),
                             pltpu.VMEM((tt, D), F32), pltpu.VMEM((SUBLANES * K, D), F32),
                             _shifted_scratch(K, SC_PAD + tt, D), _shifted_scratch(K, tt + SC_PAD, D)], tt=tt)


def _place():
    x, y, c = lax.axis_index("x"), lax.axis_index("y"), lax.axis_index("c")
    chips = [(1 - x, y), (x, 1 - y), (1 - x, 1 - y)]
    return x, y, c, 2 * x + y, chips, (x, y, 1 - c)


def _half(rows, which):
    return pl.ds(pl.multiple_of(which * (rows // 2), SUBLANES), rows // 2)


_HBM = pl.BlockSpec(memory_space=pl.ANY)


def _gather_shards(name, items):
    n = len(items)
    shapes = [a.shape[-2:] for a, _ in items]

    def body(*refs):
        srcs, outs = refs[:n], refs[n:2 * n]
        send1, recv1, send2, recv2, lsem = refs[2 * n:]
        x, y, c, k, chips, sib = _place()

        def shard(i):
            return srcs[i] if items[i][1] is None else srcs[i].at[items[i][1]]

        started, locs = [], []
        for i in range(n):
            rows = shapes[i][0]
            lc = pltpu.make_async_copy(shard(i), outs[i].at[k], lsem.at[i])
            lc.start()
            locs.append(lc)
            for j, (cx, cy) in enumerate(chips):
                cp = pltpu.make_async_remote_copy(
                    src_ref=shard(i).at[_half(rows, c)], dst_ref=outs[i].at[k, _half(rows, c)],
                    send_sem=send1.at[i, j], recv_sem=recv1.at[i, j], device_id=(cx, cy, c), device_id_type=MESH)
                cp.start()
                started.append(cp)
        for i in range(n):
            rows = shapes[i][0]
            for j, (cx, cy) in enumerate(chips):
                blk = outs[i].at[2 * cx + cy, _half(rows, c)]
                pltpu.make_async_remote_copy(
                    src_ref=blk, dst_ref=blk, send_sem=send1.at[i, j], recv_sem=recv1.at[i, j],
                    device_id=(cx, cy, c), device_id_type=MESH).wait_recv()
                fw = pltpu.make_async_remote_copy(
                    src_ref=blk, dst_ref=blk, send_sem=send2.at[i, j], recv_sem=recv2.at[i, j],
                    device_id=sib, device_id_type=MESH)
                fw.start()
                started.append(fw)
        for i in range(n):
            rows = shapes[i][0]
            for j, (cx, cy) in enumerate(chips):
                blk = outs[i].at[2 * cx + cy, _half(rows, 1 - c)]
                pltpu.make_async_remote_copy(
                    src_ref=blk, dst_ref=blk, send_sem=send2.at[i, j], recv_sem=recv2.at[i, j],
                    device_id=sib, device_id_type=MESH).wait_recv()
        for cp in started:
            cp.wait_send()
        for lc in locs:
            lc.wait()

    return pl.pallas_call(
        body, name=name,
        in_specs=[_HBM] * n, out_specs=[_HBM] * n,
        out_shape=[jax.ShapeDtypeStruct((N_SHARDS,) + tuple(s), a.dtype) for s, (a, _) in zip(shapes, items)],
        scratch_shapes=[pltpu.SemaphoreType.DMA((n, 3))] * 4 + [pltpu.SemaphoreType.DMA((n,))],
    )(*[a for a, _ in items])


def _cast_place(name, w, layer, pos):
    _, R, C = w.shape
    tr = _tile(R, 256)

    def body(x_ref, y_ref, c_ref, w_ref, o_ref):
        o_ref[...] = w_ref[...].astype(BF16)

    return pl.pallas_call(
        body, name=name,
        grid_spec=pltpu.PrefetchScalarGridSpec(
            num_scalar_prefetch=3, grid=(R // tr,),
            in_specs=[pl.BlockSpec((None, tr, C), lambda r, xr, yr, cr: (layer, r, 0))],
            out_specs=pl.BlockSpec((None, tr, C), lambda r, xr, yr, cr: (2 * xr[0] + yr[0], r, 0))),
        out_shape=jax.ShapeDtypeStruct((N_SHARDS, R, C), BF16),
        compiler_params=_cparams("parallel"),
    )(*pos, w)


_IN_HBM = pl.BlockSpec(memory_space=pltpu.HBM)
_SEM = pl.BlockSpec(memory_space=pltpu.SEMAPHORE)
_SPLIT_COPY_PARAMS = pltpu.CompilerParams(has_side_effects=pltpu.SideEffectType.DATAFLOW_SIDE_EFFECTING)


def _in_hbm(a):
    return pltpu.with_memory_space_constraint(a, pltpu.HBM)


def _gather_copy(ref, i, j, chip_xy, c, k_src, rows, send, recv):
    blk = ref.at[k_src, _half(rows, c)]
    return pltpu.make_async_remote_copy(
        src_ref=blk, dst_ref=blk, send_sem=send.at[3 * i + j], recv_sem=recv.at[3 * i + j],
        device_id=(*chip_xy, c), device_id_type=MESH)


def _gather_start(name, bufs, after):
    n = len(bufs)

    def body(*refs):
        ins = refs[:n]
        send, recv = refs[n + 1], refs[n + 2]
        token = refs[-1]
        x, y, c, k, chips, sib = _place()
        for i in range(n):
            for j, chip_xy in enumerate(chips):
                _gather_copy(ins[i], i, j, chip_xy, c, k, bufs[i].shape[1], send, recv).start()
        token[...] = jnp.zeros_like(token)

    res = pl.pallas_call(
        body, name=name,
        in_specs=[_IN_HBM] * n + [_HBM],
        out_specs=[_SEM, _SEM] + [_IN_HBM] * n + [pl.BlockSpec(memory_space=pltpu.VMEM)],
        out_shape=[pltpu.SemaphoreType.DMA((3 * n,)), pltpu.SemaphoreType.DMA((3 * n,))]
        + [pltpu.HBM(b.shape, b.dtype) for b in bufs] + [jax.ShapeDtypeStruct((SUBLANES, 128), F32)],
        input_output_aliases={i: 2 + i for i in range(n)},
        compiler_params=_SPLIT_COPY_PARAMS,
    )(*[_in_hbm(b) for b in bufs], after)
    return res[0], res[1], list(res[2:2 + n]), res[-1]


def _gather_wait(name, bufs, send, recv, after):
    n = len(bufs)

    def body(*refs):
        ins = refs[:n]
        send_ref, recv_ref = refs[n], refs[n + 1]
        x, y, c, k, chips, sib = _place()
        for i in range(n):
            for j, chip_xy in enumerate(chips):
                rows = bufs[i].shape[1]
                _gather_copy(ins[i], i, j, chip_xy, c, k, rows, send_ref, recv_ref).wait_send()
                _gather_copy(ins[i], i, j, chip_xy, c, 2 * chip_xy[0] + chip_xy[1], rows, send_ref, recv_ref).wait_recv()

    return pl.pallas_call(
        body, name=name,
        in_specs=[_IN_HBM] * n + [_SEM, _SEM, _HBM],
        out_specs=[_IN_HBM] * n,
        out_shape=[pltpu.HBM(b.shape, b.dtype) for b in bufs],
        input_output_aliases={i: i for i in range(n)},
        compiler_params=_SPLIT_COPY_PARAMS,
    )(*bufs, send, recv, after)


def _gather_forward(name, bufs):
    n = len(bufs)

    def body(*refs):
        outs = refs[n:2 * n]
        send, recv = refs[2 * n:]
        x, y, c, k, chips, sib = _place()
        started = []
        for i in range(n):
            rows = bufs[i].shape[1]
            for j, (cx, cy) in enumerate(chips):
                blk = outs[i].at[2 * cx + cy, _half(rows, c)]
                fw = pltpu.make_async_remote_copy(
                    src_ref=blk, dst_ref=blk, send_sem=send.at[i, j], recv_sem=recv.at[i, j],
                    device_id=sib, device_id_type=MESH)
                fw.start()
                started.append(fw)
        for i in range(n):
            rows = bufs[i].shape[1]
            for j, (cx, cy) in enumerate(chips):
                blk = outs[i].at[2 * cx + cy, _half(rows, 1 - c)]
                pltpu.make_async_remote_copy(
                    src_ref=blk, dst_ref=blk, send_sem=send.at[i, j], recv_sem=recv.at[i, j],
                    device_id=sib, device_id_type=MESH).wait_recv()
        for cp in started:
            cp.wait_send()

    return pl.pallas_call(
        body, name=name, in_specs=[_HBM] * n, out_specs=[_HBM] * n,
        out_shape=[jax.ShapeDtypeStruct(b.shape, b.dtype) for b in bufs],
        input_output_aliases={i: i for i in range(n)},
        scratch_shapes=[pltpu.SemaphoreType.DMA((n, 3))] * 2,
    )(*bufs)


def _swap_halves(name, parts):
    n = len(parts)

    def body(*refs):
        srcs, outs = refs[:n], refs[n:2 * n]
        send, recv = refs[2 * n:]
        x, y, c, k, chips, sib = _place()
        cps = []
        for i in range(n):
            rows = parts[i].shape[1]
            cp = pltpu.make_async_remote_copy(
                src_ref=srcs[i].at[:, _half(rows, 1 - c)], dst_ref=outs[i],
                send_sem=send.at[i], recv_sem=recv.at[i], device_id=sib, device_id_type=MESH)
            cp.start()
            cps.append(cp)
        for cp in cps:
            cp.wait()

    return pl.pallas_call(
        body, name=name, in_specs=[_HBM] * n, out_specs=[_HBM] * n,
        out_shape=[jax.ShapeDtypeStruct((p.shape[0], p.shape[1] // 2, p.shape[2]), p.dtype) for p in parts],
        scratch_shapes=[pltpu.SemaphoreType.DMA((n,))] * 2,
    )(*parts)


def _sibling_start(name, bufs, n_copies, make):
    nb = len(bufs)

    def body(*refs):
        send, recv = refs[nb], refs[nb + 1]
        token = refs[-1]
        x, y, c, k, chips, sib = _place()
        for cp in make(refs[:nb], c, sib, send, recv):
            cp.start()
        token[...] = jnp.zeros_like(token)

    res = pl.pallas_call(
        body, name=name,
        in_specs=[_IN_HBM] * nb,
        out_specs=[_SEM, _SEM] + [_IN_HBM] * nb + [pl.BlockSpec(memory_space=pltpu.VMEM)],
        out_shape=[pltpu.SemaphoreType.DMA((n_copies,)), pltpu.SemaphoreType.DMA((n_copies,))]
        + [pltpu.HBM(b.shape, b.dtype) for b in bufs] + [jax.ShapeDtypeStruct((SUBLANES, 128), F32)],
        input_output_aliases={i: 2 + i for i in range(nb)},
        compiler_params=_SPLIT_COPY_PARAMS,
    )(*[_in_hbm(b) for b in bufs])
    return res[0], res[1], list(res[2:2 + nb]), res[-1]


def _sibling_wait(name, bufs, send, recv, make, afters):
    nb = len(bufs)

    def body(*refs):
        x, y, c, k, chips, sib = _place()
        for cp in make(refs[:nb], c, sib, refs[nb], refs[nb + 1]):
            cp.wait_send()
            cp.wait_recv()

    return list(pl.pallas_call(
        body, name=name,
        in_specs=[_IN_HBM] * nb + [_SEM, _SEM] + [_HBM] * len(afters),
        out_specs=[_IN_HBM] * nb,
        out_shape=[pltpu.HBM(b.shape, b.dtype) for b in bufs],
        input_output_aliases={i: i for i in range(nb)},
        compiler_params=_SPLIT_COPY_PARAMS,
    )(*bufs, send, recv, *afters))


def _swap_copies(parts):
    n = len(parts)

    def make(refs, c, sib, send, recv):
        return [pltpu.make_async_remote_copy(
            src_ref=refs[i].at[:, _half(parts[i].shape[1], 1 - c)], dst_ref=refs[n + i],
            send_sem=send.at[i], recv_sem=recv.at[i], device_id=sib, device_id_type=MESH) for i in range(n)]

    return make


def _join_copies(fulls):
    def make(refs, c, sib, send, recv):
        cps = []
        for i, f in enumerate(fulls):
            blk = refs[i].at[_half(f.shape[0], c)]
            cps.append(pltpu.make_async_remote_copy(
                src_ref=blk, dst_ref=blk, send_sem=send.at[i], recv_sem=recv.at[i],
                device_id=sib, device_id_type=MESH))
        return cps

    return make


def _scatter_copy(src_ref, land_ref, i, j, chip_xy, c, send, recv):
    return pltpu.make_async_remote_copy(
        src_ref=src_ref.at[2 * chip_xy[0] + chip_xy[1]], dst_ref=land_ref.at[j],
        send_sem=send.at[3 * i + j], recv_sem=recv.at[3 * i + j], device_id=(*chip_xy, c), device_id_type=MESH)


def _scatter_start(name, sums):
    n = len(sums)
    lands = [lax.empty((3,) + s.shape[1:], s.dtype) for s in sums]

    def body(*refs):
        srcs, lnds = refs[:n], refs[n:2 * n]
        send, recv = refs[2 * n], refs[2 * n + 1]
        token = refs[-1]
        x, y, c, k, chips, sib = _place()
        for i in range(n):
            for j, chip_xy in enumerate(chips):
                _scatter_copy(srcs[i], lnds[i], i, j, chip_xy, c, send, recv).start()
        token[...] = jnp.zeros_like(token)

    res = pl.pallas_call(
        body, name=name,
        in_specs=[_IN_HBM] * (2 * n),
        out_specs=[_SEM, _SEM] + [_IN_HBM] * (2 * n) + [pl.BlockSpec(memory_space=pltpu.VMEM)],
        out_shape=[pltpu.SemaphoreType.DMA((3 * n,)), pltpu.SemaphoreType.DMA((3 * n,))]
        + [pltpu.HBM(a.shape, a.dtype) for a in list(sums) + lands] + [jax.ShapeDtypeStruct((SUBLANES, 128), F32)],
        input_output_aliases={i: 2 + i for i in range(2 * n)},
        compiler_params=_SPLIT_COPY_PARAMS,
    )(*[_in_hbm(a) for a in list(sums) + lands])
    return res[0], res[1], list(res[2:2 + n]), list(res[2 + n:2 + 2 * n]), res[-1]


def _scatter_wait(name, sums, lands, send, recv, afters):
    n = len(sums)

    def body(*refs):
        srcs, lnds = refs[:n], refs[n:2 * n]
        send_ref, recv_ref = refs[2 * n], refs[2 * n + 1]
        x, y, c, k, chips, sib = _place()
        for i in range(n):
            for j, chip_xy in enumerate(chips):
                cp = _scatter_copy(srcs[i], lnds[i], i, j, chip_xy, c, send_ref, recv_ref)
                cp.wait_send()
                cp.wait_recv()

    res = pl.pallas_call(
        body, name=name,
        in_specs=[_IN_HBM] * (2 * n) + [_SEM, _SEM] + [_HBM] * len(afters),
        out_specs=[_IN_HBM] * (2 * n),
        out_shape=[pltpu.HBM(a.shape, a.dtype) for a in list(sums) + list(lands)],
        input_output_aliases={i: i for i in range(2 * n)},
        compiler_params=_SPLIT_COPY_PARAMS,
    )(*sums, *lands, send, recv, *afters)
    return list(res[:n]), list(res[n:])


def _sum_over_devices(name, buf, loss_row, afters):
    R, D = buf.shape
    n_after = len(afters)

    def body(x_ref, *rest):
        all_ref, tot_ref, loss_ref, send_sems, recv_sems, local_sem = rest[n_after:]
        x, y, c, k, chips, sib = _place()
        me = (x, y, c)

        def block(px, py, pc):
            return all_ref.at[4 * px + 2 * py + pc]

        def copy(kk, blk, to, src=None):
            return pltpu.make_async_remote_copy(
                src_ref=block(*blk) if src is None else src, dst_ref=block(*blk),
                send_sem=send_sems.at[kk], recv_sem=recv_sems.at[kk], device_id=to, device_id_type=MESH)

        mine = pltpu.make_async_copy(x_ref, block(*me), local_sem)
        mine.start()
        first = [copy(0, me, sib, src=x_ref)]
        first += [copy(1 + j, me, (*chip, c), src=x_ref) for j, chip in enumerate(chips)]
        for cp in first:
            cp.start()
        passed = [copy(4 + j, (*chip, c), sib) for j, chip in enumerate(chips)]
        for j, chip in enumerate(chips):
            copy(1 + j, (*chip, c), me).wait_recv()
            passed[j].start()
        copy(0, sib, me).wait_recv()
        for j, chip in enumerate(chips):
            copy(4 + j, (*chip, 1 - c), me).wait_recv()
        for cp in first + passed:
            cp.wait_send()
        mine.wait()
        rc = _tile(R, 32)
        for r0 in range(0, R, rc):
            tot = all_ref[0, r0:r0 + rc, :]
            for d in range(1, N_DEVICES):
                tot = tot + all_ref[d, r0:r0 + rc, :]
            tot_ref[r0:r0 + rc, :] = tot
        loss = 0.5 * jnp.sum(tot_ref[loss_row:loss_row + 1, :]) / D
        loss_ref[...] = jnp.full(loss_ref.shape, loss, F32)

    vm = pl.BlockSpec(memory_space=pltpu.VMEM)
    return pl.pallas_call(
        body, name=name, in_specs=[vm] + [_HBM] * n_after, out_specs=[vm, vm, vm],
        out_shape=[jax.ShapeDtypeStruct((N_DEVICES, R, D), F32), jax.ShapeDtypeStruct((R, D), F32),
                   jax.ShapeDtypeStruct((SUBLANES, 128), F32)],
        scratch_shapes=[pltpu.SemaphoreType.DMA((7,)), pltpu.SemaphoreType.DMA((7,)), pltpu.SemaphoreType.DMA],
        compiler_params=pltpu.CompilerParams(vmem_limit_bytes=V7X_VMEM_LIMIT_BYTES),
    )(buf, *afters)[1:]


def _add_my_half(name, part, got, pos):
    S, R, C = part.shape
    R2 = R // 2
    tr = _tile(R2, 512)
    q = R2 // tr

    def body(x_ref, y_ref, c_ref, p_ref, g_ref, o_ref):
        o_ref[...] = (p_ref[...].astype(F32) + g_ref[...].astype(F32)).astype(o_ref.dtype)

    return pl.pallas_call(
        body, name=name,
        grid_spec=pltpu.PrefetchScalarGridSpec(
            num_scalar_prefetch=3, grid=(S, q),
            in_specs=[pl.BlockSpec((None, tr, C), lambda s, r, xr, yr, cr: (s, cr[0] * q + r, 0)),
                      pl.BlockSpec((None, tr, C), lambda s, r, xr, yr, cr: (s, r, 0))],
            out_specs=pl.BlockSpec((None, tr, C), lambda s, r, xr, yr, cr: (s, r, 0))),
        out_shape=jax.ShapeDtypeStruct((S, R2, C), BF16),
        compiler_params=_cparams("parallel", "parallel"),
    )(*pos, part, got)


def _add_owner(name, sums, got, pos):
    _, R2, C = sums.shape
    tr = _tile(R2, 512)
    q = R2 // tr

    def body(x_ref, y_ref, c_ref, s_ref, g_ref, o_ref):
        acc = s_ref[...].astype(F32)
        for j in range(3):
            acc = acc + g_ref[j].astype(F32)
        o_ref[...] = acc

    return pl.pallas_call(
        body, name=name,
        grid_spec=pltpu.PrefetchScalarGridSpec(
            num_scalar_prefetch=3, grid=(q,),
            in_specs=[pl.BlockSpec((None, tr, C), lambda r, xr, yr, cr: (2 * xr[0] + yr[0], r, 0)),
                      pl.BlockSpec((3, tr, C), lambda r, xr, yr, cr: (0, r, 0))],
            out_specs=pl.BlockSpec((tr, C), lambda r, xr, yr, cr: (cr[0] * q + r, 0))),
        out_shape=jax.ShapeDtypeStruct((2 * R2, C), F32),
        compiler_params=_cparams("parallel"),
    )(*pos, sums, got)


def _adamw(name, w, m, v, g):
    R, C = w.shape
    tr = SUBLANES
    while 2 * tr * C <= ADAMW_TILE_ELEMS:
        tr *= 2
    tr = _tile(R, tr)
    bc1 = 1.0 - ADAM_B1 ** ADAM_STEP
    bc2 = 1.0 - ADAM_B2 ** ADAM_STEP

    def body(w_ref, m_ref, v_ref, g_ref, go_ref, d_ref, mo_ref, vo_ref):
        gg = g_ref[...]
        m2 = ADAM_B1 * m_ref[...] + (1.0 - ADAM_B1) * gg
        v2 = ADAM_B2 * v_ref[...] + (1.0 - ADAM_B2) * (gg * gg)
        go_ref[...] = gg
        mo_ref[...] = m2
        vo_ref[...] = v2
        d_ref[...] = -ADAM_LR * ((m2 / bc1) / (jnp.sqrt(v2 / bc2) + ADAM_EPS) + ADAM_WD * w_ref[...])

    spec = pl.BlockSpec((tr, C), lambda r: (r, 0))
    return pl.pallas_call(
        body, name=name, grid=(R // tr,), in_specs=[spec] * 4, out_specs=[spec] * 4,
        out_shape=[jax.ShapeDtypeStruct((R, C), F32)] * 4,
        compiler_params=_cparams("parallel"),
    )(w, m, v, g)


def _adamw_slab(name, w, m, v, g, layer, prev):
    L, R, C = w.shape
    tr = SUBLANES
    while 2 * tr * C <= ADAMW_TILE_ELEMS:
        tr *= 2
    tr = _tile(R, tr)
    bc1 = 1.0 - ADAM_B1 ** ADAM_STEP
    bc2 = 1.0 - ADAM_B2 ** ADAM_STEP

    def body(w_ref, m_ref, v_ref, g_ref, *rest):
        go_ref, d_ref, mo_ref, vo_ref = rest[-4:]
        gg = g_ref[...]
        m2 = ADAM_B1 * m_ref[...] + (1.0 - ADAM_B1) * gg
        v2 = ADAM_B2 * v_ref[...] + (1.0 - ADAM_B2) * (gg * gg)
        go_ref[...] = gg
        mo_ref[...] = m2
        vo_ref[...] = v2
        d_ref[...] = -ADAM_LR * ((m2 / bc1) / (jnp.sqrt(v2 / bc2) + ADAM_EPS) + ADAM_WD * w_ref[...])

    slab = pl.BlockSpec((None, tr, C), lambda r: (layer, r, 0))
    n_prev = 0 if prev is None else 4
    return pl.pallas_call(
        body, name=name, grid=(R // tr,),
        in_specs=[slab] * 3 + [pl.BlockSpec((tr, C), lambda r: (r, 0))] + [_HBM] * n_prev,
        out_specs=[slab] * 4,
        out_shape=[jax.ShapeDtypeStruct((L, R, C), F32)] * 4,
        input_output_aliases={4 + i: i for i in range(n_prev)},
        compiler_params=_cparams("parallel"),
    )(w, m, v, g, *(prev or ()))


def _swap_begin(tag, parts):
    lands = [lax.empty((p.shape[0], p.shape[1] // 2, p.shape[2]), p.dtype) for p in parts]
    return _sibling_start(f"rs_swap_start_{tag}", list(parts) + lands, len(parts), _swap_copies(parts))


def _reduce_begin(tag, early, swapping, late, pos, after):
    send, recv, bufs, _ = swapping
    bufs = _sibling_wait(f"rs_swap_wait_{tag}", bufs, send, recv, _swap_copies(early), [after])
    parts = bufs[:len(early)] + list(late)
    got = bufs[len(early):] + list(_swap_halves(f"rs_swap_{tag}", late))
    sums = [_add_my_half(f"rs_add2_{tag}_{i}", p, g, pos) for i, (p, g) in enumerate(zip(parts, got))]
    return _scatter_start(f"rs_scatter_start_{tag}", sums)


def _reduce_middle(tag, started, pos, afters):
    send, recv, sums, lands, _ = started
    sums, lands = _scatter_wait(f"rs_scatter_wait_{tag}", sums, lands, send, recv, afters)
    fulls = [_add_owner(f"rs_add4_{tag}_{i}", s, q, pos) for i, (s, q) in enumerate(zip(sums, lands))]
    return _sibling_start(f"rs_join_start_{tag}", fulls, len(fulls), _join_copies(fulls))


def _reduce_end(tag, joining, afters):
    send, recv, fulls, _ = joining
    return _sibling_wait(f"rs_join_wait_{tag}", fulls, send, recv, _join_copies(fulls), afters)


def _pad_rows(a):
    r = (-a.shape[0]) % SUBLANES
    return jnp.pad(a, ((0, r), (0, 0))) if r else a


def kernel(x, p, norm_mix, norm_mlp, norm_ple, cf_w_pw1, cf_b_pw1, cf_w_dw, cf_b_dw, cf_norm, cf_w_pw2, cf_b_pw2, sc_w_in, sc_w_conv, sc_w_out, mlp_w1, mlp_w2, ple_w_proj, ple_w_gate, norm_final, loss_target, m_norm_mix, m_norm_mlp, m_norm_ple, m_cf_w_pw1, m_cf_b_pw1, m_cf_w_dw, m_cf_b_dw, m_cf_norm, m_cf_w_pw2, m_cf_b_pw2, m_sc_w_in, m_sc_w_conv, m_sc_w_out, m_mlp_w1, m_mlp_w2, m_ple_w_proj, m_ple_w_gate, m_norm_final, v_norm_mix, v_norm_mlp, v_norm_ple, v_cf_w_pw1, v_cf_b_pw1, v_cf_w_dw, v_cf_b_dw, v_cf_norm, v_cf_w_pw2, v_cf_b_pw2, v_sc_w_in, v_sc_w_conv, v_sc_w_out, v_mlp_w1, v_mlp_w2, v_ple_w_proj, v_ple_w_gate, v_norm_final):
    T, D = x.shape[1], x.shape[2]
    KA, KB = cf_w_dw.shape[1], sc_w_conv.shape[1]
    chip = (2 * lax.axis_index("x") + lax.axis_index("y")).astype(jnp.int32)
    pos = tuple(lax.axis_index(ax).astype(jnp.int32).reshape(1) for ax in ("x", "y", "c"))

    params = dict(norm_mix=norm_mix, norm_mlp=norm_mlp, norm_ple=norm_ple, cf_w_pw1=cf_w_pw1, cf_b_pw1=cf_b_pw1,
                  cf_w_dw=cf_w_dw, cf_b_dw=cf_b_dw, cf_norm=cf_norm, cf_w_pw2=cf_w_pw2, cf_b_pw2=cf_b_pw2,
                  sc_w_in=sc_w_in, sc_w_conv=sc_w_conv, sc_w_out=sc_w_out, mlp_w1=mlp_w1, mlp_w2=mlp_w2,
                  ple_w_proj=ple_w_proj, ple_w_gate=ple_w_gate, norm_final=norm_final)
    mom1 = dict(norm_mix=m_norm_mix, norm_mlp=m_norm_mlp, norm_ple=m_norm_ple, cf_w_pw1=m_cf_w_pw1,
                cf_b_pw1=m_cf_b_pw1, cf_w_dw=m_cf_w_dw, cf_b_dw=m_cf_b_dw, cf_norm=m_cf_norm, cf_w_pw2=m_cf_w_pw2,
                cf_b_pw2=m_cf_b_pw2, sc_w_in=m_sc_w_in, sc_w_conv=m_sc_w_conv, sc_w_out=m_sc_w_out,
                mlp_w1=m_mlp_w1, mlp_w2=m_mlp_w2, ple_w_proj=m_ple_w_proj, ple_w_gate=m_ple_w_gate,
                norm_final=m_norm_final)
    mom2 = dict(norm_mix=v_norm_mix, norm_mlp=v_norm_mlp, norm_ple=v_norm_ple, cf_w_pw1=v_cf_w_pw1,
                cf_b_pw1=v_cf_b_pw1, cf_w_dw=v_cf_w_dw, cf_b_dw=v_cf_b_dw, cf_norm=v_cf_norm, cf_w_pw2=v_cf_w_pw2,
                cf_b_pw2=v_cf_b_pw2, sc_w_in=v_sc_w_in, sc_w_conv=v_sc_w_conv, sc_w_out=v_sc_w_out,
                mlp_w1=v_mlp_w1, mlp_w2=v_mlp_w2, ple_w_proj=v_ple_w_proj, ple_w_gate=v_ple_w_gate,
                norm_final=v_norm_final)

    big = ("cf_w_pw1", "cf_w_pw2", "sc_w_in", "sc_w_out", "mlp_w1", "mlp_w2", "ple_w_proj", "ple_w_gate")
    row_sharded = ("cf_w_pw2", "sc_w_out", "mlp_w2", "ple_w_gate")

    def layer_names(i):
        return (["cf_w_pw1", "cf_w_pw2"] if i % 2 == 0 else ["sc_w_in", "sc_w_out"]) + \
            ["mlp_w1", "mlp_w2", "ple_w_proj", "ple_w_gate"]

    def layer_index(i, name):
        return i if name.startswith(("mlp", "ple")) else i // 2

    def gather_begin(tag, i, names, after):
        bufs = [_cast_place(f"place_{nm}_{i}", params[nm], layer_index(i, nm), pos) for nm in names]
        return names, _gather_start(f"gather_start_{tag}", bufs, after)

    def gather_end(tag, begun, after):
        names, (send, recv, bufs, _) = begun
        bufs = _gather_wait(f"gather_wait_{tag}", bufs, send, recv, after)
        bufs = _gather_forward(f"gather_fwd_{tag}", bufs)
        return {nm: g4.reshape(1, N_SHARDS * g4.shape[1], g4.shape[2]) if nm in row_sharded else g4
                for nm, g4 in zip(names, bufs)}

    conv_small = jnp.concatenate([_pad_rows(cf_w_dw[j]) for j in range(cf_w_dw.shape[0])]
                                 + [_pad_rows(sc_w_conv[j]) for j in range(sc_w_conv.shape[0])], axis=0)
    conv_shards = _gather_shards("gather_conv_w", [(conv_small, None)])[0]
    conv_all = jnp.transpose(conv_shards, (1, 0, 2)).reshape(conv_small.shape[0], D)
    ka_pad = KA + (-KA) % SUBLANES
    kb_pad = KB + (-KB) % SUBLANES
    w_dw_full = [conv_all[j * ka_pad:j * ka_pad + KA] for j in range(cf_w_dw.shape[0])]
    off = cf_w_dw.shape[0] * ka_pad
    w_conv_full = [conv_all[off + j * kb_pad:off + j * kb_pad + KB] for j in range(sc_w_conv.shape[0])]

    def vec(a):
        return a.reshape(1, -1)

    ident = lambda acc: (acc,)

    h = x[0]
    saved = []
    first = gather_begin("0m", 0, layer_names(0)[:2], conv_shards)
    rest = gather_begin("0r", 0, layer_names(0)[2:], first[1][2][0])
    W = [gather_end("0m", first, h)]
    for i in range(DEPTH):
        j = i // 2
        wl = W[i]
        s = dict(h=h)
        g_mix = vec(norm_mix[i])
        if i + 1 < DEPTH:
            nxt = gather_begin(f"{i + 1}", i + 1, layer_names(i + 1), wl[layer_names(i)[0]])
            g_mix = g_mix + nxt[1][3][0, 0]
        s["u"] = _rms_fwd(f"rms_mix_{i}", h, g_mix)
        if i % 2 == 0:
            s["a"] = _mm_nn(f"cf_pw1_{i}", s["u"], wl["cf_w_pw1"], lambda acc, b: (acc + b,), [BF16],
                            extras=[(vec(cf_b_pw1[j]), "n")])[0]
            s["v2"], s["v4"] = _cf_conv_fwd(f"cf_conv_{i}", s["a"], w_dw_full[j], vec(cf_b_dw[j]), vec(cf_norm[j]))
            h1 = _mm_nn(f"cf_pw2_{i}", s["v4"], wl["cf_w_pw2"], lambda acc, b, r: (r + (acc + b),), [F32],
                        extras=[(vec(cf_b_pw2[j]), "n"), (h, "mn")])[0]
        else:
            s["bcv"] = _mm_nn(f"sc_in_{i}", s["u"], wl["sc_w_in"], ident, [BF16])[0]
            s["y"] = _sc_conv_fwd(f"sc_conv_{i}", s["bcv"], w_conv_full[j])[0]
            h1 = _mm_nn(f"sc_out_{i}", s["y"], wl["sc_w_out"], lambda acc, r: (r + acc,), [F32],
                        extras=[(h, "mn")])[0]
        s["h1"] = h1
        if i == 0:
            wl.update(gather_end("0r", rest, h1))
        s["u2"] = _rms_fwd(f"rms_mlp_{i}", h1, vec(norm_mlp[i]))
        s["z"], s["hd"] = _mm_nn(f"mlp_w1_{i}", s["u2"], wl["mlp_w1"],
                                 lambda acc: (acc, jnp.square(jnp.maximum(acc, 0.0))), [BF16, BF16])
        h2 = _mm_nn(f"mlp_w2_{i}", s["hd"], wl["mlp_w2"], lambda acc, r: (r + acc,), [F32], extras=[(h1, "mn")])[0]
        s["h2"] = h2
        s["n3"] = _rms_fwd(f"rms_ple_{i}", h2, vec(norm_ple[i]))
        s["p"] = p[i, 0]
        s["e"] = _mm_nn(f"ple_proj_{i}", s["p"], wl["ple_w_proj"], ident, [BF16])[0]
        h, s["q"] = _mm_nn(f"ple_gate_{i}", s["n3"], wl["ple_w_gate"],
                           lambda acc, r, e: (r + _sigmoid(acc) * e.astype(F32), acc), [F32, BF16],
                           extras=[(h2, "mn"), (s["e"], "mn")])
        saved.append(s)
        if i + 1 < DEPTH:
            W.append(gather_end(f"{i + 1}", nxt, h))

    dh, dh16, dg_final, loss_cols = _loss_bwd("loss_bwd", h, vec(norm_final), loss_target[0])
    small = {"norm_final": dg_final, "loss": loss_cols}
    adam = {nm: None for nm in big}

    def reduce_names(i):
        names = layer_names(i)
        return names[2:] + names[:2]

    def update_layer(i, joining, afters):
        for nm, g in zip(reduce_names(i), _reduce_end(f"{i}", joining, afters)):
            l = layer_index(i, nm)
            adam[nm] = _adamw_slab(f"adamw_{nm}_{l}", params[nm], mom1[nm], mom2[nm], g, l, adam[nm])

    def shard_major(names, parts):
        return [pt.reshape(N_SHARDS, pt.shape[1] // N_SHARDS, pt.shape[2]) if nm in row_sharded else pt
                for nm, pt in zip(names, parts)]

    scattered = joining = None
    for i in reversed(range(DEPTH)):
        j = i // 2
        wl, s = W[i], saved[i]
        dq, de = _ple_elem_bwd(f"ple_elem_bwd_{i}", dh, s["q"], s["e"])
        d_proj = _mm_tn(f"ple_proj_dw_{i}", s["p"], de, N_SHARDS)
        d_gate = _mm_tn(f"ple_gate_dw_{i}", s["n3"], dq, 1)
        dn3 = _mm_nt(f"ple_gate_dx_{i}", dq, wl["ple_w_gate"], ident, [F32])[0]
        dh, dh16, small[f"norm_ple_{i}"] = _rms_bwd(f"rms_ple_bwd_{i}", s["h2"], vec(norm_ple[i]), dn3, dh)

        d_w2 = _mm_tn(f"mlp_w2_dw_{i}", s["hd"], dh16, 1)
        dz = _mm_nt(f"mlp_w2_dx_{i}", dh16, wl["mlp_w2"],
                    lambda acc, z: (acc * (2.0 * jnp.maximum(z.astype(F32), 0.0)),), [BF16], extras=[s["z"]])[0]
        d_w1 = _mm_tn(f"mlp_w1_dw_{i}", s["u2"], dz, N_SHARDS)
        du2 = _mm_nt(f"mlp_w1_dx_{i}", dz, wl["mlp_w1"], ident, [F32])[0]
        early = shard_major(reduce_names(i)[:4], [d_w1, d_w2, d_proj, d_gate])
        swapping = _swap_begin(f"{i}", early)
        g_mlp = vec(norm_mlp[i]) + swapping[3][0, 0]
        if i % 2 == 0:
            dh, dh16, small[f"norm_mlp_{i}"], small[f"cf_b_pw2_{j}"] = _rms_bwd(
                f"rms_mlp_bwd_{i}", s["h1"], g_mlp, du2, dh, want_colsum=True)
            d_mix_out = _mm_tn(f"cf_pw2_dw_{i}", s["v4"], dh16, 1)
            dv4 = _mm_nt(f"cf_pw2_dx_{i}", dh16, wl["cf_w_pw2"], ident, [F32])[0]
            dv2, small[f"cf_norm_{j}"], small[f"cf_b_dw_{j}"] = _cf_norm_bwd(
                f"cf_norm_bwd_{i}", s["v2"], vec(cf_norm[j]), dv4)
            da, small[f"cf_w_dw_{j}"], db1 = _cf_conv_bwd(f"cf_conv_bwd_{i}", dv2, s["a"], w_dw_full[j])
            small[f"cf_b_pw1_{j}"] = db1.reshape(2, D)
            d_mix_in = _mm_tn(f"cf_pw1_dw_{i}", s["u"], da, N_SHARDS)
            du = _mm_nt(f"cf_pw1_dx_{i}", da, wl["cf_w_pw1"], ident, [F32])[0]
        else:
            dh, dh16, small[f"norm_mlp_{i}"] = _rms_bwd(f"rms_mlp_bwd_{i}", s["h1"], g_mlp, du2, dh)
            d_mix_out = _mm_tn(f"sc_out_dw_{i}", s["y"], dh16, 1)
            dy = _mm_nt(f"sc_out_dx_{i}", dh16, wl["sc_w_out"], ident, [F32])[0]
            da, small[f"sc_w_conv_{j}"] = _sc_conv_bwd(f"sc_conv_bwd_{i}", dy, s["bcv"], w_conv_full[j])
            d_mix_in = _mm_tn(f"sc_in_dw_{i}", s["u"], da, N_SHARDS)
            du = _mm_nt(f"sc_in_dx_{i}", da, wl["sc_w_in"], ident, [F32])[0]

        late = shard_major(reduce_names(i)[4:], [d_mix_in, d_mix_out])
        started = _reduce_begin(f"{i}", early, swapping, late, pos, du)
        token = started[4]
        dh, dh16, small[f"norm_mix_{i}"] = _rms_bwd(f"rms_mix_bwd_{i}", s["h"], vec(norm_mix[i]) + token[0, 0], du, dh)
        if joining is not None:
            update_layer(i + 2, joining, [token])
            joining = None
        if scattered is not None:
            joining = _reduce_middle(f"{i + 1}", scattered, pos, [token])
        scattered = started
    grad_x = dh.reshape(x.shape)

    order = sorted(small)
    pieces, where, row = [], {}, 0
    for nm in order:
        pc = _pad_rows(small[nm])
        where[nm] = (row, small[nm].shape[0])
        row += pc.shape[0]
        pieces.append(pc)
    update_layer(1, joining, [token])
    updated = [res[3] for res in adam.values() if res is not None]
    total, loss_tile = _sum_over_devices("small_allsum", jnp.concatenate(pieces, axis=0), where["loss"][0], updated)
    loss = loss_tile[0, 0]
    update_layer(0, _reduce_middle("0", scattered, pos, [total] + updated), [])

    def small_sum(nm):
        r0, nr = where[nm]
        return total[r0:r0 + nr]

    def my_cols(a):
        return lax.dynamic_slice_in_dim(a, chip * (D // N_SHARDS), D // N_SHARDS, axis=1)

    g_small = {
        "norm_mix": jnp.concatenate([small_sum(f"norm_mix_{i}") for i in range(DEPTH)], axis=0),
        "norm_mlp": jnp.concatenate([small_sum(f"norm_mlp_{i}") for i in range(DEPTH)], axis=0),
        "norm_ple": jnp.concatenate([small_sum(f"norm_ple_{i}") for i in range(DEPTH)], axis=0),
        "cf_b_pw1": jnp.stack([small_sum(f"cf_b_pw1_{j}").reshape(2 * D) for j in range(DEPTH // 2)]),
        "cf_w_dw": jnp.stack([my_cols(small_sum(f"cf_w_dw_{j}")) for j in range(DEPTH // 2)]),
        "cf_b_dw": jnp.concatenate([small_sum(f"cf_b_dw_{j}") for j in range(DEPTH // 2)], axis=0),
        "cf_norm": jnp.concatenate([small_sum(f"cf_norm_{j}") for j in range(DEPTH // 2)], axis=0),
        "cf_b_pw2": jnp.concatenate([small_sum(f"cf_b_pw2_{j}") for j in range(DEPTH // 2)], axis=0),
        "sc_w_conv": jnp.stack([my_cols(small_sum(f"sc_w_conv_{j}")) for j in range(DEPTH // 2)]),
        "norm_final": small_sum("norm_final").reshape(D),
    }

    names_out = ["norm_mix", "norm_mlp", "norm_ple", "cf_w_pw1", "cf_b_pw1", "cf_w_dw", "cf_b_dw", "cf_norm",
                 "cf_w_pw2", "cf_b_pw2", "sc_w_in", "sc_w_conv", "sc_w_out", "mlp_w1", "mlp_w2", "ple_w_proj",
                 "ple_w_gate", "norm_final"]
    grad, delta, new_m, new_v = {}, {}, {}, {}
    for nm in names_out:
        w = params[nm]
        if nm in big:
            grad[nm], delta[nm], new_m[nm], new_v[nm] = adam[nm]
            continue
        g = g_small[nm]
        cols = w.shape[-1] if w.ndim > 1 else w.shape[0]
        two_d = lambda a: a.reshape(-1, cols)
        res = _adamw(f"adamw_{nm}", two_d(w), two_d(mom1[nm]), two_d(mom2[nm]), two_d(g))
        grad[nm], delta[nm], new_m[nm], new_v[nm] = [r.reshape(w.shape) for r in res]

    return (loss, grad_x, *[grad[n] for n in names_out], *[delta[n] for n in names_out],
            *[new_m[n] for n in names_out], *[new_v[n] for n in names_out])
```

```python
import functools

import jax
import jax.numpy as jnp
from jax import lax
from jax.experimental import pallas as pl
from jax.experimental.pallas import tpu as pltpu

F32 = jnp.float32
BF16 = jnp.bfloat16

EPS = 1e-6
ADAM_LR = 0.001
ADAM_B1 = 0.9
ADAM_B2 = 0.999
ADAM_EPS = 1e-08
ADAM_WD = 0.01
ADAM_STEP = 10

DEPTH = 4
N_SHARDS = 4
N_DEVICES = 8
V7X_VMEM_LIMIT_BYTES = 56 * 1024 * 1024
SUBLANES = 8
MESH = pl.DeviceIdType.MESH

MM_TM = 1024
MM_TN = 1024
MM_TK = 2048
MM_TW_K = 1024
MM_TW_T = 4096
MM_TX_K = 1024
MM_TX_N = 2048
MM_TN_CROWDED = 512
MM_LONG_K = 8192
MM_TN_LONG_K = 256
ADAMW_TILE_ELEMS = 256 * 2048

CONV_ROW_CHUNK = 32
CONV_LANE_CHUNK = 512
CONV_TILE_ROWS = 128
ROW_TILE = 256


def _tile(dim, pref):
    if dim <= pref:
        return dim
    t = pref
    while dim % t:
        t //= 2
    return t


def _cparams(*sem):
    return pltpu.CompilerParams(dimension_semantics=sem, vmem_limit_bytes=V7X_VMEM_LIMIT_BYTES)


def _sigmoid(x):
    return 0.5 * (jnp.tanh(0.5 * x) + 1.0)


def _rms_r(x):
    return lax.rsqrt(jnp.mean(x * x, axis=-1, keepdims=True) + EPS)


def _mm_nn(name, a, b3, epilogue, out_dtypes, extras=()):
    M, K = a.shape
    S, Kb, Ns = b3.shape
    assert Kb == K
    N = S * Ns
    crowded = sum(kind == "mn" for _, kind in extras) > 1
    tm, tn, tk = _tile(M, MM_TM), _tile(Ns, MM_TN_CROWDED if crowded else MM_TN), _tile(K, MM_TK)
    if K >= MM_LONG_K:
        tn, tk = _tile(Ns, MM_TN_LONG_K), K
    per = Ns // tn
    nk = K // tk
    in_specs = [pl.BlockSpec((tm, tk), lambda i, j, k: (i, k)),
                pl.BlockSpec((None, tk, tn), lambda i, j, k: (j // per, k, j % per))]
    for _, kind in extras:
        if kind == "mn":
            in_specs.append(pl.BlockSpec((tm, tn), lambda i, j, k: (i, j)))
        else:
            in_specs.append(pl.BlockSpec((1, tn), lambda i, j, k: (0, j)))
    n_ex, n_o = len(extras), len(out_dtypes)

    def body(*refs):
        a_ref, b_ref = refs[:2]
        ex = refs[2:2 + n_ex]
        outs = refs[2 + n_ex:2 + n_ex + n_o]
        part = jnp.dot(a_ref[...].astype(BF16), b_ref[...], preferred_element_type=F32)

        def finish(acc):
            res = epilogue(acc, *[e[...] for e in ex])
            for r, o in zip(res, outs):
                o[...] = r.astype(o.dtype)

        if nk == 1:
            finish(part)
        else:
            acc_ref = refs[-1]
            k = pl.program_id(2)

            @pl.when(k == 0)
            def _():
                acc_ref[...] = part

            @pl.when(k > 0)
            def _():
                acc_ref[...] += part

            @pl.when(k == nk - 1)
            def _():
                finish(acc_ref[...])

    res = pl.pallas_call(
        body, name=name, grid=(M // tm, N // tn, nk),
        in_specs=in_specs,
        out_specs=[pl.BlockSpec((tm, tn), lambda i, j, k: (i, j)) for _ in out_dtypes],
        out_shape=[jax.ShapeDtypeStruct((M, N), dt) for dt in out_dtypes],
        scratch_shapes=[pltpu.VMEM((tm, tn), F32)] if nk > 1 else [],
        compiler_params=_cparams("parallel", "parallel", "arbitrary"),
    )(a, b3, *[e for e, _ in extras])
    return res


def _mm_nt(name, g, w3, epilogue, out_dtypes, extras=()):
    M, N = g.shape
    S, K, Ns = w3.shape
    assert S * Ns == N
    tm, tn, tkk = _tile(M, MM_TM), _tile(Ns, MM_TX_N), _tile(K, MM_TX_K)
    per = Ns // tn
    nn = N // tn
    n_ex, n_o = len(extras), len(out_dtypes)

    def body(*refs):
        g_ref, w_ref = refs[:2]
        ex = refs[2:2 + n_ex]
        outs = refs[2 + n_ex:2 + n_ex + n_o]
        part = lax.dot_general(g_ref[...].astype(BF16), w_ref[...], (((1,), (1,)), ((), ())),
                               preferred_element_type=F32)

        def finish(acc):
            res = epilogue(acc, *[e[...] for e in ex])
            for r, o in zip(res, outs):
                o[...] = r.astype(o.dtype)

        if nn == 1:
            finish(part)
        else:
            acc_ref = refs[-1]
            n = pl.program_id(2)

            @pl.when(n == 0)
            def _():
                acc_ref[...] = part

            @pl.when(n > 0)
            def _():
                acc_ref[...] += part

            @pl.when(n == nn - 1)
            def _():
                finish(acc_ref[...])

    return pl.pallas_call(
        body, name=name, grid=(M // tm, K // tkk, nn),
        in_specs=[pl.BlockSpec((tm, tn), lambda i, kk, n: (i, n)),
                  pl.BlockSpec((None, tkk, tn), lambda i, kk, n: (n // per, kk, n % per))]
        + [pl.BlockSpec((tm, tkk), lambda i, kk, n: (i, kk)) for _ in extras],
        out_specs=[pl.BlockSpec((tm, tkk), lambda i, kk, n: (i, kk)) for _ in out_dtypes],
        out_shape=[jax.ShapeDtypeStruct((M, K), dt) for dt in out_dtypes],
        scratch_shapes=[pltpu.VMEM((tm, tkk), F32)] if nn > 1 else [],
        compiler_params=_cparams("parallel", "parallel", "arbitrary"),
    )(g, w3, *extras)


def _mm_tn(name, a, g, n_shards):
    T, K = a.shape
    _, N = g.shape
    Ns = N // n_shards
    tk, tn, tt = _tile(K, MM_TW_K), _tile(Ns, MM_TN), _tile(T, MM_TW_T)
    per = Ns // tn
    nt = T // tt

    def body(a_ref, g_ref, o_ref, *scratch):
        part = lax.dot_general(a_ref[...].astype(BF16), g_ref[...].astype(BF16), (((0,), (0,)), ((), ())),
                               preferred_element_type=F32)
        if nt == 1:
            o_ref[...] = part.astype(o_ref.dtype)
            return
        acc_ref, = scratch
        t = pl.program_id(2)

        @pl.when(t == 0)
        def _():
            acc_ref[...] = part

        @pl.when(t > 0)
        def _():
            acc_ref[...] += part

        @pl.when(t == nt - 1)
        def _():
            o_ref[...] = acc_ref[...].astype(o_ref.dtype)

    return pl.pallas_call(
        body, name=name, grid=(K // tk, N // tn, nt),
        in_specs=[pl.BlockSpec((tt, tk), lambda i, j, t: (t, i)),
                  pl.BlockSpec((tt, tn), lambda i, j, t: (t, j))],
        out_specs=pl.BlockSpec((None, tk, tn), lambda i, j, t: (j // per, i, j % per)),
        out_shape=jax.ShapeDtypeStruct((n_shards, K, Ns), BF16),
        scratch_shapes=[pltpu.VMEM((tk, tn), F32)] if nt > 1 else [],
        compiler_params=_cparams("parallel", "parallel", "arbitrary"),
    )(a, g)


def _rowwise(name, fn, ins, outs, accs=(), scratch=(), tt=ROW_TILE):
    T = next(a.shape[0] for a, kind in ins if kind == "row")
    tt = _tile(T, tt)
    n = T // tt
    in_specs = []
    for a, kind in ins:
        w = a.shape[1]
        if kind == "row":
            in_specs.append(pl.BlockSpec((tt, w), lambda i: (i, 0)))
        elif kind == "vec":
            in_specs.append(pl.BlockSpec(a.shape, lambda i: (0, 0)))
        elif kind[0] == "prev":
            pad = kind[1]
            in_specs.append(pl.BlockSpec((pad, w), lambda i, q=tt // pad: (jnp.maximum(i * q - 1, 0), 0)))
        else:
            pad = kind[1]
            in_specs.append(pl.BlockSpec((pad, w), lambda i, q=tt // pad, last=T // pad - 1:
                                         (jnp.minimum((i + 1) * q, last), 0)))
    n_in, n_out, n_acc = len(ins), len(outs), len(accs)

    def body(*refs):
        i = pl.program_id(0)
        in_refs = refs[:n_in]
        out_refs = refs[n_in:n_in + n_out]
        acc_refs = refs[n_in + n_out:n_in + n_out + n_acc]
        scr = refs[n_in + n_out + n_acc:]
        if n_acc:
            @pl.when(i == 0)
            def _():
                for r in acc_refs:
                    r[...] = jnp.zeros_like(r)
        fn(i, n, in_refs, out_refs, acc_refs, scr)

    res = pl.pallas_call(
        body, name=name, grid=(n,),
        in_specs=in_specs,
        out_specs=[pl.BlockSpec((tt, w), lambda i: (i, 0)) for w, _ in outs]
        + [pl.BlockSpec((r, w), lambda i: (0, 0)) for r, w in accs],
        out_shape=[jax.ShapeDtypeStruct((T, w), dt) for w, dt in outs]
        + [jax.ShapeDtypeStruct((r, w), F32) for r, w in accs],
        scratch_shapes=list(scratch),
        compiler_params=_cparams("arbitrary"),
    )(*[a for a, _ in ins])
    return res


def _colsum(x):
    return jnp.sum(x, axis=0, keepdims=True)


def _rms_fwd(name, h, g):
    D = h.shape[1]

    def fn(i, n, ins, outs, accs, scr):
        x = ins[0][...]
        outs[0][...] = (x * _rms_r(x) * ins[1][...]).astype(BF16)

    return _rowwise(name, fn, [(h, "row"), (g, "vec")], [(D, BF16)])[0]


def _rms_bwd(name, h, g, du, dh_in, want_colsum=False):
    D = h.shape[1]

    def fn(i, n, ins, outs, accs, scr):
        x = ins[0][...]
        gg = ins[1][...]
        d = ins[2][...].astype(F32)
        r = _rms_r(x)
        xn = x * r
        t = d * gg
        dh = ins[3][...] + r * (t - xn * jnp.mean(t * xn, axis=-1, keepdims=True))
        outs[0][...] = dh
        outs[1][...] = dh.astype(BF16)
        accs[0][...] += _colsum(d * xn)
        if want_colsum:
            accs[1][...] += _colsum(dh)

    return _rowwise(name, fn, [(h, "row"), (g, "vec"), (du, "row"), (dh_in, "row")],
                    [(D, F32), (D, BF16)], accs=[(1, D)] * (2 if want_colsum else 1))


def _loss_bwd(name, h, g, tgt):
    D = h.shape[1]

    def fn(i, n, ins, outs, accs, scr):
        x = ins[0][...]
        gg = ins[1][...]
        r = _rms_r(x)
        xn = x * r
        err = xn * gg - ins[2][...]
        dy = err / D
        t = dy * gg
        dh = r * (t - xn * jnp.mean(t * xn, axis=-1, keepdims=True))
        outs[0][...] = dh
        outs[1][...] = dh.astype(BF16)
        accs[0][...] += _colsum(dy * xn)
        accs[1][...] += _colsum(err * err)

    return _rowwise(name, fn, [(h, "row"), (g, "vec"), (tgt, "row")], [(D, F32), (D, BF16)],
                    accs=[(1, D), (1, D)])


def _ple_elem_bwd(name, dh, q, e):
    D = dh.shape[1]

    def fn(i, n, ins, outs, accs, scr):
        d = ins[0][...]
        s = _sigmoid(ins[1][...].astype(F32))
        ee = ins[2][...].astype(F32)
        outs[0][...] = (d * ee * s * (1.0 - s)).astype(BF16)
        outs[1][...] = (d * s).astype(BF16)

    return _rowwise(name, fn, [(dh, "row"), (q, "row"), (e, "row")], [(D, BF16), (D, BF16)])


def _cf_norm_bwd(name, v2, g, dv4):
    D = v2.shape[1]

    def fn(i, n, ins, outs, accs, scr):
        x = ins[0][...]
        gg = ins[1][...]
        r = _rms_r(x)
        xn = x * r
        v3 = xn * gg
        s = _sigmoid(v3)
        dv3 = ins[2][...].astype(F32) * (s * (1.0 + v3 * (1.0 - s)))
        t = dv3 * gg
        dv2 = r * (t - xn * jnp.mean(t * xn, axis=-1, keepdims=True))
        outs[0][...] = dv2
        accs[0][...] += _colsum(dv3 * xn)
        accs[1][...] += _colsum(dv2)

    return _rowwise(name, fn, [(v2, "row"), (g, "vec"), (dv4, "row")], [(D, F32)], accs=[(1, D), (1, D)])


def _chunks(tt, width):
    cc = min(CONV_LANE_CHUNK, width)
    rc = min(CONV_ROW_CHUNK, tt)
    for c0 in range(0, width, cc):
        for r0 in range(0, tt, rc):
            yield r0, rc, c0, cc


def _n_shifts(n_taps):
    return min(SUBLANES - 1, n_taps - 1)


def _shifted_scratch(n_taps, rows, width):
    return pltpu.VMEM((_n_shifts(n_taps), rows, width), F32)


def _shift_window(win_ref, sh_ref, n_taps, sign):
    rows = win_ref.shape[0] - SUBLANES
    width = win_ref.shape[1]
    cc = min(CONV_LANE_CHUNK, width)
    for b in range(1, _n_shifts(n_taps) + 1):
        off = SUBLANES - b if sign < 0 else b
        for c0 in range(0, width, cc):
            sh_ref[b - 1, 0:rows, c0:c0 + cc] = win_ref[off:off + rows, c0:c0 + cc]


def _tap(win_ref, sh_ref, base, sign, s, r0, rc, c0, cc):
    a, b = divmod(s, SUBLANES)
    if b == 0:
        row = base + r0 + sign * SUBLANES * a
        return win_ref[row:row + rc, c0:c0 + cc]
    row = base + r0 - SUBLANES * (a + 1) if sign < 0 else base + r0 + SUBLANES * a
    return sh_ref[b - 1, row:row + rc, c0:c0 + cc]


def _fir(win_ref, sh_ref, w_ref, n_taps, base, sign, tt, width, emit):
    for r0, rc, c0, cc in _chunks(tt, width):
        acc = jnp.zeros((rc, cc), F32)
        for k in range(n_taps):
            acc = acc + w_ref[k:k + 1, c0:c0 + cc] * _tap(win_ref, sh_ref, base, sign, n_taps - 1 - k, r0, rc, c0, cc)
        emit(r0, rc, c0, cc, acc)


def _fir_wgrad(d_ref, win_ref, sh_ref, dw8_ref, n_taps, pad, tt, width):
    for c0 in range(0, width, min(CONV_LANE_CHUNK, width)):
        cc = min(CONV_LANE_CHUNK, width)
        rc = min(CONV_ROW_CHUNK, tt)
        for k in range(n_taps):
            acc = jnp.zeros((SUBLANES, cc), F32)
            for r0 in range(0, tt, rc):
                prod = d_ref[r0:r0 + rc, c0:c0 + cc] * _tap(win_ref, sh_ref, pad, -1, n_taps - 1 - k, r0, rc, c0, cc)
                for q in range(0, rc, SUBLANES):
                    acc = acc + prod[q:q + SUBLANES]
            dw8_ref[SUBLANES * k:SUBLANES * (k + 1), c0:c0 + cc] += acc


def _glu(blk, D):
    return blk[:, :D].astype(F32) * _sigmoid(blk[:, D:].astype(F32))


CF_PAD = 32
SC_PAD = 16


def _cf_conv_fwd(name, a, w_dw, b_dw, g_cf):
    T, D2 = a.shape
    D = D2 // 2
    K = w_dw.shape[0]
    tt = _tile(T, CONV_TILE_ROWS)

    def fn(i, n, ins, outs, accs, scr):
        a_ref, prev_ref, w_ref, b_ref, g_ref = ins
        win_ref, v2_ref, sh_ref = scr
        win_ref[0:CF_PAD, :] = jnp.where(i > 0, _glu(prev_ref[...], D), 0.0)
        win_ref[CF_PAD:CF_PAD + tt, :] = _glu(a_ref[...], D)
        _shift_window(win_ref, sh_ref, K, -1)

        def emit(r0, rc, c0, cc, acc):
            v2_ref[r0:r0 + rc, c0:c0 + cc] = acc + b_ref[:, c0:c0 + cc]

        _fir(win_ref, sh_ref, w_ref, K, CF_PAD, -1, tt, D, emit)
        v2 = v2_ref[...]
        v3 = v2 * _rms_r(v2) * g_ref[...]
        outs[0][...] = v2
        outs[1][...] = (v3 * _sigmoid(v3)).astype(BF16)

    return _rowwise(name, fn, [(a, "row"), (a, ("prev", CF_PAD)), (w_dw, "vec"), (b_dw, "vec"), (g_cf, "vec")],
                    [(D, F32), (D, BF16)],
                    scratch=[pltpu.VMEM((CF_PAD + tt, D), F32), pltpu.VMEM((tt, D), F32),
                             _shifted_scratch(K, CF_PAD + tt, D)], tt=tt)


def _cf_conv_bwd(name, dv2, a, w_dw):
    T, D2 = a.shape
    D = D2 // 2
    K = w_dw.shape[0]
    tt = _tile(T, CONV_TILE_ROWS)

    def fn(i, n, ins, outs, accs, scr):
        d_ref, dnext_ref, a_ref, prev_ref, w_ref = ins
        v1win_ref, dwin_ref, dv1_ref, dw8_ref, v1sh_ref, dsh_ref = scr

        @pl.when(i == 0)
        def _():
            dw8_ref[...] = jnp.zeros_like(dw8_ref)

        v1win_ref[0:CF_PAD, :] = jnp.where(i > 0, _glu(prev_ref[...], D), 0.0)
        v1win_ref[CF_PAD:CF_PAD + tt, :] = _glu(a_ref[...], D)
        dwin_ref[0:tt, :] = d_ref[...]
        dwin_ref[tt:tt + CF_PAD, :] = jnp.where(i < n - 1, dnext_ref[...], 0.0)
        _shift_window(v1win_ref, v1sh_ref, K, -1)
        _shift_window(dwin_ref, dsh_ref, K, 1)

        def emit(r0, rc, c0, cc, acc):
            dv1_ref[r0:r0 + rc, c0:c0 + cc] = acc

        _fir(dwin_ref, dsh_ref, w_ref, K, 0, 1, tt, D, emit)
        _fir_wgrad(d_ref, v1win_ref, v1sh_ref, dw8_ref, K, CF_PAD, tt, D)

        blk = a_ref[...]
        val = blk[:, :D].astype(F32)
        sg = _sigmoid(blk[:, D:].astype(F32))
        dv1 = dv1_ref[...]
        dval = dv1 * sg
        dgate = dv1 * val * sg * (1.0 - sg)
        outs[0][:, :D] = dval.astype(BF16)
        outs[0][:, D:] = dgate.astype(BF16)
        accs[1][:, :D] += _colsum(dval)
        accs[1][:, D:] += _colsum(dgate)

        @pl.when(i == n - 1)
        def _():
            for k in range(K):
                accs[0][k:k + 1, :] = _colsum(dw8_ref[SUBLANES * k:SUBLANES * (k + 1), :])

    return _rowwise(name, fn, [(dv2, "row"), (dv2, ("next", CF_PAD)), (a, "row"), (a, ("prev", CF_PAD)),
                               (w_dw, "vec")],
                    [(D2, BF16)], accs=[(K, D), (1, D2)],
                    scratch=[pltpu.VMEM((CF_PAD + tt, D), F32), pltpu.VMEM((tt + CF_PAD, D), F32),
                             pltpu.VMEM((tt, D), F32), pltpu.VMEM((SUBLANES * K, D), F32),
                             _shifted_scratch(K, CF_PAD + tt, D), _shifted_scratch(K, tt + CF_PAD, D)], tt=tt)


def _sc_conv_fwd(name, bcv, w_conv):
    T, D3 = bcv.shape
    D = D3 // 3
    K = w_conv.shape[0]
    tt = _tile(T, CONV_TILE_ROWS)

    def cv_of(blk):
        return blk[:, D:2 * D].astype(F32) * blk[:, 2 * D:].astype(F32)

    def fn(i, n, ins, outs, accs, scr):
        x_ref, prev_ref, w_ref = ins
        win_ref, cc_ref, sh_ref = scr
        win_ref[0:SC_PAD, :] = jnp.where(i > 0, cv_of(prev_ref[...]), 0.0)
        win_ref[SC_PAD:SC_PAD + tt, :] = cv_of(x_ref[...])
        _shift_window(win_ref, sh_ref, K, -1)

        def emit(r0, rc, c0, cw, acc):
            cc_ref[r0:r0 + rc, c0:c0 + cw] = acc

        _fir(win_ref, sh_ref, w_ref, K, SC_PAD, -1, tt, D, emit)
        outs[0][...] = (x_ref[:, :D].astype(F32) * cc_ref[...]).astype(BF16)

    return _rowwise(name, fn, [(bcv, "row"), (bcv, ("prev", SC_PAD)), (w_conv, "vec")], [(D, BF16)],
                    scratch=[pltpu.VMEM((SC_PAD + tt, D), F32), pltpu.VMEM((tt, D), F32),
                             _shifted_scratch(K, SC_PAD + tt, D)], tt=tt)


def _sc_conv_bwd(name, dy, bcv, w_conv):
    T, D3 = bcv.shape
    D = D3 // 3
    K = w_conv.shape[0]
    tt = _tile(T, CONV_TILE_ROWS)

    def cv_of(blk):
        return blk[:, D:2 * D].astype(F32) * blk[:, 2 * D:].astype(F32)

    def fn(i, n, ins, outs, accs, scr):
        dy_ref, dynext_ref, x_ref, prev_ref, next_ref, w_ref = ins
        cvwin_ref, dccwin_ref, tmp_ref, dw8_ref, cvsh_ref, dccsh_ref = scr

        @pl.when(i == 0)
        def _():
            dw8_ref[...] = jnp.zeros_like(dw8_ref)

        cvwin_ref[0:SC_PAD, :] = jnp.where(i > 0, cv_of(prev_ref[...]), 0.0)
        cvwin_ref[SC_PAD:SC_PAD + tt, :] = cv_of(x_ref[...])
        _shift_window(cvwin_ref, cvsh_ref, K, -1)

        def emit_cc(r0, rc, c0, cw, acc):
            tmp_ref[r0:r0 + rc, c0:c0 + cw] = acc

        _fir(cvwin_ref, cvsh_ref, w_ref, K, SC_PAD, -1, tt, D, emit_cc)
        dy_v = dy_ref[...].astype(F32)
        outs[0][:, :D] = (dy_v * tmp_ref[...]).astype(BF16)
        dccwin_ref[0:tt, :] = dy_v * x_ref[:, :D].astype(F32)
        dccwin_ref[tt:tt + SC_PAD, :] = jnp.where(
            i < n - 1, dynext_ref[...].astype(F32) * next_ref[:, :D].astype(F32), 0.0)

        _shift_window(dccwin_ref, dccsh_ref, K, 1)

        def emit_dcv(r0, rc, c0, cw, acc):
            tmp_ref[r0:r0 + rc, c0:c0 + cw] = acc

        _fir(dccwin_ref, dccsh_ref, w_ref, K, 0, 1, tt, D, emit_dcv)
        _fir_wgrad(dccwin_ref, cvwin_ref, cvsh_ref, dw8_ref, K, SC_PAD, tt, D)
        dcv = tmp_ref[...]
        outs[0][:, D:2 * D] = (dcv * x_ref[:, 2 * D:].astype(F32)).astype(BF16)
        outs[0][:, 2 * D:] = (dcv * x_ref[:, D:2 * D].astype(F32)).astype(BF16)

        @pl.when(i == n - 1)
        def _():
            for k in range(K):
                accs[0][k:k + 1, :] = _colsum(dw8_ref[SUBLANES * k:SUBLANES * (k + 1), :])

    return _rowwise(name, fn, [(dy, "row"), (dy, ("next", SC_PAD)), (bcv, "row"), (bcv, ("prev", SC_PAD)),
                               (bcv, ("next", SC_PAD)), (w_conv, "vec")],
                    [(D3, BF16)], accs=[(K, D)],
                    scratch=[pltpu.VMEM((SC_PAD + tt, D), F32), pltpu.VMEM((tt + SC_PAD, D), F32),
                             pltpu.VMEM((tt, D), F32), pltpu.VMEM((SUBLANES * K, D), F32),
                             _shifted_scratch(K, SC_PAD + tt, D), _shifted_scratch(K, tt + SC_PAD, D)], tt=tt)


def _place():
    x, y, c = lax.axis_index("x"), lax.axis_index("y"), lax.axis_index("c")
    chips = [(1 - x, y), (x, 1 - y), (1 - x, 1 - y)]
    return x, y, c, 2 * x + y, chips, (x, y, 1 - c)


def _half(rows, which):
    return pl.ds(pl.multiple_of(which * (rows // 2), SUBLANES), rows // 2)


_HBM = pl.BlockSpec(memory_space=pl.ANY)


def _gather_shards(name, items):
    n = len(items)
    shapes = [a.shape[-2:] for a, _ in items]

    def body(*refs):
        srcs, outs = refs[:n], refs[n:2 * n]
        send1, recv1, send2, recv2, lsem = refs[2 * n:]
        x, y, c, k, chips, sib = _place()

        def shard(i):
            return srcs[i] if items[i][1] is None else srcs[i].at[items[i][1]]

        started, locs = [], []
        for i in range(n):
            rows = shapes[i][0]
            lc = pltpu.make_async_copy(shard(i), outs[i].at[k], lsem.at[i])
            lc.start()
            locs.append(lc)
            for j, (cx, cy) in enumerate(chips):
                cp = pltpu.make_async_remote_copy(
                    src_ref=shard(i).at[_half(rows, c)], dst_ref=outs[i].at[k, _half(rows, c)],
                    send_sem=send1.at[i, j], recv_sem=recv1.at[i, j], device_id=(cx, cy, c), device_id_type=MESH)
                cp.start()
                started.append(cp)
        for i in range(n):
            rows = shapes[i][0]
            for j, (cx, cy) in enumerate(chips):
                blk = outs[i].at[2 * cx + cy, _half(rows, c)]
                pltpu.make_async_remote_copy(
                    src_ref=blk, dst_ref=blk, send_sem=send1.at[i, j], recv_sem=recv1.at[i, j],
                    device_id=(cx, cy, c), device_id_type=MESH).wait_recv()
                fw = pltpu.make_async_remote_copy(
                    src_ref=blk, dst_ref=blk, send_sem=send2.at[i, j], recv_sem=recv2.at[i, j],
                    device_id=sib, device_id_type=MESH)
                fw.start()
                started.append(fw)
        for i in range(n):
            rows = shapes[i][0]
            for j, (cx, cy) in enumerate(chips):
                blk = outs[i].at[2 * cx + cy, _half(rows, 1 - c)]
                pltpu.make_async_remote_copy(
                    src_ref=blk, dst_ref=blk, send_sem=send2.at[i, j], recv_sem=recv2.at[i, j],
                    device_id=sib, device_id_type=MESH).wait_recv()
        for cp in started:
            cp.wait_send()
        for lc in locs:
            lc.wait()

    return pl.pallas_call(
        body, name=name,
        in_specs=[_HBM] * n, out_specs=[_HBM] * n,
        out_shape=[jax.ShapeDtypeStruct((N_SHARDS,) + tuple(s), a.dtype) for s, (a, _) in zip(shapes, items)],
        scratch_shapes=[pltpu.SemaphoreType.DMA((n, 3))] * 4 + [pltpu.SemaphoreType.DMA((n,))],
    )(*[a for a, _ in items])


def _cast_place(name, w, layer, pos):
    _, R, C = w.shape
    tr = _tile(R, 256)

    def body(x_ref, y_ref, c_ref, w_ref, o_ref):
        o_ref[...] = w_ref[...].astype(BF16)

    return pl.pallas_call(
        body, name=name,
        grid_spec=pltpu.PrefetchScalarGridSpec(
            num_scalar_prefetch=3, grid=(R // tr,),
            in_specs=[pl.BlockSpec((None, tr, C), lambda r, xr, yr, cr: (layer, r, 0))],
            out_specs=pl.BlockSpec((None, tr, C), lambda r, xr, yr, cr: (2 * xr[0] + yr[0], r, 0))),
        out_shape=jax.ShapeDtypeStruct((N_SHARDS, R, C), BF16),
        compiler_params=_cparams("parallel"),
    )(*pos, w)


_IN_HBM = pl.BlockSpec(memory_space=pltpu.HBM)
_SEM = pl.BlockSpec(memory_space=pltpu.SEMAPHORE)
_SPLIT_COPY_PARAMS = pltpu.CompilerParams(has_side_effects=pltpu.SideEffectType.DATAFLOW_SIDE_EFFECTING)


def _in_hbm(a):
    return pltpu.with_memory_space_constraint(a, pltpu.HBM)


def _gather_copy(ref, i, j, chip_xy, c, k_src, rows, send, recv):
    blk = ref.at[k_src, _half(rows, c)]
    return pltpu.make_async_remote_copy(
        src_ref=blk, dst_ref=blk, send_sem=send.at[3 * i + j], recv_sem=recv.at[3 * i + j],
        device_id=(*chip_xy, c), device_id_type=MESH)


def _gather_start(name, bufs, after):
    n = len(bufs)

    def body(*refs):
        ins = refs[:n]
        send, recv = refs[n + 1], refs[n + 2]
        token = refs[-1]
        x, y, c, k, chips, sib = _place()
        for i in range(n):
            for j, chip_xy in enumerate(chips):
                _gather_copy(ins[i], i, j, chip_xy, c, k, bufs[i].shape[1], send, recv).start()
        token[...] = jnp.zeros_like(token)

    res = pl.pallas_call(
        body, name=name,
        in_specs=[_IN_HBM] * n + [_HBM],
        out_specs=[_SEM, _SEM] + [_IN_HBM] * n + [pl.BlockSpec(memory_space=pltpu.VMEM)],
        out_shape=[pltpu.SemaphoreType.DMA((3 * n,)), pltpu.SemaphoreType.DMA((3 * n,))]
        + [pltpu.HBM(b.shape, b.dtype) for b in bufs] + [jax.ShapeDtypeStruct((SUBLANES, 128), F32)],
        input_output_aliases={i: 2 + i for i in range(n)},
        compiler_params=_SPLIT_COPY_PARAMS,
    )(*[_in_hbm(b) for b in bufs], after)
    return res[0], res[1], list(res[2:2 + n]), res[-1]


def _gather_wait(name, bufs, send, recv, after):
    n = len(bufs)

    def body(*refs):
        ins = refs[:n]
        send_ref, recv_ref = refs[n], refs[n + 1]
        x, y, c, k, chips, sib = _place()
        for i in range(n):
            for j, chip_xy in enumerate(chips):
                rows = bufs[i].shape[1]
                _gather_copy(ins[i], i, j, chip_xy, c, k, rows, send_ref, recv_ref).wait_send()
                _gather_copy(ins[i], i, j, chip_xy, c, 2 * chip_xy[0] + chip_xy[1], rows, send_ref, recv_ref).wait_recv()

    return pl.pallas_call(
        body, name=name,
        in_specs=[_IN_HBM] * n + [_SEM, _SEM, _HBM],
        out_specs=[_IN_HBM] * n,
        out_shape=[pltpu.HBM(b.shape, b.dtype) for b in bufs],
        input_output_aliases={i: i for i in range(n)},
        compiler_params=_SPLIT_COPY_PARAMS,
    )(*bufs, send, recv, after)


def _gather_forward(name, bufs):
    n = len(bufs)

    def body(*refs):
        outs = refs[n:2 * n]
        send, recv = refs[2 * n:]
        x, y, c, k, chips, sib = _place()
        started = []
        for i in range(n):
            rows = bufs[i].shape[1]
            for j, (cx, cy) in enumerate(chips):
                blk = outs[i].at[2 * cx + cy, _half(rows, c)]
                fw = pltpu.make_async_remote_copy(
                    src_ref=blk, dst_ref=blk, send_sem=send.at[i, j], recv_sem=recv.at[i, j],
                    device_id=sib, device_id_type=MESH)
                fw.start()
                started.append(fw)
        for i in range(n):
            rows = bufs[i].shape[1]
            for j, (cx, cy) in enumerate(chips):
                blk = outs[i].at[2 * cx + cy, _half(rows, 1 - c)]
                pltpu.make_async_remote_copy(
                    src_ref=blk, dst_ref=blk, send_sem=send.at[i, j], recv_sem=recv.at[i, j],
                    device_id=sib, device_id_type=MESH).wait_recv()
        for cp in started:
            cp.wait_send()

    return pl.pallas_call(
        body, name=name, in_specs=[_HBM] * n, out_specs=[_HBM] * n,
        out_shape=[jax.ShapeDtypeStruct(b.shape, b.dtype) for b in bufs],
        input_output_aliases={i: i for i in range(n)},
        scratch_shapes=[pltpu.SemaphoreType.DMA((n, 3))] * 2,
    )(*bufs)


def _swap_halves(name, parts):
    n = len(parts)

    def body(*refs):
        srcs, outs = refs[:n], refs[n:2 * n]
        send, recv = refs[2 * n:]
        x, y, c, k, chips, sib = _place()
        cps = []
        for i in range(n):
            rows = parts[i].shape[1]
            cp = pltpu.make_async_remote_copy(
                src_ref=srcs[i].at[:, _half(rows, 1 - c)], dst_ref=outs[i],
                send_sem=send.at[i], recv_sem=recv.at[i], device_id=sib, device_id_type=MESH)
            cp.start()
            cps.append(cp)
        for cp in cps:
            cp.wait()

    return pl.pallas_call(
        body, name=name, in_specs=[_HBM] * n, out_specs=[_HBM] * n,
        out_shape=[jax.ShapeDtypeStruct((p.shape[0], p.shape[1] // 2, p.shape[2]), p.dtype) for p in parts],
        scratch_shapes=[pltpu.SemaphoreType.DMA((n,))] * 2,
    )(*parts)


def _sibling_start(name, bufs, n_copies, make):
    nb = len(bufs)

    def body(*refs):
        send, recv = refs[nb], refs[nb + 1]
        token = refs[-1]
        x, y, c, k, chips, sib = _place()
        for cp in make(refs[:nb], c, sib, send, recv):
            cp.start()
        token[...] = jnp.zeros_like(token)

    res = pl.pallas_call(
        body, name=name,
        in_specs=[_IN_HBM] * nb,
        out_specs=[_SEM, _SEM] + [_IN_HBM] * nb + [pl.BlockSpec(memory_space=pltpu.VMEM)],
        out_shape=[pltpu.SemaphoreType.DMA((n_copies,)), pltpu.SemaphoreType.DMA((n_copies,))]
        + [pltpu.HBM(b.shape, b.dtype) for b in bufs] + [jax.ShapeDtypeStruct((SUBLANES, 128), F32)],
        input_output_aliases={i: 2 + i for i in range(nb)},
        compiler_params=_SPLIT_COPY_PARAMS,
    )(*[_in_hbm(b) for b in bufs])
    return res[0], res[1], list(res[2:2 + nb]), res[-1]


def _sibling_wait(name, bufs, send, recv, make, afters):
    nb = len(bufs)

    def body(*refs):
        x, y, c, k, chips, sib = _place()
        for cp in make(refs[:nb], c, sib, refs[nb], refs[nb + 1]):
            cp.wait_send()
            cp.wait_recv()

    return list(pl.pallas_call(
        body, name=name,
        in_specs=[_IN_HBM] * nb + [_SEM, _SEM] + [_HBM] * len(afters),
        out_specs=[_IN_HBM] * nb,
        out_shape=[pltpu.HBM(b.shape, b.dtype) for b in bufs],
        input_output_aliases={i: i for i in range(nb)},
        compiler_params=_SPLIT_COPY_PARAMS,
    )(*bufs, send, recv, *afters))


def _swap_copies(parts):
    n = len(parts)

    def make(refs, c, sib, send, recv):
        return [pltpu.make_async_remote_copy(
            src_ref=refs[i].at[:, _half(parts[i].shape[1], 1 - c)], dst_ref=refs[n + i],
            send_sem=send.at[i], recv_sem=recv.at[i], device_id=sib, device_id_type=MESH) for i in range(n)]

    return make


def _join_copies(fulls):
    def make(refs, c, sib, send, recv):
        cps = []
        for i, f in enumerate(fulls):
            blk = refs[i].at[_half(f.shape[0], c)]
            cps.append(pltpu.make_async_remote_copy(
                src_ref=blk, dst_ref=blk, send_sem=send.at[i], recv_sem=recv.at[i],
                device_id=sib, device_id_type=MESH))
        return cps

    return make


def _scatter_copy(src_ref, land_ref, i, j, chip_xy, c, send, recv):
    return pltpu.make_async_remote_copy(
        src_ref=src_ref.at[2 * chip_xy[0] + chip_xy[1]], dst_ref=land_ref.at[j],
        send_sem=send.at[3 * i + j], recv_sem=recv.at[3 * i + j], device_id=(*chip_xy, c), device_id_type=MESH)


def _scatter_start(name, sums):
    n = len(sums)
    lands = [lax.empty((3,) + s.shape[1:], s.dtype) for s in sums]

    def body(*refs):
        srcs, lnds = refs[:n], refs[n:2 * n]
        send, recv = refs[2 * n], refs[2 * n + 1]
        token = refs[-1]
        x, y, c, k, chips, sib = _place()
        for i in range(n):
            for j, chip_xy in enumerate(chips):
                _scatter_copy(srcs[i], lnds[i], i, j, chip_xy, c, send, recv).start()
        token[...] = jnp.zeros_like(token)

    res = pl.pallas_call(
        body, name=name,
        in_specs=[_IN_HBM] * (2 * n),
        out_specs=[_SEM, _SEM] + [_IN_HBM] * (2 * n) + [pl.BlockSpec(memory_space=pltpu.VMEM)],
        out_shape=[pltpu.SemaphoreType.DMA((3 * n,)), pltpu.SemaphoreType.DMA((3 * n,))]
        + [pltpu.HBM(a.shape, a.dtype) for a in list(sums) + lands] + [jax.ShapeDtypeStruct((SUBLANES, 128), F32)],
        input_output_aliases={i: 2 + i for i in range(2 * n)},
        compiler_params=_SPLIT_COPY_PARAMS,
    )(*[_in_hbm(a) for a in list(sums) + lands])
    return res[0], res[1], list(res[2:2 + n]), list(res[2 + n:2 + 2 * n]), res[-1]


def _scatter_wait(name, sums, lands, send, recv, afters):
    n = len(sums)

    def body(*refs):
        srcs, lnds = refs[:n], refs[n:2 * n]
        send_ref, recv_ref = refs[2 * n], refs[2 * n + 1]
        x, y, c, k, chips, sib = _place()
        for i in range(n):
            for j, chip_xy in enumerate(chips):
                cp = _scatter_copy(srcs[i], lnds[i], i, j, chip_xy, c, send_ref, recv_ref)
                cp.wait_send()
                cp.wait_recv()

    res = pl.pallas_call(
        body, name=name,
        in_specs=[_IN_HBM] * (2 * n) + [_SEM, _SEM] + [_HBM] * len(afters),
        out_specs=[_IN_HBM] * (2 * n),
        out_shape=[pltpu.HBM(a.shape, a.dtype) for a in list(sums) + list(lands)],
        input_output_aliases={i: i for i in range(2 * n)},
        compiler_params=_SPLIT_COPY_PARAMS,
    )(*sums, *lands, send, recv, *afters)
    return list(res[:n]), list(res[n:])


def _sum_over_devices(name, buf, loss_row, afters):
    R, D = buf.shape
    n_after = len(afters)

    def body(x_ref, *rest):
        all_ref, tot_ref, loss_ref, send_sems, recv_sems, local_sem = rest[n_after:]
        x, y, c, k, chips, sib = _place()
        me = (x, y, c)

        def block(px, py, pc):
            return all_ref.at[4 * px + 2 * py + pc]

        def copy(kk, blk, to, src=None):
            return pltpu.make_async_remote_copy(
                src_ref=block(*blk) if src is None else src, dst_ref=block(*blk),
                send_sem=send_sems.at[kk], recv_sem=recv_sems.at[kk], device_id=to, device_id_type=MESH)

        mine = pltpu.make_async_copy(x_ref, block(*me), local_sem)
        mine.start()
        first = [copy(0, me, sib, src=x_ref)]
        first += [copy(1 + j, me, (*chip, c), src=x_ref) for j, chip in enumerate(chips)]
        for cp in first:
            cp.start()
        passed = [copy(4 + j, (*chip, c), sib) for j, chip in enumerate(chips)]
        for j, chip in enumerate(chips):
            copy(1 + j, (*chip, c), me).wait_recv()
            passed[j].start()
        copy(0, sib, me).wait_recv()
        for j, chip in enumerate(chips):
            copy(4 + j, (*chip, 1 - c), me).wait_recv()
        for cp in first + passed:
            cp.wait_send()
        mine.wait()
        rc = _tile(R, 32)
        for r0 in range(0, R, rc):
            tot = all_ref[0, r0:r0 + rc, :]
            for d in range(1, N_DEVICES):
                tot = tot + all_ref[d, r0:r0 + rc, :]
            tot_ref[r0:r0 + rc, :] = tot
        loss = 0.5 * jnp.sum(tot_ref[loss_row:loss_row + 1, :]) / D
        loss_ref[...] = jnp.full(loss_ref.shape, loss, F32)

    vm = pl.BlockSpec(memory_space=pltpu.VMEM)
    return pl.pallas_call(
        body, name=name, in_specs=[vm] + [_HBM] * n_after, out_specs=[vm, vm, vm],
        out_shape=[jax.ShapeDtypeStruct((N_DEVICES, R, D), F32), jax.ShapeDtypeStruct((R, D), F32),
                   jax.ShapeDtypeStruct((SUBLANES, 128), F32)],
        scratch_shapes=[pltpu.SemaphoreType.DMA((7,)), pltpu.SemaphoreType.DMA((7,)), pltpu.SemaphoreType.DMA],
        compiler_params=pltpu.CompilerParams(vmem_limit_bytes=V7X_VMEM_LIMIT_BYTES),
    )(buf, *afters)[1:]


def _add_my_half(name, part, got, pos):
    S, R, C = part.shape
    R2 = R // 2
    tr = _tile(R2, 512)
    q = R2 // tr

    def body(x_ref, y_ref, c_ref, p_ref, g_ref, o_ref):
        o_ref[...] = (p_ref[...].astype(F32) + g_ref[...].astype(F32)).astype(o_ref.dtype)

    return pl.pallas_call(
        body, name=name,
        grid_spec=pltpu.PrefetchScalarGridSpec(
            num_scalar_prefetch=3, grid=(S, q),
            in_specs=[pl.BlockSpec((None, tr, C), lambda s, r, xr, yr, cr: (s, cr[0] * q + r, 0)),
                      pl.BlockSpec((None, tr, C), lambda s, r, xr, yr, cr: (s, r, 0))],
            out_specs=pl.BlockSpec((None, tr, C), lambda s, r, xr, yr, cr: (s, r, 0))),
        out_shape=jax.ShapeDtypeStruct((S, R2, C), BF16),
        compiler_params=_cparams("parallel", "parallel"),
    )(*pos, part, got)


def _add_owner(name, sums, got, pos):
    _, R2, C = sums.shape
    tr = _tile(R2, 512)
    q = R2 // tr

    def body(x_ref, y_ref, c_ref, s_ref, g_ref, o_ref):
        acc = s_ref[...].astype(F32)
        for j in range(3):
            acc = acc + g_ref[j].astype(F32)
        o_ref[...] = acc

    return pl.pallas_call(
        body, name=name,
        grid_spec=pltpu.PrefetchScalarGridSpec(
            num_scalar_prefetch=3, grid=(q,),
            in_specs=[pl.BlockSpec((None, tr, C), lambda r, xr, yr, cr: (2 * xr[0] + yr[0], r, 0)),
                      pl.BlockSpec((3, tr, C), lambda r, xr, yr, cr: (0, r, 0))],
            out_specs=pl.BlockSpec((tr, C), lambda r, xr, yr, cr: (cr[0] * q + r, 0))),
        out_shape=jax.ShapeDtypeStruct((2 * R2, C), F32),
        compiler_params=_cparams("parallel"),
    )(*pos, sums, got)


def _adamw(name, w, m, v, g):
    R, C = w.shape
    tr = SUBLANES
    while 2 * tr * C <= ADAMW_TILE_ELEMS:
        tr *= 2
    tr = _tile(R, tr)
    bc1 = 1.0 - ADAM_B1 ** ADAM_STEP
    bc2 = 1.0 - ADAM_B2 ** ADAM_STEP

    def body(w_ref, m_ref, v_ref, g_ref, go_ref, d_ref, mo_ref, vo_ref):
        gg = g_ref[...]
        m2 = ADAM_B1 * m_ref[...] + (1.0 - ADAM_B1) * gg
        v2 = ADAM_B2 * v_ref[...] + (1.0 - ADAM_B2) * (gg * gg)
        go_ref[...] = gg
        mo_ref[...] = m2
        vo_ref[...] = v2
        d_ref[...] = -ADAM_LR * ((m2 / bc1) / (jnp.sqrt(v2 / bc2) + ADAM_EPS) + ADAM_WD * w_ref[...])

    spec = pl.BlockSpec((tr, C), lambda r: (r, 0))
    return pl.pallas_call(
        body, name=name, grid=(R // tr,), in_specs=[spec] * 4, out_specs=[spec] * 4,
        out_shape=[jax.ShapeDtypeStruct((R, C), F32)] * 4,
        compiler_params=_cparams("parallel"),
    )(w, m, v, g)


def _adamw_slab(name, w, m, v, g, layer, prev):
    L, R, C = w.shape
    tr = SUBLANES
    while 2 * tr * C <= ADAMW_TILE_ELEMS:
        tr *= 2
    tr = _tile(R, tr)
    bc1 = 1.0 - ADAM_B1 ** ADAM_STEP
    bc2 = 1.0 - ADAM_B2 ** ADAM_STEP

    def body(w_ref, m_ref, v_ref, g_ref, *rest):
        go_ref, d_ref, mo_ref, vo_ref = rest[-4:]
        gg = g_ref[...]
        m2 = ADAM_B1 * m_ref[...] + (1.0 - ADAM_B1) * gg
        v2 = ADAM_B2 * v_ref[...] + (1.0 - ADAM_B2) * (gg * gg)
        go_ref[...] = gg
        mo_ref[...] = m2
        vo_ref[...] = v2
        d_ref[...] = -ADAM_LR * ((m2 / bc1) / (jnp.sqrt(v2 / bc2) + ADAM_EPS) + ADAM_WD * w_ref[...])

    slab = pl.BlockSpec((None, tr, C), lambda r: (layer, r, 0))
    n_prev = 0 if prev is None else 4
    return pl.pallas_call(
        body, name=name, grid=(R // tr,),
        in_specs=[slab] * 3 + [pl.BlockSpec((tr, C), lambda r: (r, 0))] + [_HBM] * n_prev,
        out_specs=[slab] * 4,
        out_shape=[jax.ShapeDtypeStruct((L, R, C), F32)] * 4,
        input_output_aliases={4 + i: i for i in range(n_prev)},
        compiler_params=_cparams("parallel"),
    )(w, m, v, g, *(prev or ()))


def _swap_begin(tag, parts):
    lands = [lax.empty((p.shape[0], p.shape[1] // 2, p.shape[2]), p.dtype) for p in parts]
    return _sibling_start(f"rs_swap_start_{tag}", list(parts) + lands, len(parts), _swap_copies(parts))


def _reduce_begin(tag, early, swapping, late, pos, after):
    send, recv, bufs, _ = swapping
    bufs = _sibling_wait(f"rs_swap_wait_{tag}", bufs, send, recv, _swap_copies(early), [after])
    parts = bufs[:len(early)] + list(late)
    got = bufs[len(early):] + list(_swap_halves(f"rs_swap_{tag}", late))
    sums = [_add_my_half(f"rs_add2_{tag}_{i}", p, g, pos) for i, (p, g) in enumerate(zip(parts, got))]
    return _scatter_start(f"rs_scatter_start_{tag}", sums)


def _reduce_middle(tag, started, pos, afters):
    send, recv, sums, lands, _ = started
    sums, lands = _scatter_wait(f"rs_scatter_wait_{tag}", sums, lands, send, recv, afters)
    fulls = [_add_owner(f"rs_add4_{tag}_{i}", s, q, pos) for i, (s, q) in enumerate(zip(sums, lands))]
    return _sibling_start(f"rs_join_start_{tag}", fulls, len(fulls), _join_copies(fulls))


def _reduce_end(tag, joining, afters):
    send, recv, fulls, _ = joining
    return _sibling_wait(f"rs_join_wait_{tag}", fulls, send, recv, _join_copies(fulls), afters)


def _pad_rows(a):
    r = (-a.shape[0]) % SUBLANES
    return jnp.pad(a, ((0, r), (0, 0))) if r else a


def kernel(x, p, norm_mix, norm_mlp, norm_ple, cf_w_pw1, cf_b_pw1, cf_w_dw, cf_b_dw, cf_norm, cf_w_pw2, cf_b_pw2, sc_w_in, sc_w_conv, sc_w_out, mlp_w1, mlp_w2, ple_w_proj, ple_w_gate, norm_final, loss_target, m_norm_mix, m_norm_mlp, m_norm_ple, m_cf_w_pw1, m_cf_b_pw1, m_cf_w_dw, m_cf_b_dw, m_cf_norm, m_cf_w_pw2, m_cf_b_pw2, m_sc_w_in, m_sc_w_conv, m_sc_w_out, m_mlp_w1, m_mlp_w2, m_ple_w_proj, m_ple_w_gate, m_norm_final, v_norm_mix, v_norm_mlp, v_norm_ple, v_cf_w_pw1, v_cf_b_pw1, v_cf_w_dw, v_cf_b_dw, v_cf_norm, v_cf_w_pw2, v_cf_b_pw2, v_sc_w_in, v_sc_w_conv, v_sc_w_out, v_mlp_w1, v_mlp_w2, v_ple_w_proj, v_ple_w_gate, v_norm_final):
    T, D = x.shape[1], x.shape[2]
    KA, KB = cf_w_dw.shape[1], sc_w_conv.shape[1]
    chip = (2 * lax.axis_index("x") + lax.axis_index("y")).astype(jnp.int32)
    pos = tuple(lax.axis_index(ax).astype(jnp.int32).reshape(1) for ax in ("x", "y", "c"))

    params = dict(norm_mix=norm_mix, norm_mlp=norm_mlp, norm_ple=norm_ple, cf_w_pw1=cf_w_pw1, cf_b_pw1=cf_b_pw1,
                  cf_w_dw=cf_w_dw, cf_b_dw=cf_b_dw, cf_norm=cf_norm, cf_w_pw2=cf_w_pw2, cf_b_pw2=cf_b_pw2,
                  sc_w_in=sc_w_in, sc_w_conv=sc_w_conv, sc_w_out=sc_w_out, mlp_w1=mlp_w1, mlp_w2=mlp_w2,
                  ple_w_proj=ple_w_proj, ple_w_gate=ple_w_gate, norm_final=norm_final)
    mom1 = dict(norm_mix=m_norm_mix, norm_mlp=m_norm_mlp, norm_ple=m_norm_ple, cf_w_pw1=m_cf_w_pw1,
                cf_b_pw1=m_cf_b_pw1, cf_w_dw=m_cf_w_dw, cf_b_dw=m_cf_b_dw, cf_norm=m_cf_norm, cf_w_pw2=m_cf_w_pw2,
                cf_b_pw2=m_cf_b_pw2, sc_w_in=m_sc_w_in, sc_w_conv=m_sc_w_conv, sc_w_out=m_sc_w_out,
                mlp_w1=m_mlp_w1, mlp_w2=m_mlp_w2, ple_w_proj=m_ple_w_proj, ple_w_gate=m_ple_w_gate,
                norm_final=m_norm_final)
    mom2 = dict(norm_mix=v_norm_mix, norm_mlp=v_norm_mlp, norm_ple=v_norm_ple, cf_w_pw1=v_cf_w_pw1,
                cf_b_pw1=v_cf_b_pw1, cf_w_dw=v_cf_w_dw, cf_b_dw=v_cf_b_dw, cf_norm=v_cf_norm, cf_w_pw2=v_cf_w_pw2,
                cf_b_pw2=v_cf_b_pw2, sc_w_in=v_sc_w_in, sc_w_conv=v_sc_w_conv, sc_w_out=v_sc_w_out,
                mlp_w1=v_mlp_w1, mlp_w2=v_mlp_w2, ple_w_proj=v_ple_w_proj, ple_w_gate=v_ple_w_gate,
                norm_final=v_norm_final)

    big = ("cf_w_pw1", "cf_w_pw2", "sc_w_in", "sc_w_out", "mlp_w1", "mlp_w2", "ple_w_proj", "ple_w_gate")
    row_sharded = ("cf_w_pw2", "sc_w_out", "mlp_w2", "ple_w_gate")

    def layer_names(i):
        return (["cf_w_pw1", "cf_w_pw2"] if i % 2 == 0 else ["sc_w_in", "sc_w_out"]) + \
            ["mlp_w1", "mlp_w2", "ple_w_proj", "ple_w_gate"]

    def layer_index(i, name):
        return i if name.startswith(("mlp", "ple")) else i // 2

    def gather_begin(tag, i, names, after):
        bufs = [_cast_place(f"place_{nm}_{i}", params[nm], layer_index(i, nm), pos) for nm in names]
        return names, _gather_start(f"gather_start_{tag}", bufs, after)

    def gather_end(tag, begun, after):
        names, (send, recv, bufs, _) = begun
        bufs = _gather_wait(f"gather_wait_{tag}", bufs, send, recv, after)
        bufs = _gather_forward(f"gather_fwd_{tag}", bufs)
        return {nm: g4.reshape(1, N_SHARDS * g4.shape[1], g4.shape[2]) if nm in row_sharded else g4
                for nm, g4 in zip(names, bufs)}

    conv_small = jnp.concatenate([_pad_rows(cf_w_dw[j]) for j in range(cf_w_dw.shape[0])]
                                 + [_pad_rows(sc_w_conv[j]) for j in range(sc_w_conv.shape[0])], axis=0)
    conv_shards = _gather_shards("gather_conv_w", [(conv_small, None)])[0]
    conv_all = jnp.transpose(conv_shards, (1, 0, 2)).reshape(conv_small.shape[0], D)
    ka_pad = KA + (-KA) % SUBLANES
    kb_pad = KB + (-KB) % SUBLANES
    w_dw_full = [conv_all[j * ka_pad:j * ka_pad + KA] for j in range(cf_w_dw.shape[0])]
    off = cf_w_dw.shape[0] * ka_pad
    w_conv_full = [conv_all[off + j * kb_pad:off + j * kb_pad + KB] for j in range(sc_w_conv.shape[0])]

    def vec(a):
        return a.reshape(1, -1)

    ident = lambda acc: (acc,)

    h = x[0]
    saved = []
    first = gather_begin("0m", 0, layer_names(0)[:2], conv_shards)
    rest = gather_begin("0r", 0, layer_names(0)[2:], first[1][2][0])
    W = [gather_end("0m", first, h)]
    for i in range(DEPTH):
        j = i // 2
        wl = W[i]
        s = dict(h=h)
        g_mix = vec(norm_mix[i])
        if i + 1 < DEPTH:
            nxt = gather_begin(f"{i + 1}", i + 1, layer_names(i + 1), wl[layer_names(i)[0]])
            g_mix = g_mix + nxt[1][3][0, 0]
        s["u"] = _rms_fwd(f"rms_mix_{i}", h, g_mix)
        if i % 2 == 0:
            s["a"] = _mm_nn(f"cf_pw1_{i}", s["u"], wl["cf_w_pw1"], lambda acc, b: (acc + b,), [BF16],
                            extras=[(vec(cf_b_pw1[j]), "n")])[0]
            s["v2"], s["v4"] = _cf_conv_fwd(f"cf_conv_{i}", s["a"], w_dw_full[j], vec(cf_b_dw[j]), vec(cf_norm[j]))
            h1 = _mm_nn(f"cf_pw2_{i}", s["v4"], wl["cf_w_pw2"], lambda acc, b, r: (r + (acc + b),), [F32],
                        extras=[(vec(cf_b_pw2[j]), "n"), (h, "mn")])[0]
        else:
            s["bcv"] = _mm_nn(f"sc_in_{i}", s["u"], wl["sc_w_in"], ident, [BF16])[0]
            s["y"] = _sc_conv_fwd(f"sc_conv_{i}", s["bcv"], w_conv_full[j])[0]
            h1 = _mm_nn(f"sc_out_{i}", s["y"], wl["sc_w_out"], lambda acc, r: (r + acc,), [F32],
                        extras=[(h, "mn")])[0]
        s["h1"] = h1
        if i == 0:
            wl.update(gather_end("0r", rest, h1))
        s["u2"] = _rms_fwd(f"rms_mlp_{i}", h1, vec(norm_mlp[i]))
        s["z"], s["hd"] = _mm_nn(f"mlp_w1_{i}", s["u2"], wl["mlp_w1"],
                                 lambda acc: (acc, jnp.square(jnp.maximum(acc, 0.0))), [BF16, BF16])
        h2 = _mm_nn(f"mlp_w2_{i}", s["hd"], wl["mlp_w2"], lambda acc, r: (r + acc,), [F32], extras=[(h1, "mn")])[0]
        s["h2"] = h2
        s["n3"] = _rms_fwd(f"rms_ple_{i}", h2, vec(norm_ple[i]))
        s["p"] = p[i, 0]
        s["e"] = _mm_nn(f"ple_proj_{i}", s["p"], wl["ple_w_proj"], ident, [BF16])[0]
        h, s["q"] = _mm_nn(f"ple_gate_{i}", s["n3"], wl["ple_w_gate"],
                           lambda acc, r, e: (r + _sigmoid(acc) * e.astype(F32), acc), [F32, BF16],
                           extras=[(h2, "mn"), (s["e"], "mn")])
        saved.append(s)
        if i + 1 < DEPTH:
            W.append(gather_end(f"{i + 1}", nxt, h))

    dh, dh16, dg_final, loss_cols = _loss_bwd("loss_bwd", h, vec(norm_final), loss_target[0])
    small = {"norm_final": dg_final, "loss": loss_cols}
    adam = {nm: None for nm in big}

    def reduce_names(i):
        names = layer_names(i)
        return names[2:] + names[:2]

    def update_layer(i, joining, afters):
        for nm, g in zip(reduce_names(i), _reduce_end(f"{i}", joining, afters)):
            l = layer_index(i, nm)
            adam[nm] = _adamw_slab(f"adamw_{nm}_{l}", params[nm], mom1[nm], mom2[nm], g, l, adam[nm])

    def shard_major(names, parts):
        return [pt.reshape(N_SHARDS, pt.shape[1] // N_SHARDS, pt.shape[2]) if nm in row_sharded else pt
                for nm, pt in zip(names, parts)]

    scattered = joining = None
    for i in reversed(range(DEPTH)):
        j = i // 2
        wl, s = W[i], saved[i]
        dq, de = _ple_elem_bwd(f"ple_elem_bwd_{i}", dh, s["q"], s["e"])
        d_proj = _mm_tn(f"ple_proj_dw_{i}", s["p"], de, N_SHARDS)
        d_gate = _mm_tn(f"ple_gate_dw_{i}", s["n3"], dq, 1)
        dn3 = _mm_nt(f"ple_gate_dx_{i}", dq, wl["ple_w_gate"], ident, [F32])[0]
        g_ple = vec(norm_ple[i])
        if joining is not None:
            g_ple = g_ple + joining[3][0, 0]
        dh, dh16, small[f"norm_ple_{i}"] = _rms_bwd(f"rms_ple_bwd_{i}", s["h2"], g_ple, dn3, dh)

        d_w2 = _mm_tn(f"mlp_w2_dw_{i}", s["hd"], dh16, 1)
        dz = _mm_nt(f"mlp_w2_dx_{i}", dh16, wl["mlp_w2"],
                    lambda acc, z: (acc * (2.0 * jnp.maximum(z.astype(F32), 0.0)),), [BF16], extras=[s["z"]])[0]
        d_w1 = _mm_tn(f"mlp_w1_dw_{i}", s["u2"], dz, N_SHARDS)
        du2 = _mm_nt(f"mlp_w1_dx_{i}", dz, wl["mlp_w1"], ident, [F32])[0]
        early = shard_major(reduce_names(i)[:4], [d_w1, d_w2, d_proj, d_gate])
        swapping = _swap_begin(f"{i}", early)
        g_mlp = vec(norm_mlp[i]) + swapping[3][0, 0]
        if i % 2 == 0:
            dh, dh16, small[f"norm_mlp_{i}"], small[f"cf_b_pw2_{j}"] = _rms_bwd(
                f"rms_mlp_bwd_{i}", s["h1"], g_mlp, du2, dh, want_colsum=True)
            d_mix_out = _mm_tn(f"cf_pw2_dw_{i}", s["v4"], dh16, 1)
            dv4 = _mm_nt(f"cf_pw2_dx_{i}", dh16, wl["cf_w_pw2"], ident, [F32])[0]
            dv2, small[f"cf_norm_{j}"], small[f"cf_b_dw_{j}"] = _cf_norm_bwd(
                f"cf_norm_bwd_{i}", s["v2"], vec(cf_norm[j]), dv4)
            da, small[f"cf_w_dw_{j}"], db1 = _cf_conv_bwd(f"cf_conv_bwd_{i}", dv2, s["a"], w_dw_full[j])
            small[f"cf_b_pw1_{j}"] = db1.reshape(2, D)
            d_mix_in = _mm_tn(f"cf_pw1_dw_{i}", s["u"], da, N_SHARDS)
            du = _mm_nt(f"cf_pw1_dx_{i}", da, wl["cf_w_pw1"], ident, [F32])[0]
        else:
            dh, dh16, small[f"norm_mlp_{i}"] = _rms_bwd(f"rms_mlp_bwd_{i}", s["h1"], g_mlp, du2, dh)
            d_mix_out = _mm_tn(f"sc_out_dw_{i}", s["y"], dh16, 1)
            dy = _mm_nt(f"sc_out_dx_{i}", dh16, wl["sc_w_out"], ident, [F32])[0]
            da, small[f"sc_w_conv_{j}"] = _sc_conv_bwd(f"sc_conv_bwd_{i}", dy, s["bcv"], w_conv_full[j])
            d_mix_in = _mm_tn(f"sc_in_dw_{i}", s["u"], da, N_SHARDS)
            du = _mm_nt(f"sc_in_dx_{i}", da, wl["sc_w_in"], ident, [F32])[0]

        late = shard_major(reduce_names(i)[4:], [d_mix_in, d_mix_out])
        started = _reduce_begin(f"{i}", early, swapping, late, pos, du)
        token = started[4]
        dh, dh16, small[f"norm_mix_{i}"] = _rms_bwd(f"rms_mix_bwd_{i}", s["h"], vec(norm_mix[i]) + token[0, 0], du, dh)
        if joining is not None:
            update_layer(i + 2, joining, [token])
            joining = None
        if scattered is not None:
            joining = _reduce_middle(f"{i + 1}", scattered, pos, [token])
        scattered = started
    grad_x = dh.reshape(x.shape)

    order = sorted(small)
    pieces, where, row = [], {}, 0
    for nm in order:
        pc = _pad_rows(small[nm])
        where[nm] = (row, small[nm].shape[0])
        row += pc.shape[0]
        pieces.append(pc)
    update_layer(1, joining, [token])
    updated = [res[3] for res in adam.values() if res is not None]
    total, loss_tile = _sum_over_devices("small_allsum", jnp.concatenate(pieces, axis=0), where["loss"][0], updated)
    loss = loss_tile[0, 0]
    update_layer(0, _reduce_middle("0", scattered, pos, [total] + updated), [])

    def small_sum(nm):
        r0, nr = where[nm]
        return total[r0:r0 + nr]

    def my_cols(a):
        return lax.dynamic_slice_in_dim(a, chip * (D // N_SHARDS), D // N_SHARDS, axis=1)

    g_small = {
        "norm_mix": jnp.concatenate([small_sum(f"norm_mix_{i}") for i in range(DEPTH)], axis=0),
        "norm_mlp": jnp.concatenate([small_sum(f"norm_mlp_{i}") for i in range(DEPTH)], axis=0),
        "norm_ple": jnp.concatenate([small_sum(f"norm_ple_{i}") for i in range(DEPTH)], axis=0),
        "cf_b_pw1": jnp.stack([small_sum(f"cf_b_pw1_{j}").reshape(2 * D) for j in range(DEPTH // 2)]),
        "cf_w_dw": jnp.stack([my_cols(small_sum(f"cf_w_dw_{j}")) for j in range(DEPTH // 2)]),
        "cf_b_dw": jnp.concatenate([small_sum(f"cf_b_dw_{j}") for j in range(DEPTH // 2)], axis=0),
        "cf_norm": jnp.concatenate([small_sum(f"cf_norm_{j}") for j in range(DEPTH // 2)], axis=0),
        "cf_b_pw2": jnp.concatenate([small_sum(f"cf_b_pw2_{j}") for j in range(DEPTH // 2)], axis=0),
        "sc_w_conv": jnp.stack([my_cols(small_sum(f"sc_w_conv_{j}")) for j in range(DEPTH // 2)]),
        "norm_final": small_sum("norm_final").reshape(D),
    }

    names_out = ["norm_mix", "norm_mlp", "norm_ple", "cf_w_pw1", "cf_b_pw1", "cf_w_dw", "cf_b_dw", "cf_norm",
                 "cf_w_pw2", "cf_b_pw2", "sc_w_in", "sc_w_conv", "sc_w_out", "mlp_w1", "mlp_w2", "ple_w_proj",
                 "ple_w_gate", "norm_final"]
    grad, delta, new_m, new_v = {}, {}, {}, {}
    for nm in names_out:
        w = params[nm]
        if nm in big:
            grad[nm], delta[nm], new_m[nm], new_v[nm] = adam[nm]
            continue
        g = g_small[nm]
        cols = w.shape[-1] if w.ndim > 1 else w.shape[0]
        two_d = lambda a: a.reshape(-1, cols)
        res = _adamw(f"adamw_{nm}", two_d(w), two_d(mom1[nm]), two_d(mom2[nm]), two_d(g))
        grad[nm], delta[nm], new_m[nm], new_v[nm] = [r.reshape(w.shape) for r in res]

    return (loss, grad_x, *[grad[n] for n in names_out], *[delta[n] for n in names_out],
            *[new_m[n] for n in names_out], *[new_v[n] for n in names_out])
```

```python
import functools

import jax
import jax.numpy as jnp
from jax import lax
from jax.experimental import pallas as pl
from jax.experimental.pallas import tpu as pltpu

F32 = jnp.float32
BF16 = jnp.bfloat16

EPS = 1e-6
ADAM_LR = 0.001
ADAM_B1 = 0.9
ADAM_B2 = 0.999
ADAM_EPS = 1e-08
ADAM_WD = 0.01
ADAM_STEP = 10

DEPTH = 4
N_SHARDS = 4
N_DEVICES = 8
V7X_VMEM_LIMIT_BYTES = 56 * 1024 * 1024
SUBLANES = 8
MESH = pl.DeviceIdType.MESH

MM_TM = 1024
MM_TN = 1024
MM_TK = 2048
MM_TW_K = 1024
MM_TW_T = 4096
MM_TX_K = 1024
MM_TX_N = 2048
MM_TX_K_SHARDS = 512
MM_TX_K_LONG = 256
MM_TX_LONG_N = 4096
MM_TN_CROWDED = 512
MM_LONG_K = 8192
MM_TN_LONG_K = 256
ADAMW_TILE_ELEMS = 256 * 2048

CONV_ROW_CHUNK = 32
CONV_LANE_CHUNK = 512
CONV_TILE_ROWS = 128
CONV_FWD_TILE_ROWS = 256
ROW_TILE = 256


def _tile(dim, pref):
    if dim <= pref:
        return dim
    t = pref
    while dim % t:
        t //= 2
    return t


def _cparams(*sem):
    return pltpu.CompilerParams(dimension_semantics=sem, vmem_limit_bytes=V7X_VMEM_LIMIT_BYTES)


def _sigmoid(x):
    return 0.5 * (jnp.tanh(0.5 * x) + 1.0)


def _rms_r(x):
    return lax.rsqrt(jnp.mean(x * x, axis=-1, keepdims=True) + EPS)


def _mm_nn(name, a, b3, epilogue, out_dtypes, extras=()):
    M, K = a.shape
    S, Kb, Ns = b3.shape
    assert Kb == K
    N = S * Ns
    crowded = sum(kind == "mn" for _, kind in extras) > 1
    tm, tn, tk = _tile(M, MM_TM), _tile(Ns, MM_TN_CROWDED if crowded else MM_TN), _tile(K, MM_TK)
    if K >= MM_LONG_K:
        tn, tk = _tile(Ns, MM_TN_LONG_K), K
    per = Ns // tn
    nk = K // tk
    in_specs = [pl.BlockSpec((tm, tk), lambda i, j, k: (i, k)),
                pl.BlockSpec((None, tk, tn), lambda i, j, k: (j // per, k, j % per))]
    for _, kind in extras:
        if kind == "mn":
            in_specs.append(pl.BlockSpec((tm, tn), lambda i, j, k: (i, j)))
        else:
            in_specs.append(pl.BlockSpec((1, tn), lambda i, j, k: (0, j)))
    n_ex, n_o = len(extras), len(out_dtypes)

    def body(*refs):
        a_ref, b_ref = refs[:2]
        ex = refs[2:2 + n_ex]
        outs = refs[2 + n_ex:2 + n_ex + n_o]
        part = jnp.dot(a_ref[...].astype(BF16), b_ref[...], preferred_element_type=F32)

        def finish(acc):
            res = epilogue(acc, *[e[...] for e in ex])
            for r, o in zip(res, outs):
                o[...] = r.astype(o.dtype)

        if nk == 1:
            finish(part)
        else:
            acc_ref = refs[-1]
            k = pl.program_id(2)

            @pl.when(k == 0)
            def _():
                acc_ref[...] = part

            @pl.when(k > 0)
            def _():
                acc_ref[...] += part

            @pl.when(k == nk - 1)
            def _():
                finish(acc_ref[...])

    res = pl.pallas_call(
        body, name=name, grid=(M // tm, N // tn, nk),
        in_specs=in_specs,
        out_specs=[pl.BlockSpec((tm, tn), lambda i, j, k: (i, j)) for _ in out_dtypes],
        out_shape=[jax.ShapeDtypeStruct((M, N), dt) for dt in out_dtypes],
        scratch_shapes=[pltpu.VMEM((tm, tn), F32)] if nk > 1 else [],
        compiler_params=_cparams("parallel", "parallel", "arbitrary"),
    )(a, b3, *[e for e, _ in extras])
    return res


def _mm_nt_shards(name, g, w3, epilogue, out_dtypes, extras):
    M, N = g.shape
    S, K, Ns = w3.shape
    tm = _tile(M, MM_TM)
    tkk = _tile(K, MM_TX_K_LONG if N > MM_TX_LONG_N else MM_TX_K_SHARDS)
    n_ex, n_o = len(extras), len(out_dtypes)

    def body(*refs):
        g_ref = refs[0]
        w_refs = refs[1:1 + S]
        ex = refs[1 + S:1 + S + n_ex]
        outs = refs[1 + S + n_ex:1 + S + n_ex + n_o]
        acc = None
        for s in range(S):
            part = lax.dot_general(g_ref[:, s * Ns:(s + 1) * Ns].astype(BF16), w_refs[s][...],
                                   (((1,), (1,)), ((), ())), preferred_element_type=F32)
            acc = part if acc is None else acc + part
        for r, o in zip(epilogue(acc, *[e[...] for e in ex]), outs):
            o[...] = r.astype(o.dtype)

    return pl.pallas_call(
        body, name=name, grid=(M // tm, K // tkk),
        in_specs=[pl.BlockSpec((tm, N), lambda i, kk: (i, 0))]
        + [pl.BlockSpec((None, tkk, Ns), lambda i, kk, s=s: (s, kk, 0)) for s in range(S)]
        + [pl.BlockSpec((tm, tkk), lambda i, kk: (i, kk)) for _ in extras],
        out_specs=[pl.BlockSpec((tm, tkk), lambda i, kk: (i, kk)) for _ in out_dtypes],
        out_shape=[jax.ShapeDtypeStruct((M, K), dt) for dt in out_dtypes],
        compiler_params=_cparams("parallel", "parallel"),
    )(g, *([w3] * S), *extras)


def _mm_nt(name, g, w3, epilogue, out_dtypes, extras=()):
    M, N = g.shape
    S, K, Ns = w3.shape
    assert S * Ns == N
    if S > 1:
        return _mm_nt_shards(name, g, w3, epilogue, out_dtypes, extras)
    tm, tn, tkk = _tile(M, MM_TM), _tile(Ns, MM_TX_N), _tile(K, MM_TX_K)
    per = Ns // tn
    nn = N // tn
    n_ex, n_o = len(extras), len(out_dtypes)

    def body(*refs):
        g_ref, w_ref = refs[:2]
        ex = refs[2:2 + n_ex]
        outs = refs[2 + n_ex:2 + n_ex + n_o]
        part = lax.dot_general(g_ref[...].astype(BF16), w_ref[...], (((1,), (1,)), ((), ())),
                               preferred_element_type=F32)

        def finish(acc):
            res = epilogue(acc, *[e[...] for e in ex])
            for r, o in zip(res, outs):
                o[...] = r.astype(o.dtype)

        if nn == 1:
            finish(part)
        else:
            acc_ref = refs[-1]
            n = pl.program_id(2)

            @pl.when(n == 0)
            def _():
                acc_ref[...] = part

            @pl.when(n > 0)
            def _():
                acc_ref[...] += part

            @pl.when(n == nn - 1)
            def _():
                finish(acc_ref[...])

    return pl.pallas_call(
        body, name=name, grid=(M // tm, K // tkk, nn),
        in_specs=[pl.BlockSpec((tm, tn), lambda i, kk, n: (i, n)),
                  pl.BlockSpec((None, tkk, tn), lambda i, kk, n: (n // per, kk, n % per))]
        + [pl.BlockSpec((tm, tkk), lambda i, kk, n: (i, kk)) for _ in extras],
        out_specs=[pl.BlockSpec((tm, tkk), lambda i, kk, n: (i, kk)) for _ in out_dtypes],
        out_shape=[jax.ShapeDtypeStruct((M, K), dt) for dt in out_dtypes],
        scratch_shapes=[pltpu.VMEM((tm, tkk), F32)] if nn > 1 else [],
        compiler_params=_cparams("parallel", "parallel", "arbitrary"),
    )(g, w3, *extras)


def _mm_tn(name, a, g, n_shards):
    T, K = a.shape
    _, N = g.shape
    Ns = N // n_shards
    tk, tn, tt = _tile(K, MM_TW_K), _tile(Ns, MM_TN), _tile(T, MM_TW_T)
    per = Ns // tn
    nt = T // tt

    def body(a_ref, g_ref, o_ref, *scratch):
        part = lax.dot_general(a_ref[...].astype(BF16), g_ref[...].astype(BF16), (((0,), (0,)), ((), ())),
                               preferred_element_type=F32)
        if nt == 1:
            o_ref[...] = part.astype(o_ref.dtype)
            return
        acc_ref, = scratch
        t = pl.program_id(2)

        @pl.when(t == 0)
        def _():
            acc_ref[...] = part

        @pl.when(t > 0)
        def _():
            acc_ref[...] += part

        @pl.when(t == nt - 1)
        def _():
            o_ref[...] = acc_ref[...].astype(o_ref.dtype)

    return pl.pallas_call(
        body, name=name, grid=(K // tk, N // tn, nt),
        in_specs=[pl.BlockSpec((tt, tk), lambda i, j, t: (t, i)),
                  pl.BlockSpec((tt, tn), lambda i, j, t: (t, j))],
        out_specs=pl.BlockSpec((None, tk, tn), lambda i, j, t: (j // per, i, j % per)),
        out_shape=jax.ShapeDtypeStruct((n_shards, K, Ns), BF16),
        scratch_shapes=[pltpu.VMEM((tk, tn), F32)] if nt > 1 else [],
        compiler_params=_cparams("parallel", "parallel", "arbitrary"),
    )(a, g)


def _rowwise(name, fn, ins, outs, accs=(), scratch=(), tt=ROW_TILE):
    T = next(a.shape[0] for a, kind in ins if kind == "row")
    tt = _tile(T, tt)
    n = T // tt
    in_specs = []
    for a, kind in ins:
        w = a.shape[1]
        if kind == "row":
            in_specs.append(pl.BlockSpec((tt, w), lambda i: (i, 0)))
        elif kind == "vec":
            in_specs.append(pl.BlockSpec(a.shape, lambda i: (0, 0)))
        elif kind[0] == "prev":
            pad = kind[1]
            in_specs.append(pl.BlockSpec((pad, w), lambda i, q=tt // pad: (jnp.maximum(i * q - 1, 0), 0)))
        else:
            pad = kind[1]
            in_specs.append(pl.BlockSpec((pad, w), lambda i, q=tt // pad, last=T // pad - 1:
                                         (jnp.minimum((i + 1) * q, last), 0)))
    n_in, n_out, n_acc = len(ins), len(outs), len(accs)

    def body(*refs):
        i = pl.program_id(0)
        in_refs = refs[:n_in]
        out_refs = refs[n_in:n_in + n_out]
        acc_refs = refs[n_in + n_out:n_in + n_out + n_acc]
        scr = refs[n_in + n_out + n_acc:]
        if n_acc:
            @pl.when(i == 0)
            def _():
                for r in acc_refs:
                    r[...] = jnp.zeros_like(r)
        fn(i, n, in_refs, out_refs, acc_refs, scr)

    res = pl.pallas_call(
        body, name=name, grid=(n,),
        in_specs=in_specs,
        out_specs=[pl.BlockSpec((tt, w), lambda i: (i, 0)) for w, _ in outs]
        + [pl.BlockSpec((r, w), lambda i: (0, 0)) for r, w in accs],
        out_shape=[jax.ShapeDtypeStruct((T, w), dt) for w, dt in outs]
        + [jax.ShapeDtypeStruct((r, w), F32) for r, w in accs],
        scratch_shapes=list(scratch),
        compiler_params=_cparams("arbitrary"),
    )(*[a for a, _ in ins])
    return res


def _colsum(x):
    return jnp.sum(x, axis=0, keepdims=True)


def _rms_fwd(name, h, g):
    D = h.shape[1]

    def fn(i, n, ins, outs, accs, scr):
        x = ins[0][...]
        outs[0][...] = (x * _rms_r(x) * ins[1][...]).astype(BF16)

    return _rowwise(name, fn, [(h, "row"), (g, "vec")], [(D, BF16)])[0]


def _rms_bwd(name, h, g, du, dh_in, want_colsum=False):
    D = h.shape[1]

    def fn(i, n, ins, outs, accs, scr):
        x = ins[0][...]
        gg = ins[1][...]
        d = ins[2][...].astype(F32)
        r = _rms_r(x)
        xn = x * r
        t = d * gg
        dh = ins[3][...] + r * (t - xn * jnp.mean(t * xn, axis=-1, keepdims=True))
        outs[0][...] = dh
        outs[1][...] = dh.astype(BF16)
        accs[0][...] += _colsum(d * xn)
        if want_colsum:
            accs[1][...] += _colsum(dh)

    return _rowwise(name, fn, [(h, "row"), (g, "vec"), (du, "row"), (dh_in, "row")],
                    [(D, F32), (D, BF16)], accs=[(1, D)] * (2 if want_colsum else 1))


def _loss_bwd(name, h, g, tgt):
    D = h.shape[1]

    def fn(i, n, ins, outs, accs, scr):
        x = ins[0][...]
        gg = ins[1][...]
        r = _rms_r(x)
        xn = x * r
        err = xn * gg - ins[2][...]
        dy = err / D
        t = dy * gg
        dh = r * (t - xn * jnp.mean(t * xn, axis=-1, keepdims=True))
        outs[0][...] = dh
        outs[1][...] = dh.astype(BF16)
        accs[0][...] += _colsum(dy * xn)
        accs[1][...] += _colsum(err * err)

    return _rowwise(name, fn, [(h, "row"), (g, "vec"), (tgt, "row")], [(D, F32), (D, BF16)],
                    accs=[(1, D), (1, D)])


def _ple_elem_bwd(name, dh, q, e):
    D = dh.shape[1]

    def fn(i, n, ins, outs, accs, scr):
        d = ins[0][...]
        s = _sigmoid(ins[1][...].astype(F32))
        ee = ins[2][...].astype(F32)
        outs[0][...] = (d * ee * s * (1.0 - s)).astype(BF16)
        outs[1][...] = (d * s).astype(BF16)

    return _rowwise(name, fn, [(dh, "row"), (q, "row"), (e, "row")], [(D, BF16), (D, BF16)])


def _cf_norm_bwd(name, v2, g, dv4):
    D = v2.shape[1]

    def fn(i, n, ins, outs, accs, scr):
        x = ins[0][...]
        gg = ins[1][...]
        r = _rms_r(x)
        xn = x * r
        v3 = xn * gg
        s = _sigmoid(v3)
        dv3 = ins[2][...].astype(F32) * (s * (1.0 + v3 * (1.0 - s)))
        t = dv3 * gg
        dv2 = r * (t - xn * jnp.mean(t * xn, axis=-1, keepdims=True))
        outs[0][...] = dv2
        accs[0][...] += _colsum(dv3 * xn)
        accs[1][...] += _colsum(dv2)

    return _rowwise(name, fn, [(v2, "row"), (g, "vec"), (dv4, "row")], [(D, F32)], accs=[(1, D), (1, D)])


def _chunks(tt, width):
    cc = min(CONV_LANE_CHUNK, width)
    rc = min(CONV_ROW_CHUNK, tt)
    for c0 in range(0, width, cc):
        for r0 in range(0, tt, rc):
            yield r0, rc, c0, cc


def _n_shifts(n_taps):
    return min(SUBLANES - 1, n_taps - 1)


def _shifted_scratch(n_taps, rows, width):
    return pltpu.VMEM((_n_shifts(n_taps), rows, width), F32)


def _shift_window(win_ref, sh_ref, n_taps, sign):
    rows = win_ref.shape[0] - SUBLANES
    width = win_ref.shape[1]
    cc = min(CONV_LANE_CHUNK, width)
    for b in range(1, _n_shifts(n_taps) + 1):
        off = SUBLANES - b if sign < 0 else b
        for c0 in range(0, width, cc):
            sh_ref[b - 1, 0:rows, c0:c0 + cc] = win_ref[off:off + rows, c0:c0 + cc]


def _tap(win_ref, sh_ref, base, sign, s, r0, rc, c0, cc):
    a, b = divmod(s, SUBLANES)
    if b == 0:
        row = base + r0 + sign * SUBLANES * a
        return win_ref[row:row + rc, c0:c0 + cc]
    row = base + r0 - SUBLANES * (a + 1) if sign < 0 else base + r0 + SUBLANES * a
    return sh_ref[b - 1, row:row + rc, c0:c0 + cc]


def _fir(win_ref, sh_ref, w_ref, n_taps, base, sign, tt, width, emit):
    for r0, rc, c0, cc in _chunks(tt, width):
        acc = jnp.zeros((rc, cc), F32)
        for k in range(n_taps):
            acc = acc + w_ref[k:k + 1, c0:c0 + cc] * _tap(win_ref, sh_ref, base, sign, n_taps - 1 - k, r0, rc, c0, cc)
        emit(r0, rc, c0, cc, acc)


def _fir_wgrad(d_ref, win_ref, sh_ref, dw8_ref, n_taps, pad, tt, width):
    for c0 in range(0, width, min(CONV_LANE_CHUNK, width)):
        cc = min(CONV_LANE_CHUNK, width)
        rc = min(CONV_ROW_CHUNK, tt)
        for k in range(n_taps):
            acc = jnp.zeros((SUBLANES, cc), F32)
            for r0 in range(0, tt, rc):
                prod = d_ref[r0:r0 + rc, c0:c0 + cc] * _tap(win_ref, sh_ref, pad, -1, n_taps - 1 - k, r0, rc, c0, cc)
                for q in range(0, rc, SUBLANES):
                    acc = acc + prod[q:q + SUBLANES]
            dw8_ref[SUBLANES * k:SUBLANES * (k + 1), c0:c0 + cc] += acc


def _glu(blk, D):
    return blk[:, :D].astype(F32) * _sigmoid(blk[:, D:].astype(F32))


CF_PAD = 32
SC_PAD = 16


def _cf_conv_fwd(name, a, w_dw, b_dw, g_cf):
    T, D2 = a.shape
    D = D2 // 2
    K = w_dw.shape[0]
    tt = _tile(T, CONV_FWD_TILE_ROWS)

    def fn(i, n, ins, outs, accs, scr):
        a_ref, prev_ref, w_ref, b_ref, g_ref = ins
        win_ref, v2_ref, sh_ref = scr
        win_ref[0:CF_PAD, :] = jnp.where(i > 0, _glu(prev_ref[...], D), 0.0)
        win_ref[CF_PAD:CF_PAD + tt, :] = _glu(a_ref[...], D)
        _shift_window(win_ref, sh_ref, K, -1)

        def emit(r0, rc, c0, cc, acc):
            v2_ref[r0:r0 + rc, c0:c0 + cc] = acc + b_ref[:, c0:c0 + cc]

        _fir(win_ref, sh_ref, w_ref, K, CF_PAD, -1, tt, D, emit)
        v2 = v2_ref[...]
        v3 = v2 * _rms_r(v2) * g_ref[...]
        outs[0][...] = v2
        outs[1][...] = (v3 * _sigmoid(v3)).astype(BF16)

    return _rowwise(name, fn, [(a, "row"), (a, ("prev", CF_PAD)), (w_dw, "vec"), (b_dw, "vec"), (g_cf, "vec")],
                    [(D, F32), (D, BF16)],
                    scratch=[pltpu.VMEM((CF_PAD + tt, D), F32), pltpu.VMEM((tt, D), F32),
                             _shifted_scratch(K, CF_PAD + tt, D)], tt=tt)


def _cf_conv_bwd(name, dv2, a, w_dw):
    T, D2 = a.shape
    D = D2 // 2
    K = w_dw.shape[0]
    tt = _tile(T, CONV_TILE_ROWS)

    def fn(i, n, ins, outs, accs, scr):
        d_ref, dnext_ref, a_ref, prev_ref, w_ref = ins
        v1win_ref, dwin_ref, dv1_ref, dw8_ref, v1sh_ref, dsh_ref = scr

        @pl.when(i == 0)
        def _():
            dw8_ref[...] = jnp.zeros_like(dw8_ref)

        v1win_ref[0:CF_PAD, :] = jnp.where(i > 0, _glu(prev_ref[...], D), 0.0)
        v1win_ref[CF_PAD:CF_PAD + tt, :] = _glu(a_ref[...], D)
        dwin_ref[0:tt, :] = d_ref[...]
        dwin_ref[tt:tt + CF_PAD, :] = jnp.where(i < n - 1, dnext_ref[...], 0.0)
        _shift_window(v1win_ref, v1sh_ref, K, -1)
        _shift_window(dwin_ref, dsh_ref, K, 1)

        def emit(r0, rc, c0, cc, acc):
            dv1_ref[r0:r0 + rc, c0:c0 + cc] = acc

        _fir(dwin_ref, dsh_ref, w_ref, K, 0, 1, tt, D, emit)
        _fir_wgrad(d_ref, v1win_ref, v1sh_ref, dw8_ref, K, CF_PAD, tt, D)

        blk = a_ref[...]
        val = blk[:, :D].astype(F32)
        sg = _sigmoid(blk[:, D:].astype(F32))
        dv1 = dv1_ref[...]
        dval = dv1 * sg
        dgate = dv1 * val * sg * (1.0 - sg)
        outs[0][:, :D] = dval.astype(BF16)
        outs[0][:, D:] = dgate.astype(BF16)
        accs[1][:, :D] += _colsum(dval)
        accs[1][:, D:] += _colsum(dgate)

        @pl.when(i == n - 1)
        def _():
            for k in range(K):
                accs[0][k:k + 1, :] = _colsum(dw8_ref[SUBLANES * k:SUBLANES * (k + 1), :])

    return _rowwise(name, fn, [(dv2, "row"), (dv2, ("next", CF_PAD)), (a, "row"), (a, ("prev", CF_PAD)),
                               (w_dw, "vec")],
                    [(D2, BF16)], accs=[(K, D), (1, D2)],
                    scratch=[pltpu.VMEM((CF_PAD + tt, D), F32), pltpu.VMEM((tt + CF_PAD, D), F32),
                             pltpu.VMEM((tt, D), F32), pltpu.VMEM((SUBLANES * K, D), F32),
                             _shifted_scratch(K, CF_PAD + tt, D), _shifted_scratch(K, tt + CF_PAD, D)], tt=tt)


def _sc_conv_fwd(name, bcv, w_conv):
    T, D3 = bcv.shape
    D = D3 // 3
    K = w_conv.shape[0]
    tt = _tile(T, CONV_TILE_ROWS)

    def cv_of(blk):
        return blk[:, D:2 * D].astype(F32) * blk[:, 2 * D:].astype(F32)

    def fn(i, n, ins, outs, accs, scr):
        x_ref, prev_ref, w_ref = ins
        win_ref, cc_ref, sh_ref = scr
        win_ref[0:SC_PAD, :] = jnp.where(i > 0, cv_of(prev_ref[...]), 0.0)
        win_ref[SC_PAD:SC_PAD + tt, :] = cv_of(x_ref[...])
        _shift_window(win_ref, sh_ref, K, -1)

        def emit(r0, rc, c0, cw, acc):
            cc_ref[r0:r0 + rc, c0:c0 + cw] = acc

        _fir(win_ref, sh_ref, w_ref, K, SC_PAD, -1, tt, D, emit)
        outs[0][...] = (x_ref[:, :D].astype(F32) * cc_ref[...]).astype(BF16)

    return _rowwise(name, fn, [(bcv, "row"), (bcv, ("prev", SC_PAD)), (w_conv, "vec")], [(D, BF16)],
                    scratch=[pltpu.VMEM((SC_PAD + tt, D), F32), pltpu.VMEM((tt, D), F32),
                             _shifted_scratch(K, SC_PAD + tt, D)], tt=tt)


def _sc_conv_bwd(name, dy, bcv, w_conv):
    T, D3 = bcv.shape
    D = D3 // 3
    K = w_conv.shape[0]
    tt = _tile(T, CONV_TILE_ROWS)

    def cv_of(blk):
        return blk[:, D:2 * D].astype(F32) * blk[:, 2 * D:].astype(F32)

    def fn(i, n, ins, outs, accs, scr):
        dy_ref, dynext_ref, x_ref, prev_ref, next_ref, w_ref = ins
        cvwin_ref, dccwin_ref, tmp_ref, dw8_ref, cvsh_ref, dccsh_ref = scr

        @pl.when(i == 0)
        def _():
            dw8_ref[...] = jnp.zeros_like(dw8_ref)

        cvwin_ref[0:SC_PAD, :] = jnp.where(i > 0, cv_of(prev_ref[...]), 0.0)
        cvwin_ref[SC_PAD:SC_PAD + tt, :] = cv_of(x_ref[...])
        _shift_window(cvwin_ref, cvsh_ref, K, -1)

        def emit_cc(r0, rc, c0, cw, acc):
            tmp_ref[r0:r0 + rc, c0:c0 + cw] = acc

        _fir(cvwin_ref, cvsh_ref, w_ref, K, SC_PAD, -1, tt, D, emit_cc)
        dy_v = dy_ref[...].astype(F32)
        outs[0][:, :D] = (dy_v * tmp_ref[...]).astype(BF16)
        dccwin_ref[0:tt, :] = dy_v * x_ref[:, :D].astype(F32)
        dccwin_ref[tt:tt + SC_PAD, :] = jnp.where(
            i < n - 1, dynext_ref[...].astype(F32) * next_ref[:, :D].astype(F32), 0.0)

        _shift_window(dccwin_ref, dccsh_ref, K, 1)

        def emit_dcv(r0, rc, c0, cw, acc):
            tmp_ref[r0:r0 + rc, c0:c0 + cw] = acc

        _fir(dccwin_ref, dccsh_ref, w_ref, K, 0, 1, tt, D, emit_dcv)
        _fir_wgrad(dccwin_ref, cvwin_ref, cvsh_ref, dw8_ref, K, SC_PAD, tt, D)
        dcv = tmp_ref[...]
        outs[0][:, D:2 * D] = (dcv * x_ref[:, 2 * D:].astype(F32)).astype(BF16)
        outs[0][:, 2 * D:] = (dcv * x_ref[:, D:2 * D].astype(F32)).astype(BF16)

        @pl.when(i == n - 1)
        def _():
            for k in range(K):
                accs[0][k:k + 1, :] = _colsum(dw8_ref[SUBLANES * k:SUBLANES * (k + 1), :])

    return _rowwise(name, fn, [(dy, "row"), (dy, ("next", SC_PAD)), (bcv, "row"), (bcv, ("prev", SC_PAD)),
                               (bcv, ("next", SC_PAD)), (w_conv, "vec")],
                    [(D3, BF16)], accs=[(K, D)],
                    scratch=[pltpu.VMEM((SC_PAD + tt, D), F32), pltpu.VMEM((tt + SC_PAD, D), F32),
                             pltpu.VMEM((tt, D), F32), pltpu.VMEM((SUBLANES * K, D), F32),
                             _shifted_scratch(K, SC_PAD + tt, D), _shifted_scratch(K, tt + SC_PAD, D)], tt=tt)


def _place():
    x, y, c = lax.axis_index("x"), lax.axis_index("y"), lax.axis_index("c")
    chips = [(1 - x, y), (x, 1 - y), (1 - x, 1 - y)]
    return x, y, c, 2 * x + y, chips, (x, y, 1 - c)


def _half(rows, which):
    return pl.ds(pl.multiple_of(which * (rows // 2), SUBLANES), rows // 2)


_HBM = pl.BlockSpec(memory_space=pl.ANY)


def _gather_shards(name, items):
    n = len(items)
    shapes = [a.shape[-2:] for a, _ in items]

    def body(*refs):
        srcs, outs = refs[:n], refs[n:2 * n]
        send1, recv1, send2, recv2, lsem = refs[2 * n:]
        x, y, c, k, chips, sib = _place()

        def shard(i):
            return srcs[i] if items[i][1] is None else srcs[i].at[items[i][1]]

        started, locs = [], []
        for i in range(n):
            rows = shapes[i][0]
            lc = pltpu.make_async_copy(shard(i), outs[i].at[k], lsem.at[i])
            lc.start()
            locs.append(lc)
            for j, (cx, cy) in enumerate(chips):
                cp = pltpu.make_async_remote_copy(
                    src_ref=shard(i).at[_half(rows, c)], dst_ref=outs[i].at[k, _half(rows, c)],
                    send_sem=send1.at[i, j], recv_sem=recv1.at[i, j], device_id=(cx, cy, c), device_id_type=MESH)
                cp.start()
                started.append(cp)
        for i in range(n):
            rows = shapes[i][0]
            for j, (cx, cy) in enumerate(chips):
                blk = outs[i].at[2 * cx + cy, _half(rows, c)]
                pltpu.make_async_remote_copy(
                    src_ref=blk, dst_ref=blk, send_sem=send1.at[i, j], recv_sem=recv1.at[i, j],
                    device_id=(cx, cy, c), device_id_type=MESH).wait_recv()
                fw = pltpu.make_async_remote_copy(
                    src_ref=blk, dst_ref=blk, send_sem=send2.at[i, j], recv_sem=recv2.at[i, j],
                    device_id=sib, device_id_type=MESH)
                fw.start()
                started.append(fw)
        for i in range(n):
            rows = shapes[i][0]
            for j, (cx, cy) in enumerate(chips):
                blk = outs[i].at[2 * cx + cy, _half(rows, 1 - c)]
                pltpu.make_async_remote_copy(
                    src_ref=blk, dst_ref=blk, send_sem=send2.at[i, j], recv_sem=recv2.at[i, j],
                    device_id=sib, device_id_type=MESH).wait_recv()
        for cp in started:
            cp.wait_send()
        for lc in locs:
            lc.wait()

    return pl.pallas_call(
        body, name=name,
        in_specs=[_HBM] * n, out_specs=[_HBM] * n,
        out_shape=[jax.ShapeDtypeStruct((N_SHARDS,) + tuple(s), a.dtype) for s, (a, _) in zip(shapes, items)],
        scratch_shapes=[pltpu.SemaphoreType.DMA((n, 3))] * 4 + [pltpu.SemaphoreType.DMA((n,))],
    )(*[a for a, _ in items])


def _cast_place(name, w, layer, pos):
    _, R, C = w.shape
    tr = _tile(R, 256)

    def body(x_ref, y_ref, c_ref, w_ref, o_ref):
        o_ref[...] = w_ref[...].astype(BF16)

    return pl.pallas_call(
        body, name=name,
        grid_spec=pltpu.PrefetchScalarGridSpec(
            num_scalar_prefetch=3, grid=(R // tr,),
            in_specs=[pl.BlockSpec((None, tr, C), lambda r, xr, yr, cr: (layer, r, 0))],
            out_specs=pl.BlockSpec((None, tr, C), lambda r, xr, yr, cr: (2 * xr[0] + yr[0], r, 0))),
        out_shape=jax.ShapeDtypeStruct((N_SHARDS, R, C), BF16),
        compiler_params=_cparams("parallel"),
    )(*pos, w)


_IN_HBM = pl.BlockSpec(memory_space=pltpu.HBM)
_SEM = pl.BlockSpec(memory_space=pltpu.SEMAPHORE)
_SPLIT_COPY_PARAMS = pltpu.CompilerParams(has_side_effects=pltpu.SideEffectType.DATAFLOW_SIDE_EFFECTING)


def _in_hbm(a):
    return pltpu.with_memory_space_constraint(a, pltpu.HBM)


def _gather_copy(ref, i, j, chip_xy, c, k_src, rows, send, recv):
    blk = ref.at[k_src, _half(rows, c)]
    return pltpu.make_async_remote_copy(
        src_ref=blk, dst_ref=blk, send_sem=send.at[3 * i + j], recv_sem=recv.at[3 * i + j],
        device_id=(*chip_xy, c), device_id_type=MESH)


def _gather_start(name, bufs, after):
    n = len(bufs)

    def body(*refs):
        ins = refs[:n]
        send, recv = refs[n + 1], refs[n + 2]
        token = refs[-1]
        x, y, c, k, chips, sib = _place()
        for i in range(n):
            for j, chip_xy in enumerate(chips):
                _gather_copy(ins[i], i, j, chip_xy, c, k, bufs[i].shape[1], send, recv).start()
        token[...] = jnp.zeros_like(token)

    res = pl.pallas_call(
        body, name=name,
        in_specs=[_IN_HBM] * n + [_HBM],
        out_specs=[_SEM, _SEM] + [_IN_HBM] * n + [pl.BlockSpec(memory_space=pltpu.VMEM)],
        out_shape=[pltpu.SemaphoreType.DMA((3 * n,)), pltpu.SemaphoreType.DMA((3 * n,))]
        + [pltpu.HBM(b.shape, b.dtype) for b in bufs] + [jax.ShapeDtypeStruct((SUBLANES, 128), F32)],
        input_output_aliases={i: 2 + i for i in range(n)},
        compiler_params=_SPLIT_COPY_PARAMS,
    )(*[_in_hbm(b) for b in bufs], after)
    return res[0], res[1], list(res[2:2 + n]), res[-1]


def _gather_wait(name, bufs, send, recv, after):
    n = len(bufs)

    def body(*refs):
        ins = refs[:n]
        send_ref, recv_ref = refs[n], refs[n + 1]
        x, y, c, k, chips, sib = _place()
        for i in range(n):
            for j, chip_xy in enumerate(chips):
                rows = bufs[i].shape[1]
                _gather_copy(ins[i], i, j, chip_xy, c, k, rows, send_ref, recv_ref).wait_send()
                _gather_copy(ins[i], i, j, chip_xy, c, 2 * chip_xy[0] + chip_xy[1], rows, send_ref, recv_ref).wait_recv()

    return pl.pallas_call(
        body, name=name,
        in_specs=[_IN_HBM] * n + [_SEM, _SEM, _HBM],
        out_specs=[_IN_HBM] * n,
        out_shape=[pltpu.HBM(b.shape, b.dtype) for b in bufs],
        input_output_aliases={i: i for i in range(n)},
        compiler_params=_SPLIT_COPY_PARAMS,
    )(*bufs, send, recv, after)


def _gather_forward(name, bufs):
    n = len(bufs)

    def body(*refs):
        outs = refs[n:2 * n]
        send, recv = refs[2 * n:]
        x, y, c, k, chips, sib = _place()
        started = []
        for i in range(n):
            rows = bufs[i].shape[1]
            for j, (cx, cy) in enumerate(chips):
                blk = outs[i].at[2 * cx + cy, _half(rows, c)]
                fw = pltpu.make_async_remote_copy(
                    src_ref=blk, dst_ref=blk, send_sem=send.at[i, j], recv_sem=recv.at[i, j],
                    device_id=sib, device_id_type=MESH)
                fw.start()
                started.append(fw)
        for i in range(n):
            rows = bufs[i].shape[1]
            for j, (cx, cy) in enumerate(chips):
                blk = outs[i].at[2 * cx + cy, _half(rows, 1 - c)]
                pltpu.make_async_remote_copy(
                    src_ref=blk, dst_ref=blk, send_sem=send.at[i, j], recv_sem=recv.at[i, j],
                    device_id=sib, device_id_type=MESH).wait_recv()
        for cp in started:
            cp.wait_send()

    return pl.pallas_call(
        body, name=name, in_specs=[_HBM] * n, out_specs=[_HBM] * n,
        out_shape=[jax.ShapeDtypeStruct(b.shape, b.dtype) for b in bufs],
        input_output_aliases={i: i for i in range(n)},
        scratch_shapes=[pltpu.SemaphoreType.DMA((n, 3))] * 2,
    )(*bufs)


def _swap_halves(name, parts):
    n = len(parts)

    def body(*refs):
        srcs, outs = refs[:n], refs[n:2 * n]
        send, recv = refs[2 * n:]
        x, y, c, k, chips, sib = _place()
        cps = []
        for i in range(n):
            rows = parts[i].shape[1]
            cp = pltpu.make_async_remote_copy(
                src_ref=srcs[i].at[:, _half(rows, 1 - c)], dst_ref=outs[i],
                send_sem=send.at[i], recv_sem=recv.at[i], device_id=sib, device_id_type=MESH)
            cp.start()
            cps.append(cp)
        for cp in cps:
            cp.wait()

    return pl.pallas_call(
        body, name=name, in_specs=[_HBM] * n, out_specs=[_HBM] * n,
        out_shape=[jax.ShapeDtypeStruct((p.shape[0], p.shape[1] // 2, p.shape[2]), p.dtype) for p in parts],
        scratch_shapes=[pltpu.SemaphoreType.DMA((n,))] * 2,
    )(*parts)


def _sibling_start(name, bufs, n_copies, make):
    nb = len(bufs)

    def body(*refs):
        send, recv = refs[nb], refs[nb + 1]
        token = refs[-1]
        x, y, c, k, chips, sib = _place()
        for cp in make(refs[:nb], c, sib, send, recv):
            cp.start()
        token[...] = jnp.zeros_like(token)

    res = pl.pallas_call(
        body, name=name,
        in_specs=[_IN_HBM] * nb,
        out_specs=[_SEM, _SEM] + [_IN_HBM] * nb + [pl.BlockSpec(memory_space=pltpu.VMEM)],
        out_shape=[pltpu.SemaphoreType.DMA((n_copies,)), pltpu.SemaphoreType.DMA((n_copies,))]
        + [pltpu.HBM(b.shape, b.dtype) for b in bufs] + [jax.ShapeDtypeStruct((SUBLANES, 128), F32)],
        input_output_aliases={i: 2 + i for i in range(nb)},
        compiler_params=_SPLIT_COPY_PARAMS,
    )(*[_in_hbm(b) for b in bufs])
    return res[0], res[1], list(res[2:2 + nb]), res[-1]


def _sibling_wait(name, bufs, send, recv, make, afters):
    nb = len(bufs)

    def body(*refs):
        x, y, c, k, chips, sib = _place()
        for cp in make(refs[:nb], c, sib, refs[nb], refs[nb + 1]):
            cp.wait_send()
            cp.wait_recv()

    return list(pl.pallas_call(
        body, name=name,
        in_specs=[_IN_HBM] * nb + [_SEM, _SEM] + [_HBM] * len(afters),
        out_specs=[_IN_HBM] * nb,
        out_shape=[pltpu.HBM(b.shape, b.dtype) for b in bufs],
        input_output_aliases={i: i for i in range(nb)},
        compiler_params=_SPLIT_COPY_PARAMS,
    )(*bufs, send, recv, *afters))


def _swap_copies(parts):
    n = len(parts)

    def make(refs, c, sib, send, recv):
        return [pltpu.make_async_remote_copy(
            src_ref=refs[i].at[:, _half(parts[i].shape[1], 1 - c)], dst_ref=refs[n + i],
            send_sem=send.at[i], recv_sem=recv.at[i], device_id=sib, device_id_type=MESH) for i in range(n)]

    return make


def _join_copies(fulls):
    def make(refs, c, sib, send, recv):
        cps = []
        for i, f in enumerate(fulls):
            blk = refs[i].at[_half(f.shape[0], c)]
            cps.append(pltpu.make_async_remote_copy(
                src_ref=blk, dst_ref=blk, send_sem=send.at[i], recv_sem=recv.at[i],
                device_id=sib, device_id_type=MESH))
        return cps

    return make


def _scatter_copy(src_ref, land_ref, i, j, chip_xy, c, send, recv):
    return pltpu.make_async_remote_copy(
        src_ref=src_ref.at[2 * chip_xy[0] + chip_xy[1]], dst_ref=land_ref.at[j],
        send_sem=send.at[3 * i + j], recv_sem=recv.at[3 * i + j], device_id=(*chip_xy, c), device_id_type=MESH)


def _scatter_start(name, sums):
    n = len(sums)
    lands = [lax.empty((3,) + s.shape[1:], s.dtype) for s in sums]

    def body(*refs):
        srcs, lnds = refs[:n], refs[n:2 * n]
        send, recv = refs[2 * n], refs[2 * n + 1]
        token = refs[-1]
        x, y, c, k, chips, sib = _place()
        for i in range(n):
            for j, chip_xy in enumerate(chips):
                _scatter_copy(srcs[i], lnds[i], i, j, chip_xy, c, send, recv).start()
        token[...] = jnp.zeros_like(token)

    res = pl.pallas_call(
        body, name=name,
        in_specs=[_IN_HBM] * (2 * n),
        out_specs=[_SEM, _SEM] + [_IN_HBM] * (2 * n) + [pl.BlockSpec(memory_space=pltpu.VMEM)],
        out_shape=[pltpu.SemaphoreType.DMA((3 * n,)), pltpu.SemaphoreType.DMA((3 * n,))]
        + [pltpu.HBM(a.shape, a.dtype) for a in list(sums) + lands] + [jax.ShapeDtypeStruct((SUBLANES, 128), F32)],
        input_output_aliases={i: 2 + i for i in range(2 * n)},
        compiler_params=_SPLIT_COPY_PARAMS,
    )(*[_in_hbm(a) for a in list(sums) + lands])
    return res[0], res[1], list(res[2:2 + n]), list(res[2 + n:2 + 2 * n]), res[-1]


def _scatter_wait(name, sums, lands, send, recv, afters):
    n = len(sums)

    def body(*refs):
        srcs, lnds = refs[:n], refs[n:2 * n]
        send_ref, recv_ref = refs[2 * n], refs[2 * n + 1]
        x, y, c, k, chips, sib = _place()
        for i in range(n):
            for j, chip_xy in enumerate(chips):
                cp = _scatter_copy(srcs[i], lnds[i], i, j, chip_xy, c, send_ref, recv_ref)
                cp.wait_send()
                cp.wait_recv()

    res = pl.pallas_call(
        body, name=name,
        in_specs=[_IN_HBM] * (2 * n) + [_SEM, _SEM] + [_HBM] * len(afters),
        out_specs=[_IN_HBM] * (2 * n),
        out_shape=[pltpu.HBM(a.shape, a.dtype) for a in list(sums) + list(lands)],
        input_output_aliases={i: i for i in range(2 * n)},
        compiler_params=_SPLIT_COPY_PARAMS,
    )(*sums, *lands, send, recv, *afters)
    return list(res[:n]), list(res[n:])


def _sum_over_devices(name, buf, loss_row, afters):
    R, D = buf.shape
    n_after = len(afters)

    def body(x_ref, *rest):
        all_ref, tot_ref, loss_ref, send_sems, recv_sems, local_sem = rest[n_after:]
        x, y, c, k, chips, sib = _place()
        me = (x, y, c)

        def block(px, py, pc):
            return all_ref.at[4 * px + 2 * py + pc]

        def copy(kk, blk, to, src=None):
            return pltpu.make_async_remote_copy(
                src_ref=block(*blk) if src is None else src, dst_ref=block(*blk),
                send_sem=send_sems.at[kk], recv_sem=recv_sems.at[kk], device_id=to, device_id_type=MESH)

        mine = pltpu.make_async_copy(x_ref, block(*me), local_sem)
        mine.start()
        first = [copy(0, me, sib, src=x_ref)]
        first += [copy(1 + j, me, (*chip, c), src=x_ref) for j, chip in enumerate(chips)]
        for cp in first:
            cp.start()
        passed = [copy(4 + j, (*chip, c), sib) for j, chip in enumerate(chips)]
        for j, chip in enumerate(chips):
            copy(1 + j, (*chip, c), me).wait_recv()
            passed[j].start()
        copy(0, sib, me).wait_recv()
        for j, chip in enumerate(chips):
            copy(4 + j, (*chip, 1 - c), me).wait_recv()
        for cp in first + passed:
            cp.wait_send()
        mine.wait()
        rc = _tile(R, 32)
        for r0 in range(0, R, rc):
            tot = all_ref[0, r0:r0 + rc, :]
            for d in range(1, N_DEVICES):
                tot = tot + all_ref[d, r0:r0 + rc, :]
            tot_ref[r0:r0 + rc, :] = tot
        loss = 0.5 * jnp.sum(tot_ref[loss_row:loss_row + 1, :]) / D
        loss_ref[...] = jnp.full(loss_ref.shape, loss, F32)

    vm = pl.BlockSpec(memory_space=pltpu.VMEM)
    return pl.pallas_call(
        body, name=name, in_specs=[vm] + [_HBM] * n_after, out_specs=[vm, vm, vm],
        out_shape=[jax.ShapeDtypeStruct((N_DEVICES, R, D), F32), jax.ShapeDtypeStruct((R, D), F32),
                   jax.ShapeDtypeStruct((SUBLANES, 128), F32)],
        scratch_shapes=[pltpu.SemaphoreType.DMA((7,)), pltpu.SemaphoreType.DMA((7,)), pltpu.SemaphoreType.DMA],
        compiler_params=pltpu.CompilerParams(vmem_limit_bytes=V7X_VMEM_LIMIT_BYTES),
    )(buf, *afters)[1:]


def _add_my_half(name, part, got, pos):
    S, R, C = part.shape
    R2 = R // 2
    tr = _tile(R2, 512)
    q = R2 // tr

    def body(x_ref, y_ref, c_ref, p_ref, g_ref, o_ref):
        o_ref[...] = (p_ref[...].astype(F32) + g_ref[...].astype(F32)).astype(o_ref.dtype)

    return pl.pallas_call(
        body, name=name,
        grid_spec=pltpu.PrefetchScalarGridSpec(
            num_scalar_prefetch=3, grid=(S, q),
            in_specs=[pl.BlockSpec((None, tr, C), lambda s, r, xr, yr, cr: (s, cr[0] * q + r, 0)),
                      pl.BlockSpec((None, tr, C), lambda s, r, xr, yr, cr: (s, r, 0))],
            out_specs=pl.BlockSpec((None, tr, C), lambda s, r, xr, yr, cr: (s, r, 0))),
        out_shape=jax.ShapeDtypeStruct((S, R2, C), BF16),
        compiler_params=_cparams("parallel", "parallel"),
    )(*pos, part, got)


def _add_owner(name, sums, got, pos):
    _, R2, C = sums.shape
    tr = _tile(R2, 512)
    q = R2 // tr

    def body(x_ref, y_ref, c_ref, s_ref, g_ref, o_ref):
        acc = s_ref[...].astype(F32)
        for j in range(3):
            acc = acc + g_ref[j].astype(F32)
        o_ref[...] = acc

    return pl.pallas_call(
        body, name=name,
        grid_spec=pltpu.PrefetchScalarGridSpec(
            num_scalar_prefetch=3, grid=(q,),
            in_specs=[pl.BlockSpec((None, tr, C), lambda r, xr, yr, cr: (2 * xr[0] + yr[0], r, 0)),
                      pl.BlockSpec((3, tr, C), lambda r, xr, yr, cr: (0, r, 0))],
            out_specs=pl.BlockSpec((tr, C), lambda r, xr, yr, cr: (cr[0] * q + r, 0))),
        out_shape=jax.ShapeDtypeStruct((2 * R2, C), F32),
        compiler_params=_cparams("parallel"),
    )(*pos, sums, got)


def _adamw(name, w, m, v, g):
    R, C = w.shape
    tr = SUBLANES
    while 2 * tr * C <= ADAMW_TILE_ELEMS:
        tr *= 2
    tr = _tile(R, tr)
    bc1 = 1.0 - ADAM_B1 ** ADAM_STEP
    bc2 = 1.0 - ADAM_B2 ** ADAM_STEP

    def body(w_ref, m_ref, v_ref, g_ref, go_ref, d_ref, mo_ref, vo_ref):
        gg = g_ref[...]
        m2 = ADAM_B1 * m_ref[...] + (1.0 - ADAM_B1) * gg
        v2 = ADAM_B2 * v_ref[...] + (1.0 - ADAM_B2) * (gg * gg)
        go_ref[...] = gg
        mo_ref[...] = m2
        vo_ref[...] = v2
        d_ref[...] = -ADAM_LR * ((m2 / bc1) / (jnp.sqrt(v2 / bc2) + ADAM_EPS) + ADAM_WD * w_ref[...])

    spec = pl.BlockSpec((tr, C), lambda r: (r, 0))
    return pl.pallas_call(
        body, name=name, grid=(R // tr,), in_specs=[spec] * 4, out_specs=[spec] * 4,
        out_shape=[jax.ShapeDtypeStruct((R, C), F32)] * 4,
        compiler_params=_cparams("parallel"),
    )(w, m, v, g)


def _adamw_slab(name, w, m, v, g, layer, prev):
    L, R, C = w.shape
    tr = SUBLANES
    while 2 * tr * C <= ADAMW_TILE_ELEMS:
        tr *= 2
    tr = _tile(R, tr)
    bc1 = 1.0 - ADAM_B1 ** ADAM_STEP
    bc2 = 1.0 - ADAM_B2 ** ADAM_STEP

    def body(w_ref, m_ref, v_ref, g_ref, *rest):
        go_ref, d_ref, mo_ref, vo_ref = rest[-4:]
        gg = g_ref[...]
        m2 = ADAM_B1 * m_ref[...] + (1.0 - ADAM_B1) * gg
        v2 = ADAM_B2 * v_ref[...] + (1.0 - ADAM_B2) * (gg * gg)
        go_ref[...] = gg
        mo_ref[...] = m2
        vo_ref[...] = v2
        d_ref[...] = -ADAM_LR * ((m2 / bc1) / (jnp.sqrt(v2 / bc2) + ADAM_EPS) + ADAM_WD * w_ref[...])

    slab = pl.BlockSpec((None, tr, C), lambda r: (layer, r, 0))
    n_prev = 0 if prev is None else 4
    return pl.pallas_call(
        body, name=name, grid=(R // tr,),
        in_specs=[slab] * 3 + [pl.BlockSpec((tr, C), lambda r: (r, 0))] + [_HBM] * n_prev,
        out_specs=[slab] * 4,
        out_shape=[jax.ShapeDtypeStruct((L, R, C), F32)] * 4,
        input_output_aliases={4 + i: i for i in range(n_prev)},
        compiler_params=_cparams("parallel"),
    )(w, m, v, g, *(prev or ()))


def _swap_begin(tag, parts):
    lands = [lax.empty((p.shape[0], p.shape[1] // 2, p.shape[2]), p.dtype) for p in parts]
    return _sibling_start(f"rs_swap_start_{tag}", list(parts) + lands, len(parts), _swap_copies(parts))


def _reduce_begin(tag, early, swapping, late, pos, after):
    send, recv, bufs, _ = swapping
    bufs = _sibling_wait(f"rs_swap_wait_{tag}", bufs, send, recv, _swap_copies(early), [after])
    parts = bufs[:len(early)] + list(late)
    got = bufs[len(early):] + list(_swap_halves(f"rs_swap_{tag}", late))
    sums = [_add_my_half(f"rs_add2_{tag}_{i}", p, g, pos) for i, (p, g) in enumerate(zip(parts, got))]
    return _scatter_start(f"rs_scatter_start_{tag}", sums)


def _reduce_middle(tag, started, pos, afters):
    send, recv, sums, lands, _ = started
    sums, lands = _scatter_wait(f"rs_scatter_wait_{tag}", sums, lands, send, recv, afters)
    fulls = [_add_owner(f"rs_add4_{tag}_{i}", s, q, pos) for i, (s, q) in enumerate(zip(sums, lands))]
    return _sibling_start(f"rs_join_start_{tag}", fulls, len(fulls), _join_copies(fulls))


def _reduce_end(tag, joining, afters):
    send, recv, fulls, _ = joining
    return _sibling_wait(f"rs_join_wait_{tag}", fulls, send, recv, _join_copies(fulls), afters)


def _pad_rows(a):
    r = (-a.shape[0]) % SUBLANES
    return jnp.pad(a, ((0, r), (0, 0))) if r else a


def kernel(x, p, norm_mix, norm_mlp, norm_ple, cf_w_pw1, cf_b_pw1, cf_w_dw, cf_b_dw, cf_norm, cf_w_pw2, cf_b_pw2, sc_w_in, sc_w_conv, sc_w_out, mlp_w1, mlp_w2, ple_w_proj, ple_w_gate, norm_final, loss_target, m_norm_mix, m_norm_mlp, m_norm_ple, m_cf_w_pw1, m_cf_b_pw1, m_cf_w_dw, m_cf_b_dw, m_cf_norm, m_cf_w_pw2, m_cf_b_pw2, m_sc_w_in, m_sc_w_conv, m_sc_w_out, m_mlp_w1, m_mlp_w2, m_ple_w_proj, m_ple_w_gate, m_norm_final, v_norm_mix, v_norm_mlp, v_norm_ple, v_cf_w_pw1, v_cf_b_pw1, v_cf_w_dw, v_cf_b_dw, v_cf_norm, v_cf_w_pw2, v_cf_b_pw2, v_sc_w_in, v_sc_w_conv, v_sc_w_out, v_mlp_w1, v_mlp_w2, v_ple_w_proj, v_ple_w_gate, v_norm_final):
    T, D = x.shape[1], x.shape[2]
    KA, KB = cf_w_dw.shape[1], sc_w_conv.shape[1]
    chip = (2 * lax.axis_index("x") + lax.axis_index("y")).astype(jnp.int32)
    pos = tuple(lax.axis_index(ax).astype(jnp.int32).reshape(1) for ax in ("x", "y", "c"))

    params = dict(norm_mix=norm_mix, norm_mlp=norm_mlp, norm_ple=norm_ple, cf_w_pw1=cf_w_pw1, cf_b_pw1=cf_b_pw1,
                  cf_w_dw=cf_w_dw, cf_b_dw=cf_b_dw, cf_norm=cf_norm, cf_w_pw2=cf_w_pw2, cf_b_pw2=cf_b_pw2,
                  sc_w_in=sc_w_in, sc_w_conv=sc_w_conv, sc_w_out=sc_w_out, mlp_w1=mlp_w1, mlp_w2=mlp_w2,
                  ple_w_proj=ple_w_proj, ple_w_gate=ple_w_gate, norm_final=norm_final)
    mom1 = dict(norm_mix=m_norm_mix, norm_mlp=m_norm_mlp, norm_ple=m_norm_ple, cf_w_pw1=m_cf_w_pw1,
                cf_b_pw1=m_cf_b_pw1, cf_w_dw=m_cf_w_dw, cf_b_dw=m_cf_b_dw, cf_norm=m_cf_norm, cf_w_pw2=m_cf_w_pw2,
                cf_b_pw2=m_cf_b_pw2, sc_w_in=m_sc_w_in, sc_w_conv=m_sc_w_conv, sc_w_out=m_sc_w_out,
                mlp_w1=m_mlp_w1, mlp_w2=m_mlp_w2, ple_w_proj=m_ple_w_proj, ple_w_gate=m_ple_w_gate,
                norm_final=m_norm_final)
    mom2 = dict(norm_mix=v_norm_mix, norm_mlp=v_norm_mlp, norm_ple=v_norm_ple, cf_w_pw1=v_cf_w_pw1,
                cf_b_pw1=v_cf_b_pw1, cf_w_dw=v_cf_w_dw, cf_b_dw=v_cf_b_dw, cf_norm=v_cf_norm, cf_w_pw2=v_cf_w_pw2,
                cf_b_pw2=v_cf_b_pw2, sc_w_in=v_sc_w_in, sc_w_conv=v_sc_w_conv, sc_w_out=v_sc_w_out,
                mlp_w1=v_mlp_w1, mlp_w2=v_mlp_w2, ple_w_proj=v_ple_w_proj, ple_w_gate=v_ple_w_gate,
                norm_final=v_norm_final)

    big = ("cf_w_pw1", "cf_w_pw2", "sc_w_in", "sc_w_out", "mlp_w1", "mlp_w2", "ple_w_proj", "ple_w_gate")
    row_sharded = ("cf_w_pw2", "sc_w_out", "mlp_w2", "ple_w_gate")

    def layer_names(i):
        return (["cf_w_pw1", "cf_w_pw2"] if i % 2 == 0 else ["sc_w_in", "sc_w_out"]) + \
            ["mlp_w1", "mlp_w2", "ple_w_proj", "ple_w_gate"]

    def layer_index(i, name):
        return i if name.startswith(("mlp", "ple")) else i // 2

    def gather_begin(tag, i, names, after):
        bufs = [_cast_place(f"place_{nm}_{i}", params[nm], layer_index(i, nm), pos) for nm in names]
        return names, _gather_start(f"gather_start_{tag}", bufs, after)

    def gather_end(tag, begun, after):
        names, (send, recv, bufs, _) = begun
        bufs = _gather_wait(f"gather_wait_{tag}", bufs, send, recv, after)
        bufs = _gather_forward(f"gather_fwd_{tag}", bufs)
        return {nm: g4.reshape(1, N_SHARDS * g4.shape[1], g4.shape[2]) if nm in row_sharded else g4
                for nm, g4 in zip(names, bufs)}

    conv_small = jnp.concatenate([_pad_rows(cf_w_dw[j]) for j in range(cf_w_dw.shape[0])]
                                 + [_pad_rows(sc_w_conv[j]) for j in range(sc_w_conv.shape[0])], axis=0)
    conv_shards = _gather_shards("gather_conv_w", [(conv_small, None)])[0]
    conv_all = jnp.transpose(conv_shards, (1, 0, 2)).reshape(conv_small.shape[0], D)
    ka_pad = KA + (-KA) % SUBLANES
    kb_pad = KB + (-KB) % SUBLANES
    w_dw_full = [conv_all[j * ka_pad:j * ka_pad + KA] for j in range(cf_w_dw.shape[0])]
    off = cf_w_dw.shape[0] * ka_pad
    w_conv_full = [conv_all[off + j * kb_pad:off + j * kb_pad + KB] for j in range(sc_w_conv.shape[0])]

    def vec(a):
        return a.reshape(1, -1)

    ident = lambda acc: (acc,)

    h = x[0]
    saved = []
    first = gather_begin("0m", 0, layer_names(0)[:2], conv_shards)
    rest = gather_begin("0r", 0, layer_names(0)[2:], first[1][2][0])
    W = [gather_end("0m", first, h)]
    for i in range(DEPTH):
        j = i // 2
        wl = W[i]
        s = dict(h=h)
        g_mix = vec(norm_mix[i])
        if i + 1 < DEPTH:
            nxt = gather_begin(f"{i + 1}", i + 1, layer_names(i + 1), wl[layer_names(i)[0]])
            g_mix = g_mix + nxt[1][3][0, 0]
        s["u"] = _rms_fwd(f"rms_mix_{i}", h, g_mix)
        if i % 2 == 0:
            s["a"] = _mm_nn(f"cf_pw1_{i}", s["u"], wl["cf_w_pw1"], lambda acc, b: (acc + b,), [BF16],
                            extras=[(vec(cf_b_pw1[j]), "n")])[0]
            s["v2"], s["v4"] = _cf_conv_fwd(f"cf_conv_{i}", s["a"], w_dw_full[j], vec(cf_b_dw[j]), vec(cf_norm[j]))
            h1 = _mm_nn(f"cf_pw2_{i}", s["v4"], wl["cf_w_pw2"], lambda acc, b, r: (r + (acc + b),), [F32],
                        extras=[(vec(cf_b_pw2[j]), "n"), (h, "mn")])[0]
        else:
            s["bcv"] = _mm_nn(f"sc_in_{i}", s["u"], wl["sc_w_in"], ident, [BF16])[0]
            s["y"] = _sc_conv_fwd(f"sc_conv_{i}", s["bcv"], w_conv_full[j])[0]
            h1 = _mm_nn(f"sc_out_{i}", s["y"], wl["sc_w_out"], lambda acc, r: (r + acc,), [F32],
                        extras=[(h, "mn")])[0]
        s["h1"] = h1
        if i == 0:
            wl.update(gather_end("0r", rest, h1))
        s["u2"] = _rms_fwd(f"rms_mlp_{i}", h1, vec(norm_mlp[i]))
        s["z"], s["hd"] = _mm_nn(f"mlp_w1_{i}", s["u2"], wl["mlp_w1"],
                                 lambda acc: (acc, jnp.square(jnp.maximum(acc, 0.0))), [BF16, BF16])
        h2 = _mm_nn(f"mlp_w2_{i}", s["hd"], wl["mlp_w2"], lambda acc, r: (r + acc,), [F32], extras=[(h1, "mn")])[0]
        s["h2"] = h2
        s["n3"] = _rms_fwd(f"rms_ple_{i}", h2, vec(norm_ple[i]))
        s["p"] = p[i, 0]
        s["e"] = _mm_nn(f"ple_proj_{i}", s["p"], wl["ple_w_proj"], ident, [BF16])[0]
        h, s["q"] = _mm_nn(f"ple_gate_{i}", s["n3"], wl["ple_w_gate"],
                           lambda acc, r, e: (r + _sigmoid(acc) * e.astype(F32), acc), [F32, BF16],
                           extras=[(h2, "mn"), (s["e"], "mn")])
        saved.append(s)
        if i + 1 < DEPTH:
            W.append(gather_end(f"{i + 1}", nxt, h))

    dh, dh16, dg_final, loss_cols = _loss_bwd("loss_bwd", h, vec(norm_final), loss_target[0])
    small = {"norm_final": dg_final, "loss": loss_cols}
    adam = {nm: None for nm in big}

    def reduce_names(i):
        names = layer_names(i)
        return names[2:] + names[:2]

    def update_layer(i, joining, afters):
        for nm, g in zip(reduce_names(i), _reduce_end(f"{i}", joining, afters)):
            l = layer_index(i, nm)
            adam[nm] = _adamw_slab(f"adamw_{nm}_{l}", params[nm], mom1[nm], mom2[nm], g, l, adam[nm])

    def shard_major(names, parts):
        return [pt.reshape(N_SHARDS, pt.shape[1] // N_SHARDS, pt.shape[2]) if nm in row_sharded else pt
                for nm, pt in zip(names, parts)]

    scattered = joining = None
    for i in reversed(range(DEPTH)):
        j = i // 2
        wl, s = W[i], saved[i]
        dq, de = _ple_elem_bwd(f"ple_elem_bwd_{i}", dh, s["q"], s["e"])
        d_proj = _mm_tn(f"ple_proj_dw_{i}", s["p"], de, N_SHARDS)
        d_gate = _mm_tn(f"ple_gate_dw_{i}", s["n3"], dq, 1)
        dn3 = _mm_nt(f"ple_gate_dx_{i}", dq, wl["ple_w_gate"], ident, [F32])[0]
        g_ple = vec(norm_ple[i])
        if joining is not None:
            g_ple = g_ple + joining[3][0, 0]
        dh, dh16, small[f"norm_ple_{i}"] = _rms_bwd(f"rms_ple_bwd_{i}", s["h2"], g_ple, dn3, dh)

        d_w2 = _mm_tn(f"mlp_w2_dw_{i}", s["hd"], dh16, 1)
        dz = _mm_nt(f"mlp_w2_dx_{i}", dh16, wl["mlp_w2"],
                    lambda acc, z: (acc * (2.0 * jnp.maximum(z.astype(F32), 0.0)),), [BF16], extras=[s["z"]])[0]
        d_w1 = _mm_tn(f"mlp_w1_dw_{i}", s["u2"], dz, N_SHARDS)
        du2 = _mm_nt(f"mlp_w1_dx_{i}", dz, wl["mlp_w1"], ident, [F32])[0]
        early = shard_major(reduce_names(i)[:4], [d_w1, d_w2, d_proj, d_gate])
        swapping = _swap_begin(f"{i}", early)
        g_mlp = vec(norm_mlp[i]) + swapping[3][0, 0]
        if i % 2 == 0:
            dh, dh16, small[f"norm_mlp_{i}"], small[f"cf_b_pw2_{j}"] = _rms_bwd(
                f"rms_mlp_bwd_{i}", s["h1"], g_mlp, du2, dh, want_colsum=True)
            d_mix_out = _mm_tn(f"cf_pw2_dw_{i}", s["v4"], dh16, 1)
            dv4 = _mm_nt(f"cf_pw2_dx_{i}", dh16, wl["cf_w_pw2"], ident, [F32])[0]
            dv2, small[f"cf_norm_{j}"], small[f"cf_b_dw_{j}"] = _cf_norm_bwd(
                f"cf_norm_bwd_{i}", s["v2"], vec(cf_norm[j]), dv4)
            da, small[f"cf_w_dw_{j}"], db1 = _cf_conv_bwd(f"cf_conv_bwd_{i}", dv2, s["a"], w_dw_full[j])
            small[f"cf_b_pw1_{j}"] = db1.reshape(2, D)
            d_mix_in = _mm_tn(f"cf_pw1_dw_{i}", s["u"], da, N_SHARDS)
            du = _mm_nt(f"cf_pw1_dx_{i}", da, wl["cf_w_pw1"], ident, [F32])[0]
        else:
            dh, dh16, small[f"norm_mlp_{i}"] = _rms_bwd(f"rms_mlp_bwd_{i}", s["h1"], g_mlp, du2, dh)
            d_mix_out = _mm_tn(f"sc_out_dw_{i}", s["y"], dh16, 1)
            dy = _mm_nt(f"sc_out_dx_{i}", dh16, wl["sc_w_out"], ident, [F32])[0]
            da, small[f"sc_w_conv_{j}"] = _sc_conv_bwd(f"sc_conv_bwd_{i}", dy, s["bcv"], w_conv_full[j])
            d_mix_in = _mm_tn(f"sc_in_dw_{i}", s["u"], da, N_SHARDS)
            du = _mm_nt(f"sc_in_dx_{i}", da, wl["sc_w_in"], ident, [F32])[0]

        late = shard_major(reduce_names(i)[4:], [d_mix_in, d_mix_out])
        started = _reduce_begin(f"{i}", early, swapping, late, pos, du)
        token = started[4]
        dh, dh16, small[f"norm_mix_{i}"] = _rms_bwd(f"rms_mix_bwd_{i}", s["h"], vec(norm_mix[i]) + token[0, 0], du, dh)
        if joining is not None:
            update_layer(i + 2, joining, [token])
            joining = None
        if scattered is not None:
            joining = _reduce_middle(f"{i + 1}", scattered, pos, [token])
        scattered = started
    grad_x = dh.reshape(x.shape)

    order = sorted(small)
    pieces, where, row = [], {}, 0
    for nm in order:
        pc = _pad_rows(small[nm])
        where[nm] = (row, small[nm].shape[0])
        row += pc.shape[0]
        pieces.append(pc)
    update_layer(1, joining, [token])
    updated = [res[3] for res in adam.values() if res is not None]
    total, loss_tile = _sum_over_devices("small_allsum", jnp.concatenate(pieces, axis=0), where["loss"][0], updated)
    loss = loss_tile[0, 0]
    update_layer(0, _reduce_middle("0", scattered, pos, [total] + updated), [])

    def small_sum(nm):
        r0, nr = where[nm]
        return total[r0:r0 + nr]

    def my_cols(a):
        return lax.dynamic_slice_in_dim(a, chip * (D // N_SHARDS), D // N_SHARDS, axis=1)

    g_small = {
        "norm_mix": jnp.concatenate([small_sum(f"norm_mix_{i}") for i in range(DEPTH)], axis=0),
        "norm_mlp": jnp.concatenate([small_sum(f"norm_mlp_{i}") for i in range(DEPTH)], axis=0),
        "norm_ple": jnp.concatenate([small_sum(f"norm_ple_{i}") for i in range(DEPTH)], axis=0),
        "cf_b_pw1": jnp.stack([small_sum(f"cf_b_pw1_{j}").reshape(2 * D) for j in range(DEPTH // 2)]),
        "cf_w_dw": jnp.stack([my_cols(small_sum(f"cf_w_dw_{j}")) for j in range(DEPTH // 2)]),
        "cf_b_dw": jnp.concatenate([small_sum(f"cf_b_dw_{j}") for j in range(DEPTH // 2)], axis=0),
        "cf_norm": jnp.concatenate([small_sum(f"cf_norm_{j}") for j in range(DEPTH // 2)], axis=0),
        "cf_b_pw2": jnp.concatenate([small_sum(f"cf_b_pw2_{j}") for j in range(DEPTH // 2)], axis=0),
        "sc_w_conv": jnp.stack([my_cols(small_sum(f"sc_w_conv_{j}")) for j in range(DEPTH // 2)]),
        "norm_final": small_sum("norm_final").reshape(D),
    }

    names_out = ["norm_mix", "norm_mlp", "norm_ple", "cf_w_pw1", "cf_b_pw1", "cf_w_dw", "cf_b_dw", "cf_norm",
                 "cf_w_pw2", "cf_b_pw2", "sc_w_in", "sc_w_conv", "sc_w_out", "mlp_w1", "mlp_w2", "ple_w_proj",
                 "ple_w_gate", "norm_final"]
    grad, delta, new_m, new_v = {}, {}, {}, {}
    for nm in names_out:
        w = params[nm]
        if nm in big:
            grad[nm], delta[nm], new_m[nm], new_v[nm] = adam[nm]
            continue
        g = g_small[nm]
        cols = w.shape[-1] if w.ndim > 1 else w.shape[0]
        two_d = lambda a: a.reshape(-1, cols)
        res = _adamw(f"adamw_{nm}", two_d(w), two_d(mom1[nm]), two_d(mom2[nm]), two_d(g))
        grad[nm], delta[nm], new_m[nm], new_v[nm] = [r.reshape(w.shape) for r in res]

    return (loss, grad_x, *[grad[n] for n in names_out], *[delta[n] for n in names_out],
            *[new_m[n] for n in names_out], *[new_v[n] for n in names_out])
```

```python
import jax
import jax.numpy as jnp
from jax import lax
from jax.experimental import pallas as pl
from jax.experimental.pallas import tpu as pltpu

F32 = jnp.float32
BF16 = jnp.bfloat16

EPS = 1e-6
ADAM_LR = 0.001
ADAM_B1 = 0.9
ADAM_B2 = 0.999
ADAM_EPS = 1e-08
ADAM_WD = 0.01
ADAM_STEP = 10

DEPTH = 4
N_SHARDS = 4
N_DEVICES = 8
V7X_VMEM_LIMIT_BYTES = 56 * 1024 * 1024
SUBLANES = 8
MESH = pl.DeviceIdType.MESH

MM_TM = 1024
MM_TN = 1024
MM_TK = 2048
MM_TW_K = 1024
MM_TW_T = 4096
MM_TX_K = 1024
MM_TX_N = 2048
MM_TX_K_SHARDS = 512
MM_TX_K_LONG = 256
MM_TX_LONG_N = 4096
MM_TN_CROWDED = 512
MM_LONG_K = 8192
MM_TN_LONG_K = 256
ADAMW_TILE_ELEMS = 256 * 2048

CONV_ROW_CHUNK = 32
CONV_LANE_CHUNK = 512
CONV_TILE_ROWS = 128
CONV_FWD_TILE_ROWS = 256
ROW_TILE = 256
LIGHT_ROW_TILE = 512


def _tile(dim, pref):
    if dim <= pref:
        return dim
    t = pref
    while dim % t:
        t //= 2
    return t


def _cparams(*sem):
    return pltpu.CompilerParams(dimension_semantics=sem, vmem_limit_bytes=V7X_VMEM_LIMIT_BYTES)


def _sigmoid(x):
    return 0.5 * (jnp.tanh(0.5 * x) + 1.0)


def _rms_r(x):
    return lax.rsqrt(jnp.mean(x * x, axis=-1, keepdims=True) + EPS)


def _mm_nn(name, a, b3, epilogue, out_dtypes, extras=()):
    M, K = a.shape
    S, Kb, Ns = b3.shape
    assert Kb == K
    N = S * Ns
    crowded = sum(kind == "mn" for _, kind in extras) > 1
    tm, tn, tk = _tile(M, MM_TM), _tile(Ns, MM_TN_CROWDED if crowded else MM_TN), _tile(K, MM_TK)
    if K >= MM_LONG_K:
        tn, tk = _tile(Ns, MM_TN_LONG_K), K
    per = Ns // tn
    nk = K // tk
    in_specs = [pl.BlockSpec((tm, tk), lambda i, j, k: (i, k)),
                pl.BlockSpec((None, tk, tn), lambda i, j, k: (j // per, k, j % per))]
    for _, kind in extras:
        if kind == "mn":
            in_specs.append(pl.BlockSpec((tm, tn), lambda i, j, k: (i, j)))
        else:
            in_specs.append(pl.BlockSpec((1, tn), lambda i, j, k: (0, j)))
    n_ex, n_o = len(extras), len(out_dtypes)

    def body(*refs):
        a_ref, b_ref = refs[:2]
        ex = refs[2:2 + n_ex]
        outs = refs[2 + n_ex:2 + n_ex + n_o]
        part = jnp.dot(a_ref[...].astype(BF16), b_ref[...], preferred_element_type=F32)

        def finish(acc):
            res = epilogue(acc, *[e[...] for e in ex])
            for r, o in zip(res, outs):
                o[...] = r.astype(o.dtype)

        if nk == 1:
            finish(part)
        else:
            acc_ref = refs[-1]
            k = pl.program_id(2)

            @pl.when(k == 0)
            def _():
                acc_ref[...] = part

            @pl.when(k > 0)
            def _():
                acc_ref[...] += part

            @pl.when(k == nk - 1)
            def _():
                finish(acc_ref[...])

    res = pl.pallas_call(
        body, name=name, grid=(M // tm, N // tn, nk),
        in_specs=in_specs,
        out_specs=[pl.BlockSpec((tm, tn), lambda i, j, k: (i, j)) for _ in out_dtypes],
        out_shape=[jax.ShapeDtypeStruct((M, N), dt) for dt in out_dtypes],
        scratch_shapes=[pltpu.VMEM((tm, tn), F32)] if nk > 1 else [],
        compiler_params=_cparams("parallel", "parallel", "arbitrary"),
    )(a, b3, *[e for e, _ in extras])
    return res


def _mm_nt_shards(name, g, w3, epilogue, out_dtypes, extras):
    M, N = g.shape
    S, K, Ns = w3.shape
    tm = _tile(M, MM_TM)
    tkk = _tile(K, MM_TX_K_LONG if N > MM_TX_LONG_N else MM_TX_K_SHARDS)
    n_ex, n_o = len(extras), len(out_dtypes)

    def body(*refs):
        g_ref = refs[0]
        w_refs = refs[1:1 + S]
        ex = refs[1 + S:1 + S + n_ex]
        outs = refs[1 + S + n_ex:1 + S + n_ex + n_o]
        acc = None
        for s in range(S):
            part = lax.dot_general(g_ref[:, s * Ns:(s + 1) * Ns].astype(BF16), w_refs[s][...],
                                   (((1,), (1,)), ((), ())), preferred_element_type=F32)
            acc = part if acc is None else acc + part
        for r, o in zip(epilogue(acc, *[e[...] for e in ex]), outs):
            o[...] = r.astype(o.dtype)

    return pl.pallas_call(
        body, name=name, grid=(M // tm, K // tkk),
        in_specs=[pl.BlockSpec((tm, N), lambda i, kk: (i, 0))]
        + [pl.BlockSpec((None, tkk, Ns), lambda i, kk, s=s: (s, kk, 0)) for s in range(S)]
        + [pl.BlockSpec((tm, tkk), lambda i, kk: (i, kk)) for _ in extras],
        out_specs=[pl.BlockSpec((tm, tkk), lambda i, kk: (i, kk)) for _ in out_dtypes],
        out_shape=[jax.ShapeDtypeStruct((M, K), dt) for dt in out_dtypes],
        compiler_params=_cparams("parallel", "parallel"),
    )(g, *([w3] * S), *extras)


def _mm_nt(name, g, w3, epilogue, out_dtypes, extras=()):
    M, N = g.shape
    S, K, Ns = w3.shape
    assert S * Ns == N
    if S > 1:
        return _mm_nt_shards(name, g, w3, epilogue, out_dtypes, extras)
    tm, tn, tkk = _tile(M, MM_TM), _tile(Ns, MM_TX_N), _tile(K, MM_TX_K)
    per = Ns // tn
    nn = N // tn
    n_ex, n_o = len(extras), len(out_dtypes)

    def body(*refs):
        g_ref, w_ref = refs[:2]
        ex = refs[2:2 + n_ex]
        outs = refs[2 + n_ex:2 + n_ex + n_o]
        part = lax.dot_general(g_ref[...].astype(BF16), w_ref[...], (((1,), (1,)), ((), ())),
                               preferred_element_type=F32)

        def finish(acc):
            res = epilogue(acc, *[e[...] for e in ex])
            for r, o in zip(res, outs):
                o[...] = r.astype(o.dtype)

        if nn == 1:
            finish(part)
        else:
            acc_ref = refs[-1]
            n = pl.program_id(2)

            @pl.when(n == 0)
            def _():
                acc_ref[...] = part

            @pl.when(n > 0)
            def _():
                acc_ref[...] += part

            @pl.when(n == nn - 1)
            def _():
                finish(acc_ref[...])

    return pl.pallas_call(
        body, name=name, grid=(M // tm, K // tkk, nn),
        in_specs=[pl.BlockSpec((tm, tn), lambda i, kk, n: (i, n)),
                  pl.BlockSpec((None, tkk, tn), lambda i, kk, n: (n // per, kk, n % per))]
        + [pl.BlockSpec((tm, tkk), lambda i, kk, n: (i, kk)) for _ in extras],
        out_specs=[pl.BlockSpec((tm, tkk), lambda i, kk, n: (i, kk)) for _ in out_dtypes],
        out_shape=[jax.ShapeDtypeStruct((M, K), dt) for dt in out_dtypes],
        scratch_shapes=[pltpu.VMEM((tm, tkk), F32)] if nn > 1 else [],
        compiler_params=_cparams("parallel", "parallel", "arbitrary"),
    )(g, w3, *extras)


def _mm_tn(name, a, g, n_shards):
    T, K = a.shape
    _, N = g.shape
    Ns = N // n_shards
    tk, tn, tt = _tile(K, MM_TW_K), _tile(Ns, MM_TN), _tile(T, MM_TW_T)
    per = Ns // tn
    nt = T // tt

    def body(a_ref, g_ref, o_ref, *scratch):
        part = lax.dot_general(a_ref[...].astype(BF16), g_ref[...].astype(BF16), (((0,), (0,)), ((), ())),
                               preferred_element_type=F32)
        if nt == 1:
            o_ref[...] = part.astype(o_ref.dtype)
            return
        acc_ref, = scratch
        t = pl.program_id(2)

        @pl.when(t == 0)
        def _():
            acc_ref[...] = part

        @pl.when(t > 0)
        def _():
            acc_ref[...] += part

        @pl.when(t == nt - 1)
        def _():
            o_ref[...] = acc_ref[...].astype(o_ref.dtype)

    return pl.pallas_call(
        body, name=name, grid=(K // tk, N // tn, nt),
        in_specs=[pl.BlockSpec((tt, tk), lambda i, j, t: (t, i)),
                  pl.BlockSpec((tt, tn), lambda i, j, t: (t, j))],
        out_specs=pl.BlockSpec((None, tk, tn), lambda i, j, t: (j // per, i, j % per)),
        out_shape=jax.ShapeDtypeStruct((n_shards, K, Ns), BF16),
        scratch_shapes=[pltpu.VMEM((tk, tn), F32)] if nt > 1 else [],
        compiler_params=_cparams("parallel", "parallel", "arbitrary"),
    )(a, g)


def _rowwise(name, fn, ins, outs, accs=(), scratch=(), tt=ROW_TILE):
    T = next(a.shape[0] for a, kind in ins if kind == "row")
    tt = _tile(T, tt)
    n = T // tt
    in_specs = []
    for a, kind in ins:
        w = a.shape[1]
        if kind == "row":
            in_specs.append(pl.BlockSpec((tt, w), lambda i: (i, 0)))
        elif kind == "vec":
            in_specs.append(pl.BlockSpec(a.shape, lambda i: (0, 0)))
        elif kind[0] == "prev":
            pad = kind[1]
            in_specs.append(pl.BlockSpec((pad, w), lambda i, q=tt // pad: (jnp.maximum(i * q - 1, 0), 0)))
        else:
            pad = kind[1]
            in_specs.append(pl.BlockSpec((pad, w), lambda i, q=tt // pad, last=T // pad - 1:
                                         (jnp.minimum((i + 1) * q, last), 0)))
    n_in, n_out, n_acc = len(ins), len(outs), len(accs)

    def body(*refs):
        i = pl.program_id(0)
        in_refs = refs[:n_in]
        out_refs = refs[n_in:n_in + n_out]
        acc_refs = refs[n_in + n_out:n_in + n_out + n_acc]
        scr = refs[n_in + n_out + n_acc:]
        if n_acc:
            @pl.when(i == 0)
            def _():
                for r in acc_refs:
                    r[...] = jnp.zeros_like(r)
        fn(i, n, in_refs, out_refs, acc_refs, scr)

    res = pl.pallas_call(
        body, name=name, grid=(n,),
        in_specs=in_specs,
        out_specs=[pl.BlockSpec((tt, w), lambda i: (i, 0)) for w, _ in outs]
        + [pl.BlockSpec((r, w), lambda i: (0, 0)) for r, w in accs],
        out_shape=[jax.ShapeDtypeStruct((T, w), dt) for w, dt in outs]
        + [jax.ShapeDtypeStruct((r, w), F32) for r, w in accs],
        scratch_shapes=list(scratch),
        compiler_params=_cparams("arbitrary"),
    )(*[a for a, _ in ins])
    return res


def _colsum(x):
    return jnp.sum(x, axis=0, keepdims=True)


def _rms_fwd(name, h, g):
    D = h.shape[1]

    def fn(i, n, ins, outs, accs, scr):
        x = ins[0][...]
        outs[0][...] = (x * _rms_r(x) * ins[1][...]).astype(BF16)

    return _rowwise(name, fn, [(h, "row"), (g, "vec")], [(D, BF16)], tt=LIGHT_ROW_TILE)[0]


def _rms_bwd(name, h, g, du, dh_in, want_colsum=False):
    D = h.shape[1]

    def fn(i, n, ins, outs, accs, scr):
        x = ins[0][...]
        gg = ins[1][...]
        d = ins[2][...].astype(F32)
        r = _rms_r(x)
        xn = x * r
        t = d * gg
        dh = ins[3][...] + r * (t - xn * jnp.mean(t * xn, axis=-1, keepdims=True))
        outs[0][...] = dh
        outs[1][...] = dh.astype(BF16)
        accs[0][...] += _colsum(d * xn)
        if want_colsum:
            accs[1][...] += _colsum(dh)

    return _rowwise(name, fn, [(h, "row"), (g, "vec"), (du, "row"), (dh_in, "row")],
                    [(D, F32), (D, BF16)], accs=[(1, D)] * (2 if want_colsum else 1))


def _loss_bwd(name, h, g, tgt):
    D = h.shape[1]

    def fn(i, n, ins, outs, accs, scr):
        x = ins[0][...]
        gg = ins[1][...]
        r = _rms_r(x)
        xn = x * r
        err = xn * gg - ins[2][...]
        dy = err / D
        t = dy * gg
        dh = r * (t - xn * jnp.mean(t * xn, axis=-1, keepdims=True))
        outs[0][...] = dh
        outs[1][...] = dh.astype(BF16)
        accs[0][...] += _colsum(dy * xn)
        accs[1][...] += _colsum(err * err)

    return _rowwise(name, fn, [(h, "row"), (g, "vec"), (tgt, "row")], [(D, F32), (D, BF16)],
                    accs=[(1, D), (1, D)])


def _ple_elem_bwd(name, dh, q, e):
    D = dh.shape[1]

    def fn(i, n, ins, outs, accs, scr):
        d = ins[0][...]
        s = _sigmoid(ins[1][...].astype(F32))
        ee = ins[2][...].astype(F32)
        outs[0][...] = (d * ee * s * (1.0 - s)).astype(BF16)
        outs[1][...] = (d * s).astype(BF16)

    return _rowwise(name, fn, [(dh, "row"), (q, "row"), (e, "row")], [(D, BF16), (D, BF16)], tt=LIGHT_ROW_TILE)


def _cf_norm_bwd(name, v2, g, dv4):
    D = v2.shape[1]

    def fn(i, n, ins, outs, accs, scr):
        x = ins[0][...]
        gg = ins[1][...]
        r = _rms_r(x)
        xn = x * r
        v3 = xn * gg
        s = _sigmoid(v3)
        dv3 = ins[2][...].astype(F32) * (s * (1.0 + v3 * (1.0 - s)))
        t = dv3 * gg
        dv2 = r * (t - xn * jnp.mean(t * xn, axis=-1, keepdims=True))
        outs[0][...] = dv2
        accs[0][...] += _colsum(dv3 * xn)
        accs[1][...] += _colsum(dv2)

    return _rowwise(name, fn, [(v2, "row"), (g, "vec"), (dv4, "row")], [(D, F32)], accs=[(1, D), (1, D)])


def _chunks(tt, width):
    cc = min(CONV_LANE_CHUNK, width)
    rc = min(CONV_ROW_CHUNK, tt)
    for c0 in range(0, width, cc):
        for r0 in range(0, tt, rc):
            yield r0, rc, c0, cc


def _n_shifts(n_taps):
    return min(SUBLANES - 1, n_taps - 1)


def _shifted_scratch(n_taps, rows, width):
    return pltpu.VMEM((_n_shifts(n_taps), rows, width), F32)


def _shift_window(win_ref, sh_ref, n_taps, sign):
    rows = win_ref.shape[0] - SUBLANES
    width = win_ref.shape[1]
    cc = min(CONV_LANE_CHUNK, width)
    for b in range(1, _n_shifts(n_taps) + 1):
        off = SUBLANES - b if sign < 0 else b
        for c0 in range(0, width, cc):
            sh_ref[b - 1, 0:rows, c0:c0 + cc] = win_ref[off:off + rows, c0:c0 + cc]


def _tap(win_ref, sh_ref, base, sign, s, r0, rc, c0, cc):
    a, b = divmod(s, SUBLANES)
    if b == 0:
        row = base + r0 + sign * SUBLANES * a
        return win_ref[row:row + rc, c0:c0 + cc]
    row = base + r0 - SUBLANES * (a + 1) if sign < 0 else base + r0 + SUBLANES * a
    return sh_ref[b - 1, row:row + rc, c0:c0 + cc]


def _fir(win_ref, sh_ref, w_ref, n_taps, base, sign, tt, width, emit):
    for r0, rc, c0, cc in _chunks(tt, width):
        acc = jnp.zeros((rc, cc), F32)
        for k in range(n_taps):
            acc = acc + w_ref[k:k + 1, c0:c0 + cc] * _tap(win_ref, sh_ref, base, sign, n_taps - 1 - k, r0, rc, c0, cc)
        emit(r0, rc, c0, cc, acc)


def _fir_wgrad(d_ref, win_ref, sh_ref, dw8_ref, n_taps, pad, tt, width):
    for c0 in range(0, width, min(CONV_LANE_CHUNK, width)):
        cc = min(CONV_LANE_CHUNK, width)
        rc = min(CONV_ROW_CHUNK, tt)
        for k in range(n_taps):
            acc = jnp.zeros((SUBLANES, cc), F32)
            for r0 in range(0, tt, rc):
                prod = d_ref[r0:r0 + rc, c0:c0 + cc] * _tap(win_ref, sh_ref, pad, -1, n_taps - 1 - k, r0, rc, c0, cc)
                for q in range(0, rc, SUBLANES):
                    acc = acc + prod[q:q + SUBLANES]
            dw8_ref[SUBLANES * k:SUBLANES * (k + 1), c0:c0 + cc] += acc


def _glu(blk, D):
    return blk[:, :D].astype(F32) * _sigmoid(blk[:, D:].astype(F32))


CF_PAD = 32
SC_PAD = 16


def _cf_conv_fwd(name, a, w_dw, b_dw, g_cf):
    T, D2 = a.shape
    D = D2 // 2
    K = w_dw.shape[0]
    tt = _tile(T, CONV_FWD_TILE_ROWS)

    def fn(i, n, ins, outs, accs, scr):
        a_ref, prev_ref, w_ref, b_ref, g_ref = ins
        win_ref, v2_ref, sh_ref = scr
        win_ref[0:CF_PAD, :] = jnp.where(i > 0, _glu(prev_ref[...], D), 0.0)
        win_ref[CF_PAD:CF_PAD + tt, :] = _glu(a_ref[...], D)
        _shift_window(win_ref, sh_ref, K, -1)

        def emit(r0, rc, c0, cc, acc):
            v2_ref[r0:r0 + rc, c0:c0 + cc] = acc + b_ref[:, c0:c0 + cc]

        _fir(win_ref, sh_ref, w_ref, K, CF_PAD, -1, tt, D, emit)
        v2 = v2_ref[...]
        v3 = v2 * _rms_r(v2) * g_ref[...]
        outs[0][...] = v2
        outs[1][...] = (v3 * _sigmoid(v3)).astype(BF16)

    return _rowwise(name, fn, [(a, "row"), (a, ("prev", CF_PAD)), (w_dw, "vec"), (b_dw, "vec"), (g_cf, "vec")],
                    [(D, F32), (D, BF16)],
                    scratch=[pltpu.VMEM((CF_PAD + tt, D), F32), pltpu.VMEM((tt, D), F32),
                             _shifted_scratch(K, CF_PAD + tt, D)], tt=tt)


def _cf_conv_bwd(name, dv2, a, w_dw):
    T, D2 = a.shape
    D = D2 // 2
    K = w_dw.shape[0]
    tt = _tile(T, CONV_TILE_ROWS)

    def fn(i, n, ins, outs, accs, scr):
        d_ref, dnext_ref, a_ref, prev_ref, w_ref = ins
        v1win_ref, dwin_ref, dv1_ref, dw8_ref, v1sh_ref, dsh_ref = scr

        @pl.when(i == 0)
        def _():
            dw8_ref[...] = jnp.zeros_like(dw8_ref)

        v1win_ref[0:CF_PAD, :] = jnp.where(i > 0, _glu(prev_ref[...], D), 0.0)
        v1win_ref[CF_PAD:CF_PAD + tt, :] = _glu(a_ref[...], D)
        dwin_ref[0:tt, :] = d_ref[...]
        dwin_ref[tt:tt + CF_PAD, :] = jnp.where(i < n - 1, dnext_ref[...], 0.0)
        _shift_window(v1win_ref, v1sh_ref, K, -1)
        _shift_window(dwin_ref, dsh_ref, K, 1)

        def emit(r0, rc, c0, cc, acc):
            dv1_ref[r0:r0 + rc, c0:c0 + cc] = acc

        _fir(dwin_ref, dsh_ref, w_ref, K, 0, 1, tt, D, emit)
        _fir_wgrad(d_ref, v1win_ref, v1sh_ref, dw8_ref, K, CF_PAD, tt, D)

        blk = a_ref[...]
        val = blk[:, :D].astype(F32)
        sg = _sigmoid(blk[:, D:].astype(F32))
        dv1 = dv1_ref[...]
        dval = dv1 * sg
        dgate = dv1 * val * sg * (1.0 - sg)
        outs[0][:, :D] = dval.astype(BF16)
        outs[0][:, D:] = dgate.astype(BF16)
        accs[1][:, :D] += _colsum(dval)
        accs[1][:, D:] += _colsum(dgate)

        @pl.when(i == n - 1)
        def _():
            for k in range(K):
                accs[0][k:k + 1, :] = _colsum(dw8_ref[SUBLANES * k:SUBLANES * (k + 1), :])

    return _rowwise(name, fn, [(dv2, "row"), (dv2, ("next", CF_PAD)), (a, "row"), (a, ("prev", CF_PAD)),
                               (w_dw, "vec")],
                    [(D2, BF16)], accs=[(K, D), (1, D2)],
                    scratch=[pltpu.VMEM((CF_PAD + tt, D), F32), pltpu.VMEM((tt + CF_PAD, D), F32),
                             pltpu.VMEM((tt, D), F32), pltpu.VMEM((SUBLANES * K, D), F32),
                             _shifted_scratch(K, CF_PAD + tt, D), _shifted_scratch(K, tt + CF_PAD, D)], tt=tt)


def _sc_conv_fwd(name, bcv, w_conv):
    T, D3 = bcv.shape
    D = D3 // 3
    K = w_conv.shape[0]
    tt = _tile(T, CONV_TILE_ROWS)

    def cv_of(blk):
        return blk[:, D:2 * D].astype(F32) * blk[:, 2 * D:].astype(F32)

    def fn(i, n, ins, outs, accs, scr):
        x_ref, prev_ref, w_ref = ins
        win_ref, cc_ref, sh_ref = scr
        win_ref[0:SC_PAD, :] = jnp.where(i > 0, cv_of(prev_ref[...]), 0.0)
        win_ref[SC_PAD:SC_PAD + tt, :] = cv_of(x_ref[...])
        _shift_window(win_ref, sh_ref, K, -1)

        def emit(r0, rc, c0, cw, acc):
            cc_ref[r0:r0 + rc, c0:c0 + cw] = acc

        _fir(win_ref, sh_ref, w_ref, K, SC_PAD, -1, tt, D, emit)
        outs[0][...] = (x_ref[:, :D].astype(F32) * cc_ref[...]).astype(BF16)

    return _rowwise(name, fn, [(bcv, "row"), (bcv, ("prev", SC_PAD)), (w_conv, "vec")], [(D, BF16)],
                    scratch=[pltpu.VMEM((SC_PAD + tt, D), F32), pltpu.VMEM((tt, D), F32),
                             _shifted_scratch(K, SC_PAD + tt, D)], tt=tt)


def _sc_conv_bwd(name, dy, bcv, w_conv):
    T, D3 = bcv.shape
    D = D3 // 3
    K = w_conv.shape[0]
    tt = _tile(T, CONV_TILE_ROWS)

    def cv_of(blk):
        return blk[:, D:2 * D].astype(F32) * blk[:, 2 * D:].astype(F32)

    def fn(i, n, ins, outs, accs, scr):
        dy_ref, dynext_ref, x_ref, prev_ref, next_ref, w_ref = ins
        cvwin_ref, dccwin_ref, tmp_ref, dw8_ref, cvsh_ref, dccsh_ref = scr

        @pl.when(i == 0)
        def _():
            dw8_ref[...] = jnp.zeros_like(dw8_ref)

        cvwin_ref[0:SC_PAD, :] = jnp.where(i > 0, cv_of(prev_ref[...]), 0.0)
        cvwin_ref[SC_PAD:SC_PAD + tt, :] = cv_of(x_ref[...])
        _shift_window(cvwin_ref, cvsh_ref, K, -1)

        def emit_cc(r0, rc, c0, cw, acc):
            tmp_ref[r0:r0 + rc, c0:c0 + cw] = acc

        _fir(cvwin_ref, cvsh_ref, w_ref, K, SC_PAD, -1, tt, D, emit_cc)
        dy_v = dy_ref[...].astype(F32)
        outs[0][:, :D] = (dy_v * tmp_ref[...]).astype(BF16)
        dccwin_ref[0:tt, :] = dy_v * x_ref[:, :D].astype(F32)
        dccwin_ref[tt:tt + SC_PAD, :] = jnp.where(
            i < n - 1, dynext_ref[...].astype(F32) * next_ref[:, :D].astype(F32), 0.0)

        _shift_window(dccwin_ref, dccsh_ref, K, 1)

        def emit_dcv(r0, rc, c0, cw, acc):
            tmp_ref[r0:r0 + rc, c0:c0 + cw] = acc

        _fir(dccwin_ref, dccsh_ref, w_ref, K, 0, 1, tt, D, emit_dcv)
        _fir_wgrad(dccwin_ref, cvwin_ref, cvsh_ref, dw8_ref, K, SC_PAD, tt, D)
        dcv = tmp_ref[...]
        outs[0][:, D:2 * D] = (dcv * x_ref[:, 2 * D:].astype(F32)).astype(BF16)
        outs[0][:, 2 * D:] = (dcv * x_ref[:, D:2 * D].astype(F32)).astype(BF16)

        @pl.when(i == n - 1)
        def _():
            for k in range(K):
                accs[0][k:k + 1, :] = _colsum(dw8_ref[SUBLANES * k:SUBLANES * (k + 1), :])

    return _rowwise(name, fn, [(dy, "row"), (dy, ("next", SC_PAD)), (bcv, "row"), (bcv, ("prev", SC_PAD)),
                               (bcv, ("next", SC_PAD)), (w_conv, "vec")],
                    [(D3, BF16)], accs=[(K, D)],
                    scratch=[pltpu.VMEM((SC_PAD + tt, D), F32), pltpu.VMEM((tt + SC_PAD, D), F32),
                             pltpu.VMEM((tt, D), F32), pltpu.VMEM((SUBLANES * K, D), F32),
                             _shifted_scratch(K, SC_PAD + tt, D), _shifted_scratch(K, tt + SC_PAD, D)], tt=tt)


def _place():
    x, y, c = lax.axis_index("x"), lax.axis_index("y"), lax.axis_index("c")
    chips = [(1 - x, y), (x, 1 - y), (1 - x, 1 - y)]
    return x, y, c, 2 * x + y, chips, (x, y, 1 - c)


def _half(rows, which):
    return pl.ds(pl.multiple_of(which * (rows // 2), SUBLANES), rows // 2)


_HBM = pl.BlockSpec(memory_space=pl.ANY)


def _gather_shards(name, items):
    n = len(items)
    shapes = [a.shape[-2:] for a, _ in items]

    def body(*refs):
        srcs, outs = refs[:n], refs[n:2 * n]
        send1, recv1, send2, recv2, lsem = refs[2 * n:]
        x, y, c, k, chips, sib = _place()

        def shard(i):
            return srcs[i] if items[i][1] is None else srcs[i].at[items[i][1]]

        started, locs = [], []
        for i in range(n):
            rows = shapes[i][0]
            lc = pltpu.make_async_copy(shard(i), outs[i].at[k], lsem.at[i])
            lc.start()
            locs.append(lc)
            for j, (cx, cy) in enumerate(chips):
                cp = pltpu.make_async_remote_copy(
                    src_ref=shard(i).at[_half(rows, c)], dst_ref=outs[i].at[k, _half(rows, c)],
                    send_sem=send1.at[i, j], recv_sem=recv1.at[i, j], device_id=(cx, cy, c), device_id_type=MESH)
                cp.start()
                started.append(cp)
        for i in range(n):
            rows = shapes[i][0]
            for j, (cx, cy) in enumerate(chips):
                blk = outs[i].at[2 * cx + cy, _half(rows, c)]
                pltpu.make_async_remote_copy(
                    src_ref=blk, dst_ref=blk, send_sem=send1.at[i, j], recv_sem=recv1.at[i, j],
                    device_id=(cx, cy, c), device_id_type=MESH).wait_recv()
                fw = pltpu.make_async_remote_copy(
                    src_ref=blk, dst_ref=blk, send_sem=send2.at[i, j], recv_sem=recv2.at[i, j],
                    device_id=sib, device_id_type=MESH)
                fw.start()
                started.append(fw)
        for i in range(n):
            rows = shapes[i][0]
            for j, (cx, cy) in enumerate(chips):
                blk = outs[i].at[2 * cx + cy, _half(rows, 1 - c)]
                pltpu.make_async_remote_copy(
                    src_ref=blk, dst_ref=blk, send_sem=send2.at[i, j], recv_sem=recv2.at[i, j],
                    device_id=sib, device_id_type=MESH).wait_recv()
        for cp in started:
            cp.wait_send()
        for lc in locs:
            lc.wait()

    return pl.pallas_call(
        body, name=name,
        in_specs=[_HBM] * n, out_specs=[_HBM] * n,
        out_shape=[jax.ShapeDtypeStruct((N_SHARDS,) + tuple(s), a.dtype) for s, (a, _) in zip(shapes, items)],
        scratch_shapes=[pltpu.SemaphoreType.DMA((n, 3))] * 4 + [pltpu.SemaphoreType.DMA((n,))],
    )(*[a for a, _ in items])


def _cast_place(name, w, layer, pos):
    _, R, C = w.shape
    tr = _tile(R, 256)

    def body(x_ref, y_ref, c_ref, w_ref, o_ref):
        o_ref[...] = w_ref[...].astype(BF16)

    return pl.pallas_call(
        body, name=name,
        grid_spec=pltpu.PrefetchScalarGridSpec(
            num_scalar_prefetch=3, grid=(R // tr,),
            in_specs=[pl.BlockSpec((None, tr, C), lambda r, xr, yr, cr: (layer, r, 0))],
            out_specs=pl.BlockSpec((None, tr, C), lambda r, xr, yr, cr: (2 * xr[0] + yr[0], r, 0))),
        out_shape=jax.ShapeDtypeStruct((N_SHARDS, R, C), BF16),
        compiler_params=_cparams("parallel"),
    )(*pos, w)


_IN_HBM = pl.BlockSpec(memory_space=pltpu.HBM)
_SEM = pl.BlockSpec(memory_space=pltpu.SEMAPHORE)
_SPLIT_COPY_PARAMS = pltpu.CompilerParams(has_side_effects=pltpu.SideEffectType.DATAFLOW_SIDE_EFFECTING)


def _in_hbm(a):
    return pltpu.with_memory_space_constraint(a, pltpu.HBM)


def _gather_copy(ref, i, j, chip_xy, c, k_src, rows, send, recv):
    blk = ref.at[k_src, _half(rows, c)]
    return pltpu.make_async_remote_copy(
        src_ref=blk, dst_ref=blk, send_sem=send.at[3 * i + j], recv_sem=recv.at[3 * i + j],
        device_id=(*chip_xy, c), device_id_type=MESH)


def _gather_start(name, bufs, after):
    n = len(bufs)

    def body(*refs):
        ins = refs[:n]
        send, recv = refs[n + 1], refs[n + 2]
        token = refs[-1]
        x, y, c, k, chips, sib = _place()
        for i in range(n):
            for j, chip_xy in enumerate(chips):
                _gather_copy(ins[i], i, j, chip_xy, c, k, bufs[i].shape[1], send, recv).start()
        token[...] = jnp.zeros_like(token)

    res = pl.pallas_call(
        body, name=name,
        in_specs=[_IN_HBM] * n + [_HBM],
        out_specs=[_SEM, _SEM] + [_IN_HBM] * n + [pl.BlockSpec(memory_space=pltpu.VMEM)],
        out_shape=[pltpu.SemaphoreType.DMA((3 * n,)), pltpu.SemaphoreType.DMA((3 * n,))]
        + [pltpu.HBM(b.shape, b.dtype) for b in bufs] + [jax.ShapeDtypeStruct((SUBLANES, 128), F32)],
        input_output_aliases={i: 2 + i for i in range(n)},
        compiler_params=_SPLIT_COPY_PARAMS,
    )(*[_in_hbm(b) for b in bufs], after)
    return res[0], res[1], list(res[2:2 + n]), res[-1]


def _gather_wait(name, bufs, send, recv, after):
    n = len(bufs)

    def body(*refs):
        ins = refs[:n]
        send_ref, recv_ref = refs[n], refs[n + 1]
        x, y, c, k, chips, sib = _place()
        for i in range(n):
            for j, chip_xy in enumerate(chips):
                rows = bufs[i].shape[1]
                _gather_copy(ins[i], i, j, chip_xy, c, k, rows, send_ref, recv_ref).wait_send()
                _gather_copy(ins[i], i, j, chip_xy, c, 2 * chip_xy[0] + chip_xy[1], rows, send_ref, recv_ref).wait_recv()

    return pl.pallas_call(
        body, name=name,
        in_specs=[_IN_HBM] * n + [_SEM, _SEM, _HBM],
        out_specs=[_IN_HBM] * n,
        out_shape=[pltpu.HBM(b.shape, b.dtype) for b in bufs],
        input_output_aliases={i: i for i in range(n)},
        compiler_params=_SPLIT_COPY_PARAMS,
    )(*bufs, send, recv, after)


def _gather_forward(name, bufs):
    n = len(bufs)

    def body(*refs):
        outs = refs[n:2 * n]
        send, recv = refs[2 * n:]
        x, y, c, k, chips, sib = _place()
        started = []
        for i in range(n):
            rows = bufs[i].shape[1]
            for j, (cx, cy) in enumerate(chips):
                blk = outs[i].at[2 * cx + cy, _half(rows, c)]
                fw = pltpu.make_async_remote_copy(
                    src_ref=blk, dst_ref=blk, send_sem=send.at[i, j], recv_sem=recv.at[i, j],
                    device_id=sib, device_id_type=MESH)
                fw.start()
                started.append(fw)
        for i in range(n):
            rows = bufs[i].shape[1]
            for j, (cx, cy) in enumerate(chips):
                blk = outs[i].at[2 * cx + cy, _half(rows, 1 - c)]
                pltpu.make_async_remote_copy(
                    src_ref=blk, dst_ref=blk, send_sem=send.at[i, j], recv_sem=recv.at[i, j],
                    device_id=sib, device_id_type=MESH).wait_recv()
        for cp in started:
            cp.wait_send()

    return pl.pallas_call(
        body, name=name, in_specs=[_HBM] * n, out_specs=[_HBM] * n,
        out_shape=[jax.ShapeDtypeStruct(b.shape, b.dtype) for b in bufs],
        input_output_aliases={i: i for i in range(n)},
        scratch_shapes=[pltpu.SemaphoreType.DMA((n, 3))] * 2,
    )(*bufs)


def _swap_halves(name, parts):
    n = len(parts)

    def body(*refs):
        srcs, outs = refs[:n], refs[n:2 * n]
        send, recv = refs[2 * n:]
        x, y, c, k, chips, sib = _place()
        cps = []
        for i in range(n):
            rows = parts[i].shape[1]
            cp = pltpu.make_async_remote_copy(
                src_ref=srcs[i].at[:, _half(rows, 1 - c)], dst_ref=outs[i],
                send_sem=send.at[i], recv_sem=recv.at[i], device_id=sib, device_id_type=MESH)
            cp.start()
            cps.append(cp)
        for cp in cps:
            cp.wait()

    return pl.pallas_call(
        body, name=name, in_specs=[_HBM] * n, out_specs=[_HBM] * n,
        out_shape=[jax.ShapeDtypeStruct((p.shape[0], p.shape[1] // 2, p.shape[2]), p.dtype) for p in parts],
        scratch_shapes=[pltpu.SemaphoreType.DMA((n,))] * 2,
    )(*parts)


def _sibling_start(name, bufs, n_copies, make):
    nb = len(bufs)

    def body(*refs):
        send, recv = refs[nb], refs[nb + 1]
        token = refs[-1]
        x, y, c, k, chips, sib = _place()
        for cp in make(refs[:nb], c, sib, send, recv):
            cp.start()
        token[...] = jnp.zeros_like(token)

    res = pl.pallas_call(
        body, name=name,
        in_specs=[_IN_HBM] * nb,
        out_specs=[_SEM, _SEM] + [_IN_HBM] * nb + [pl.BlockSpec(memory_space=pltpu.VMEM)],
        out_shape=[pltpu.SemaphoreType.DMA((n_copies,)), pltpu.SemaphoreType.DMA((n_copies,))]
        + [pltpu.HBM(b.shape, b.dtype) for b in bufs] + [jax.ShapeDtypeStruct((SUBLANES, 128), F32)],
        input_output_aliases={i: 2 + i for i in range(nb)},
        compiler_params=_SPLIT_COPY_PARAMS,
    )(*[_in_hbm(b) for b in bufs])
    return res[0], res[1], list(res[2:2 + nb]), res[-1]


def _sibling_wait(name, bufs, send, recv, make, afters):
    nb = len(bufs)

    def body(*refs):
        x, y, c, k, chips, sib = _place()
        for cp in make(refs[:nb], c, sib, refs[nb], refs[nb + 1]):
            cp.wait_send()
            cp.wait_recv()

    return list(pl.pallas_call(
        body, name=name,
        in_specs=[_IN_HBM] * nb + [_SEM, _SEM] + [_HBM] * len(afters),
        out_specs=[_IN_HBM] * nb,
        out_shape=[pltpu.HBM(b.shape, b.dtype) for b in bufs],
        input_output_aliases={i: i for i in range(nb)},
        compiler_params=_SPLIT_COPY_PARAMS,
    )(*bufs, send, recv, *afters))


def _swap_copies(parts):
    n = len(parts)

    def make(refs, c, sib, send, recv):
        return [pltpu.make_async_remote_copy(
            src_ref=refs[i].at[:, _half(parts[i].shape[1], 1 - c)], dst_ref=refs[n + i],
            send_sem=send.at[i], recv_sem=recv.at[i], device_id=sib, device_id_type=MESH) for i in range(n)]

    return make


def _join_copies(fulls):
    def make(refs, c, sib, send, recv):
        cps = []
        for i, f in enumerate(fulls):
            blk = refs[i].at[_half(f.shape[0], c)]
            cps.append(pltpu.make_async_remote_copy(
                src_ref=blk, dst_ref=blk, send_sem=send.at[i], recv_sem=recv.at[i],
                device_id=sib, device_id_type=MESH))
        return cps

    return make


def _scatter_copy(src_ref, land_ref, i, j, chip_xy, c, send, recv):
    return pltpu.make_async_remote_copy(
        src_ref=src_ref.at[2 * chip_xy[0] + chip_xy[1]], dst_ref=land_ref.at[j],
        send_sem=send.at[3 * i + j], recv_sem=recv.at[3 * i + j], device_id=(*chip_xy, c), device_id_type=MESH)


def _scatter_start(name, sums):
    n = len(sums)
    lands = [lax.empty((3,) + s.shape[1:], s.dtype) for s in sums]

    def body(*refs):
        srcs, lnds = refs[:n], refs[n:2 * n]
        send, recv = refs[2 * n], refs[2 * n + 1]
        token = refs[-1]
        x, y, c, k, chips, sib = _place()
        for i in range(n):
            for j, chip_xy in enumerate(chips):
                _scatter_copy(srcs[i], lnds[i], i, j, chip_xy, c, send, recv).start()
        token[...] = jnp.zeros_like(token)

    res = pl.pallas_call(
        body, name=name,
        in_specs=[_IN_HBM] * (2 * n),
        out_specs=[_SEM, _SEM] + [_IN_HBM] * (2 * n) + [pl.BlockSpec(memory_space=pltpu.VMEM)],
        out_shape=[pltpu.SemaphoreType.DMA((3 * n,)), pltpu.SemaphoreType.DMA((3 * n,))]
        + [pltpu.HBM(a.shape, a.dtype) for a in list(sums) + lands] + [jax.ShapeDtypeStruct((SUBLANES, 128), F32)],
        input_output_aliases={i: 2 + i for i in range(2 * n)},
        compiler_params=_SPLIT_COPY_PARAMS,
    )(*[_in_hbm(a) for a in list(sums) + lands])
    return res[0], res[1], list(res[2:2 + n]), list(res[2 + n:2 + 2 * n]), res[-1]


def _scatter_wait(name, sums, lands, send, recv, afters):
    n = len(sums)

    def body(*refs):
        srcs, lnds = refs[:n], refs[n:2 * n]
        send_ref, recv_ref = refs[2 * n], refs[2 * n + 1]
        x, y, c, k, chips, sib = _place()
        for i in range(n):
            for j, chip_xy in enumerate(chips):
                cp = _scatter_copy(srcs[i], lnds[i], i, j, chip_xy, c, send_ref, recv_ref)
                cp.wait_send()
                cp.wait_recv()

    res = pl.pallas_call(
        body, name=name,
        in_specs=[_IN_HBM] * (2 * n) + [_SEM, _SEM] + [_HBM] * len(afters),
        out_specs=[_IN_HBM] * (2 * n),
        out_shape=[pltpu.HBM(a.shape, a.dtype) for a in list(sums) + list(lands)],
        input_output_aliases={i: i for i in range(2 * n)},
        compiler_params=_SPLIT_COPY_PARAMS,
    )(*sums, *lands, send, recv, *afters)
    return list(res[:n]), list(res[n:])


def _sum_over_devices(name, buf, loss_row, afters):
    R, D = buf.shape
    n_after = len(afters)

    def body(x_ref, *rest):
        all_ref, tot_ref, loss_ref, send_sems, recv_sems, local_sem = rest[n_after:]
        x, y, c, k, chips, sib = _place()
        me = (x, y, c)

        def block(px, py, pc):
            return all_ref.at[4 * px + 2 * py + pc]

        def copy(kk, blk, to, src=None):
            return pltpu.make_async_remote_copy(
                src_ref=block(*blk) if src is None else src, dst_ref=block(*blk),
                send_sem=send_sems.at[kk], recv_sem=recv_sems.at[kk], device_id=to, device_id_type=MESH)

        mine = pltpu.make_async_copy(x_ref, block(*me), local_sem)
        mine.start()
        first = [copy(0, me, sib, src=x_ref)]
        first += [copy(1 + j, me, (*chip, c), src=x_ref) for j, chip in enumerate(chips)]
        for cp in first:
            cp.start()
        passed = [copy(4 + j, (*chip, c), sib) for j, chip in enumerate(chips)]
        for j, chip in enumerate(chips):
            copy(1 + j, (*chip, c), me).wait_recv()
            passed[j].start()
        copy(0, sib, me).wait_recv()
        for j, chip in enumerate(chips):
            copy(4 + j, (*chip, 1 - c), me).wait_recv()
        for cp in first + passed:
            cp.wait_send()
        mine.wait()
        rc = _tile(R, 32)
        for r0 in range(0, R, rc):
            tot = all_ref[0, r0:r0 + rc, :]
            for d in range(1, N_DEVICES):
                tot = tot + all_ref[d, r0:r0 + rc, :]
            tot_ref[r0:r0 + rc, :] = tot
        loss = 0.5 * jnp.sum(tot_ref[loss_row:loss_row + 1, :]) / D
        loss_ref[...] = jnp.full(loss_ref.shape, loss, F32)

    vm = pl.BlockSpec(memory_space=pltpu.VMEM)
    return pl.pallas_call(
        body, name=name, in_specs=[vm] + [_HBM] * n_after, out_specs=[vm, vm, vm],
        out_shape=[jax.ShapeDtypeStruct((N_DEVICES, R, D), F32), jax.ShapeDtypeStruct((R, D), F32),
                   jax.ShapeDtypeStruct((SUBLANES, 128), F32)],
        scratch_shapes=[pltpu.SemaphoreType.DMA((7,)), pltpu.SemaphoreType.DMA((7,)), pltpu.SemaphoreType.DMA],
        compiler_params=pltpu.CompilerParams(vmem_limit_bytes=V7X_VMEM_LIMIT_BYTES),
    )(buf, *afters)[1:]


def _add_my_half(name, part, got, pos):
    S, R, C = part.shape
    R2 = R // 2
    tr = _tile(R2, 1024)
    q = R2 // tr

    def body(x_ref, y_ref, c_ref, p_ref, g_ref, o_ref):
        o_ref[...] = (p_ref[...].astype(F32) + g_ref[...].astype(F32)).astype(o_ref.dtype)

    return pl.pallas_call(
        body, name=name,
        grid_spec=pltpu.PrefetchScalarGridSpec(
            num_scalar_prefetch=3, grid=(S, q),
            in_specs=[pl.BlockSpec((None, tr, C), lambda s, r, xr, yr, cr: (s, cr[0] * q + r, 0)),
                      pl.BlockSpec((None, tr, C), lambda s, r, xr, yr, cr: (s, r, 0))],
            out_specs=pl.BlockSpec((None, tr, C), lambda s, r, xr, yr, cr: (s, r, 0))),
        out_shape=jax.ShapeDtypeStruct((S, R2, C), BF16),
        compiler_params=_cparams("parallel", "parallel"),
    )(*pos, part, got)


def _add_owner(name, sums, got, pos):
    _, R2, C = sums.shape
    tr = _tile(R2, 512)
    q = R2 // tr

    def body(x_ref, y_ref, c_ref, s_ref, g_ref, o_ref):
        acc = s_ref[...].astype(F32)
        for j in range(3):
            acc = acc + g_ref[j].astype(F32)
        o_ref[...] = acc

    return pl.pallas_call(
        body, name=name,
        grid_spec=pltpu.PrefetchScalarGridSpec(
            num_scalar_prefetch=3, grid=(q,),
            in_specs=[pl.BlockSpec((None, tr, C), lambda r, xr, yr, cr: (2 * xr[0] + yr[0], r, 0)),
                      pl.BlockSpec((3, tr, C), lambda r, xr, yr, cr: (0, r, 0))],
            out_specs=pl.BlockSpec((tr, C), lambda r, xr, yr, cr: (cr[0] * q + r, 0))),
        out_shape=jax.ShapeDtypeStruct((2 * R2, C), F32),
        compiler_params=_cparams("parallel"),
    )(*pos, sums, got)


def _adamw(name, w, m, v, g):
    R, C = w.shape
    tr = SUBLANES
    while 2 * tr * C <= ADAMW_TILE_ELEMS:
        tr *= 2
    tr = _tile(R, tr)
    bc1 = 1.0 - ADAM_B1 ** ADAM_STEP
    bc2 = 1.0 - ADAM_B2 ** ADAM_STEP

    def body(w_ref, m_ref, v_ref, g_ref, go_ref, d_ref, mo_ref, vo_ref):
        gg = g_ref[...]
        m2 = ADAM_B1 * m_ref[...] + (1.0 - ADAM_B1) * gg
        v2 = ADAM_B2 * v_ref[...] + (1.0 - ADAM_B2) * (gg * gg)
        go_ref[...] = gg
        mo_ref[...] = m2
        vo_ref[...] = v2
        d_ref[...] = -ADAM_LR * ((m2 / bc1) / (jnp.sqrt(v2 / bc2) + ADAM_EPS) + ADAM_WD * w_ref[...])

    spec = pl.BlockSpec((tr, C), lambda r: (r, 0))
    return pl.pallas_call(
        body, name=name, grid=(R // tr,), in_specs=[spec] * 4, out_specs=[spec] * 4,
        out_shape=[jax.ShapeDtypeStruct((R, C), F32)] * 4,
        compiler_params=_cparams("parallel"),
    )(w, m, v, g)


def _adamw_slab(name, w, m, v, g, layer, prev):
    L, R, C = w.shape
    tr = SUBLANES
    while 2 * tr * C <= ADAMW_TILE_ELEMS:
        tr *= 2
    tr = _tile(R, tr)
    bc1 = 1.0 - ADAM_B1 ** ADAM_STEP
    bc2 = 1.0 - ADAM_B2 ** ADAM_STEP

    def body(w_ref, m_ref, v_ref, g_ref, *rest):
        go_ref, d_ref, mo_ref, vo_ref = rest[-4:]
        gg = g_ref[...]
        m2 = ADAM_B1 * m_ref[...] + (1.0 - ADAM_B1) * gg
        v2 = ADAM_B2 * v_ref[...] + (1.0 - ADAM_B2) * (gg * gg)
        go_ref[...] = gg
        mo_ref[...] = m2
        vo_ref[...] = v2
        d_ref[...] = -ADAM_LR * ((m2 / bc1) / (jnp.sqrt(v2 / bc2) + ADAM_EPS) + ADAM_WD * w_ref[...])

    slab = pl.BlockSpec((None, tr, C), lambda r: (layer, r, 0))
    n_prev = 0 if prev is None else 4
    return pl.pallas_call(
        body, name=name, grid=(R // tr,),
        in_specs=[slab] * 3 + [pl.BlockSpec((tr, C), lambda r: (r, 0))] + [_HBM] * n_prev,
        out_specs=[slab] * 4,
        out_shape=[jax.ShapeDtypeStruct((L, R, C), F32)] * 4,
        input_output_aliases={4 + i: i for i in range(n_prev)},
        compiler_params=_cparams("parallel"),
    )(w, m, v, g, *(prev or ()))


def _swap_begin(tag, parts):
    lands = [lax.empty((p.shape[0], p.shape[1] // 2, p.shape[2]), p.dtype) for p in parts]
    return _sibling_start(f"rs_swap_start_{tag}", list(parts) + lands, len(parts), _swap_copies(parts))


def _reduce_begin(tag, early, swapping, late, pos, after):
    send, recv, bufs, _ = swapping
    bufs = _sibling_wait(f"rs_swap_wait_{tag}", bufs, send, recv, _swap_copies(early), [after])
    parts = bufs[:len(early)] + list(late)
    got = bufs[len(early):] + list(_swap_halves(f"rs_swap_{tag}", late))
    sums = [_add_my_half(f"rs_add2_{tag}_{i}", p, g, pos) for i, (p, g) in enumerate(zip(parts, got))]
    return _scatter_start(f"rs_scatter_start_{tag}", sums)


def _reduce_middle(tag, started, pos, afters):
    send, recv, sums, lands, _ = started
    sums, lands = _scatter_wait(f"rs_scatter_wait_{tag}", sums, lands, send, recv, afters)
    fulls = [_add_owner(f"rs_add4_{tag}_{i}", s, q, pos) for i, (s, q) in enumerate(zip(sums, lands))]
    return _sibling_start(f"rs_join_start_{tag}", fulls, len(fulls), _join_copies(fulls))


def _reduce_end(tag, joining, afters):
    send, recv, fulls, _ = joining
    return _sibling_wait(f"rs_join_wait_{tag}", fulls, send, recv, _join_copies(fulls), afters)


def _pad_rows(a):
    r = (-a.shape[0]) % SUBLANES
    return jnp.pad(a, ((0, r), (0, 0))) if r else a


def kernel(x, p, norm_mix, norm_mlp, norm_ple, cf_w_pw1, cf_b_pw1, cf_w_dw, cf_b_dw, cf_norm, cf_w_pw2, cf_b_pw2, sc_w_in, sc_w_conv, sc_w_out, mlp_w1, mlp_w2, ple_w_proj, ple_w_gate, norm_final, loss_target, m_norm_mix, m_norm_mlp, m_norm_ple, m_cf_w_pw1, m_cf_b_pw1, m_cf_w_dw, m_cf_b_dw, m_cf_norm, m_cf_w_pw2, m_cf_b_pw2, m_sc_w_in, m_sc_w_conv, m_sc_w_out, m_mlp_w1, m_mlp_w2, m_ple_w_proj, m_ple_w_gate, m_norm_final, v_norm_mix, v_norm_mlp, v_norm_ple, v_cf_w_pw1, v_cf_b_pw1, v_cf_w_dw, v_cf_b_dw, v_cf_norm, v_cf_w_pw2, v_cf_b_pw2, v_sc_w_in, v_sc_w_conv, v_sc_w_out, v_mlp_w1, v_mlp_w2, v_ple_w_proj, v_ple_w_gate, v_norm_final):
    T, D = x.shape[1], x.shape[2]
    KA, KB = cf_w_dw.shape[1], sc_w_conv.shape[1]
    chip = (2 * lax.axis_index("x") + lax.axis_index("y")).astype(jnp.int32)
    pos = tuple(lax.axis_index(ax).astype(jnp.int32).reshape(1) for ax in ("x", "y", "c"))

    params = dict(norm_mix=norm_mix, norm_mlp=norm_mlp, norm_ple=norm_ple, cf_w_pw1=cf_w_pw1, cf_b_pw1=cf_b_pw1,
                  cf_w_dw=cf_w_dw, cf_b_dw=cf_b_dw, cf_norm=cf_norm, cf_w_pw2=cf_w_pw2, cf_b_pw2=cf_b_pw2,
                  sc_w_in=sc_w_in, sc_w_conv=sc_w_conv, sc_w_out=sc_w_out, mlp_w1=mlp_w1, mlp_w2=mlp_w2,
                  ple_w_proj=ple_w_proj, ple_w_gate=ple_w_gate, norm_final=norm_final)
    mom1 = dict(norm_mix=m_norm_mix, norm_mlp=m_norm_mlp, norm_ple=m_norm_ple, cf_w_pw1=m_cf_w_pw1,
                cf_b_pw1=m_cf_b_pw1, cf_w_dw=m_cf_w_dw, cf_b_dw=m_cf_b_dw, cf_norm=m_cf_norm, cf_w_pw2=m_cf_w_pw2,
                cf_b_pw2=m_cf_b_pw2, sc_w_in=m_sc_w_in, sc_w_conv=m_sc_w_conv, sc_w_out=m_sc_w_out,
                mlp_w1=m_mlp_w1, mlp_w2=m_mlp_w2, ple_w_proj=m_ple_w_proj, ple_w_gate=m_ple_w_gate,
                norm_final=m_norm_final)
    mom2 = dict(norm_mix=v_norm_mix, norm_mlp=v_norm_mlp, norm_ple=v_norm_ple, cf_w_pw1=v_cf_w_pw1,
                cf_b_pw1=v_cf_b_pw1, cf_w_dw=v_cf_w_dw, cf_b_dw=v_cf_b_dw, cf_norm=v_cf_norm, cf_w_pw2=v_cf_w_pw2,
                cf_b_pw2=v_cf_b_pw2, sc_w_in=v_sc_w_in, sc_w_conv=v_sc_w_conv, sc_w_out=v_sc_w_out,
                mlp_w1=v_mlp_w1, mlp_w2=v_mlp_w2, ple_w_proj=v_ple_w_proj, ple_w_gate=v_ple_w_gate,
                norm_final=v_norm_final)

    big = ("cf_w_pw1", "cf_w_pw2", "sc_w_in", "sc_w_out", "mlp_w1", "mlp_w2", "ple_w_proj", "ple_w_gate")
    row_sharded = ("cf_w_pw2", "sc_w_out", "mlp_w2", "ple_w_gate")

    def layer_names(i):
        return (["cf_w_pw1", "cf_w_pw2"] if i % 2 == 0 else ["sc_w_in", "sc_w_out"]) + \
            ["mlp_w1", "mlp_w2", "ple_w_proj", "ple_w_gate"]

    def layer_index(i, name):
        return i if name.startswith(("mlp", "ple")) else i // 2

    def gather_begin(tag, i, names, after):
        bufs = [_cast_place(f"place_{nm}_{i}", params[nm], layer_index(i, nm), pos) for nm in names]
        return names, _gather_start(f"gather_start_{tag}", bufs, after)

    def gather_end(tag, begun, after):
        names, (send, recv, bufs, _) = begun
        bufs = _gather_wait(f"gather_wait_{tag}", bufs, send, recv, after)
        bufs = _gather_forward(f"gather_fwd_{tag}", bufs)
        return {nm: g4.reshape(1, N_SHARDS * g4.shape[1], g4.shape[2]) if nm in row_sharded else g4
                for nm, g4 in zip(names, bufs)}

    conv_small = jnp.concatenate([_pad_rows(cf_w_dw[j]) for j in range(cf_w_dw.shape[0])]
                                 + [_pad_rows(sc_w_conv[j]) for j in range(sc_w_conv.shape[0])], axis=0)
    conv_shards = _gather_shards("gather_conv_w", [(conv_small, None)])[0]
    conv_all = jnp.transpose(conv_shards, (1, 0, 2)).reshape(conv_small.shape[0], D)
    ka_pad = KA + (-KA) % SUBLANES
    kb_pad = KB + (-KB) % SUBLANES
    w_dw_full = [conv_all[j * ka_pad:j * ka_pad + KA] for j in range(cf_w_dw.shape[0])]
    off = cf_w_dw.shape[0] * ka_pad
    w_conv_full = [conv_all[off + j * kb_pad:off + j * kb_pad + KB] for j in range(sc_w_conv.shape[0])]

    def vec(a):
        return a.reshape(1, -1)

    ident = lambda acc: (acc,)

    h = x[0]
    saved = []
    first = gather_begin("0m", 0, layer_names(0)[:2], conv_shards)
    rest = gather_begin("0r", 0, layer_names(0)[2:], first[1][2][0])
    W = [gather_end("0m", first, h)]
    for i in range(DEPTH):
        j = i // 2
        wl = W[i]
        s = dict(h=h)
        g_mix = vec(norm_mix[i])
        if i + 1 < DEPTH:
            nxt = gather_begin(f"{i + 1}", i + 1, layer_names(i + 1), wl[layer_names(i)[0]])
            g_mix = g_mix + nxt[1][3][0, 0]
        s["u"] = _rms_fwd(f"rms_mix_{i}", h, g_mix)
        if i % 2 == 0:
            s["a"] = _mm_nn(f"cf_pw1_{i}", s["u"], wl["cf_w_pw1"], lambda acc, b: (acc + b,), [BF16],
                            extras=[(vec(cf_b_pw1[j]), "n")])[0]
            s["v2"], s["v4"] = _cf_conv_fwd(f"cf_conv_{i}", s["a"], w_dw_full[j], vec(cf_b_dw[j]), vec(cf_norm[j]))
            h1 = _mm_nn(f"cf_pw2_{i}", s["v4"], wl["cf_w_pw2"], lambda acc, b, r: (r + (acc + b),), [F32],
                        extras=[(vec(cf_b_pw2[j]), "n"), (h, "mn")])[0]
        else:
            s["bcv"] = _mm_nn(f"sc_in_{i}", s["u"], wl["sc_w_in"], ident, [BF16])[0]
            s["y"] = _sc_conv_fwd(f"sc_conv_{i}", s["bcv"], w_conv_full[j])[0]
            h1 = _mm_nn(f"sc_out_{i}", s["y"], wl["sc_w_out"], lambda acc, r: (r + acc,), [F32],
                        extras=[(h, "mn")])[0]
        s["h1"] = h1
        if i == 0:
            wl.update(gather_end("0r", rest, h1))
        s["u2"] = _rms_fwd(f"rms_mlp_{i}", h1, vec(norm_mlp[i]))
        s["z"], s["hd"] = _mm_nn(f"mlp_w1_{i}", s["u2"], wl["mlp_w1"],
                                 lambda acc: (acc, jnp.square(jnp.maximum(acc, 0.0))), [BF16, BF16])
        h2 = _mm_nn(f"mlp_w2_{i}", s["hd"], wl["mlp_w2"], lambda acc, r: (r + acc,), [F32], extras=[(h1, "mn")])[0]
        s["h2"] = h2
        s["n3"] = _rms_fwd(f"rms_ple_{i}", h2, vec(norm_ple[i]))
        s["p"] = p[i, 0]
        s["e"] = _mm_nn(f"ple_proj_{i}", s["p"], wl["ple_w_proj"], ident, [BF16])[0]
        h, s["q"] = _mm_nn(f"ple_gate_{i}", s["n3"], wl["ple_w_gate"],
                           lambda acc, r, e: (r + _sigmoid(acc) * e.astype(F32), acc), [F32, BF16],
                           extras=[(h2, "mn"), (s["e"], "mn")])
        saved.append(s)
        if i + 1 < DEPTH:
            W.append(gather_end(f"{i + 1}", nxt, h))

    dh, dh16, dg_final, loss_cols = _loss_bwd("loss_bwd", h, vec(norm_final), loss_target[0])
    small = {"norm_final": dg_final, "loss": loss_cols}
    adam = {nm: None for nm in big}

    def reduce_names(i):
        names = layer_names(i)
        return names[2:] + names[:2]

    def update_layer(i, joining, afters):
        for nm, g in zip(reduce_names(i), _reduce_end(f"{i}", joining, afters)):
            l = layer_index(i, nm)
            adam[nm] = _adamw_slab(f"adamw_{nm}_{l}", params[nm], mom1[nm], mom2[nm], g, l, adam[nm])

    def shard_major(names, parts):
        return [pt.reshape(N_SHARDS, pt.shape[1] // N_SHARDS, pt.shape[2]) if nm in row_sharded else pt
                for nm, pt in zip(names, parts)]

    scattered = joining = None
    for i in reversed(range(DEPTH)):
        j = i // 2
        wl, s = W[i], saved[i]
        dq, de = _ple_elem_bwd(f"ple_elem_bwd_{i}", dh, s["q"], s["e"])
        d_proj = _mm_tn(f"ple_proj_dw_{i}", s["p"], de, N_SHARDS)
        d_gate = _mm_tn(f"ple_gate_dw_{i}", s["n3"], dq, 1)
        dn3 = _mm_nt(f"ple_gate_dx_{i}", dq, wl["ple_w_gate"], ident, [F32])[0]
        g_ple = vec(norm_ple[i])
        if joining is not None:
            g_ple = g_ple + joining[3][0, 0]
        dh, dh16, small[f"norm_ple_{i}"] = _rms_bwd(f"rms_ple_bwd_{i}", s["h2"], g_ple, dn3, dh)

        d_w2 = _mm_tn(f"mlp_w2_dw_{i}", s["hd"], dh16, 1)
        dz = _mm_nt(f"mlp_w2_dx_{i}", dh16, wl["mlp_w2"],
                    lambda acc, z: (acc * (2.0 * jnp.maximum(z.astype(F32), 0.0)),), [BF16], extras=[s["z"]])[0]
        d_w1 = _mm_tn(f"mlp_w1_dw_{i}", s["u2"], dz, N_SHARDS)
        du2 = _mm_nt(f"mlp_w1_dx_{i}", dz, wl["mlp_w1"], ident, [F32])[0]
        early = shard_major(reduce_names(i)[:4], [d_w1, d_w2, d_proj, d_gate])
        swapping = _swap_begin(f"{i}", early)
        g_mlp = vec(norm_mlp[i]) + swapping[3][0, 0]
        if i % 2 == 0:
            dh, dh16, small[f"norm_mlp_{i}"], small[f"cf_b_pw2_{j}"] = _rms_bwd(
                f"rms_mlp_bwd_{i}", s["h1"], g_mlp, du2, dh, want_colsum=True)
            d_mix_out = _mm_tn(f"cf_pw2_dw_{i}", s["v4"], dh16, 1)
            dv4 = _mm_nt(f"cf_pw2_dx_{i}", dh16, wl["cf_w_pw2"], ident, [F32])[0]
            dv2, small[f"cf_norm_{j}"], small[f"cf_b_dw_{j}"] = _cf_norm_bwd(
                f"cf_norm_bwd_{i}", s["v2"], vec(cf_norm[j]), dv4)
            da, small[f"cf_w_dw_{j}"], db1 = _cf_conv_bwd(f"cf_conv_bwd_{i}", dv2, s["a"], w_dw_full[j])
            small[f"cf_b_pw1_{j}"] = db1.reshape(2, D)
            d_mix_in = _mm_tn(f"cf_pw1_dw_{i}", s["u"], da, N_SHARDS)
            du = _mm_nt(f"cf_pw1_dx_{i}", da, wl["cf_w_pw1"], ident, [F32])[0]
        else:
            dh, dh16, small[f"norm_mlp_{i}"] = _rms_bwd(f"rms_mlp_bwd_{i}", s["h1"], g_mlp, du2, dh)
            d_mix_out = _mm_tn(f"sc_out_dw_{i}", s["y"], dh16, 1)
            dy = _mm_nt(f"sc_out_dx_{i}", dh16, wl["sc_w_out"], ident, [F32])[0]
            da, small[f"sc_w_conv_{j}"] = _sc_conv_bwd(f"sc_conv_bwd_{i}", dy, s["bcv"], w_conv_full[j])
            d_mix_in = _mm_tn(f"sc_in_dw_{i}", s["u"], da, N_SHARDS)
            du = _mm_nt(f"sc_in_dx_{i}", da, wl["sc_w_in"], ident, [F32])[0]

        late = shard_major(reduce_names(i)[4:], [d_mix_in, d_mix_out])
        started = _reduce_begin(f"{i}", early, swapping, late, pos, du)
        token = started[4]
        dh, dh16, small[f"norm_mix_{i}"] = _rms_bwd(f"rms_mix_bwd_{i}", s["h"], vec(norm_mix[i]) + token[0, 0], du, dh)
        if joining is not None:
            update_layer(i + 2, joining, [token])
            joining = None
        if scattered is not None:
            joining = _reduce_middle(f"{i + 1}", scattered, pos, [token])
        scattered = started
    grad_x = dh.reshape(x.shape)

    order = sorted(small)
    pieces, where, row = [], {}, 0
    for nm in order:
        pc = _pad_rows(small[nm])
        where[nm] = (row, small[nm].shape[0])
        row += pc.shape[0]
        pieces.append(pc)
    update_layer(1, joining, [token])
    updated = [res[3] for res in adam.values() if res is not None]
    total, loss_tile = _sum_over_devices("small_allsum", jnp.concatenate(pieces, axis=0), where["loss"][0], updated)
    loss = loss_tile[0, 0]
    update_layer(0, _reduce_middle("0", scattered, pos, [total] + updated), [])

    def small_sum(nm):
        r0, nr = where[nm]
        return total[r0:r0 + nr]

    def my_cols(a):
        return lax.dynamic_slice_in_dim(a, chip * (D // N_SHARDS), D // N_SHARDS, axis=1)

    g_small = {
        "norm_mix": jnp.concatenate([small_sum(f"norm_mix_{i}") for i in range(DEPTH)], axis=0),
        "norm_mlp": jnp.concatenate([small_sum(f"norm_mlp_{i}") for i in range(DEPTH)], axis=0),
        "norm_ple": jnp.concatenate([small_sum(f"norm_ple_{i}") for i in range(DEPTH)], axis=0),
        "cf_b_pw1": jnp.stack([small_sum(f"cf_b_pw1_{j}").reshape(2 * D) for j in range(DEPTH // 2)]),
        "cf_w_dw": jnp.stack([my_cols(small_sum(f"cf_w_dw_{j}")) for j in range(DEPTH // 2)]),
        "cf_b_dw": jnp.concatenate([small_sum(f"cf_b_dw_{j}") for j in range(DEPTH // 2)], axis=0),
        "cf_norm": jnp.concatenate([small_sum(f"cf_norm_{j}") for j in range(DEPTH // 2)], axis=0),
        "cf_b_pw2": jnp.concatenate([small_sum(f"cf_b_pw2_{j}") for j in range(DEPTH // 2)], axis=0),
        "sc_w_conv": jnp.stack([my_cols(small_sum(f"sc_w_conv_{j}")) for j in range(DEPTH // 2)]),
        "norm_final": small_sum("norm_final").reshape(D),
    }

    names_out = ["norm_mix", "norm_mlp", "norm_ple", "cf_w_pw1", "cf_b_pw1", "cf_w_dw", "cf_b_dw", "cf_norm",
                 "cf_w_pw2", "cf_b_pw2", "sc_w_in", "sc_w_conv", "sc_w_out", "mlp_w1", "mlp_w2", "ple_w_proj",
                 "ple_w_gate", "norm_final"]
    grad, delta, new_m, new_v = {}, {}, {}, {}
    for nm in names_out:
        w = params[nm]
        if nm in big:
            grad[nm], delta[nm], new_m[nm], new_v[nm] = adam[nm]
            continue
        g = g_small[nm]
        cols = w.shape[-1] if w.ndim > 1 else w.shape[0]
        two_d = lambda a: a.reshape(-1, cols)
        res = _adamw(f"adamw_{nm}", two_d(w), two_d(mom1[nm]), two_d(mom2[nm]), two_d(g))
        grad[nm], delta[nm], new_m[nm], new_v[nm] = [r.reshape(w.shape) for r in res]

    return (loss, grad_x, *[grad[n] for n in names_out], *[delta[n] for n in names_out],
            *[new_m[n] for n in names_out], *[new_v[n] for n in names_out])
```

```python
import jax
import jax.numpy as jnp
from jax import lax
from jax.experimental import pallas as pl
from jax.experimental.pallas import tpu as pltpu

F32 = jnp.float32
BF16 = jnp.bfloat16

EPS = 1e-6
ADAM_LR = 0.001
ADAM_B1 = 0.9
ADAM_B2 = 0.999
ADAM_EPS = 1e-08
ADAM_WD = 0.01
ADAM_STEP = 10

DEPTH = 4
N_SHARDS = 4
N_DEVICES = 8
V7X_VMEM_LIMIT_BYTES = 56 * 1024 * 1024
SUBLANES = 8
MESH = pl.DeviceIdType.MESH

MM_TM = 1024
MM_TN = 1024
MM_TK = 2048
MM_TW_K = 1024
MM_TW_T = 4096
MM_TX_K = 1024
MM_TX_N = 2048
MM_TX_K_SHARDS = 512
MM_TX_K_LONG = 256
MM_TX_LONG_N = 4096
MM_TN_CROWDED = 512
MM_LONG_K = 8192
MM_TN_LONG_K = 256
ADAMW_TILE_ELEMS = 256 * 2048

CONV_ROW_CHUNK = 32
CONV_LANE_CHUNK = 512
CONV_TILE_ROWS = 128
CONV_FWD_TILE_ROWS = 256
ROW_TILE = 256
LIGHT_ROW_TILE = 512


def _tile(dim, pref):
    if dim <= pref:
        return dim
    t = pref
    while dim % t:
        t //= 2
    return t


def _cparams(*sem):
    return pltpu.CompilerParams(dimension_semantics=sem, vmem_limit_bytes=V7X_VMEM_LIMIT_BYTES)


def _sigmoid(x):
    return 0.5 * (jnp.tanh(0.5 * x) + 1.0)


def _rms_r(x):
    return lax.rsqrt(jnp.mean(x * x, axis=-1, keepdims=True) + EPS)


def _mm_nn(name, a, b3, epilogue, out_dtypes, extras=()):
    M, K = a.shape
    S, Kb, Ns = b3.shape
    assert Kb == K
    N = S * Ns
    crowded = sum(kind == "mn" for _, kind in extras) > 1
    tm, tn, tk = _tile(M, MM_TM), _tile(Ns, MM_TN_CROWDED if crowded else MM_TN), _tile(K, MM_TK)
    if K >= MM_LONG_K:
        tn, tk = _tile(Ns, MM_TN_LONG_K), K
    per = Ns // tn
    nk = K // tk
    in_specs = [pl.BlockSpec((tm, tk), lambda i, j, k: (i, k)),
                pl.BlockSpec((None, tk, tn), lambda i, j, k: (j // per, k, j % per))]
    for _, kind in extras:
        if kind == "mn":
            in_specs.append(pl.BlockSpec((tm, tn), lambda i, j, k: (i, j)))
        else:
            in_specs.append(pl.BlockSpec((1, tn), lambda i, j, k: (0, j)))
    n_ex, n_o = len(extras), len(out_dtypes)

    def body(*refs):
        a_ref, b_ref = refs[:2]
        ex = refs[2:2 + n_ex]
        outs = refs[2 + n_ex:2 + n_ex + n_o]
        part = jnp.dot(a_ref[...].astype(BF16), b_ref[...], preferred_element_type=F32)

        def finish(acc):
            res = epilogue(acc, *[e[...] for e in ex])
            for r, o in zip(res, outs):
                o[...] = r.astype(o.dtype)

        if nk == 1:
            finish(part)
        else:
            acc_ref = refs[-1]
            k = pl.program_id(2)

            @pl.when(k == 0)
            def _():
                acc_ref[...] = part

            @pl.when(k > 0)
            def _():
                acc_ref[...] += part

            @pl.when(k == nk - 1)
            def _():
                finish(acc_ref[...])

    res = pl.pallas_call(
        body, name=name, grid=(M // tm, N // tn, nk),
        in_specs=in_specs,
        out_specs=[pl.BlockSpec((tm, tn), lambda i, j, k: (i, j)) for _ in out_dtypes],
        out_shape=[jax.ShapeDtypeStruct((M, N), dt) for dt in out_dtypes],
        scratch_shapes=[pltpu.VMEM((tm, tn), F32)] if nk > 1 else [],
        compiler_params=_cparams("parallel", "parallel", "arbitrary"),
    )(a, b3, *[e for e, _ in extras])
    return res


def _mm_nt_shards(name, g, w3, epilogue, out_dtypes, extras):
    M, N = g.shape
    S, K, Ns = w3.shape
    tm = _tile(M, MM_TM)
    tkk = _tile(K, MM_TX_K_LONG if N > MM_TX_LONG_N else MM_TX_K_SHARDS)
    n_ex, n_o = len(extras), len(out_dtypes)

    def body(*refs):
        g_ref = refs[0]
        w_refs = refs[1:1 + S]
        ex = refs[1 + S:1 + S + n_ex]
        outs = refs[1 + S + n_ex:1 + S + n_ex + n_o]
        acc = None
        for s in range(S):
            part = lax.dot_general(g_ref[:, s * Ns:(s + 1) * Ns].astype(BF16), w_refs[s][...],
                                   (((1,), (1,)), ((), ())), preferred_element_type=F32)
            acc = part if acc is None else acc + part
        for r, o in zip(epilogue(acc, *[e[...] for e in ex]), outs):
            o[...] = r.astype(o.dtype)

    return pl.pallas_call(
        body, name=name, grid=(M // tm, K // tkk),
        in_specs=[pl.BlockSpec((tm, N), lambda i, kk: (i, 0))]
        + [pl.BlockSpec((None, tkk, Ns), lambda i, kk, s=s: (s, kk, 0)) for s in range(S)]
        + [pl.BlockSpec((tm, tkk), lambda i, kk: (i, kk)) for _ in extras],
        out_specs=[pl.BlockSpec((tm, tkk), lambda i, kk: (i, kk)) for _ in out_dtypes],
        out_shape=[jax.ShapeDtypeStruct((M, K), dt) for dt in out_dtypes],
        compiler_params=_cparams("parallel", "parallel"),
    )(g, *([w3] * S), *extras)


def _mm_nt(name, g, w3, epilogue, out_dtypes, extras=()):
    M, N = g.shape
    S, K, Ns = w3.shape
    assert S * Ns == N
    if S > 1:
        return _mm_nt_shards(name, g, w3, epilogue, out_dtypes, extras)
    tm, tn, tkk = _tile(M, MM_TM), _tile(Ns, MM_TX_N), _tile(K, MM_TX_K)
    per = Ns // tn
    nn = N // tn
    n_ex, n_o = len(extras), len(out_dtypes)

    def body(*refs):
        g_ref, w_ref = refs[:2]
        ex = refs[2:2 + n_ex]
        outs = refs[2 + n_ex:2 + n_ex + n_o]
        part = lax.dot_general(g_ref[...].astype(BF16), w_ref[...], (((1,), (1,)), ((), ())),
                               preferred_element_type=F32)

        def finish(acc):
            res = epilogue(acc, *[e[...] for e in ex])
            for r, o in zip(res, outs):
                o[...] = r.astype(o.dtype)

        if nn == 1:
            finish(part)
        else:
            acc_ref = refs[-1]
            n = pl.program_id(2)

            @pl.when(n == 0)
            def _():
                acc_ref[...] = part

            @pl.when(n > 0)
            def _():
                acc_ref[...] += part

            @pl.when(n == nn - 1)
            def _():
                finish(acc_ref[...])

    return pl.pallas_call(
        body, name=name, grid=(M // tm, K // tkk, nn),
        in_specs=[pl.BlockSpec((tm, tn), lambda i, kk, n: (i, n)),
                  pl.BlockSpec((None, tkk, tn), lambda i, kk, n: (n // per, kk, n % per))]
        + [pl.BlockSpec((tm, tkk), lambda i, kk, n: (i, kk)) for _ in extras],
        out_specs=[pl.BlockSpec((tm, tkk), lambda i, kk, n: (i, kk)) for _ in out_dtypes],
        out_shape=[jax.ShapeDtypeStruct((M, K), dt) for dt in out_dtypes],
        scratch_shapes=[pltpu.VMEM((tm, tkk), F32)] if nn > 1 else [],
        compiler_params=_cparams("parallel", "parallel", "arbitrary"),
    )(g, w3, *extras)


def _mm_tn(name, a, g, n_shards):
    T, K = a.shape
    _, N = g.shape
    Ns = N // n_shards
    tk, tn, tt = _tile(K, MM_TW_K), _tile(Ns, MM_TN), _tile(T, MM_TW_T)
    per = Ns // tn
    nt = T // tt

    def body(a_ref, g_ref, o_ref, *scratch):
        part = lax.dot_general(a_ref[...].astype(BF16), g_ref[...].astype(BF16), (((0,), (0,)), ((), ())),
                               preferred_element_type=F32)
        if nt == 1:
            o_ref[...] = part.astype(o_ref.dtype)
            return
        acc_ref, = scratch
        t = pl.program_id(2)

        @pl.when(t == 0)
        def _():
            acc_ref[...] = part

        @pl.when(t > 0)
        def _():
            acc_ref[...] += part

        @pl.when(t == nt - 1)
        def _():
            o_ref[...] = acc_ref[...].astype(o_ref.dtype)

    return pl.pallas_call(
        body, name=name, grid=(K // tk, N // tn, nt),
        in_specs=[pl.BlockSpec((tt, tk), lambda i, j, t: (t, i)),
                  pl.BlockSpec((tt, tn), lambda i, j, t: (t, j))],
        out_specs=pl.BlockSpec((None, tk, tn), lambda i, j, t: (j // per, i, j % per)),
        out_shape=jax.ShapeDtypeStruct((n_shards, K, Ns), BF16),
        scratch_shapes=[pltpu.VMEM((tk, tn), F32)] if nt > 1 else [],
        compiler_params=_cparams("parallel", "parallel", "arbitrary"),
    )(a, g)


def _rowwise(name, fn, ins, outs, accs=(), scratch=(), tt=ROW_TILE):
    T = next(a.shape[0] for a, kind in ins if kind == "row")
    tt = _tile(T, tt)
    n = T // tt
    in_specs = []
    for a, kind in ins:
        w = a.shape[1]
        if kind == "row":
            in_specs.append(pl.BlockSpec((tt, w), lambda i: (i, 0)))
        elif kind == "vec":
            in_specs.append(pl.BlockSpec(a.shape, lambda i: (0, 0)))
        elif kind[0] == "prev":
            pad = kind[1]
            in_specs.append(pl.BlockSpec((pad, w), lambda i, q=tt // pad: (jnp.maximum(i * q - 1, 0), 0)))
        else:
            pad = kind[1]
            in_specs.append(pl.BlockSpec((pad, w), lambda i, q=tt // pad, last=T // pad - 1:
                                         (jnp.minimum((i + 1) * q, last), 0)))
    n_in, n_out, n_acc = len(ins), len(outs), len(accs)

    def body(*refs):
        i = pl.program_id(0)
        in_refs = refs[:n_in]
        out_refs = refs[n_in:n_in + n_out]
        acc_refs = refs[n_in + n_out:n_in + n_out + n_acc]
        scr = refs[n_in + n_out + n_acc:]
        if n_acc:
            @pl.when(i == 0)
            def _():
                for r in acc_refs:
                    r[...] = jnp.zeros_like(r)
        fn(i, n, in_refs, out_refs, acc_refs, scr)

    res = pl.pallas_call(
        body, name=name, grid=(n,),
        in_specs=in_specs,
        out_specs=[pl.BlockSpec((tt, w), lambda i: (i, 0)) for w, _ in outs]
        + [pl.BlockSpec((r, w), lambda i: (0, 0)) for r, w in accs],
        out_shape=[jax.ShapeDtypeStruct((T, w), dt) for w, dt in outs]
        + [jax.ShapeDtypeStruct((r, w), F32) for r, w in accs],
        scratch_shapes=list(scratch),
        compiler_params=_cparams("arbitrary"),
    )(*[a for a, _ in ins])
    return res


def _colsum(x):
    return jnp.sum(x, axis=0, keepdims=True)


def _rms_fwd(name, h, g):
    D = h.shape[1]

    def fn(i, n, ins, outs, accs, scr):
        x = ins[0][...]
        outs[0][...] = (x * _rms_r(x) * ins[1][...]).astype(BF16)

    return _rowwise(name, fn, [(h, "row"), (g, "vec")], [(D, BF16)], tt=LIGHT_ROW_TILE)[0]


def _rms_bwd(name, h, g, du, dh_in, want_colsum=False):
    D = h.shape[1]

    def fn(i, n, ins, outs, accs, scr):
        x = ins[0][...]
        gg = ins[1][...]
        d = ins[2][...].astype(F32)
        r = _rms_r(x)
        xn = x * r
        t = d * gg
        dh = ins[3][...] + r * (t - xn * jnp.mean(t * xn, axis=-1, keepdims=True))
        outs[0][...] = dh
        outs[1][...] = dh.astype(BF16)
        accs[0][...] += _colsum(d * xn)
        if want_colsum:
            accs[1][...] += _colsum(dh)

    return _rowwise(name, fn, [(h, "row"), (g, "vec"), (du, "row"), (dh_in, "row")],
                    [(D, F32), (D, BF16)], accs=[(1, D)] * (2 if want_colsum else 1))


def _loss_bwd(name, h, g, tgt):
    D = h.shape[1]

    def fn(i, n, ins, outs, accs, scr):
        x = ins[0][...]
        gg = ins[1][...]
        r = _rms_r(x)
        xn = x * r
        err = xn * gg - ins[2][...]
        dy = err / D
        t = dy * gg
        dh = r * (t - xn * jnp.mean(t * xn, axis=-1, keepdims=True))
        outs[0][...] = dh
        outs[1][...] = dh.astype(BF16)
        accs[0][...] += _colsum(dy * xn)
        accs[1][...] += _colsum(err * err)

    return _rowwise(name, fn, [(h, "row"), (g, "vec"), (tgt, "row")], [(D, F32), (D, BF16)],
                    accs=[(1, D), (1, D)])


def _ple_elem_bwd(name, dh, q, e):
    D = dh.shape[1]

    def fn(i, n, ins, outs, accs, scr):
        d = ins[0][...]
        s = _sigmoid(ins[1][...].astype(F32))
        ee = ins[2][...].astype(F32)
        outs[0][...] = (d * ee * s * (1.0 - s)).astype(BF16)
        outs[1][...] = (d * s).astype(BF16)

    return _rowwise(name, fn, [(dh, "row"), (q, "row"), (e, "row")], [(D, BF16), (D, BF16)], tt=LIGHT_ROW_TILE)


def _cf_norm_bwd(name, v2, g, dv4):
    D = v2.shape[1]

    def fn(i, n, ins, outs, accs, scr):
        x = ins[0][...]
        gg = ins[1][...]
        r = _rms_r(x)
        xn = x * r
        v3 = xn * gg
        s = _sigmoid(v3)
        dv3 = ins[2][...].astype(F32) * (s * (1.0 + v3 * (1.0 - s)))
        t = dv3 * gg
        dv2 = r * (t - xn * jnp.mean(t * xn, axis=-1, keepdims=True))
        outs[0][...] = dv2
        accs[0][...] += _colsum(dv3 * xn)
        accs[1][...] += _colsum(dv2)

    return _rowwise(name, fn, [(v2, "row"), (g, "vec"), (dv4, "row")], [(D, F32)], accs=[(1, D), (1, D)])


def _chunks(tt, width):
    cc = min(CONV_LANE_CHUNK, width)
    rc = min(CONV_ROW_CHUNK, tt)
    for c0 in range(0, width, cc):
        for r0 in range(0, tt, rc):
            yield r0, rc, c0, cc


def _n_shifts(n_taps):
    return min(SUBLANES - 1, n_taps - 1)


def _shifted_scratch(n_taps, rows, width):
    return pltpu.VMEM((_n_shifts(n_taps), rows, width), F32)


def _shift_window(win_ref, sh_ref, n_taps, sign):
    rows = win_ref.shape[0] - SUBLANES
    width = win_ref.shape[1]
    cc = min(CONV_LANE_CHUNK, width)
    for b in range(1, _n_shifts(n_taps) + 1):
        off = SUBLANES - b if sign < 0 else b
        for c0 in range(0, width, cc):
            sh_ref[b - 1, 0:rows, c0:c0 + cc] = win_ref[off:off + rows, c0:c0 + cc]


def _tap(win_ref, sh_ref, base, sign, s, r0, rc, c0, cc):
    a, b = divmod(s, SUBLANES)
    if b == 0:
        row = base + r0 + sign * SUBLANES * a
        return win_ref[row:row + rc, c0:c0 + cc]
    row = base + r0 - SUBLANES * (a + 1) if sign < 0 else base + r0 + SUBLANES * a
    return sh_ref[b - 1, row:row + rc, c0:c0 + cc]


def _fir(win_ref, sh_ref, w_ref, n_taps, base, sign, tt, width, emit):
    for r0, rc, c0, cc in _chunks(tt, width):
        acc = jnp.zeros((rc, cc), F32)
        for k in range(n_taps):
            acc = acc + w_ref[k:k + 1, c0:c0 + cc] * _tap(win_ref, sh_ref, base, sign, n_taps - 1 - k, r0, rc, c0, cc)
        emit(r0, rc, c0, cc, acc)


def _fir_wgrad(d_ref, win_ref, sh_ref, dw8_ref, n_taps, pad, tt, width):
    for c0 in range(0, width, min(CONV_LANE_CHUNK, width)):
        cc = min(CONV_LANE_CHUNK, width)
        rc = min(CONV_ROW_CHUNK, tt)
        for k in range(n_taps):
            acc = jnp.zeros((SUBLANES, cc), F32)
            for r0 in range(0, tt, rc):
                prod = d_ref[r0:r0 + rc, c0:c0 + cc] * _tap(win_ref, sh_ref, pad, -1, n_taps - 1 - k, r0, rc, c0, cc)
                for q in range(0, rc, SUBLANES):
                    acc = acc + prod[q:q + SUBLANES]
            dw8_ref[SUBLANES * k:SUBLANES * (k + 1), c0:c0 + cc] += acc


def _glu(blk, D):
    return blk[:, :D].astype(F32) * _sigmoid(blk[:, D:].astype(F32))


CF_PAD = 32
SC_PAD = 16


def _cf_conv_fwd(name, a, w_dw, b_dw, g_cf):
    T, D2 = a.shape
    D = D2 // 2
    K = w_dw.shape[0]
    tt = _tile(T, CONV_FWD_TILE_ROWS)

    def fn(i, n, ins, outs, accs, scr):
        a_ref, prev_ref, w_ref, b_ref, g_ref = ins
        win_ref, v2_ref, sh_ref = scr
        win_ref[0:CF_PAD, :] = jnp.where(i > 0, _glu(prev_ref[...], D), 0.0)
        win_ref[CF_PAD:CF_PAD + tt, :] = _glu(a_ref[...], D)
        _shift_window(win_ref, sh_ref, K, -1)

        def emit(r0, rc, c0, cc, acc):
            v2_ref[r0:r0 + rc, c0:c0 + cc] = acc + b_ref[:, c0:c0 + cc]

        _fir(win_ref, sh_ref, w_ref, K, CF_PAD, -1, tt, D, emit)
        v2 = v2_ref[...]
        v3 = v2 * _rms_r(v2) * g_ref[...]
        outs[0][...] = v2
        outs[1][...] = (v3 * _sigmoid(v3)).astype(BF16)

    return _rowwise(name, fn, [(a, "row"), (a, ("prev", CF_PAD)), (w_dw, "vec"), (b_dw, "vec"), (g_cf, "vec")],
                    [(D, F32), (D, BF16)],
                    scratch=[pltpu.VMEM((CF_PAD + tt, D), F32), pltpu.VMEM((tt, D), F32),
                             _shifted_scratch(K, CF_PAD + tt, D)], tt=tt)


def _cf_conv_bwd(name, dv2, a, w_dw):
    T, D2 = a.shape
    D = D2 // 2
    K = w_dw.shape[0]
    tt = _tile(T, CONV_TILE_ROWS)

    def fn(i, n, ins, outs, accs, scr):
        d_ref, dnext_ref, a_ref, prev_ref, w_ref = ins
        v1win_ref, dwin_ref, dv1_ref, dw8_ref, v1sh_ref, dsh_ref = scr

        @pl.when(i == 0)
        def _():
            dw8_ref[...] = jnp.zeros_like(dw8_ref)

        v1win_ref[0:CF_PAD, :] = jnp.where(i > 0, _glu(prev_ref[...], D), 0.0)
        v1win_ref[CF_PAD:CF_PAD + tt, :] = _glu(a_ref[...], D)
        dwin_ref[0:tt, :] = d_ref[...]
        dwin_ref[tt:tt + CF_PAD, :] = jnp.where(i < n - 1, dnext_ref[...], 0.0)
        _shift_window(v1win_ref, v1sh_ref, K, -1)
        _shift_window(dwin_ref, dsh_ref, K, 1)

        def emit(r0, rc, c0, cc, acc):
            dv1_ref[r0:r0 + rc, c0:c0 + cc] = acc

        _fir(dwin_ref, dsh_ref, w_ref, K, 0, 1, tt, D, emit)
        _fir_wgrad(d_ref, v1win_ref, v1sh_ref, dw8_ref, K, CF_PAD, tt, D)

        blk = a_ref[...]
        val = blk[:, :D].astype(F32)
        sg = _sigmoid(blk[:, D:].astype(F32))
        dv1 = dv1_ref[...]
        dval = dv1 * sg
        dgate = dv1 * val * sg * (1.0 - sg)
        outs[0][:, :D] = dval.astype(BF16)
        outs[0][:, D:] = dgate.astype(BF16)
        accs[1][:, :D] += _colsum(dval)
        accs[1][:, D:] += _colsum(dgate)

        @pl.when(i == n - 1)
        def _():
            for k in range(K):
                accs[0][k:k + 1, :] = _colsum(dw8_ref[SUBLANES * k:SUBLANES * (k + 1), :])

    return _rowwise(name, fn, [(dv2, "row"), (dv2, ("next", CF_PAD)), (a, "row"), (a, ("prev", CF_PAD)),
                               (w_dw, "vec")],
                    [(D2, BF16)], accs=[(K, D), (1, D2)],
                    scratch=[pltpu.VMEM((CF_PAD + tt, D), F32), pltpu.VMEM((tt + CF_PAD, D), F32),
                             pltpu.VMEM((tt, D), F32), pltpu.VMEM((SUBLANES * K, D), F32),
                             _shifted_scratch(K, CF_PAD + tt, D), _shifted_scratch(K, tt + CF_PAD, D)], tt=tt)


def _sc_conv_fwd(name, bcv, w_conv):
    T, D3 = bcv.shape
    D = D3 // 3
    K = w_conv.shape[0]
    tt = _tile(T, CONV_TILE_ROWS)

    def cv_of(blk):
        return blk[:, D:2 * D].astype(F32) * blk[:, 2 * D:].astype(F32)

    def fn(i, n, ins, outs, accs, scr):
        x_ref, prev_ref, w_ref = ins
        win_ref, cc_ref, sh_ref = scr
        win_ref[0:SC_PAD, :] = jnp.where(i > 0, cv_of(prev_ref[...]), 0.0)
        win_ref[SC_PAD:SC_PAD + tt, :] = cv_of(x_ref[...])
        _shift_window(win_ref, sh_ref, K, -1)

        def emit(r0, rc, c0, cw, acc):
            cc_ref[r0:r0 + rc, c0:c0 + cw] = acc

        _fir(win_ref, sh_ref, w_ref, K, SC_PAD, -1, tt, D, emit)
        outs[0][...] = (x_ref[:, :D].astype(F32) * cc_ref[...]).astype(BF16)

    return _rowwise(name, fn, [(bcv, "row"), (bcv, ("prev", SC_PAD)), (w_conv, "vec")], [(D, BF16)],
                    scratch=[pltpu.VMEM((SC_PAD + tt, D), F32), pltpu.VMEM((tt, D), F32),
                             _shifted_scratch(K, SC_PAD + tt, D)], tt=tt)


def _sc_conv_bwd(name, dy, bcv, w_conv):
    T, D3 = bcv.shape
    D = D3 // 3
    K = w_conv.shape[0]
    tt = _tile(T, CONV_TILE_ROWS)

    def cv_of(blk):
        return blk[:, D:2 * D].astype(F32) * blk[:, 2 * D:].astype(F32)

    def fn(i, n, ins, outs, accs, scr):
        dy_ref, dynext_ref, x_ref, prev_ref, next_ref, w_ref = ins
        cvwin_ref, dccwin_ref, tmp_ref, dw8_ref, cvsh_ref, dccsh_ref = scr

        @pl.when(i == 0)
        def _():
            dw8_ref[...] = jnp.zeros_like(dw8_ref)

        cvwin_ref[0:SC_PAD, :] = jnp.where(i > 0, cv_of(prev_ref[...]), 0.0)
        cvwin_ref[SC_PAD:SC_PAD + tt, :] = cv_of(x_ref[...])
        _shift_window(cvwin_ref, cvsh_ref, K, -1)

        def emit_cc(r0, rc, c0, cw, acc):
            tmp_ref[r0:r0 + rc, c0:c0 + cw] = acc

        _fir(cvwin_ref, cvsh_ref, w_ref, K, SC_PAD, -1, tt, D, emit_cc)
        dy_v = dy_ref[...].astype(F32)
        outs[0][:, :D] = (dy_v * tmp_ref[...]).astype(BF16)
        dccwin_ref[0:tt, :] = dy_v * x_ref[:, :D].astype(F32)
        dccwin_ref[tt:tt + SC_PAD, :] = jnp.where(
            i < n - 1, dynext_ref[...].astype(F32) * next_ref[:, :D].astype(F32), 0.0)

        _shift_window(dccwin_ref, dccsh_ref, K, 1)

        def emit_dcv(r0, rc, c0, cw, acc):
            tmp_ref[r0:r0 + rc, c0:c0 + cw] = acc

        _fir(dccwin_ref, dccsh_ref, w_ref, K, 0, 1, tt, D, emit_dcv)
        _fir_wgrad(dccwin_ref, cvwin_ref, cvsh_ref, dw8_ref, K, SC_PAD, tt, D)
        dcv = tmp_ref[...]
        outs[0][:, D:2 * D] = (dcv * x_ref[:, 2 * D:].astype(F32)).astype(BF16)
        outs[0][:, 2 * D:] = (dcv * x_ref[:, D:2 * D].astype(F32)).astype(BF16)

        @pl.when(i == n - 1)
        def _():
            for k in range(K):
                accs[0][k:k + 1, :] = _colsum(dw8_ref[SUBLANES * k:SUBLANES * (k + 1), :])

    return _rowwise(name, fn, [(dy, "row"), (dy, ("next", SC_PAD)), (bcv, "row"), (bcv, ("prev", SC_PAD)),
                               (bcv, ("next", SC_PAD)), (w_conv, "vec")],
                    [(D3, BF16)], accs=[(K, D)],
                    scratch=[pltpu.VMEM((SC_PAD + tt, D), F32), pltpu.VMEM((tt + SC_PAD, D), F32),
                             pltpu.VMEM((tt, D), F32), pltpu.VMEM((SUBLANES * K, D), F32),
                             _shifted_scratch(K, SC_PAD + tt, D), _shifted_scratch(K, tt + SC_PAD, D)], tt=tt)


def _place():
    x, y, c = lax.axis_index("x"), lax.axis_index("y"), lax.axis_index("c")
    chips = [(1 - x, y), (x, 1 - y), (1 - x, 1 - y)]
    return x, y, c, 2 * x + y, chips, (x, y, 1 - c)


def _half(rows, which):
    return pl.ds(pl.multiple_of(which * (rows // 2), SUBLANES), rows // 2)


_HBM = pl.BlockSpec(memory_space=pl.ANY)


def _gather_shards(name, items):
    n = len(items)
    shapes = [a.shape[-2:] for a, _ in items]

    def body(*refs):
        srcs, outs = refs[:n], refs[n:2 * n]
        send1, recv1, send2, recv2, lsem = refs[2 * n:]
        x, y, c, k, chips, sib = _place()

        def shard(i):
            return srcs[i] if items[i][1] is None else srcs[i].at[items[i][1]]

        started, locs = [], []
        for i in range(n):
            rows = shapes[i][0]
            lc = pltpu.make_async_copy(shard(i), outs[i].at[k], lsem.at[i])
            lc.start()
            locs.append(lc)
            for j, (cx, cy) in enumerate(chips):
                cp = pltpu.make_async_remote_copy(
                    src_ref=shard(i).at[_half(rows, c)], dst_ref=outs[i].at[k, _half(rows, c)],
                    send_sem=send1.at[i, j], recv_sem=recv1.at[i, j], device_id=(cx, cy, c), device_id_type=MESH)
                cp.start()
                started.append(cp)
        for i in range(n):
            rows = shapes[i][0]
            for j, (cx, cy) in enumerate(chips):
                blk = outs[i].at[2 * cx + cy, _half(rows, c)]
                pltpu.make_async_remote_copy(
                    src_ref=blk, dst_ref=blk, send_sem=send1.at[i, j], recv_sem=recv1.at[i, j],
                    device_id=(cx, cy, c), device_id_type=MESH).wait_recv()
                fw = pltpu.make_async_remote_copy(
                    src_ref=blk, dst_ref=blk, send_sem=send2.at[i, j], recv_sem=recv2.at[i, j],
                    device_id=sib, device_id_type=MESH)
                fw.start()
                started.append(fw)
        for i in range(n):
            rows = shapes[i][0]
            for j, (cx, cy) in enumerate(chips):
                blk = outs[i].at[2 * cx + cy, _half(rows, 1 - c)]
                pltpu.make_async_remote_copy(
                    src_ref=blk, dst_ref=blk, send_sem=send2.at[i, j], recv_sem=recv2.at[i, j],
                    device_id=sib, device_id_type=MESH).wait_recv()
        for cp in started:
            cp.wait_send()
        for lc in locs:
            lc.wait()

    return pl.pallas_call(
        body, name=name,
        in_specs=[_HBM] * n, out_specs=[_HBM] * n,
        out_shape=[jax.ShapeDtypeStruct((N_SHARDS,) + tuple(s), a.dtype) for s, (a, _) in zip(shapes, items)],
        scratch_shapes=[pltpu.SemaphoreType.DMA((n, 3))] * 4 + [pltpu.SemaphoreType.DMA((n,))],
    )(*[a for a, _ in items])


def _cast_place(name, w, layer, pos):
    _, R, C = w.shape
    tr = _tile(R, 256)

    def body(x_ref, y_ref, c_ref, w_ref, o_ref):
        o_ref[...] = w_ref[...].astype(BF16)

    return pl.pallas_call(
        body, name=name,
        grid_spec=pltpu.PrefetchScalarGridSpec(
            num_scalar_prefetch=3, grid=(R // tr,),
            in_specs=[pl.BlockSpec((None, tr, C), lambda r, xr, yr, cr: (layer, r, 0))],
            out_specs=pl.BlockSpec((None, tr, C), lambda r, xr, yr, cr: (2 * xr[0] + yr[0], r, 0))),
        out_shape=jax.ShapeDtypeStruct((N_SHARDS, R, C), BF16),
        compiler_params=_cparams("parallel"),
    )(*pos, w)


_IN_HBM = pl.BlockSpec(memory_space=pltpu.HBM)
_SEM = pl.BlockSpec(memory_space=pltpu.SEMAPHORE)
_SPLIT_COPY_PARAMS = pltpu.CompilerParams(has_side_effects=pltpu.SideEffectType.DATAFLOW_SIDE_EFFECTING)


def _in_hbm(a):
    return pltpu.with_memory_space_constraint(a, pltpu.HBM)


def _gather_copy(ref, i, j, chip_xy, c, k_src, rows, send, recv):
    blk = ref.at[k_src, _half(rows, c)]
    return pltpu.make_async_remote_copy(
        src_ref=blk, dst_ref=blk, send_sem=send.at[3 * i + j], recv_sem=recv.at[3 * i + j],
        device_id=(*chip_xy, c), device_id_type=MESH)


def _gather_start(name, bufs, after):
    n = len(bufs)

    def body(*refs):
        ins = refs[:n]
        send, recv = refs[n + 1], refs[n + 2]
        token = refs[-1]
        x, y, c, k, chips, sib = _place()
        for i in range(n):
            for j, chip_xy in enumerate(chips):
                _gather_copy(ins[i], i, j, chip_xy, c, k, bufs[i].shape[1], send, recv).start()
        token[...] = jnp.zeros_like(token)

    res = pl.pallas_call(
        body, name=name,
        in_specs=[_IN_HBM] * n + [_HBM],
        out_specs=[_SEM, _SEM] + [_IN_HBM] * n + [pl.BlockSpec(memory_space=pltpu.VMEM)],
        out_shape=[pltpu.SemaphoreType.DMA((3 * n,)), pltpu.SemaphoreType.DMA((3 * n,))]
        + [pltpu.HBM(b.shape, b.dtype) for b in bufs] + [jax.ShapeDtypeStruct((SUBLANES, 128), F32)],
        input_output_aliases={i: 2 + i for i in range(n)},
        compiler_params=_SPLIT_COPY_PARAMS,
    )(*[_in_hbm(b) for b in bufs], after)
    return res[0], res[1], list(res[2:2 + n]), res[-1]


def _gather_wait(name, bufs, send, recv, afters):
    n = len(bufs)

    def body(*refs):
        ins = refs[:n]
        send_ref, recv_ref = refs[n], refs[n + 1]
        x, y, c, k, chips, sib = _place()
        for i in range(n):
            for j, chip_xy in enumerate(chips):
                rows = bufs[i].shape[1]
                _gather_copy(ins[i], i, j, chip_xy, c, k, rows, send_ref, recv_ref).wait_send()
                _gather_copy(ins[i], i, j, chip_xy, c, 2 * chip_xy[0] + chip_xy[1], rows, send_ref, recv_ref).wait_recv()

    return pl.pallas_call(
        body, name=name,
        in_specs=[_IN_HBM] * n + [_SEM, _SEM] + [_HBM] * len(afters),
        out_specs=[_IN_HBM] * n,
        out_shape=[pltpu.HBM(b.shape, b.dtype) for b in bufs],
        input_output_aliases={i: i for i in range(n)},
        compiler_params=_SPLIT_COPY_PARAMS,
    )(*bufs, send, recv, *afters)


def _gather_forward(name, bufs):
    n = len(bufs)

    def body(*refs):
        outs = refs[n:2 * n]
        send, recv = refs[2 * n:]
        x, y, c, k, chips, sib = _place()
        started = []
        for i in range(n):
            rows = bufs[i].shape[1]
            for j, (cx, cy) in enumerate(chips):
                blk = outs[i].at[2 * cx + cy, _half(rows, c)]
                fw = pltpu.make_async_remote_copy(
                    src_ref=blk, dst_ref=blk, send_sem=send.at[i, j], recv_sem=recv.at[i, j],
                    device_id=sib, device_id_type=MESH)
                fw.start()
                started.append(fw)
        for i in range(n):
            rows = bufs[i].shape[1]
            for j, (cx, cy) in enumerate(chips):
                blk = outs[i].at[2 * cx + cy, _half(rows, 1 - c)]
                pltpu.make_async_remote_copy(
                    src_ref=blk, dst_ref=blk, send_sem=send.at[i, j], recv_sem=recv.at[i, j],
                    device_id=sib, device_id_type=MESH).wait_recv()
        for cp in started:
            cp.wait_send()

    return pl.pallas_call(
        body, name=name, in_specs=[_HBM] * n, out_specs=[_HBM] * n,
        out_shape=[jax.ShapeDtypeStruct(b.shape, b.dtype) for b in bufs],
        input_output_aliases={i: i for i in range(n)},
        scratch_shapes=[pltpu.SemaphoreType.DMA((n, 3))] * 2,
    )(*bufs)


def _swap_halves(name, parts):
    n = len(parts)

    def body(*refs):
        srcs, outs = refs[:n], refs[n:2 * n]
        send, recv = refs[2 * n:]
        x, y, c, k, chips, sib = _place()
        cps = []
        for i in range(n):
            rows = parts[i].shape[1]
            cp = pltpu.make_async_remote_copy(
                src_ref=srcs[i].at[:, _half(rows, 1 - c)], dst_ref=outs[i],
                send_sem=send.at[i], recv_sem=recv.at[i], device_id=sib, device_id_type=MESH)
            cp.start()
            cps.append(cp)
        for cp in cps:
            cp.wait()

    return pl.pallas_call(
        body, name=name, in_specs=[_HBM] * n, out_specs=[_HBM] * n,
        out_shape=[jax.ShapeDtypeStruct((p.shape[0], p.shape[1] // 2, p.shape[2]), p.dtype) for p in parts],
        scratch_shapes=[pltpu.SemaphoreType.DMA((n,))] * 2,
    )(*parts)


def _sibling_start(name, bufs, n_copies, make):
    nb = len(bufs)

    def body(*refs):
        send, recv = refs[nb], refs[nb + 1]
        token = refs[-1]
        x, y, c, k, chips, sib = _place()
        for cp in make(refs[:nb], c, sib, send, recv):
            cp.start()
        token[...] = jnp.zeros_like(token)

    res = pl.pallas_call(
        body, name=name,
        in_specs=[_IN_HBM] * nb,
        out_specs=[_SEM, _SEM] + [_IN_HBM] * nb + [pl.BlockSpec(memory_space=pltpu.VMEM)],
        out_shape=[pltpu.SemaphoreType.DMA((n_copies,)), pltpu.SemaphoreType.DMA((n_copies,))]
        + [pltpu.HBM(b.shape, b.dtype) for b in bufs] + [jax.ShapeDtypeStruct((SUBLANES, 128), F32)],
        input_output_aliases={i: 2 + i for i in range(nb)},
        compiler_params=_SPLIT_COPY_PARAMS,
    )(*[_in_hbm(b) for b in bufs])
    return res[0], res[1], list(res[2:2 + nb]), res[-1]


def _sibling_wait(name, bufs, send, recv, make, afters):
    nb = len(bufs)

    def body(*refs):
        x, y, c, k, chips, sib = _place()
        for cp in make(refs[:nb], c, sib, refs[nb], refs[nb + 1]):
            cp.wait_send()
            cp.wait_recv()

    return list(pl.pallas_call(
        body, name=name,
        in_specs=[_IN_HBM] * nb + [_SEM, _SEM] + [_HBM] * len(afters),
        out_specs=[_IN_HBM] * nb,
        out_shape=[pltpu.HBM(b.shape, b.dtype) for b in bufs],
        input_output_aliases={i: i for i in range(nb)},
        compiler_params=_SPLIT_COPY_PARAMS,
    )(*bufs, send, recv, *afters))


def _swap_copies(parts):
    n = len(parts)

    def make(refs, c, sib, send, recv):
        return [pltpu.make_async_remote_copy(
            src_ref=refs[i].at[:, _half(parts[i].shape[1], 1 - c)], dst_ref=refs[n + i],
            send_sem=send.at[i], recv_sem=recv.at[i], device_id=sib, device_id_type=MESH) for i in range(n)]

    return make


def _join_copies(fulls):
    def make(refs, c, sib, send, recv):
        cps = []
        for i, f in enumerate(fulls):
            blk = refs[i].at[_half(f.shape[0], c)]
            cps.append(pltpu.make_async_remote_copy(
                src_ref=blk, dst_ref=blk, send_sem=send.at[i], recv_sem=recv.at[i],
                device_id=sib, device_id_type=MESH))
        return cps

    return make


def _scatter_copy(src_ref, land_ref, i, j, chip_xy, c, send, recv):
    return pltpu.make_async_remote_copy(
        src_ref=src_ref.at[2 * chip_xy[0] + chip_xy[1]], dst_ref=land_ref.at[j],
        send_sem=send.at[3 * i + j], recv_sem=recv.at[3 * i + j], device_id=(*chip_xy, c), device_id_type=MESH)


def _scatter_start(name, sums):
    n = len(sums)
    lands = [lax.empty((3,) + s.shape[1:], s.dtype) for s in sums]

    def body(*refs):
        srcs, lnds = refs[:n], refs[n:2 * n]
        send, recv = refs[2 * n], refs[2 * n + 1]
        token = refs[-1]
        x, y, c, k, chips, sib = _place()
        for i in range(n):
            for j, chip_xy in enumerate(chips):
                _scatter_copy(srcs[i], lnds[i], i, j, chip_xy, c, send, recv).start()
        token[...] = jnp.zeros_like(token)

    res = pl.pallas_call(
        body, name=name,
        in_specs=[_IN_HBM] * (2 * n),
        out_specs=[_SEM, _SEM] + [_IN_HBM] * (2 * n) + [pl.BlockSpec(memory_space=pltpu.VMEM)],
        out_shape=[pltpu.SemaphoreType.DMA((3 * n,)), pltpu.SemaphoreType.DMA((3 * n,))]
        + [pltpu.HBM(a.shape, a.dtype) for a in list(sums) + lands] + [jax.ShapeDtypeStruct((SUBLANES, 128), F32)],
        input_output_aliases={i: 2 + i for i in range(2 * n)},
        compiler_params=_SPLIT_COPY_PARAMS,
    )(*[_in_hbm(a) for a in list(sums) + lands])
    return res[0], res[1], list(res[2:2 + n]), list(res[2 + n:2 + 2 * n]), res[-1]


def _scatter_wait(name, sums, lands, send, recv, afters):
    n = len(sums)

    def body(*refs):
        srcs, lnds = refs[:n], refs[n:2 * n]
        send_ref, recv_ref = refs[2 * n], refs[2 * n + 1]
        x, y, c, k, chips, sib = _place()
        for i in range(n):
            for j, chip_xy in enumerate(chips):
                cp = _scatter_copy(srcs[i], lnds[i], i, j, chip_xy, c, send_ref, recv_ref)
                cp.wait_send()
                cp.wait_recv()

    res = pl.pallas_call(
        body, name=name,
        in_specs=[_IN_HBM] * (2 * n) + [_SEM, _SEM] + [_HBM] * len(afters),
        out_specs=[_IN_HBM] * (2 * n),
        out_shape=[pltpu.HBM(a.shape, a.dtype) for a in list(sums) + list(lands)],
        input_output_aliases={i: i for i in range(2 * n)},
        compiler_params=_SPLIT_COPY_PARAMS,
    )(*sums, *lands, send, recv, *afters)
    return list(res[:n]), list(res[n:])


def _sum_over_devices(name, buf, loss_row, afters):
    R, D = buf.shape
    n_after = len(afters)

    def body(x_ref, *rest):
        all_ref, tot_ref, loss_ref, send_sems, recv_sems, local_sem = rest[n_after:]
        x, y, c, k, chips, sib = _place()
        me = (x, y, c)

        def block(px, py, pc):
            return all_ref.at[4 * px + 2 * py + pc]

        def copy(kk, blk, to, src=None):
            return pltpu.make_async_remote_copy(
                src_ref=block(*blk) if src is None else src, dst_ref=block(*blk),
                send_sem=send_sems.at[kk], recv_sem=recv_sems.at[kk], device_id=to, device_id_type=MESH)

        mine = pltpu.make_async_copy(x_ref, block(*me), local_sem)
        mine.start()
        first = [copy(0, me, sib, src=x_ref)]
        first += [copy(1 + j, me, (*chip, c), src=x_ref) for j, chip in enumerate(chips)]
        for cp in first:
            cp.start()
        passed = [copy(4 + j, (*chip, c), sib) for j, chip in enumerate(chips)]
        for j, chip in enumerate(chips):
            copy(1 + j, (*chip, c), me).wait_recv()
            passed[j].start()
        copy(0, sib, me).wait_recv()
        for j, chip in enumerate(chips):
            copy(4 + j, (*chip, 1 - c), me).wait_recv()
        for cp in first + passed:
            cp.wait_send()
        mine.wait()
        rc = _tile(R, 32)
        for r0 in range(0, R, rc):
            tot = all_ref[0, r0:r0 + rc, :]
            for d in range(1, N_DEVICES):
                tot = tot + all_ref[d, r0:r0 + rc, :]
            tot_ref[r0:r0 + rc, :] = tot
        loss = 0.5 * jnp.sum(tot_ref[loss_row:loss_row + 1, :]) / D
        loss_ref[...] = jnp.full(loss_ref.shape, loss, F32)

    vm = pl.BlockSpec(memory_space=pltpu.VMEM)
    return pl.pallas_call(
        body, name=name, in_specs=[vm] + [_HBM] * n_after, out_specs=[vm, vm, vm],
        out_shape=[jax.ShapeDtypeStruct((N_DEVICES, R, D), F32), jax.ShapeDtypeStruct((R, D), F32),
                   jax.ShapeDtypeStruct((SUBLANES, 128), F32)],
        scratch_shapes=[pltpu.SemaphoreType.DMA((7,)), pltpu.SemaphoreType.DMA((7,)), pltpu.SemaphoreType.DMA],
        compiler_params=pltpu.CompilerParams(vmem_limit_bytes=V7X_VMEM_LIMIT_BYTES),
    )(buf, *afters)[1:]


def _add_my_half(name, part, got, pos):
    S, R, C = part.shape
    R2 = R // 2
    tr = _tile(R2, 1024)
    q = R2 // tr

    def body(x_ref, y_ref, c_ref, p_ref, g_ref, o_ref):
        o_ref[...] = (p_ref[...].astype(F32) + g_ref[...].astype(F32)).astype(o_ref.dtype)

    return pl.pallas_call(
        body, name=name,
        grid_spec=pltpu.PrefetchScalarGridSpec(
            num_scalar_prefetch=3, grid=(S, q),
            in_specs=[pl.BlockSpec((None, tr, C), lambda s, r, xr, yr, cr: (s, cr[0] * q + r, 0)),
                      pl.BlockSpec((None, tr, C), lambda s, r, xr, yr, cr: (s, r, 0))],
            out_specs=pl.BlockSpec((None, tr, C), lambda s, r, xr, yr, cr: (s, r, 0))),
        out_shape=jax.ShapeDtypeStruct((S, R2, C), BF16),
        compiler_params=_cparams("parallel", "parallel"),
    )(*pos, part, got)


def _add_owner(name, sums, got, pos):
    _, R2, C = sums.shape
    tr = _tile(R2, 512)
    q = R2 // tr

    def body(x_ref, y_ref, c_ref, s_ref, g_ref, o_ref):
        acc = s_ref[...].astype(F32)
        for j in range(3):
            acc = acc + g_ref[j].astype(F32)
        o_ref[...] = acc

    return pl.pallas_call(
        body, name=name,
        grid_spec=pltpu.PrefetchScalarGridSpec(
            num_scalar_prefetch=3, grid=(q,),
            in_specs=[pl.BlockSpec((None, tr, C), lambda r, xr, yr, cr: (2 * xr[0] + yr[0], r, 0)),
                      pl.BlockSpec((3, tr, C), lambda r, xr, yr, cr: (0, r, 0))],
            out_specs=pl.BlockSpec((tr, C), lambda r, xr, yr, cr: (cr[0] * q + r, 0))),
        out_shape=jax.ShapeDtypeStruct((2 * R2, C), F32),
        compiler_params=_cparams("parallel"),
    )(*pos, sums, got)


def _adamw(name, w, m, v, g):
    R, C = w.shape
    tr = SUBLANES
    while 2 * tr * C <= ADAMW_TILE_ELEMS:
        tr *= 2
    tr = _tile(R, tr)
    bc1 = 1.0 - ADAM_B1 ** ADAM_STEP
    bc2 = 1.0 - ADAM_B2 ** ADAM_STEP

    def body(w_ref, m_ref, v_ref, g_ref, go_ref, d_ref, mo_ref, vo_ref):
        gg = g_ref[...]
        m2 = ADAM_B1 * m_ref[...] + (1.0 - ADAM_B1) * gg
        v2 = ADAM_B2 * v_ref[...] + (1.0 - ADAM_B2) * (gg * gg)
        go_ref[...] = gg
        mo_ref[...] = m2
        vo_ref[...] = v2
        d_ref[...] = -ADAM_LR * ((m2 / bc1) / (jnp.sqrt(v2 / bc2) + ADAM_EPS) + ADAM_WD * w_ref[...])

    spec = pl.BlockSpec((tr, C), lambda r: (r, 0))
    return pl.pallas_call(
        body, name=name, grid=(R // tr,), in_specs=[spec] * 4, out_specs=[spec] * 4,
        out_shape=[jax.ShapeDtypeStruct((R, C), F32)] * 4,
        compiler_params=_cparams("parallel"),
    )(w, m, v, g)


def _adamw_slab(name, w, m, v, g, layer, prev):
    L, R, C = w.shape
    tr = SUBLANES
    while 2 * tr * C <= ADAMW_TILE_ELEMS:
        tr *= 2
    tr = _tile(R, tr)
    bc1 = 1.0 - ADAM_B1 ** ADAM_STEP
    bc2 = 1.0 - ADAM_B2 ** ADAM_STEP

    def body(w_ref, m_ref, v_ref, g_ref, *rest):
        go_ref, d_ref, mo_ref, vo_ref = rest[-4:]
        gg = g_ref[...]
        m2 = ADAM_B1 * m_ref[...] + (1.0 - ADAM_B1) * gg
        v2 = ADAM_B2 * v_ref[...] + (1.0 - ADAM_B2) * (gg * gg)
        go_ref[...] = gg
        mo_ref[...] = m2
        vo_ref[...] = v2
        d_ref[...] = -ADAM_LR * ((m2 / bc1) / (jnp.sqrt(v2 / bc2) + ADAM_EPS) + ADAM_WD * w_ref[...])

    slab = pl.BlockSpec((None, tr, C), lambda r: (layer, r, 0))
    n_prev = 0 if prev is None else 4
    return pl.pallas_call(
        body, name=name, grid=(R // tr,),
        in_specs=[slab] * 3 + [pl.BlockSpec((tr, C), lambda r: (r, 0))] + [_HBM] * n_prev,
        out_specs=[slab] * 4,
        out_shape=[jax.ShapeDtypeStruct((L, R, C), F32)] * 4,
        input_output_aliases={4 + i: i for i in range(n_prev)},
        compiler_params=_cparams("parallel"),
    )(w, m, v, g, *(prev or ()))


def _swap_begin(tag, parts):
    lands = [lax.empty((p.shape[0], p.shape[1] // 2, p.shape[2]), p.dtype) for p in parts]
    return _sibling_start(f"rs_swap_start_{tag}", list(parts) + lands, len(parts), _swap_copies(parts))


def _reduce_begin(tag, early, swapping, late, pos, after):
    send, recv, bufs, _ = swapping
    bufs = _sibling_wait(f"rs_swap_wait_{tag}", bufs, send, recv, _swap_copies(early), [after])
    parts = bufs[:len(early)] + list(late)
    got = bufs[len(early):] + list(_swap_halves(f"rs_swap_{tag}", late))
    sums = [_add_my_half(f"rs_add2_{tag}_{i}", p, g, pos) for i, (p, g) in enumerate(zip(parts, got))]
    return _scatter_start(f"rs_scatter_start_{tag}", sums)


def _reduce_middle(tag, started, pos, afters):
    send, recv, sums, lands, _ = started
    sums, lands = _scatter_wait(f"rs_scatter_wait_{tag}", sums, lands, send, recv, afters)
    fulls = [_add_owner(f"rs_add4_{tag}_{i}", s, q, pos) for i, (s, q) in enumerate(zip(sums, lands))]
    return _sibling_start(f"rs_join_start_{tag}", fulls, len(fulls), _join_copies(fulls))


def _reduce_end(tag, joining, afters):
    send, recv, fulls, _ = joining
    return _sibling_wait(f"rs_join_wait_{tag}", fulls, send, recv, _join_copies(fulls), afters)


def _pad_rows(a):
    r = (-a.shape[0]) % SUBLANES
    return jnp.pad(a, ((0, r), (0, 0))) if r else a


def kernel(x, p, norm_mix, norm_mlp, norm_ple, cf_w_pw1, cf_b_pw1, cf_w_dw, cf_b_dw, cf_norm, cf_w_pw2, cf_b_pw2, sc_w_in, sc_w_conv, sc_w_out, mlp_w1, mlp_w2, ple_w_proj, ple_w_gate, norm_final, loss_target, m_norm_mix, m_norm_mlp, m_norm_ple, m_cf_w_pw1, m_cf_b_pw1, m_cf_w_dw, m_cf_b_dw, m_cf_norm, m_cf_w_pw2, m_cf_b_pw2, m_sc_w_in, m_sc_w_conv, m_sc_w_out, m_mlp_w1, m_mlp_w2, m_ple_w_proj, m_ple_w_gate, m_norm_final, v_norm_mix, v_norm_mlp, v_norm_ple, v_cf_w_pw1, v_cf_b_pw1, v_cf_w_dw, v_cf_b_dw, v_cf_norm, v_cf_w_pw2, v_cf_b_pw2, v_sc_w_in, v_sc_w_conv, v_sc_w_out, v_mlp_w1, v_mlp_w2, v_ple_w_proj, v_ple_w_gate, v_norm_final):
    T, D = x.shape[1], x.shape[2]
    KA, KB = cf_w_dw.shape[1], sc_w_conv.shape[1]
    chip = (2 * lax.axis_index("x") + lax.axis_index("y")).astype(jnp.int32)
    pos = tuple(lax.axis_index(ax).astype(jnp.int32).reshape(1) for ax in ("x", "y", "c"))

    params = dict(norm_mix=norm_mix, norm_mlp=norm_mlp, norm_ple=norm_ple, cf_w_pw1=cf_w_pw1, cf_b_pw1=cf_b_pw1,
                  cf_w_dw=cf_w_dw, cf_b_dw=cf_b_dw, cf_norm=cf_norm, cf_w_pw2=cf_w_pw2, cf_b_pw2=cf_b_pw2,
                  sc_w_in=sc_w_in, sc_w_conv=sc_w_conv, sc_w_out=sc_w_out, mlp_w1=mlp_w1, mlp_w2=mlp_w2,
                  ple_w_proj=ple_w_proj, ple_w_gate=ple_w_gate, norm_final=norm_final)
    mom1 = dict(norm_mix=m_norm_mix, norm_mlp=m_norm_mlp, norm_ple=m_norm_ple, cf_w_pw1=m_cf_w_pw1,
                cf_b_pw1=m_cf_b_pw1, cf_w_dw=m_cf_w_dw, cf_b_dw=m_cf_b_dw, cf_norm=m_cf_norm, cf_w_pw2=m_cf_w_pw2,
                cf_b_pw2=m_cf_b_pw2, sc_w_in=m_sc_w_in, sc_w_conv=m_sc_w_conv, sc_w_out=m_sc_w_out,
                mlp_w1=m_mlp_w1, mlp_w2=m_mlp_w2, ple_w_proj=m_ple_w_proj, ple_w_gate=m_ple_w_gate,
                norm_final=m_norm_final)
    mom2 = dict(norm_mix=v_norm_mix, norm_mlp=v_norm_mlp, norm_ple=v_norm_ple, cf_w_pw1=v_cf_w_pw1,
                cf_b_pw1=v_cf_b_pw1, cf_w_dw=v_cf_w_dw, cf_b_dw=v_cf_b_dw, cf_norm=v_cf_norm, cf_w_pw2=v_cf_w_pw2,
                cf_b_pw2=v_cf_b_pw2, sc_w_in=v_sc_w_in, sc_w_conv=v_sc_w_conv, sc_w_out=v_sc_w_out,
                mlp_w1=v_mlp_w1, mlp_w2=v_mlp_w2, ple_w_proj=v_ple_w_proj, ple_w_gate=v_ple_w_gate,
                norm_final=v_norm_final)

    big = ("cf_w_pw1", "cf_w_pw2", "sc_w_in", "sc_w_out", "mlp_w1", "mlp_w2", "ple_w_proj", "ple_w_gate")
    row_sharded = ("cf_w_pw2", "sc_w_out", "mlp_w2", "ple_w_gate")

    def layer_names(i):
        return (["cf_w_pw1", "cf_w_pw2"] if i % 2 == 0 else ["sc_w_in", "sc_w_out"]) + \
            ["mlp_w1", "mlp_w2", "ple_w_proj", "ple_w_gate"]

    def layer_index(i, name):
        return i if name.startswith(("mlp", "ple")) else i // 2

    placed = {(i, nm): _cast_place(f"place_{nm}_{i}", params[nm], layer_index(i, nm), pos)
              for i in range(DEPTH) for nm in layer_names(i)}

    def gather_begin(tag, i, names, after):
        return names, _gather_start(f"gather_start_{tag}", [placed[i, nm] for nm in names], after)

    def gather_end(tag, begun, afters):
        names, (send, recv, bufs, _) = begun
        bufs = _gather_wait(f"gather_wait_{tag}", bufs, send, recv, afters)
        bufs = _gather_forward(f"gather_fwd_{tag}", bufs)
        return {nm: g4.reshape(1, N_SHARDS * g4.shape[1], g4.shape[2]) if nm in row_sharded else g4
                for nm, g4 in zip(names, bufs)}

    conv_small = jnp.concatenate([_pad_rows(cf_w_dw[j]) for j in range(cf_w_dw.shape[0])]
                                 + [_pad_rows(sc_w_conv[j]) for j in range(sc_w_conv.shape[0])], axis=0)
    conv_shards = _gather_shards("gather_conv_w", [(conv_small, None)])[0]
    conv_all = jnp.transpose(conv_shards, (1, 0, 2)).reshape(conv_small.shape[0], D)
    ka_pad = KA + (-KA) % SUBLANES
    kb_pad = KB + (-KB) % SUBLANES
    w_dw_full = [conv_all[j * ka_pad:j * ka_pad + KA] for j in range(cf_w_dw.shape[0])]
    off = cf_w_dw.shape[0] * ka_pad
    w_conv_full = [conv_all[off + j * kb_pad:off + j * kb_pad + KB] for j in range(sc_w_conv.shape[0])]

    def vec(a):
        return a.reshape(1, -1)

    ident = lambda acc: (acc,)

    h = x[0]
    saved = []
    first = gather_begin("0m", 0, layer_names(0)[:2], conv_shards)
    rest = gather_begin("0r", 0, layer_names(0)[2:], first[1][2][0])
    later = [placed[i, nm] for i in range(1, DEPTH) for nm in layer_names(i)]
    W = [gather_end("0m", first, [h] + later)]
    for i in range(DEPTH):
        j = i // 2
        wl = W[i]
        s = dict(h=h)
        g_mix = vec(norm_mix[i])
        if i + 1 < DEPTH:
            nxt = gather_begin(f"{i + 1}", i + 1, layer_names(i + 1), wl[layer_names(i)[0]])
            g_mix = g_mix + nxt[1][3][0, 0]
        s["u"] = _rms_fwd(f"rms_mix_{i}", h, g_mix)
        if i % 2 == 0:
            s["a"] = _mm_nn(f"cf_pw1_{i}", s["u"], wl["cf_w_pw1"], lambda acc, b: (acc + b,), [BF16],
                            extras=[(vec(cf_b_pw1[j]), "n")])[0]
            s["v2"], s["v4"] = _cf_conv_fwd(f"cf_conv_{i}", s["a"], w_dw_full[j], vec(cf_b_dw[j]), vec(cf_norm[j]))
            h1 = _mm_nn(f"cf_pw2_{i}", s["v4"], wl["cf_w_pw2"], lambda acc, b, r: (r + (acc + b),), [F32],
                        extras=[(vec(cf_b_pw2[j]), "n"), (h, "mn")])[0]
        else:
            s["bcv"] = _mm_nn(f"sc_in_{i}", s["u"], wl["sc_w_in"], ident, [BF16])[0]
            s["y"] = _sc_conv_fwd(f"sc_conv_{i}", s["bcv"], w_conv_full[j])[0]
            h1 = _mm_nn(f"sc_out_{i}", s["y"], wl["sc_w_out"], lambda acc, r: (r + acc,), [F32],
                        extras=[(h, "mn")])[0]
        s["h1"] = h1
        if i == 0:
            wl.update(gather_end("0r", rest, [h1]))
        s["u2"] = _rms_fwd(f"rms_mlp_{i}", h1, vec(norm_mlp[i]))
        s["z"], s["hd"] = _mm_nn(f"mlp_w1_{i}", s["u2"], wl["mlp_w1"],
                                 lambda acc: (acc, jnp.square(jnp.maximum(acc, 0.0))), [BF16, BF16])
        h2 = _mm_nn(f"mlp_w2_{i}", s["hd"], wl["mlp_w2"], lambda acc, r: (r + acc,), [F32], extras=[(h1, "mn")])[0]
        s["h2"] = h2
        s["n3"] = _rms_fwd(f"rms_ple_{i}", h2, vec(norm_ple[i]))
        s["p"] = p[i, 0]
        s["e"] = _mm_nn(f"ple_proj_{i}", s["p"], wl["ple_w_proj"], ident, [BF16])[0]
        h, s["q"] = _mm_nn(f"ple_gate_{i}", s["n3"], wl["ple_w_gate"],
                           lambda acc, r, e: (r + _sigmoid(acc) * e.astype(F32), acc), [F32, BF16],
                           extras=[(h2, "mn"), (s["e"], "mn")])
        saved.append(s)
        if i + 1 < DEPTH:
            W.append(gather_end(f"{i + 1}", nxt, [h]))

    dh, dh16, dg_final, loss_cols = _loss_bwd("loss_bwd", h, vec(norm_final), loss_target[0])
    small = {"norm_final": dg_final, "loss": loss_cols}
    adam = {nm: None for nm in big}

    def reduce_names(i):
        names = layer_names(i)
        return names[2:] + names[:2]

    def update_layer(i, joining, afters):
        for nm, g in zip(reduce_names(i), _reduce_end(f"{i}", joining, afters)):
            l = layer_index(i, nm)
            adam[nm] = _adamw_slab(f"adamw_{nm}_{l}", params[nm], mom1[nm], mom2[nm], g, l, adam[nm])

    def shard_major(names, parts):
        return [pt.reshape(N_SHARDS, pt.shape[1] // N_SHARDS, pt.shape[2]) if nm in row_sharded else pt
                for nm, pt in zip(names, parts)]

    scattered = joining = None
    for i in reversed(range(DEPTH)):
        j = i // 2
        wl, s = W[i], saved[i]
        dq, de = _ple_elem_bwd(f"ple_elem_bwd_{i}", dh, s["q"], s["e"])
        d_proj = _mm_tn(f"ple_proj_dw_{i}", s["p"], de, N_SHARDS)
        d_gate = _mm_tn(f"ple_gate_dw_{i}", s["n3"], dq, 1)
        dn3 = _mm_nt(f"ple_gate_dx_{i}", dq, wl["ple_w_gate"], ident, [F32])[0]
        g_ple = vec(norm_ple[i])
        if joining is not None:
            g_ple = g_ple + joining[3][0, 0]
        dh, dh16, small[f"norm_ple_{i}"] = _rms_bwd(f"rms_ple_bwd_{i}", s["h2"], g_ple, dn3, dh)

        d_w2 = _mm_tn(f"mlp_w2_dw_{i}", s["hd"], dh16, 1)
        dz = _mm_nt(f"mlp_w2_dx_{i}", dh16, wl["mlp_w2"],
                    lambda acc, z: (acc * (2.0 * jnp.maximum(z.astype(F32), 0.0)),), [BF16], extras=[s["z"]])[0]
        d_w1 = _mm_tn(f"mlp_w1_dw_{i}", s["u2"], dz, N_SHARDS)
        du2 = _mm_nt(f"mlp_w1_dx_{i}", dz, wl["mlp_w1"], ident, [F32])[0]
        early = shard_major(reduce_names(i)[:4], [d_w1, d_w2, d_proj, d_gate])
        swapping = _swap_begin(f"{i}", early)
        g_mlp = vec(norm_mlp[i]) + swapping[3][0, 0]
        if i % 2 == 0:
            dh, dh16, small[f"norm_mlp_{i}"], small[f"cf_b_pw2_{j}"] = _rms_bwd(
                f"rms_mlp_bwd_{i}", s["h1"], g_mlp, du2, dh, want_colsum=True)
            d_mix_out = _mm_tn(f"cf_pw2_dw_{i}", s["v4"], dh16, 1)
            dv4 = _mm_nt(f"cf_pw2_dx_{i}", dh16, wl["cf_w_pw2"], ident, [F32])[0]
            dv2, small[f"cf_norm_{j}"], small[f"cf_b_dw_{j}"] = _cf_norm_bwd(
                f"cf_norm_bwd_{i}", s["v2"], vec(cf_norm[j]), dv4)
            da, small[f"cf_w_dw_{j}"], db1 = _cf_conv_bwd(f"cf_conv_bwd_{i}", dv2, s["a"], w_dw_full[j])
            small[f"cf_b_pw1_{j}"] = db1.reshape(2, D)
            d_mix_in = _mm_tn(f"cf_pw1_dw_{i}", s["u"], da, N_SHARDS)
            du = _mm_nt(f"cf_pw1_dx_{i}", da, wl["cf_w_pw1"], ident, [F32])[0]
        else:
            dh, dh16, small[f"norm_mlp_{i}"] = _rms_bwd(f"rms_mlp_bwd_{i}", s["h1"], g_mlp, du2, dh)
            d_mix_out = _mm_tn(f"sc_out_dw_{i}", s["y"], dh16, 1)
            dy = _mm_nt(f"sc_out_dx_{i}", dh16, wl["sc_w_out"], ident, [F32])[0]
            da, small[f"sc_w_conv_{j}"] = _sc_conv_bwd(f"sc_conv_bwd_{i}", dy, s["bcv"], w_conv_full[j])
            d_mix_in = _mm_tn(f"sc_in_dw_{i}", s["u"], da, N_SHARDS)
            du = _mm_nt(f"sc_in_dx_{i}", da, wl["sc_w_in"], ident, [F32])[0]

        late = shard_major(reduce_names(i)[4:], [d_mix_in, d_mix_out])
        started = _reduce_begin(f"{i}", early, swapping, late, pos, du)
        token = started[4]
        dh, dh16, small[f"norm_mix_{i}"] = _rms_bwd(f"rms_mix_bwd_{i}", s["h"], vec(norm_mix[i]) + token[0, 0], du, dh)
        if joining is not None:
            update_layer(i + 2, joining, [token])
            joining = None
        if scattered is not None:
            joining = _reduce_middle(f"{i + 1}", scattered, pos, [token])
        scattered = started
    grad_x = dh.reshape(x.shape)

    order = sorted(small)
    pieces, where, row = [], {}, 0
    for nm in order:
        pc = _pad_rows(small[nm])
        where[nm] = (row, small[nm].shape[0])
        row += pc.shape[0]
        pieces.append(pc)
    update_layer(1, joining, [token])
    updated = [res[3] for res in adam.values() if res is not None]
    total, loss_tile = _sum_over_devices("small_allsum", jnp.concatenate(pieces, axis=0), where["loss"][0], updated)
    loss = loss_tile[0, 0]
    update_layer(0, _reduce_middle("0", scattered, pos, [total] + updated), [])

    def small_sum(nm):
        r0, nr = where[nm]
        return total[r0:r0 + nr]

    def my_cols(a):
        return lax.dynamic_slice_in_dim(a, chip * (D // N_SHARDS), D // N_SHARDS, axis=1)

    g_small = {
        "norm_mix": jnp.concatenate([small_sum(f"norm_mix_{i}") for i in range(DEPTH)], axis=0),
        "norm_mlp": jnp.concatenate([small_sum(f"norm_mlp_{i}") for i in range(DEPTH)], axis=0),
        "norm_ple": jnp.concatenate([small_sum(f"norm_ple_{i}") for i in range(DEPTH)], axis=0),
        "cf_b_pw1": jnp.stack([small_sum(f"cf_b_pw1_{j}").reshape(2 * D) for j in range(DEPTH // 2)]),
        "cf_w_dw": jnp.stack([my_cols(small_sum(f"cf_w_dw_{j}")) for j in range(DEPTH // 2)]),
        "cf_b_dw": jnp.concatenate([small_sum(f"cf_b_dw_{j}") for j in range(DEPTH // 2)], axis=0),
        "cf_norm": jnp.concatenate([small_sum(f"cf_norm_{j}") for j in range(DEPTH // 2)], axis=0),
        "cf_b_pw2": jnp.concatenate([small_sum(f"cf_b_pw2_{j}") for j in range(DEPTH // 2)], axis=0),
        "sc_w_conv": jnp.stack([my_cols(small_sum(f"sc_w_conv_{j}")) for j in range(DEPTH // 2)]),
        "norm_final": small_sum("norm_final").reshape(D),
    }

    names_out = ["norm_mix", "norm_mlp", "norm_ple", "cf_w_pw1", "cf_b_pw1", "cf_w_dw", "cf_b_dw", "cf_norm",
                 "cf_w_pw2", "cf_b_pw2", "sc_w_in", "sc_w_conv", "sc_w_out", "mlp_w1", "mlp_w2", "ple_w_proj",
                 "ple_w_gate", "norm_final"]
    grad, delta, new_m, new_v = {}, {}, {}, {}
    for nm in names_out:
        w = params[nm]
        if nm in big:
            grad[nm], delta[nm], new_m[nm], new_v[nm] = adam[nm]
            continue
        g = g_small[nm]
        cols = w.shape[-1] if w.ndim > 1 else w.shape[0]
        two_d = lambda a: a.reshape(-1, cols)
        res = _adamw(f"adamw_{nm}", two_d(w), two_d(mom1[nm]), two_d(mom2[nm]), two_d(g))
        grad[nm], delta[nm], new_m[nm], new_v[nm] = [r.reshape(w.shape) for r in res]

    return (loss, grad_x, *[grad[n] for n in names_out], *[delta[n] for n in names_out],
            *[new_m[n] for n in names_out], *[new_v[n] for n in names_out])
```

```python
import jax
import jax.numpy as jnp
from jax import lax
from jax.experimental import pallas as pl
from jax.experimental.pallas import tpu as pltpu

F32 = jnp.float32
BF16 = jnp.bfloat16

EPS = 1e-6
ADAM_LR = 0.001
ADAM_B1 = 0.9
ADAM_B2 = 0.999
ADAM_EPS = 1e-08
ADAM_WD = 0.01
ADAM_STEP = 10

DEPTH = 4
N_SHARDS = 4
N_DEVICES = 8
V7X_VMEM_LIMIT_BYTES = 56 * 1024 * 1024
SUBLANES = 8
MESH = pl.DeviceIdType.MESH

MM_TM = 1024
MM_TN = 1024
MM_TK = 2048
MM_TW_K = 1024
MM_TW_T = 4096
MM_TX_K = 1024
MM_TX_N = 2048
MM_TX_K_SHARDS = 512
MM_TX_K_LONG = 256
MM_TX_LONG_N = 4096
MM_TN_CROWDED = 512
MM_LONG_K = 8192
MM_TN_LONG_K = 256
ADAMW_TILE_ELEMS = 256 * 2048

CONV_ROW_CHUNK = 32
CONV_LANE_CHUNK = 512
CONV_TILE_ROWS = 128
CONV_FWD_TILE_ROWS = 256
ROW_TILE = 256
LIGHT_ROW_TILE = 512


def _tile(dim, pref):
    if dim <= pref:
        return dim
    t = pref
    while dim % t:
        t //= 2
    return t


def _cparams(*sem):
    return pltpu.CompilerParams(dimension_semantics=sem, vmem_limit_bytes=V7X_VMEM_LIMIT_BYTES)


def _sigmoid(x):
    return 0.5 * (jnp.tanh(0.5 * x) + 1.0)


def _rms_r(x):
    return lax.rsqrt(jnp.mean(x * x, axis=-1, keepdims=True) + EPS)


def _mm_nn(name, a, b3, epilogue, out_dtypes, extras=()):
    M, K = a.shape
    S, Kb, Ns = b3.shape
    assert Kb == K
    N = S * Ns
    crowded = sum(kind == "mn" for _, kind in extras) > 1
    tm, tn, tk = _tile(M, MM_TM), _tile(Ns, MM_TN_CROWDED if crowded else MM_TN), _tile(K, MM_TK)
    if K >= MM_LONG_K:
        tn, tk = _tile(Ns, MM_TN_LONG_K), K
    per = Ns // tn
    nk = K // tk
    in_specs = [pl.BlockSpec((tm, tk), lambda i, j, k: (i, k)),
                pl.BlockSpec((None, tk, tn), lambda i, j, k: (j // per, k, j % per))]
    for _, kind in extras:
        if kind == "mn":
            in_specs.append(pl.BlockSpec((tm, tn), lambda i, j, k: (i, j)))
        else:
            in_specs.append(pl.BlockSpec((1, tn), lambda i, j, k: (0, j)))
    n_ex, n_o = len(extras), len(out_dtypes)

    def body(*refs):
        a_ref, b_ref = refs[:2]
        ex = refs[2:2 + n_ex]
        outs = refs[2 + n_ex:2 + n_ex + n_o]
        part = jnp.dot(a_ref[...].astype(BF16), b_ref[...], preferred_element_type=F32)

        def finish(acc):
            res = epilogue(acc, *[e[...] for e in ex])
            for r, o in zip(res, outs):
                o[...] = r.astype(o.dtype)

        if nk == 1:
            finish(part)
        else:
            acc_ref = refs[-1]
            k = pl.program_id(2)

            @pl.when(k == 0)
            def _():
                acc_ref[...] = part

            @pl.when(k > 0)
            def _():
                acc_ref[...] += part

            @pl.when(k == nk - 1)
            def _():
                finish(acc_ref[...])

    res = pl.pallas_call(
        body, name=name, grid=(M // tm, N // tn, nk),
        in_specs=in_specs,
        out_specs=[pl.BlockSpec((tm, tn), lambda i, j, k: (i, j)) for _ in out_dtypes],
        out_shape=[jax.ShapeDtypeStruct((M, N), dt) for dt in out_dtypes],
        scratch_shapes=[pltpu.VMEM((tm, tn), F32)] if nk > 1 else [],
        compiler_params=_cparams("parallel", "parallel", "arbitrary"),
    )(a, b3, *[e for e, _ in extras])
    return res


def _mm_nt_shards(name, g, w3, epilogue, out_dtypes, extras):
    M, N = g.shape
    S, K, Ns = w3.shape
    tm = _tile(M, MM_TM)
    tkk = _tile(K, MM_TX_K_LONG if N > MM_TX_LONG_N else MM_TX_K_SHARDS)
    n_ex, n_o = len(extras), len(out_dtypes)

    def body(*refs):
        g_ref = refs[0]
        w_refs = refs[1:1 + S]
        ex = refs[1 + S:1 + S + n_ex]
        outs = refs[1 + S + n_ex:1 + S + n_ex + n_o]
        acc = None
        for s in range(S):
            part = lax.dot_general(g_ref[:, s * Ns:(s + 1) * Ns].astype(BF16), w_refs[s][...],
                                   (((1,), (1,)), ((), ())), preferred_element_type=F32)
            acc = part if acc is None else acc + part
        for r, o in zip(epilogue(acc, *[e[...] for e in ex]), outs):
            o[...] = r.astype(o.dtype)

    return pl.pallas_call(
        body, name=name, grid=(M // tm, K // tkk),
        in_specs=[pl.BlockSpec((tm, N), lambda i, kk: (i, 0))]
        + [pl.BlockSpec((None, tkk, Ns), lambda i, kk, s=s: (s, kk, 0)) for s in range(S)]
        + [pl.BlockSpec((tm, tkk), lambda i, kk: (i, kk)) for _ in extras],
        out_specs=[pl.BlockSpec((tm, tkk), lambda i, kk: (i, kk)) for _ in out_dtypes],
        out_shape=[jax.ShapeDtypeStruct((M, K), dt) for dt in out_dtypes],
        compiler_params=_cparams("parallel", "parallel"),
    )(g, *([w3] * S), *extras)


def _mm_nt(name, g, w3, epilogue, out_dtypes, extras=()):
    M, N = g.shape
    S, K, Ns = w3.shape
    assert S * Ns == N
    if S > 1:
        return _mm_nt_shards(name, g, w3, epilogue, out_dtypes, extras)
    tm, tn, tkk = _tile(M, MM_TM), _tile(Ns, MM_TX_N), _tile(K, MM_TX_K)
    per = Ns // tn
    nn = N // tn
    n_ex, n_o = len(extras), len(out_dtypes)

    def body(*refs):
        g_ref, w_ref = refs[:2]
        ex = refs[2:2 + n_ex]
        outs = refs[2 + n_ex:2 + n_ex + n_o]
        part = lax.dot_general(g_ref[...].astype(BF16), w_ref[...], (((1,), (1,)), ((), ())),
                               preferred_element_type=F32)

        def finish(acc):
            res = epilogue(acc, *[e[...] for e in ex])
            for r, o in zip(res, outs):
                o[...] = r.astype(o.dtype)

        if nn == 1:
            finish(part)
        else:
            acc_ref = refs[-1]
            n = pl.program_id(2)

            @pl.when(n == 0)
            def _():
                acc_ref[...] = part

            @pl.when(n > 0)
            def _():
                acc_ref[...] += part

            @pl.when(n == nn - 1)
            def _():
                finish(acc_ref[...])

    return pl.pallas_call(
        body, name=name, grid=(M // tm, K // tkk, nn),
        in_specs=[pl.BlockSpec((tm, tn), lambda i, kk, n: (i, n)),
                  pl.BlockSpec((None, tkk, tn), lambda i, kk, n: (n // per, kk, n % per))]
        + [pl.BlockSpec((tm, tkk), lambda i, kk, n: (i, kk)) for _ in extras],
        out_specs=[pl.BlockSpec((tm, tkk), lambda i, kk, n: (i, kk)) for _ in out_dtypes],
        out_shape=[jax.ShapeDtypeStruct((M, K), dt) for dt in out_dtypes],
        scratch_shapes=[pltpu.VMEM((tm, tkk), F32)] if nn > 1 else [],
        compiler_params=_cparams("parallel", "parallel", "arbitrary"),
    )(g, w3, *extras)


def _mm_tn(name, a, g, n_shards):
    T, K = a.shape
    _, N = g.shape
    Ns = N // n_shards
    tk, tn, tt = _tile(K, MM_TW_K), _tile(Ns, MM_TN), _tile(T, MM_TW_T)
    per = Ns // tn
    nt = T // tt

    def body(a_ref, g_ref, o_ref, *scratch):
        part = lax.dot_general(a_ref[...].astype(BF16), g_ref[...].astype(BF16), (((0,), (0,)), ((), ())),
                               preferred_element_type=F32)
        if nt == 1:
            o_ref[...] = part.astype(o_ref.dtype)
            return
        acc_ref, = scratch
        t = pl.program_id(2)

        @pl.when(t == 0)
        def _():
            acc_ref[...] = part

        @pl.when(t > 0)
        def _():
            acc_ref[...] += part

        @pl.when(t == nt - 1)
        def _():
            o_ref[...] = acc_ref[...].astype(o_ref.dtype)

    return pl.pallas_call(
        body, name=name, grid=(K // tk, N // tn, nt),
        in_specs=[pl.BlockSpec((tt, tk), lambda i, j, t: (t, i)),
                  pl.BlockSpec((tt, tn), lambda i, j, t: (t, j))],
        out_specs=pl.BlockSpec((None, tk, tn), lambda i, j, t: (j // per, i, j % per)),
        out_shape=jax.ShapeDtypeStruct((n_shards, K, Ns), BF16),
        scratch_shapes=[pltpu.VMEM((tk, tn), F32)] if nt > 1 else [],
        compiler_params=_cparams("parallel", "parallel", "arbitrary"),
    )(a, g)


def _rowwise(name, fn, ins, outs, accs=(), scratch=(), tt=ROW_TILE):
    T = next(a.shape[0] for a, kind in ins if kind == "row")
    tt = _tile(T, tt)
    n = T // tt
    in_specs = []
    for a, kind in ins:
        w = a.shape[1]
        if kind == "row":
            in_specs.append(pl.BlockSpec((tt, w), lambda i: (i, 0)))
        elif kind == "vec":
            in_specs.append(pl.BlockSpec(a.shape, lambda i: (0, 0)))
        elif kind[0] == "prev":
            pad = kind[1]
            in_specs.append(pl.BlockSpec((pad, w), lambda i, q=tt // pad: (jnp.maximum(i * q - 1, 0), 0)))
        else:
            pad = kind[1]
            in_specs.append(pl.BlockSpec((pad, w), lambda i, q=tt // pad, last=T // pad - 1:
                                         (jnp.minimum((i + 1) * q, last), 0)))
    n_in, n_out, n_acc = len(ins), len(outs), len(accs)

    def body(*refs):
        i = pl.program_id(0)
        in_refs = refs[:n_in]
        out_refs = refs[n_in:n_in + n_out]
        acc_refs = refs[n_in + n_out:n_in + n_out + n_acc]
        scr = refs[n_in + n_out + n_acc:]
        if n_acc:
            @pl.when(i == 0)
            def _():
                for r in acc_refs:
                    r[...] = jnp.zeros_like(r)
        fn(i, n, in_refs, out_refs, acc_refs, scr)

    res = pl.pallas_call(
        body, name=name, grid=(n,),
        in_specs=in_specs,
        out_specs=[pl.BlockSpec((tt, w), lambda i: (i, 0)) for w, _ in outs]
        + [pl.BlockSpec((r, w), lambda i: (0, 0)) for r, w in accs],
        out_shape=[jax.ShapeDtypeStruct((T, w), dt) for w, dt in outs]
        + [jax.ShapeDtypeStruct((r, w), F32) for r, w in accs],
        scratch_shapes=list(scratch),
        compiler_params=_cparams("arbitrary"),
    )(*[a for a, _ in ins])
    return res


def _colsum(x):
    return jnp.sum(x, axis=0, keepdims=True)


def _rms_fwd(name, h, g):
    D = h.shape[1]

    def fn(i, n, ins, outs, accs, scr):
        x = ins[0][...]
        outs[0][...] = (x * _rms_r(x) * ins[1][...]).astype(BF16)

    return _rowwise(name, fn, [(h, "row"), (g, "vec")], [(D, BF16)], tt=LIGHT_ROW_TILE)[0]


def _rms_bwd(name, h, g, du, dh_in, want_colsum=False):
    D = h.shape[1]

    def fn(i, n, ins, outs, accs, scr):
        x = ins[0][...]
        gg = ins[1][...]
        d = ins[2][...].astype(F32)
        r = _rms_r(x)
        xn = x * r
        t = d * gg
        dh = ins[3][...] + r * (t - xn * jnp.mean(t * xn, axis=-1, keepdims=True))
        outs[0][...] = dh
        outs[1][...] = dh.astype(BF16)
        accs[0][...] += _colsum(d * xn)
        if want_colsum:
            accs[1][...] += _colsum(dh)

    return _rowwise(name, fn, [(h, "row"), (g, "vec"), (du, "row"), (dh_in, "row")],
                    [(D, F32), (D, BF16)], accs=[(1, D)] * (2 if want_colsum else 1))


def _loss_bwd(name, h, g, tgt):
    D = h.shape[1]

    def fn(i, n, ins, outs, accs, scr):
        x = ins[0][...]
        gg = ins[1][...]
        r = _rms_r(x)
        xn = x * r
        err = xn * gg - ins[2][...]
        dy = err / D
        t = dy * gg
        dh = r * (t - xn * jnp.mean(t * xn, axis=-1, keepdims=True))
        outs[0][...] = dh
        outs[1][...] = dh.astype(BF16)
        accs[0][...] += _colsum(dy * xn)
        accs[1][...] += _colsum(err * err)

    return _rowwise(name, fn, [(h, "row"), (g, "vec"), (tgt, "row")], [(D, F32), (D, BF16)],
                    accs=[(1, D), (1, D)])


def _ple_elem_bwd(name, dh, q, e):
    D = dh.shape[1]

    def fn(i, n, ins, outs, accs, scr):
        d = ins[0][...]
        s = _sigmoid(ins[1][...].astype(F32))
        ee = ins[2][...].astype(F32)
        outs[0][...] = (d * ee * s * (1.0 - s)).astype(BF16)
        outs[1][...] = (d * s).astype(BF16)

    return _rowwise(name, fn, [(dh, "row"), (q, "row"), (e, "row")], [(D, BF16), (D, BF16)], tt=LIGHT_ROW_TILE)


def _cf_norm_bwd(name, v2, g, dv4):
    D = v2.shape[1]

    def fn(i, n, ins, outs, accs, scr):
        x = ins[0][...]
        gg = ins[1][...]
        r = _rms_r(x)
        xn = x * r
        v3 = xn * gg
        s = _sigmoid(v3)
        dv3 = ins[2][...].astype(F32) * (s * (1.0 + v3 * (1.0 - s)))
        t = dv3 * gg
        dv2 = r * (t - xn * jnp.mean(t * xn, axis=-1, keepdims=True))
        outs[0][...] = dv2
        accs[0][...] += _colsum(dv3 * xn)
        accs[1][...] += _colsum(dv2)

    return _rowwise(name, fn, [(v2, "row"), (g, "vec"), (dv4, "row")], [(D, F32)], accs=[(1, D), (1, D)])


def _chunks(tt, width):
    cc = min(CONV_LANE_CHUNK, width)
    rc = min(CONV_ROW_CHUNK, tt)
    for c0 in range(0, width, cc):
        for r0 in range(0, tt, rc):
            yield r0, rc, c0, cc


def _n_shifts(n_taps):
    return min(SUBLANES - 1, n_taps - 1)


def _shifted_scratch(n_taps, rows, width):
    return pltpu.VMEM((_n_shifts(n_taps), rows, width), F32)


def _shift_window(win_ref, sh_ref, n_taps, sign):
    rows = win_ref.shape[0] - SUBLANES
    width = win_ref.shape[1]
    cc = min(CONV_LANE_CHUNK, width)
    for b in range(1, _n_shifts(n_taps) + 1):
        off = SUBLANES - b if sign < 0 else b
        for c0 in range(0, width, cc):
            sh_ref[b - 1, 0:rows, c0:c0 + cc] = win_ref[off:off + rows, c0:c0 + cc]


def _tap(win_ref, sh_ref, base, sign, s, r0, rc, c0, cc):
    a, b = divmod(s, SUBLANES)
    if b == 0:
        row = base + r0 + sign * SUBLANES * a
        return win_ref[row:row + rc, c0:c0 + cc]
    row = base + r0 - SUBLANES * (a + 1) if sign < 0 else base + r0 + SUBLANES * a
    return sh_ref[b - 1, row:row + rc, c0:c0 + cc]


def _fir(win_ref, sh_ref, w_ref, n_taps, base, sign, tt, width, emit):
    for r0, rc, c0, cc in _chunks(tt, width):
        acc = jnp.zeros((rc, cc), F32)
        for k in range(n_taps):
            acc = acc + w_ref[k:k + 1, c0:c0 + cc] * _tap(win_ref, sh_ref, base, sign, n_taps - 1 - k, r0, rc, c0, cc)
        emit(r0, rc, c0, cc, acc)


def _fir_wgrad(d_ref, win_ref, sh_ref, dw8_ref, n_taps, pad, tt, width):
    for c0 in range(0, width, min(CONV_LANE_CHUNK, width)):
        cc = min(CONV_LANE_CHUNK, width)
        rc = min(CONV_ROW_CHUNK, tt)
        for k in range(n_taps):
            acc = jnp.zeros((SUBLANES, cc), F32)
            for r0 in range(0, tt, rc):
                prod = d_ref[r0:r0 + rc, c0:c0 + cc] * _tap(win_ref, sh_ref, pad, -1, n_taps - 1 - k, r0, rc, c0, cc)
                for q in range(0, rc, SUBLANES):
                    acc = acc + prod[q:q + SUBLANES]
            dw8_ref[SUBLANES * k:SUBLANES * (k + 1), c0:c0 + cc] += acc


def _glu(blk, D):
    return blk[:, :D].astype(F32) * _sigmoid(blk[:, D:].astype(F32))


CF_PAD = 32
SC_PAD = 16


def _cf_conv_fwd(name, a, w_dw, b_dw, g_cf):
    T, D2 = a.shape
    D = D2 // 2
    K = w_dw.shape[0]
    tt = _tile(T, CONV_FWD_TILE_ROWS)

    def fn(i, n, ins, outs, accs, scr):
        a_ref, prev_ref, w_ref, b_ref, g_ref = ins
        win_ref, v2_ref, sh_ref = scr
        win_ref[0:CF_PAD, :] = jnp.where(i > 0, _glu(prev_ref[...], D), 0.0)
        win_ref[CF_PAD:CF_PAD + tt, :] = _glu(a_ref[...], D)
        _shift_window(win_ref, sh_ref, K, -1)

        def emit(r0, rc, c0, cc, acc):
            v2_ref[r0:r0 + rc, c0:c0 + cc] = acc + b_ref[:, c0:c0 + cc]

        _fir(win_ref, sh_ref, w_ref, K, CF_PAD, -1, tt, D, emit)
        v2 = v2_ref[...]
        v3 = v2 * _rms_r(v2) * g_ref[...]
        outs[0][...] = v2
        outs[1][...] = (v3 * _sigmoid(v3)).astype(BF16)

    return _rowwise(name, fn, [(a, "row"), (a, ("prev", CF_PAD)), (w_dw, "vec"), (b_dw, "vec"), (g_cf, "vec")],
                    [(D, F32), (D, BF16)],
                    scratch=[pltpu.VMEM((CF_PAD + tt, D), F32), pltpu.VMEM((tt, D), F32),
                             _shifted_scratch(K, CF_PAD + tt, D)], tt=tt)


def _cf_conv_bwd(name, dv2, a, w_dw):
    T, D2 = a.shape
    D = D2 // 2
    K = w_dw.shape[0]
    tt = _tile(T, CONV_TILE_ROWS)

    def fn(i, n, ins, outs, accs, scr):
        d_ref, dnext_ref, a_ref, prev_ref, w_ref = ins
        v1win_ref, dwin_ref, dv1_ref, dw8_ref, v1sh_ref, dsh_ref = scr

        @pl.when(i == 0)
        def _():
            dw8_ref[...] = jnp.zeros_like(dw8_ref)

        v1win_ref[0:CF_PAD, :] = jnp.where(i > 0, _glu(prev_ref[...], D), 0.0)
        v1win_ref[CF_PAD:CF_PAD + tt, :] = _glu(a_ref[...], D)
        dwin_ref[0:tt, :] = d_ref[...]
        dwin_ref[tt:tt + CF_PAD, :] = jnp.where(i < n - 1, dnext_ref[...], 0.0)
        _shift_window(v1win_ref, v1sh_ref, K, -1)
        _shift_window(dwin_ref, dsh_ref, K, 1)

        def emit(r0, rc, c0, cc, acc):
            dv1_ref[r0:r0 + rc, c0:c0 + cc] = acc

        _fir(dwin_ref, dsh_ref, w_ref, K, 0, 1, tt, D, emit)
        _fir_wgrad(d_ref, v1win_ref, v1sh_ref, dw8_ref, K, CF_PAD, tt, D)

        blk = a_ref[...]
        val = blk[:, :D].astype(F32)
        sg = _sigmoid(blk[:, D:].astype(F32))
        dv1 = dv1_ref[...]
        dval = dv1 * sg
        dgate = dv1 * val * sg * (1.0 - sg)
        outs[0][:, :D] = dval.astype(BF16)
        outs[0][:, D:] = dgate.astype(BF16)
        accs[1][:, :D] += _colsum(dval)
        accs[1][:, D:] += _colsum(dgate)

        @pl.when(i == n - 1)
        def _():
            for k in range(K):
                accs[0][k:k + 1, :] = _colsum(dw8_ref[SUBLANES * k:SUBLANES * (k + 1), :])

    return _rowwise(name, fn, [(dv2, "row"), (dv2, ("next", CF_PAD)), (a, "row"), (a, ("prev", CF_PAD)),
                               (w_dw, "vec")],
                    [(D2, BF16)], accs=[(K, D), (1, D2)],
                    scratch=[pltpu.VMEM((CF_PAD + tt, D), F32), pltpu.VMEM((tt + CF_PAD, D), F32),
                             pltpu.VMEM((tt, D), F32), pltpu.VMEM((SUBLANES * K, D), F32),
                             _shifted_scratch(K, CF_PAD + tt, D), _shifted_scratch(K, tt + CF_PAD, D)], tt=tt)


def _sc_conv_fwd(name, bcv, w_conv):
    T, D3 = bcv.shape
    D = D3 // 3
    K = w_conv.shape[0]
    tt = _tile(T, CONV_TILE_ROWS)

    def cv_of(blk):
        return blk[:, D:2 * D].astype(F32) * blk[:, 2 * D:].astype(F32)

    def fn(i, n, ins, outs, accs, scr):
        x_ref, prev_ref, w_ref = ins
        win_ref, cc_ref, sh_ref = scr
        win_ref[0:SC_PAD, :] = jnp.where(i > 0, cv_of(prev_ref[...]), 0.0)
        win_ref[SC_PAD:SC_PAD + tt, :] = cv_of(x_ref[...])
        _shift_window(win_ref, sh_ref, K, -1)

        def emit(r0, rc, c0, cw, acc):
            cc_ref[r0:r0 + rc, c0:c0 + cw] = acc

        _fir(win_ref, sh_ref, w_ref, K, SC_PAD, -1, tt, D, emit)
        outs[0][...] = (x_ref[:, :D].astype(F32) * cc_ref[...]).astype(BF16)

    return _rowwise(name, fn, [(bcv, "row"), (bcv, ("prev", SC_PAD)), (w_conv, "vec")], [(D, BF16)],
                    scratch=[pltpu.VMEM((SC_PAD + tt, D), F32), pltpu.VMEM((tt, D), F32),
                             _shifted_scratch(K, SC_PAD + tt, D)], tt=tt)


def _sc_conv_bwd(name, dy, bcv, w_conv):
    T, D3 = bcv.shape
    D = D3 // 3
    K = w_conv.shape[0]
    tt = _tile(T, CONV_TILE_ROWS)

    def cv_of(blk):
        return blk[:, D:2 * D].astype(F32) * blk[:, 2 * D:].astype(F32)

    def fn(i, n, ins, outs, accs, scr):
        dy_ref, dynext_ref, x_ref, prev_ref, next_ref, w_ref = ins
        cvwin_ref, dccwin_ref, tmp_ref, dw8_ref, cvsh_ref, dccsh_ref = scr

        @pl.when(i == 0)
        def _():
            dw8_ref[...] = jnp.zeros_like(dw8_ref)

        cvwin_ref[0:SC_PAD, :] = jnp.where(i > 0, cv_of(prev_ref[...]), 0.0)
        cvwin_ref[SC_PAD:SC_PAD + tt, :] = cv_of(x_ref[...])
        _shift_window(cvwin_ref, cvsh_ref, K, -1)

        def emit_cc(r0, rc, c0, cw, acc):
            tmp_ref[r0:r0 + rc, c0:c0 + cw] = acc

        _fir(cvwin_ref, cvsh_ref, w_ref, K, SC_PAD, -1, tt, D, emit_cc)
        dy_v = dy_ref[...].astype(F32)
        outs[0][:, :D] = (dy_v * tmp_ref[...]).astype(BF16)
        dccwin_ref[0:tt, :] = dy_v * x_ref[:, :D].astype(F32)
        dccwin_ref[tt:tt + SC_PAD, :] = jnp.where(
            i < n - 1, dynext_ref[...].astype(F32) * next_ref[:, :D].astype(F32), 0.0)

        _shift_window(dccwin_ref, dccsh_ref, K, 1)

        def emit_dcv(r0, rc, c0, cw, acc):
            tmp_ref[r0:r0 + rc, c0:c0 + cw] = acc

        _fir(dccwin_ref, dccsh_ref, w_ref, K, 0, 1, tt, D, emit_dcv)
        _fir_wgrad(dccwin_ref, cvwin_ref, cvsh_ref, dw8_ref, K, SC_PAD, tt, D)
        dcv = tmp_ref[...]
        outs[0][:, D:2 * D] = (dcv * x_ref[:, 2 * D:].astype(F32)).astype(BF16)
        outs[0][:, 2 * D:] = (dcv * x_ref[:, D:2 * D].astype(F32)).astype(BF16)

        @pl.when(i == n - 1)
        def _():
            for k in range(K):
                accs[0][k:k + 1, :] = _colsum(dw8_ref[SUBLANES * k:SUBLANES * (k + 1), :])

    return _rowwise(name, fn, [(dy, "row"), (dy, ("next", SC_PAD)), (bcv, "row"), (bcv, ("prev", SC_PAD)),
                               (bcv, ("next", SC_PAD)), (w_conv, "vec")],
                    [(D3, BF16)], accs=[(K, D)],
                    scratch=[pltpu.VMEM((SC_PAD + tt, D), F32), pltpu.VMEM((tt + SC_PAD, D), F32),
                             pltpu.VMEM((tt, D), F32), pltpu.VMEM((SUBLANES * K, D), F32),
                             _shifted_scratch(K, SC_PAD + tt, D), _shifted_scratch(K, tt + SC_PAD, D)], tt=tt)


def _place():
    x, y, c = lax.axis_index("x"), lax.axis_index("y"), lax.axis_index("c")
    chips = [(1 - x, y), (x, 1 - y), (1 - x, 1 - y)]
    return x, y, c, 2 * x + y, chips, (x, y, 1 - c)


def _half(rows, which):
    return pl.ds(pl.multiple_of(which * (rows // 2), SUBLANES), rows // 2)


_HBM = pl.BlockSpec(memory_space=pl.ANY)


def _gather_shards(name, items):
    n = len(items)
    shapes = [a.shape[-2:] for a, _ in items]

    def body(*refs):
        srcs, outs = refs[:n], refs[n:2 * n]
        send1, recv1, send2, recv2, lsem = refs[2 * n:]
        x, y, c, k, chips, sib = _place()

        def shard(i):
            return srcs[i] if items[i][1] is None else srcs[i].at[items[i][1]]

        started, locs = [], []
        for i in range(n):
            rows = shapes[i][0]
            lc = pltpu.make_async_copy(shard(i), outs[i].at[k], lsem.at[i])
            lc.start()
            locs.append(lc)
            for j, (cx, cy) in enumerate(chips):
                cp = pltpu.make_async_remote_copy(
                    src_ref=shard(i).at[_half(rows, c)], dst_ref=outs[i].at[k, _half(rows, c)],
                    send_sem=send1.at[i, j], recv_sem=recv1.at[i, j], device_id=(cx, cy, c), device_id_type=MESH)
                cp.start()
                started.append(cp)
        for i in range(n):
            rows = shapes[i][0]
            for j, (cx, cy) in enumerate(chips):
                blk = outs[i].at[2 * cx + cy, _half(rows, c)]
                pltpu.make_async_remote_copy(
                    src_ref=blk, dst_ref=blk, send_sem=send1.at[i, j], recv_sem=recv1.at[i, j],
                    device_id=(cx, cy, c), device_id_type=MESH).wait_recv()
                fw = pltpu.make_async_remote_copy(
                    src_ref=blk, dst_ref=blk, send_sem=send2.at[i, j], recv_sem=recv2.at[i, j],
                    device_id=sib, device_id_type=MESH)
                fw.start()
                started.append(fw)
        for i in range(n):
            rows = shapes[i][0]
            for j, (cx, cy) in enumerate(chips):
                blk = outs[i].at[2 * cx + cy, _half(rows, 1 - c)]
                pltpu.make_async_remote_copy(
                    src_ref=blk, dst_ref=blk, send_sem=send2.at[i, j], recv_sem=recv2.at[i, j],
                    device_id=sib, device_id_type=MESH).wait_recv()
        for cp in started:
            cp.wait_send()
        for lc in locs:
            lc.wait()

    return pl.pallas_call(
        body, name=name,
        in_specs=[_HBM] * n, out_specs=[_HBM] * n,
        out_shape=[jax.ShapeDtypeStruct((N_SHARDS,) + tuple(s), a.dtype) for s, (a, _) in zip(shapes, items)],
        scratch_shapes=[pltpu.SemaphoreType.DMA((n, 3))] * 4 + [pltpu.SemaphoreType.DMA((n,))],
    )(*[a for a, _ in items])


def _cast_place(name, w, layer, pos):
    _, R, C = w.shape
    tr = _tile(R, 256)

    def body(x_ref, y_ref, c_ref, w_ref, o_ref):
        o_ref[...] = w_ref[...].astype(BF16)

    return pl.pallas_call(
        body, name=name,
        grid_spec=pltpu.PrefetchScalarGridSpec(
            num_scalar_prefetch=3, grid=(R // tr,),
            in_specs=[pl.BlockSpec((None, tr, C), lambda r, xr, yr, cr: (layer, r, 0))],
            out_specs=pl.BlockSpec((None, tr, C), lambda r, xr, yr, cr: (2 * xr[0] + yr[0], r, 0))),
        out_shape=jax.ShapeDtypeStruct((N_SHARDS, R, C), BF16),
        compiler_params=_cparams("parallel"),
    )(*pos, w)


_IN_HBM = pl.BlockSpec(memory_space=pltpu.HBM)
_SEM = pl.BlockSpec(memory_space=pltpu.SEMAPHORE)
_SPLIT_COPY_PARAMS = pltpu.CompilerParams(has_side_effects=pltpu.SideEffectType.DATAFLOW_SIDE_EFFECTING)


def _in_hbm(a):
    return pltpu.with_memory_space_constraint(a, pltpu.HBM)


def _gather_copy(ref, i, j, chip_xy, c, k_src, rows, send, recv):
    blk = ref.at[k_src, _half(rows, c)]
    return pltpu.make_async_remote_copy(
        src_ref=blk, dst_ref=blk, send_sem=send.at[3 * i + j], recv_sem=recv.at[3 * i + j],
        device_id=(*chip_xy, c), device_id_type=MESH)


def _gather_start(name, bufs, after):
    n = len(bufs)

    def body(*refs):
        ins = refs[:n]
        send, recv = refs[n + 1], refs[n + 2]
        token = refs[-1]
        x, y, c, k, chips, sib = _place()
        for i in range(n):
            for j, chip_xy in enumerate(chips):
                _gather_copy(ins[i], i, j, chip_xy, c, k, bufs[i].shape[1], send, recv).start()
        token[...] = jnp.zeros_like(token)

    res = pl.pallas_call(
        body, name=name,
        in_specs=[_IN_HBM] * n + [_HBM],
        out_specs=[_SEM, _SEM] + [_IN_HBM] * n + [pl.BlockSpec(memory_space=pltpu.VMEM)],
        out_shape=[pltpu.SemaphoreType.DMA((3 * n,)), pltpu.SemaphoreType.DMA((3 * n,))]
        + [pltpu.HBM(b.shape, b.dtype) for b in bufs] + [jax.ShapeDtypeStruct((SUBLANES, 128), F32)],
        input_output_aliases={i: 2 + i for i in range(n)},
        compiler_params=_SPLIT_COPY_PARAMS,
    )(*[_in_hbm(b) for b in bufs], after)
    return res[0], res[1], list(res[2:2 + n]), res[-1]


def _gather_wait(name, bufs, send, recv, afters):
    n = len(bufs)

    def body(*refs):
        ins = refs[:n]
        send_ref, recv_ref = refs[n], refs[n + 1]
        x, y, c, k, chips, sib = _place()
        for i in range(n):
            for j, chip_xy in enumerate(chips):
                rows = bufs[i].shape[1]
                _gather_copy(ins[i], i, j, chip_xy, c, k, rows, send_ref, recv_ref).wait_send()
                _gather_copy(ins[i], i, j, chip_xy, c, 2 * chip_xy[0] + chip_xy[1], rows, send_ref, recv_ref).wait_recv()

    return pl.pallas_call(
        body, name=name,
        in_specs=[_IN_HBM] * n + [_SEM, _SEM] + [_HBM] * len(afters),
        out_specs=[_IN_HBM] * n,
        out_shape=[pltpu.HBM(b.shape, b.dtype) for b in bufs],
        input_output_aliases={i: i for i in range(n)},
        compiler_params=_SPLIT_COPY_PARAMS,
    )(*bufs, send, recv, *afters)


def _gather_forward(name, bufs):
    n = len(bufs)

    def body(*refs):
        outs = refs[n:2 * n]
        send, recv = refs[2 * n:]
        x, y, c, k, chips, sib = _place()
        started = []
        for i in range(n):
            rows = bufs[i].shape[1]
            for j, (cx, cy) in enumerate(chips):
                blk = outs[i].at[2 * cx + cy, _half(rows, c)]
                fw = pltpu.make_async_remote_copy(
                    src_ref=blk, dst_ref=blk, send_sem=send.at[i, j], recv_sem=recv.at[i, j],
                    device_id=sib, device_id_type=MESH)
                fw.start()
                started.append(fw)
        for i in range(n):
            rows = bufs[i].shape[1]
            for j, (cx, cy) in enumerate(chips):
                blk = outs[i].at[2 * cx + cy, _half(rows, 1 - c)]
                pltpu.make_async_remote_copy(
                    src_ref=blk, dst_ref=blk, send_sem=send.at[i, j], recv_sem=recv.at[i, j],
                    device_id=sib, device_id_type=MESH).wait_recv()
        for cp in started:
            cp.wait_send()

    return pl.pallas_call(
        body, name=name, in_specs=[_HBM] * n, out_specs=[_HBM] * n,
        out_shape=[jax.ShapeDtypeStruct(b.shape, b.dtype) for b in bufs],
        input_output_aliases={i: i for i in range(n)},
        scratch_shapes=[pltpu.SemaphoreType.DMA((n, 3))] * 2,
    )(*bufs)


def _swap_halves(name, parts):
    n = len(parts)

    def body(*refs):
        srcs, outs = refs[:n], refs[n:2 * n]
        send, recv = refs[2 * n:]
        x, y, c, k, chips, sib = _place()
        cps = []
        for i in range(n):
            rows = parts[i].shape[1]
            cp = pltpu.make_async_remote_copy(
                src_ref=srcs[i].at[:, _half(rows, 1 - c)], dst_ref=outs[i],
                send_sem=send.at[i], recv_sem=recv.at[i], device_id=sib, device_id_type=MESH)
            cp.start()
            cps.append(cp)
        for cp in cps:
            cp.wait()

    return pl.pallas_call(
        body, name=name, in_specs=[_HBM] * n, out_specs=[_HBM] * n,
        out_shape=[jax.ShapeDtypeStruct((p.shape[0], p.shape[1] // 2, p.shape[2]), p.dtype) for p in parts],
        scratch_shapes=[pltpu.SemaphoreType.DMA((n,))] * 2,
    )(*parts)


def _sibling_start(name, bufs, n_copies, make):
    nb = len(bufs)

    def body(*refs):
        send, recv = refs[nb], refs[nb + 1]
        token = refs[-1]
        x, y, c, k, chips, sib = _place()
        for cp in make(refs[:nb], c, sib, send, recv):
            cp.start()
        token[...] = jnp.zeros_like(token)

    res = pl.pallas_call(
        body, name=name,
        in_specs=[_IN_HBM] * nb,
        out_specs=[_SEM, _SEM] + [_IN_HBM] * nb + [pl.BlockSpec(memory_space=pltpu.VMEM)],
        out_shape=[pltpu.SemaphoreType.DMA((n_copies,)), pltpu.SemaphoreType.DMA((n_copies,))]
        + [pltpu.HBM(b.shape, b.dtype) for b in bufs] + [jax.ShapeDtypeStruct((SUBLANES, 128), F32)],
        input_output_aliases={i: 2 + i for i in range(nb)},
        compiler_params=_SPLIT_COPY_PARAMS,
    )(*[_in_hbm(b) for b in bufs])
    return res[0], res[1], list(res[2:2 + nb]), res[-1]


def _sibling_wait(name, bufs, send, recv, make, afters):
    nb = len(bufs)

    def body(*refs):
        x, y, c, k, chips, sib = _place()
        for cp in make(refs[:nb], c, sib, refs[nb], refs[nb + 1]):
            cp.wait_send()
            cp.wait_recv()

    return list(pl.pallas_call(
        body, name=name,
        in_specs=[_IN_HBM] * nb + [_SEM, _SEM] + [_HBM] * len(afters),
        out_specs=[_IN_HBM] * nb,
        out_shape=[pltpu.HBM(b.shape, b.dtype) for b in bufs],
        input_output_aliases={i: i for i in range(nb)},
        compiler_params=_SPLIT_COPY_PARAMS,
    )(*bufs, send, recv, *afters))


def _swap_copies(parts):
    n = len(parts)

    def make(refs, c, sib, send, recv):
        return [pltpu.make_async_remote_copy(
            src_ref=refs[i].at[:, _half(parts[i].shape[1], 1 - c)], dst_ref=refs[n + i],
            send_sem=send.at[i], recv_sem=recv.at[i], device_id=sib, device_id_type=MESH) for i in range(n)]

    return make


def _join_copies(fulls):
    def make(refs, c, sib, send, recv):
        cps = []
        for i, f in enumerate(fulls):
            blk = refs[i].at[_half(f.shape[0], c)]
            cps.append(pltpu.make_async_remote_copy(
                src_ref=blk, dst_ref=blk, send_sem=send.at[i], recv_sem=recv.at[i],
                device_id=sib, device_id_type=MESH))
        return cps

    return make


def _scatter_copy(src_ref, land_ref, i, j, chip_xy, c, send, recv):
    return pltpu.make_async_remote_copy(
        src_ref=src_ref.at[2 * chip_xy[0] + chip_xy[1]], dst_ref=land_ref.at[j],
        send_sem=send.at[3 * i + j], recv_sem=recv.at[3 * i + j], device_id=(*chip_xy, c), device_id_type=MESH)


def _scatter_start(name, sums):
    n = len(sums)
    lands = [lax.empty((3,) + s.shape[1:], s.dtype) for s in sums]

    def body(*refs):
        srcs, lnds = refs[:n], refs[n:2 * n]
        send, recv = refs[2 * n], refs[2 * n + 1]
        token = refs[-1]
        x, y, c, k, chips, sib = _place()
        for i in range(n):
            for j, chip_xy in enumerate(chips):
                _scatter_copy(srcs[i], lnds[i], i, j, chip_xy, c, send, recv).start()
        token[...] = jnp.zeros_like(token)

    res = pl.pallas_call(
        body, name=name,
        in_specs=[_IN_HBM] * (2 * n),
        out_specs=[_SEM, _SEM] + [_IN_HBM] * (2 * n) + [pl.BlockSpec(memory_space=pltpu.VMEM)],
        out_shape=[pltpu.SemaphoreType.DMA((3 * n,)), pltpu.SemaphoreType.DMA((3 * n,))]
        + [pltpu.HBM(a.shape, a.dtype) for a in list(sums) + lands] + [jax.ShapeDtypeStruct((SUBLANES, 128), F32)],
        input_output_aliases={i: 2 + i for i in range(2 * n)},
        compiler_params=_SPLIT_COPY_PARAMS,
    )(*[_in_hbm(a) for a in list(sums) + lands])
    return res[0], res[1], list(res[2:2 + n]), list(res[2 + n:2 + 2 * n]), res[-1]


def _scatter_wait(name, sums, lands, send, recv, afters):
    n = len(sums)

    def body(*refs):
        srcs, lnds = refs[:n], refs[n:2 * n]
        send_ref, recv_ref = refs[2 * n], refs[2 * n + 1]
        x, y, c, k, chips, sib = _place()
        for i in range(n):
            for j, chip_xy in enumerate(chips):
                cp = _scatter_copy(srcs[i], lnds[i], i, j, chip_xy, c, send_ref, recv_ref)
                cp.wait_send()
                cp.wait_recv()

    res = pl.pallas_call(
        body, name=name,
        in_specs=[_IN_HBM] * (2 * n) + [_SEM, _SEM] + [_HBM] * len(afters),
        out_specs=[_IN_HBM] * (2 * n),
        out_shape=[pltpu.HBM(a.shape, a.dtype) for a in list(sums) + list(lands)],
        input_output_aliases={i: i for i in range(2 * n)},
        compiler_params=_SPLIT_COPY_PARAMS,
    )(*sums, *lands, send, recv, *afters)
    return list(res[:n]), list(res[n:])


def _sum_over_devices(name, buf, loss_row, afters):
    R, D = buf.shape
    n_after = len(afters)

    def body(x_ref, *rest):
        all_ref, tot_ref, loss_ref, send_sems, recv_sems, local_sem = rest[n_after:]
        x, y, c, k, chips, sib = _place()
        me = (x, y, c)

        def block(px, py, pc):
            return all_ref.at[4 * px + 2 * py + pc]

        def copy(kk, blk, to, src=None):
            return pltpu.make_async_remote_copy(
                src_ref=block(*blk) if src is None else src, dst_ref=block(*blk),
                send_sem=send_sems.at[kk], recv_sem=recv_sems.at[kk], device_id=to, device_id_type=MESH)

        mine = pltpu.make_async_copy(x_ref, block(*me), local_sem)
        mine.start()
        first = [copy(0, me, sib, src=x_ref)]
        first += [copy(1 + j, me, (*chip, c), src=x_ref) for j, chip in enumerate(chips)]
        for cp in first:
            cp.start()
        passed = [copy(4 + j, (*chip, c), sib) for j, chip in enumerate(chips)]
        for j, chip in enumerate(chips):
            copy(1 + j, (*chip, c), me).wait_recv()
            passed[j].start()
        copy(0, sib, me).wait_recv()
        for j, chip in enumerate(chips):
            copy(4 + j, (*chip, 1 - c), me).wait_recv()
        for cp in first + passed:
            cp.wait_send()
        mine.wait()
        rc = _tile(R, 32)
        for r0 in range(0, R, rc):
            tot = all_ref[0, r0:r0 + rc, :]
            for d in range(1, N_DEVICES):
                tot = tot + all_ref[d, r0:r0 + rc, :]
            tot_ref[r0:r0 + rc, :] = tot
        loss = 0.5 * jnp.sum(tot_ref[loss_row:loss_row + 1, :]) / D
        loss_ref[...] = jnp.full(loss_ref.shape, loss, F32)

    vm = pl.BlockSpec(memory_space=pltpu.VMEM)
    return pl.pallas_call(
        body, name=name, in_specs=[vm] + [_HBM] * n_after, out_specs=[vm, vm, vm],
        out_shape=[jax.ShapeDtypeStruct((N_DEVICES, R, D), F32), jax.ShapeDtypeStruct((R, D), F32),
                   jax.ShapeDtypeStruct((SUBLANES, 128), F32)],
        scratch_shapes=[pltpu.SemaphoreType.DMA((7,)), pltpu.SemaphoreType.DMA((7,)), pltpu.SemaphoreType.DMA],
        compiler_params=pltpu.CompilerParams(vmem_limit_bytes=V7X_VMEM_LIMIT_BYTES),
    )(buf, *afters)[1:]


def _add_my_half(name, part, got, pos):
    S, R, C = part.shape
    R2 = R // 2
    tr = _tile(R2, 1024)
    q = R2 // tr

    def body(x_ref, y_ref, c_ref, p_ref, g_ref, o_ref):
        o_ref[...] = (p_ref[...].astype(F32) + g_ref[...].astype(F32)).astype(o_ref.dtype)

    return pl.pallas_call(
        body, name=name,
        grid_spec=pltpu.PrefetchScalarGridSpec(
            num_scalar_prefetch=3, grid=(S, q),
            in_specs=[pl.BlockSpec((None, tr, C), lambda s, r, xr, yr, cr: (s, cr[0] * q + r, 0)),
                      pl.BlockSpec((None, tr, C), lambda s, r, xr, yr, cr: (s, r, 0))],
            out_specs=pl.BlockSpec((None, tr, C), lambda s, r, xr, yr, cr: (s, r, 0))),
        out_shape=jax.ShapeDtypeStruct((S, R2, C), BF16),
        compiler_params=_cparams("parallel", "parallel"),
    )(*pos, part, got)


def _add_owner(name, sums, got, pos):
    _, R2, C = sums.shape
    tr = _tile(R2, 512)
    q = R2 // tr

    def body(x_ref, y_ref, c_ref, s_ref, g_ref, o_ref):
        acc = s_ref[...].astype(F32)
        for j in range(3):
            acc = acc + g_ref[j].astype(F32)
        o_ref[...] = acc

    return pl.pallas_call(
        body, name=name,
        grid_spec=pltpu.PrefetchScalarGridSpec(
            num_scalar_prefetch=3, grid=(q,),
            in_specs=[pl.BlockSpec((None, tr, C), lambda r, xr, yr, cr: (2 * xr[0] + yr[0], r, 0)),
                      pl.BlockSpec((3, tr, C), lambda r, xr, yr, cr: (0, r, 0))],
            out_specs=pl.BlockSpec((tr, C), lambda r, xr, yr, cr: (cr[0] * q + r, 0))),
        out_shape=jax.ShapeDtypeStruct((2 * R2, C), F32),
        compiler_params=_cparams("parallel"),
    )(*pos, sums, got)


def _adamw(name, w, m, v, g):
    R, C = w.shape
    tr = SUBLANES
    while 2 * tr * C <= ADAMW_TILE_ELEMS:
        tr *= 2
    tr = _tile(R, tr)
    bc1 = 1.0 - ADAM_B1 ** ADAM_STEP
    bc2 = 1.0 - ADAM_B2 ** ADAM_STEP

    def body(w_ref, m_ref, v_ref, g_ref, go_ref, d_ref, mo_ref, vo_ref):
        gg = g_ref[...]
        m2 = ADAM_B1 * m_ref[...] + (1.0 - ADAM_B1) * gg
        v2 = ADAM_B2 * v_ref[...] + (1.0 - ADAM_B2) * (gg * gg)
        go_ref[...] = gg
        mo_ref[...] = m2
        vo_ref[...] = v2
        d_ref[...] = -ADAM_LR * ((m2 / bc1) / (jnp.sqrt(v2 / bc2) + ADAM_EPS) + ADAM_WD * w_ref[...])

    spec = pl.BlockSpec((tr, C), lambda r: (r, 0))
    return pl.pallas_call(
        body, name=name, grid=(R // tr,), in_specs=[spec] * 4, out_specs=[spec] * 4,
        out_shape=[jax.ShapeDtypeStruct((R, C), F32)] * 4,
        compiler_params=_cparams("parallel"),
    )(w, m, v, g)


def _adamw_slab(name, w, m, v, g, layer, prev):
    L, R, C = w.shape
    tr = SUBLANES
    while 2 * tr * C <= ADAMW_TILE_ELEMS:
        tr *= 2
    tr = _tile(R, tr)
    bc1 = 1.0 - ADAM_B1 ** ADAM_STEP
    bc2 = 1.0 - ADAM_B2 ** ADAM_STEP

    def body(w_ref, m_ref, v_ref, g_ref, *rest):
        go_ref, d_ref, mo_ref, vo_ref = rest[-4:]
        gg = g_ref[...]
        m2 = ADAM_B1 * m_ref[...] + (1.0 - ADAM_B1) * gg
        v2 = ADAM_B2 * v_ref[...] + (1.0 - ADAM_B2) * (gg * gg)
        go_ref[...] = gg
        mo_ref[...] = m2
        vo_ref[...] = v2
        d_ref[...] = -ADAM_LR * ((m2 / bc1) / (jnp.sqrt(v2 / bc2) + ADAM_EPS) + ADAM_WD * w_ref[...])

    slab = pl.BlockSpec((None, tr, C), lambda r: (layer, r, 0))
    n_prev = 0 if prev is None else 4
    return pl.pallas_call(
        body, name=name, grid=(R // tr,),
        in_specs=[slab] * 3 + [pl.BlockSpec((tr, C), lambda r: (r, 0))] + [_HBM] * n_prev,
        out_specs=[slab] * 4,
        out_shape=[jax.ShapeDtypeStruct((L, R, C), F32)] * 4,
        input_output_aliases={4 + i: i for i in range(n_prev)},
        compiler_params=_cparams("parallel"),
    )(w, m, v, g, *(prev or ()))


def _swap_begin(tag, parts):
    lands = [lax.empty((p.shape[0], p.shape[1] // 2, p.shape[2]), p.dtype) for p in parts]
    return _sibling_start(f"rs_swap_start_{tag}", list(parts) + lands, len(parts), _swap_copies(parts))


def _reduce_begin(tag, early, swapping, late, pos, after):
    send, recv, bufs, _ = swapping
    bufs = _sibling_wait(f"rs_swap_wait_{tag}", bufs, send, recv, _swap_copies(early), [after])
    parts = bufs[:len(early)] + list(late)
    got = bufs[len(early):] + list(_swap_halves(f"rs_swap_{tag}", late))
    sums = [_add_my_half(f"rs_add2_{tag}_{i}", p, g, pos) for i, (p, g) in enumerate(zip(parts, got))]
    return _scatter_start(f"rs_scatter_start_{tag}", sums)


def _reduce_middle(tag, started, pos, afters):
    send, recv, sums, lands, _ = started
    sums, lands = _scatter_wait(f"rs_scatter_wait_{tag}", sums, lands, send, recv, afters)
    fulls = [_add_owner(f"rs_add4_{tag}_{i}", s, q, pos) for i, (s, q) in enumerate(zip(sums, lands))]
    return _sibling_start(f"rs_join_start_{tag}", fulls, len(fulls), _join_copies(fulls))


def _reduce_end(tag, joining, afters):
    send, recv, fulls, _ = joining
    return _sibling_wait(f"rs_join_wait_{tag}", fulls, send, recv, _join_copies(fulls), afters)


def _pad_rows(a):
    r = (-a.shape[0]) % SUBLANES
    return jnp.pad(a, ((0, r), (0, 0))) if r else a


def kernel(x, p, norm_mix, norm_mlp, norm_ple, cf_w_pw1, cf_b_pw1, cf_w_dw, cf_b_dw, cf_norm, cf_w_pw2, cf_b_pw2, sc_w_in, sc_w_conv, sc_w_out, mlp_w1, mlp_w2, ple_w_proj, ple_w_gate, norm_final, loss_target, m_norm_mix, m_norm_mlp, m_norm_ple, m_cf_w_pw1, m_cf_b_pw1, m_cf_w_dw, m_cf_b_dw, m_cf_norm, m_cf_w_pw2, m_cf_b_pw2, m_sc_w_in, m_sc_w_conv, m_sc_w_out, m_mlp_w1, m_mlp_w2, m_ple_w_proj, m_ple_w_gate, m_norm_final, v_norm_mix, v_norm_mlp, v_norm_ple, v_cf_w_pw1, v_cf_b_pw1, v_cf_w_dw, v_cf_b_dw, v_cf_norm, v_cf_w_pw2, v_cf_b_pw2, v_sc_w_in, v_sc_w_conv, v_sc_w_out, v_mlp_w1, v_mlp_w2, v_ple_w_proj, v_ple_w_gate, v_norm_final):
    T, D = x.shape[1], x.shape[2]
    KA, KB = cf_w_dw.shape[1], sc_w_conv.shape[1]
    chip = (2 * lax.axis_index("x") + lax.axis_index("y")).astype(jnp.int32)
    pos = tuple(lax.axis_index(ax).astype(jnp.int32).reshape(1) for ax in ("x", "y", "c"))

    params = dict(norm_mix=norm_mix, norm_mlp=norm_mlp, norm_ple=norm_ple, cf_w_pw1=cf_w_pw1, cf_b_pw1=cf_b_pw1,
                  cf_w_dw=cf_w_dw, cf_b_dw=cf_b_dw, cf_norm=cf_norm, cf_w_pw2=cf_w_pw2, cf_b_pw2=cf_b_pw2,
                  sc_w_in=sc_w_in, sc_w_conv=sc_w_conv, sc_w_out=sc_w_out, mlp_w1=mlp_w1, mlp_w2=mlp_w2,
                  ple_w_proj=ple_w_proj, ple_w_gate=ple_w_gate, norm_final=norm_final)
    mom1 = dict(norm_mix=m_norm_mix, norm_mlp=m_norm_mlp, norm_ple=m_norm_ple, cf_w_pw1=m_cf_w_pw1,
                cf_b_pw1=m_cf_b_pw1, cf_w_dw=m_cf_w_dw, cf_b_dw=m_cf_b_dw, cf_norm=m_cf_norm, cf_w_pw2=m_cf_w_pw2,
                cf_b_pw2=m_cf_b_pw2, sc_w_in=m_sc_w_in, sc_w_conv=m_sc_w_conv, sc_w_out=m_sc_w_out,
                mlp_w1=m_mlp_w1, mlp_w2=m_mlp_w2, ple_w_proj=m_ple_w_proj, ple_w_gate=m_ple_w_gate,
                norm_final=m_norm_final)
    mom2 = dict(norm_mix=v_norm_mix, norm_mlp=v_norm_mlp, norm_ple=v_norm_ple, cf_w_pw1=v_cf_w_pw1,
                cf_b_pw1=v_cf_b_pw1, cf_w_dw=v_cf_w_dw, cf_b_dw=v_cf_b_dw, cf_norm=v_cf_norm, cf_w_pw2=v_cf_w_pw2,
                cf_b_pw2=v_cf_b_pw2, sc_w_in=v_sc_w_in, sc_w_conv=v_sc_w_conv, sc_w_out=v_sc_w_out,
                mlp_w1=v_mlp_w1, mlp_w2=v_mlp_w2, ple_w_proj=v_ple_w_proj, ple_w_gate=v_ple_w_gate,
                norm_final=v_norm_final)

    big = ("cf_w_pw1", "cf_w_pw2", "sc_w_in", "sc_w_out", "mlp_w1", "mlp_w2", "ple_w_proj", "ple_w_gate")
    row_sharded = ("cf_w_pw2", "sc_w_out", "mlp_w2", "ple_w_gate")

    def layer_names(i):
        return (["cf_w_pw1", "cf_w_pw2"] if i % 2 == 0 else ["sc_w_in", "sc_w_out"]) + \
            ["mlp_w1", "mlp_w2", "ple_w_proj", "ple_w_gate"]

    def layer_index(i, name):
        return i if name.startswith(("mlp", "ple")) else i // 2

    placed = {(i, nm): _cast_place(f"place_{nm}_{i}", params[nm], layer_index(i, nm), pos)
              for i in range(DEPTH) for nm in layer_names(i)}

    def gather_begin(tag, i, names, after):
        return names, _gather_start(f"gather_start_{tag}", [placed[i, nm] for nm in names], after)

    def gather_end(tag, begun, afters):
        names, (send, recv, bufs, _) = begun
        bufs = _gather_wait(f"gather_wait_{tag}", bufs, send, recv, afters)
        bufs = _gather_forward(f"gather_fwd_{tag}", bufs)
        return {nm: g4.reshape(1, N_SHARDS * g4.shape[1], g4.shape[2]) if nm in row_sharded else g4
                for nm, g4 in zip(names, bufs)}

    conv_small = jnp.concatenate([_pad_rows(cf_w_dw[j]) for j in range(cf_w_dw.shape[0])]
                                 + [_pad_rows(sc_w_conv[j]) for j in range(sc_w_conv.shape[0])], axis=0)
    conv_shards = _gather_shards("gather_conv_w", [(conv_small, None)])[0]
    conv_all = jnp.transpose(conv_shards, (1, 0, 2)).reshape(conv_small.shape[0], D)
    ka_pad = KA + (-KA) % SUBLANES
    kb_pad = KB + (-KB) % SUBLANES
    w_dw_full = [conv_all[j * ka_pad:j * ka_pad + KA] for j in range(cf_w_dw.shape[0])]
    off = cf_w_dw.shape[0] * ka_pad
    w_conv_full = [conv_all[off + j * kb_pad:off + j * kb_pad + KB] for j in range(sc_w_conv.shape[0])]

    def vec(a):
        return a.reshape(1, -1)

    ident = lambda acc: (acc,)

    h = x[0]
    saved = []
    first = gather_begin("0m", 0, layer_names(0)[:2], conv_shards)
    rest = gather_begin("0r", 0, layer_names(0)[2:], first[1][2][0])
    later = [placed[i, nm] for i in range(1, DEPTH) for nm in layer_names(i)]
    W = [gather_end("0m", first, [h] + later)]
    for i in range(DEPTH):
        j = i // 2
        wl = W[i]
        s = dict(h=h)
        g_mix = vec(norm_mix[i])
        if i + 1 < DEPTH:
            nxt = gather_begin(f"{i + 1}", i + 1, layer_names(i + 1), wl[layer_names(i)[0]])
            g_mix = g_mix + nxt[1][3][0, 0]
        s["u"] = _rms_fwd(f"rms_mix_{i}", h, g_mix)
        if i % 2 == 0:
            s["a"] = _mm_nn(f"cf_pw1_{i}", s["u"], wl["cf_w_pw1"], lambda acc, b: (acc + b,), [BF16],
                            extras=[(vec(cf_b_pw1[j]), "n")])[0]
            s["v2"], s["v4"] = _cf_conv_fwd(f"cf_conv_{i}", s["a"], w_dw_full[j], vec(cf_b_dw[j]), vec(cf_norm[j]))
            h1 = _mm_nn(f"cf_pw2_{i}", s["v4"], wl["cf_w_pw2"], lambda acc, b, r: (r + (acc + b),), [F32],
                        extras=[(vec(cf_b_pw2[j]), "n"), (h, "mn")])[0]
        else:
            s["bcv"] = _mm_nn(f"sc_in_{i}", s["u"], wl["sc_w_in"], ident, [BF16])[0]
            s["y"] = _sc_conv_fwd(f"sc_conv_{i}", s["bcv"], w_conv_full[j])[0]
            h1 = _mm_nn(f"sc_out_{i}", s["y"], wl["sc_w_out"], lambda acc, r: (r + acc,), [F32],
                        extras=[(h, "mn")])[0]
        s["h1"] = h1
        if i == 0:
            wl.update(gather_end("0r", rest, [h1]))
        s["u2"] = _rms_fwd(f"rms_mlp_{i}", h1, vec(norm_mlp[i]))
        s["z"], s["hd"] = _mm_nn(f"mlp_w1_{i}", s["u2"], wl["mlp_w1"],
                                 lambda acc: (acc, jnp.square(jnp.maximum(acc, 0.0))), [BF16, BF16])
        h2 = _mm_nn(f"mlp_w2_{i}", s["hd"], wl["mlp_w2"], lambda acc, r: (r + acc,), [F32], extras=[(h1, "mn")])[0]
        s["h2"] = h2
        s["n3"] = _rms_fwd(f"rms_ple_{i}", h2, vec(norm_ple[i]))
        s["p"] = p[i, 0]
        s["e"] = _mm_nn(f"ple_proj_{i}", s["p"], wl["ple_w_proj"], ident, [BF16])[0]
        h, s["q"] = _mm_nn(f"ple_gate_{i}", s["n3"], wl["ple_w_gate"],
                           lambda acc, r, e: (r + _sigmoid(acc) * e.astype(F32), acc), [F32, BF16],
                           extras=[(h2, "mn"), (s["e"], "mn")])
        saved.append(s)
        if i + 1 < DEPTH:
            W.append(gather_end(f"{i + 1}", nxt, [h]))

    dh, dh16, dg_final, loss_cols = _loss_bwd("loss_bwd", h, vec(norm_final), loss_target[0])
    small = {"norm_final": dg_final, "loss": loss_cols}
    adam = {nm: None for nm in big}

    def reduce_names(i):
        names = layer_names(i)
        return names[2:] + names[:2]

    def update_layer(i, joining, afters):
        for nm, g in zip(reduce_names(i), _reduce_end(f"{i}", joining, afters)):
            l = layer_index(i, nm)
            adam[nm] = _adamw_slab(f"adamw_{nm}_{l}", params[nm], mom1[nm], mom2[nm], g, l, adam[nm])

    def shard_major(names, parts):
        return [pt.reshape(N_SHARDS, pt.shape[1] // N_SHARDS, pt.shape[2]) if nm in row_sharded else pt
                for nm, pt in zip(names, parts)]

    scattered = joining = None
    for i in reversed(range(DEPTH)):
        j = i // 2
        wl, s = W[i], saved[i]
        dq, de = _ple_elem_bwd(f"ple_elem_bwd_{i}", dh, s["q"], s["e"])
        d_proj = _mm_tn(f"ple_proj_dw_{i}", s["p"], de, N_SHARDS)
        d_gate = _mm_tn(f"ple_gate_dw_{i}", s["n3"], dq, 1)
        dn3 = _mm_nt(f"ple_gate_dx_{i}", dq, wl["ple_w_gate"], ident, [F32])[0]
        g_ple = vec(norm_ple[i])
        if joining is not None:
            g_ple = g_ple + joining[3][0, 0]
        dh, dh16, small[f"norm_ple_{i}"] = _rms_bwd(f"rms_ple_bwd_{i}", s["h2"], g_ple, dn3, dh)

        d_w2 = _mm_tn(f"mlp_w2_dw_{i}", s["hd"], dh16, 1)
        dz = _mm_nt(f"mlp_w2_dx_{i}", dh16, wl["mlp_w2"],
                    lambda acc, z: (acc * (2.0 * jnp.maximum(z.astype(F32), 0.0)),), [BF16], extras=[s["z"]])[0]
        d_w1 = _mm_tn(f"mlp_w1_dw_{i}", s["u2"], dz, N_SHARDS)
        du2 = _mm_nt(f"mlp_w1_dx_{i}", dz, wl["mlp_w1"], ident, [F32])[0]
        early = shard_major(reduce_names(i)[:4], [d_w1, d_w2, d_proj, d_gate])
        swapping = _swap_begin(f"{i}", early)
        g_mlp = vec(norm_mlp[i]) + swapping[3][0, 0]
        if i % 2 == 0:
            dh, dh16, small[f"norm_mlp_{i}"], small[f"cf_b_pw2_{j}"] = _rms_bwd(
                f"rms_mlp_bwd_{i}", s["h1"], g_mlp, du2, dh, want_colsum=True)
            d_mix_out = _mm_tn(f"cf_pw2_dw_{i}", s["v4"], dh16, 1)
            dv4 = _mm_nt(f"cf_pw2_dx_{i}", dh16, wl["cf_w_pw2"], ident, [F32])[0]
            dv2, small[f"cf_norm_{j}"], small[f"cf_b_dw_{j}"] = _cf_norm_bwd(
                f"cf_norm_bwd_{i}", s["v2"], vec(cf_norm[j]), dv4)
            da, small[f"cf_w_dw_{j}"], db1 = _cf_conv_bwd(f"cf_conv_bwd_{i}", dv2, s["a"], w_dw_full[j])
            small[f"cf_b_pw1_{j}"] = db1.reshape(2, D)
            d_mix_in = _mm_tn(f"cf_pw1_dw_{i}", s["u"], da, N_SHARDS)
            du = _mm_nt(f"cf_pw1_dx_{i}", da, wl["cf_w_pw1"], ident, [F32])[0]
        else:
            dh, dh16, small[f"norm_mlp_{i}"] = _rms_bwd(f"rms_mlp_bwd_{i}", s["h1"], g_mlp, du2, dh)
            d_mix_out = _mm_tn(f"sc_out_dw_{i}", s["y"], dh16, 1)
            dy = _mm_nt(f"sc_out_dx_{i}", dh16, wl["sc_w_out"], ident, [F32])[0]
            da, small[f"sc_w_conv_{j}"] = _sc_conv_bwd(f"sc_conv_bwd_{i}", dy, s["bcv"], w_conv_full[j])
            d_mix_in = _mm_tn(f"sc_in_dw_{i}", s["u"], da, N_SHARDS)
            du = _mm_nt(f"sc_in_dx_{i}", da, wl["sc_w_in"], ident, [F32])[0]

        late = shard_major(reduce_names(i)[4:], [d_mix_in, d_mix_out])
        started = _reduce_begin(f"{i}", early, swapping, late, pos, du)
        token = started[4]
        dh, dh16, small[f"norm_mix_{i}"] = _rms_bwd(f"rms_mix_bwd_{i}", s["h"], vec(norm_mix[i]) + token[0, 0], du, dh)
        begun = _reduce_middle(f"{i + 1}", scattered, pos, [token]) if scattered is not None else None
        if joining is not None:
            update_layer(i + 2, joining, [token] + ([begun[3]] if begun is not None else []))
        joining, scattered = begun, started
    grad_x = dh.reshape(x.shape)

    order = sorted(small)
    pieces, where, row = [], {}, 0
    for nm in order:
        pc = _pad_rows(small[nm])
        where[nm] = (row, small[nm].shape[0])
        row += pc.shape[0]
        pieces.append(pc)
    update_layer(1, joining, [token])
    updated = [res[3] for res in adam.values() if res is not None]
    total, loss_tile = _sum_over_devices("small_allsum", jnp.concatenate(pieces, axis=0), where["loss"][0], updated)
    loss = loss_tile[0, 0]
    joining = _reduce_middle("0", scattered, pos, [total] + updated)

    def small_sum(nm):
        r0, nr = where[nm]
        return total[r0:r0 + nr]

    def my_cols(a):
        return lax.dynamic_slice_in_dim(a, chip * (D // N_SHARDS), D // N_SHARDS, axis=1)

    g_small = {
        "norm_mix": jnp.concatenate([small_sum(f"norm_mix_{i}") for i in range(DEPTH)], axis=0),
        "norm_mlp": jnp.concatenate([small_sum(f"norm_mlp_{i}") for i in range(DEPTH)], axis=0),
        "norm_ple": jnp.concatenate([small_sum(f"norm_ple_{i}") for i in range(DEPTH)], axis=0),
        "cf_b_pw1": jnp.stack([small_sum(f"cf_b_pw1_{j}").reshape(2 * D) for j in range(DEPTH // 2)]),
        "cf_w_dw": jnp.stack([my_cols(small_sum(f"cf_w_dw_{j}")) for j in range(DEPTH // 2)]),
        "cf_b_dw": jnp.concatenate([small_sum(f"cf_b_dw_{j}") for j in range(DEPTH // 2)], axis=0),
        "cf_norm": jnp.concatenate([small_sum(f"cf_norm_{j}") for j in range(DEPTH // 2)], axis=0),
        "cf_b_pw2": jnp.concatenate([small_sum(f"cf_b_pw2_{j}") for j in range(DEPTH // 2)], axis=0),
        "sc_w_conv": jnp.stack([my_cols(small_sum(f"sc_w_conv_{j}")) for j in range(DEPTH // 2)]),
        "norm_final": small_sum("norm_final").reshape(D),
    }

    names_out = ["norm_mix", "norm_mlp", "norm_ple", "cf_w_pw1", "cf_b_pw1", "cf_w_dw", "cf_b_dw", "cf_norm",
                 "cf_w_pw2", "cf_b_pw2", "sc_w_in", "sc_w_conv", "sc_w_out", "mlp_w1", "mlp_w2", "ple_w_proj",
                 "ple_w_gate", "norm_final"]
    grad, delta, new_m, new_v = {}, {}, {}, {}
    for nm in names_out:
        if nm in big:
            continue
        w = params[nm]
        cols = w.shape[-1] if w.ndim > 1 else w.shape[0]
        two_d = lambda a: a.reshape(-1, cols)
        res = _adamw(f"adamw_{nm}", two_d(w), two_d(mom1[nm]), two_d(mom2[nm]), two_d(g_small[nm]))
        grad[nm], delta[nm], new_m[nm], new_v[nm] = [r.reshape(w.shape) for r in res]
    update_layer(0, joining, list(delta.values()))
    for nm in big:
        grad[nm], delta[nm], new_m[nm], new_v[nm] = adam[nm]

    return (loss, grad_x, *[grad[n] for n in names_out], *[delta[n] for n in names_out],
            *[new_m[n] for n in names_out], *[new_v[n] for n in names_out])
```

```python
import jax
import jax.numpy as jnp
from jax import lax
from jax.experimental import pallas as pl
from jax.experimental.pallas import tpu as pltpu

F32 = jnp.float32
BF16 = jnp.bfloat16

EPS = 1e-6
ADAM_LR = 0.001
ADAM_B1 = 0.9
ADAM_B2 = 0.999
ADAM_EPS = 1e-08
ADAM_WD = 0.01
ADAM_STEP = 10

DEPTH = 4
N_SHARDS = 4
N_DEVICES = 8
V7X_VMEM_LIMIT_BYTES = 56 * 1024 * 1024
SUBLANES = 8
MESH = pl.DeviceIdType.MESH

MM_TM = 1024
MM_TN = 1024
MM_TK = 2048
MM_TW_K = 1024
MM_TW_T = 4096
MM_TX_K = 1024
MM_TX_N = 2048
MM_TX_K_SHARDS = 512
MM_TX_K_LONG = 256
MM_TX_LONG_N = 4096
MM_TN_CROWDED = 512
MM_LONG_K = 8192
MM_TN_LONG_K = 256
ADAMW_TILE_ELEMS = 256 * 2048

CONV_ROW_CHUNK = 32
CONV_LANE_CHUNK = 512
CONV_TILE_ROWS = 128
CONV_FWD_TILE_ROWS = 256
ROW_TILE = 256
LIGHT_ROW_TILE = 512


def _tile(dim, pref):
    if dim <= pref:
        return dim
    t = pref
    while dim % t:
        t //= 2
    return t


def _cparams(*sem):
    return pltpu.CompilerParams(dimension_semantics=sem, vmem_limit_bytes=V7X_VMEM_LIMIT_BYTES)


def _sigmoid(x):
    return 0.5 * (jnp.tanh(0.5 * x) + 1.0)


def _rms_r(x):
    return lax.rsqrt(jnp.mean(x * x, axis=-1, keepdims=True) + EPS)


def _mm_nn(name, a, b3, epilogue, out_dtypes, extras=()):
    M, K = a.shape
    S, Kb, Ns = b3.shape
    assert Kb == K
    N = S * Ns
    crowded = sum(kind == "mn" for _, kind in extras) > 1
    tm, tn, tk = _tile(M, MM_TM), _tile(Ns, MM_TN_CROWDED if crowded else MM_TN), _tile(K, MM_TK)
    if K >= MM_LONG_K:
        tn, tk = _tile(Ns, MM_TN_LONG_K), K
    per = Ns // tn
    nk = K // tk
    in_specs = [pl.BlockSpec((tm, tk), lambda i, j, k: (i, k)),
                pl.BlockSpec((None, tk, tn), lambda i, j, k: (j // per, k, j % per))]
    for _, kind in extras:
        if kind == "mn":
            in_specs.append(pl.BlockSpec((tm, tn), lambda i, j, k: (i, j)))
        else:
            in_specs.append(pl.BlockSpec((1, tn), lambda i, j, k: (0, j)))
    n_ex, n_o = len(extras), len(out_dtypes)

    def body(*refs):
        a_ref, b_ref = refs[:2]
        ex = refs[2:2 + n_ex]
        outs = refs[2 + n_ex:2 + n_ex + n_o]
        part = jnp.dot(a_ref[...].astype(BF16), b_ref[...], preferred_element_type=F32)

        def finish(acc):
            res = epilogue(acc, *[e[...] for e in ex])
            for r, o in zip(res, outs):
                o[...] = r.astype(o.dtype)

        if nk == 1:
            finish(part)
        else:
            acc_ref = refs[-1]
            k = pl.program_id(2)

            @pl.when(k == 0)
            def _():
                acc_ref[...] = part

            @pl.when(k > 0)
            def _():
                acc_ref[...] += part

            @pl.when(k == nk - 1)
            def _():
                finish(acc_ref[...])

    res = pl.pallas_call(
        body, name=name, grid=(M // tm, N // tn, nk),
        in_specs=in_specs,
        out_specs=[pl.BlockSpec((tm, tn), lambda i, j, k: (i, j)) for _ in out_dtypes],
        out_shape=[jax.ShapeDtypeStruct((M, N), dt) for dt in out_dtypes],
        scratch_shapes=[pltpu.VMEM((tm, tn), F32)] if nk > 1 else [],
        compiler_params=_cparams("parallel", "parallel", "arbitrary"),
    )(a, b3, *[e for e, _ in extras])
    return res


def _mm_nt_shards(name, g, w3, epilogue, out_dtypes, extras):
    M, N = g.shape
    S, K, Ns = w3.shape
    tm = _tile(M, MM_TM)
    tkk = _tile(K, MM_TX_K_LONG if N > MM_TX_LONG_N else MM_TX_K_SHARDS)
    n_ex, n_o = len(extras), len(out_dtypes)

    def body(*refs):
        g_ref = refs[0]
        w_refs = refs[1:1 + S]
        ex = refs[1 + S:1 + S + n_ex]
        outs = refs[1 + S + n_ex:1 + S + n_ex + n_o]
        acc = None
        for s in range(S):
            part = lax.dot_general(g_ref[:, s * Ns:(s + 1) * Ns].astype(BF16), w_refs[s][...],
                                   (((1,), (1,)), ((), ())), preferred_element_type=F32)
            acc = part if acc is None else acc + part
        for r, o in zip(epilogue(acc, *[e[...] for e in ex]), outs):
            o[...] = r.astype(o.dtype)

    return pl.pallas_call(
        body, name=name, grid=(M // tm, K // tkk),
        in_specs=[pl.BlockSpec((tm, N), lambda i, kk: (i, 0))]
        + [pl.BlockSpec((None, tkk, Ns), lambda i, kk, s=s: (s, kk, 0)) for s in range(S)]
        + [pl.BlockSpec((tm, tkk), lambda i, kk: (i, kk)) for _ in extras],
        out_specs=[pl.BlockSpec((tm, tkk), lambda i, kk: (i, kk)) for _ in out_dtypes],
        out_shape=[jax.ShapeDtypeStruct((M, K), dt) for dt in out_dtypes],
        compiler_params=_cparams("parallel", "parallel"),
    )(g, *([w3] * S), *extras)


def _mm_nt(name, g, w3, epilogue, out_dtypes, extras=()):
    M, N = g.shape
    S, K, Ns = w3.shape
    assert S * Ns == N
    if S > 1:
        return _mm_nt_shards(name, g, w3, epilogue, out_dtypes, extras)
    tm, tn, tkk = _tile(M, MM_TM), _tile(Ns, MM_TX_N), _tile(K, MM_TX_K)
    per = Ns // tn
    nn = N // tn
    n_ex, n_o = len(extras), len(out_dtypes)

    def body(*refs):
        g_ref, w_ref = refs[:2]
        ex = refs[2:2 + n_ex]
        outs = refs[2 + n_ex:2 + n_ex + n_o]
        part = lax.dot_general(g_ref[...].astype(BF16), w_ref[...], (((1,), (1,)), ((), ())),
                               preferred_element_type=F32)

        def finish(acc):
            res = epilogue(acc, *[e[...] for e in ex])
            for r, o in zip(res, outs):
                o[...] = r.astype(o.dtype)

        if nn == 1:
            finish(part)
        else:
            acc_ref = refs[-1]
            n = pl.program_id(2)

            @pl.when(n == 0)
            def _():
                acc_ref[...] = part

            @pl.when(n > 0)
            def _():
                acc_ref[...] += part

            @pl.when(n == nn - 1)
            def _():
                finish(acc_ref[...])

    return pl.pallas_call(
        body, name=name, grid=(M // tm, K // tkk, nn),
        in_specs=[pl.BlockSpec((tm, tn), lambda i, kk, n: (i, n)),
                  pl.BlockSpec((None, tkk, tn), lambda i, kk, n: (n // per, kk, n % per))]
        + [pl.BlockSpec((tm, tkk), lambda i, kk, n: (i, kk)) for _ in extras],
        out_specs=[pl.BlockSpec((tm, tkk), lambda i, kk, n: (i, kk)) for _ in out_dtypes],
        out_shape=[jax.ShapeDtypeStruct((M, K), dt) for dt in out_dtypes],
        scratch_shapes=[pltpu.VMEM((tm, tkk), F32)] if nn > 1 else [],
        compiler_params=_cparams("parallel", "parallel", "arbitrary"),
    )(g, w3, *extras)


def _mm_tn(name, a, g, n_shards):
    T, K = a.shape
    _, N = g.shape
    Ns = N // n_shards
    tk, tn, tt = _tile(K, MM_TW_K), _tile(Ns, MM_TN), _tile(T, MM_TW_T)
    per = Ns // tn
    nt = T // tt

    def body(a_ref, g_ref, o_ref, *scratch):
        part = lax.dot_general(a_ref[...].astype(BF16), g_ref[...].astype(BF16), (((0,), (0,)), ((), ())),
                               preferred_element_type=F32)
        if nt == 1:
            o_ref[...] = part.astype(o_ref.dtype)
            return
        acc_ref, = scratch
        t = pl.program_id(2)

        @pl.when(t == 0)
        def _():
            acc_ref[...] = part

        @pl.when(t > 0)
        def _():
            acc_ref[...] += part

        @pl.when(t == nt - 1)
        def _():
            o_ref[...] = acc_ref[...].astype(o_ref.dtype)

    return pl.pallas_call(
        body, name=name, grid=(K // tk, N // tn, nt),
        in_specs=[pl.BlockSpec((tt, tk), lambda i, j, t: (t, i)),
                  pl.BlockSpec((tt, tn), lambda i, j, t: (t, j))],
        out_specs=pl.BlockSpec((None, tk, tn), lambda i, j, t: (j // per, i, j % per)),
        out_shape=jax.ShapeDtypeStruct((n_shards, K, Ns), BF16),
        scratch_shapes=[pltpu.VMEM((tk, tn), F32)] if nt > 1 else [],
        compiler_params=_cparams("parallel", "parallel", "arbitrary"),
    )(a, g)


def _rowwise(name, fn, ins, outs, accs=(), scratch=(), tt=ROW_TILE):
    T = next(a.shape[0] for a, kind in ins if kind == "row")
    tt = _tile(T, tt)
    n = T // tt
    in_specs = []
    for a, kind in ins:
        w = a.shape[1]
        if kind == "row":
            in_specs.append(pl.BlockSpec((tt, w), lambda i: (i, 0)))
        elif kind == "vec":
            in_specs.append(pl.BlockSpec(a.shape, lambda i: (0, 0)))
        elif kind[0] == "prev":
            pad = kind[1]
            in_specs.append(pl.BlockSpec((pad, w), lambda i, q=tt // pad: (jnp.maximum(i * q - 1, 0), 0)))
        else:
            pad = kind[1]
            in_specs.append(pl.BlockSpec((pad, w), lambda i, q=tt // pad, last=T // pad - 1:
                                         (jnp.minimum((i + 1) * q, last), 0)))
    n_in, n_out, n_acc = len(ins), len(outs), len(accs)

    def body(*refs):
        i = pl.program_id(0)
        in_refs = refs[:n_in]
        out_refs = refs[n_in:n_in + n_out]
        acc_refs = refs[n_in + n_out:n_in + n_out + n_acc]
        scr = refs[n_in + n_out + n_acc:]
        if n_acc:
            @pl.when(i == 0)
            def _():
                for r in acc_refs:
                    r[...] = jnp.zeros_like(r)
        fn(i, n, in_refs, out_refs, acc_refs, scr)

    res = pl.pallas_call(
        body, name=name, grid=(n,),
        in_specs=in_specs,
        out_specs=[pl.BlockSpec((tt, w), lambda i: (i, 0)) for w, _ in outs]
        + [pl.BlockSpec((r, w), lambda i: (0, 0)) for r, w in accs],
        out_shape=[jax.ShapeDtypeStruct((T, w), dt) for w, dt in outs]
        + [jax.ShapeDtypeStruct((r, w), F32) for r, w in accs],
        scratch_shapes=list(scratch),
        compiler_params=_cparams("arbitrary"),
    )(*[a for a, _ in ins])
    return res


def _colsum(x):
    return jnp.sum(x, axis=0, keepdims=True)


def _rms_fwd(name, h, g):
    D = h.shape[1]

    def fn(i, n, ins, outs, accs, scr):
        x = ins[0][...]
        outs[0][...] = (x * _rms_r(x) * ins[1][...]).astype(BF16)

    return _rowwise(name, fn, [(h, "row"), (g, "vec")], [(D, BF16)], tt=LIGHT_ROW_TILE)[0]


def _rms_bwd(name, h, g, du, dh_in, want_colsum=False):
    D = h.shape[1]

    def fn(i, n, ins, outs, accs, scr):
        x = ins[0][...]
        gg = ins[1][...]
        d = ins[2][...].astype(F32)
        r = _rms_r(x)
        xn = x * r
        t = d * gg
        dh = ins[3][...] + r * (t - xn * jnp.mean(t * xn, axis=-1, keepdims=True))
        outs[0][...] = dh
        outs[1][...] = dh.astype(BF16)
        accs[0][...] += _colsum(d * xn)
        if want_colsum:
            accs[1][...] += _colsum(dh)

    return _rowwise(name, fn, [(h, "row"), (g, "vec"), (du, "row"), (dh_in, "row")],
                    [(D, F32), (D, BF16)], accs=[(1, D)] * (2 if want_colsum else 1))


def _loss_bwd(name, h, g, tgt):
    D = h.shape[1]

    def fn(i, n, ins, outs, accs, scr):
        x = ins[0][...]
        gg = ins[1][...]
        r = _rms_r(x)
        xn = x * r
        err = xn * gg - ins[2][...]
        dy = err / D
        t = dy * gg
        dh = r * (t - xn * jnp.mean(t * xn, axis=-1, keepdims=True))
        outs[0][...] = dh
        outs[1][...] = dh.astype(BF16)
        accs[0][...] += _colsum(dy * xn)
        accs[1][...] += _colsum(err * err)

    return _rowwise(name, fn, [(h, "row"), (g, "vec"), (tgt, "row")], [(D, F32), (D, BF16)],
                    accs=[(1, D), (1, D)])


def _ple_elem_bwd(name, dh, q, e):
    D = dh.shape[1]

    def fn(i, n, ins, outs, accs, scr):
        d = ins[0][...]
        s = _sigmoid(ins[1][...].astype(F32))
        ee = ins[2][...].astype(F32)
        outs[0][...] = (d * ee * s * (1.0 - s)).astype(BF16)
        outs[1][...] = (d * s).astype(BF16)

    return _rowwise(name, fn, [(dh, "row"), (q, "row"), (e, "row")], [(D, BF16), (D, BF16)], tt=LIGHT_ROW_TILE)


def _cf_norm_bwd(name, v2, g, dv4):
    D = v2.shape[1]

    def fn(i, n, ins, outs, accs, scr):
        x = ins[0][...]
        gg = ins[1][...]
        r = _rms_r(x)
        xn = x * r
        v3 = xn * gg
        s = _sigmoid(v3)
        dv3 = ins[2][...].astype(F32) * (s * (1.0 + v3 * (1.0 - s)))
        t = dv3 * gg
        dv2 = r * (t - xn * jnp.mean(t * xn, axis=-1, keepdims=True))
        outs[0][...] = dv2
        accs[0][...] += _colsum(dv3 * xn)
        accs[1][...] += _colsum(dv2)

    return _rowwise(name, fn, [(v2, "row"), (g, "vec"), (dv4, "row")], [(D, F32)], accs=[(1, D), (1, D)])


def _chunks(tt, width):
    cc = min(CONV_LANE_CHUNK, width)
    rc = min(CONV_ROW_CHUNK, tt)
    for c0 in range(0, width, cc):
        for r0 in range(0, tt, rc):
            yield r0, rc, c0, cc


def _n_shifts(n_taps):
    return min(SUBLANES - 1, n_taps - 1)


def _shifted_scratch(n_taps, rows, width):
    return pltpu.VMEM((_n_shifts(n_taps), rows, width), F32)


def _shift_window(win_ref, sh_ref, n_taps, sign):
    rows = win_ref.shape[0] - SUBLANES
    width = win_ref.shape[1]
    cc = min(CONV_LANE_CHUNK, width)
    for b in range(1, _n_shifts(n_taps) + 1):
        off = SUBLANES - b if sign < 0 else b
        for c0 in range(0, width, cc):
            sh_ref[b - 1, 0:rows, c0:c0 + cc] = win_ref[off:off + rows, c0:c0 + cc]


def _tap(win_ref, sh_ref, base, sign, s, r0, rc, c0, cc):
    a, b = divmod(s, SUBLANES)
    if b == 0:
        row = base + r0 + sign * SUBLANES * a
        return win_ref[row:row + rc, c0:c0 + cc]
    row = base + r0 - SUBLANES * (a + 1) if sign < 0 else base + r0 + SUBLANES * a
    return sh_ref[b - 1, row:row + rc, c0:c0 + cc]


def _fir(win_ref, sh_ref, w_ref, n_taps, base, sign, tt, width, emit):
    for r0, rc, c0, cc in _chunks(tt, width):
        acc = jnp.zeros((rc, cc), F32)
        for k in range(n_taps):
            acc = acc + w_ref[k:k + 1, c0:c0 + cc] * _tap(win_ref, sh_ref, base, sign, n_taps - 1 - k, r0, rc, c0, cc)
        emit(r0, rc, c0, cc, acc)


def _fir_wgrad(d_ref, win_ref, sh_ref, dw8_ref, n_taps, pad, tt, width):
    for c0 in range(0, width, min(CONV_LANE_CHUNK, width)):
        cc = min(CONV_LANE_CHUNK, width)
        rc = min(CONV_ROW_CHUNK, tt)
        for k in range(n_taps):
            acc = jnp.zeros((SUBLANES, cc), F32)
            for r0 in range(0, tt, rc):
                prod = d_ref[r0:r0 + rc, c0:c0 + cc] * _tap(win_ref, sh_ref, pad, -1, n_taps - 1 - k, r0, rc, c0, cc)
                for q in range(0, rc, SUBLANES):
                    acc = acc + prod[q:q + SUBLANES]
            dw8_ref[SUBLANES * k:SUBLANES * (k + 1), c0:c0 + cc] += acc


def _glu(blk, D):
    return blk[:, :D].astype(F32) * _sigmoid(blk[:, D:].astype(F32))


CF_PAD = 32
SC_PAD = 16


def _cf_conv_fwd(name, a, w_dw, b_dw, g_cf):
    T, D2 = a.shape
    D = D2 // 2
    K = w_dw.shape[0]
    tt = _tile(T, CONV_FWD_TILE_ROWS)

    def fn(i, n, ins, outs, accs, scr):
        a_ref, prev_ref, w_ref, b_ref, g_ref = ins
        win_ref, v2_ref, sh_ref = scr
        win_ref[0:CF_PAD, :] = jnp.where(i > 0, _glu(prev_ref[...], D), 0.0)
        win_ref[CF_PAD:CF_PAD + tt, :] = _glu(a_ref[...], D)
        _shift_window(win_ref, sh_ref, K, -1)

        def emit(r0, rc, c0, cc, acc):
            v2_ref[r0:r0 + rc, c0:c0 + cc] = acc + b_ref[:, c0:c0 + cc]

        _fir(win_ref, sh_ref, w_ref, K, CF_PAD, -1, tt, D, emit)
        v2 = v2_ref[...]
        v3 = v2 * _rms_r(v2) * g_ref[...]
        outs[0][...] = v2
        outs[1][...] = (v3 * _sigmoid(v3)).astype(BF16)

    return _rowwise(name, fn, [(a, "row"), (a, ("prev", CF_PAD)), (w_dw, "vec"), (b_dw, "vec"), (g_cf, "vec")],
                    [(D, F32), (D, BF16)],
                    scratch=[pltpu.VMEM((CF_PAD + tt, D), F32), pltpu.VMEM((tt, D), F32),
                             _shifted_scratch(K, CF_PAD + tt, D)], tt=tt)


def _cf_conv_bwd(name, dv2, a, w_dw):
    T, D2 = a.shape
    D = D2 // 2
    K = w_dw.shape[0]
    tt = _tile(T, CONV_TILE_ROWS)

    def fn(i, n, ins, outs, accs, scr):
        d_ref, dnext_ref, a_ref, prev_ref, w_ref = ins
        v1win_ref, dwin_ref, dv1_ref, dw8_ref, v1sh_ref, dsh_ref = scr

        @pl.when(i == 0)
        def _():
            dw8_ref[...] = jnp.zeros_like(dw8_ref)

        v1win_ref[0:CF_PAD, :] = jnp.where(i > 0, _glu(prev_ref[...], D), 0.0)
        v1win_ref[CF_PAD:CF_PAD + tt, :] = _glu(a_ref[...], D)
        dwin_ref[0:tt, :] = d_ref[...]
        dwin_ref[tt:tt + CF_PAD, :] = jnp.where(i < n - 1, dnext_ref[...], 0.0)
        _shift_window(v1win_ref, v1sh_ref, K, -1)
        _shift_window(dwin_ref, dsh_ref, K, 1)

        def emit(r0, rc, c0, cc, acc):
            dv1_ref[r0:r0 + rc, c0:c0 + cc] = acc

        _fir(dwin_ref, dsh_ref, w_ref, K, 0, 1, tt, D, emit)
        _fir_wgrad(d_ref, v1win_ref, v1sh_ref, dw8_ref, K, CF_PAD, tt, D)

        blk = a_ref[...]
        val = blk[:, :D].astype(F32)
        sg = _sigmoid(blk[:, D:].astype(F32))
        dv1 = dv1_ref[...]
        dval = dv1 * sg
        dgate = dv1 * val * sg * (1.0 - sg)
        outs[0][:, :D] = dval.astype(BF16)
        outs[0][:, D:] = dgate.astype(BF16)
        accs[1][:, :D] += _colsum(dval)
        accs[1][:, D:] += _colsum(dgate)

        @pl.when(i == n - 1)
        def _():
            for k in range(K):
                accs[0][k:k + 1, :] = _colsum(dw8_ref[SUBLANES * k:SUBLANES * (k + 1), :])

    return _rowwise(name, fn, [(dv2, "row"), (dv2, ("next", CF_PAD)), (a, "row"), (a, ("prev", CF_PAD)),
                               (w_dw, "vec")],
                    [(D2, BF16)], accs=[(K, D), (1, D2)],
                    scratch=[pltpu.VMEM((CF_PAD + tt, D), F32), pltpu.VMEM((tt + CF_PAD, D), F32),
                             pltpu.VMEM((tt, D), F32), pltpu.VMEM((SUBLANES * K, D), F32),
                             _shifted_scratch(K, CF_PAD + tt, D), _shifted_scratch(K, tt + CF_PAD, D)], tt=tt)


def _sc_conv_fwd(name, bcv, w_conv):
    T, D3 = bcv.shape
    D = D3 // 3
    K = w_conv.shape[0]
    tt = _tile(T, CONV_TILE_ROWS)

    def cv_of(blk):
        return blk[:, D:2 * D].astype(F32) * blk[:, 2 * D:].astype(F32)

    def fn(i, n, ins, outs, accs, scr):
        x_ref, prev_ref, w_ref = ins
        win_ref, cc_ref, sh_ref = scr
        win_ref[0:SC_PAD, :] = jnp.where(i > 0, cv_of(prev_ref[...]), 0.0)
        win_ref[SC_PAD:SC_PAD + tt, :] = cv_of(x_ref[...])
        _shift_window(win_ref, sh_ref, K, -1)

        def emit(r0, rc, c0, cw, acc):
            cc_ref[r0:r0 + rc, c0:c0 + cw] = acc

        _fir(win_ref, sh_ref, w_ref, K, SC_PAD, -1, tt, D, emit)
        outs[0][...] = (x_ref[:, :D].astype(F32) * cc_ref[...]).astype(BF16)

    return _rowwise(name, fn, [(bcv, "row"), (bcv, ("prev", SC_PAD)), (w_conv, "vec")], [(D, BF16)],
                    scratch=[pltpu.VMEM((SC_PAD + tt, D), F32), pltpu.VMEM((tt, D), F32),
                             _shifted_scratch(K, SC_PAD + tt, D)], tt=tt)


def _sc_conv_bwd(name, dy, bcv, w_conv):
    T, D3 = bcv.shape
    D = D3 // 3
    K = w_conv.shape[0]
    tt = _tile(T, CONV_TILE_ROWS)

    def cv_of(blk):
        return blk[:, D:2 * D].astype(F32) * blk[:, 2 * D:].astype(F32)

    def fn(i, n, ins, outs, accs, scr):
        dy_ref, dynext_ref, x_ref, prev_ref, next_ref, w_ref = ins
        cvwin_ref, dccwin_ref, tmp_ref, dw8_ref, cvsh_ref, dccsh_ref = scr

        @pl.when(i == 0)
        def _():
            dw8_ref[...] = jnp.zeros_like(dw8_ref)

        cvwin_ref[0:SC_PAD, :] = jnp.where(i > 0, cv_of(prev_ref[...]), 0.0)
        cvwin_ref[SC_PAD:SC_PAD + tt, :] = cv_of(x_ref[...])
        _shift_window(cvwin_ref, cvsh_ref, K, -1)

        def emit_cc(r0, rc, c0, cw, acc):
            tmp_ref[r0:r0 + rc, c0:c0 + cw] = acc

        _fir(cvwin_ref, cvsh_ref, w_ref, K, SC_PAD, -1, tt, D, emit_cc)
        dy_v = dy_ref[...].astype(F32)
        outs[0][:, :D] = (dy_v * tmp_ref[...]).astype(BF16)
        dccwin_ref[0:tt, :] = dy_v * x_ref[:, :D].astype(F32)
        dccwin_ref[tt:tt + SC_PAD, :] = jnp.where(
            i < n - 1, dynext_ref[...].astype(F32) * next_ref[:, :D].astype(F32), 0.0)

        _shift_window(dccwin_ref, dccsh_ref, K, 1)

        def emit_dcv(r0, rc, c0, cw, acc):
            tmp_ref[r0:r0 + rc, c0:c0 + cw] = acc

        _fir(dccwin_ref, dccsh_ref, w_ref, K, 0, 1, tt, D, emit_dcv)
        _fir_wgrad(dccwin_ref, cvwin_ref, cvsh_ref, dw8_ref, K, SC_PAD, tt, D)
        dcv = tmp_ref[...]
        outs[0][:, D:2 * D] = (dcv * x_ref[:, 2 * D:].astype(F32)).astype(BF16)
        outs[0][:, 2 * D:] = (dcv * x_ref[:, D:2 * D].astype(F32)).astype(BF16)

        @pl.when(i == n - 1)
        def _():
            for k in range(K):
                accs[0][k:k + 1, :] = _colsum(dw8_ref[SUBLANES * k:SUBLANES * (k + 1), :])

    return _rowwise(name, fn, [(dy, "row"), (dy, ("next", SC_PAD)), (bcv, "row"), (bcv, ("prev", SC_PAD)),
                               (bcv, ("next", SC_PAD)), (w_conv, "vec")],
                    [(D3, BF16)], accs=[(K, D)],
                    scratch=[pltpu.VMEM((SC_PAD + tt, D), F32), pltpu.VMEM((tt + SC_PAD, D), F32),
                             pltpu.VMEM((tt, D), F32), pltpu.VMEM((SUBLANES * K, D), F32),
                             _shifted_scratch(K, SC_PAD + tt, D), _shifted_scratch(K, tt + SC_PAD, D)], tt=tt)


def _place():
    x, y, c = lax.axis_index("x"), lax.axis_index("y"), lax.axis_index("c")
    chips = [(1 - x, y), (x, 1 - y), (1 - x, 1 - y)]
    return x, y, c, 2 * x + y, chips, (x, y, 1 - c)


def _half(rows, which):
    return pl.ds(pl.multiple_of(which * (rows // 2), SUBLANES), rows // 2)


_HBM = pl.BlockSpec(memory_space=pl.ANY)


def _gather_shards(name, items):
    n = len(items)
    shapes = [a.shape[-2:] for a, _ in items]

    def body(*refs):
        srcs, outs = refs[:n], refs[n:2 * n]
        send1, recv1, send2, recv2, lsem = refs[2 * n:]
        x, y, c, k, chips, sib = _place()

        def shard(i):
            return srcs[i] if items[i][1] is None else srcs[i].at[items[i][1]]

        started, locs = [], []
        for i in range(n):
            rows = shapes[i][0]
            lc = pltpu.make_async_copy(shard(i), outs[i].at[k], lsem.at[i])
            lc.start()
            locs.append(lc)
            for j, (cx, cy) in enumerate(chips):
                cp = pltpu.make_async_remote_copy(
                    src_ref=shard(i).at[_half(rows, c)], dst_ref=outs[i].at[k, _half(rows, c)],
                    send_sem=send1.at[i, j], recv_sem=recv1.at[i, j], device_id=(cx, cy, c), device_id_type=MESH)
                cp.start()
                started.append(cp)
        for i in range(n):
            rows = shapes[i][0]
            for j, (cx, cy) in enumerate(chips):
                blk = outs[i].at[2 * cx + cy, _half(rows, c)]
                pltpu.make_async_remote_copy(
                    src_ref=blk, dst_ref=blk, send_sem=send1.at[i, j], recv_sem=recv1.at[i, j],
                    device_id=(cx, cy, c), device_id_type=MESH).wait_recv()
                fw = pltpu.make_async_remote_copy(
                    src_ref=blk, dst_ref=blk, send_sem=send2.at[i, j], recv_sem=recv2.at[i, j],
                    device_id=sib, device_id_type=MESH)
                fw.start()
                started.append(fw)
        for i in range(n):
            rows = shapes[i][0]
            for j, (cx, cy) in enumerate(chips):
                blk = outs[i].at[2 * cx + cy, _half(rows, 1 - c)]
                pltpu.make_async_remote_copy(
                    src_ref=blk, dst_ref=blk, send_sem=send2.at[i, j], recv_sem=recv2.at[i, j],
                    device_id=sib, device_id_type=MESH).wait_recv()
        for cp in started:
            cp.wait_send()
        for lc in locs:
            lc.wait()

    return pl.pallas_call(
        body, name=name,
        in_specs=[_HBM] * n, out_specs=[_HBM] * n,
        out_shape=[jax.ShapeDtypeStruct((N_SHARDS,) + tuple(s), a.dtype) for s, (a, _) in zip(shapes, items)],
        scratch_shapes=[pltpu.SemaphoreType.DMA((n, 3))] * 4 + [pltpu.SemaphoreType.DMA((n,))],
    )(*[a for a, _ in items])


def _cast_place(name, w, layer, pos):
    _, R, C = w.shape
    tr = _tile(R, 256)

    def body(x_ref, y_ref, c_ref, w_ref, o_ref):
        o_ref[...] = w_ref[...].astype(BF16)

    return pl.pallas_call(
        body, name=name,
        grid_spec=pltpu.PrefetchScalarGridSpec(
            num_scalar_prefetch=3, grid=(R // tr,),
            in_specs=[pl.BlockSpec((None, tr, C), lambda r, xr, yr, cr: (layer, r, 0))],
            out_specs=pl.BlockSpec((None, tr, C), lambda r, xr, yr, cr: (2 * xr[0] + yr[0], r, 0))),
        out_shape=jax.ShapeDtypeStruct((N_SHARDS, R, C), BF16),
        compiler_params=_cparams("parallel"),
    )(*pos, w)


_IN_HBM = pl.BlockSpec(memory_space=pltpu.HBM)
_SEM = pl.BlockSpec(memory_space=pltpu.SEMAPHORE)
_SPLIT_COPY_PARAMS = pltpu.CompilerParams(has_side_effects=pltpu.SideEffectType.DATAFLOW_SIDE_EFFECTING)


def _in_hbm(a):
    return pltpu.with_memory_space_constraint(a, pltpu.HBM)


def _gather_copy(ref, i, j, chip_xy, c, k_src, rows, send, recv):
    blk = ref.at[k_src, _half(rows, c)]
    return pltpu.make_async_remote_copy(
        src_ref=blk, dst_ref=blk, send_sem=send.at[3 * i + j], recv_sem=recv.at[3 * i + j],
        device_id=(*chip_xy, c), device_id_type=MESH)


def _gather_start(name, bufs, after):
    n = len(bufs)

    def body(*refs):
        ins = refs[:n]
        send, recv = refs[n + 1], refs[n + 2]
        token = refs[-1]
        x, y, c, k, chips, sib = _place()
        for i in range(n):
            for j, chip_xy in enumerate(chips):
                _gather_copy(ins[i], i, j, chip_xy, c, k, bufs[i].shape[1], send, recv).start()
        token[...] = jnp.zeros_like(token)

    res = pl.pallas_call(
        body, name=name,
        in_specs=[_IN_HBM] * n + [_HBM],
        out_specs=[_SEM, _SEM] + [_IN_HBM] * n + [pl.BlockSpec(memory_space=pltpu.VMEM)],
        out_shape=[pltpu.SemaphoreType.DMA((3 * n,)), pltpu.SemaphoreType.DMA((3 * n,))]
        + [pltpu.HBM(b.shape, b.dtype) for b in bufs] + [jax.ShapeDtypeStruct((SUBLANES, 128), F32)],
        input_output_aliases={i: 2 + i for i in range(n)},
        compiler_params=_SPLIT_COPY_PARAMS,
    )(*[_in_hbm(b) for b in bufs], after)
    return res[0], res[1], list(res[2:2 + n]), res[-1]


def _gather_wait(name, bufs, send, recv, afters):
    n = len(bufs)

    def body(*refs):
        ins = refs[:n]
        send_ref, recv_ref = refs[n], refs[n + 1]
        x, y, c, k, chips, sib = _place()
        for i in range(n):
            for j, chip_xy in enumerate(chips):
                rows = bufs[i].shape[1]
                _gather_copy(ins[i], i, j, chip_xy, c, k, rows, send_ref, recv_ref).wait_send()
                _gather_copy(ins[i], i, j, chip_xy, c, 2 * chip_xy[0] + chip_xy[1], rows, send_ref, recv_ref).wait_recv()

    return pl.pallas_call(
        body, name=name,
        in_specs=[_IN_HBM] * n + [_SEM, _SEM] + [_HBM] * len(afters),
        out_specs=[_IN_HBM] * n,
        out_shape=[pltpu.HBM(b.shape, b.dtype) for b in bufs],
        input_output_aliases={i: i for i in range(n)},
        compiler_params=_SPLIT_COPY_PARAMS,
    )(*bufs, send, recv, *afters)


def _gather_forward(name, bufs):
    n = len(bufs)

    def body(*refs):
        outs = refs[n:2 * n]
        send, recv = refs[2 * n:]
        x, y, c, k, chips, sib = _place()
        started = []
        for i in range(n):
            rows = bufs[i].shape[1]
            for j, (cx, cy) in enumerate(chips):
                blk = outs[i].at[2 * cx + cy, _half(rows, c)]
                fw = pltpu.make_async_remote_copy(
                    src_ref=blk, dst_ref=blk, send_sem=send.at[i, j], recv_sem=recv.at[i, j],
                    device_id=sib, device_id_type=MESH)
                fw.start()
                started.append(fw)
        for i in range(n):
            rows = bufs[i].shape[1]
            for j, (cx, cy) in enumerate(chips):
                blk = outs[i].at[2 * cx + cy, _half(rows, 1 - c)]
                pltpu.make_async_remote_copy(
                    src_ref=blk, dst_ref=blk, send_sem=send.at[i, j], recv_sem=recv.at[i, j],
                    device_id=sib, device_id_type=MESH).wait_recv()
        for cp in started:
            cp.wait_send()

    return pl.pallas_call(
        body, name=name, in_specs=[_HBM] * n, out_specs=[_HBM] * n,
        out_shape=[jax.ShapeDtypeStruct(b.shape, b.dtype) for b in bufs],
        input_output_aliases={i: i for i in range(n)},
        scratch_shapes=[pltpu.SemaphoreType.DMA((n, 3))] * 2,
    )(*bufs)


def _swap_halves(name, parts):
    n = len(parts)

    def body(*refs):
        srcs, outs = refs[:n], refs[n:2 * n]
        send, recv = refs[2 * n:]
        x, y, c, k, chips, sib = _place()
        cps = []
        for i in range(n):
            rows = parts[i].shape[1]
            cp = pltpu.make_async_remote_copy(
                src_ref=srcs[i].at[:, _half(rows, 1 - c)], dst_ref=outs[i],
                send_sem=send.at[i], recv_sem=recv.at[i], device_id=sib, device_id_type=MESH)
            cp.start()
            cps.append(cp)
        for cp in cps:
            cp.wait()

    return pl.pallas_call(
        body, name=name, in_specs=[_HBM] * n, out_specs=[_HBM] * n,
        out_shape=[jax.ShapeDtypeStruct((p.shape[0], p.shape[1] // 2, p.shape[2]), p.dtype) for p in parts],
        scratch_shapes=[pltpu.SemaphoreType.DMA((n,))] * 2,
    )(*parts)


def _sibling_start(name, bufs, n_copies, make):
    nb = len(bufs)

    def body(*refs):
        send, recv = refs[nb], refs[nb + 1]
        token = refs[-1]
        x, y, c, k, chips, sib = _place()
        for cp in make(refs[:nb], c, sib, send, recv):
            cp.start()
        token[...] = jnp.zeros_like(token)

    res = pl.pallas_call(
        body, name=name,
        in_specs=[_IN_HBM] * nb,
        out_specs=[_SEM, _SEM] + [_IN_HBM] * nb + [pl.BlockSpec(memory_space=pltpu.VMEM)],
        out_shape=[pltpu.SemaphoreType.DMA((n_copies,)), pltpu.SemaphoreType.DMA((n_copies,))]
        + [pltpu.HBM(b.shape, b.dtype) for b in bufs] + [jax.ShapeDtypeStruct((SUBLANES, 128), F32)],
        input_output_aliases={i: 2 + i for i in range(nb)},
        compiler_params=_SPLIT_COPY_PARAMS,
    )(*[_in_hbm(b) for b in bufs])
    return res[0], res[1], list(res[2:2 + nb]), res[-1]


def _sibling_wait(name, bufs, send, recv, make, afters):
    nb = len(bufs)

    def body(*refs):
        x, y, c, k, chips, sib = _place()
        for cp in make(refs[:nb], c, sib, refs[nb], refs[nb + 1]):
            cp.wait_send()
            cp.wait_recv()

    return list(pl.pallas_call(
        body, name=name,
        in_specs=[_IN_HBM] * nb + [_SEM, _SEM] + [_HBM] * len(afters),
        out_specs=[_IN_HBM] * nb,
        out_shape=[pltpu.HBM(b.shape, b.dtype) for b in bufs],
        input_output_aliases={i: i for i in range(nb)},
        compiler_params=_SPLIT_COPY_PARAMS,
    )(*bufs, send, recv, *afters))


def _swap_copies(parts):
    n = len(parts)

    def make(refs, c, sib, send, recv):
        return [pltpu.make_async_remote_copy(
            src_ref=refs[i].at[:, _half(parts[i].shape[1], 1 - c)], dst_ref=refs[n + i],
            send_sem=send.at[i], recv_sem=recv.at[i], device_id=sib, device_id_type=MESH) for i in range(n)]

    return make


def _join_copies(fulls):
    def make(refs, c, sib, send, recv):
        cps = []
        for i, f in enumerate(fulls):
            blk = refs[i].at[_half(f.shape[0], c)]
            cps.append(pltpu.make_async_remote_copy(
                src_ref=blk, dst_ref=blk, send_sem=send.at[i], recv_sem=recv.at[i],
                device_id=sib, device_id_type=MESH))
        return cps

    return make


def _scatter_copy(src_ref, land_ref, i, j, chip_xy, c, send, recv):
    return pltpu.make_async_remote_copy(
        src_ref=src_ref.at[2 * chip_xy[0] + chip_xy[1]], dst_ref=land_ref.at[j],
        send_sem=send.at[3 * i + j], recv_sem=recv.at[3 * i + j], device_id=(*chip_xy, c), device_id_type=MESH)


def _scatter_start(name, sums):
    n = len(sums)
    lands = [lax.empty((3,) + s.shape[1:], s.dtype) for s in sums]

    def body(*refs):
        srcs, lnds = refs[:n], refs[n:2 * n]
        send, recv = refs[2 * n], refs[2 * n + 1]
        token = refs[-1]
        x, y, c, k, chips, sib = _place()
        for i in range(n):
            for j, chip_xy in enumerate(chips):
                _scatter_copy(srcs[i], lnds[i], i, j, chip_xy, c, send, recv).start()
        token[...] = jnp.zeros_like(token)

    res = pl.pallas_call(
        body, name=name,
        in_specs=[_IN_HBM] * (2 * n),
        out_specs=[_SEM, _SEM] + [_IN_HBM] * (2 * n) + [pl.BlockSpec(memory_space=pltpu.VMEM)],
        out_shape=[pltpu.SemaphoreType.DMA((3 * n,)), pltpu.SemaphoreType.DMA((3 * n,))]
        + [pltpu.HBM(a.shape, a.dtype) for a in list(sums) + lands] + [jax.ShapeDtypeStruct((SUBLANES, 128), F32)],
        input_output_aliases={i: 2 + i for i in range(2 * n)},
        compiler_params=_SPLIT_COPY_PARAMS,
    )(*[_in_hbm(a) for a in list(sums) + lands])
    return res[0], res[1], list(res[2:2 + n]), list(res[2 + n:2 + 2 * n]), res[-1]


def _scatter_wait(name, sums, lands, send, recv, afters):
    n = len(sums)

    def body(*refs):
        srcs, lnds = refs[:n], refs[n:2 * n]
        send_ref, recv_ref = refs[2 * n], refs[2 * n + 1]
        x, y, c, k, chips, sib = _place()
        for i in range(n):
            for j, chip_xy in enumerate(chips):
                cp = _scatter_copy(srcs[i], lnds[i], i, j, chip_xy, c, send_ref, recv_ref)
                cp.wait_send()
                cp.wait_recv()

    res = pl.pallas_call(
        body, name=name,
        in_specs=[_IN_HBM] * (2 * n) + [_SEM, _SEM] + [_HBM] * len(afters),
        out_specs=[_IN_HBM] * (2 * n),
        out_shape=[pltpu.HBM(a.shape, a.dtype) for a in list(sums) + list(lands)],
        input_output_aliases={i: i for i in range(2 * n)},
        compiler_params=_SPLIT_COPY_PARAMS,
    )(*sums, *lands, send, recv, *afters)
    return list(res[:n]), list(res[n:])


def _sum_over_devices(name, buf, loss_row, afters):
    R, D = buf.shape
    n_after = len(afters)

    def body(x_ref, *rest):
        all_ref, tot_ref, loss_ref, send_sems, recv_sems, local_sem = rest[n_after:]
        x, y, c, k, chips, sib = _place()
        me = (x, y, c)

        def block(px, py, pc):
            return all_ref.at[4 * px + 2 * py + pc]

        def copy(kk, blk, to, src=None):
            return pltpu.make_async_remote_copy(
                src_ref=block(*blk) if src is None else src, dst_ref=block(*blk),
                send_sem=send_sems.at[kk], recv_sem=recv_sems.at[kk], device_id=to, device_id_type=MESH)

        mine = pltpu.make_async_copy(x_ref, block(*me), local_sem)
        mine.start()
        first = [copy(0, me, sib, src=x_ref)]
        first += [copy(1 + j, me, (*chip, c), src=x_ref) for j, chip in enumerate(chips)]
        for cp in first:
            cp.start()
        passed = [copy(4 + j, (*chip, c), sib) for j, chip in enumerate(chips)]
        for j, chip in enumerate(chips):
            copy(1 + j, (*chip, c), me).wait_recv()
            passed[j].start()
        copy(0, sib, me).wait_recv()
        for j, chip in enumerate(chips):
            copy(4 + j, (*chip, 1 - c), me).wait_recv()
        for cp in first + passed:
            cp.wait_send()
        mine.wait()
        rc = _tile(R, 32)
        for r0 in range(0, R, rc):
            tot = all_ref[0, r0:r0 + rc, :]
            for d in range(1, N_DEVICES):
                tot = tot + all_ref[d, r0:r0 + rc, :]
            tot_ref[r0:r0 + rc, :] = tot
        loss = 0.5 * jnp.sum(tot_ref[loss_row:loss_row + 1, :]) / D
        loss_ref[...] = jnp.full(loss_ref.shape, loss, F32)

    vm = pl.BlockSpec(memory_space=pltpu.VMEM)
    return pl.pallas_call(
        body, name=name, in_specs=[vm] + [_HBM] * n_after, out_specs=[vm, vm, vm],
        out_shape=[jax.ShapeDtypeStruct((N_DEVICES, R, D), F32), jax.ShapeDtypeStruct((R, D), F32),
                   jax.ShapeDtypeStruct((SUBLANES, 128), F32)],
        scratch_shapes=[pltpu.SemaphoreType.DMA((7,)), pltpu.SemaphoreType.DMA((7,)), pltpu.SemaphoreType.DMA],
        compiler_params=pltpu.CompilerParams(vmem_limit_bytes=V7X_VMEM_LIMIT_BYTES),
    )(buf, *afters)[1:]


def _add_my_half(name, part, got, pos):
    S, R, C = part.shape
    R2 = R // 2
    tr = _tile(R2, 1024)
    q = R2 // tr

    def body(x_ref, y_ref, c_ref, p_ref, g_ref, o_ref):
        o_ref[...] = (p_ref[...].astype(F32) + g_ref[...].astype(F32)).astype(o_ref.dtype)

    return pl.pallas_call(
        body, name=name,
        grid_spec=pltpu.PrefetchScalarGridSpec(
            num_scalar_prefetch=3, grid=(S, q),
            in_specs=[pl.BlockSpec((None, tr, C), lambda s, r, xr, yr, cr: (s, cr[0] * q + r, 0)),
                      pl.BlockSpec((None, tr, C), lambda s, r, xr, yr, cr: (s, r, 0))],
            out_specs=pl.BlockSpec((None, tr, C), lambda s, r, xr, yr, cr: (s, r, 0))),
        out_shape=jax.ShapeDtypeStruct((S, R2, C), BF16),
        compiler_params=_cparams("parallel", "parallel"),
    )(*pos, part, got)


def _add_owner(name, sums, got, pos):
    _, R2, C = sums.shape
    tr = _tile(R2, 512)
    q = R2 // tr

    def body(x_ref, y_ref, c_ref, s_ref, g_ref, o_ref):
        acc = s_ref[...].astype(F32)
        for j in range(3):
            acc = acc + g_ref[j].astype(F32)
        o_ref[...] = acc

    return pl.pallas_call(
        body, name=name,
        grid_spec=pltpu.PrefetchScalarGridSpec(
            num_scalar_prefetch=3, grid=(q,),
            in_specs=[pl.BlockSpec((None, tr, C), lambda r, xr, yr, cr: (2 * xr[0] + yr[0], r, 0)),
                      pl.BlockSpec((3, tr, C), lambda r, xr, yr, cr: (0, r, 0))],
            out_specs=pl.BlockSpec((tr, C), lambda r, xr, yr, cr: (cr[0] * q + r, 0))),
        out_shape=jax.ShapeDtypeStruct((2 * R2, C), F32),
        compiler_params=_cparams("parallel"),
    )(*pos, sums, got)


def _adamw(name, w, m, v, g):
    R, C = w.shape
    tr = SUBLANES
    while 2 * tr * C <= ADAMW_TILE_ELEMS:
        tr *= 2
    tr = _tile(R, tr)
    bc1 = 1.0 - ADAM_B1 ** ADAM_STEP
    bc2 = 1.0 - ADAM_B2 ** ADAM_STEP

    def body(w_ref, m_ref, v_ref, g_ref, go_ref, d_ref, mo_ref, vo_ref):
        gg = g_ref[...]
        m2 = ADAM_B1 * m_ref[...] + (1.0 - ADAM_B1) * gg
        v2 = ADAM_B2 * v_ref[...] + (1.0 - ADAM_B2) * (gg * gg)
        go_ref[...] = gg
        mo_ref[...] = m2
        vo_ref[...] = v2
        d_ref[...] = -ADAM_LR * ((m2 / bc1) / (jnp.sqrt(v2 / bc2) + ADAM_EPS) + ADAM_WD * w_ref[...])

    spec = pl.BlockSpec((tr, C), lambda r: (r, 0))
    return pl.pallas_call(
        body, name=name, grid=(R // tr,), in_specs=[spec] * 4, out_specs=[spec] * 4,
        out_shape=[jax.ShapeDtypeStruct((R, C), F32)] * 4,
        compiler_params=_cparams("parallel"),
    )(w, m, v, g)


def _adamw_slab(name, w, m, v, g, layer, prev):
    L, R, C = w.shape
    tr = SUBLANES
    while 2 * tr * C <= ADAMW_TILE_ELEMS:
        tr *= 2
    tr = _tile(R, tr)
    bc1 = 1.0 - ADAM_B1 ** ADAM_STEP
    bc2 = 1.0 - ADAM_B2 ** ADAM_STEP

    def body(w_ref, m_ref, v_ref, g_ref, *rest):
        go_ref, d_ref, mo_ref, vo_ref = rest[-4:]
        gg = g_ref[...]
        m2 = ADAM_B1 * m_ref[...] + (1.0 - ADAM_B1) * gg
        v2 = ADAM_B2 * v_ref[...] + (1.0 - ADAM_B2) * (gg * gg)
        go_ref[...] = gg
        mo_ref[...] = m2
        vo_ref[...] = v2
        d_ref[...] = -ADAM_LR * ((m2 / bc1) / (jnp.sqrt(v2 / bc2) + ADAM_EPS) + ADAM_WD * w_ref[...])

    slab = pl.BlockSpec((None, tr, C), lambda r: (layer, r, 0))
    n_prev = 0 if prev is None else 4
    return pl.pallas_call(
        body, name=name, grid=(R // tr,),
        in_specs=[slab] * 3 + [pl.BlockSpec((tr, C), lambda r: (r, 0))] + [_HBM] * n_prev,
        out_specs=[slab] * 4,
        out_shape=[jax.ShapeDtypeStruct((L, R, C), F32)] * 4,
        input_output_aliases={4 + i: i for i in range(n_prev)},
        compiler_params=_cparams("parallel"),
    )(w, m, v, g, *(prev or ()))


def _swap_begin(tag, parts):
    lands = [lax.empty((p.shape[0], p.shape[1] // 2, p.shape[2]), p.dtype) for p in parts]
    return _sibling_start(f"rs_swap_start_{tag}", list(parts) + lands, len(parts), _swap_copies(parts))


def _reduce_begin(tag, early, swapping, late, pos, after):
    send, recv, bufs, _ = swapping
    bufs = _sibling_wait(f"rs_swap_wait_{tag}", bufs, send, recv, _swap_copies(early), [after])
    parts = bufs[:len(early)] + list(late)
    got = bufs[len(early):] + list(_swap_halves(f"rs_swap_{tag}", late))
    sums = [_add_my_half(f"rs_add2_{tag}_{i}", p, g, pos) for i, (p, g) in enumerate(zip(parts, got))]
    return _scatter_start(f"rs_scatter_start_{tag}", sums)


def _reduce_middle(tag, started, pos, afters):
    send, recv, sums, lands, _ = started
    sums, lands = _scatter_wait(f"rs_scatter_wait_{tag}", sums, lands, send, recv, afters)
    fulls = [_add_owner(f"rs_add4_{tag}_{i}", s, q, pos) for i, (s, q) in enumerate(zip(sums, lands))]
    return _sibling_start(f"rs_join_start_{tag}", fulls, len(fulls), _join_copies(fulls))


def _reduce_end(tag, joining, afters):
    send, recv, fulls, _ = joining
    return _sibling_wait(f"rs_join_wait_{tag}", fulls, send, recv, _join_copies(fulls), afters)


def _pad_rows(a):
    r = (-a.shape[0]) % SUBLANES
    return jnp.pad(a, ((0, r), (0, 0))) if r else a


def kernel(x, p, norm_mix, norm_mlp, norm_ple, cf_w_pw1, cf_b_pw1, cf_w_dw, cf_b_dw, cf_norm, cf_w_pw2, cf_b_pw2, sc_w_in, sc_w_conv, sc_w_out, mlp_w1, mlp_w2, ple_w_proj, ple_w_gate, norm_final, loss_target, m_norm_mix, m_norm_mlp, m_norm_ple, m_cf_w_pw1, m_cf_b_pw1, m_cf_w_dw, m_cf_b_dw, m_cf_norm, m_cf_w_pw2, m_cf_b_pw2, m_sc_w_in, m_sc_w_conv, m_sc_w_out, m_mlp_w1, m_mlp_w2, m_ple_w_proj, m_ple_w_gate, m_norm_final, v_norm_mix, v_norm_mlp, v_norm_ple, v_cf_w_pw1, v_cf_b_pw1, v_cf_w_dw, v_cf_b_dw, v_cf_norm, v_cf_w_pw2, v_cf_b_pw2, v_sc_w_in, v_sc_w_conv, v_sc_w_out, v_mlp_w1, v_mlp_w2, v_ple_w_proj, v_ple_w_gate, v_norm_final):
    T, D = x.shape[1], x.shape[2]
    KA, KB = cf_w_dw.shape[1], sc_w_conv.shape[1]
    chip = (2 * lax.axis_index("x") + lax.axis_index("y")).astype(jnp.int32)
    pos = tuple(lax.axis_index(ax).astype(jnp.int32).reshape(1) for ax in ("x", "y", "c"))

    params = dict(norm_mix=norm_mix, norm_mlp=norm_mlp, norm_ple=norm_ple, cf_w_pw1=cf_w_pw1, cf_b_pw1=cf_b_pw1,
                  cf_w_dw=cf_w_dw, cf_b_dw=cf_b_dw, cf_norm=cf_norm, cf_w_pw2=cf_w_pw2, cf_b_pw2=cf_b_pw2,
                  sc_w_in=sc_w_in, sc_w_conv=sc_w_conv, sc_w_out=sc_w_out, mlp_w1=mlp_w1, mlp_w2=mlp_w2,
                  ple_w_proj=ple_w_proj, ple_w_gate=ple_w_gate, norm_final=norm_final)
    mom1 = dict(norm_mix=m_norm_mix, norm_mlp=m_norm_mlp, norm_ple=m_norm_ple, cf_w_pw1=m_cf_w_pw1,
                cf_b_pw1=m_cf_b_pw1, cf_w_dw=m_cf_w_dw, cf_b_dw=m_cf_b_dw, cf_norm=m_cf_norm, cf_w_pw2=m_cf_w_pw2,
                cf_b_pw2=m_cf_b_pw2, sc_w_in=m_sc_w_in, sc_w_conv=m_sc_w_conv, sc_w_out=m_sc_w_out,
                mlp_w1=m_mlp_w1, mlp_w2=m_mlp_w2, ple_w_proj=m_ple_w_proj, ple_w_gate=m_ple_w_gate,
                norm_final=m_norm_final)
    mom2 = dict(norm_mix=v_norm_mix, norm_mlp=v_norm_mlp, norm_ple=v_norm_ple, cf_w_pw1=v_cf_w_pw1,
                cf_b_pw1=v_cf_b_pw1, cf_w_dw=v_cf_w_dw, cf_b_dw=v_cf_b_dw, cf_norm=v_cf_norm, cf_w_pw2=v_cf_w_pw2,
                cf_b_pw2=v_cf_b_pw2, sc_w_in=v_sc_w_in, sc_w_conv=v_sc_w_conv, sc_w_out=v_sc_w_out,
                mlp_w1=v_mlp_w1, mlp_w2=v_mlp_w2, ple_w_proj=v_ple_w_proj, ple_w_gate=v_ple_w_gate,
                norm_final=v_norm_final)

    big = ("cf_w_pw1", "cf_w_pw2", "sc_w_in", "sc_w_out", "mlp_w1", "mlp_w2", "ple_w_proj", "ple_w_gate")
    row_sharded = ("cf_w_pw2", "sc_w_out", "mlp_w2", "ple_w_gate")

    def layer_names(i):
        return (["cf_w_pw1", "cf_w_pw2"] if i % 2 == 0 else ["sc_w_in", "sc_w_out"]) + \
            ["mlp_w1", "mlp_w2", "ple_w_proj", "ple_w_gate"]

    def layer_index(i, name):
        return i if name.startswith(("mlp", "ple")) else i // 2

    placed = {(i, nm): _cast_place(f"place_{nm}_{i}", params[nm], layer_index(i, nm), pos)
              for i in range(DEPTH) for nm in layer_names(i)}

    def gather_begin(tag, i, names, after):
        return names, _gather_start(f"gather_start_{tag}", [placed[i, nm] for nm in names], after)

    def gather_end(tag, begun, afters):
        names, (send, recv, bufs, _) = begun
        bufs = _gather_wait(f"gather_wait_{tag}", bufs, send, recv, afters)
        bufs = _gather_forward(f"gather_fwd_{tag}", bufs)
        return {nm: g4.reshape(1, N_SHARDS * g4.shape[1], g4.shape[2]) if nm in row_sharded else g4
                for nm, g4 in zip(names, bufs)}

    conv_small = jnp.concatenate([_pad_rows(cf_w_dw[j]) for j in range(cf_w_dw.shape[0])]
                                 + [_pad_rows(sc_w_conv[j]) for j in range(sc_w_conv.shape[0])], axis=0)
    conv_shards = _gather_shards("gather_conv_w", [(conv_small, None)])[0]
    conv_all = jnp.transpose(conv_shards, (1, 0, 2)).reshape(conv_small.shape[0], D)
    ka_pad = KA + (-KA) % SUBLANES
    kb_pad = KB + (-KB) % SUBLANES
    w_dw_full = [conv_all[j * ka_pad:j * ka_pad + KA] for j in range(cf_w_dw.shape[0])]
    off = cf_w_dw.shape[0] * ka_pad
    w_conv_full = [conv_all[off + j * kb_pad:off + j * kb_pad + KB] for j in range(sc_w_conv.shape[0])]

    def vec(a):
        return a.reshape(1, -1)

    ident = lambda acc: (acc,)

    h = x[0]
    saved = []
    first = gather_begin("0m", 0, layer_names(0)[:2], conv_shards)
    rests = {0: gather_begin("0r", 0, layer_names(0)[2:], first[1][2][0])}
    later = [placed[i, nm] for i in range(1, DEPTH) for nm in layer_names(i)]
    W = [gather_end("0m", first, [h] + later)]
    for i in range(DEPTH):
        j = i // 2
        wl = W[i]
        s = dict(h=h)
        g_mix = vec(norm_mix[i])
        if i + 1 < DEPTH:
            names_n = layer_names(i + 1)
            if i + 1 == 1:
                nxt = gather_begin("1m", 1, names_n[:2], wl[layer_names(i)[0]])
                rests[1] = gather_begin("1r", 1, names_n[2:], nxt[1][2][0])
                g_mix = g_mix + rests[1][1][3][0, 0]
            else:
                nxt = gather_begin(f"{i + 1}", i + 1, names_n, wl[layer_names(i)[0]])
            g_mix = g_mix + nxt[1][3][0, 0]
        s["u"] = _rms_fwd(f"rms_mix_{i}", h, g_mix)
        if i % 2 == 0:
            s["a"] = _mm_nn(f"cf_pw1_{i}", s["u"], wl["cf_w_pw1"], lambda acc, b: (acc + b,), [BF16],
                            extras=[(vec(cf_b_pw1[j]), "n")])[0]
            s["v2"], s["v4"] = _cf_conv_fwd(f"cf_conv_{i}", s["a"], w_dw_full[j], vec(cf_b_dw[j]), vec(cf_norm[j]))
            h1 = _mm_nn(f"cf_pw2_{i}", s["v4"], wl["cf_w_pw2"], lambda acc, b, r: (r + (acc + b),), [F32],
                        extras=[(vec(cf_b_pw2[j]), "n"), (h, "mn")])[0]
        else:
            s["bcv"] = _mm_nn(f"sc_in_{i}", s["u"], wl["sc_w_in"], ident, [BF16])[0]
            s["y"] = _sc_conv_fwd(f"sc_conv_{i}", s["bcv"], w_conv_full[j])[0]
            h1 = _mm_nn(f"sc_out_{i}", s["y"], wl["sc_w_out"], lambda acc, r: (r + acc,), [F32],
                        extras=[(h, "mn")])[0]
        s["h1"] = h1
        if i in rests:
            wl.update(gather_end(f"{i}r", rests[i], [h1]))
        s["u2"] = _rms_fwd(f"rms_mlp_{i}", h1, vec(norm_mlp[i]))
        s["z"], s["hd"] = _mm_nn(f"mlp_w1_{i}", s["u2"], wl["mlp_w1"],
                                 lambda acc: (acc, jnp.square(jnp.maximum(acc, 0.0))), [BF16, BF16])
        h2 = _mm_nn(f"mlp_w2_{i}", s["hd"], wl["mlp_w2"], lambda acc, r: (r + acc,), [F32], extras=[(h1, "mn")])[0]
        s["h2"] = h2
        s["n3"] = _rms_fwd(f"rms_ple_{i}", h2, vec(norm_ple[i]))
        s["p"] = p[i, 0]
        s["e"] = _mm_nn(f"ple_proj_{i}", s["p"], wl["ple_w_proj"], ident, [BF16])[0]
        h, s["q"] = _mm_nn(f"ple_gate_{i}", s["n3"], wl["ple_w_gate"],
                           lambda acc, r, e: (r + _sigmoid(acc) * e.astype(F32), acc), [F32, BF16],
                           extras=[(h2, "mn"), (s["e"], "mn")])
        saved.append(s)
        if i + 1 < DEPTH:
            W.append(gather_end("1m" if i + 1 == 1 else f"{i + 1}", nxt, [h]))

    dh, dh16, dg_final, loss_cols = _loss_bwd("loss_bwd", h, vec(norm_final), loss_target[0])
    small = {"norm_final": dg_final, "loss": loss_cols}
    adam = {nm: None for nm in big}

    def reduce_names(i):
        names = layer_names(i)
        return names[2:] + names[:2]

    def update_layer(i, joining, afters):
        for nm, g in zip(reduce_names(i), _reduce_end(f"{i}", joining, afters)):
            l = layer_index(i, nm)
            adam[nm] = _adamw_slab(f"adamw_{nm}_{l}", params[nm], mom1[nm], mom2[nm], g, l, adam[nm])

    def shard_major(names, parts):
        return [pt.reshape(N_SHARDS, pt.shape[1] // N_SHARDS, pt.shape[2]) if nm in row_sharded else pt
                for nm, pt in zip(names, parts)]

    scattered = joining = None
    for i in reversed(range(DEPTH)):
        j = i // 2
        wl, s = W[i], saved[i]
        dq, de = _ple_elem_bwd(f"ple_elem_bwd_{i}", dh, s["q"], s["e"])
        d_proj = _mm_tn(f"ple_proj_dw_{i}", s["p"], de, N_SHARDS)
        d_gate = _mm_tn(f"ple_gate_dw_{i}", s["n3"], dq, 1)
        dn3 = _mm_nt(f"ple_gate_dx_{i}", dq, wl["ple_w_gate"], ident, [F32])[0]
        g_ple = vec(norm_ple[i])
        if joining is not None:
            g_ple = g_ple + joining[3][0, 0]
        dh, dh16, small[f"norm_ple_{i}"] = _rms_bwd(f"rms_ple_bwd_{i}", s["h2"], g_ple, dn3, dh)

        d_w2 = _mm_tn(f"mlp_w2_dw_{i}", s["hd"], dh16, 1)
        dz = _mm_nt(f"mlp_w2_dx_{i}", dh16, wl["mlp_w2"],
                    lambda acc, z: (acc * (2.0 * jnp.maximum(z.astype(F32), 0.0)),), [BF16], extras=[s["z"]])[0]
        d_w1 = _mm_tn(f"mlp_w1_dw_{i}", s["u2"], dz, N_SHARDS)
        du2 = _mm_nt(f"mlp_w1_dx_{i}", dz, wl["mlp_w1"], ident, [F32])[0]
        early = shard_major(reduce_names(i)[:4], [d_w1, d_w2, d_proj, d_gate])
        swapping = _swap_begin(f"{i}", early)
        g_mlp = vec(norm_mlp[i]) + swapping[3][0, 0]
        if i % 2 == 0:
            dh, dh16, small[f"norm_mlp_{i}"], small[f"cf_b_pw2_{j}"] = _rms_bwd(
                f"rms_mlp_bwd_{i}", s["h1"], g_mlp, du2, dh, want_colsum=True)
            d_mix_out = _mm_tn(f"cf_pw2_dw_{i}", s["v4"], dh16, 1)
            dv4 = _mm_nt(f"cf_pw2_dx_{i}", dh16, wl["cf_w_pw2"], ident, [F32])[0]
            dv2, small[f"cf_norm_{j}"], small[f"cf_b_dw_{j}"] = _cf_norm_bwd(
                f"cf_norm_bwd_{i}", s["v2"], vec(cf_norm[j]), dv4)
            da, small[f"cf_w_dw_{j}"], db1 = _cf_conv_bwd(f"cf_conv_bwd_{i}", dv2, s["a"], w_dw_full[j])
            small[f"cf_b_pw1_{j}"] = db1.reshape(2, D)
            d_mix_in = _mm_tn(f"cf_pw1_dw_{i}", s["u"], da, N_SHARDS)
            du = _mm_nt(f"cf_pw1_dx_{i}", da, wl["cf_w_pw1"], ident, [F32])[0]
        else:
            dh, dh16, small[f"norm_mlp_{i}"] = _rms_bwd(f"rms_mlp_bwd_{i}", s["h1"], g_mlp, du2, dh)
            d_mix_out = _mm_tn(f"sc_out_dw_{i}", s["y"], dh16, 1)
            dy = _mm_nt(f"sc_out_dx_{i}", dh16, wl["sc_w_out"], ident, [F32])[0]
            da, small[f"sc_w_conv_{j}"] = _sc_conv_bwd(f"sc_conv_bwd_{i}", dy, s["bcv"], w_conv_full[j])
            d_mix_in = _mm_tn(f"sc_in_dw_{i}", s["u"], da, N_SHARDS)
            du = _mm_nt(f"sc_in_dx_{i}", da, wl["sc_w_in"], ident, [F32])[0]

        late = shard_major(reduce_names(i)[4:], [d_mix_in, d_mix_out])
        started = _reduce_begin(f"{i}", early, swapping, late, pos, du)
        token = started[4]
        dh, dh16, small[f"norm_mix_{i}"] = _rms_bwd(f"rms_mix_bwd_{i}", s["h"], vec(norm_mix[i]) + token[0, 0], du, dh)
        begun = _reduce_middle(f"{i + 1}", scattered, pos, [token]) if scattered is not None else None
        if joining is not None:
            update_layer(i + 2, joining, [token] + ([begun[3]] if begun is not None else []))
        joining, scattered = begun, started
    grad_x = dh.reshape(x.shape)

    order = sorted(small)
    pieces, where, row = [], {}, 0
    for nm in order:
        pc = _pad_rows(small[nm])
        where[nm] = (row, small[nm].shape[0])
        row += pc.shape[0]
        pieces.append(pc)
    update_layer(1, joining, [token])
    updated = [res[3] for res in adam.values() if res is not None]
    total, loss_tile = _sum_over_devices("small_allsum", jnp.concatenate(pieces, axis=0), where["loss"][0], updated)
    loss = loss_tile[0, 0]
    joining = _reduce_middle("0", scattered, pos, [total] + updated)

    def small_sum(nm):
        r0, nr = where[nm]
        return total[r0:r0 + nr]

    def my_cols(a):
        return lax.dynamic_slice_in_dim(a, chip * (D // N_SHARDS), D // N_SHARDS, axis=1)

    g_small = {
        "norm_mix": jnp.concatenate([small_sum(f"norm_mix_{i}") for i in range(DEPTH)], axis=0),
        "norm_mlp": jnp.concatenate([small_sum(f"norm_mlp_{i}") for i in range(DEPTH)], axis=0),
        "norm_ple": jnp.concatenate([small_sum(f"norm_ple_{i}") for i in range(DEPTH)], axis=0),
        "cf_b_pw1": jnp.stack([small_sum(f"cf_b_pw1_{j}").reshape(2 * D) for j in range(DEPTH // 2)]),
        "cf_w_dw": jnp.stack([my_cols(small_sum(f"cf_w_dw_{j}")) for j in range(DEPTH // 2)]),
        "cf_b_dw": jnp.concatenate([small_sum(f"cf_b_dw_{j}") for j in range(DEPTH // 2)], axis=0),
        "cf_norm": jnp.concatenate([small_sum(f"cf_norm_{j}") for j in range(DEPTH // 2)], axis=0),
        "cf_b_pw2": jnp.concatenate([small_sum(f"cf_b_pw2_{j}") for j in range(DEPTH // 2)], axis=0),
        "sc_w_conv": jnp.stack([my_cols(small_sum(f"sc_w_conv_{j}")) for j in range(DEPTH // 2)]),
        "norm_final": small_sum("norm_final").reshape(D),
    }

    names_out = ["norm_mix", "norm_mlp", "norm_ple", "cf_w_pw1", "cf_b_pw1", "cf_w_dw", "cf_b_dw", "cf_norm",
                 "cf_w_pw2", "cf_b_pw2", "sc_w_in", "sc_w_conv", "sc_w_out", "mlp_w1", "mlp_w2", "ple_w_proj",
                 "ple_w_gate", "norm_final"]
    grad, delta, new_m, new_v = {}, {}, {}, {}
    for nm in names_out:
        if nm in big:
            continue
        w = params[nm]
        cols = w.shape[-1] if w.ndim > 1 else w.shape[0]
        two_d = lambda a: a.reshape(-1, cols)
        res = _adamw(f"adamw_{nm}", two_d(w), two_d(mom1[nm]), two_d(mom2[nm]), two_d(g_small[nm]))
        grad[nm], delta[nm], new_m[nm], new_v[nm] = [r.reshape(w.shape) for r in res]
    update_layer(0, joining, list(delta.values()))
    for nm in big:
        grad[nm], delta[nm], new_m[nm], new_v[nm] = adam[nm]

    return (loss, grad_x, *[grad[n] for n in names_out], *[delta[n] for n in names_out],
            *[new_m[n] for n in names_out], *[new_v[n] for n in names_out])
```

```python
import jax
import jax.numpy as jnp
from jax import lax
from jax.experimental import pallas as pl
from jax.experimental.pallas import tpu as pltpu

F32 = jnp.float32
BF16 = jnp.bfloat16

EPS = 1e-6
ADAM_LR = 0.001
ADAM_B1 = 0.9
ADAM_B2 = 0.999
ADAM_EPS = 1e-08
ADAM_WD = 0.01
ADAM_STEP = 10

DEPTH = 4
N_SHARDS = 4
N_DEVICES = 8
V7X_VMEM_LIMIT_BYTES = 56 * 1024 * 1024
SUBLANES = 8
MESH = pl.DeviceIdType.MESH

MM_TM = 1024
MM_TN = 1024
MM_TK = 2048
MM_TW_K = 1024
MM_TW_T = 4096
MM_TX_K = 1024
MM_TX_N = 2048
MM_TX_K_SHARDS = 512
MM_TX_K_LONG = 256
MM_TX_LONG_N = 4096
MM_TN_CROWDED = 512
MM_LONG_K = 8192
MM_TN_LONG_K = 256
ADAMW_TILE_ELEMS = 256 * 2048

CONV_ROW_CHUNK = 32
CONV_LANE_CHUNK = 512
CONV_TILE_ROWS = 128
CONV_FWD_TILE_ROWS = 256
ROW_TILE = 256
LIGHT_ROW_TILE = 512


def _tile(dim, pref):
    if dim <= pref:
        return dim
    t = pref
    while dim % t:
        t //= 2
    return t


def _cparams(*sem):
    return pltpu.CompilerParams(dimension_semantics=sem, vmem_limit_bytes=V7X_VMEM_LIMIT_BYTES)


def _sigmoid(x):
    return 0.5 * (jnp.tanh(0.5 * x) + 1.0)


def _rms_r(x):
    return lax.rsqrt(jnp.mean(x * x, axis=-1, keepdims=True) + EPS)


def _mm_nn(name, a, b3, epilogue, out_dtypes, extras=()):
    M, K = a.shape
    S, Kb, Ns = b3.shape
    assert Kb == K
    N = S * Ns
    crowded = sum(kind == "mn" for _, kind in extras) > 1
    tm, tn, tk = _tile(M, MM_TM), _tile(Ns, MM_TN_CROWDED if crowded else MM_TN), _tile(K, MM_TK)
    if K >= MM_LONG_K:
        tn, tk = _tile(Ns, MM_TN_LONG_K), K
    per = Ns // tn
    nk = K // tk
    in_specs = [pl.BlockSpec((tm, tk), lambda i, j, k: (i, k)),
                pl.BlockSpec((None, tk, tn), lambda i, j, k: (j // per, k, j % per))]
    for _, kind in extras:
        if kind == "mn":
            in_specs.append(pl.BlockSpec((tm, tn), lambda i, j, k: (i, j)))
        else:
            in_specs.append(pl.BlockSpec((1, tn), lambda i, j, k: (0, j)))
    n_ex, n_o = len(extras), len(out_dtypes)

    def body(*refs):
        a_ref, b_ref = refs[:2]
        ex = refs[2:2 + n_ex]
        outs = refs[2 + n_ex:2 + n_ex + n_o]
        part = jnp.dot(a_ref[...].astype(BF16), b_ref[...], preferred_element_type=F32)

        def finish(acc):
            res = epilogue(acc, *[e[...] for e in ex])
            for r, o in zip(res, outs):
                o[...] = r.astype(o.dtype)

        if nk == 1:
            finish(part)
        else:
            acc_ref = refs[-1]
            k = pl.program_id(2)

            @pl.when(k == 0)
            def _():
                acc_ref[...] = part

            @pl.when(k > 0)
            def _():
                acc_ref[...] += part

            @pl.when(k == nk - 1)
            def _():
                finish(acc_ref[...])

    res = pl.pallas_call(
        body, name=name, grid=(M // tm, N // tn, nk),
        in_specs=in_specs,
        out_specs=[pl.BlockSpec((tm, tn), lambda i, j, k: (i, j)) for _ in out_dtypes],
        out_shape=[jax.ShapeDtypeStruct((M, N), dt) for dt in out_dtypes],
        scratch_shapes=[pltpu.VMEM((tm, tn), F32)] if nk > 1 else [],
        compiler_params=_cparams("parallel", "parallel", "arbitrary"),
    )(a, b3, *[e for e, _ in extras])
    return res


def _mm_nt_shards(name, g, w3, epilogue, out_dtypes, extras):
    M, N = g.shape
    S, K, Ns = w3.shape
    tm = _tile(M, MM_TM)
    tkk = _tile(K, MM_TX_K_LONG if N > MM_TX_LONG_N else MM_TX_K_SHARDS)
    n_ex, n_o = len(extras), len(out_dtypes)

    def body(*refs):
        g_ref = refs[0]
        w_refs = refs[1:1 + S]
        ex = refs[1 + S:1 + S + n_ex]
        outs = refs[1 + S + n_ex:1 + S + n_ex + n_o]
        acc = None
        for s in range(S):
            part = lax.dot_general(g_ref[:, s * Ns:(s + 1) * Ns].astype(BF16), w_refs[s][...],
                                   (((1,), (1,)), ((), ())), preferred_element_type=F32)
            acc = part if acc is None else acc + part
        for r, o in zip(epilogue(acc, *[e[...] for e in ex]), outs):
            o[...] = r.astype(o.dtype)

    return pl.pallas_call(
        body, name=name, grid=(M // tm, K // tkk),
        in_specs=[pl.BlockSpec((tm, N), lambda i, kk: (i, 0))]
        + [pl.BlockSpec((None, tkk, Ns), lambda i, kk, s=s: (s, kk, 0)) for s in range(S)]
        + [pl.BlockSpec((tm, tkk), lambda i, kk: (i, kk)) for _ in extras],
        out_specs=[pl.BlockSpec((tm, tkk), lambda i, kk: (i, kk)) for _ in out_dtypes],
        out_shape=[jax.ShapeDtypeStruct((M, K), dt) for dt in out_dtypes],
        compiler_params=_cparams("parallel", "parallel"),
    )(g, *([w3] * S), *extras)


def _mm_nt(name, g, w3, epilogue, out_dtypes, extras=()):
    M, N = g.shape
    S, K, Ns = w3.shape
    assert S * Ns == N
    if S > 1:
        return _mm_nt_shards(name, g, w3, epilogue, out_dtypes, extras)
    tm, tn, tkk = _tile(M, MM_TM), _tile(Ns, MM_TX_N), _tile(K, MM_TX_K)
    per = Ns // tn
    nn = N // tn
    n_ex, n_o = len(extras), len(out_dtypes)

    def body(*refs):
        g_ref, w_ref = refs[:2]
        ex = refs[2:2 + n_ex]
        outs = refs[2 + n_ex:2 + n_ex + n_o]
        part = lax.dot_general(g_ref[...].astype(BF16), w_ref[...], (((1,), (1,)), ((), ())),
                               preferred_element_type=F32)

        def finish(acc):
            res = epilogue(acc, *[e[...] for e in ex])
            for r, o in zip(res, outs):
                o[...] = r.astype(o.dtype)

        if nn == 1:
            finish(part)
        else:
            acc_ref = refs[-1]
            n = pl.program_id(2)

            @pl.when(n == 0)
            def _():
                acc_ref[...] = part

            @pl.when(n > 0)
            def _():
                acc_ref[...] += part

            @pl.when(n == nn - 1)
            def _():
                finish(acc_ref[...])

    return pl.pallas_call(
        body, name=name, grid=(M // tm, K // tkk, nn),
        in_specs=[pl.BlockSpec((tm, tn), lambda i, kk, n: (i, n)),
                  pl.BlockSpec((None, tkk, tn), lambda i, kk, n: (n // per, kk, n % per))]
        + [pl.BlockSpec((tm, tkk), lambda i, kk, n: (i, kk)) for _ in extras],
        out_specs=[pl.BlockSpec((tm, tkk), lambda i, kk, n: (i, kk)) for _ in out_dtypes],
        out_shape=[jax.ShapeDtypeStruct((M, K), dt) for dt in out_dtypes],
        scratch_shapes=[pltpu.VMEM((tm, tkk), F32)] if nn > 1 else [],
        compiler_params=_cparams("parallel", "parallel", "arbitrary"),
    )(g, w3, *extras)


def _mm_tn(name, a, g, n_shards):
    T, K = a.shape
    _, N = g.shape
    Ns = N // n_shards
    tk, tn, tt = _tile(K, MM_TW_K), _tile(Ns, MM_TN), _tile(T, MM_TW_T)
    per = Ns // tn
    nt = T // tt

    def body(a_ref, g_ref, o_ref, *scratch):
        part = lax.dot_general(a_ref[...].astype(BF16), g_ref[...].astype(BF16), (((0,), (0,)), ((), ())),
                               preferred_element_type=F32)
        if nt == 1:
            o_ref[...] = part.astype(o_ref.dtype)
            return
        acc_ref, = scratch
        t = pl.program_id(2)

        @pl.when(t == 0)
        def _():
            acc_ref[...] = part

        @pl.when(t > 0)
        def _():
            acc_ref[...] += part

        @pl.when(t == nt - 1)
        def _():
            o_ref[...] = acc_ref[...].astype(o_ref.dtype)

    return pl.pallas_call(
        body, name=name, grid=(K // tk, N // tn, nt),
        in_specs=[pl.BlockSpec((tt, tk), lambda i, j, t: (t, i)),
                  pl.BlockSpec((tt, tn), lambda i, j, t: (t, j))],
        out_specs=pl.BlockSpec((None, tk, tn), lambda i, j, t: (j // per, i, j % per)),
        out_shape=jax.ShapeDtypeStruct((n_shards, K, Ns), BF16),
        scratch_shapes=[pltpu.VMEM((tk, tn), F32)] if nt > 1 else [],
        compiler_params=_cparams("parallel", "parallel", "arbitrary"),
    )(a, g)


def _rowwise(name, fn, ins, outs, accs=(), scratch=(), tt=ROW_TILE):
    T = next(a.shape[0] for a, kind in ins if kind == "row")
    tt = _tile(T, tt)
    n = T // tt
    in_specs = []
    for a, kind in ins:
        w = a.shape[1]
        if kind == "row":
            in_specs.append(pl.BlockSpec((tt, w), lambda i: (i, 0)))
        elif kind == "vec":
            in_specs.append(pl.BlockSpec(a.shape, lambda i: (0, 0)))
        elif kind[0] == "prev":
            pad = kind[1]
            in_specs.append(pl.BlockSpec((pad, w), lambda i, q=tt // pad: (jnp.maximum(i * q - 1, 0), 0)))
        else:
            pad = kind[1]
            in_specs.append(pl.BlockSpec((pad, w), lambda i, q=tt // pad, last=T // pad - 1:
                                         (jnp.minimum((i + 1) * q, last), 0)))
    n_in, n_out, n_acc = len(ins), len(outs), len(accs)

    def body(*refs):
        i = pl.program_id(0)
        in_refs = refs[:n_in]
        out_refs = refs[n_in:n_in + n_out]
        acc_refs = refs[n_in + n_out:n_in + n_out + n_acc]
        scr = refs[n_in + n_out + n_acc:]
        if n_acc:
            @pl.when(i == 0)
            def _():
                for r in acc_refs:
                    r[...] = jnp.zeros_like(r)
        fn(i, n, in_refs, out_refs, acc_refs, scr)

    res = pl.pallas_call(
        body, name=name, grid=(n,),
        in_specs=in_specs,
        out_specs=[pl.BlockSpec((tt, w), lambda i: (i, 0)) for w, _ in outs]
        + [pl.BlockSpec((r, w), lambda i: (0, 0)) for r, w in accs],
        out_shape=[jax.ShapeDtypeStruct((T, w), dt) for w, dt in outs]
        + [jax.ShapeDtypeStruct((r, w), F32) for r, w in accs],
        scratch_shapes=list(scratch),
        compiler_params=_cparams("arbitrary"),
    )(*[a for a, _ in ins])
    return res


def _colsum(x):
    return jnp.sum(x, axis=0, keepdims=True)


def _rms_fwd(name, h, g):
    D = h.shape[1]

    def fn(i, n, ins, outs, accs, scr):
        x = ins[0][...]
        outs[0][...] = (x * _rms_r(x) * ins[1][...]).astype(BF16)

    return _rowwise(name, fn, [(h, "row"), (g, "vec")], [(D, BF16)], tt=LIGHT_ROW_TILE)[0]


def _rms_bwd(name, h, g, du, dh_in, want_colsum=False):
    D = h.shape[1]

    def fn(i, n, ins, outs, accs, scr):
        x = ins[0][...]
        gg = ins[1][...]
        d = ins[2][...].astype(F32)
        r = _rms_r(x)
        xn = x * r
        t = d * gg
        dh = ins[3][...] + r * (t - xn * jnp.mean(t * xn, axis=-1, keepdims=True))
        outs[0][...] = dh
        outs[1][...] = dh.astype(BF16)
        accs[0][...] += _colsum(d * xn)
        if want_colsum:
            accs[1][...] += _colsum(dh)

    return _rowwise(name, fn, [(h, "row"), (g, "vec"), (du, "row"), (dh_in, "row")],
                    [(D, F32), (D, BF16)], accs=[(1, D)] * (2 if want_colsum else 1))


def _loss_bwd(name, h, g, tgt):
    D = h.shape[1]

    def fn(i, n, ins, outs, accs, scr):
        x = ins[0][...]
        gg = ins[1][...]
        r = _rms_r(x)
        xn = x * r
        err = xn * gg - ins[2][...]
        dy = err / D
        t = dy * gg
        dh = r * (t - xn * jnp.mean(t * xn, axis=-1, keepdims=True))
        outs[0][...] = dh
        outs[1][...] = dh.astype(BF16)
        accs[0][...] += _colsum(dy * xn)
        accs[1][...] += _colsum(err * err)

    return _rowwise(name, fn, [(h, "row"), (g, "vec"), (tgt, "row")], [(D, F32), (D, BF16)],
                    accs=[(1, D), (1, D)])


def _ple_elem_bwd(name, dh, q, e):
    D = dh.shape[1]

    def fn(i, n, ins, outs, accs, scr):
        d = ins[0][...]
        s = _sigmoid(ins[1][...].astype(F32))
        ee = ins[2][...].astype(F32)
        outs[0][...] = (d * ee * s * (1.0 - s)).astype(BF16)
        outs[1][...] = (d * s).astype(BF16)

    return _rowwise(name, fn, [(dh, "row"), (q, "row"), (e, "row")], [(D, BF16), (D, BF16)], tt=LIGHT_ROW_TILE)


def _cf_norm_bwd(name, v2, g, dv4):
    D = v2.shape[1]

    def fn(i, n, ins, outs, accs, scr):
        x = ins[0][...]
        gg = ins[1][...]
        r = _rms_r(x)
        xn = x * r
        v3 = xn * gg
        s = _sigmoid(v3)
        dv3 = ins[2][...].astype(F32) * (s * (1.0 + v3 * (1.0 - s)))
        t = dv3 * gg
        dv2 = r * (t - xn * jnp.mean(t * xn, axis=-1, keepdims=True))
        outs[0][...] = dv2
        accs[0][...] += _colsum(dv3 * xn)
        accs[1][...] += _colsum(dv2)

    return _rowwise(name, fn, [(v2, "row"), (g, "vec"), (dv4, "row")], [(D, F32)], accs=[(1, D), (1, D)])


def _chunks(tt, width):
    cc = min(CONV_LANE_CHUNK, width)
    rc = min(CONV_ROW_CHUNK, tt)
    for c0 in range(0, width, cc):
        for r0 in range(0, tt, rc):
            yield r0, rc, c0, cc


def _n_shifts(n_taps):
    return min(SUBLANES - 1, n_taps - 1)


def _shifted_scratch(n_taps, rows, width):
    return pltpu.VMEM((_n_shifts(n_taps), rows, width), F32)


def _shift_window(win_ref, sh_ref, n_taps, sign):
    rows = win_ref.shape[0] - SUBLANES
    width = win_ref.shape[1]
    cc = min(CONV_LANE_CHUNK, width)
    for b in range(1, _n_shifts(n_taps) + 1):
        off = SUBLANES - b if sign < 0 else b
        for c0 in range(0, width, cc):
            sh_ref[b - 1, 0:rows, c0:c0 + cc] = win_ref[off:off + rows, c0:c0 + cc]


def _tap(win_ref, sh_ref, base, sign, s, r0, rc, c0, cc):
    a, b = divmod(s, SUBLANES)
    if b == 0:
        row = base + r0 + sign * SUBLANES * a
        return win_ref[row:row + rc, c0:c0 + cc]
    row = base + r0 - SUBLANES * (a + 1) if sign < 0 else base + r0 + SUBLANES * a
    return sh_ref[b - 1, row:row + rc, c0:c0 + cc]


def _fir(win_ref, sh_ref, w_ref, n_taps, base, sign, tt, width, emit):
    for r0, rc, c0, cc in _chunks(tt, width):
        acc = jnp.zeros((rc, cc), F32)
        for k in range(n_taps):
            acc = acc + w_ref[k:k + 1, c0:c0 + cc] * _tap(win_ref, sh_ref, base, sign, n_taps - 1 - k, r0, rc, c0, cc)
        emit(r0, rc, c0, cc, acc)


def _fir_wgrad(d_ref, win_ref, sh_ref, dw8_ref, n_taps, pad, tt, width):
    for c0 in range(0, width, min(CONV_LANE_CHUNK, width)):
        cc = min(CONV_LANE_CHUNK, width)
        rc = min(CONV_ROW_CHUNK, tt)
        for k in range(n_taps):
            acc = jnp.zeros((SUBLANES, cc), F32)
            for r0 in range(0, tt, rc):
                prod = d_ref[r0:r0 + rc, c0:c0 + cc] * _tap(win_ref, sh_ref, pad, -1, n_taps - 1 - k, r0, rc, c0, cc)
                for q in range(0, rc, SUBLANES):
                    acc = acc + prod[q:q + SUBLANES]
            dw8_ref[SUBLANES * k:SUBLANES * (k + 1), c0:c0 + cc] += acc


def _glu(blk, D):
    return blk[:, :D].astype(F32) * _sigmoid(blk[:, D:].astype(F32))


CF_PAD = 32
SC_PAD = 16


def _cf_conv_fwd(name, a, w_dw, b_dw, g_cf):
    T, D2 = a.shape
    D = D2 // 2
    K = w_dw.shape[0]
    tt = _tile(T, CONV_FWD_TILE_ROWS)

    def fn(i, n, ins, outs, accs, scr):
        a_ref, prev_ref, w_ref, b_ref, g_ref = ins
        win_ref, v2_ref, sh_ref = scr
        win_ref[0:CF_PAD, :] = jnp.where(i > 0, _glu(prev_ref[...], D), 0.0)
        win_ref[CF_PAD:CF_PAD + tt, :] = _glu(a_ref[...], D)
        _shift_window(win_ref, sh_ref, K, -1)

        def emit(r0, rc, c0, cc, acc):
            v2_ref[r0:r0 + rc, c0:c0 + cc] = acc + b_ref[:, c0:c0 + cc]

        _fir(win_ref, sh_ref, w_ref, K, CF_PAD, -1, tt, D, emit)
        v2 = v2_ref[...]
        v3 = v2 * _rms_r(v2) * g_ref[...]
        outs[0][...] = v2
        outs[1][...] = (v3 * _sigmoid(v3)).astype(BF16)

    return _rowwise(name, fn, [(a, "row"), (a, ("prev", CF_PAD)), (w_dw, "vec"), (b_dw, "vec"), (g_cf, "vec")],
                    [(D, F32), (D, BF16)],
                    scratch=[pltpu.VMEM((CF_PAD + tt, D), F32), pltpu.VMEM((tt, D), F32),
                             _shifted_scratch(K, CF_PAD + tt, D)], tt=tt)


def _cf_conv_bwd(name, dv2, a, w_dw):
    T, D2 = a.shape
    D = D2 // 2
    K = w_dw.shape[0]
    tt = _tile(T, CONV_TILE_ROWS)

    def fn(i, n, ins, outs, accs, scr):
        d_ref, dnext_ref, a_ref, prev_ref, w_ref = ins
        v1win_ref, dwin_ref, dv1_ref, dw8_ref, v1sh_ref, dsh_ref = scr

        @pl.when(i == 0)
        def _():
            dw8_ref[...] = jnp.zeros_like(dw8_ref)

        v1win_ref[0:CF_PAD, :] = jnp.where(i > 0, _glu(prev_ref[...], D), 0.0)
        v1win_ref[CF_PAD:CF_PAD + tt, :] = _glu(a_ref[...], D)
        dwin_ref[0:tt, :] = d_ref[...]
        dwin_ref[tt:tt + CF_PAD, :] = jnp.where(i < n - 1, dnext_ref[...], 0.0)
        _shift_window(v1win_ref, v1sh_ref, K, -1)
        _shift_window(dwin_ref, dsh_ref, K, 1)

        def emit(r0, rc, c0, cc, acc):
            dv1_ref[r0:r0 + rc, c0:c0 + cc] = acc

        _fir(dwin_ref, dsh_ref, w_ref, K, 0, 1, tt, D, emit)
        _fir_wgrad(d_ref, v1win_ref, v1sh_ref, dw8_ref, K, CF_PAD, tt, D)

        blk = a_ref[...]
        val = blk[:, :D].astype(F32)
        sg = _sigmoid(blk[:, D:].astype(F32))
        dv1 = dv1_ref[...]
        dval = dv1 * sg
        dgate = dv1 * val * sg * (1.0 - sg)
        outs[0][:, :D] = dval.astype(BF16)
        outs[0][:, D:] = dgate.astype(BF16)
        accs[1][:, :D] += _colsum(dval)
        accs[1][:, D:] += _colsum(dgate)

        @pl.when(i == n - 1)
        def _():
            for k in range(K):
                accs[0][k:k + 1, :] = _colsum(dw8_ref[SUBLANES * k:SUBLANES * (k + 1), :])

    return _rowwise(name, fn, [(dv2, "row"), (dv2, ("next", CF_PAD)), (a, "row"), (a, ("prev", CF_PAD)),
                               (w_dw, "vec")],
                    [(D2, BF16)], accs=[(K, D), (1, D2)],
                    scratch=[pltpu.VMEM((CF_PAD + tt, D), F32), pltpu.VMEM((tt + CF_PAD, D), F32),
                             pltpu.VMEM((tt, D), F32), pltpu.VMEM((SUBLANES * K, D), F32),
                             _shifted_scratch(K, CF_PAD + tt, D), _shifted_scratch(K, tt + CF_PAD, D)], tt=tt)


def _sc_conv_fwd(name, bcv, w_conv):
    T, D3 = bcv.shape
    D = D3 // 3
    K = w_conv.shape[0]
    tt = _tile(T, CONV_TILE_ROWS)

    def cv_of(blk):
        return blk[:, D:2 * D].astype(F32) * blk[:, 2 * D:].astype(F32)

    def fn(i, n, ins, outs, accs, scr):
        x_ref, prev_ref, w_ref = ins
        win_ref, cc_ref, sh_ref = scr
        win_ref[0:SC_PAD, :] = jnp.where(i > 0, cv_of(prev_ref[...]), 0.0)
        win_ref[SC_PAD:SC_PAD + tt, :] = cv_of(x_ref[...])
        _shift_window(win_ref, sh_ref, K, -1)

        def emit(r0, rc, c0, cw, acc):
            cc_ref[r0:r0 + rc, c0:c0 + cw] = acc

        _fir(win_ref, sh_ref, w_ref, K, SC_PAD, -1, tt, D, emit)
        outs[0][...] = (x_ref[:, :D].astype(F32) * cc_ref[...]).astype(BF16)

    return _rowwise(name, fn, [(bcv, "row"), (bcv, ("prev", SC_PAD)), (w_conv, "vec")], [(D, BF16)],
                    scratch=[pltpu.VMEM((SC_PAD + tt, D), F32), pltpu.VMEM((tt, D), F32),
                             _shifted_scratch(K, SC_PAD + tt, D)], tt=tt)


def _sc_conv_bwd(name, dy, bcv, w_conv):
    T, D3 = bcv.shape
    D = D3 // 3
    K = w_conv.shape[0]
    tt = _tile(T, CONV_TILE_ROWS)

    def cv_of(blk):
        return blk[:, D:2 * D].astype(F32) * blk[:, 2 * D:].astype(F32)

    def fn(i, n, ins, outs, accs, scr):
        dy_ref, dynext_ref, x_ref, prev_ref, next_ref, w_ref = ins
        cvwin_ref, dccwin_ref, tmp_ref, dw8_ref, cvsh_ref, dccsh_ref = scr

        @pl.when(i == 0)
        def _():
            dw8_ref[...] = jnp.zeros_like(dw8_ref)

        cvwin_ref[0:SC_PAD, :] = jnp.where(i > 0, cv_of(prev_ref[...]), 0.0)
        cvwin_ref[SC_PAD:SC_PAD + tt, :] = cv_of(x_ref[...])
        _shift_window(cvwin_ref, cvsh_ref, K, -1)

        def emit_cc(r0, rc, c0, cw, acc):
            tmp_ref[r0:r0 + rc, c0:c0 + cw] = acc

        _fir(cvwin_ref, cvsh_ref, w_ref, K, SC_PAD, -1, tt, D, emit_cc)
        dy_v = dy_ref[...].astype(F32)
        outs[0][:, :D] = (dy_v * tmp_ref[...]).astype(BF16)
        dccwin_ref[0:tt, :] = dy_v * x_ref[:, :D].astype(F32)
        dccwin_ref[tt:tt + SC_PAD, :] = jnp.where(
            i < n - 1, dynext_ref[...].astype(F32) * next_ref[:, :D].astype(F32), 0.0)

        _shift_window(dccwin_ref, dccsh_ref, K, 1)

        def emit_dcv(r0, rc, c0, cw, acc):
            tmp_ref[r0:r0 + rc, c0:c0 + cw] = acc

        _fir(dccwin_ref, dccsh_ref, w_ref, K, 0, 1, tt, D, emit_dcv)
        _fir_wgrad(dccwin_ref, cvwin_ref, cvsh_ref, dw8_ref, K, SC_PAD, tt, D)
        dcv = tmp_ref[...]
        outs[0][:, D:2 * D] = (dcv * x_ref[:, 2 * D:].astype(F32)).astype(BF16)
        outs[0][:, 2 * D:] = (dcv * x_ref[:, D:2 * D].astype(F32)).astype(BF16)

        @pl.when(i == n - 1)
        def _():
            for k in range(K):
                accs[0][k:k + 1, :] = _colsum(dw8_ref[SUBLANES * k:SUBLANES * (k + 1), :])

    return _rowwise(name, fn, [(dy, "row"), (dy, ("next", SC_PAD)), (bcv, "row"), (bcv, ("prev", SC_PAD)),
                               (bcv, ("next", SC_PAD)), (w_conv, "vec")],
                    [(D3, BF16)], accs=[(K, D)],
                    scratch=[pltpu.VMEM((SC_PAD + tt, D), F32), pltpu.VMEM((tt + SC_PAD, D), F32),
                             pltpu.VMEM((tt, D), F32), pltpu.VMEM((SUBLANES * K, D), F32),
                             _shifted_scratch(K, SC_PAD + tt, D), _shifted_scratch(K, tt + SC_PAD, D)], tt=tt)


def _place():
    x, y, c = lax.axis_index("x"), lax.axis_index("y"), lax.axis_index("c")
    chips = [(1 - x, y), (x, 1 - y), (1 - x, 1 - y)]
    return x, y, c, 2 * x + y, chips, (x, y, 1 - c)


def _half(rows, which):
    return pl.ds(pl.multiple_of(which * (rows // 2), SUBLANES), rows // 2)


_HBM = pl.BlockSpec(memory_space=pl.ANY)


def _gather_shards(name, items):
    n = len(items)
    shapes = [a.shape[-2:] for a, _ in items]

    def body(*refs):
        srcs, outs = refs[:n], refs[n:2 * n]
        send1, recv1, send2, recv2, lsem = refs[2 * n:]
        x, y, c, k, chips, sib = _place()

        def shard(i):
            return srcs[i] if items[i][1] is None else srcs[i].at[items[i][1]]

        started, locs = [], []
        for i in range(n):
            rows = shapes[i][0]
            lc = pltpu.make_async_copy(shard(i), outs[i].at[k], lsem.at[i])
            lc.start()
            locs.append(lc)
            for j, (cx, cy) in enumerate(chips):
                cp = pltpu.make_async_remote_copy(
                    src_ref=shard(i).at[_half(rows, c)], dst_ref=outs[i].at[k, _half(rows, c)],
                    send_sem=send1.at[i, j], recv_sem=recv1.at[i, j], device_id=(cx, cy, c), device_id_type=MESH)
                cp.start()
                started.append(cp)
        for i in range(n):
            rows = shapes[i][0]
            for j, (cx, cy) in enumerate(chips):
                blk = outs[i].at[2 * cx + cy, _half(rows, c)]
                pltpu.make_async_remote_copy(
                    src_ref=blk, dst_ref=blk, send_sem=send1.at[i, j], recv_sem=recv1.at[i, j],
                    device_id=(cx, cy, c), device_id_type=MESH).wait_recv()
                fw = pltpu.make_async_remote_copy(
                    src_ref=blk, dst_ref=blk, send_sem=send2.at[i, j], recv_sem=recv2.at[i, j],
                    device_id=sib, device_id_type=MESH)
                fw.start()
                started.append(fw)
        for i in range(n):
            rows = shapes[i][0]
            for j, (cx, cy) in enumerate(chips):
                blk = outs[i].at[2 * cx + cy, _half(rows, 1 - c)]
                pltpu.make_async_remote_copy(
                    src_ref=blk, dst_ref=blk, send_sem=send2.at[i, j], recv_sem=recv2.at[i, j],
                    device_id=sib, device_id_type=MESH).wait_recv()
        for cp in started:
            cp.wait_send()
        for lc in locs:
            lc.wait()

    return pl.pallas_call(
        body, name=name,
        in_specs=[_HBM] * n, out_specs=[_HBM] * n,
        out_shape=[jax.ShapeDtypeStruct((N_SHARDS,) + tuple(s), a.dtype) for s, (a, _) in zip(shapes, items)],
        scratch_shapes=[pltpu.SemaphoreType.DMA((n, 3))] * 4 + [pltpu.SemaphoreType.DMA((n,))],
    )(*[a for a, _ in items])


def _cast_place(name, w, layer, pos):
    _, R, C = w.shape
    tr = _tile(R, 256)

    def body(x_ref, y_ref, c_ref, w_ref, o_ref):
        o_ref[...] = w_ref[...].astype(BF16)

    return pl.pallas_call(
        body, name=name,
        grid_spec=pltpu.PrefetchScalarGridSpec(
            num_scalar_prefetch=3, grid=(R // tr,),
            in_specs=[pl.BlockSpec((None, tr, C), lambda r, xr, yr, cr: (layer, r, 0))],
            out_specs=pl.BlockSpec((None, tr, C), lambda r, xr, yr, cr: (2 * xr[0] + yr[0], r, 0))),
        out_shape=jax.ShapeDtypeStruct((N_SHARDS, R, C), BF16),
        compiler_params=_cparams("parallel"),
    )(*pos, w)


_IN_HBM = pl.BlockSpec(memory_space=pltpu.HBM)
_SEM = pl.BlockSpec(memory_space=pltpu.SEMAPHORE)
_SPLIT_COPY_PARAMS = pltpu.CompilerParams(has_side_effects=pltpu.SideEffectType.DATAFLOW_SIDE_EFFECTING)


def _in_hbm(a):
    return pltpu.with_memory_space_constraint(a, pltpu.HBM)


def _gather_copy(ref, i, j, chip_xy, c, k_src, rows, send, recv):
    blk = ref.at[k_src, _half(rows, c)]
    return pltpu.make_async_remote_copy(
        src_ref=blk, dst_ref=blk, send_sem=send.at[3 * i + j], recv_sem=recv.at[3 * i + j],
        device_id=(*chip_xy, c), device_id_type=MESH)


def _gather_start(name, bufs, after):
    n = len(bufs)

    def body(*refs):
        ins = refs[:n]
        send, recv = refs[n + 1], refs[n + 2]
        token = refs[-1]
        x, y, c, k, chips, sib = _place()
        for i in range(n):
            for j, chip_xy in enumerate(chips):
                _gather_copy(ins[i], i, j, chip_xy, c, k, bufs[i].shape[1], send, recv).start()
        token[...] = jnp.zeros_like(token)

    res = pl.pallas_call(
        body, name=name,
        in_specs=[_IN_HBM] * n + [_HBM],
        out_specs=[_SEM, _SEM] + [_IN_HBM] * n + [pl.BlockSpec(memory_space=pltpu.VMEM)],
        out_shape=[pltpu.SemaphoreType.DMA((3 * n,)), pltpu.SemaphoreType.DMA((3 * n,))]
        + [pltpu.HBM(b.shape, b.dtype) for b in bufs] + [jax.ShapeDtypeStruct((SUBLANES, 128), F32)],
        input_output_aliases={i: 2 + i for i in range(n)},
        compiler_params=_SPLIT_COPY_PARAMS,
    )(*[_in_hbm(b) for b in bufs], after)
    return res[0], res[1], list(res[2:2 + n]), res[-1]


def _gather_wait(name, bufs, send, recv, afters):
    n = len(bufs)

    def body(*refs):
        ins = refs[:n]
        send_ref, recv_ref = refs[n], refs[n + 1]
        x, y, c, k, chips, sib = _place()
        for i in range(n):
            for j, chip_xy in enumerate(chips):
                rows = bufs[i].shape[1]
                _gather_copy(ins[i], i, j, chip_xy, c, k, rows, send_ref, recv_ref).wait_send()
                _gather_copy(ins[i], i, j, chip_xy, c, 2 * chip_xy[0] + chip_xy[1], rows, send_ref, recv_ref).wait_recv()

    return pl.pallas_call(
        body, name=name,
        in_specs=[_IN_HBM] * n + [_SEM, _SEM] + [_HBM] * len(afters),
        out_specs=[_IN_HBM] * n,
        out_shape=[pltpu.HBM(b.shape, b.dtype) for b in bufs],
        input_output_aliases={i: i for i in range(n)},
        compiler_params=_SPLIT_COPY_PARAMS,
    )(*bufs, send, recv, *afters)


def _gather_forward(name, bufs):
    n = len(bufs)

    def body(*refs):
        outs = refs[n:2 * n]
        send, recv = refs[2 * n:]
        x, y, c, k, chips, sib = _place()
        started = []
        for i in range(n):
            rows = bufs[i].shape[1]
            for j, (cx, cy) in enumerate(chips):
                blk = outs[i].at[2 * cx + cy, _half(rows, c)]
                fw = pltpu.make_async_remote_copy(
                    src_ref=blk, dst_ref=blk, send_sem=send.at[i, j], recv_sem=recv.at[i, j],
                    device_id=sib, device_id_type=MESH)
                fw.start()
                started.append(fw)
        for i in range(n):
            rows = bufs[i].shape[1]
            for j, (cx, cy) in enumerate(chips):
                blk = outs[i].at[2 * cx + cy, _half(rows, 1 - c)]
                pltpu.make_async_remote_copy(
                    src_ref=blk, dst_ref=blk, send_sem=send.at[i, j], recv_sem=recv.at[i, j],
                    device_id=sib, device_id_type=MESH).wait_recv()
        for cp in started:
            cp.wait_send()

    return pl.pallas_call(
        body, name=name, in_specs=[_HBM] * n, out_specs=[_HBM] * n,
        out_shape=[jax.ShapeDtypeStruct(b.shape, b.dtype) for b in bufs],
        input_output_aliases={i: i for i in range(n)},
        scratch_shapes=[pltpu.SemaphoreType.DMA((n, 3))] * 2,
    )(*bufs)


def _swap_halves(name, parts):
    n = len(parts)

    def body(*refs):
        srcs, outs = refs[:n], refs[n:2 * n]
        send, recv = refs[2 * n:]
        x, y, c, k, chips, sib = _place()
        cps = []
        for i in range(n):
            rows = parts[i].shape[1]
            cp = pltpu.make_async_remote_copy(
                src_ref=srcs[i].at[:, _half(rows, 1 - c)], dst_ref=outs[i],
                send_sem=send.at[i], recv_sem=recv.at[i], device_id=sib, device_id_type=MESH)
            cp.start()
            cps.append(cp)
        for cp in cps:
            cp.wait()

    return pl.pallas_call(
        body, name=name, in_specs=[_HBM] * n, out_specs=[_HBM] * n,
        out_shape=[jax.ShapeDtypeStruct((p.shape[0], p.shape[1] // 2, p.shape[2]), p.dtype) for p in parts],
        scratch_shapes=[pltpu.SemaphoreType.DMA((n,))] * 2,
    )(*parts)


def _sibling_start(name, bufs, n_copies, make):
    nb = len(bufs)

    def body(*refs):
        send, recv = refs[nb], refs[nb + 1]
        token = refs[-1]
        x, y, c, k, chips, sib = _place()
        for cp in make(refs[:nb], c, sib, send, recv):
            cp.start()
        token[...] = jnp.zeros_like(token)

    res = pl.pallas_call(
        body, name=name,
        in_specs=[_IN_HBM] * nb,
        out_specs=[_SEM, _SEM] + [_IN_HBM] * nb + [pl.BlockSpec(memory_space=pltpu.VMEM)],
        out_shape=[pltpu.SemaphoreType.DMA((n_copies,)), pltpu.SemaphoreType.DMA((n_copies,))]
        + [pltpu.HBM(b.shape, b.dtype) for b in bufs] + [jax.ShapeDtypeStruct((SUBLANES, 128), F32)],
        input_output_aliases={i: 2 + i for i in range(nb)},
        compiler_params=_SPLIT_COPY_PARAMS,
    )(*[_in_hbm(b) for b in bufs])
    return res[0], res[1], list(res[2:2 + nb]), res[-1]


def _sibling_wait(name, bufs, send, recv, make, afters):
    nb = len(bufs)

    def body(*refs):
        x, y, c, k, chips, sib = _place()
        for cp in make(refs[:nb], c, sib, refs[nb], refs[nb + 1]):
            cp.wait_send()
            cp.wait_recv()

    return list(pl.pallas_call(
        body, name=name,
        in_specs=[_IN_HBM] * nb + [_SEM, _SEM] + [_HBM] * len(afters),
        out_specs=[_IN_HBM] * nb,
        out_shape=[pltpu.HBM(b.shape, b.dtype) for b in bufs],
        input_output_aliases={i: i for i in range(nb)},
        compiler_params=_SPLIT_COPY_PARAMS,
    )(*bufs, send, recv, *afters))


def _swap_copies(parts):
    n = len(parts)

    def make(refs, c, sib, send, recv):
        return [pltpu.make_async_remote_copy(
            src_ref=refs[i].at[:, _half(parts[i].shape[1], 1 - c)], dst_ref=refs[n + i],
            send_sem=send.at[i], recv_sem=recv.at[i], device_id=sib, device_id_type=MESH) for i in range(n)]

    return make


def _join_copies(fulls):
    def make(refs, c, sib, send, recv):
        cps = []
        for i, f in enumerate(fulls):
            blk = refs[i].at[_half(f.shape[0], c)]
            cps.append(pltpu.make_async_remote_copy(
                src_ref=blk, dst_ref=blk, send_sem=send.at[i], recv_sem=recv.at[i],
                device_id=sib, device_id_type=MESH))
        return cps

    return make


def _forward_copies(bufs):
    def make(refs, c, sib, send, recv):
        chips = _place()[4]
        cps = []
        for i, b in enumerate(bufs):
            for j, (cx, cy) in enumerate(chips):
                blk = refs[i].at[2 * cx + cy, _half(b.shape[1], c)]
                cps.append(pltpu.make_async_remote_copy(
                    src_ref=blk, dst_ref=blk, send_sem=send.at[3 * i + j], recv_sem=recv.at[3 * i + j],
                    device_id=sib, device_id_type=MESH))
        return cps

    return make


def _scatter_copy(src_ref, land_ref, i, j, chip_xy, c, send, recv):
    return pltpu.make_async_remote_copy(
        src_ref=src_ref.at[2 * chip_xy[0] + chip_xy[1]], dst_ref=land_ref.at[j],
        send_sem=send.at[3 * i + j], recv_sem=recv.at[3 * i + j], device_id=(*chip_xy, c), device_id_type=MESH)


def _scatter_start(name, sums):
    n = len(sums)
    lands = [lax.empty((3,) + s.shape[1:], s.dtype) for s in sums]

    def body(*refs):
        srcs, lnds = refs[:n], refs[n:2 * n]
        send, recv = refs[2 * n], refs[2 * n + 1]
        token = refs[-1]
        x, y, c, k, chips, sib = _place()
        for i in range(n):
            for j, chip_xy in enumerate(chips):
                _scatter_copy(srcs[i], lnds[i], i, j, chip_xy, c, send, recv).start()
        token[...] = jnp.zeros_like(token)

    res = pl.pallas_call(
        body, name=name,
        in_specs=[_IN_HBM] * (2 * n),
        out_specs=[_SEM, _SEM] + [_IN_HBM] * (2 * n) + [pl.BlockSpec(memory_space=pltpu.VMEM)],
        out_shape=[pltpu.SemaphoreType.DMA((3 * n,)), pltpu.SemaphoreType.DMA((3 * n,))]
        + [pltpu.HBM(a.shape, a.dtype) for a in list(sums) + lands] + [jax.ShapeDtypeStruct((SUBLANES, 128), F32)],
        input_output_aliases={i: 2 + i for i in range(2 * n)},
        compiler_params=_SPLIT_COPY_PARAMS,
    )(*[_in_hbm(a) for a in list(sums) + lands])
    return res[0], res[1], list(res[2:2 + n]), list(res[2 + n:2 + 2 * n]), res[-1]


def _scatter_wait(name, sums, lands, send, recv, afters):
    n = len(sums)

    def body(*refs):
        srcs, lnds = refs[:n], refs[n:2 * n]
        send_ref, recv_ref = refs[2 * n], refs[2 * n + 1]
        x, y, c, k, chips, sib = _place()
        for i in range(n):
            for j, chip_xy in enumerate(chips):
                cp = _scatter_copy(srcs[i], lnds[i], i, j, chip_xy, c, send_ref, recv_ref)
                cp.wait_send()
                cp.wait_recv()

    res = pl.pallas_call(
        body, name=name,
        in_specs=[_IN_HBM] * (2 * n) + [_SEM, _SEM] + [_HBM] * len(afters),
        out_specs=[_IN_HBM] * (2 * n),
        out_shape=[pltpu.HBM(a.shape, a.dtype) for a in list(sums) + list(lands)],
        input_output_aliases={i: i for i in range(2 * n)},
        compiler_params=_SPLIT_COPY_PARAMS,
    )(*sums, *lands, send, recv, *afters)
    return list(res[:n]), list(res[n:])


def _sum_over_devices(name, buf, loss_row, afters):
    R, D = buf.shape
    n_after = len(afters)

    def body(x_ref, *rest):
        all_ref, tot_ref, loss_ref, send_sems, recv_sems, local_sem = rest[n_after:]
        x, y, c, k, chips, sib = _place()
        me = (x, y, c)

        def block(px, py, pc):
            return all_ref.at[4 * px + 2 * py + pc]

        def copy(kk, blk, to, src=None):
            return pltpu.make_async_remote_copy(
                src_ref=block(*blk) if src is None else src, dst_ref=block(*blk),
                send_sem=send_sems.at[kk], recv_sem=recv_sems.at[kk], device_id=to, device_id_type=MESH)

        mine = pltpu.make_async_copy(x_ref, block(*me), local_sem)
        mine.start()
        first = [copy(0, me, sib, src=x_ref)]
        first += [copy(1 + j, me, (*chip, c), src=x_ref) for j, chip in enumerate(chips)]
        for cp in first:
            cp.start()
        passed = [copy(4 + j, (*chip, c), sib) for j, chip in enumerate(chips)]
        for j, chip in enumerate(chips):
            copy(1 + j, (*chip, c), me).wait_recv()
            passed[j].start()
        copy(0, sib, me).wait_recv()
        for j, chip in enumerate(chips):
            copy(4 + j, (*chip, 1 - c), me).wait_recv()
        for cp in first + passed:
            cp.wait_send()
        mine.wait()
        rc = _tile(R, 32)
        for r0 in range(0, R, rc):
            tot = all_ref[0, r0:r0 + rc, :]
            for d in range(1, N_DEVICES):
                tot = tot + all_ref[d, r0:r0 + rc, :]
            tot_ref[r0:r0 + rc, :] = tot
        loss = 0.5 * jnp.sum(tot_ref[loss_row:loss_row + 1, :]) / D
        loss_ref[...] = jnp.full(loss_ref.shape, loss, F32)

    vm = pl.BlockSpec(memory_space=pltpu.VMEM)
    return pl.pallas_call(
        body, name=name, in_specs=[vm] + [_HBM] * n_after, out_specs=[vm, vm, vm],
        out_shape=[jax.ShapeDtypeStruct((N_DEVICES, R, D), F32), jax.ShapeDtypeStruct((R, D), F32),
                   jax.ShapeDtypeStruct((SUBLANES, 128), F32)],
        scratch_shapes=[pltpu.SemaphoreType.DMA((7,)), pltpu.SemaphoreType.DMA((7,)), pltpu.SemaphoreType.DMA],
        compiler_params=pltpu.CompilerParams(vmem_limit_bytes=V7X_VMEM_LIMIT_BYTES),
    )(buf, *afters)[1:]


def _add_my_half(name, part, got, pos):
    S, R, C = part.shape
    R2 = R // 2
    tr = _tile(R2, 1024)
    q = R2 // tr

    def body(x_ref, y_ref, c_ref, p_ref, g_ref, o_ref):
        o_ref[...] = (p_ref[...].astype(F32) + g_ref[...].astype(F32)).astype(o_ref.dtype)

    return pl.pallas_call(
        body, name=name,
        grid_spec=pltpu.PrefetchScalarGridSpec(
            num_scalar_prefetch=3, grid=(S, q),
            in_specs=[pl.BlockSpec((None, tr, C), lambda s, r, xr, yr, cr: (s, cr[0] * q + r, 0)),
                      pl.BlockSpec((None, tr, C), lambda s, r, xr, yr, cr: (s, r, 0))],
            out_specs=pl.BlockSpec((None, tr, C), lambda s, r, xr, yr, cr: (s, r, 0))),
        out_shape=jax.ShapeDtypeStruct((S, R2, C), BF16),
        compiler_params=_cparams("parallel", "parallel"),
    )(*pos, part, got)


def _add_owner(name, sums, got, pos):
    _, R2, C = sums.shape
    tr = _tile(R2, 512)
    q = R2 // tr

    def body(x_ref, y_ref, c_ref, s_ref, g_ref, o_ref):
        acc = s_ref[...].astype(F32)
        for j in range(3):
            acc = acc + g_ref[j].astype(F32)
        o_ref[...] = acc

    return pl.pallas_call(
        body, name=name,
        grid_spec=pltpu.PrefetchScalarGridSpec(
            num_scalar_prefetch=3, grid=(q,),
            in_specs=[pl.BlockSpec((None, tr, C), lambda r, xr, yr, cr: (2 * xr[0] + yr[0], r, 0)),
                      pl.BlockSpec((3, tr, C), lambda r, xr, yr, cr: (0, r, 0))],
            out_specs=pl.BlockSpec((tr, C), lambda r, xr, yr, cr: (cr[0] * q + r, 0))),
        out_shape=jax.ShapeDtypeStruct((2 * R2, C), F32),
        compiler_params=_cparams("parallel"),
    )(*pos, sums, got)


def _adamw(name, w, m, v, g):
    R, C = w.shape
    tr = SUBLANES
    while 2 * tr * C <= ADAMW_TILE_ELEMS:
        tr *= 2
    tr = _tile(R, tr)
    bc1 = 1.0 - ADAM_B1 ** ADAM_STEP
    bc2 = 1.0 - ADAM_B2 ** ADAM_STEP

    def body(w_ref, m_ref, v_ref, g_ref, go_ref, d_ref, mo_ref, vo_ref):
        gg = g_ref[...]
        m2 = ADAM_B1 * m_ref[...] + (1.0 - ADAM_B1) * gg
        v2 = ADAM_B2 * v_ref[...] + (1.0 - ADAM_B2) * (gg * gg)
        go_ref[...] = gg
        mo_ref[...] = m2
        vo_ref[...] = v2
        d_ref[...] = -ADAM_LR * ((m2 / bc1) / (jnp.sqrt(v2 / bc2) + ADAM_EPS) + ADAM_WD * w_ref[...])

    spec = pl.BlockSpec((tr, C), lambda r: (r, 0))
    return pl.pallas_call(
        body, name=name, grid=(R // tr,), in_specs=[spec] * 4, out_specs=[spec] * 4,
        out_shape=[jax.ShapeDtypeStruct((R, C), F32)] * 4,
        compiler_params=_cparams("parallel"),
    )(w, m, v, g)


def _adamw_slab(name, w, m, v, g, layer, prev):
    L, R, C = w.shape
    tr = SUBLANES
    while 2 * tr * C <= ADAMW_TILE_ELEMS:
        tr *= 2
    tr = _tile(R, tr)
    bc1 = 1.0 - ADAM_B1 ** ADAM_STEP
    bc2 = 1.0 - ADAM_B2 ** ADAM_STEP

    def body(w_ref, m_ref, v_ref, g_ref, *rest):
        go_ref, d_ref, mo_ref, vo_ref = rest[-4:]
        gg = g_ref[...]
        m2 = ADAM_B1 * m_ref[...] + (1.0 - ADAM_B1) * gg
        v2 = ADAM_B2 * v_ref[...] + (1.0 - ADAM_B2) * (gg * gg)
        go_ref[...] = gg
        mo_ref[...] = m2
        vo_ref[...] = v2
        d_ref[...] = -ADAM_LR * ((m2 / bc1) / (jnp.sqrt(v2 / bc2) + ADAM_EPS) + ADAM_WD * w_ref[...])

    slab = pl.BlockSpec((None, tr, C), lambda r: (layer, r, 0))
    n_prev = 0 if prev is None else 4
    return pl.pallas_call(
        body, name=name, grid=(R // tr,),
        in_specs=[slab] * 3 + [pl.BlockSpec((tr, C), lambda r: (r, 0))] + [_HBM] * n_prev,
        out_specs=[slab] * 4,
        out_shape=[jax.ShapeDtypeStruct((L, R, C), F32)] * 4,
        input_output_aliases={4 + i: i for i in range(n_prev)},
        compiler_params=_cparams("parallel"),
    )(w, m, v, g, *(prev or ()))


def _swap_begin(tag, parts):
    lands = [lax.empty((p.shape[0], p.shape[1] // 2, p.shape[2]), p.dtype) for p in parts]
    return _sibling_start(f"rs_swap_start_{tag}", list(parts) + lands, len(parts), _swap_copies(parts))


def _reduce_begin(tag, early, swapping, late, pos, after):
    send, recv, bufs, _ = swapping
    bufs = _sibling_wait(f"rs_swap_wait_{tag}", bufs, send, recv, _swap_copies(early), [after])
    parts = bufs[:len(early)] + list(late)
    got = bufs[len(early):] + list(_swap_halves(f"rs_swap_{tag}", late))
    sums = [_add_my_half(f"rs_add2_{tag}_{i}", p, g, pos) for i, (p, g) in enumerate(zip(parts, got))]
    return _scatter_start(f"rs_scatter_start_{tag}", sums)


def _reduce_middle(tag, started, pos, afters):
    send, recv, sums, lands, _ = started
    sums, lands = _scatter_wait(f"rs_scatter_wait_{tag}", sums, lands, send, recv, afters)
    fulls = [_add_owner(f"rs_add4_{tag}_{i}", s, q, pos) for i, (s, q) in enumerate(zip(sums, lands))]
    return _sibling_start(f"rs_join_start_{tag}", fulls, len(fulls), _join_copies(fulls))


def _reduce_end(tag, joining, afters):
    send, recv, fulls, _ = joining
    return _sibling_wait(f"rs_join_wait_{tag}", fulls, send, recv, _join_copies(fulls), afters)


def _pad_rows(a):
    r = (-a.shape[0]) % SUBLANES
    return jnp.pad(a, ((0, r), (0, 0))) if r else a


def kernel(x, p, norm_mix, norm_mlp, norm_ple, cf_w_pw1, cf_b_pw1, cf_w_dw, cf_b_dw, cf_norm, cf_w_pw2, cf_b_pw2, sc_w_in, sc_w_conv, sc_w_out, mlp_w1, mlp_w2, ple_w_proj, ple_w_gate, norm_final, loss_target, m_norm_mix, m_norm_mlp, m_norm_ple, m_cf_w_pw1, m_cf_b_pw1, m_cf_w_dw, m_cf_b_dw, m_cf_norm, m_cf_w_pw2, m_cf_b_pw2, m_sc_w_in, m_sc_w_conv, m_sc_w_out, m_mlp_w1, m_mlp_w2, m_ple_w_proj, m_ple_w_gate, m_norm_final, v_norm_mix, v_norm_mlp, v_norm_ple, v_cf_w_pw1, v_cf_b_pw1, v_cf_w_dw, v_cf_b_dw, v_cf_norm, v_cf_w_pw2, v_cf_b_pw2, v_sc_w_in, v_sc_w_conv, v_sc_w_out, v_mlp_w1, v_mlp_w2, v_ple_w_proj, v_ple_w_gate, v_norm_final):
    T, D = x.shape[1], x.shape[2]
    KA, KB = cf_w_dw.shape[1], sc_w_conv.shape[1]
    chip = (2 * lax.axis_index("x") + lax.axis_index("y")).astype(jnp.int32)
    pos = tuple(lax.axis_index(ax).astype(jnp.int32).reshape(1) for ax in ("x", "y", "c"))

    params = dict(norm_mix=norm_mix, norm_mlp=norm_mlp, norm_ple=norm_ple, cf_w_pw1=cf_w_pw1, cf_b_pw1=cf_b_pw1,
                  cf_w_dw=cf_w_dw, cf_b_dw=cf_b_dw, cf_norm=cf_norm, cf_w_pw2=cf_w_pw2, cf_b_pw2=cf_b_pw2,
                  sc_w_in=sc_w_in, sc_w_conv=sc_w_conv, sc_w_out=sc_w_out, mlp_w1=mlp_w1, mlp_w2=mlp_w2,
                  ple_w_proj=ple_w_proj, ple_w_gate=ple_w_gate, norm_final=norm_final)
    mom1 = dict(norm_mix=m_norm_mix, norm_mlp=m_norm_mlp, norm_ple=m_norm_ple, cf_w_pw1=m_cf_w_pw1,
                cf_b_pw1=m_cf_b_pw1, cf_w_dw=m_cf_w_dw, cf_b_dw=m_cf_b_dw, cf_norm=m_cf_norm, cf_w_pw2=m_cf_w_pw2,
                cf_b_pw2=m_cf_b_pw2, sc_w_in=m_sc_w_in, sc_w_conv=m_sc_w_conv, sc_w_out=m_sc_w_out,
                mlp_w1=m_mlp_w1, mlp_w2=m_mlp_w2, ple_w_proj=m_ple_w_proj, ple_w_gate=m_ple_w_gate,
                norm_final=m_norm_final)
    mom2 = dict(norm_mix=v_norm_mix, norm_mlp=v_norm_mlp, norm_ple=v_norm_ple, cf_w_pw1=v_cf_w_pw1,
                cf_b_pw1=v_cf_b_pw1, cf_w_dw=v_cf_w_dw, cf_b_dw=v_cf_b_dw, cf_norm=v_cf_norm, cf_w_pw2=v_cf_w_pw2,
                cf_b_pw2=v_cf_b_pw2, sc_w_in=v_sc_w_in, sc_w_conv=v_sc_w_conv, sc_w_out=v_sc_w_out,
                mlp_w1=v_mlp_w1, mlp_w2=v_mlp_w2, ple_w_proj=v_ple_w_proj, ple_w_gate=v_ple_w_gate,
                norm_final=v_norm_final)

    big = ("cf_w_pw1", "cf_w_pw2", "sc_w_in", "sc_w_out", "mlp_w1", "mlp_w2", "ple_w_proj", "ple_w_gate")
    row_sharded = ("cf_w_pw2", "sc_w_out", "mlp_w2", "ple_w_gate")

    def layer_names(i):
        return (["cf_w_pw1", "cf_w_pw2"] if i % 2 == 0 else ["sc_w_in", "sc_w_out"]) + \
            ["mlp_w1", "mlp_w2", "ple_w_proj", "ple_w_gate"]

    def layer_index(i, name):
        return i if name.startswith(("mlp", "ple")) else i // 2

    placed = {(i, nm): _cast_place(f"place_{nm}_{i}", params[nm], layer_index(i, nm), pos)
              for i in range(DEPTH) for nm in layer_names(i)}

    def gather_begin(tag, i, names, after):
        return names, _gather_start(f"gather_start_{tag}", [placed[i, nm] for nm in names], after)

    def weights_of(names, bufs):
        return {nm: g4.reshape(1, N_SHARDS * g4.shape[1], g4.shape[2]) if nm in row_sharded else g4
                for nm, g4 in zip(names, bufs)}

    def gather_end(tag, begun, afters):
        names, (send, recv, bufs, _) = begun
        bufs = _gather_wait(f"gather_wait_{tag}", bufs, send, recv, afters)
        return weights_of(names, _gather_forward(f"gather_fwd_{tag}", bufs))

    conv_small = jnp.concatenate([_pad_rows(cf_w_dw[j]) for j in range(cf_w_dw.shape[0])]
                                 + [_pad_rows(sc_w_conv[j]) for j in range(sc_w_conv.shape[0])], axis=0)
    conv_shards = _gather_shards("gather_conv_w", [(conv_small, None)])[0]
    conv_all = jnp.transpose(conv_shards, (1, 0, 2)).reshape(conv_small.shape[0], D)
    ka_pad = KA + (-KA) % SUBLANES
    kb_pad = KB + (-KB) % SUBLANES
    w_dw_full = [conv_all[j * ka_pad:j * ka_pad + KA] for j in range(cf_w_dw.shape[0])]
    off = cf_w_dw.shape[0] * ka_pad
    w_conv_full = [conv_all[off + j * kb_pad:off + j * kb_pad + KB] for j in range(sc_w_conv.shape[0])]

    def vec(a):
        return a.reshape(1, -1)

    ident = lambda acc: (acc,)

    h = x[0]
    saved = []
    first = gather_begin("0m", 0, layer_names(0)[:2], conv_shards)
    rests = {0: gather_begin("0r", 0, layer_names(0)[2:], first[1][2][0])}
    later = [placed[i, nm] for i in range(1, DEPTH) for nm in layer_names(i)]
    W = [gather_end("0m", first, [h] + later)]
    for i in range(DEPTH):
        j = i // 2
        wl = W[i]
        s = dict(h=h)
        g_mix = vec(norm_mix[i])
        if i + 1 < DEPTH:
            names_n = layer_names(i + 1)
            if i + 1 == 1:
                nxt = gather_begin("1m", 1, names_n[:2], wl[layer_names(i)[0]])
                rests[1] = gather_begin("1r", 1, names_n[2:], nxt[1][2][0])
                g_mix = g_mix + rests[1][1][3][0, 0]
            else:
                nxt = gather_begin(f"{i + 1}", i + 1, names_n, wl[layer_names(i)[0]])
            g_mix = g_mix + nxt[1][3][0, 0]
        s["u"] = _rms_fwd(f"rms_mix_{i}", h, g_mix)
        if i % 2 == 0:
            s["a"] = _mm_nn(f"cf_pw1_{i}", s["u"], wl["cf_w_pw1"], lambda acc, b: (acc + b,), [BF16],
                            extras=[(vec(cf_b_pw1[j]), "n")])[0]
            s["v2"], s["v4"] = _cf_conv_fwd(f"cf_conv_{i}", s["a"], w_dw_full[j], vec(cf_b_dw[j]), vec(cf_norm[j]))
            h1 = _mm_nn(f"cf_pw2_{i}", s["v4"], wl["cf_w_pw2"], lambda acc, b, r: (r + (acc + b),), [F32],
                        extras=[(vec(cf_b_pw2[j]), "n"), (h, "mn")])[0]
        else:
            s["bcv"] = _mm_nn(f"sc_in_{i}", s["u"], wl["sc_w_in"], ident, [BF16])[0]
            s["y"] = _sc_conv_fwd(f"sc_conv_{i}", s["bcv"], w_conv_full[j])[0]
            h1 = _mm_nn(f"sc_out_{i}", s["y"], wl["sc_w_out"], lambda acc, r: (r + acc,), [F32],
                        extras=[(h, "mn")])[0]
        s["h1"] = h1
        if i in rests:
            wl.update(gather_end(f"{i}r", rests[i], [h1]))
        s["u2"] = _rms_fwd(f"rms_mlp_{i}", h1, vec(norm_mlp[i]))
        s["z"], s["hd"] = _mm_nn(f"mlp_w1_{i}", s["u2"], wl["mlp_w1"],
                                 lambda acc: (acc, jnp.square(jnp.maximum(acc, 0.0))), [BF16, BF16])
        h2 = _mm_nn(f"mlp_w2_{i}", s["hd"], wl["mlp_w2"], lambda acc, r: (r + acc,), [F32], extras=[(h1, "mn")])[0]
        s["h2"] = h2
        g_ple = vec(norm_ple[i])
        passing = None
        if 1 <= i < DEPTH - 1:
            send, recv, bufs, _ = nxt[1]
            bufs = _gather_wait(f"gather_wait_{i + 1}", bufs, send, recv, [h2])
            passing = _sibling_start(f"gather_fwd_start_{i + 1}", bufs, 3 * len(bufs), _forward_copies(bufs))
            g_ple = g_ple + passing[3][0, 0]
        s["n3"] = _rms_fwd(f"rms_ple_{i}", h2, g_ple)
        s["p"] = p[i, 0]
        s["e"] = _mm_nn(f"ple_proj_{i}", s["p"], wl["ple_w_proj"], ident, [BF16])[0]
        h, s["q"] = _mm_nn(f"ple_gate_{i}", s["n3"], wl["ple_w_gate"],
                           lambda acc, r, e: (r + _sigmoid(acc) * e.astype(F32), acc), [F32, BF16],
                           extras=[(h2, "mn"), (s["e"], "mn")])
        saved.append(s)
        if passing is not None:
            send, recv, bufs, _ = passing
            bufs = _sibling_wait(f"gather_fwd_wait_{i + 1}", bufs, send, recv, _forward_copies(bufs), [h])
            W.append(weights_of(nxt[0], bufs))
        elif i + 1 < DEPTH:
            W.append(gather_end("1m" if i + 1 == 1 else f"{i + 1}", nxt, [h]))

    dh, dh16, dg_final, loss_cols = _loss_bwd("loss_bwd", h, vec(norm_final), loss_target[0])
    small = {"norm_final": dg_final, "loss": loss_cols}
    adam = {nm: None for nm in big}

    def reduce_names(i):
        names = layer_names(i)
        return names[2:] + names[:2]

    def update_layer(i, joining, afters):
        for nm, g in zip(reduce_names(i), _reduce_end(f"{i}", joining, afters)):
            l = layer_index(i, nm)
            adam[nm] = _adamw_slab(f"adamw_{nm}_{l}", params[nm], mom1[nm], mom2[nm], g, l, adam[nm])

    def shard_major(names, parts):
        return [pt.reshape(N_SHARDS, pt.shape[1] // N_SHARDS, pt.shape[2]) if nm in row_sharded else pt
                for nm, pt in zip(names, parts)]

    scattered = joining = None
    for i in reversed(range(DEPTH)):
        j = i // 2
        wl, s = W[i], saved[i]
        dq, de = _ple_elem_bwd(f"ple_elem_bwd_{i}", dh, s["q"], s["e"])
        d_proj = _mm_tn(f"ple_proj_dw_{i}", s["p"], de, N_SHARDS)
        d_gate = _mm_tn(f"ple_gate_dw_{i}", s["n3"], dq, 1)
        dn3 = _mm_nt(f"ple_gate_dx_{i}", dq, wl["ple_w_gate"], ident, [F32])[0]
        g_ple = vec(norm_ple[i])
        if joining is not None:
            g_ple = g_ple + joining[3][0, 0]
        dh, dh16, small[f"norm_ple_{i}"] = _rms_bwd(f"rms_ple_bwd_{i}", s["h2"], g_ple, dn3, dh)

        d_w2 = _mm_tn(f"mlp_w2_dw_{i}", s["hd"], dh16, 1)
        dz = _mm_nt(f"mlp_w2_dx_{i}", dh16, wl["mlp_w2"],
                    lambda acc, z: (acc * (2.0 * jnp.maximum(z.astype(F32), 0.0)),), [BF16], extras=[s["z"]])[0]
        d_w1 = _mm_tn(f"mlp_w1_dw_{i}", s["u2"], dz, N_SHARDS)
        du2 = _mm_nt(f"mlp_w1_dx_{i}", dz, wl["mlp_w1"], ident, [F32])[0]
        early = shard_major(reduce_names(i)[:4], [d_w1, d_w2, d_proj, d_gate])
        swapping = _swap_begin(f"{i}", early)
        g_mlp = vec(norm_mlp[i]) + swapping[3][0, 0]
        if i % 2 == 0:
            dh, dh16, small[f"norm_mlp_{i}"], small[f"cf_b_pw2_{j}"] = _rms_bwd(
                f"rms_mlp_bwd_{i}", s["h1"], g_mlp, du2, dh, want_colsum=True)
            d_mix_out = _mm_tn(f"cf_pw2_dw_{i}", s["v4"], dh16, 1)
            dv4 = _mm_nt(f"cf_pw2_dx_{i}", dh16, wl["cf_w_pw2"], ident, [F32])[0]
            dv2, small[f"cf_norm_{j}"], small[f"cf_b_dw_{j}"] = _cf_norm_bwd(
                f"cf_norm_bwd_{i}", s["v2"], vec(cf_norm[j]), dv4)
            da, small[f"cf_w_dw_{j}"], db1 = _cf_conv_bwd(f"cf_conv_bwd_{i}", dv2, s["a"], w_dw_full[j])
            small[f"cf_b_pw1_{j}"] = db1.reshape(2, D)
            d_mix_in = _mm_tn(f"cf_pw1_dw_{i}", s["u"], da, N_SHARDS)
            du = _mm_nt(f"cf_pw1_dx_{i}", da, wl["cf_w_pw1"], ident, [F32])[0]
        else:
            dh, dh16, small[f"norm_mlp_{i}"] = _rms_bwd(f"rms_mlp_bwd_{i}", s["h1"], g_mlp, du2, dh)
            d_mix_out = _mm_tn(f"sc_out_dw_{i}", s["y"], dh16, 1)
            dy = _mm_nt(f"sc_out_dx_{i}", dh16, wl["sc_w_out"], ident, [F32])[0]
            da, small[f"sc_w_conv_{j}"] = _sc_conv_bwd(f"sc_conv_bwd_{i}", dy, s["bcv"], w_conv_full[j])
            d_mix_in = _mm_tn(f"sc_in_dw_{i}", s["u"], da, N_SHARDS)
            du = _mm_nt(f"sc_in_dx_{i}", da, wl["sc_w_in"], ident, [F32])[0]

        late = shard_major(reduce_names(i)[4:], [d_mix_in, d_mix_out])
        started = _reduce_begin(f"{i}", early, swapping, late, pos, du)
        token = started[4]
        dh, dh16, small[f"norm_mix_{i}"] = _rms_bwd(f"rms_mix_bwd_{i}", s["h"], vec(norm_mix[i]) + token[0, 0], du, dh)
        begun = _reduce_middle(f"{i + 1}", scattered, pos, [token]) if scattered is not None else None
        if joining is not None:
            update_layer(i + 2, joining, [token] + ([begun[3]] if begun is not None else []))
        joining, scattered = begun, started
    grad_x = dh.reshape(x.shape)

    order = sorted(small)
    pieces, where, row = [], {}, 0
    for nm in order:
        pc = _pad_rows(small[nm])
        where[nm] = (row, small[nm].shape[0])
        row += pc.shape[0]
        pieces.append(pc)
    update_layer(1, joining, [token])
    updated = [res[3] for res in adam.values() if res is not None]
    total, loss_tile = _sum_over_devices("small_allsum", jnp.concatenate(pieces, axis=0), where["loss"][0], updated)
    loss = loss_tile[0, 0]
    joining = _reduce_middle("0", scattered, pos, [total] + updated)

    def small_sum(nm):
        r0, nr = where[nm]
        return total[r0:r0 + nr]

    def my_cols(a):
        return lax.dynamic_slice_in_dim(a, chip * (D // N_SHARDS), D // N_SHARDS, axis=1)

    g_small = {
        "norm_mix": jnp.concatenate([small_sum(f"norm_mix_{i}") for i in range(DEPTH)], axis=0),
        "norm_mlp": jnp.concatenate([small_sum(f"norm_mlp_{i}") for i in range(DEPTH)], axis=0),
        "norm_ple": jnp.concatenate([small_sum(f"norm_ple_{i}") for i in range(DEPTH)], axis=0),
        "cf_b_pw1": jnp.stack([small_sum(f"cf_b_pw1_{j}").reshape(2 * D) for j in range(DEPTH // 2)]),
        "cf_w_dw": jnp.stack([my_cols(small_sum(f"cf_w_dw_{j}")) for j in range(DEPTH // 2)]),
        "cf_b_dw": jnp.concatenate([small_sum(f"cf_b_dw_{j}") for j in range(DEPTH // 2)], axis=0),
        "cf_norm": jnp.concatenate([small_sum(f"cf_norm_{j}") for j in range(DEPTH // 2)], axis=0),
        "cf_b_pw2": jnp.concatenate([small_sum(f"cf_b_pw2_{j}") for j in range(DEPTH // 2)], axis=0),
        "sc_w_conv": jnp.stack([my_cols(small_sum(f"sc_w_conv_{j}")) for j in range(DEPTH // 2)]),
        "norm_final": small_sum("norm_final").reshape(D),
    }

    names_out = ["norm_mix", "norm_mlp", "norm_ple", "cf_w_pw1", "cf_b_pw1", "cf_w_dw", "cf_b_dw", "cf_norm",
                 "cf_w_pw2", "cf_b_pw2", "sc_w_in", "sc_w_conv", "sc_w_out", "mlp_w1", "mlp_w2", "ple_w_proj",
                 "ple_w_gate", "norm_final"]
    grad, delta, new_m, new_v = {}, {}, {}, {}
    for nm in names_out:
        if nm in big:
            continue
        w = params[nm]
        cols = w.shape[-1] if w.ndim > 1 else w.shape[0]
        two_d = lambda a: a.reshape(-1, cols)
        res = _adamw(f"adamw_{nm}", two_d(w), two_d(mom1[nm]), two_d(mom2[nm]), two_d(g_small[nm]))
        grad[nm], delta[nm], new_m[nm], new_v[nm] = [r.reshape(w.shape) for r in res]
    update_layer(0, joining, list(delta.values()))
    for nm in big:
        grad[nm], delta[nm], new_m[nm], new_v[nm] = adam[nm]

    return (loss, grad_x, *[grad[n] for n in names_out], *[delta[n] for n in names_out],
            *[new_m[n] for n in names_out], *[new_v[n] for n in names_out])
```
